```python
import jax, jax.numpy as jnp
from jax import lax
import numpy as np

D_MODEL = 1024
BATCH = 16
SEQ = 4096
DEPTH = 4

MEM_LEN = 256
CONV_WIDTH = 512
CONV_K = 3
HG_HEADS = 4
HG_F = 128
HG_I = 128
HG_QK = HG_HEADS * HG_F
HG_WIDTH = HG_HEADS * HG_I
HG_CHUNK = 32
MEM_HEADS = 4
MEM_HEAD_DIM = 128
MEM_WIDTH = MEM_HEADS * MEM_HEAD_DIM
N_BRANCH = 3
BRANCH_WIDTH = 512
D_FF = 4 * D_MODEL
ALPHA = (2.0 * DEPTH) ** 0.25
BETA = (8.0 * DEPTH) ** -0.25
LN_EPS = 1e-5
RMS_EPS = 1e-6
IN_SPLITS = (CONV_WIDTH, CONV_WIDTH, CONV_WIDTH, HG_QK, HG_QK, HG_WIDTH, HG_WIDTH, MEM_WIDTH, D_MODEL, D_MODEL, D_MODEL)
IN_COLS = sum(IN_SPLITS)

kernel_name = "hybrid_conv_hgrn2_memattn_postnorm"


def _layer_norm(x, g, b):
    xf = x.astype(jnp.float32)
    mu = jnp.mean(xf, axis=-1, keepdims=True)
    var = jnp.mean(jnp.square(xf - mu), axis=-1, keepdims=True)
    y = (xf - mu) * lax.rsqrt(var + LN_EPS) * g.astype(jnp.float32) + b.astype(jnp.float32)
    return y.astype(x.dtype)


def _short_conv_mixer(gate_b, gate_c, h, conv_w):
    u = gate_c * h
    seq = u.shape[1]
    u_pad = jnp.pad(u, ((0, 0), (CONV_K - 1, 0), (0, 0)))
    y = u_pad[:, 0:seq] * conv_w[0]
    for tap in range(1, CONV_K):
        y = y + u_pad[:, tap:tap + seq] * conv_w[tap]
    return gate_b * y


def _hgrn2_mixer(q, f_logit, i, g, lb, norm_w):
    bsz, seq, _ = q.shape
    n_chunk = seq // HG_CHUNK
    f32 = jnp.float32
    fl = f_logit.astype(f32)
    lb = lb.astype(f32)
    sig = jax.nn.sigmoid(fl)
    log_f = jnp.log(lb + (1.0 - lb) * sig)
    k = (1.0 - lb) * jax.nn.sigmoid(-fl)

    def to_chunks(t, d):
        return t.reshape(bsz, n_chunk, HG_CHUNK, HG_HEADS, d).transpose(0, 3, 1, 2, 4)

    qc = to_chunks(jax.nn.silu(q.astype(f32)), HG_F)
    kc = to_chunks(k, HG_F)
    vc = to_chunks(i.astype(f32), HG_I)
    bc = jnp.cumsum(to_chunks(log_f, HG_F), axis=3)
    b_ref = bc[:, :, :, HG_CHUNK // 2 - 1:HG_CHUNK // 2, :]
    b_last = bc[:, :, :, -1:, :]

    causal = jnp.tril(jnp.ones((HG_CHUNK, HG_CHUNK), dtype=bool))
    scores = jnp.einsum('bhntf,bhnsf->bhnts', qc * jnp.exp(bc - b_ref), kc * jnp.exp(b_ref - bc))
    scores = jnp.where(causal, scores, 0.0)
    o_intra = jnp.einsum('bhnts,bhnsv->bhntv', scores, vc)

    q_in = qc * jnp.exp(bc)
    k_out = kc * jnp.exp(b_last - bc)
    decay = jnp.exp(b_last[:, :, :, 0, :])

    def step(state, xs):
        q_n, k_n, v_n, dec_n = xs
        o_n = jnp.einsum('bhtf,bhfv->bhtv', q_n, state)
        state = dec_n[..., None] * state + jnp.einsum('bhsf,bhsv->bhfv', k_n, v_n)
        return state, o_n

    xs = (jnp.moveaxis(q_in, 2, 0), jnp.moveaxis(k_out, 2, 0), jnp.moveaxis(vc, 2, 0), jnp.moveaxis(decay, 2, 0))
    init = jnp.zeros((bsz, HG_HEADS, HG_F, HG_I), f32)
    _, o_inter = lax.scan(step, init, xs)
    o = o_intra + jnp.moveaxis(o_inter, 0, 2)

    o = o.transpose(0, 2, 3, 1, 4).reshape(bsz, seq, HG_HEADS, HG_I)
    o = o * lax.rsqrt(jnp.mean(o * o, axis=-1, keepdims=True) + RMS_EPS) * norm_w.astype(f32)
    o = o.reshape(bsz, seq, HG_WIDTH) * jax.nn.silu(g.astype(f32))
    return o.astype(q.dtype)


def _memory_attention(q, mem_k, mem_v):
    bsz, seq, _ = q.shape
    qh = q.reshape(bsz, seq, MEM_HEADS, MEM_HEAD_DIM)
    kh = mem_k.reshape(bsz, MEM_LEN, MEM_HEADS, MEM_HEAD_DIM)
    vh = mem_v.reshape(bsz, MEM_LEN, MEM_HEADS, MEM_HEAD_DIM)
    s = jnp.einsum('bthd,bmhd->bhtm', qh, kh).astype(jnp.float32) * (MEM_HEAD_DIM ** -0.5)
    p = jax.nn.softmax(s, axis=-1).astype(vh.dtype)
    o = jnp.einsum('bhtm,bmhd->bthd', p, vh)
    return o.reshape(bsz, seq, MEM_WIDTH)


def _hybrid_layer(x, mem, lb, w_in, conv_w, hg_norm_w, w_mem_k, w_mem_v, w_branch, b_gate, w_o,
                  ln1_g, ln1_b, w_up, w_down, ln2_g, ln2_b):
    bsz, seq, _ = x.shape
    proj = jnp.einsum('bsd,dc->bsc', x, w_in)
    split_idx = [int(v) for v in np.cumsum(IN_SPLITS)[:-1]]
    cb, cc, ch, hq, hf, hi, hg, mq, ga, gb, gc = jnp.split(proj, split_idx, axis=-1)

    y_a = _short_conv_mixer(cb, cc, ch, conv_w)
    y_b = _hgrn2_mixer(hq, hf, hi, hg, lb, hg_norm_w)
    y_c = _memory_attention(mq, jnp.einsum('bmd,dw->bmw', mem, w_mem_k), jnp.einsum('bmd,dw->bmw', mem, w_mem_v))

    b_ga, b_gb, b_gc = jnp.split(b_gate, N_BRANCH, axis=-1)
    merged = (jax.nn.sigmoid(ga + b_ga) * jnp.einsum('bsw,wd->bsd', y_a, w_branch[0])
              + jax.nn.sigmoid(gb + b_gb) * jnp.einsum('bsw,wd->bsd', y_b, w_branch[1])
              + jax.nn.sigmoid(gc + b_gc) * jnp.einsum('bsw,wd->bsd', y_c, w_branch[2]))
    mixed = jnp.einsum('bsd,de->bse', merged, w_o)
    x = _layer_norm(ALPHA * x + mixed, ln1_g, ln1_b)

    h = jnp.square(jax.nn.relu(jnp.einsum('bsd,df->bsf', x, w_up)))
    x = _layer_norm(ALPHA * x + jnp.einsum('bsf,fd->bsd', h, w_down), ln2_g, ln2_b)
    return x


def _fwd_setup_inputs(seed: int = 0) -> dict:
    key = jax.random.key(seed)
    ks = jax.random.split(key, 20)
    nrm = jax.random.normal
    f32 = jnp.float32
    return {
        'x': nrm(ks[0], (BATCH, SEQ, D_MODEL), f32),
        'mem': nrm(ks[1], (BATCH, MEM_LEN, D_MODEL), f32),
        'lower_bounds': 0.02 * nrm(ks[2], (DEPTH, HG_QK), f32),
        'w_in': nrm(ks[3], (DEPTH, D_MODEL, IN_COLS), f32) * D_MODEL ** -0.5,
        'conv_w': nrm(ks[4], (DEPTH, CONV_K, CONV_WIDTH), f32) * CONV_K ** -0.5,
        'hg_norm_w': 1.0 + 0.02 * nrm(ks[5], (DEPTH, HG_I), f32),
        'w_mem_k': nrm(ks[6], (DEPTH, D_MODEL, MEM_WIDTH), f32) * D_MODEL ** -0.5,
        'w_mem_v': nrm(ks[7], (DEPTH, D_MODEL, MEM_WIDTH), f32) * (D_MODEL ** -0.5 * BETA),
        'w_branch': nrm(ks[8], (DEPTH, N_BRANCH, BRANCH_WIDTH, D_MODEL), f32) * (BRANCH_WIDTH ** -0.5 * BETA),
        'b_gate': 0.02 * nrm(ks[9], (DEPTH, N_BRANCH * D_MODEL), f32),
        'w_o': nrm(ks[10], (DEPTH, D_MODEL, D_MODEL), f32) * (D_MODEL ** -0.5 * BETA),
        'ln1_g': 1.0 + 0.02 * nrm(ks[11], (DEPTH, D_MODEL), f32),
        'ln1_b': 0.02 * nrm(ks[12], (DEPTH, D_MODEL), f32),
        'w_up': nrm(ks[13], (DEPTH, D_MODEL, D_FF), f32) * (D_MODEL ** -0.5 * BETA),
        'w_down': nrm(ks[14], (DEPTH, D_FF, D_MODEL), f32) * (D_FF ** -0.5 * BETA),
        'ln2_g': 1.0 + 0.02 * nrm(ks[15], (DEPTH, D_MODEL), f32),
        'ln2_b': 0.02 * nrm(ks[16], (DEPTH, D_MODEL), f32),
    }


def _fwd_reference(x, mem, lower_bounds, w_in, conv_w, hg_norm_w, w_mem_k, w_mem_v, w_branch, b_gate, w_o,
              ln1_g, ln1_b, w_up, w_down, ln2_g, ln2_b):
    lb_soft = jax.nn.softmax(lower_bounds.astype(jnp.float32), axis=0)
    lb_all = jnp.cumsum(lb_soft, axis=0) - lb_soft[0:1]
    for layer in range(DEPTH):
        x = _hybrid_layer(x, mem, lb_all[layer], w_in[layer], conv_w[layer], hg_norm_w[layer],
                          w_mem_k[layer], w_mem_v[layer], w_branch[layer], b_gate[layer], w_o[layer],
                          ln1_g[layer], ln1_b[layer], w_up[layer], w_down[layer], ln2_g[layer], ln2_b[layer])
    return x


import jax as _jax
import jax.numpy as _jnp

TWIN_FORMAT = 'train_step'
FWD_PARAMS = ['x', 'mem', 'lower_bounds', 'w_in', 'conv_w', 'hg_norm_w', 'w_mem_k', 'w_mem_v', 'w_branch', 'b_gate', 'w_o', 'ln1_g', 'ln1_b', 'w_up', 'w_down', 'ln2_g', 'ln2_b']
TWIN_WEIGHTS = ['lower_bounds', 'w_in', 'conv_w', 'hg_norm_w', 'w_mem_k', 'w_mem_v', 'w_branch', 'b_gate', 'w_o', 'ln1_g', 'ln1_b', 'w_up', 'w_down', 'ln2_g', 'ln2_b']
TWIN_DIFF_INPUT = 'x'
TWIN_INPUTS = ['x', 'mem', 'lower_bounds', 'w_in', 'conv_w', 'hg_norm_w', 'w_mem_k', 'w_mem_v', 'w_branch', 'b_gate', 'w_o', 'ln1_g', 'ln1_b', 'w_up', 'w_down', 'ln2_g', 'ln2_b', 'loss_target', 'm_lower_bounds', 'm_w_in', 'm_conv_w', 'm_hg_norm_w', 'm_w_mem_k', 'm_w_mem_v', 'm_w_branch', 'm_b_gate', 'm_w_o', 'm_ln1_g', 'm_ln1_b', 'm_w_up', 'm_w_down', 'm_ln2_g', 'm_ln2_b', 'v_lower_bounds', 'v_w_in', 'v_conv_w', 'v_hg_norm_w', 'v_w_mem_k', 'v_w_mem_v', 'v_w_branch', 'v_b_gate', 'v_w_o', 'v_ln1_g', 'v_ln1_b', 'v_w_up', 'v_w_down', 'v_ln2_g', 'v_ln2_b']
TWIN_OUTPUTS = ['loss', 'grad_x', 'grad_lower_bounds', 'grad_w_in', 'grad_conv_w', 'grad_hg_norm_w', 'grad_w_mem_k', 'grad_w_mem_v', 'grad_w_branch', 'grad_b_gate', 'grad_w_o', 'grad_ln1_g', 'grad_ln1_b', 'grad_w_up', 'grad_w_down', 'grad_ln2_g', 'grad_ln2_b', 'delta_lower_bounds', 'delta_w_in', 'delta_conv_w', 'delta_hg_norm_w', 'delta_w_mem_k', 'delta_w_mem_v', 'delta_w_branch', 'delta_b_gate', 'delta_w_o', 'delta_ln1_g', 'delta_ln1_b', 'delta_w_up', 'delta_w_down', 'delta_ln2_g', 'delta_ln2_b', 'new_m_lower_bounds', 'new_m_w_in', 'new_m_conv_w', 'new_m_hg_norm_w', 'new_m_w_mem_k', 'new_m_w_mem_v', 'new_m_w_branch', 'new_m_b_gate', 'new_m_w_o', 'new_m_ln1_g', 'new_m_ln1_b', 'new_m_w_up', 'new_m_w_down', 'new_m_ln2_g', 'new_m_ln2_b', 'new_v_lower_bounds', 'new_v_w_in', 'new_v_conv_w', 'new_v_hg_norm_w', 'new_v_w_mem_k', 'new_v_w_mem_v', 'new_v_w_branch', 'new_v_b_gate', 'new_v_w_o', 'new_v_ln1_g', 'new_v_ln1_b', 'new_v_w_up', 'new_v_w_down', 'new_v_ln2_g', 'new_v_ln2_b']
TWIN_LEAF_KINDS = {'loss': 'loss', 'grad_x': 'grad_x', 'grad_lower_bounds': 'grad_w', 'grad_w_in': 'grad_w', 'grad_conv_w': 'grad_w', 'grad_hg_norm_w': 'grad_w', 'grad_w_mem_k': 'grad_w', 'grad_w_mem_v': 'grad_w', 'grad_w_branch': 'grad_w', 'grad_b_gate': 'grad_w', 'grad_w_o': 'grad_w', 'grad_ln1_g': 'grad_w', 'grad_ln1_b': 'grad_w', 'grad_w_up': 'grad_w', 'grad_w_down': 'grad_w', 'grad_ln2_g': 'grad_w', 'grad_ln2_b': 'grad_w', 'delta_lower_bounds': 'delta_w', 'delta_w_in': 'delta_w', 'delta_conv_w': 'delta_w', 'delta_hg_norm_w': 'delta_w', 'delta_w_mem_k': 'delta_w', 'delta_w_mem_v': 'delta_w', 'delta_w_branch': 'delta_w', 'delta_b_gate': 'delta_w', 'delta_w_o': 'delta_w', 'delta_ln1_g': 'delta_w', 'delta_ln1_b': 'delta_w', 'delta_w_up': 'delta_w', 'delta_w_down': 'delta_w', 'delta_ln2_g': 'delta_w', 'delta_ln2_b': 'delta_w', 'new_m_lower_bounds': 'new_m', 'new_m_w_in': 'new_m', 'new_m_conv_w': 'new_m', 'new_m_hg_norm_w': 'new_m', 'new_m_w_mem_k': 'new_m', 'new_m_w_mem_v': 'new_m', 'new_m_w_branch': 'new_m', 'new_m_b_gate': 'new_m', 'new_m_w_o': 'new_m', 'new_m_ln1_g': 'new_m', 'new_m_ln1_b': 'new_m', 'new_m_w_up': 'new_m', 'new_m_w_down': 'new_m', 'new_m_ln2_g': 'new_m', 'new_m_ln2_b': 'new_m', 'new_v_lower_bounds': 'new_v', 'new_v_w_in': 'new_v', 'new_v_conv_w': 'new_v', 'new_v_hg_norm_w': 'new_v', 'new_v_w_mem_k': 'new_v', 'new_v_w_mem_v': 'new_v', 'new_v_w_branch': 'new_v', 'new_v_b_gate': 'new_v', 'new_v_w_o': 'new_v', 'new_v_ln1_g': 'new_v', 'new_v_ln1_b': 'new_v', 'new_v_w_up': 'new_v', 'new_v_w_down': 'new_v', 'new_v_ln2_g': 'new_v', 'new_v_ln2_b': 'new_v'}


def _forward(args):
    return _fwd_reference(*[args[k] for k in FWD_PARAMS])


def _output_shape():
    out = _jax.eval_shape(lambda: _forward(_fwd_setup_inputs(0)))
    return out.shape, out.dtype

N_MICROBATCH = 1
ADAM_LR = 0.001
ADAM_B1 = 0.9
ADAM_B2 = 0.999
ADAM_EPS = 1e-08
ADAM_WD = 0.01
ADAM_STEP = 10
PER_EXAMPLE_BATCH_AXIS = {'x': 0, 'mem': 0, 'loss_target': 0}
SHARED_INPUTS = []
_WEIGHT_DTYPES = {'lower_bounds': _jnp.float32, 'w_in': _jnp.float32, 'conv_w': _jnp.float32, 'hg_norm_w': _jnp.float32, 'w_mem_k': _jnp.float32, 'w_mem_v': _jnp.float32, 'w_branch': _jnp.float32, 'b_gate': _jnp.float32, 'w_o': _jnp.float32, 'ln1_g': _jnp.float32, 'ln1_b': _jnp.float32, 'w_up': _jnp.float32, 'w_down': _jnp.float32, 'ln2_g': _jnp.float32, 'ln2_b': _jnp.float32}
MOMENT_SCALE = {'lower_bounds': 1.256659e-03, 'w_in': 1.079644e-02, 'conv_w': 2.060240e-02, 'hg_norm_w': 2.689008e-02, 'w_mem_k': 9.026327e-04, 'w_mem_v': 2.461037e-03, 'w_branch': 2.308792e-02, 'b_gate': 3.764250e-03, 'w_o': 3.996581e-02, 'ln1_g': 1.902345e+00, 'ln1_b': 7.795627e-01, 'w_up': 1.876511e-02, 'w_down': 4.418127e-02, 'ln2_g': 3.214767e+01, 'ln2_b': 1.782837e+00}


def _to_microbatches(a, axis):
    t = _jnp.moveaxis(a, axis, 0)
    t = t.reshape((N_MICROBATCH, t.shape[0] // N_MICROBATCH) + t.shape[1:])
    return _jnp.moveaxis(t, 1, axis + 1)


def setup_inputs(seed: int = 0) -> dict:
    inp = _fwd_setup_inputs(seed)
    key = _jax.random.fold_in(_jax.random.key(seed), 7919)
    shape, _ = _output_shape()
    out = dict(inp)
    out["loss_target"] = _jax.random.normal(_jax.random.fold_in(key, 0), shape, _jnp.float32)
    for i, name in enumerate(TWIN_WEIGHTS):
        w = inp[name].astype(_jnp.float32)
        if MOMENT_SCALE is None:
            s = _jnp.sqrt(_jnp.mean(_jnp.square(w)) + 1e-30)
        else:
            s = MOMENT_SCALE[name]
        km, kv = _jax.random.split(_jax.random.fold_in(key, i + 1))
        out[name] = w
        out["m_" + name] = s * _jax.random.normal(km, w.shape, _jnp.float32)
        out["v_" + name] = (s * s) * _jax.random.uniform(kv, w.shape, _jnp.float32, 0.5, 1.5)
    if N_MICROBATCH > 1:
        for name, axis in PER_EXAMPLE_BATCH_AXIS.items():
            out[name] = _to_microbatches(out[name], axis)
    return {'x': out['x'], 'mem': out['mem'], 'lower_bounds': out['lower_bounds'], 'w_in': out['w_in'], 'conv_w': out['conv_w'], 'hg_norm_w': out['hg_norm_w'], 'w_mem_k': out['w_mem_k'], 'w_mem_v': out['w_mem_v'], 'w_branch': out['w_branch'], 'b_gate': out['b_gate'], 'w_o': out['w_o'], 'ln1_g': out['ln1_g'], 'ln1_b': out['ln1_b'], 'w_up': out['w_up'], 'w_down': out['w_down'], 'ln2_g': out['ln2_g'], 'ln2_b': out['ln2_b'], 'loss_target': out['loss_target'], 'm_lower_bounds': out['m_lower_bounds'], 'm_w_in': out['m_w_in'], 'm_conv_w': out['m_conv_w'], 'm_hg_norm_w': out['m_hg_norm_w'], 'm_w_mem_k': out['m_w_mem_k'], 'm_w_mem_v': out['m_w_mem_v'], 'm_w_branch': out['m_w_branch'], 'm_b_gate': out['m_b_gate'], 'm_w_o': out['m_w_o'], 'm_ln1_g': out['m_ln1_g'], 'm_ln1_b': out['m_ln1_b'], 'm_w_up': out['m_w_up'], 'm_w_down': out['m_w_down'], 'm_ln2_g': out['m_ln2_g'], 'm_ln2_b': out['m_ln2_b'], 'v_lower_bounds': out['v_lower_bounds'], 'v_w_in': out['v_w_in'], 'v_conv_w': out['v_conv_w'], 'v_hg_norm_w': out['v_hg_norm_w'], 'v_w_mem_k': out['v_w_mem_k'], 'v_w_mem_v': out['v_w_mem_v'], 'v_w_branch': out['v_w_branch'], 'v_b_gate': out['v_b_gate'], 'v_w_o': out['v_w_o'], 'v_ln1_g': out['v_ln1_g'], 'v_ln1_b': out['v_ln1_b'], 'v_w_up': out['v_w_up'], 'v_w_down': out['v_w_down'], 'v_ln2_g': out['v_ln2_g'], 'v_ln2_b': out['v_ln2_b']}


def _loss(weights, diff, rest, loss_target):
    with _jax.named_scope("forward"):
        args = {**rest, TWIN_DIFF_INPUT: diff, **{k: w.astype(_WEIGHT_DTYPES[k]) for k, w in weights.items()}}
        y = _forward(args)
    with _jax.named_scope("loss_head"):
        err = _jnp.square(y.astype(_jnp.float32) - loss_target)
        return 0.5 * _jnp.sum(_jnp.mean(err, axis=-1)) if err.ndim else 0.5 * err


def _adamw(w, g, m, v):
    m = ADAM_B1 * m + (1.0 - ADAM_B1) * g
    v = ADAM_B2 * v + (1.0 - ADAM_B2) * _jnp.square(g)
    m_hat = m / (1.0 - ADAM_B1 ** ADAM_STEP)
    v_hat = v / (1.0 - ADAM_B2 ** ADAM_STEP)
    delta = -ADAM_LR * (m_hat / (_jnp.sqrt(v_hat) + ADAM_EPS) + ADAM_WD * w)
    return delta, m, v


def reference(x, mem, lower_bounds, w_in, conv_w, hg_norm_w, w_mem_k, w_mem_v, w_branch, b_gate, w_o, ln1_g, ln1_b, w_up, w_down, ln2_g, ln2_b, loss_target, m_lower_bounds, m_w_in, m_conv_w, m_hg_norm_w, m_w_mem_k, m_w_mem_v, m_w_branch, m_b_gate, m_w_o, m_ln1_g, m_ln1_b, m_w_up, m_w_down, m_ln2_g, m_ln2_b, v_lower_bounds, v_w_in, v_conv_w, v_hg_norm_w, v_w_mem_k, v_w_mem_v, v_w_branch, v_b_gate, v_w_o, v_ln1_g, v_ln1_b, v_w_up, v_w_down, v_ln2_g, v_ln2_b):
    given = dict(x=x, mem=mem, lower_bounds=lower_bounds, w_in=w_in, conv_w=conv_w, hg_norm_w=hg_norm_w, w_mem_k=w_mem_k, w_mem_v=w_mem_v, w_branch=w_branch, b_gate=b_gate, w_o=w_o, ln1_g=ln1_g, ln1_b=ln1_b, w_up=w_up, w_down=w_down, ln2_g=ln2_g, ln2_b=ln2_b, loss_target=loss_target, m_lower_bounds=m_lower_bounds, m_w_in=m_w_in, m_conv_w=m_conv_w, m_hg_norm_w=m_hg_norm_w, m_w_mem_k=m_w_mem_k, m_w_mem_v=m_w_mem_v, m_w_branch=m_w_branch, m_b_gate=m_b_gate, m_w_o=m_w_o, m_ln1_g=m_ln1_g, m_ln1_b=m_ln1_b, m_w_up=m_w_up, m_w_down=m_w_down, m_ln2_g=m_ln2_g, m_ln2_b=m_ln2_b, v_lower_bounds=v_lower_bounds, v_w_in=v_w_in, v_conv_w=v_conv_w, v_hg_norm_w=v_hg_norm_w, v_w_mem_k=v_w_mem_k, v_w_mem_v=v_w_mem_v, v_w_branch=v_w_branch, v_b_gate=v_b_gate, v_w_o=v_w_o, v_ln1_g=v_ln1_g, v_ln1_b=v_ln1_b, v_w_up=v_w_up, v_w_down=v_w_down, v_ln2_g=v_ln2_g, v_ln2_b=v_ln2_b)
    weights = {n: given[n] for n in TWIN_WEIGHTS}
    shared = {n: given[n] for n in SHARED_INPUTS}
    per_example = {n: given[n] for n in ['x', 'mem']}
    grad_fn = _jax.value_and_grad(_loss, argnums=(0, 1))

    def one_microbatch(ex, loss_target):
        ex = dict(ex)
        diff = ex.pop(TWIN_DIFF_INPUT)
        return grad_fn(weights, diff, {**shared, **ex}, loss_target)

    if N_MICROBATCH == 1:
        loss, (grad_w, grad_x) = one_microbatch(per_example, given["loss_target"])
    else:
        def body(carry, xs):
            loss_sum, grad_sum = carry
            l_k, (gw_k, gx_k) = one_microbatch(xs[0], xs[1])
            with _jax.named_scope("update"):
                return (loss_sum + l_k, _jax.tree.map(_jnp.add, grad_sum, gw_k)), gx_k

        init = (_jnp.zeros((), _jnp.float32), _jax.tree.map(_jnp.zeros_like, weights))
        (loss, grad_w), grad_x = _jax.lax.scan(body, init, (per_example, given["loss_target"]))
    with _jax.named_scope("update"):
        delta_w, new_m, new_v = {}, {}, {}
        for n in TWIN_WEIGHTS:
            delta_w[n], new_m[n], new_v[n] = _adamw(weights[n], grad_w[n], given["m_" + n], given["v_" + n])
    return (loss, grad_x, *[grad_w[n] for n in TWIN_WEIGHTS], *[delta_w[n] for n in TWIN_WEIGHTS],
            *[new_m[n] for n in TWIN_WEIGHTS], *[new_v[n] for n in TWIN_WEIGHTS])
```

```python
import functools

import jax
import jax.numpy as jnp
from jax import lax
from jax.experimental import pallas as pl
from jax.experimental.pallas import tpu as pltpu

F32 = jnp.float32
BF16 = jnp.bfloat16

N_DEV = 8
D_MODEL = 1024
DEPTH = 4
MEM_LEN = 256
CONV_K = 3
WIDTH = 512
HEADS = 4
HEAD_DIM = 128
CHUNK = 32
D_FF = 4 * D_MODEL
IN_COLS = 7168
ALPHA = (2.0 * DEPTH) ** 0.25
LN_EPS = 1e-5
RMS_EPS = 1e-6
ADAM_LR = 0.001
ADAM_B1 = 0.9
ADAM_B2 = 0.999
ADAM_EPS = 1e-08
ADAM_WD = 0.01
ADAM_STEP = 10

C_CB, C_CC, C_CH, C_HQ, C_HF, C_HI, C_HG, C_MQ, C_GA = 0, 512, 1024, 1536, 2048, 2560, 3072, 3584, 4096

ROWS_HG = 256
NT_DIMS = (((1,), (1,)), ((), ()))
TN_DIMS = (((0,), (0,)), ((), ()))
MESH = pl.DeviceIdType.MESH


def _dot(a, b):
    return jnp.dot(a, b, preferred_element_type=F32)


def _dot_nt(a, b):
    return lax.dot_general(a, b, NT_DIMS, preferred_element_type=F32)


def _dot_tn(a, b):
    return lax.dot_general(a, b, TN_DIMS, preferred_element_type=F32)


def _sigmoid(x):
    return 1.0 / (1.0 + jnp.exp(-x))


def _params(*sem):
    return pltpu.CompilerParams(dimension_semantics=sem)


def _resident(shape):
    nd = len(shape)
    return pl.BlockSpec(shape, lambda *_: (0,) * nd)


def _my_id():
    return 4 * lax.axis_index("x") + 2 * lax.axis_index("y") + lax.axis_index("c")


def _exchange(name, srcs, slicers, piece_shapes):
    n = len(srcs)

    def body(*refs):
        ins, outs = refs[:n], refs[n:2 * n]
        send_sems, recv_sems, local_sems = refs[2 * n:]
        me = _my_id()

        def remote(k, j):
            return pltpu.make_async_remote_copy(
                src_ref=slicers[k](ins[k], j), dst_ref=outs[k].at[me],
                send_sem=send_sems.at[k * N_DEV + j], recv_sem=recv_sems.at[k * N_DEV + me],
                device_id=(j // 4, (j // 2) % 2, j % 2), device_id_type=MESH)

        def local(k, j):
            return pltpu.make_async_copy(slicers[k](ins[k], j), outs[k].at[me], local_sems.at[k])

        for k in range(n):
            for j in range(N_DEV):
                @pl.when(j != me)
                def _():
                    remote(k, j).start()

                @pl.when(j == me)
                def _():
                    local(k, j).start()

        for k in range(n):
            for j in range(N_DEV):
                @pl.when(j != me)
                def _():
                    pltpu.make_async_remote_copy(
                        src_ref=slicers[k](ins[k], j), dst_ref=outs[k].at[j],
                        send_sem=send_sems.at[k * N_DEV + j], recv_sem=recv_sems.at[k * N_DEV + j],
                        device_id=(j // 4, (j // 2) % 2, j % 2), device_id_type=MESH).wait_recv()
                    remote(k, j).wait_send()

                @pl.when(j == me)
                def _():
                    local(k, j).wait()

    any_spec = pl.BlockSpec(memory_space=pl.ANY)
    return pl.pallas_call(
        body, name=name,
        in_specs=[any_spec] * n, out_specs=[any_spec] * n,
        out_shape=[jax.ShapeDtypeStruct((N_DEV,) + tuple(s), a.dtype) for s, a in zip(piece_shapes, srcs)],
        scratch_shapes=[pltpu.SemaphoreType.DMA((n * N_DEV,)), pltpu.SemaphoreType.DMA((n * N_DEV,)),
                        pltpu.SemaphoreType.DMA((n,))],
        compiler_params=pltpu.CompilerParams(has_side_effects=True),
    )(*srcs)


def _whole(ref, j):
    return ref


def _cols(width):
    return lambda ref, j: ref.at[(slice(None),) * (len(ref.shape) - 1) + (pl.ds(j * width, width),)]


def _rows(height):
    return lambda ref, j: ref.at[pl.ds(j * height, height)]


def _mm_nn(name, a, w, *, tm, tn, out_dtype, relu2=False):
    t, k = a.shape
    n = w.shape[1]

    def body(a_ref, w_ref, o_ref):
        acc = _dot(a_ref[...].astype(BF16), w_ref[...])
        if relu2:
            r = jnp.maximum(acc, 0.0)
            acc = r * r
        o_ref[...] = acc.astype(out_dtype)

    return pl.pallas_call(
        body, name=name, grid=(t // tm, n // tn),
        in_specs=[pl.BlockSpec((tm, k), lambda i, j: (i, 0)), pl.BlockSpec((k, tn), lambda i, j: (0, j))],
        out_specs=pl.BlockSpec((tm, tn), lambda i, j: (i, j)),
        out_shape=jax.ShapeDtypeStruct((t, n), out_dtype),
        compiler_params=_params("parallel", "parallel"),
    )(a, w)


def _layer_norm(z, g, b):
    mu = jnp.mean(z, axis=-1, keepdims=True)
    zc = z - mu
    var = jnp.mean(zc * zc, axis=-1, keepdims=True)
    return zc * lax.rsqrt(var + LN_EPS) * g + b


def _linear_ln(name, a, w, resid, g, b, *, tm):
    t, k = a.shape

    def body(a_ref, w_ref, r_ref, g_ref, b_ref, z_ref, x_ref):
        z = ALPHA * r_ref[...] + _dot(a_ref[...], w_ref[...])
        z_ref[...] = z
        x_ref[...] = _layer_norm(z, g_ref[...], b_ref[...])

    row = pl.BlockSpec((tm, D_MODEL), lambda i: (i, 0))
    return pl.pallas_call(
        body, name=name, grid=(t // tm,),
        in_specs=[pl.BlockSpec((tm, k), lambda i: (i, 0)), _resident((k, D_MODEL)), row,
                  _resident((1, D_MODEL)), _resident((1, D_MODEL))],
        out_specs=[row, row],
        out_shape=[jax.ShapeDtypeStruct((t, D_MODEL), F32)] * 2,
        compiler_params=_params("parallel"),
    )(a, w, resid, g, b)


def _ln_bwd_mm_nt(name, dy, z, g, w, h=None, *, tm, tn):
    t = dy.shape[0]
    n = w.shape[0]

    def body(*refs):
        if h is None:
            dy_ref, z_ref, g_ref, w_ref, dz_ref, dzb_ref, o_ref, dg_ref, db_ref = refs
        else:
            dy_ref, z_ref, g_ref, w_ref, h_ref, dz_ref, dzb_ref, o_ref, dg_ref, db_ref = refs
        i, j = pl.program_id(0), pl.program_id(1)

        @pl.when(j == 0)
        def _():
            zv = z_ref[...]
            mu = jnp.mean(zv, axis=-1, keepdims=True)
            zc = zv - mu
            var = jnp.mean(zc * zc, axis=-1, keepdims=True)
            rstd = lax.rsqrt(var + LN_EPS)
            xh = zc * rstd
            dyv = dy_ref[...]
            gdy = dyv * g_ref[...]
            m1 = jnp.mean(gdy, axis=-1, keepdims=True)
            m2 = jnp.mean(gdy * xh, axis=-1, keepdims=True)
            dz = rstd * (gdy - m1 - xh * m2)
            dz_ref[...] = dz
            dzb_ref[...] = dz.astype(BF16)

            @pl.when(i == 0)
            def _():
                dg_ref[...] = jnp.zeros_like(dg_ref)
                db_ref[...] = jnp.zeros_like(db_ref)

            dg_ref[...] += jnp.sum(dyv * xh, axis=0, keepdims=True)
            db_ref[...] += jnp.sum(dyv, axis=0, keepdims=True)

        acc = _dot_nt(dzb_ref[...], w_ref[...])
        if h is not None:
            acc = acc * (2.0 * jnp.sqrt(h_ref[...].astype(F32)))
        o_ref[...] = acc.astype(BF16)

    row = pl.BlockSpec((tm, D_MODEL), lambda i, j: (i, 0))
    vec = pl.BlockSpec((1, D_MODEL), lambda i, j: (0, 0))
    tile = pl.BlockSpec((tm, tn), lambda i, j: (i, j))
    in_specs = [row, row, vec, pl.BlockSpec((tn, D_MODEL), lambda i, j: (j, 0))]
    args = [dy, z, g, w]
    if h is not None:
        in_specs.append(tile)
        args.append(h)
    return pl.pallas_call(
        body, name=name, grid=(t // tm, n // tn),
        in_specs=in_specs, out_specs=[row, row, tile, vec, vec],
        out_shape=[jax.ShapeDtypeStruct((t, D_MODEL), F32), jax.ShapeDtypeStruct((t, D_MODEL), BF16),
                   jax.ShapeDtypeStruct((t, n), BF16), jax.ShapeDtypeStruct((1, D_MODEL), F32),
                   jax.ShapeDtypeStruct((1, D_MODEL), F32)],
        compiler_params=_params("arbitrary", "arbitrary"),
    )(*args)


def _mm_tn(name, a, b, *, tk, tmo, tno):
    t, m = a.shape
    n = b.shape[1]
    nk = t // tk

    def body(a_ref, b_ref, o_ref, acc_ref):
        k = pl.program_id(2)
        p = _dot_tn(a_ref[...].astype(BF16), b_ref[...].astype(BF16))

        @pl.when(k == 0)
        def _():
            acc_ref[...] = p

        @pl.when(k > 0)
        def _():
            acc_ref[...] += p

        @pl.when(k == nk - 1)
        def _():
            o_ref[...] = acc_ref[...].astype(BF16)

    return pl.pallas_call(
        body, name=name, grid=(m // tmo, n // tno, nk),
        in_specs=[pl.BlockSpec((tk, tmo), lambda i, j, k: (k, i)), pl.BlockSpec((tk, tno), lambda i, j, k: (k, j))],
        out_specs=pl.BlockSpec((tmo, tno), lambda i, j, k: (i, j)),
        out_shape=jax.ShapeDtypeStruct((m, n), BF16),
        scratch_shapes=[pltpu.VMEM((tmo, tno), F32)],
        compiler_params=_params("parallel", "parallel", "arbitrary"),
    )(a, b)


def _mm_nt_sum(name, pieces, offsets, w, resid, *, tm):
    t = resid.shape[0]
    widths = [p.shape[1] for p in pieces]
    n_p = len(pieces)

    def body(*refs):
        p_refs, w_ref, r_ref, o_ref = refs[:n_p], refs[n_p], refs[n_p + 1], refs[n_p + 2]
        acc = ALPHA * r_ref[...]
        for p_ref, off, wd in zip(p_refs, offsets, widths):
            acc = acc + _dot_nt(p_ref[...], w_ref[:, off:off + wd])
        o_ref[...] = acc

    row = pl.BlockSpec((tm, D_MODEL), lambda i: (i, 0))
    return pl.pallas_call(
        body, name=name, grid=(t // tm,),
        in_specs=[pl.BlockSpec((tm, wd), lambda i: (i, 0)) for wd in widths] + [_resident(w.shape), row],
        out_specs=row,
        out_shape=jax.ShapeDtypeStruct((t, D_MODEL), F32),
        compiler_params=_params("parallel"),
    )(*pieces, w, resid)


def _chunk_masks(rows):
    r = lax.broadcasted_iota(jnp.int32, (rows, rows), 0)
    c = lax.broadcasted_iota(jnp.int32, (rows, rows), 1)
    same = (r // CHUNK) == (c // CHUNK)
    causal = same & (c <= r)
    upto_mid = same & ((c % CHUNK) < CHUNK // 2)
    return causal, upto_mid, same


def _split3(x):
    hi = x.astype(BF16)
    r1 = x - hi.astype(F32)
    mid = r1.astype(BF16)
    lo = (r1 - mid.astype(F32)).astype(BF16)
    return hi, mid, lo


def _mask_sum(mask_b, parts, transpose=False):
    f = _dot_tn if transpose else _dot
    return f(mask_b, parts[0]) + f(mask_b, parts[1]) + f(mask_b, parts[2])


def _hg_gates(q_raw, fl, lb, rows):
    causal, upto_mid, same = _chunk_masks(rows)
    tri = causal.astype(BF16)
    same_b = same.astype(BF16)
    sg = _sigmoid(fl)
    forget = lb + (1.0 - lb) * sg
    k = (1.0 - lb) * _sigmoid(-fl)
    sq = _sigmoid(q_raw)
    qs = q_raw * sq
    parts = _split3(jnp.log(forget))
    bc = _mask_sum(tri, parts)
    bref = _mask_sum(upto_mid.astype(BF16), parts)
    blast = _mask_sum(same_b, parts)
    return dict(mask=causal, tri=tri, same=same_b, sg=sg, forget=forget, k=k, sq=sq, qs=qs,
                e_a=jnp.exp(bc - bref), e_b=jnp.exp(bref - bc), e_q=jnp.exp(bc), e_k=jnp.exp(blast - bc),
                dec=jnp.exp(blast))


def _hgrn_fwd(proj, lb, nw, *, bsz, seq):
    rows = min(ROWS_HG, seq)
    nt = seq // rows
    nc = rows // CHUNK
    t = bsz * seq

    def body(q_ref, f_ref, v_ref, g_ref, lb_ref, nw_ref, y_ref, o_ref, st_ref, s_scr):
        j = pl.program_id(2)

        @pl.when(j == 0)
        def _():
            s_scr[...] = jnp.zeros_like(s_scr)

        gt = _hg_gates(q_ref[...], f_ref[...], lb_ref[...], rows)
        v_b = v_ref[...].astype(BF16)
        a_b = (gt["qs"] * gt["e_a"]).astype(BF16)
        b_b = (gt["k"] * gt["e_b"]).astype(BF16)
        qi_b = (gt["qs"] * gt["e_q"]).astype(BF16)
        ko_b = (gt["k"] * gt["e_k"]).astype(BF16)
        scores = jnp.where(gt["mask"], _dot_nt(a_b, b_b), 0.0)
        o_intra = _dot(scores.astype(BF16), v_b)

        s = s_scr[...]
        parts = []
        for n in range(nc):
            sl = slice(n * CHUNK, (n + 1) * CHUNK)
            s_b = s.astype(BF16)
            st_ref[n] = s_b
            parts.append(_dot_nt(qi_b[sl], s_b))
            s = s * gt["dec"][n * CHUNK:n * CHUNK + 1] + _dot_tn(v_b[sl], ko_b[sl])
        s_scr[...] = s
        o = o_intra + jnp.concatenate(parts, axis=0)
        o_ref[...] = o
        r = lax.rsqrt(jnp.mean(o * o, axis=-1, keepdims=True) + RMS_EPS)
        g = g_ref[...]
        y_ref[...] = (o * r * nw_ref[...] * (g * _sigmoid(g))).astype(BF16)

    def col(base):
        return pl.BlockSpec((rows, HEAD_DIM), lambda h, b, j: (b * nt + j, base // HEAD_DIM + h))

    out_tile = pl.BlockSpec((rows, HEAD_DIM), lambda h, b, j: (b * nt + j, h))
    return pl.pallas_call(
        body, name="hgrn_fwd", grid=(HEADS, bsz, nt),
        in_specs=[col(C_HQ), col(C_HF), col(C_HI), col(C_HG),
                  pl.BlockSpec((1, HEAD_DIM), lambda h, b, j: (0, h)), _resident((1, HEAD_DIM))],
        out_specs=[out_tile, out_tile,
                   pl.BlockSpec((None, None, nc, HEAD_DIM, HEAD_DIM), lambda h, b, j: (h, b, j, 0, 0))],
        out_shape=[jax.ShapeDtypeStruct((t, WIDTH), BF16), jax.ShapeDtypeStruct((t, WIDTH), F32),
                   jax.ShapeDtypeStruct((HEADS, bsz, seq // CHUNK, HEAD_DIM, HEAD_DIM), BF16)],
        scratch_shapes=[pltpu.VMEM((HEAD_DIM, HEAD_DIM), F32)],
        compiler_params=_params("parallel", "parallel", "arbitrary"),
    )(proj, proj, proj, proj, lb, nw)


def _hgrn_bwd(proj, lb, nw, o_pre, states, dy, *, bsz, seq):
    rows = min(ROWS_HG, seq)
    nt = seq // rows
    nc = rows // CHUNK
    t = bsz * seq

    def body(q_ref, f_ref, v_ref, g_ref, lb_ref, nw_ref, o_ref, st_ref, dy_ref,
             dq_ref, df_ref, dv_ref, dg_ref, dlb_ref, dnw_ref, ds_scr):
        h, b, j = pl.program_id(0), pl.program_id(1), pl.program_id(2)

        @pl.when(j == 0)
        def _():
            ds_scr[...] = jnp.zeros_like(ds_scr)

        @pl.when((b == 0) & (j == 0))
        def _():
            dlb_ref[...] = jnp.zeros_like(dlb_ref)

        @pl.when((h == 0) & (b == 0) & (j == 0))
        def _():
            dnw_ref[...] = jnp.zeros_like(dnw_ref)

        q_raw, fl, lb_v = q_ref[...], f_ref[...], lb_ref[...]
        gt = _hg_gates(q_raw, fl, lb_v, rows)
        mask = gt["mask"]
        v = v_ref[...]
        v_b = v.astype(BF16)
        a_f = gt["qs"] * gt["e_a"]
        b_f = gt["k"] * gt["e_b"]
        qi_f = gt["qs"] * gt["e_q"]
        ko_f = gt["k"] * gt["e_k"]
        a_b, b_b, qi_b, ko_b = a_f.astype(BF16), b_f.astype(BF16), qi_f.astype(BF16), ko_f.astype(BF16)

        o = o_ref[...]
        nw_v = nw_ref[...]
        g = g_ref[...]
        dyv = dy_ref[...]
        r = lax.rsqrt(jnp.mean(o * o, axis=-1, keepdims=True) + RMS_EPS)
        sgg = _sigmoid(g)
        d_g = dyv * (o * r * nw_v) * (sgg * (1.0 + g * (1.0 - sgg)))
        d_on = dyv * (g * sgg)
        dnw_ref[...] += jnp.sum(d_on * o * r, axis=0, keepdims=True)
        tt = d_on * nw_v
        d_o = r * tt - o * (r * r * r) * jnp.mean(tt * o, axis=-1, keepdims=True)
        do_b = d_o.astype(BF16)

        sc_b = jnp.where(mask, _dot_nt(a_b, b_b), 0.0).astype(BF16)
        dsc_b = jnp.where(mask, _dot_nt(do_b, v_b), 0.0).astype(BF16)
        d_v = _dot_tn(sc_b, do_b)
        d_a = _dot(dsc_b, b_b)
        d_bm = _dot_tn(dsc_b, a_b)

        ds = ds_scr[...]
        dqi_parts, dko_parts, dvi_parts, ddec_parts = [None] * nc, [None] * nc, [None] * nc, [None] * nc
        for n in reversed(range(nc)):
            sl = slice(n * CHUNK, (n + 1) * CHUNK)
            dec_n = gt["dec"][n * CHUNK:n * CHUNK + 1]
            ds_b = ds.astype(BF16)
            s_n = st_ref[n]
            dqi_parts[n] = _dot(do_b[sl], s_n)
            dko_parts[n] = _dot(v_b[sl], ds_b)
            dvi_parts[n] = _dot_nt(ko_b[sl], ds_b)
            d_dec = jnp.sum(ds * s_n.astype(F32), axis=0, keepdims=True)
            ddec_parts[n] = jnp.broadcast_to(d_dec * dec_n, (CHUNK, HEAD_DIM))
            ds = ds * dec_n + _dot_tn(do_b[sl], qi_b[sl])
        ds_scr[...] = ds
        d_qi = jnp.concatenate(dqi_parts, axis=0)
        d_ko = jnp.concatenate(dko_parts, axis=0)
        d_v = d_v + jnp.concatenate(dvi_parts, axis=0)

        d_qs = d_a * gt["e_a"] + d_qi * gt["e_q"]
        d_k = d_bm * gt["e_b"] + d_ko * gt["e_k"]
        t_a, t_b, t_q, t_k = d_a * a_f, d_bm * b_f, d_qi * qi_f, d_ko * ko_f
        d_bref = _mask_sum(gt["same"], _split3(t_b - t_a))
        d_blast = _mask_sum(gt["same"], _split3(t_k)) + jnp.concatenate(ddec_parts, axis=0)
        pos = lax.broadcasted_iota(jnp.int32, (rows, HEAD_DIM), 0) % CHUNK
        d_bc = (t_a - t_b + t_q - t_k + jnp.where(pos == CHUNK // 2 - 1, d_bref, 0.0)
                + jnp.where(pos == CHUNK - 1, d_blast, 0.0))
        d_logf = _mask_sum(gt["tri"], _split3(d_bc), transpose=True)

        sg, forget = gt["sg"], gt["forget"]
        one_m_sg = 1.0 - sg
        common = (1.0 - lb_v) * sg * one_m_sg
        d_fl = d_logf * common / forget - d_k * common
        dlb_ref[...] += jnp.sum(d_logf * one_m_sg / forget - d_k * one_m_sg, axis=0, keepdims=True)
        sq = gt["sq"]
        dq_ref[...] = (d_qs * (sq * (1.0 + q_raw * (1.0 - sq)))).astype(BF16)
        df_ref[...] = d_fl.astype(BF16)
        dv_ref[...] = d_v.astype(BF16)
        dg_ref[...] = d_g.astype(BF16)

    def rowblk(b, j):
        return b * nt + (nt - 1 - j)

    def col(base):
        return pl.BlockSpec((rows, HEAD_DIM), lambda h, b, j: (rowblk(b, j), base // HEAD_DIM + h))

    tile = pl.BlockSpec((rows, HEAD_DIM), lambda h, b, j: (rowblk(b, j), h))
    head_vec = pl.BlockSpec((1, HEAD_DIM), lambda h, b, j: (0, h))
    d_out = jax.ShapeDtypeStruct((t, WIDTH), BF16)
    return pl.pallas_call(
        body, name="hgrn_bwd", grid=(HEADS, bsz, nt),
        in_specs=[col(C_HQ), col(C_HF), col(C_HI), col(C_HG), head_vec, _resident((1, HEAD_DIM)), tile,
                  pl.BlockSpec((None, None, nc, HEAD_DIM, HEAD_DIM), lambda h, b, j: (h, b, nt - 1 - j, 0, 0)),
                  tile],
        out_specs=[tile, tile, tile, tile, head_vec, _resident((1, HEAD_DIM))],
        out_shape=[d_out, d_out, d_out, d_out, jax.ShapeDtypeStruct((1, WIDTH), F32),
                   jax.ShapeDtypeStruct((1, HEAD_DIM), F32)],
        scratch_shapes=[pltpu.VMEM((HEAD_DIM, HEAD_DIM), F32)],
        compiler_params=_params("arbitrary", "arbitrary", "arbitrary"),
    )(proj, proj, proj, proj, lb, nw, o_pre, states, dy)


def _mem_kv(mem2d, w_k, w_v):
    rows = mem2d.shape[0]

    def body(m_ref, wk_ref, wv_ref, k_ref, v_ref):
        m_b = m_ref[...].astype(BF16)
        k_ref[...] = _dot(m_b, wk_ref[...]).astype(BF16)
        v_ref[...] = _dot(m_b, wv_ref[...]).astype(BF16)

    return pl.pallas_call(
        body, name="mem_kv", grid=(rows // MEM_LEN,),
        in_specs=[pl.BlockSpec((MEM_LEN, D_MODEL), lambda i: (i, 0)), _resident((D_MODEL, WIDTH)),
                  _resident((D_MODEL, WIDTH))],
        out_specs=[pl.BlockSpec((MEM_LEN, WIDTH), lambda i: (i, 0))] * 2,
        out_shape=[jax.ShapeDtypeStruct((rows, WIDTH), BF16)] * 2,
        compiler_params=_params("parallel"),
    )(mem2d, w_k, w_v)


def _softmax_rows(s):
    m = jnp.max(s, axis=-1, keepdims=True)
    e = jnp.exp(s - m)
    return e / jnp.sum(e, axis=-1, keepdims=True)


def _attn_fwd(proj, mk, mv, *, tm, seq):
    t = proj.shape[0]
    per_b = seq // tm
    scale = HEAD_DIM ** -0.5

    def body(q_ref, k_ref, v_ref, y_ref):
        outs = []
        for h in range(HEADS):
            sl = slice(h * HEAD_DIM, (h + 1) * HEAD_DIM)
            s = _dot_nt(q_ref[:, sl].astype(BF16), k_ref[:, sl]) * scale
            p = _softmax_rows(s)
            outs.append(_dot(p.astype(BF16), v_ref[:, sl]))
        y_ref[...] = jnp.concatenate(outs, axis=-1).astype(BF16)

    kv = pl.BlockSpec((MEM_LEN, WIDTH), lambda i: (i // per_b, 0))
    return pl.pallas_call(
        body, name="attn_fwd", grid=(t // tm,),
        in_specs=[pl.BlockSpec((tm, WIDTH), lambda i: (i, C_MQ // WIDTH)), kv, kv],
        out_specs=pl.BlockSpec((tm, WIDTH), lambda i: (i, 0)),
        out_shape=jax.ShapeDtypeStruct((t, WIDTH), BF16),
        compiler_params=_params("parallel"),
    )(proj, mk, mv)


def _attn_bwd(proj, mk, mv, dy, *, tm, seq):
    t = proj.shape[0]
    per_b = seq // tm
    scale = HEAD_DIM ** -0.5

    def body(q_ref, k_ref, v_ref, dy_ref, dq_ref, dk_ref, dv_ref):
        i = pl.program_id(0)

        @pl.when(i % per_b == 0)
        def _():
            dk_ref[...] = jnp.zeros_like(dk_ref)
            dv_ref[...] = jnp.zeros_like(dv_ref)

        dqs, dks, dvs = [], [], []
        for h in range(HEADS):
            sl = slice(h * HEAD_DIM, (h + 1) * HEAD_DIM)
            q_b = q_ref[:, sl].astype(BF16)
            k_b, v_b = k_ref[:, sl], v_ref[:, sl]
            p = _softmax_rows(_dot_nt(q_b, k_b) * scale)
            dy_b = dy_ref[:, sl].astype(BF16)
            dp = _dot_nt(dy_b, v_b)
            dvs.append(_dot_tn(p.astype(BF16), dy_b))
            ds_b = (p * (dp - jnp.sum(dp * p, axis=-1, keepdims=True)) * scale).astype(BF16)
            dqs.append(_dot(ds_b, k_b))
            dks.append(_dot_tn(ds_b, q_b))
        dq_ref[...] = jnp.concatenate(dqs, axis=-1).astype(BF16)
        dk_ref[...] += jnp.concatenate(dks, axis=-1)
        dv_ref[...] += jnp.concatenate(dvs, axis=-1)

    kv = pl.BlockSpec((MEM_LEN, WIDTH), lambda i: (i // per_b, 0))
    tile = pl.BlockSpec((tm, WIDTH), lambda i: (i, 0))
    n_mem = mk.shape[0]
    return pl.pallas_call(
        body, name="attn_bwd", grid=(t // tm,),
        in_specs=[pl.BlockSpec((tm, WIDTH), lambda i: (i, C_MQ // WIDTH)), kv, kv, tile],
        out_specs=[tile, kv, kv],
        out_shape=[jax.ShapeDtypeStruct((t, WIDTH), BF16), jax.ShapeDtypeStruct((n_mem, WIDTH), F32),
                   jax.ShapeDtypeStruct((n_mem, WIDTH), F32)],
        compiler_params=_params("arbitrary"),
    )(proj, mk, mv, dy)


HALO = 8


def _shift_down(u, halo, k, row):
    out = pltpu.roll(u, k, 0)
    for m in range(k):
        out = jnp.where(row == m, halo[HALO - k + m:HALO - k + m + 1, :], out)
    return out


def _shift_up(u, halo, k, row, tm):
    out = pltpu.roll(u, tm - k, 0)
    for m in range(k):
        out = jnp.where(row == tm - k + m, halo[m:m + 1, :], out)
    return out


def _merge_fwd(proj, y_b, y_c, conv_w, w_branch, b_gate, *, tm, seq):
    t = proj.shape[0]
    per_b = seq // tm
    hb = tm // HALO

    def body(cb_ref, cc_ref, ch_ref, cch_ref, chh_ref, ga_ref, gb_ref, gc_ref, yb_ref, yc_ref, cw_ref, wb_ref,
             bg_ref, ya_ref, pa_ref, pb_ref, pc_ref, mg_ref):
        i = pl.program_id(0)
        row = lax.broadcasted_iota(jnp.int32, (tm, WIDTH), 0)
        u = cc_ref[...] * ch_ref[...]
        halo = jnp.where(i % per_b == 0, 0.0, cch_ref[...] * chh_ref[...])
        cw = cw_ref[...]
        y = cw[0:1] * _shift_down(u, halo, 2, row) + cw[1:2] * _shift_down(u, halo, 1, row) + cw[2:3] * u
        ya_b = (cb_ref[...] * y).astype(BF16)
        ya_ref[...] = ya_b
        merged = None
        for idx, (y_in, g_ref, p_ref) in enumerate(((ya_b, ga_ref, pa_ref), (yb_ref[...], gb_ref, pb_ref),
                                                     (yc_ref[...], gc_ref, pc_ref))):
            p = _dot(y_in, wb_ref[idx])
            p_ref[...] = p.astype(BF16)
            term = _sigmoid(g_ref[...] + bg_ref[:, idx * D_MODEL:(idx + 1) * D_MODEL]) * p
            merged = term if merged is None else merged + term
        mg_ref[...] = merged.astype(BF16)

    def half(c):
        return pl.BlockSpec((tm, WIDTH), lambda i: (i, c // WIDTH))

    def prev(c):
        return pl.BlockSpec((HALO, WIDTH), lambda i: (jnp.maximum(i * hb - 1, 0), c // WIDTH))

    def gate(k):
        return pl.BlockSpec((tm, D_MODEL), lambda i: (i, C_GA // D_MODEL + k))

    tile512 = pl.BlockSpec((tm, WIDTH), lambda i: (i, 0))
    tile1k = pl.BlockSpec((tm, D_MODEL), lambda i: (i, 0))
    return pl.pallas_call(
        body, name="merge_fwd", grid=(t // tm,),
        in_specs=[half(C_CB), half(C_CC), half(C_CH), prev(C_CC), prev(C_CH), gate(0), gate(1), gate(2),
                  tile512, tile512, _resident((CONV_K, WIDTH)), _resident((3, WIDTH, D_MODEL)),
                  _resident((1, 3 * D_MODEL))],
        out_specs=[tile512, tile1k, tile1k, tile1k, tile1k],
        out_shape=[jax.ShapeDtypeStruct((t, WIDTH), BF16)] + [jax.ShapeDtypeStruct((t, D_MODEL), BF16)] * 4,
        compiler_params=_params("parallel"),
    )(proj, proj, proj, proj, proj, proj, proj, proj, y_b, y_c, conv_w, w_branch, b_gate)


def _merge_bwd(proj, dmerged, pa, pb, pc, w_branch, b_gate, *, tm):
    t = proj.shape[0]

    def body(dm_ref, pa_ref, pb_ref, pc_ref, ga_ref, gb_ref, gc_ref, wb_ref, bg_ref,
             dgt_ref, dpa_ref, dpb_ref, dpc_ref, dya_ref, dyb_ref, dyc_ref, dbg_ref):
        i = pl.program_id(0)

        @pl.when(i == 0)
        def _():
            dbg_ref[...] = jnp.zeros_like(dbg_ref)

        dm = dm_ref[...].astype(F32)
        for idx, (p_ref, g_ref, dp_ref, dy_ref) in enumerate(((pa_ref, ga_ref, dpa_ref, dya_ref),
                                                              (pb_ref, gb_ref, dpb_ref, dyb_ref),
                                                              (pc_ref, gc_ref, dpc_ref, dyc_ref))):
            cols = slice(idx * D_MODEL, (idx + 1) * D_MODEL)
            sg = _sigmoid(g_ref[...] + bg_ref[:, cols])
            dp_b = (dm * sg).astype(BF16)
            dp_ref[...] = dp_b
            dgate = dm * p_ref[...].astype(F32) * sg * (1.0 - sg)
            dgt_ref[:, cols] = dgate.astype(BF16)
            dbg_ref[:, cols] += jnp.sum(dgate, axis=0, keepdims=True)
            dy_ref[...] = _dot_nt(dp_b, wb_ref[idx])

    def gate(k):
        return pl.BlockSpec((tm, D_MODEL), lambda i: (i, C_GA // D_MODEL + k))

    tile512 = pl.BlockSpec((tm, WIDTH), lambda i: (i, 0))
    tile1k = pl.BlockSpec((tm, D_MODEL), lambda i: (i, 0))
    return pl.pallas_call(
        body, name="merge_bwd", grid=(t // tm,),
        in_specs=[tile1k, tile1k, tile1k, tile1k, gate(0), gate(1), gate(2), _resident((3, WIDTH, D_MODEL)),
                  _resident((1, 3 * D_MODEL))],
        out_specs=[pl.BlockSpec((tm, 3 * D_MODEL), lambda i: (i, 0)), tile1k, tile1k, tile1k,
                   tile512, tile512, tile512, _resident((1, 3 * D_MODEL))],
        out_shape=[jax.ShapeDtypeStruct((t, 3 * D_MODEL), BF16)] + [jax.ShapeDtypeStruct((t, D_MODEL), BF16)] * 3
                  + [jax.ShapeDtypeStruct((t, WIDTH), F32)] * 3 + [jax.ShapeDtypeStruct((1, 3 * D_MODEL), F32)],
        compiler_params=_params("arbitrary"),
    )(dmerged, pa, pb, pc, proj, proj, proj, w_branch, b_gate)


def _conv_bwd(proj, dya, conv_w, *, tm, seq):
    t = proj.shape[0]
    per_b = seq // tm
    hb = tm // HALO
    last_blk = t // HALO - 1

    def body(cb_ref, cc_ref, ch_ref, cch_ref, chh_ref, dya_ref, cbn_ref, dyan_ref, cw_ref, d_ref, dcw_ref):
        i = pl.program_id(0)

        @pl.when(i == 0)
        def _():
            dcw_ref[...] = jnp.zeros_like(dcw_ref)

        row = lax.broadcasted_iota(jnp.int32, (tm, WIDTH), 0)
        cb, cc, ch = cb_ref[...], cc_ref[...], ch_ref[...]
        u = cc * ch
        halo = jnp.where(i % per_b == 0, 0.0, cch_ref[...] * chh_ref[...])
        u1 = _shift_down(u, halo, 1, row)
        u2 = _shift_down(u, halo, 2, row)
        cw = cw_ref[...]
        y = cw[0:1] * u2 + cw[1:2] * u1 + cw[2:3] * u
        dya = dya_ref[...]
        dy = dya * cb
        nxt = jnp.where(i % per_b == per_b - 1, 0.0, dyan_ref[...] * cbn_ref[...])
        du = cw[2:3] * dy + cw[1:2] * _shift_up(dy, nxt, 1, row, tm) + cw[0:1] * _shift_up(dy, nxt, 2, row, tm)
        d_ref[:, 0:WIDTH] = (dya * y).astype(BF16)
        d_ref[:, WIDTH:2 * WIDTH] = (du * ch).astype(BF16)
        d_ref[:, 2 * WIDTH:3 * WIDTH] = (du * cc).astype(BF16)
        dcw_ref[0:1, :] += jnp.sum(dy * u2, axis=0, keepdims=True)
        dcw_ref[1:2, :] += jnp.sum(dy * u1, axis=0, keepdims=True)
        dcw_ref[2:3, :] += jnp.sum(dy * u, axis=0, keepdims=True)

    def half(c):
        return pl.BlockSpec((tm, WIDTH), lambda i: (i, c // WIDTH))

    def prev(c):
        return pl.BlockSpec((HALO, WIDTH), lambda i: (jnp.maximum(i * hb - 1, 0), c // WIDTH))

    def nxt(c):
        return pl.BlockSpec((HALO, WIDTH), lambda i: (jnp.minimum((i + 1) * hb, last_blk), c // WIDTH))

    return pl.pallas_call(
        body, name="conv_bwd", grid=(t // tm,),
        in_specs=[half(C_CB), half(C_CC), half(C_CH), prev(C_CC), prev(C_CH),
                  pl.BlockSpec((tm, WIDTH), lambda i: (i, 0)), nxt(C_CB), nxt(0), _resident((CONV_K, WIDTH))],
        out_specs=[pl.BlockSpec((tm, 3 * WIDTH), lambda i: (i, 0)), _resident((CONV_K, WIDTH))],
        out_shape=[jax.ShapeDtypeStruct((t, 3 * WIDTH), BF16), jax.ShapeDtypeStruct((CONV_K, WIDTH), F32)],
        compiler_params=_params("arbitrary"),
    )(proj, proj, proj, proj, proj, dya, proj, dya, conv_w)


def _loss_head(y, target, *, tm):
    t = y.shape[0]

    def body(y_ref, t_ref, dy_ref, l_ref):
        @pl.when(pl.program_id(0) == 0)
        def _():
            l_ref[...] = jnp.zeros_like(l_ref)

        err = y_ref[...] - t_ref[...]
        dy_ref[...] = err * (1.0 / D_MODEL)
        per_row = jnp.sum(err * err, axis=-1, keepdims=True) * (1.0 / D_MODEL)
        l_ref[...] += 0.5 * jnp.sum(per_row, axis=0, keepdims=True)

    row = pl.BlockSpec((tm, D_MODEL), lambda i: (i, 0))
    return pl.pallas_call(
        body, name="loss_head", grid=(t // tm,),
        in_specs=[row, row], out_specs=[row, _resident((8, 128))],
        out_shape=[jax.ShapeDtypeStruct((t, D_MODEL), F32), jax.ShapeDtypeStruct((8, 128), F32)],
        compiler_params=_params("arbitrary"),
    )(y, target)


def _lb_softmax(lower_bounds):
    x = lower_bounds
    e = jnp.exp(x - jnp.max(x, axis=0, keepdims=True))
    return e / jnp.sum(e, axis=0, keepdims=True)


def _lb_fwd(lower_bounds):
    def body(x_ref, o_ref):
        s = _lb_softmax(x_ref[...])
        c = s[0:1]
        o_ref[0:1, :] = c - s[0:1]
        for l in range(1, DEPTH):
            c = c + s[l:l + 1]
            o_ref[l:l + 1, :] = c - s[0:1]

    return pl.pallas_call(body, name="lb_fwd", out_shape=jax.ShapeDtypeStruct(lower_bounds.shape, F32))(lower_bounds)


def _lb_bwd(lower_bounds, d_lb_all):
    def body(x_ref, d_ref, o_ref):
        s = _lb_softmax(x_ref[...])
        d = d_ref[...]
        rows = [jnp.zeros_like(d[0:1])]
        for j in range(1, DEPTH):
            acc = d[j:j + 1]
            for l in range(j + 1, DEPTH):
                acc = acc + d[l:l + 1]
            rows.append(acc)
        inner = rows[0] * s[0:1]
        for j in range(1, DEPTH):
            inner = inner + rows[j] * s[j:j + 1]
        for j in range(DEPTH):
            o_ref[j:j + 1, :] = s[j:j + 1] * (rows[j] - inner)

    return pl.pallas_call(body, name="lb_bwd", out_shape=jax.ShapeDtypeStruct(lower_bounds.shape, F32))(
        lower_bounds, d_lb_all)


def _adamw(w, g, m, v):
    m2 = ADAM_B1 * m + (1.0 - ADAM_B1) * g
    v2 = ADAM_B2 * v + (1.0 - ADAM_B2) * (g * g)
    m_hat = m2 / (1.0 - ADAM_B1 ** ADAM_STEP)
    v_hat = v2 / (1.0 - ADAM_B2 ** ADAM_STEP)
    delta = -ADAM_LR * (m_hat / (jnp.sqrt(v_hat) + ADAM_EPS) + ADAM_WD * w)
    return delta, m2, v2


def _adam_small(name, g, w, m, v):
    shape = w.shape
    flat = (-1, shape[-1])
    g2, w2, m2, v2 = (a.reshape(flat) for a in (g, w, m, v))

    def body(g_ref, w_ref, m_ref, v_ref, d_ref, mo_ref, vo_ref):
        d, mm, vv = _adamw(w_ref[...], g_ref[...], m_ref[...], v_ref[...])
        d_ref[...] = d
        mo_ref[...] = mm
        vo_ref[...] = vv

    outs = pl.pallas_call(body, name=name, out_shape=[jax.ShapeDtypeStruct(w2.shape, F32)] * 3)(g2, w2, m2, v2)
    return [o.reshape(shape) for o in outs]


def _adam_shard(name, recvs, w, m, v, *, tr):
    _, r, c = w.shape

    def body(*refs):
        rc, (w_ref, m_ref, v_ref), (g_ref, d_ref, mo_ref, vo_ref) = refs[:DEPTH], refs[DEPTH:DEPTH + 3], refs[DEPTH + 3:]
        layer = pl.program_id(0)
        for cand in range(DEPTH):
            @pl.when(layer == cand)
            def _():
                g = rc[cand][0].astype(F32)
                for d in range(1, N_DEV):
                    g = g + rc[cand][d].astype(F32)
                dl, mm, vv = _adamw(w_ref[...], g, m_ref[...], v_ref[...])
                g_ref[...] = g
                d_ref[...] = dl
                mo_ref[...] = mm
                vo_ref[...] = vv

    def recv_spec(cand):
        return pl.BlockSpec((N_DEV, tr, c), lambda l, i: (0, jnp.where(l == cand, i, 0), 0))

    tile = pl.BlockSpec((None, tr, c), lambda l, i: (l, i, 0))
    return pl.pallas_call(
        body, name=name, grid=(DEPTH, r // tr),
        in_specs=[recv_spec(cand) for cand in range(DEPTH)] + [tile] * 3,
        out_specs=[tile] * 4,
        out_shape=[jax.ShapeDtypeStruct(w.shape, F32)] * 4,
        compiler_params=_params("parallel", "parallel"),
    )(*recvs, w, m, v)


def _sum_devices(name, x):
    def body(x_ref, o_ref):
        acc = x_ref[0]
        for d in range(1, N_DEV):
            acc = acc + x_ref[d]
        o_ref[...] = acc

    return pl.pallas_call(body, name=name, out_shape=jax.ShapeDtypeStruct(x.shape[1:], x.dtype))(x)


SMALL = (("lower_bounds", 512), ("conv_w", CONV_K * WIDTH), ("hg_norm_w", HEAD_DIM), ("b_gate", 3 * D_MODEL),
         ("ln1_g", D_MODEL), ("ln1_b", D_MODEL), ("ln2_g", D_MODEL), ("ln2_b", D_MODEL))
SMALL_PER_LAYER = sum(n for _, n in SMALL)
SMALL_ROWS = 296


def _natural_cols(g):
    nd = g.ndim
    perm = tuple(range(1, nd - 1)) + (0, nd - 1)
    t = jnp.transpose(g, perm)
    return t.reshape(t.shape[:-2] + (t.shape[-2] * t.shape[-1],))


def _natural_rows(g):
    t = jnp.transpose(g, (1, 0, 2, 3))
    return t.reshape(t.shape[0], t.shape[1] * t.shape[2], t.shape[3])


def _layer_fwd(cur, mem2d, wl, *, bsz, seq):
    tm = min(512, seq)
    proj = _mm_nn("in_proj", cur, wl["w_in"], tm=min(1024, seq), tn=1024, out_dtype=F32)
    y_b, o_pre, states = _hgrn_fwd(proj, wl["lb"], wl["nw"], bsz=bsz, seq=seq)
    mk, mv = _mem_kv(mem2d, wl["w_mk"], wl["w_mv"])
    y_c = _attn_fwd(proj, mk, mv, tm=tm, seq=seq)
    y_a, pa, pb, pc, merged = _merge_fwd(proj, y_b, y_c, wl["conv"], wl["w_br"], wl["b_gate"], tm=tm, seq=seq)
    z1, x1 = _linear_ln("wo_ln", merged, wl["w_o"], cur, wl["ln1_g"], wl["ln1_b"], tm=tm)
    hid = _mm_nn("mlp_up", x1, wl["w_up"], tm=tm, tn=1024, out_dtype=BF16, relu2=True)
    z2, x2 = _linear_ln("down_ln", hid, wl["w_down"], x1, wl["ln2_g"], wl["ln2_b"], tm=tm)
    return dict(x=cur, proj=proj, y_a=y_a, y_b=y_b, y_c=y_c, o_pre=o_pre, states=states, mk=mk, mv=mv,
                pa=pa, pb=pb, pc=pc, merged=merged, z1=z1, x1=x1, hid=hid, z2=z2, x2=x2)


def _layer_bwd(dcur, mem2d, s, wl, *, bsz, seq):
    tm = min(512, seq)
    tk = min(1024, bsz * seq)
    dz2, dz2_b, dhpre, d_ln2g, d_ln2b = _ln_bwd_mm_nt("ln2_bwd_down", dcur, s["z2"], wl["ln2_g"], wl["w_down"],
                                                      s["hid"], tm=tm, tn=1024)
    g_down = _mm_tn("grad_w_down", s["hid"], dz2_b, tk=tk, tmo=1024, tno=1024)
    dx1 = _mm_nt_sum("mlp_up_bwd", [dhpre], [0], wl["w_up"], dz2, tm=tm)
    g_up = _mm_tn("grad_w_up", s["x1"], dhpre, tk=tk, tmo=1024, tno=1024)
    dz1, dz1_b, dmerged, d_ln1g, d_ln1b = _ln_bwd_mm_nt("ln1_bwd_wo", dx1, s["z1"], wl["ln1_g"], wl["w_o"],
                                                        tm=tm, tn=1024)
    g_o = _mm_tn("grad_w_o", s["merged"], dz1_b, tk=tk, tmo=1024, tno=1024)
    dgate, dpa, dpb, dpc, dya, dyb, dyc, d_bg = _merge_bwd(s["proj"], dmerged, s["pa"], s["pb"], s["pc"],
                                                           wl["w_br"], wl["b_gate"], tm=tm)
    g_br = jnp.stack([_mm_tn("grad_w_branch", yy, dp, tk=tk, tmo=512, tno=1024)
                      for yy, dp in ((s["y_a"], dpa), (s["y_b"], dpb), (s["y_c"], dpc))])
    d_conv, d_cw = _conv_bwd(s["proj"], dya, wl["conv"], tm=tm, seq=seq)
    dq, df, di, dg, d_lb, d_nw = _hgrn_bwd(s["proj"], wl["lb"], wl["nw"], s["o_pre"], s["states"], dyb,
                                           bsz=bsz, seq=seq)
    dmq, dmk, dmv = _attn_bwd(s["proj"], s["mk"], s["mv"], dyc, tm=tm, seq=seq)
    tkm = min(512, mem2d.shape[0])
    g_mk = _mm_tn("grad_w_mem", mem2d, dmk, tk=tkm, tmo=1024, tno=512)
    g_mv = _mm_tn("grad_w_mem", mem2d, dmv, tk=tkm, tmo=1024, tno=512)
    pieces = [d_conv, dq, df, di, dg, dmq, dgate]
    offsets = [C_CB, C_HQ, C_HF, C_HI, C_HG, C_MQ, C_GA]
    dx = _mm_nt_sum("in_proj_bwd", pieces, offsets, wl["w_in"], dz1, tm=min(256, seq))
    g_in = jnp.concatenate(
        [_mm_tn("grad_w_in_%d" % p.shape[1], s["x"], p, tk=tk, tmo=1024,
                tno=(1024 if p.shape[1] % 1024 == 0 else 512)) for p in pieces], axis=1)
    small = jnp.concatenate([d_lb[0], d_cw.reshape(-1), d_nw[0], d_bg[0], d_ln1g[0], d_ln1b[0], d_ln2g[0],
                             d_ln2b[0]])
    return dx, [g_in, g_mk, g_mv, g_br, g_o, g_up, g_down], small


def kernel(x, mem, lower_bounds, w_in, conv_w, hg_norm_w, w_mem_k, w_mem_v, w_branch, b_gate, w_o, ln1_g, ln1_b, w_up, w_down, ln2_g, ln2_b, loss_target, m_lower_bounds, m_w_in, m_conv_w, m_hg_norm_w, m_w_mem_k, m_w_mem_v, m_w_branch, m_b_gate, m_w_o, m_ln1_g, m_ln1_b, m_w_up, m_w_down, m_ln2_g, m_ln2_b, v_lower_bounds, v_w_in, v_conv_w, v_hg_norm_w, v_w_mem_k, v_w_mem_v, v_w_branch, v_b_gate, v_w_o, v_ln1_g, v_ln1_b, v_w_up, v_w_down, v_ln2_g, v_ln2_b):
    bsz, seq, _ = x.shape
    t = bsz * seq
    me = _my_id()

    shards = [w_in.astype(BF16), w_mem_k.astype(BF16), w_mem_v.astype(BF16), w_branch.astype(BF16),
              w_o.astype(BF16), w_up.astype(BF16), w_down.astype(BF16),
              conv_w.reshape(DEPTH * CONV_K * (WIDTH // N_DEV) // 128, 128)]
    gathered = _exchange("gather_weights", shards, [_whole] * len(shards), [s.shape for s in shards])
    wn_in = _natural_cols(gathered[0])
    wn_mk = _natural_rows(gathered[1])
    wn_mv = _natural_rows(gathered[2])
    wn_br = _natural_cols(gathered[3])
    wn_o = _natural_rows(gathered[4])
    wn_up = _natural_cols(gathered[5])
    wn_down = _natural_rows(gathered[6])
    conv_full = _natural_cols(gathered[7].reshape(N_DEV, DEPTH, CONV_K, WIDTH // N_DEV))

    lb_all = _lb_fwd(lower_bounds)
    x2d = x.reshape(t, D_MODEL)
    mem2d = mem.reshape(bsz * MEM_LEN, D_MODEL)
    target2d = loss_target.reshape(t, D_MODEL)

    saved = []
    cur = x2d
    for l in range(DEPTH):
        wl = dict(w_in=wn_in[l], w_mk=wn_mk[l], w_mv=wn_mv[l], w_br=wn_br[l], w_o=wn_o[l], w_up=wn_up[l],
                  w_down=wn_down[l], conv=conv_full[l], lb=lb_all[l][None], nw=hg_norm_w[l][None],
                  b_gate=b_gate[l][None], ln1_g=ln1_g[l][None], ln1_b=ln1_b[l][None], ln2_g=ln2_g[l][None],
                  ln2_b=ln2_b[l][None])
        s = _layer_fwd(cur, mem2d, wl, bsz=bsz, seq=seq)
        saved.append((s, wl))
        cur = s["x2"]

    dcur, loss_tile = _loss_head(cur, target2d, tm=min(512, seq))
    loss = lax.psum(loss_tile[0, 0], ("x", "y", "c"))

    recv = [None] * DEPTH
    small_rows = [None] * DEPTH
    for l in reversed(range(DEPTH)):
        s, wl = saved[l]
        dcur, big_grads, small_rows[l] = _layer_bwd(dcur, mem2d, s, wl, bsz=bsz, seq=seq)
        recv[l] = _exchange(
            "scatter_grads", big_grads,
            [_cols(IN_COLS // N_DEV), _rows(D_MODEL // N_DEV), _rows(D_MODEL // N_DEV), _cols(D_MODEL // N_DEV),
             _rows(D_MODEL // N_DEV), _cols(D_FF // N_DEV), _rows(D_FF // N_DEV)],
            [w_in.shape[1:], w_mem_k.shape[1:], w_mem_v.shape[1:], w_branch.shape[1:], w_o.shape[1:],
             w_up.shape[1:], w_down.shape[1:]])

    packed = jnp.concatenate(small_rows + [jnp.zeros((SMALL_ROWS * 128 - DEPTH * SMALL_PER_LAYER,), F32)])
    packed = packed.reshape(SMALL_ROWS, 128)
    all_small = _exchange("gather_small_grads", [packed], [_whole], [packed.shape])[0]
    summed = _sum_devices("sum_small_grads", all_small).reshape(-1)[:DEPTH * SMALL_PER_LAYER]
    summed = summed.reshape(DEPTH, SMALL_PER_LAYER)
    small_grads = {}
    off = 0
    for name, n in SMALL:
        small_grads[name] = summed[:, off:off + n]
        off += n
    small_grads["lower_bounds"] = _lb_bwd(lower_bounds, small_grads["lower_bounds"])
    conv_all = small_grads["conv_w"].reshape(DEPTH, CONV_K, WIDTH)
    small_grads["conv_w"] = lax.dynamic_slice_in_dim(conv_all, me * (WIDTH // N_DEV), WIDTH // N_DEV, axis=2)

    grads, deltas, new_m, new_v = {}, {}, {}, {}
    given = dict(lower_bounds=(lower_bounds, m_lower_bounds, v_lower_bounds), conv_w=(conv_w, m_conv_w, v_conv_w),
                 hg_norm_w=(hg_norm_w, m_hg_norm_w, v_hg_norm_w), b_gate=(b_gate, m_b_gate, v_b_gate),
                 ln1_g=(ln1_g, m_ln1_g, v_ln1_g), ln1_b=(ln1_b, m_ln1_b, v_ln1_b),
                 ln2_g=(ln2_g, m_ln2_g, v_ln2_g), ln2_b=(ln2_b, m_ln2_b, v_ln2_b))
    for name, (w_, m_, v_) in given.items():
        g_ = small_grads[name].reshape(w_.shape)
        grads[name] = g_
        deltas[name], new_m[name], new_v[name] = _adam_small("adam_" + name, g_, w_, m_, v_)

    big = dict(w_in=(0, w_in, m_w_in, v_w_in, 128), w_mem_k=(1, w_mem_k, m_w_mem_k, v_w_mem_k, 128),
               w_mem_v=(2, w_mem_v, m_w_mem_v, v_w_mem_v, 128), w_branch=(3, w_branch, m_w_branch, v_w_branch, 512),
               w_o=(4, w_o, m_w_o, v_w_o, 128), w_up=(5, w_up, m_w_up, v_w_up, 256),
               w_down=(6, w_down, m_w_down, v_w_down, 128))
    for name, (k, w_, m_, v_, tr) in big.items():
        shape = w_.shape
        flat = (DEPTH, -1, shape[-1])
        rc = [recv[l][k].reshape((N_DEV,) + w_.reshape(flat).shape[1:]) for l in range(DEPTH)]
        outs = _adam_shard("adam_" + name, rc, w_.reshape(flat), m_.reshape(flat), v_.reshape(flat), tr=tr)
        grads[name], deltas[name], new_m[name], new_v[name] = (o.reshape(shape) for o in outs)

    order = ["lower_bounds", "w_in", "conv_w", "hg_norm_w", "w_mem_k", "w_mem_v", "w_branch", "b_gate", "w_o",
             "ln1_g", "ln1_b", "w_up", "w_down", "ln2_g", "ln2_b"]
    return (loss, dcur.reshape(x.shape), *[grads[n] for n in order], *[deltas[n] for n in order],
            *[new_m[n] for n in order], *[new_v[n] for n in order])
```

```python
import functools

import jax
import jax.numpy as jnp
from jax import lax
from jax.experimental import pallas as pl
from jax.experimental.pallas import tpu as pltpu

F32 = jnp.float32
BF16 = jnp.bfloat16

N_DEV = 8
D_MODEL = 1024
DEPTH = 4
MEM_LEN = 256
CONV_K = 3
WIDTH = 512
HEADS = 4
HEAD_DIM = 128
CHUNK = 32
D_FF = 4 * D_MODEL
IN_COLS = 7168
ALPHA = (2.0 * DEPTH) ** 0.25
LN_EPS = 1e-5
RMS_EPS = 1e-6
ADAM_LR = 0.001
ADAM_B1 = 0.9
ADAM_B2 = 0.999
ADAM_EPS = 1e-08
ADAM_WD = 0.01
ADAM_STEP = 10

C_CB, C_CC, C_CH, C_HQ, C_HF, C_HI, C_HG, C_MQ, C_GA = 0, 512, 1024, 1536, 2048, 2560, 3072, 3584, 4096

ROWS_HG = 256
NT_DIMS = (((1,), (1,)), ((), ()))
TN_DIMS = (((0,), (0,)), ((), ()))
MESH = pl.DeviceIdType.MESH


def _dot(a, b):
    return jnp.dot(a, b, preferred_element_type=F32)


def _dot_nt(a, b):
    return lax.dot_general(a, b, NT_DIMS, preferred_element_type=F32)


def _dot_tn(a, b):
    return lax.dot_general(a, b, TN_DIMS, preferred_element_type=F32)


def _sigmoid(x):
    return 1.0 / (1.0 + jnp.exp(-x))


def _params(*sem):
    return pltpu.CompilerParams(dimension_semantics=sem)


def _resident(shape):
    nd = len(shape)
    return pl.BlockSpec(shape, lambda *_: (0,) * nd)


def _my_id():
    return 4 * lax.axis_index("x") + 2 * lax.axis_index("y") + lax.axis_index("c")


class _Exchange:
    def __init__(self, srcs, slicers, piece_shapes):
        self.srcs, self.slicers, self.n = list(srcs), list(slicers), len(srcs)
        any_spec = pl.BlockSpec(memory_space=pl.ANY)
        self.in_specs = [any_spec] * self.n
        self.out_specs = [any_spec] * self.n
        self.out_shape = [jax.ShapeDtypeStruct((N_DEV,) + tuple(s), a.dtype) for s, a in zip(piece_shapes, srcs)]
        self.scratch = [pltpu.SemaphoreType.DMA((self.n * N_DEV,)), pltpu.SemaphoreType.DMA((self.n * N_DEV,)),
                        pltpu.SemaphoreType.DMA((self.n,))]

    def _remote(self, ins, outs, sems, k, j, me):
        return pltpu.make_async_remote_copy(
            src_ref=self.slicers[k](ins[k], j), dst_ref=outs[k].at[me],
            send_sem=sems[0].at[k * N_DEV + j], recv_sem=sems[1].at[k * N_DEV + me],
            device_id=(j // 4, (j // 2) % 2, j % 2), device_id_type=MESH)

    def _local(self, ins, outs, sems, k, j, me):
        return pltpu.make_async_copy(self.slicers[k](ins[k], j), outs[k].at[me], sems[2].at[k])

    def start(self, ins, outs, sems):
        me = _my_id()
        for k in range(self.n):
            for j in range(N_DEV):
                @pl.when(j != me)
                def _():
                    self._remote(ins, outs, sems, k, j, me).start()

                @pl.when(j == me)
                def _():
                    self._local(ins, outs, sems, k, j, me).start()

    def wait(self, ins, outs, sems):
        me = _my_id()
        for k in range(self.n):
            for j in range(N_DEV):
                @pl.when(j != me)
                def _():
                    pltpu.make_async_remote_copy(
                        src_ref=self.slicers[k](ins[k], j), dst_ref=outs[k].at[j],
                        send_sem=sems[0].at[k * N_DEV + j], recv_sem=sems[1].at[k * N_DEV + j],
                        device_id=(j // 4, (j // 2) % 2, j % 2), device_id_type=MESH).wait_recv()
                    self._remote(ins, outs, sems, k, j, me).wait_send()

                @pl.when(j == me)
                def _():
                    self._local(ins, outs, sems, k, j, me).wait()


def _exchange(name, srcs, slicers, piece_shapes):
    ex = _Exchange(srcs, slicers, piece_shapes)

    def body(*refs):
        ins, outs, sems = refs[:ex.n], refs[ex.n:2 * ex.n], refs[2 * ex.n:]
        ex.start(ins, outs, sems)
        ex.wait(ins, outs, sems)

    return pl.pallas_call(
        body, name=name, in_specs=ex.in_specs, out_specs=ex.out_specs, out_shape=ex.out_shape,
        scratch_shapes=ex.scratch, compiler_params=pltpu.CompilerParams(has_side_effects=True),
    )(*ex.srcs)


def _call(body, name, grid, in_specs, out_specs, out_shape, args, scratch=(), sem=None, carry=None):
    n_in, n_out, n_scr = len(in_specs), len(out_specs), len(scratch)
    if carry is None:
        outs = pl.pallas_call(body, name=name, grid=grid, in_specs=in_specs, out_specs=out_specs,
                              out_shape=out_shape, scratch_shapes=list(scratch),
                              compiler_params=_params(*sem))(*args)
        return outs, None
    nc = carry.n

    def hosted(*refs):
        ins, c_in = refs[:n_in], refs[n_in:n_in + nc]
        outs = refs[n_in + nc:n_in + nc + n_out]
        c_out = refs[n_in + nc + n_out:n_in + 2 * nc + n_out]
        rest = refs[n_in + 2 * nc + n_out:]
        scr, sems = rest[:n_scr], rest[n_scr:]
        first, last = True, True
        for d, size in enumerate(grid):
            first = first & (pl.program_id(d) == 0)
            last = last & (pl.program_id(d) == size - 1)

        @pl.when(first)
        def _():
            carry.start(c_in, c_out, sems)

        body(*ins, *outs, *scr)

        @pl.when(last)
        def _():
            carry.wait(c_in, c_out, sems)

    outs = pl.pallas_call(
        hosted, name=name + "_x", grid=grid, in_specs=list(in_specs) + carry.in_specs,
        out_specs=list(out_specs) + carry.out_specs, out_shape=list(out_shape) + carry.out_shape,
        scratch_shapes=list(scratch) + carry.scratch,
        compiler_params=_params(*(["arbitrary"] * len(grid))))(*args, *carry.srcs)
    return outs[:n_out], outs[n_out:]


def _whole(ref, j):
    return ref


def _cols(width):
    return lambda ref, j: ref.at[(slice(None),) * (len(ref.shape) - 1) + (pl.ds(j * width, width),)]


def _rows(height):
    return lambda ref, j: ref.at[pl.ds(j * height, height)]


def _mm_nn(name, a, w, *, tm, tn, out_dtype, relu2=False, carry=None):
    t, k = a.shape
    n = w.shape[1]

    def body(a_ref, w_ref, o_ref):
        acc = _dot(a_ref[...].astype(BF16), w_ref[...])
        if relu2:
            r = jnp.maximum(acc, 0.0)
            acc = r * r
        o_ref[...] = acc.astype(out_dtype)

    outs, recv = _call(
        body, name, (t // tm, n // tn),
        [pl.BlockSpec((tm, k), lambda i, j: (i, 0)), pl.BlockSpec((k, tn), lambda i, j: (0, j))],
        [pl.BlockSpec((tm, tn), lambda i, j: (i, j))], [jax.ShapeDtypeStruct((t, n), out_dtype)], (a, w),
        sem=("parallel", "parallel"), carry=carry)
    return outs[0] if carry is None else (outs[0], recv)


def _layer_norm(z, g, b):
    mu = jnp.mean(z, axis=-1, keepdims=True)
    zc = z - mu
    var = jnp.mean(zc * zc, axis=-1, keepdims=True)
    return zc * lax.rsqrt(var + LN_EPS) * g + b


def _linear_ln(name, a, w, resid, g, b, *, tm):
    t, k = a.shape

    def body(a_ref, w_ref, r_ref, g_ref, b_ref, z_ref, x_ref):
        z = ALPHA * r_ref[...] + _dot(a_ref[...], w_ref[...])
        z_ref[...] = z
        x_ref[...] = _layer_norm(z, g_ref[...], b_ref[...])

    row = pl.BlockSpec((tm, D_MODEL), lambda i: (i, 0))
    return pl.pallas_call(
        body, name=name, grid=(t // tm,),
        in_specs=[pl.BlockSpec((tm, k), lambda i: (i, 0)), _resident((k, D_MODEL)), row,
                  _resident((1, D_MODEL)), _resident((1, D_MODEL))],
        out_specs=[row, row],
        out_shape=[jax.ShapeDtypeStruct((t, D_MODEL), F32)] * 2,
        compiler_params=_params("parallel"),
    )(a, w, resid, g, b)


def _ln_bwd_mm_nt(name, dy, z, g, w, h=None, *, tm, tn, carry=None):
    t = dy.shape[0]
    n = w.shape[0]

    def body(*refs):
        if h is None:
            dy_ref, z_ref, g_ref, w_ref, dz_ref, dzb_ref, o_ref, dg_ref, db_ref = refs
        else:
            dy_ref, z_ref, g_ref, w_ref, h_ref, dz_ref, dzb_ref, o_ref, dg_ref, db_ref = refs
        i, j = pl.program_id(0), pl.program_id(1)

        @pl.when(j == 0)
        def _():
            zv = z_ref[...]
            mu = jnp.mean(zv, axis=-1, keepdims=True)
            zc = zv - mu
            var = jnp.mean(zc * zc, axis=-1, keepdims=True)
            rstd = lax.rsqrt(var + LN_EPS)
            xh = zc * rstd
            dyv = dy_ref[...]
            gdy = dyv * g_ref[...]
            m1 = jnp.mean(gdy, axis=-1, keepdims=True)
            m2 = jnp.mean(gdy * xh, axis=-1, keepdims=True)
            dz = rstd * (gdy - m1 - xh * m2)
            dz_ref[...] = dz
            dzb_ref[...] = dz.astype(BF16)

            @pl.when(i == 0)
            def _():
                dg_ref[...] = jnp.zeros_like(dg_ref)
                db_ref[...] = jnp.zeros_like(db_ref)

            dg_ref[...] += jnp.sum(dyv * xh, axis=0, keepdims=True)
            db_ref[...] += jnp.sum(dyv, axis=0, keepdims=True)

        acc = _dot_nt(dzb_ref[...], w_ref[...])
        if h is not None:
            acc = acc * (2.0 * jnp.sqrt(h_ref[...].astype(F32)))
        o_ref[...] = acc.astype(BF16)

    row = pl.BlockSpec((tm, D_MODEL), lambda i, j: (i, 0))
    vec = pl.BlockSpec((1, D_MODEL), lambda i, j: (0, 0))
    tile = pl.BlockSpec((tm, tn), lambda i, j: (i, j))
    in_specs = [row, row, vec, pl.BlockSpec((tn, D_MODEL), lambda i, j: (j, 0))]
    args = [dy, z, g, w]
    if h is not None:
        in_specs.append(tile)
        args.append(h)
    outs, recv = _call(
        body, name, (t // tm, n // tn), in_specs, [row, row, tile, vec, vec],
        [jax.ShapeDtypeStruct((t, D_MODEL), F32), jax.ShapeDtypeStruct((t, D_MODEL), BF16),
         jax.ShapeDtypeStruct((t, n), BF16), jax.ShapeDtypeStruct((1, D_MODEL), F32),
         jax.ShapeDtypeStruct((1, D_MODEL), F32)], args, sem=("arbitrary", "arbitrary"), carry=carry)
    return outs if carry is None else (outs, recv)


def _mm_tn(name, a, b, *, tk, tmo, tno):
    t, m = a.shape
    n = b.shape[1]
    nk = t // tk

    def body(a_ref, b_ref, o_ref, acc_ref):
        k = pl.program_id(2)
        p = _dot_tn(a_ref[...].astype(BF16), b_ref[...].astype(BF16))

        @pl.when(k == 0)
        def _():
            acc_ref[...] = p

        @pl.when(k > 0)
        def _():
            acc_ref[...] += p

        @pl.when(k == nk - 1)
        def _():
            o_ref[...] = acc_ref[...].astype(BF16)

    return pl.pallas_call(
        body, name=name, grid=(m // tmo, n // tno, nk),
        in_specs=[pl.BlockSpec((tk, tmo), lambda i, j, k: (k, i)), pl.BlockSpec((tk, tno), lambda i, j, k: (k, j))],
        out_specs=pl.BlockSpec((tmo, tno), lambda i, j, k: (i, j)),
        out_shape=jax.ShapeDtypeStruct((m, n), BF16),
        scratch_shapes=[pltpu.VMEM((tmo, tno), F32)],
        compiler_params=_params("parallel", "parallel", "arbitrary"),
    )(a, b)


def _mm_nt_sum(name, pieces, offsets, w, resid, *, tm):
    t = resid.shape[0]
    widths = [p.shape[1] for p in pieces]
    n_p = len(pieces)

    def body(*refs):
        p_refs, w_ref, r_ref, o_ref = refs[:n_p], refs[n_p], refs[n_p + 1], refs[n_p + 2]
        acc = ALPHA * r_ref[...]
        for p_ref, off, wd in zip(p_refs, offsets, widths):
            acc = acc + _dot_nt(p_ref[...], w_ref[:, off:off + wd])
        o_ref[...] = acc

    row = pl.BlockSpec((tm, D_MODEL), lambda i: (i, 0))
    return pl.pallas_call(
        body, name=name, grid=(t // tm,),
        in_specs=[pl.BlockSpec((tm, wd), lambda i: (i, 0)) for wd in widths] + [_resident(w.shape), row],
        out_specs=row,
        out_shape=jax.ShapeDtypeStruct((t, D_MODEL), F32),
        compiler_params=_params("parallel"),
    )(*pieces, w, resid)


def _chunk_masks(rows):
    r = lax.broadcasted_iota(jnp.int32, (rows, rows), 0)
    c = lax.broadcasted_iota(jnp.int32, (rows, rows), 1)
    same = (r // CHUNK) == (c // CHUNK)
    causal = same & (c <= r)
    upto_mid = same & ((c % CHUNK) < CHUNK // 2)
    return causal, upto_mid, same


def _split3(x):
    hi = x.astype(BF16)
    r1 = x - hi.astype(F32)
    mid = r1.astype(BF16)
    lo = (r1 - mid.astype(F32)).astype(BF16)
    return hi, mid, lo


def _mask_sum(mask_b, parts, transpose=False):
    f = _dot_tn if transpose else _dot
    return f(mask_b, parts[0]) + f(mask_b, parts[1]) + f(mask_b, parts[2])


def _hg_gates(q_raw, fl, lb, rows):
    causal, upto_mid, same = _chunk_masks(rows)
    tri = causal.astype(BF16)
    same_b = same.astype(BF16)
    sg = _sigmoid(fl)
    forget = lb + (1.0 - lb) * sg
    k = (1.0 - lb) * _sigmoid(-fl)
    sq = _sigmoid(q_raw)
    qs = q_raw * sq
    parts = _split3(jnp.log(forget))
    bc = _mask_sum(tri, parts)
    bref = _mask_sum(upto_mid.astype(BF16), parts)
    blast = _mask_sum(same_b, parts)
    return dict(mask=causal, tri=tri, same=same_b, sg=sg, forget=forget, k=k, sq=sq, qs=qs,
                e_a=jnp.exp(bc - bref), e_b=jnp.exp(bref - bc), e_q=jnp.exp(bc), e_k=jnp.exp(blast - bc),
                dec=jnp.exp(blast))


def _hgrn_fwd(proj, lb, nw, *, bsz, seq, carry=None):
    rows = min(ROWS_HG, seq)
    nt = seq // rows
    nc = rows // CHUNK
    t = bsz * seq

    def body(q_ref, f_ref, v_ref, g_ref, lb_ref, nw_ref, y_ref, o_ref, st_ref, s_scr):
        j = pl.program_id(2)

        @pl.when(j == 0)
        def _():
            s_scr[...] = jnp.zeros_like(s_scr)

        gt = _hg_gates(q_ref[...], f_ref[...], lb_ref[...], rows)
        v_b = v_ref[...].astype(BF16)
        a_b = (gt["qs"] * gt["e_a"]).astype(BF16)
        b_b = (gt["k"] * gt["e_b"]).astype(BF16)
        qi_b = (gt["qs"] * gt["e_q"]).astype(BF16)
        ko_b = (gt["k"] * gt["e_k"]).astype(BF16)
        scores = jnp.where(gt["mask"], _dot_nt(a_b, b_b), 0.0)
        o_intra = _dot(scores.astype(BF16), v_b)

        s = s_scr[...]
        parts = []
        for n in range(nc):
            sl = slice(n * CHUNK, (n + 1) * CHUNK)
            s_b = s.astype(BF16)
            st_ref[n] = s_b
            parts.append(_dot_nt(qi_b[sl], s_b))
            s = s * gt["dec"][n * CHUNK:n * CHUNK + 1] + _dot_tn(v_b[sl], ko_b[sl])
        s_scr[...] = s
        o = o_intra + jnp.concatenate(parts, axis=0)
        o_ref[...] = o
        r = lax.rsqrt(jnp.mean(o * o, axis=-1, keepdims=True) + RMS_EPS)
        g = g_ref[...]
        y_ref[...] = (o * r * nw_ref[...] * (g * _sigmoid(g))).astype(BF16)

    def col(base):
        return pl.BlockSpec((rows, HEAD_DIM), lambda h, b, j: (b * nt + j, base // HEAD_DIM + h))

    out_tile = pl.BlockSpec((rows, HEAD_DIM), lambda h, b, j: (b * nt + j, h))
    outs, recv = _call(
        body, "hgrn_fwd", (HEADS, bsz, nt),
        [col(C_HQ), col(C_HF), col(C_HI), col(C_HG),
         pl.BlockSpec((1, HEAD_DIM), lambda h, b, j: (0, h)), _resident((1, HEAD_DIM))],
        [out_tile, out_tile, pl.BlockSpec((None, None, nc, HEAD_DIM, HEAD_DIM), lambda h, b, j: (h, b, j, 0, 0))],
        [jax.ShapeDtypeStruct((t, WIDTH), BF16), jax.ShapeDtypeStruct((t, WIDTH), F32),
         jax.ShapeDtypeStruct((HEADS, bsz, seq // CHUNK, HEAD_DIM, HEAD_DIM), BF16)],
        (proj, proj, proj, proj, lb, nw), scratch=[pltpu.VMEM((HEAD_DIM, HEAD_DIM), F32)],
        sem=("parallel", "parallel", "arbitrary"), carry=carry)
    return outs if carry is None else (outs, recv)


def _hgrn_bwd(proj, lb, nw, o_pre, states, dy, *, bsz, seq, carry=None):
    rows = min(ROWS_HG, seq)
    nt = seq // rows
    nc = rows // CHUNK
    t = bsz * seq

    def body(q_ref, f_ref, v_ref, g_ref, lb_ref, nw_ref, o_ref, st_ref, dy_ref,
             dq_ref, df_ref, dv_ref, dg_ref, dlb_ref, dnw_ref, ds_scr):
        h, b, j = pl.program_id(0), pl.program_id(1), pl.program_id(2)

        @pl.when(j == 0)
        def _():
            ds_scr[...] = jnp.zeros_like(ds_scr)

        @pl.when((b == 0) & (j == 0))
        def _():
            dlb_ref[...] = jnp.zeros_like(dlb_ref)

        @pl.when((h == 0) & (b == 0) & (j == 0))
        def _():
            dnw_ref[...] = jnp.zeros_like(dnw_ref)

        q_raw, fl, lb_v = q_ref[...], f_ref[...], lb_ref[...]
        gt = _hg_gates(q_raw, fl, lb_v, rows)
        mask = gt["mask"]
        v = v_ref[...]
        v_b = v.astype(BF16)
        a_f = gt["qs"] * gt["e_a"]
        b_f = gt["k"] * gt["e_b"]
        qi_f = gt["qs"] * gt["e_q"]
        ko_f = gt["k"] * gt["e_k"]
        a_b, b_b, qi_b, ko_b = a_f.astype(BF16), b_f.astype(BF16), qi_f.astype(BF16), ko_f.astype(BF16)

        o = o_ref[...]
        nw_v = nw_ref[...]
        g = g_ref[...]
        dyv = dy_ref[...]
        r = lax.rsqrt(jnp.mean(o * o, axis=-1, keepdims=True) + RMS_EPS)
        sgg = _sigmoid(g)
        d_g = dyv * (o * r * nw_v) * (sgg * (1.0 + g * (1.0 - sgg)))
        d_on = dyv * (g * sgg)
        dnw_ref[...] += jnp.sum(d_on * o * r, axis=0, keepdims=True)
        tt = d_on * nw_v
        d_o = r * tt - o * (r * r * r) * jnp.mean(tt * o, axis=-1, keepdims=True)
        do_b = d_o.astype(BF16)

        sc_b = jnp.where(mask, _dot_nt(a_b, b_b), 0.0).astype(BF16)
        dsc_b = jnp.where(mask, _dot_nt(do_b, v_b), 0.0).astype(BF16)
        d_v = _dot_tn(sc_b, do_b)
        d_a = _dot(dsc_b, b_b)
        d_bm = _dot_tn(dsc_b, a_b)

        ds = ds_scr[...]
        dqi_parts, dko_parts, dvi_parts, ddec_parts = [None] * nc, [None] * nc, [None] * nc, [None] * nc
        for n in reversed(range(nc)):
            sl = slice(n * CHUNK, (n + 1) * CHUNK)
            dec_n = gt["dec"][n * CHUNK:n * CHUNK + 1]
            ds_b = ds.astype(BF16)
            s_n = st_ref[n]
            dqi_parts[n] = _dot(do_b[sl], s_n)
            dko_parts[n] = _dot(v_b[sl], ds_b)
            dvi_parts[n] = _dot_nt(ko_b[sl], ds_b)
            d_dec = jnp.sum(ds * s_n.astype(F32), axis=0, keepdims=True)
            ddec_parts[n] = jnp.broadcast_to(d_dec * dec_n, (CHUNK, HEAD_DIM))
            ds = ds * dec_n + _dot_tn(do_b[sl], qi_b[sl])
        ds_scr[...] = ds
        d_qi = jnp.concatenate(dqi_parts, axis=0)
        d_ko = jnp.concatenate(dko_parts, axis=0)
        d_v = d_v + jnp.concatenate(dvi_parts, axis=0)

        d_qs = d_a * gt["e_a"] + d_qi * gt["e_q"]
        d_k = d_bm * gt["e_b"] + d_ko * gt["e_k"]
        t_a, t_b, t_q, t_k = d_a * a_f, d_bm * b_f, d_qi * qi_f, d_ko * ko_f
        d_bref = _mask_sum(gt["same"], _split3(t_b - t_a))
        d_blast = _mask_sum(gt["same"], _split3(t_k)) + jnp.concatenate(ddec_parts, axis=0)
        pos = lax.broadcasted_iota(jnp.int32, (rows, HEAD_DIM), 0) % CHUNK
        d_bc = (t_a - t_b + t_q - t_k + jnp.where(pos == CHUNK // 2 - 1, d_bref, 0.0)
                + jnp.where(pos == CHUNK - 1, d_blast, 0.0))
        d_logf = _mask_sum(gt["tri"], _split3(d_bc), transpose=True)

        sg, forget = gt["sg"], gt["forget"]
        one_m_sg = 1.0 - sg
        common = (1.0 - lb_v) * sg * one_m_sg
        d_fl = d_logf * common / forget - d_k * common
        dlb_ref[...] += jnp.sum(d_logf * one_m_sg / forget - d_k * one_m_sg, axis=0, keepdims=True)
        sq = gt["sq"]
        dq_ref[...] = (d_qs * (sq * (1.0 + q_raw * (1.0 - sq)))).astype(BF16)
        df_ref[...] = d_fl.astype(BF16)
        dv_ref[...] = d_v.astype(BF16)
        dg_ref[...] = d_g.astype(BF16)

    def rowblk(b, j):
        return b * nt + (nt - 1 - j)

    def col(base):
        return pl.BlockSpec((rows, HEAD_DIM), lambda h, b, j: (rowblk(b, j), base // HEAD_DIM + h))

    tile = pl.BlockSpec((rows, HEAD_DIM), lambda h, b, j: (rowblk(b, j), h))
    head_vec = pl.BlockSpec((1, HEAD_DIM), lambda h, b, j: (0, h))
    d_out = jax.ShapeDtypeStruct((t, WIDTH), BF16)
    outs, recv = _call(
        body, "hgrn_bwd", (HEADS, bsz, nt),
        [col(C_HQ), col(C_HF), col(C_HI), col(C_HG), head_vec, _resident((1, HEAD_DIM)), tile,
         pl.BlockSpec((None, None, nc, HEAD_DIM, HEAD_DIM), lambda h, b, j: (h, b, nt - 1 - j, 0, 0)), tile],
        [tile, tile, tile, tile, head_vec, _resident((1, HEAD_DIM))],
        [d_out, d_out, d_out, d_out, jax.ShapeDtypeStruct((1, WIDTH), F32),
         jax.ShapeDtypeStruct((1, HEAD_DIM), F32)],
        (proj, proj, proj, proj, lb, nw, o_pre, states, dy), scratch=[pltpu.VMEM((HEAD_DIM, HEAD_DIM), F32)],
        sem=("arbitrary", "arbitrary", "arbitrary"), carry=carry)
    return outs if carry is None else (outs, recv)


def _mem_kv(mem2d, w_k, w_v):
    rows = mem2d.shape[0]

    def body(m_ref, wk_ref, wv_ref, k_ref, v_ref):
        m_b = m_ref[...].astype(BF16)
        k_ref[...] = _dot(m_b, wk_ref[...]).astype(BF16)
        v_ref[...] = _dot(m_b, wv_ref[...]).astype(BF16)

    return pl.pallas_call(
        body, name="mem_kv", grid=(rows // MEM_LEN,),
        in_specs=[pl.BlockSpec((MEM_LEN, D_MODEL), lambda i: (i, 0)), _resident((D_MODEL, WIDTH)),
                  _resident((D_MODEL, WIDTH))],
        out_specs=[pl.BlockSpec((MEM_LEN, WIDTH), lambda i: (i, 0))] * 2,
        out_shape=[jax.ShapeDtypeStruct((rows, WIDTH), BF16)] * 2,
        compiler_params=_params("parallel"),
    )(mem2d, w_k, w_v)


def _softmax_rows(s):
    m = jnp.max(s, axis=-1, keepdims=True)
    e = jnp.exp(s - m)
    return e / jnp.sum(e, axis=-1, keepdims=True)


def _attn_fwd(proj, mk, mv, *, tm, seq):
    t = proj.shape[0]
    per_b = seq // tm
    scale = HEAD_DIM ** -0.5

    def body(q_ref, k_ref, v_ref, y_ref):
        outs = []
        for h in range(HEADS):
            sl = slice(h * HEAD_DIM, (h + 1) * HEAD_DIM)
            s = _dot_nt(q_ref[:, sl].astype(BF16), k_ref[:, sl]) * scale
            p = _softmax_rows(s)
            outs.append(_dot(p.astype(BF16), v_ref[:, sl]))
        y_ref[...] = jnp.concatenate(outs, axis=-1).astype(BF16)

    kv = pl.BlockSpec((MEM_LEN, WIDTH), lambda i: (i // per_b, 0))
    return pl.pallas_call(
        body, name="attn_fwd", grid=(t // tm,),
        in_specs=[pl.BlockSpec((tm, WIDTH), lambda i: (i, C_MQ // WIDTH)), kv, kv],
        out_specs=pl.BlockSpec((tm, WIDTH), lambda i: (i, 0)),
        out_shape=jax.ShapeDtypeStruct((t, WIDTH), BF16),
        compiler_params=_params("parallel"),
    )(proj, mk, mv)


def _attn_bwd(proj, mk, mv, dy, *, tm, seq):
    t = proj.shape[0]
    per_b = seq // tm
    scale = HEAD_DIM ** -0.5

    def body(q_ref, k_ref, v_ref, dy_ref, dq_ref, dk_ref, dv_ref):
        i = pl.program_id(0)

        @pl.when(i % per_b == 0)
        def _():
            dk_ref[...] = jnp.zeros_like(dk_ref)
            dv_ref[...] = jnp.zeros_like(dv_ref)

        dqs, dks, dvs = [], [], []
        for h in range(HEADS):
            sl = slice(h * HEAD_DIM, (h + 1) * HEAD_DIM)
            q_b = q_ref[:, sl].astype(BF16)
            k_b, v_b = k_ref[:, sl], v_ref[:, sl]
            p = _softmax_rows(_dot_nt(q_b, k_b) * scale)
            dy_b = dy_ref[:, sl].astype(BF16)
            dp = _dot_nt(dy_b, v_b)
            dvs.append(_dot_tn(p.astype(BF16), dy_b))
            ds_b = (p * (dp - jnp.sum(dp * p, axis=-1, keepdims=True)) * scale).astype(BF16)
            dqs.append(_dot(ds_b, k_b))
            dks.append(_dot_tn(ds_b, q_b))
        dq_ref[...] = jnp.concatenate(dqs, axis=-1).astype(BF16)
        dk_ref[...] += jnp.concatenate(dks, axis=-1)
        dv_ref[...] += jnp.concatenate(dvs, axis=-1)

    kv = pl.BlockSpec((MEM_LEN, WIDTH), lambda i: (i // per_b, 0))
    tile = pl.BlockSpec((tm, WIDTH), lambda i: (i, 0))
    n_mem = mk.shape[0]
    return pl.pallas_call(
        body, name="attn_bwd", grid=(t // tm,),
        in_specs=[pl.BlockSpec((tm, WIDTH), lambda i: (i, C_MQ // WIDTH)), kv, kv, tile],
        out_specs=[tile, kv, kv],
        out_shape=[jax.ShapeDtypeStruct((t, WIDTH), BF16), jax.ShapeDtypeStruct((n_mem, WIDTH), F32),
                   jax.ShapeDtypeStruct((n_mem, WIDTH), F32)],
        compiler_params=_params("arbitrary"),
    )(proj, mk, mv, dy)


HALO = 8


def _shift_down(u, halo, k, row):
    out = pltpu.roll(u, k, 0)
    for m in range(k):
        out = jnp.where(row == m, halo[HALO - k + m:HALO - k + m + 1, :], out)
    return out


def _shift_up(u, halo, k, row, tm):
    out = pltpu.roll(u, tm - k, 0)
    for m in range(k):
        out = jnp.where(row == tm - k + m, halo[m:m + 1, :], out)
    return out


def _merge_fwd(proj, y_b, y_c, conv_w, w_branch, b_gate, *, tm, seq):
    t = proj.shape[0]
    per_b = seq // tm
    hb = tm // HALO

    def body(cb_ref, cc_ref, ch_ref, cch_ref, chh_ref, ga_ref, gb_ref, gc_ref, yb_ref, yc_ref, cw_ref, wb_ref,
             bg_ref, ya_ref, pa_ref, pb_ref, pc_ref, mg_ref):
        i = pl.program_id(0)
        row = lax.broadcasted_iota(jnp.int32, (tm, WIDTH), 0)
        u = cc_ref[...] * ch_ref[...]
        halo = jnp.where(i % per_b == 0, 0.0, cch_ref[...] * chh_ref[...])
        cw = cw_ref[...]
        y = cw[0:1] * _shift_down(u, halo, 2, row) + cw[1:2] * _shift_down(u, halo, 1, row) + cw[2:3] * u
        ya_b = (cb_ref[...] * y).astype(BF16)
        ya_ref[...] = ya_b
        merged = None
        for idx, (y_in, g_ref, p_ref) in enumerate(((ya_b, ga_ref, pa_ref), (yb_ref[...], gb_ref, pb_ref),
                                                     (yc_ref[...], gc_ref, pc_ref))):
            p = _dot(y_in, wb_ref[idx])
            p_ref[...] = p.astype(BF16)
            term = _sigmoid(g_ref[...] + bg_ref[:, idx * D_MODEL:(idx + 1) * D_MODEL]) * p
            merged = term if merged is None else merged + term
        mg_ref[...] = merged.astype(BF16)

    def half(c):
        return pl.BlockSpec((tm, WIDTH), lambda i: (i, c // WIDTH))

    def prev(c):
        return pl.BlockSpec((HALO, WIDTH), lambda i: (jnp.maximum(i * hb - 1, 0), c // WIDTH))

    def gate(k):
        return pl.BlockSpec((tm, D_MODEL), lambda i: (i, C_GA // D_MODEL + k))

    tile512 = pl.BlockSpec((tm, WIDTH), lambda i: (i, 0))
    tile1k = pl.BlockSpec((tm, D_MODEL), lambda i: (i, 0))
    return pl.pallas_call(
        body, name="merge_fwd", grid=(t // tm,),
        in_specs=[half(C_CB), half(C_CC), half(C_CH), prev(C_CC), prev(C_CH), gate(0), gate(1), gate(2),
                  tile512, tile512, _resident((CONV_K, WIDTH)), _resident((3, WIDTH, D_MODEL)),
                  _resident((1, 3 * D_MODEL))],
        out_specs=[tile512, tile1k, tile1k, tile1k, tile1k],
        out_shape=[jax.ShapeDtypeStruct((t, WIDTH), BF16)] + [jax.ShapeDtypeStruct((t, D_MODEL), BF16)] * 4,
        compiler_params=_params("parallel"),
    )(proj, proj, proj, proj, proj, proj, proj, proj, y_b, y_c, conv_w, w_branch, b_gate)


def _merge_bwd(proj, dmerged, pa, pb, pc, w_branch, b_gate, *, tm):
    t = proj.shape[0]

    def body(dm_ref, pa_ref, pb_ref, pc_ref, ga_ref, gb_ref, gc_ref, wb_ref, bg_ref,
             dgt_ref, dpa_ref, dpb_ref, dpc_ref, dya_ref, dyb_ref, dyc_ref, dbg_ref):
        i = pl.program_id(0)

        @pl.when(i == 0)
        def _():
            dbg_ref[...] = jnp.zeros_like(dbg_ref)

        dm = dm_ref[...].astype(F32)
        for idx, (p_ref, g_ref, dp_ref, dy_ref) in enumerate(((pa_ref, ga_ref, dpa_ref, dya_ref),
                                                              (pb_ref, gb_ref, dpb_ref, dyb_ref),
                                                              (pc_ref, gc_ref, dpc_ref, dyc_ref))):
            cols = slice(idx * D_MODEL, (idx + 1) * D_MODEL)
            sg = _sigmoid(g_ref[...] + bg_ref[:, cols])
            dp_b = (dm * sg).astype(BF16)
            dp_ref[...] = dp_b
            dgate = dm * p_ref[...].astype(F32) * sg * (1.0 - sg)
            dgt_ref[:, cols] = dgate.astype(BF16)
            dbg_ref[:, cols] += jnp.sum(dgate, axis=0, keepdims=True)
            dy_ref[...] = _dot_nt(dp_b, wb_ref[idx])

    def gate(k):
        return pl.BlockSpec((tm, D_MODEL), lambda i: (i, C_GA // D_MODEL + k))

    tile512 = pl.BlockSpec((tm, WIDTH), lambda i: (i, 0))
    tile1k = pl.BlockSpec((tm, D_MODEL), lambda i: (i, 0))
    return pl.pallas_call(
        body, name="merge_bwd", grid=(t // tm,),
        in_specs=[tile1k, tile1k, tile1k, tile1k, gate(0), gate(1), gate(2), _resident((3, WIDTH, D_MODEL)),
                  _resident((1, 3 * D_MODEL))],
        out_specs=[pl.BlockSpec((tm, 3 * D_MODEL), lambda i: (i, 0)), tile1k, tile1k, tile1k,
                   tile512, tile512, tile512, _resident((1, 3 * D_MODEL))],
        out_shape=[jax.ShapeDtypeStruct((t, 3 * D_MODEL), BF16)] + [jax.ShapeDtypeStruct((t, D_MODEL), BF16)] * 3
                  + [jax.ShapeDtypeStruct((t, WIDTH), F32)] * 3 + [jax.ShapeDtypeStruct((1, 3 * D_MODEL), F32)],
        compiler_params=_params("arbitrary"),
    )(dmerged, pa, pb, pc, proj, proj, proj, w_branch, b_gate)


def _conv_bwd(proj, dya, conv_w, *, tm, seq):
    t = proj.shape[0]
    per_b = seq // tm
    hb = tm // HALO
    last_blk = t // HALO - 1

    def body(cb_ref, cc_ref, ch_ref, cch_ref, chh_ref, dya_ref, cbn_ref, dyan_ref, cw_ref, d_ref, dcw_ref):
        i = pl.program_id(0)

        @pl.when(i == 0)
        def _():
            dcw_ref[...] = jnp.zeros_like(dcw_ref)

        row = lax.broadcasted_iota(jnp.int32, (tm, WIDTH), 0)
        cb, cc, ch = cb_ref[...], cc_ref[...], ch_ref[...]
        u = cc * ch
        halo = jnp.where(i % per_b == 0, 0.0, cch_ref[...] * chh_ref[...])
        u1 = _shift_down(u, halo, 1, row)
        u2 = _shift_down(u, halo, 2, row)
        cw = cw_ref[...]
        y = cw[0:1] * u2 + cw[1:2] * u1 + cw[2:3] * u
        dya = dya_ref[...]
        dy = dya * cb
        nxt = jnp.where(i % per_b == per_b - 1, 0.0, dyan_ref[...] * cbn_ref[...])
        du = cw[2:3] * dy + cw[1:2] * _shift_up(dy, nxt, 1, row, tm) + cw[0:1] * _shift_up(dy, nxt, 2, row, tm)
        d_ref[:, 0:WIDTH] = (dya * y).astype(BF16)
        d_ref[:, WIDTH:2 * WIDTH] = (du * ch).astype(BF16)
        d_ref[:, 2 * WIDTH:3 * WIDTH] = (du * cc).astype(BF16)
        dcw_ref[0:1, :] += jnp.sum(dy * u2, axis=0, keepdims=True)
        dcw_ref[1:2, :] += jnp.sum(dy * u1, axis=0, keepdims=True)
        dcw_ref[2:3, :] += jnp.sum(dy * u, axis=0, keepdims=True)

    def half(c):
        return pl.BlockSpec((tm, WIDTH), lambda i: (i, c // WIDTH))

    def prev(c):
        return pl.BlockSpec((HALO, WIDTH), lambda i: (jnp.maximum(i * hb - 1, 0), c // WIDTH))

    def nxt(c):
        return pl.BlockSpec((HALO, WIDTH), lambda i: (jnp.minimum((i + 1) * hb, last_blk), c // WIDTH))

    return pl.pallas_call(
        body, name="conv_bwd", grid=(t // tm,),
        in_specs=[half(C_CB), half(C_CC), half(C_CH), prev(C_CC), prev(C_CH),
                  pl.BlockSpec((tm, WIDTH), lambda i: (i, 0)), nxt(C_CB), nxt(0), _resident((CONV_K, WIDTH))],
        out_specs=[pl.BlockSpec((tm, 3 * WIDTH), lambda i: (i, 0)), _resident((CONV_K, WIDTH))],
        out_shape=[jax.ShapeDtypeStruct((t, 3 * WIDTH), BF16), jax.ShapeDtypeStruct((CONV_K, WIDTH), F32)],
        compiler_params=_params("arbitrary"),
    )(proj, proj, proj, proj, proj, dya, proj, dya, conv_w)


def _loss_head(y, target, *, tm):
    t = y.shape[0]

    def body(y_ref, t_ref, dy_ref, l_ref):
        @pl.when(pl.program_id(0) == 0)
        def _():
            l_ref[...] = jnp.zeros_like(l_ref)

        err = y_ref[...] - t_ref[...]
        dy_ref[...] = err * (1.0 / D_MODEL)
        per_row = jnp.sum(err * err, axis=-1, keepdims=True) * (1.0 / D_MODEL)
        l_ref[...] += 0.5 * jnp.sum(per_row, axis=0, keepdims=True)

    row = pl.BlockSpec((tm, D_MODEL), lambda i: (i, 0))
    return pl.pallas_call(
        body, name="loss_head", grid=(t // tm,),
        in_specs=[row, row], out_specs=[row, _resident((8, 128))],
        out_shape=[jax.ShapeDtypeStruct((t, D_MODEL), F32), jax.ShapeDtypeStruct((8, 128), F32)],
        compiler_params=_params("arbitrary"),
    )(y, target)


def _lb_softmax(lower_bounds):
    x = lower_bounds
    e = jnp.exp(x - jnp.max(x, axis=0, keepdims=True))
    return e / jnp.sum(e, axis=0, keepdims=True)


def _lb_fwd(lower_bounds):
    def body(x_ref, o_ref):
        s = _lb_softmax(x_ref[...])
        c = s[0:1]
        o_ref[0:1, :] = c - s[0:1]
        for l in range(1, DEPTH):
            c = c + s[l:l + 1]
            o_ref[l:l + 1, :] = c - s[0:1]

    return pl.pallas_call(body, name="lb_fwd", out_shape=jax.ShapeDtypeStruct(lower_bounds.shape, F32))(lower_bounds)


def _lb_bwd(lower_bounds, d_lb_all):
    def body(x_ref, d_ref, o_ref):
        s = _lb_softmax(x_ref[...])
        d = d_ref[...]
        rows = [jnp.zeros_like(d[0:1])]
        for j in range(1, DEPTH):
            acc = d[j:j + 1]
            for l in range(j + 1, DEPTH):
                acc = acc + d[l:l + 1]
            rows.append(acc)
        inner = rows[0] * s[0:1]
        for j in range(1, DEPTH):
            inner = inner + rows[j] * s[j:j + 1]
        for j in range(DEPTH):
            o_ref[j:j + 1, :] = s[j:j + 1] * (rows[j] - inner)

    return pl.pallas_call(body, name="lb_bwd", out_shape=jax.ShapeDtypeStruct(lower_bounds.shape, F32))(
        lower_bounds, d_lb_all)


def _adamw(w, g, m, v):
    m2 = ADAM_B1 * m + (1.0 - ADAM_B1) * g
    v2 = ADAM_B2 * v + (1.0 - ADAM_B2) * (g * g)
    m_hat = m2 / (1.0 - ADAM_B1 ** ADAM_STEP)
    v_hat = v2 / (1.0 - ADAM_B2 ** ADAM_STEP)
    delta = -ADAM_LR * (m_hat / (jnp.sqrt(v_hat) + ADAM_EPS) + ADAM_WD * w)
    return delta, m2, v2


def _adam_small(name, g, w, m, v):
    shape = w.shape
    flat = (-1, shape[-1])
    g2, w2, m2, v2 = (a.reshape(flat) for a in (g, w, m, v))

    def body(g_ref, w_ref, m_ref, v_ref, d_ref, mo_ref, vo_ref):
        d, mm, vv = _adamw(w_ref[...], g_ref[...], m_ref[...], v_ref[...])
        d_ref[...] = d
        mo_ref[...] = mm
        vo_ref[...] = vv

    outs = pl.pallas_call(body, name=name, out_shape=[jax.ShapeDtypeStruct(w2.shape, F32)] * 3)(g2, w2, m2, v2)
    return [o.reshape(shape) for o in outs]


def _adam_shard(name, recvs, w, m, v, *, tr):
    _, r, c = w.shape

    def body(*refs):
        rc, (w_ref, m_ref, v_ref), (g_ref, d_ref, mo_ref, vo_ref) = refs[:DEPTH], refs[DEPTH:DEPTH + 3], refs[DEPTH + 3:]
        layer = pl.program_id(0)
        for cand in range(DEPTH):
            @pl.when(layer == cand)
            def _():
                g = rc[cand][0].astype(F32)
                for d in range(1, N_DEV):
                    g = g + rc[cand][d].astype(F32)
                dl, mm, vv = _adamw(w_ref[...], g, m_ref[...], v_ref[...])
                g_ref[...] = g
                d_ref[...] = dl
                mo_ref[...] = mm
                vo_ref[...] = vv

    def recv_spec(cand):
        return pl.BlockSpec((N_DEV, tr, c), lambda l, i: (0, jnp.where(l == cand, i, 0), 0))

    tile = pl.BlockSpec((None, tr, c), lambda l, i: (l, i, 0))
    return pl.pallas_call(
        body, name=name, grid=(DEPTH, r // tr),
        in_specs=[recv_spec(cand) for cand in range(DEPTH)] + [tile] * 3,
        out_specs=[tile] * 4,
        out_shape=[jax.ShapeDtypeStruct(w.shape, F32)] * 4,
        compiler_params=_params("parallel", "parallel"),
    )(*recvs, w, m, v)


def _sum_devices(name, x):
    def body(x_ref, o_ref):
        acc = x_ref[0]
        for d in range(1, N_DEV):
            acc = acc + x_ref[d]
        o_ref[...] = acc

    return pl.pallas_call(body, name=name, out_shape=jax.ShapeDtypeStruct(x.shape[1:], x.dtype))(x)


SMALL = (("lower_bounds", 512), ("conv_w", CONV_K * WIDTH), ("hg_norm_w", HEAD_DIM), ("b_gate", 3 * D_MODEL),
         ("ln1_g", D_MODEL), ("ln1_b", D_MODEL), ("ln2_g", D_MODEL), ("ln2_b", D_MODEL))
SMALL_PER_LAYER = sum(n for _, n in SMALL)
SMALL_ROWS = 296


def _natural_cols(g):
    nd = g.ndim
    perm = tuple(range(1, nd - 1)) + (0, nd - 1)
    t = jnp.transpose(g, perm)
    return t.reshape(t.shape[:-2] + (t.shape[-2] * t.shape[-1],))


def _natural_rows(g):
    return g.reshape(g.shape[0] * g.shape[1], g.shape[2])


def _hosted(hosts, key, fn):
    if not hosts or key not in hosts:
        return fn(None)
    ex, hook = hosts[key]
    outs, recv = fn(ex)
    hook(recv)
    return outs


def _layer_fwd(cur, mem2d, wl, *, bsz, seq, hosts=None):
    tm = min(512, seq)
    proj = _hosted(hosts, "in_proj", lambda c: _mm_nn("in_proj", cur, wl["w_in"], tm=min(1024, seq), tn=1024,
                                                       out_dtype=F32, carry=c))
    y_b, o_pre, states = _hosted(hosts, "hgrn_fwd", lambda c: _hgrn_fwd(proj, wl["lb"], wl["nw"], bsz=bsz, seq=seq,
                                                                         carry=c))
    mk, mv = _mem_kv(mem2d, wl["w_mk"], wl["w_mv"])
    y_c = _attn_fwd(proj, mk, mv, tm=tm, seq=seq)
    y_a, pa, pb, pc, merged = _merge_fwd(proj, y_b, y_c, wl["conv"], wl["w_br"], wl["b_gate"], tm=tm, seq=seq)
    z1, x1 = _linear_ln("wo_ln", merged, wl["w_o"], cur, wl["ln1_g"], wl["ln1_b"], tm=tm)
    hid = _hosted(hosts, "mlp_up", lambda c: _mm_nn("mlp_up", x1, wl["w_up"], tm=tm, tn=1024, out_dtype=BF16,
                                                     relu2=True, carry=c))
    z2, x2 = _linear_ln("down_ln", hid, wl["w_down"], x1, wl["ln2_g"], wl["ln2_b"], tm=tm)
    return dict(x=cur, proj=proj, y_a=y_a, y_b=y_b, y_c=y_c, o_pre=o_pre, states=states, mk=mk, mv=mv,
                pa=pa, pb=pb, pc=pc, merged=merged, z1=z1, x1=x1, hid=hid, z2=z2, x2=x2)


def _layer_bwd(dcur, mem2d, s, wl, *, bsz, seq, hosts=None, early=None):
    tm = min(512, seq)
    tk = min(1024, bsz * seq)
    dz2, dz2_b, dhpre, d_ln2g, d_ln2b = _hosted(
        hosts, "ln2_bwd_down", lambda c: _ln_bwd_mm_nt("ln2_bwd_down", dcur, s["z2"], wl["ln2_g"], wl["w_down"],
                                                       s["hid"], tm=tm, tn=1024, carry=c))
    g_down = _mm_tn("grad_w_down", s["hid"], dz2_b, tk=tk, tmo=1024, tno=1024)
    dx1 = _mm_nt_sum("mlp_up_bwd", [dhpre], [0], wl["w_up"], dz2, tm=tm)
    g_up = _mm_tn("grad_w_up", s["x1"], dhpre, tk=tk, tmo=1024, tno=1024)
    dz1, dz1_b, dmerged, d_ln1g, d_ln1b = _ln_bwd_mm_nt("ln1_bwd_wo", dx1, s["z1"], wl["ln1_g"], wl["w_o"],
                                                        tm=tm, tn=1024)
    g_o = _mm_tn("grad_w_o", s["merged"], dz1_b, tk=tk, tmo=1024, tno=1024)
    dgate, dpa, dpb, dpc, dya, dyb, dyc, d_bg = _merge_bwd(s["proj"], dmerged, s["pa"], s["pb"], s["pc"],
                                                           wl["w_br"], wl["b_gate"], tm=tm)
    g_br = jnp.stack([_mm_tn("grad_w_branch", yy, dp, tk=tk, tmo=512, tno=1024)
                      for yy, dp in ((s["y_a"], dpa), (s["y_b"], dpb), (s["y_c"], dpc))])
    d_conv, d_cw = _conv_bwd(s["proj"], dya, wl["conv"], tm=tm, seq=seq)
    hg_hosts = {"hgrn_bwd": early(g_br, g_o, g_up, g_down)} if early is not None else None
    dq, df, di, dg, d_lb, d_nw = _hosted(
        hg_hosts, "hgrn_bwd", lambda c: _hgrn_bwd(s["proj"], wl["lb"], wl["nw"], s["o_pre"], s["states"], dyb,
                                                  bsz=bsz, seq=seq, carry=c))
    dmq, dmk, dmv = _attn_bwd(s["proj"], s["mk"], s["mv"], dyc, tm=tm, seq=seq)
    tkm = min(512, mem2d.shape[0])
    g_mk = _mm_tn("grad_w_mem", mem2d, dmk, tk=tkm, tmo=1024, tno=512)
    g_mv = _mm_tn("grad_w_mem", mem2d, dmv, tk=tkm, tmo=1024, tno=512)
    pieces = [d_conv, dq, df, di, dg, dmq, dgate]
    offsets = [C_CB, C_HQ, C_HF, C_HI, C_HG, C_MQ, C_GA]
    dx = _mm_nt_sum("in_proj_bwd", pieces, offsets, wl["w_in"], dz1, tm=min(256, seq))
    g_in = jnp.concatenate(
        [_mm_tn("grad_w_in_%d" % p.shape[1], s["x"], p, tk=tk, tmo=1024,
                tno=(1024 if p.shape[1] % 1024 == 0 else 512)) for p in pieces], axis=1)
    small = jnp.concatenate([d_lb[0], d_cw.reshape(-1), d_nw[0], d_bg[0], d_ln1g[0], d_ln1b[0], d_ln2g[0],
                             d_ln2b[0]])
    return dx, [g_in, g_mk, g_mv, g_br, g_o, g_up, g_down], small


def kernel(x, mem, lower_bounds, w_in, conv_w, hg_norm_w, w_mem_k, w_mem_v, w_branch, b_gate, w_o, ln1_g, ln1_b, w_up, w_down, ln2_g, ln2_b, loss_target, m_lower_bounds, m_w_in, m_conv_w, m_hg_norm_w, m_w_mem_k, m_w_mem_v, m_w_branch, m_b_gate, m_w_o, m_ln1_g, m_ln1_b, m_w_up, m_w_down, m_ln2_g, m_ln2_b, v_lower_bounds, v_w_in, v_conv_w, v_hg_norm_w, v_w_mem_k, v_w_mem_v, v_w_branch, v_b_gate, v_w_o, v_ln1_g, v_ln1_b, v_w_up, v_w_down, v_ln2_g, v_ln2_b):
    bsz, seq, _ = x.shape
    t = bsz * seq
    me = _my_id()

    sh = dict(w_in=w_in.astype(BF16), w_mk=w_mem_k.astype(BF16), w_mv=w_mem_v.astype(BF16),
              w_br=w_branch.astype(BF16), w_o=w_o.astype(BF16), w_up=w_up.astype(BF16), w_down=w_down.astype(BF16))
    natural = dict(w_in=_natural_cols, w_mk=_natural_rows, w_mv=_natural_rows, w_br=_natural_cols,
                   w_o=_natural_rows, w_up=_natural_cols, w_down=_natural_rows)

    def gather_of(names, l):
        srcs = [sh[n][l] for n in names]
        return _Exchange(srcs, [_whole] * len(srcs), [s_.shape for s_ in srcs])

    def put(names, into):
        def hook(recv_):
            for n, r in zip(names, recv_):
                into[n] = natural[n](r)
        return hook

    lb_all = _lb_fwd(lower_bounds)
    layer_w = [dict(lb=lb_all[l][None], nw=hg_norm_w[l][None], b_gate=b_gate[l][None], ln1_g=ln1_g[l][None],
                    ln1_b=ln1_b[l][None], ln2_g=ln2_g[l][None], ln2_b=ln2_b[l][None]) for l in range(DEPTH)]
    first = ["w_in", "w_mk", "w_mv", "w_br", "w_o"]
    conv_shard = conv_w.reshape(DEPTH * CONV_K * (WIDTH // N_DEV) // 128, 128)
    ex0 = gather_of(first, 0)
    got = _exchange("gather_first", ex0.srcs + [conv_shard], [_whole] * (len(first) + 1),
                    [s_.shape for s_ in ex0.srcs] + [conv_shard.shape])
    put(first, layer_w[0])(got[:len(first)])
    conv_full = _natural_cols(got[-1].reshape(N_DEV, DEPTH, CONV_K, WIDTH // N_DEV))

    x2d = x.reshape(t, D_MODEL)
    mem2d = mem.reshape(bsz * MEM_LEN, D_MODEL)
    target2d = loss_target.reshape(t, D_MODEL)

    saved = []
    cur = x2d
    for l in range(DEPTH):
        wl = layer_w[l]
        wl["conv"] = conv_full[l]
        hosts = {"in_proj": (gather_of(["w_up", "w_down"], l), put(["w_up", "w_down"], wl))}
        if l + 1 < DEPTH:
            hosts["hgrn_fwd"] = (gather_of(["w_in"], l + 1), put(["w_in"], layer_w[l + 1]))
            rest = ["w_mk", "w_mv", "w_br", "w_o"]
            hosts["mlp_up"] = (gather_of(rest, l + 1), put(rest, layer_w[l + 1]))
        s = _layer_fwd(cur, mem2d, wl, bsz=bsz, seq=seq, hosts=hosts)
        saved.append(s)
        cur = s["x2"]

    dcur, loss_tile = _loss_head(cur, target2d, tm=min(512, seq))
    loss = lax.psum(loss_tile[0, 0], ("x", "y", "c"))

    slicer = dict(w_in=_cols(IN_COLS // N_DEV), w_mk=_rows(D_MODEL // N_DEV), w_mv=_rows(D_MODEL // N_DEV),
                  w_br=_cols(D_MODEL // N_DEV), w_o=_rows(D_MODEL // N_DEV), w_up=_cols(D_FF // N_DEV),
                  w_down=_rows(D_FF // N_DEV))
    recv = [dict() for _ in range(DEPTH)]

    def scatter_of(names, grads_):
        return _Exchange(grads_, [slicer[n] for n in names], [sh[n].shape[1:] for n in names])

    def keep(names, into):
        def hook(recv_):
            into.update(zip(names, recv_))
        return hook

    early_names = ["w_br", "w_o", "w_up", "w_down"]
    late_names = ["w_in", "w_mk", "w_mv"]
    small_rows = [None] * DEPTH
    late = None
    for l in reversed(range(DEPTH)):
        hosts = None
        if late is not None:
            hosts = {"ln2_bwd_down": (scatter_of(late_names, late), keep(late_names, recv[l + 1]))}
        dcur, big_grads, small_rows[l] = _layer_bwd(
            dcur, mem2d, saved[l], layer_w[l], bsz=bsz, seq=seq, hosts=hosts,
            early=lambda *g, l=l: (scatter_of(early_names, list(g)), keep(early_names, recv[l])))
        late = big_grads[:3]
    keep(late_names, recv[0])(_exchange("scatter_last", late, [slicer[n] for n in late_names],
                                        [sh[n].shape[1:] for n in late_names]))

    packed = jnp.concatenate(small_rows + [jnp.zeros((SMALL_ROWS * 128 - DEPTH * SMALL_PER_LAYER,), F32)])
    packed = packed.reshape(SMALL_ROWS, 128)
    all_small = _exchange("gather_small_grads", [packed], [_whole], [packed.shape])[0]
    summed = _sum_devices("sum_small_grads", all_small).reshape(-1)[:DEPTH * SMALL_PER_LAYER]
    summed = summed.reshape(DEPTH, SMALL_PER_LAYER)
    small_grads = {}
    off = 0
    for name, n in SMALL:
        small_grads[name] = summed[:, off:off + n]
        off += n
    small_grads["lower_bounds"] = _lb_bwd(lower_bounds, small_grads["lower_bounds"])
    conv_all = small_grads["conv_w"].reshape(DEPTH, CONV_K, WIDTH)
    small_grads["conv_w"] = lax.dynamic_slice_in_dim(conv_all, me * (WIDTH // N_DEV), WIDTH // N_DEV, axis=2)

    grads, deltas, new_m, new_v = {}, {}, {}, {}
    given = dict(lower_bounds=(lower_bounds, m_lower_bounds, v_lower_bounds), conv_w=(conv_w, m_conv_w, v_conv_w),
                 hg_norm_w=(hg_norm_w, m_hg_norm_w, v_hg_norm_w), b_gate=(b_gate, m_b_gate, v_b_gate),
                 ln1_g=(ln1_g, m_ln1_g, v_ln1_g), ln1_b=(ln1_b, m_ln1_b, v_ln1_b),
                 ln2_g=(ln2_g, m_ln2_g, v_ln2_g), ln2_b=(ln2_b, m_ln2_b, v_ln2_b))
    for name, (w_, m_, v_) in given.items():
        g_ = small_grads[name].reshape(w_.shape)
        grads[name] = g_
        deltas[name], new_m[name], new_v[name] = _adam_small("adam_" + name, g_, w_, m_, v_)

    big = dict(w_in=("w_in", w_in, m_w_in, v_w_in, 128), w_mem_k=("w_mk", w_mem_k, m_w_mem_k, v_w_mem_k, 128),
               w_mem_v=("w_mv", w_mem_v, m_w_mem_v, v_w_mem_v, 128),
               w_branch=("w_br", w_branch, m_w_branch, v_w_branch, 512), w_o=("w_o", w_o, m_w_o, v_w_o, 128),
               w_up=("w_up", w_up, m_w_up, v_w_up, 256), w_down=("w_down", w_down, m_w_down, v_w_down, 128))
    for name, (k, w_, m_, v_, tr) in big.items():
        shape = w_.shape
        flat = (DEPTH, -1, shape[-1])
        rc = [recv[l][k].reshape((N_DEV,) + w_.reshape(flat).shape[1:]) for l in range(DEPTH)]
        outs = _adam_shard("adam_" + name, rc, w_.reshape(flat), m_.reshape(flat), v_.reshape(flat), tr=tr)
        grads[name], deltas[name], new_m[name], new_v[name] = (o.reshape(shape) for o in outs)

    order = ["lower_bounds", "w_in", "conv_w", "hg_norm_w", "w_mem_k", "w_mem_v", "w_branch", "b_gate", "w_o",
             "ln1_g", "ln1_b", "w_up", "w_down", "ln2_g", "ln2_b"]
    return (loss, dcur.reshape(x.shape), *[grads[n] for n in order], *[deltas[n] for n in order],
            *[new_m[n] for n in order], *[new_v[n] for n in order])
```

```python
import functools

import jax
import jax.numpy as jnp
from jax import lax
from jax.experimental import pallas as pl
from jax.experimental.pallas import tpu as pltpu

F32 = jnp.float32
BF16 = jnp.bfloat16

N_DEV = 8
D_MODEL = 1024
DEPTH = 4
MEM_LEN = 256
CONV_K = 3
WIDTH = 512
HEADS = 4
HEAD_DIM = 128
CHUNK = 32
D_FF = 4 * D_MODEL
IN_COLS = 7168
ALPHA = (2.0 * DEPTH) ** 0.25
LN_EPS = 1e-5
RMS_EPS = 1e-6
ADAM_LR = 0.001
ADAM_B1 = 0.9
ADAM_B2 = 0.999
ADAM_EPS = 1e-08
ADAM_WD = 0.01
ADAM_STEP = 10

C_CB, C_CC, C_CH, C_HQ, C_HF, C_HI, C_HG, C_MQ, C_GA = 0, 512, 1024, 1536, 2048, 2560, 3072, 3584, 4096

ROWS_HG = 256
NT_DIMS = (((1,), (1,)), ((), ()))
TN_DIMS = (((0,), (0,)), ((), ()))
MESH = pl.DeviceIdType.MESH


def _dot(a, b):
    return jnp.dot(a, b, preferred_element_type=F32)


def _dot_nt(a, b):
    return lax.dot_general(a, b, NT_DIMS, preferred_element_type=F32)


def _dot_tn(a, b):
    return lax.dot_general(a, b, TN_DIMS, preferred_element_type=F32)


def _sigmoid(x):
    return 1.0 / (1.0 + jnp.exp(-x))


def _params(*sem):
    return pltpu.CompilerParams(dimension_semantics=sem)


def _resident(shape):
    nd = len(shape)
    return pl.BlockSpec(shape, lambda *_: (0,) * nd)


def _my_id():
    return 4 * lax.axis_index("x") + 2 * lax.axis_index("y") + lax.axis_index("c")


class _Exchange:
    def __init__(self, srcs, slicers, piece_shapes):
        self.srcs, self.slicers, self.n = list(srcs), list(slicers), len(srcs)
        any_spec = pl.BlockSpec(memory_space=pl.ANY)
        self.in_specs = [any_spec] * self.n
        self.out_specs = [any_spec] * self.n
        self.out_shape = [jax.ShapeDtypeStruct((N_DEV,) + tuple(s), a.dtype) for s, a in zip(piece_shapes, srcs)]
        self.scratch = [pltpu.SemaphoreType.DMA((self.n * N_DEV,)), pltpu.SemaphoreType.DMA((self.n * N_DEV,)),
                        pltpu.SemaphoreType.DMA((self.n,))]

    def _remote(self, ins, outs, sems, k, j, me):
        return pltpu.make_async_remote_copy(
            src_ref=self.slicers[k](ins[k], j), dst_ref=outs[k].at[me],
            send_sem=sems[0].at[k * N_DEV + j], recv_sem=sems[1].at[k * N_DEV + me],
            device_id=(j // 4, (j // 2) % 2, j % 2), device_id_type=MESH)

    def _local(self, ins, outs, sems, k, j, me):
        return pltpu.make_async_copy(self.slicers[k](ins[k], j), outs[k].at[me], sems[2].at[k])

    def start(self, ins, outs, sems):
        me = _my_id()
        for k in range(self.n):
            for j in range(N_DEV):
                @pl.when(j != me)
                def _():
                    self._remote(ins, outs, sems, k, j, me).start()

                @pl.when(j == me)
                def _():
                    self._local(ins, outs, sems, k, j, me).start()

    def wait(self, ins, outs, sems):
        me = _my_id()
        for k in range(self.n):
            for j in range(N_DEV):
                @pl.when(j != me)
                def _():
                    pltpu.make_async_remote_copy(
                        src_ref=self.slicers[k](ins[k], j), dst_ref=outs[k].at[j],
                        send_sem=sems[0].at[k * N_DEV + j], recv_sem=sems[1].at[k * N_DEV + j],
                        device_id=(j // 4, (j // 2) % 2, j % 2), device_id_type=MESH).wait_recv()
                    self._remote(ins, outs, sems, k, j, me).wait_send()

                @pl.when(j == me)
                def _():
                    self._local(ins, outs, sems, k, j, me).wait()


def _exchange(name, srcs, slicers, piece_shapes):
    ex = _Exchange(srcs, slicers, piece_shapes)

    def body(*refs):
        ins, outs, sems = refs[:ex.n], refs[ex.n:2 * ex.n], refs[2 * ex.n:]
        ex.start(ins, outs, sems)
        ex.wait(ins, outs, sems)

    return pl.pallas_call(
        body, name=name, in_specs=ex.in_specs, out_specs=ex.out_specs, out_shape=ex.out_shape,
        scratch_shapes=ex.scratch, compiler_params=pltpu.CompilerParams(has_side_effects=True),
    )(*ex.srcs)


def _call(body, name, grid, in_specs, out_specs, out_shape, args, scratch=(), sem=None, carry=None):
    n_in, n_out, n_scr = len(in_specs), len(out_specs), len(scratch)
    if carry is None:
        outs = pl.pallas_call(body, name=name, grid=grid, in_specs=in_specs, out_specs=out_specs,
                              out_shape=out_shape, scratch_shapes=list(scratch),
                              compiler_params=_params(*sem))(*args)
        return outs, None
    nc = carry.n

    def hosted(*refs):
        ins, c_in = refs[:n_in], refs[n_in:n_in + nc]
        outs = refs[n_in + nc:n_in + nc + n_out]
        c_out = refs[n_in + nc + n_out:n_in + 2 * nc + n_out]
        rest = refs[n_in + 2 * nc + n_out:]
        scr, sems = rest[:n_scr], rest[n_scr:]
        first, last = True, True
        for d, size in enumerate(grid):
            first = first & (pl.program_id(d) == 0)
            last = last & (pl.program_id(d) == size - 1)

        @pl.when(first)
        def _():
            carry.start(c_in, c_out, sems)

        body(*ins, *outs, *scr)

        @pl.when(last)
        def _():
            carry.wait(c_in, c_out, sems)

    outs = pl.pallas_call(
        hosted, name=name + "_x", grid=grid, in_specs=list(in_specs) + carry.in_specs,
        out_specs=list(out_specs) + carry.out_specs, out_shape=list(out_shape) + carry.out_shape,
        scratch_shapes=list(scratch) + carry.scratch,
        compiler_params=_params(*(["arbitrary"] * len(grid))))(*args, *carry.srcs)
    return outs[:n_out], outs[n_out:]


def _whole(ref, j):
    return ref


def _cols(width):
    return lambda ref, j: ref.at[(slice(None),) * (len(ref.shape) - 1) + (pl.ds(j * width, width),)]


def _rows(height):
    return lambda ref, j: ref.at[pl.ds(j * height, height)]


def _mm_nn(name, a, w, *, tm, tn, out_dtype, relu2=False, carry=None):
    t, k = a.shape
    n = w.shape[1]

    def body(a_ref, w_ref, o_ref):
        acc = _dot(a_ref[...].astype(BF16), w_ref[...])
        if relu2:
            r = jnp.maximum(acc, 0.0)
            acc = r * r
        o_ref[...] = acc.astype(out_dtype)

    outs, recv = _call(
        body, name, (t // tm, n // tn),
        [pl.BlockSpec((tm, k), lambda i, j: (i, 0)), pl.BlockSpec((k, tn), lambda i, j: (0, j))],
        [pl.BlockSpec((tm, tn), lambda i, j: (i, j))], [jax.ShapeDtypeStruct((t, n), out_dtype)], (a, w),
        sem=("parallel", "parallel"), carry=carry)
    return outs[0] if carry is None else (outs[0], recv)


def _layer_norm(z, g, b):
    mu = jnp.mean(z, axis=-1, keepdims=True)
    zc = z - mu
    var = jnp.mean(zc * zc, axis=-1, keepdims=True)
    return zc * lax.rsqrt(var + LN_EPS) * g + b


def _linear_ln(name, a, w, resid, g, b, *, tm):
    t, k = a.shape

    def body(a_ref, w_ref, r_ref, g_ref, b_ref, z_ref, x_ref, xb_ref):
        z = ALPHA * r_ref[...] + _dot(a_ref[...], w_ref[...])
        z_ref[...] = z
        y = _layer_norm(z, g_ref[...], b_ref[...])
        x_ref[...] = y
        xb_ref[...] = y.astype(BF16)

    row = pl.BlockSpec((tm, D_MODEL), lambda i: (i, 0))
    return pl.pallas_call(
        body, name=name, grid=(t // tm,),
        in_specs=[pl.BlockSpec((tm, k), lambda i: (i, 0)), _resident((k, D_MODEL)), row,
                  _resident((1, D_MODEL)), _resident((1, D_MODEL))],
        out_specs=[row, row, row],
        out_shape=[jax.ShapeDtypeStruct((t, D_MODEL), F32)] * 2 + [jax.ShapeDtypeStruct((t, D_MODEL), BF16)],
        compiler_params=_params("parallel"),
    )(a, w, resid, g, b)


def _ln_bwd_mm_nt(name, dy, z, g, w, h=None, *, tm, tn, carry=None):
    t = dy.shape[0]
    n = w.shape[0]

    def body(*refs):
        if h is None:
            dy_ref, z_ref, g_ref, w_ref, dz_ref, dzb_ref, o_ref, dg_ref, db_ref = refs
        else:
            dy_ref, z_ref, g_ref, w_ref, h_ref, dz_ref, dzb_ref, o_ref, dg_ref, db_ref = refs
        i, j = pl.program_id(0), pl.program_id(1)

        @pl.when(j == 0)
        def _():
            zv = z_ref[...]
            mu = jnp.mean(zv, axis=-1, keepdims=True)
            zc = zv - mu
            var = jnp.mean(zc * zc, axis=-1, keepdims=True)
            rstd = lax.rsqrt(var + LN_EPS)
            xh = zc * rstd
            dyv = dy_ref[...]
            gdy = dyv * g_ref[...]
            m1 = jnp.mean(gdy, axis=-1, keepdims=True)
            m2 = jnp.mean(gdy * xh, axis=-1, keepdims=True)
            dz = rstd * (gdy - m1 - xh * m2)
            dz_ref[...] = dz
            dzb_ref[...] = dz.astype(BF16)

            @pl.when(i == 0)
            def _():
                dg_ref[...] = jnp.zeros_like(dg_ref)
                db_ref[...] = jnp.zeros_like(db_ref)

            dg_ref[...] += jnp.sum(dyv * xh, axis=0, keepdims=True)
            db_ref[...] += jnp.sum(dyv, axis=0, keepdims=True)

        acc = _dot_nt(dzb_ref[...], w_ref[...])
        if h is not None:
            acc = acc * (2.0 * jnp.sqrt(h_ref[...].astype(F32)))
        o_ref[...] = acc.astype(BF16)

    row = pl.BlockSpec((tm, D_MODEL), lambda i, j: (i, 0))
    vec = pl.BlockSpec((1, D_MODEL), lambda i, j: (0, 0))
    tile = pl.BlockSpec((tm, tn), lambda i, j: (i, j))
    in_specs = [row, row, vec, pl.BlockSpec((tn, D_MODEL), lambda i, j: (j, 0))]
    args = [dy, z, g, w]
    if h is not None:
        in_specs.append(tile)
        args.append(h)
    outs, recv = _call(
        body, name, (t // tm, n // tn), in_specs, [row, row, tile, vec, vec],
        [jax.ShapeDtypeStruct((t, D_MODEL), F32), jax.ShapeDtypeStruct((t, D_MODEL), BF16),
         jax.ShapeDtypeStruct((t, n), BF16), jax.ShapeDtypeStruct((1, D_MODEL), F32),
         jax.ShapeDtypeStruct((1, D_MODEL), F32)], args, sem=("arbitrary", "arbitrary"), carry=carry)
    return outs if carry is None else (outs, recv)


def _mm_tn(name, a, b, *, tk, tmo, tno):
    t, m = a.shape
    n = b.shape[1]
    nk = t // tk

    def body(a_ref, b_ref, o_ref, acc_ref):
        k = pl.program_id(2)
        p = _dot_tn(a_ref[...].astype(BF16), b_ref[...].astype(BF16))

        @pl.when(k == 0)
        def _():
            acc_ref[...] = p

        @pl.when(k > 0)
        def _():
            acc_ref[...] += p

        @pl.when(k == nk - 1)
        def _():
            o_ref[...] = acc_ref[...].astype(BF16)

    return pl.pallas_call(
        body, name=name, grid=(m // tmo, n // tno, nk),
        in_specs=[pl.BlockSpec((tk, tmo), lambda i, j, k: (k, i)), pl.BlockSpec((tk, tno), lambda i, j, k: (k, j))],
        out_specs=pl.BlockSpec((tmo, tno), lambda i, j, k: (i, j)),
        out_shape=jax.ShapeDtypeStruct((m, n), BF16),
        scratch_shapes=[pltpu.VMEM((tmo, tno), F32)],
        compiler_params=_params("parallel", "parallel", "arbitrary"),
    )(a, b)


def _mm_nt_sum(name, pieces, offsets, w, resid, *, tm, carry=None):
    t = resid.shape[0]
    widths = [p.shape[1] for p in pieces]
    n_p = len(pieces)

    def body(*refs):
        p_refs, w_ref, r_ref, o_ref = refs[:n_p], refs[n_p], refs[n_p + 1], refs[n_p + 2]
        acc = ALPHA * r_ref[...]
        for p_ref, off, wd in zip(p_refs, offsets, widths):
            acc = acc + _dot_nt(p_ref[...], w_ref[:, off:off + wd])
        o_ref[...] = acc

    row = pl.BlockSpec((tm, D_MODEL), lambda i: (i, 0))
    outs, recv = _call(
        body, name, (t // tm,),
        [pl.BlockSpec((tm, wd), lambda i: (i, 0)) for wd in widths] + [_resident(w.shape), row],
        [row], [jax.ShapeDtypeStruct((t, D_MODEL), F32)], (*pieces, w, resid), sem=("parallel",), carry=carry)
    return outs[0] if carry is None else (outs[0], recv)


def _chunk_mask(rows):
    r = lax.broadcasted_iota(jnp.int32, (rows, rows), 0)
    c = lax.broadcasted_iota(jnp.int32, (rows, rows), 1)
    return ((r // CHUNK) == (c // CHUNK)) & (c <= r)


def _mask_sum(mask_b, x, transpose=False):
    f = _dot_tn if transpose else _dot
    hi = x.astype(BF16)
    lo = (x - hi.astype(F32)).astype(BF16)
    return f(mask_b, hi) + f(mask_b, lo)


def _chunk_row(x, pos, rows):
    nc = rows // CHUNK
    x3 = x.reshape(nc, CHUNK, HEAD_DIM)
    return jnp.broadcast_to(x3[:, pos:pos + 1, :], (nc, CHUNK, HEAD_DIM)).reshape(rows, HEAD_DIM)


def _chunk_total(x, rows):
    nc = rows // CHUNK
    tot = jnp.sum(x.reshape(nc, CHUNK, HEAD_DIM), axis=1, keepdims=True)
    return jnp.broadcast_to(tot, (nc, CHUNK, HEAD_DIM)).reshape(rows, HEAD_DIM)


def _sigmoid_pair(x):
    e = jnp.exp(-jnp.abs(x))
    big = 1.0 / (1.0 + e)
    small = e * big
    pos = x >= 0.0
    return jnp.where(pos, big, small), jnp.where(pos, small, big)


def _hg_gates(q_raw, fl, lb, rows, mask):
    tri = mask.astype(BF16)
    sg, sg_neg = _sigmoid_pair(fl)
    forget = lb + (1.0 - lb) * sg
    k = (1.0 - lb) * sg_neg
    sq = _sigmoid(q_raw)
    qs = q_raw * sq
    bc = _mask_sum(tri, jnp.log(forget))
    bref = _chunk_row(bc, CHUNK // 2 - 1, rows)
    blast = _chunk_row(bc, CHUNK - 1, rows)
    return dict(tri=tri, sg=sg, sg_neg=sg_neg, forget=forget, k=k, sq=sq, qs=qs,
                e_a=jnp.exp(bc - bref), e_b=jnp.exp(bref - bc), e_q=jnp.exp(bc), e_k=jnp.exp(blast - bc),
                dec=jnp.exp(blast))


def _hgrn_fwd(proj, lb, nw, *, bsz, seq, carry=None):
    rows = min(ROWS_HG, seq)
    nt = seq // rows
    nc = rows // CHUNK
    t = bsz * seq

    def body(q_ref, f_ref, v_ref, g_ref, lb_ref, nw_ref, y_ref, o_ref, st_ref, s_scr):
        @pl.when(pl.program_id(1) == 0)
        def _():
            s_scr[...] = jnp.zeros_like(s_scr)

        mask = _chunk_mask(rows)
        for b in range(bsz):
            gt = _hg_gates(q_ref[b], f_ref[b], lb_ref[...], rows, mask)
            v_b = v_ref[b].astype(BF16)
            a_b = (gt["qs"] * gt["e_a"]).astype(BF16)
            b_b = (gt["k"] * gt["e_b"]).astype(BF16)
            qi_b = (gt["qs"] * gt["e_q"]).astype(BF16)
            ko_b = (gt["k"] * gt["e_k"]).astype(BF16)
            scores = jnp.where(mask, _dot_nt(a_b, b_b), 0.0)
            o_intra = _dot(scores.astype(BF16), v_b)

            s = s_scr[b]
            parts = []
            for n in range(nc):
                sl = slice(n * CHUNK, (n + 1) * CHUNK)
                s_b = s.astype(BF16)
                st_ref[b, n] = s_b
                parts.append(_dot_nt(qi_b[sl], s_b))
                s = s * gt["dec"][n * CHUNK:n * CHUNK + 1] + _dot_tn(v_b[sl], ko_b[sl])
            s_scr[b] = s
            o = o_intra + jnp.concatenate(parts, axis=0)
            o_ref[b] = o
            r = lax.rsqrt(jnp.mean(o * o, axis=-1, keepdims=True) + RMS_EPS)
            g = g_ref[b]
            y_ref[b] = (o * r * nw_ref[...] * (g * _sigmoid(g))).astype(BF16)

    def col(base):
        return pl.BlockSpec((bsz, rows, HEAD_DIM), lambda h, j: (0, j, base // HEAD_DIM + h))

    out_tile = pl.BlockSpec((bsz, rows, HEAD_DIM), lambda h, j: (0, j, h))
    p3 = proj.reshape(bsz, seq, IN_COLS)
    outs, recv = _call(
        body, "hgrn_fwd", (HEADS, nt),
        [col(C_HQ), col(C_HF), col(C_HI), col(C_HG),
         pl.BlockSpec((1, HEAD_DIM), lambda h, j: (0, h)), _resident((1, HEAD_DIM))],
        [out_tile, out_tile, pl.BlockSpec((None, bsz, nc, HEAD_DIM, HEAD_DIM), lambda h, j: (h, 0, j, 0, 0))],
        [jax.ShapeDtypeStruct((bsz, seq, WIDTH), BF16), jax.ShapeDtypeStruct((bsz, seq, WIDTH), F32),
         jax.ShapeDtypeStruct((HEADS, bsz, seq // CHUNK, HEAD_DIM, HEAD_DIM), BF16)],
        (p3, p3, p3, p3, lb, nw), scratch=[pltpu.VMEM((bsz, HEAD_DIM, HEAD_DIM), F32)],
        sem=("parallel", "arbitrary"), carry=carry)
    outs = [outs[0].reshape(t, WIDTH), outs[1].reshape(t, WIDTH), outs[2]]
    return outs if carry is None else (outs, recv)


def _hgrn_bwd(proj, lb, nw, o_pre, states, dy, *, bsz, seq, carry=None):
    rows = min(ROWS_HG, seq)
    nt = seq // rows
    nc = rows // CHUNK
    t = bsz * seq

    def body(q_ref, f_ref, v_ref, g_ref, lb_ref, nw_ref, o_ref, st_ref, dy_ref,
             dq_ref, df_ref, dv_ref, dg_ref, dlb_ref, dnw_ref, ds_scr):
        h, j = pl.program_id(0), pl.program_id(1)

        @pl.when(j == 0)
        def _():
            ds_scr[...] = jnp.zeros_like(ds_scr)
            dlb_ref[...] = jnp.zeros_like(dlb_ref)

        @pl.when((h == 0) & (j == 0))
        def _():
            dnw_ref[...] = jnp.zeros_like(dnw_ref)

        mask = _chunk_mask(rows)
        for b in range(bsz):
            one_example(b, mask, q_ref, f_ref, v_ref, g_ref, lb_ref, nw_ref, o_ref, st_ref, dy_ref,
                        dq_ref, df_ref, dv_ref, dg_ref, dlb_ref, dnw_ref, ds_scr)

    def one_example(b, mask, q_ref, f_ref, v_ref, g_ref, lb_ref, nw_ref, o_ref, st_ref, dy_ref,
                    dq_ref, df_ref, dv_ref, dg_ref, dlb_ref, dnw_ref, ds_scr):
        q_raw, fl, lb_v = q_ref[b], f_ref[b], lb_ref[...]
        gt = _hg_gates(q_raw, fl, lb_v, rows, mask)
        v_b = v_ref[b].astype(BF16)
        a_f = gt["qs"] * gt["e_a"]
        b_f = gt["k"] * gt["e_b"]
        qi_f = gt["qs"] * gt["e_q"]
        ko_f = gt["k"] * gt["e_k"]
        a_b, b_b, qi_b, ko_b = a_f.astype(BF16), b_f.astype(BF16), qi_f.astype(BF16), ko_f.astype(BF16)

        o = o_ref[b]
        nw_v = nw_ref[...]
        g = g_ref[b]
        dyv = dy_ref[b]
        r = lax.rsqrt(jnp.mean(o * o, axis=-1, keepdims=True) + RMS_EPS)
        sgg = _sigmoid(g)
        d_g = dyv * (o * r * nw_v) * (sgg * (1.0 + g * (1.0 - sgg)))
        d_on = dyv * (g * sgg)
        dnw_ref[...] += jnp.sum(d_on * o * r, axis=0, keepdims=True)
        tt = d_on * nw_v
        d_o = r * tt - o * (r * r * r) * jnp.mean(tt * o, axis=-1, keepdims=True)
        do_b = d_o.astype(BF16)

        sc_b = jnp.where(mask, _dot_nt(a_b, b_b), 0.0).astype(BF16)
        dsc_b = jnp.where(mask, _dot_nt(do_b, v_b), 0.0).astype(BF16)
        d_v = _dot_tn(sc_b, do_b)
        d_a = _dot(dsc_b, b_b)
        d_bm = _dot_tn(dsc_b, a_b)

        ds = ds_scr[b]
        dqi_parts, dko_parts, dvi_parts, ddec_parts = [None] * nc, [None] * nc, [None] * nc, [None] * nc
        for n in reversed(range(nc)):
            sl = slice(n * CHUNK, (n + 1) * CHUNK)
            dec_n = gt["dec"][n * CHUNK:n * CHUNK + 1]
            ds_b = ds.astype(BF16)
            s_n = st_ref[b, n]
            dqi_parts[n] = _dot(do_b[sl], s_n)
            dko_parts[n] = _dot(v_b[sl], ds_b)
            dvi_parts[n] = _dot_nt(ko_b[sl], ds_b)
            d_dec = jnp.sum(ds * s_n.astype(F32), axis=0, keepdims=True)
            ddec_parts[n] = jnp.broadcast_to(d_dec * dec_n, (CHUNK, HEAD_DIM))
            ds = ds * dec_n + _dot_tn(do_b[sl], qi_b[sl])
        ds_scr[b] = ds
        d_qi = jnp.concatenate(dqi_parts, axis=0)
        d_ko = jnp.concatenate(dko_parts, axis=0)
        d_v = d_v + jnp.concatenate(dvi_parts, axis=0)

        d_qs = d_a * gt["e_a"] + d_qi * gt["e_q"]
        d_k = d_bm * gt["e_b"] + d_ko * gt["e_k"]
        t_a, t_b, t_q, t_k = d_a * a_f, d_bm * b_f, d_qi * qi_f, d_ko * ko_f
        d_bref = _chunk_total(t_b - t_a, rows)
        d_blast = _chunk_total(t_k, rows) + jnp.concatenate(ddec_parts, axis=0)
        pos = lax.broadcasted_iota(jnp.int32, (rows, HEAD_DIM), 0) % CHUNK
        d_bc = (t_a - t_b + t_q - t_k + jnp.where(pos == CHUNK // 2 - 1, d_bref, 0.0)
                + jnp.where(pos == CHUNK - 1, d_blast, 0.0))
        d_logf = _mask_sum(gt["tri"], d_bc, transpose=True)

        sg, sg_neg = gt["sg"], gt["sg_neg"]
        inv_f = 1.0 / gt["forget"]
        common = (1.0 - lb_v) * sg * sg_neg
        d_fl = common * (d_logf * inv_f - d_k)
        dlb_ref[...] += jnp.sum(sg_neg * (d_logf * inv_f - d_k), axis=0, keepdims=True)
        sq = gt["sq"]
        dq_ref[b] = (d_qs * (sq * (1.0 + q_raw * (1.0 - sq)))).astype(BF16)
        df_ref[b] = d_fl.astype(BF16)
        dv_ref[b] = d_v.astype(BF16)
        dg_ref[b] = d_g.astype(BF16)

    def col(base):
        return pl.BlockSpec((bsz, rows, HEAD_DIM), lambda h, j: (0, nt - 1 - j, base // HEAD_DIM + h))

    tile = pl.BlockSpec((bsz, rows, HEAD_DIM), lambda h, j: (0, nt - 1 - j, h))
    head_vec = pl.BlockSpec((1, HEAD_DIM), lambda h, j: (0, h))
    d_out = jax.ShapeDtypeStruct((bsz, seq, WIDTH), BF16)
    p3 = proj.reshape(bsz, seq, IN_COLS)
    outs, recv = _call(
        body, "hgrn_bwd", (HEADS, nt),
        [col(C_HQ), col(C_HF), col(C_HI), col(C_HG), head_vec, _resident((1, HEAD_DIM)), tile,
         pl.BlockSpec((None, bsz, nc, HEAD_DIM, HEAD_DIM), lambda h, j: (h, 0, nt - 1 - j, 0, 0)), tile],
        [tile, tile, tile, tile, head_vec, _resident((1, HEAD_DIM))],
        [d_out, d_out, d_out, d_out, jax.ShapeDtypeStruct((1, WIDTH), F32),
         jax.ShapeDtypeStruct((1, HEAD_DIM), F32)],
        (p3, p3, p3, p3, lb, nw, o_pre.reshape(bsz, seq, WIDTH), states, dy.reshape(bsz, seq, WIDTH)),
        scratch=[pltpu.VMEM((bsz, HEAD_DIM, HEAD_DIM), F32)],
        sem=("arbitrary", "arbitrary"), carry=carry)
    outs = [o_.reshape(t, WIDTH) for o_ in outs[:4]] + list(outs[4:])
    return outs if carry is None else (outs, recv)


def _mem_kv(mem2d, w_k, w_v):
    rows = mem2d.shape[0]

    def body(m_ref, wk_ref, wv_ref, k_ref, v_ref):
        m_b = m_ref[...].astype(BF16)
        k_ref[...] = _dot(m_b, wk_ref[...]).astype(BF16)
        v_ref[...] = _dot(m_b, wv_ref[...]).astype(BF16)

    return pl.pallas_call(
        body, name="mem_kv", grid=(rows // MEM_LEN,),
        in_specs=[pl.BlockSpec((MEM_LEN, D_MODEL), lambda i: (i, 0)), _resident((D_MODEL, WIDTH)),
                  _resident((D_MODEL, WIDTH))],
        out_specs=[pl.BlockSpec((MEM_LEN, WIDTH), lambda i: (i, 0))] * 2,
        out_shape=[jax.ShapeDtypeStruct((rows, WIDTH), BF16)] * 2,
        compiler_params=_params("parallel"),
    )(mem2d, w_k, w_v)


def _softmax_rows(s):
    m = jnp.max(s, axis=-1, keepdims=True)
    e = jnp.exp(s - m)
    return e / jnp.sum(e, axis=-1, keepdims=True)


def _attn_fwd(proj, mk, mv, *, tm, seq):
    t = proj.shape[0]
    per_b = seq // tm
    scale = HEAD_DIM ** -0.5

    def body(q_ref, k_ref, v_ref, y_ref):
        outs = []
        for h in range(HEADS):
            sl = slice(h * HEAD_DIM, (h + 1) * HEAD_DIM)
            s = _dot_nt(q_ref[:, sl].astype(BF16), k_ref[:, sl]) * scale
            p = _softmax_rows(s)
            outs.append(_dot(p.astype(BF16), v_ref[:, sl]))
        y_ref[...] = jnp.concatenate(outs, axis=-1).astype(BF16)

    kv = pl.BlockSpec((MEM_LEN, WIDTH), lambda i: (i // per_b, 0))
    return pl.pallas_call(
        body, name="attn_fwd", grid=(t // tm,),
        in_specs=[pl.BlockSpec((tm, WIDTH), lambda i: (i, C_MQ // WIDTH)), kv, kv],
        out_specs=pl.BlockSpec((tm, WIDTH), lambda i: (i, 0)),
        out_shape=jax.ShapeDtypeStruct((t, WIDTH), BF16),
        compiler_params=_params("parallel"),
    )(proj, mk, mv)


def _attn_bwd(proj, mk, mv, dy, *, tm, seq):
    t = proj.shape[0]
    per_b = seq // tm
    scale = HEAD_DIM ** -0.5

    def body(q_ref, k_ref, v_ref, dy_ref, dq_ref, dk_ref, dv_ref):
        i = pl.program_id(0)

        @pl.when(i % per_b == 0)
        def _():
            dk_ref[...] = jnp.zeros_like(dk_ref)
            dv_ref[...] = jnp.zeros_like(dv_ref)

        dqs, dks, dvs = [], [], []
        for h in range(HEADS):
            sl = slice(h * HEAD_DIM, (h + 1) * HEAD_DIM)
            q_b = q_ref[:, sl].astype(BF16)
            k_b, v_b = k_ref[:, sl], v_ref[:, sl]
            p = _softmax_rows(_dot_nt(q_b, k_b) * scale)
            dy_b = dy_ref[:, sl].astype(BF16)
            dp = _dot_nt(dy_b, v_b)
            dvs.append(_dot_tn(p.astype(BF16), dy_b))
            ds_b = (p * (dp - jnp.sum(dp * p, axis=-1, keepdims=True)) * scale).astype(BF16)
            dqs.append(_dot(ds_b, k_b))
            dks.append(_dot_tn(ds_b, q_b))
        dq_ref[...] = jnp.concatenate(dqs, axis=-1).astype(BF16)
        dk_ref[...] += jnp.concatenate(dks, axis=-1)
        dv_ref[...] += jnp.concatenate(dvs, axis=-1)

    kv = pl.BlockSpec((MEM_LEN, WIDTH), lambda i: (i // per_b, 0))
    tile = pl.BlockSpec((tm, WIDTH), lambda i: (i, 0))
    n_mem = mk.shape[0]
    return pl.pallas_call(
        body, name="attn_bwd", grid=(t // tm,),
        in_specs=[pl.BlockSpec((tm, WIDTH), lambda i: (i, C_MQ // WIDTH)), kv, kv, tile],
        out_specs=[tile, kv, kv],
        out_shape=[jax.ShapeDtypeStruct((t, WIDTH), BF16), jax.ShapeDtypeStruct((n_mem, WIDTH), F32),
                   jax.ShapeDtypeStruct((n_mem, WIDTH), F32)],
        compiler_params=_params("arbitrary"),
    )(proj, mk, mv, dy)


HALO = 8


def _shift_down(u, halo, k, row):
    out = pltpu.roll(u, k, 0)
    for m in range(k):
        out = jnp.where(row == m, halo[HALO - k + m:HALO - k + m + 1, :], out)
    return out


def _shift_up(u, halo, k, row, tm):
    out = pltpu.roll(u, tm - k, 0)
    for m in range(k):
        out = jnp.where(row == tm - k + m, halo[m:m + 1, :], out)
    return out


def _merge_fwd(proj, y_b, y_c, conv_w, w_branch, b_gate, *, tm, seq):
    t = proj.shape[0]
    per_b = seq // tm
    hb = tm // HALO

    def body(cb_ref, cc_ref, ch_ref, cch_ref, chh_ref, ga_ref, gb_ref, gc_ref, yb_ref, yc_ref, cw_ref, wb_ref,
             bg_ref, ya_ref, pa_ref, pb_ref, pc_ref, mg_ref):
        i = pl.program_id(0)
        row = lax.broadcasted_iota(jnp.int32, (tm, WIDTH), 0)
        u = cc_ref[...] * ch_ref[...]
        halo = jnp.where(i % per_b == 0, 0.0, cch_ref[...] * chh_ref[...])
        cw = cw_ref[...]
        y = cw[0:1] * _shift_down(u, halo, 2, row) + cw[1:2] * _shift_down(u, halo, 1, row) + cw[2:3] * u
        ya_b = (cb_ref[...] * y).astype(BF16)
        ya_ref[...] = ya_b
        merged = None
        for idx, (y_in, g_ref, p_ref) in enumerate(((ya_b, ga_ref, pa_ref), (yb_ref[...], gb_ref, pb_ref),
                                                     (yc_ref[...], gc_ref, pc_ref))):
            p = _dot(y_in, wb_ref[idx])
            p_ref[...] = p.astype(BF16)
            term = _sigmoid(g_ref[...] + bg_ref[:, idx * D_MODEL:(idx + 1) * D_MODEL]) * p
            merged = term if merged is None else merged + term
        mg_ref[...] = merged.astype(BF16)

    def half(c):
        return pl.BlockSpec((tm, WIDTH), lambda i: (i, c // WIDTH))

    def prev(c):
        return pl.BlockSpec((HALO, WIDTH), lambda i: (jnp.maximum(i * hb - 1, 0), c // WIDTH))

    def gate(k):
        return pl.BlockSpec((tm, D_MODEL), lambda i: (i, C_GA // D_MODEL + k))

    tile512 = pl.BlockSpec((tm, WIDTH), lambda i: (i, 0))
    tile1k = pl.BlockSpec((tm, D_MODEL), lambda i: (i, 0))
    return pl.pallas_call(
        body, name="merge_fwd", grid=(t // tm,),
        in_specs=[half(C_CB), half(C_CC), half(C_CH), prev(C_CC), prev(C_CH), gate(0), gate(1), gate(2),
                  tile512, tile512, _resident((CONV_K, WIDTH)), _resident((3, WIDTH, D_MODEL)),
                  _resident((1, 3 * D_MODEL))],
        out_specs=[tile512, tile1k, tile1k, tile1k, tile1k],
        out_shape=[jax.ShapeDtypeStruct((t, WIDTH), BF16)] + [jax.ShapeDtypeStruct((t, D_MODEL), BF16)] * 4,
        compiler_params=_params("parallel"),
    )(proj, proj, proj, proj, proj, proj, proj, proj, y_b, y_c, conv_w, w_branch, b_gate)


def _merge_bwd(proj, dmerged, pa, pb, pc, w_branch, b_gate, *, tm):
    t = proj.shape[0]

    def body(dm_ref, pa_ref, pb_ref, pc_ref, ga_ref, gb_ref, gc_ref, wb_ref, bg_ref,
             dgt_ref, dpa_ref, dpb_ref, dpc_ref, dya_ref, dyb_ref, dyc_ref, dbg_ref):
        i = pl.program_id(0)

        @pl.when(i == 0)
        def _():
            dbg_ref[...] = jnp.zeros_like(dbg_ref)

        dm = dm_ref[...].astype(F32)
        for idx, (p_ref, g_ref, dp_ref, dy_ref) in enumerate(((pa_ref, ga_ref, dpa_ref, dya_ref),
                                                              (pb_ref, gb_ref, dpb_ref, dyb_ref),
                                                              (pc_ref, gc_ref, dpc_ref, dyc_ref))):
            cols = slice(idx * D_MODEL, (idx + 1) * D_MODEL)
            sg = _sigmoid(g_ref[...] + bg_ref[:, cols])
            dp_b = (dm * sg).astype(BF16)
            dp_ref[...] = dp_b
            dgate = dm * p_ref[...].astype(F32) * sg * (1.0 - sg)
            dgt_ref[:, cols] = dgate.astype(BF16)
            dbg_ref[:, cols] += jnp.sum(dgate, axis=0, keepdims=True)
            dy_ref[...] = _dot_nt(dp_b, wb_ref[idx])

    def gate(k):
        return pl.BlockSpec((tm, D_MODEL), lambda i: (i, C_GA // D_MODEL + k))

    tile512 = pl.BlockSpec((tm, WIDTH), lambda i: (i, 0))
    tile1k = pl.BlockSpec((tm, D_MODEL), lambda i: (i, 0))
    return pl.pallas_call(
        body, name="merge_bwd", grid=(t // tm,),
        in_specs=[tile1k, tile1k, tile1k, tile1k, gate(0), gate(1), gate(2), _resident((3, WIDTH, D_MODEL)),
                  _resident((1, 3 * D_MODEL))],
        out_specs=[pl.BlockSpec((tm, 3 * D_MODEL), lambda i: (i, 0)), tile1k, tile1k, tile1k,
                   tile512, tile512, tile512, _resident((1, 3 * D_MODEL))],
        out_shape=[jax.ShapeDtypeStruct((t, 3 * D_MODEL), BF16)] + [jax.ShapeDtypeStruct((t, D_MODEL), BF16)] * 3
                  + [jax.ShapeDtypeStruct((t, WIDTH), F32)] * 3 + [jax.ShapeDtypeStruct((1, 3 * D_MODEL), F32)],
        compiler_params=_params("arbitrary"),
    )(dmerged, pa, pb, pc, proj, proj, proj, w_branch, b_gate)


def _conv_bwd(proj, dya, conv_w, *, tm, seq):
    t = proj.shape[0]
    per_b = seq // tm
    hb = tm // HALO
    last_blk = t // HALO - 1

    def body(cb_ref, cc_ref, ch_ref, cch_ref, chh_ref, dya_ref, cbn_ref, dyan_ref, cw_ref, d_ref, dcw_ref):
        i = pl.program_id(0)

        @pl.when(i == 0)
        def _():
            dcw_ref[...] = jnp.zeros_like(dcw_ref)

        row = lax.broadcasted_iota(jnp.int32, (tm, WIDTH), 0)
        cb, cc, ch = cb_ref[...], cc_ref[...], ch_ref[...]
        u = cc * ch
        halo = jnp.where(i % per_b == 0, 0.0, cch_ref[...] * chh_ref[...])
        u1 = _shift_down(u, halo, 1, row)
        u2 = _shift_down(u, halo, 2, row)
        cw = cw_ref[...]
        y = cw[0:1] * u2 + cw[1:2] * u1 + cw[2:3] * u
        dya = dya_ref[...]
        dy = dya * cb
        nxt = jnp.where(i % per_b == per_b - 1, 0.0, dyan_ref[...] * cbn_ref[...])
        du = cw[2:3] * dy + cw[1:2] * _shift_up(dy, nxt, 1, row, tm) + cw[0:1] * _shift_up(dy, nxt, 2, row, tm)
        d_ref[:, 0:WIDTH] = (dya * y).astype(BF16)
        d_ref[:, WIDTH:2 * WIDTH] = (du * ch).astype(BF16)
        d_ref[:, 2 * WIDTH:3 * WIDTH] = (du * cc).astype(BF16)
        dcw_ref[0:1, :] += jnp.sum(dy * u2, axis=0, keepdims=True)
        dcw_ref[1:2, :] += jnp.sum(dy * u1, axis=0, keepdims=True)
        dcw_ref[2:3, :] += jnp.sum(dy * u, axis=0, keepdims=True)

    def half(c):
        return pl.BlockSpec((tm, WIDTH), lambda i: (i, c // WIDTH))

    def prev(c):
        return pl.BlockSpec((HALO, WIDTH), lambda i: (jnp.maximum(i * hb - 1, 0), c // WIDTH))

    def nxt(c):
        return pl.BlockSpec((HALO, WIDTH), lambda i: (jnp.minimum((i + 1) * hb, last_blk), c // WIDTH))

    return pl.pallas_call(
        body, name="conv_bwd", grid=(t // tm,),
        in_specs=[half(C_CB), half(C_CC), half(C_CH), prev(C_CC), prev(C_CH),
                  pl.BlockSpec((tm, WIDTH), lambda i: (i, 0)), nxt(C_CB), nxt(0), _resident((CONV_K, WIDTH))],
        out_specs=[pl.BlockSpec((tm, 3 * WIDTH), lambda i: (i, 0)), _resident((CONV_K, WIDTH))],
        out_shape=[jax.ShapeDtypeStruct((t, 3 * WIDTH), BF16), jax.ShapeDtypeStruct((CONV_K, WIDTH), F32)],
        compiler_params=_params("arbitrary"),
    )(proj, proj, proj, proj, proj, dya, proj, dya, conv_w)


def _loss_head(y, target, *, tm):
    t = y.shape[0]

    def body(y_ref, t_ref, dy_ref, l_ref):
        @pl.when(pl.program_id(0) == 0)
        def _():
            l_ref[...] = jnp.zeros_like(l_ref)

        err = y_ref[...] - t_ref[...]
        dy_ref[...] = err * (1.0 / D_MODEL)
        per_row = jnp.sum(err * err, axis=-1, keepdims=True) * (1.0 / D_MODEL)
        l_ref[...] += 0.5 * jnp.sum(per_row, axis=0, keepdims=True)

    row = pl.BlockSpec((tm, D_MODEL), lambda i: (i, 0))
    return pl.pallas_call(
        body, name="loss_head", grid=(t // tm,),
        in_specs=[row, row], out_specs=[row, _resident((8, 128))],
        out_shape=[jax.ShapeDtypeStruct((t, D_MODEL), F32), jax.ShapeDtypeStruct((8, 128), F32)],
        compiler_params=_params("arbitrary"),
    )(y, target)


def _lb_softmax(lower_bounds):
    x = lower_bounds
    e = jnp.exp(x - jnp.max(x, axis=0, keepdims=True))
    return e / jnp.sum(e, axis=0, keepdims=True)


def _lb_fwd(lower_bounds):
    def body(x_ref, o_ref):
        s = _lb_softmax(x_ref[...])
        c = s[0:1]
        o_ref[0:1, :] = c - s[0:1]
        for l in range(1, DEPTH):
            c = c + s[l:l + 1]
            o_ref[l:l + 1, :] = c - s[0:1]

    return pl.pallas_call(body, name="lb_fwd", out_shape=jax.ShapeDtypeStruct(lower_bounds.shape, F32))(lower_bounds)


def _lb_bwd(lower_bounds, d_lb_all):
    def body(x_ref, d_ref, o_ref):
        s = _lb_softmax(x_ref[...])
        d = d_ref[...]
        rows = [jnp.zeros_like(d[0:1])]
        for j in range(1, DEPTH):
            acc = d[j:j + 1]
            for l in range(j + 1, DEPTH):
                acc = acc + d[l:l + 1]
            rows.append(acc)
        inner = rows[0] * s[0:1]
        for j in range(1, DEPTH):
            inner = inner + rows[j] * s[j:j + 1]
        for j in range(DEPTH):
            o_ref[j:j + 1, :] = s[j:j + 1] * (rows[j] - inner)

    return pl.pallas_call(body, name="lb_bwd", out_shape=jax.ShapeDtypeStruct(lower_bounds.shape, F32))(
        lower_bounds, d_lb_all)


def _adamw(w, g, m, v):
    m2 = ADAM_B1 * m + (1.0 - ADAM_B1) * g
    v2 = ADAM_B2 * v + (1.0 - ADAM_B2) * (g * g)
    m_hat = m2 / (1.0 - ADAM_B1 ** ADAM_STEP)
    v_hat = v2 / (1.0 - ADAM_B2 ** ADAM_STEP)
    delta = -ADAM_LR * (m_hat / (jnp.sqrt(v_hat) + ADAM_EPS) + ADAM_WD * w)
    return delta, m2, v2


def _adam_small(name, g, w, m, v):
    shape = w.shape
    flat = (-1, shape[-1])
    g2, w2, m2, v2 = (a.reshape(flat) for a in (g, w, m, v))

    def body(g_ref, w_ref, m_ref, v_ref, d_ref, mo_ref, vo_ref):
        d, mm, vv = _adamw(w_ref[...], g_ref[...], m_ref[...], v_ref[...])
        d_ref[...] = d
        mo_ref[...] = mm
        vo_ref[...] = vv

    outs = pl.pallas_call(body, name=name, out_shape=[jax.ShapeDtypeStruct(w2.shape, F32)] * 3)(g2, w2, m2, v2)
    return [o.reshape(shape) for o in outs]


def _adam_shard(name, recvs, w, m, v, *, tr):
    _, r, c = w.shape

    def body(*refs):
        rc, (w_ref, m_ref, v_ref), (g_ref, d_ref, mo_ref, vo_ref) = refs[:DEPTH], refs[DEPTH:DEPTH + 3], refs[DEPTH + 3:]
        layer = pl.program_id(0)
        for cand in range(DEPTH):
            @pl.when(layer == cand)
            def _():
                g = rc[cand][0].astype(F32)
                for d in range(1, N_DEV):
                    g = g + rc[cand][d].astype(F32)
                dl, mm, vv = _adamw(w_ref[...], g, m_ref[...], v_ref[...])
                g_ref[...] = g
                d_ref[...] = dl
                mo_ref[...] = mm
                vo_ref[...] = vv

    def recv_spec(cand):
        return pl.BlockSpec((N_DEV, tr, c), lambda l, i: (0, jnp.where(l == cand, i, 0), 0))

    tile = pl.BlockSpec((None, tr, c), lambda l, i: (l, i, 0))
    return pl.pallas_call(
        body, name=name, grid=(DEPTH, r // tr),
        in_specs=[recv_spec(cand) for cand in range(DEPTH)] + [tile] * 3,
        out_specs=[tile] * 4,
        out_shape=[jax.ShapeDtypeStruct(w.shape, F32)] * 4,
        compiler_params=_params("parallel", "parallel"),
    )(*recvs, w, m, v)


def _sum_devices(name, x):
    def body(x_ref, o_ref):
        acc = x_ref[0]
        for d in range(1, N_DEV):
            acc = acc + x_ref[d]
        o_ref[...] = acc

    return pl.pallas_call(body, name=name, out_shape=jax.ShapeDtypeStruct(x.shape[1:], x.dtype))(x)


SMALL = (("lower_bounds", 512), ("conv_w", CONV_K * WIDTH), ("hg_norm_w", HEAD_DIM), ("b_gate", 3 * D_MODEL),
         ("ln1_g", D_MODEL), ("ln1_b", D_MODEL), ("ln2_g", D_MODEL), ("ln2_b", D_MODEL))
SMALL_PER_LAYER = sum(n for _, n in SMALL)
SMALL_ROWS = 296


def _natural_cols(g):
    nd = g.ndim
    perm = tuple(range(1, nd - 1)) + (0, nd - 1)
    t = jnp.transpose(g, perm)
    return t.reshape(t.shape[:-2] + (t.shape[-2] * t.shape[-1],))


def _natural_rows(g):
    return g.reshape(g.shape[0] * g.shape[1], g.shape[2])


def _hosted(hosts, key, fn):
    if not hosts or key not in hosts:
        return fn(None)
    ex, hook = hosts[key]
    outs, recv = fn(ex)
    hook(recv)
    return outs


def _layer_fwd(cur, cur_b, mem2d, wl, *, bsz, seq, hosts=None):
    tm = min(512, seq)
    proj = _hosted(hosts, "in_proj", lambda c: _mm_nn("in_proj", cur_b, wl["w_in"], tm=min(1024, seq), tn=1024,
                                                       out_dtype=F32, carry=c))
    y_b, o_pre, states = _hosted(hosts, "hgrn_fwd", lambda c: _hgrn_fwd(proj, wl["lb"], wl["nw"], bsz=bsz, seq=seq,
                                                                         carry=c))
    mk, mv = _mem_kv(mem2d, wl["w_mk"], wl["w_mv"])
    y_c = _attn_fwd(proj, mk, mv, tm=tm, seq=seq)
    y_a, pa, pb, pc, merged = _merge_fwd(proj, y_b, y_c, wl["conv"], wl["w_br"], wl["b_gate"], tm=tm, seq=seq)
    z1, x1, x1_b = _linear_ln("wo_ln", merged, wl["w_o"], cur, wl["ln1_g"], wl["ln1_b"], tm=tm)
    hid = _hosted(hosts, "mlp_up", lambda c: _mm_nn("mlp_up", x1_b, wl["w_up"], tm=tm, tn=1024, out_dtype=BF16,
                                                     relu2=True, carry=c))
    z2, x2, x2_b = _linear_ln("down_ln", hid, wl["w_down"], x1, wl["ln2_g"], wl["ln2_b"], tm=tm)
    return dict(x_b=cur_b, proj=proj, y_a=y_a, y_b=y_b, y_c=y_c, o_pre=o_pre, states=states, mk=mk, mv=mv,
                pa=pa, pb=pb, pc=pc, merged=merged, z1=z1, x1_b=x1_b, hid=hid, z2=z2, x2=x2, x2_b=x2_b)


def _layer_bwd(dcur, mem2d, s, wl, *, bsz, seq, hosts=None, early=None, late=None):
    tm = min(512, seq)
    tk = min(1024, bsz * seq)
    dz2, dz2_b, dhpre, d_ln2g, d_ln2b = _hosted(
        hosts, "ln2_bwd_down", lambda c: _ln_bwd_mm_nt("ln2_bwd_down", dcur, s["z2"], wl["ln2_g"], wl["w_down"],
                                                       s["hid"], tm=tm, tn=1024, carry=c))
    g_down = _mm_tn("grad_w_down", s["hid"], dz2_b, tk=tk, tmo=1024, tno=1024)
    dx1 = _mm_nt_sum("mlp_up_bwd", [dhpre], [0], wl["w_up"], dz2, tm=tm)
    g_up = _mm_tn("grad_w_up", s["x1_b"], dhpre, tk=tk, tmo=1024, tno=1024)
    dz1, dz1_b, dmerged, d_ln1g, d_ln1b = _ln_bwd_mm_nt("ln1_bwd_wo", dx1, s["z1"], wl["ln1_g"], wl["w_o"],
                                                        tm=tm, tn=1024)
    g_o = _mm_tn("grad_w_o", s["merged"], dz1_b, tk=tk, tmo=1024, tno=1024)
    dgate, dpa, dpb, dpc, dya, dyb, dyc, d_bg = _merge_bwd(s["proj"], dmerged, s["pa"], s["pb"], s["pc"],
                                                           wl["w_br"], wl["b_gate"], tm=tm)
    g_br = jnp.stack([_mm_tn("grad_w_branch", yy, dp, tk=tk, tmo=512, tno=1024)
                      for yy, dp in ((s["y_a"], dpa), (s["y_b"], dpb), (s["y_c"], dpc))])
    d_conv, d_cw = _conv_bwd(s["proj"], dya, wl["conv"], tm=tm, seq=seq)
    hg_hosts = {"hgrn_bwd": early(g_br, g_o, g_up, g_down)} if early is not None else None
    dq, df, di, dg, d_lb, d_nw = _hosted(
        hg_hosts, "hgrn_bwd", lambda c: _hgrn_bwd(s["proj"], wl["lb"], wl["nw"], s["o_pre"], s["states"], dyb,
                                                  bsz=bsz, seq=seq, carry=c))
    dmq, dmk, dmv = _attn_bwd(s["proj"], s["mk"], s["mv"], dyc, tm=tm, seq=seq)
    tkm = min(512, mem2d.shape[0])
    g_mk = _mm_tn("grad_w_mem", mem2d, dmk, tk=tkm, tmo=1024, tno=512)
    g_mv = _mm_tn("grad_w_mem", mem2d, dmv, tk=tkm, tmo=1024, tno=512)
    pieces = [d_conv, dq, df, di, dg, dmq, dgate]
    offsets = [C_CB, C_HQ, C_HF, C_HI, C_HG, C_MQ, C_GA]
    g_in = jnp.concatenate(
        [_mm_tn("grad_w_in_%d" % p.shape[1], s["x_b"], p, tk=tk, tmo=1024,
                tno=(1024 if p.shape[1] % 1024 == 0 else 512)) for p in pieces], axis=1)
    dx_hosts = {"in_proj_bwd": late(g_in, g_mk, g_mv)} if late is not None else None
    dx = _hosted(dx_hosts, "in_proj_bwd", lambda c: _mm_nt_sum("in_proj_bwd", pieces, offsets, wl["w_in"], dz1,
                                                               tm=min(256, seq), carry=c))
    small = jnp.concatenate([d_lb[0], d_cw.reshape(-1), d_nw[0], d_bg[0], d_ln1g[0], d_ln1b[0], d_ln2g[0],
                             d_ln2b[0]])
    return dx, [g_in, g_mk, g_mv, g_br, g_o, g_up, g_down], small


def kernel(x, mem, lower_bounds, w_in, conv_w, hg_norm_w, w_mem_k, w_mem_v, w_branch, b_gate, w_o, ln1_g, ln1_b, w_up, w_down, ln2_g, ln2_b, loss_target, m_lower_bounds, m_w_in, m_conv_w, m_hg_norm_w, m_w_mem_k, m_w_mem_v, m_w_branch, m_b_gate, m_w_o, m_ln1_g, m_ln1_b, m_w_up, m_w_down, m_ln2_g, m_ln2_b, v_lower_bounds, v_w_in, v_conv_w, v_hg_norm_w, v_w_mem_k, v_w_mem_v, v_w_branch, v_b_gate, v_w_o, v_ln1_g, v_ln1_b, v_w_up, v_w_down, v_ln2_g, v_ln2_b):
    bsz, seq, _ = x.shape
    t = bsz * seq
    me = _my_id()

    sh = dict(w_in=w_in.astype(BF16), w_mk=w_mem_k.astype(BF16), w_mv=w_mem_v.astype(BF16),
              w_br=w_branch.astype(BF16), w_o=w_o.astype(BF16), w_up=w_up.astype(BF16), w_down=w_down.astype(BF16))
    natural = dict(w_in=_natural_cols, w_mk=_natural_rows, w_mv=_natural_rows, w_br=_natural_cols,
                   w_o=_natural_rows, w_up=_natural_cols, w_down=_natural_rows)

    def gather_of(names, l):
        srcs = [sh[n][l] for n in names]
        return _Exchange(srcs, [_whole] * len(srcs), [s_.shape for s_ in srcs])

    def put(names, into):
        def hook(recv_):
            for n, r in zip(names, recv_):
                into[n] = natural[n](r)
        return hook

    lb_all = _lb_fwd(lower_bounds)
    layer_w = [dict(lb=lb_all[l][None], nw=hg_norm_w[l][None], b_gate=b_gate[l][None], ln1_g=ln1_g[l][None],
                    ln1_b=ln1_b[l][None], ln2_g=ln2_g[l][None], ln2_b=ln2_b[l][None]) for l in range(DEPTH)]
    first = ["w_in", "w_mk", "w_mv", "w_br", "w_o"]
    conv_shard = conv_w.reshape(DEPTH * CONV_K * (WIDTH // N_DEV) // 128, 128)
    ex0 = gather_of(first, 0)
    got = _exchange("gather_first", ex0.srcs + [conv_shard], [_whole] * (len(first) + 1),
                    [s_.shape for s_ in ex0.srcs] + [conv_shard.shape])
    put(first, layer_w[0])(got[:len(first)])
    conv_full = _natural_cols(got[-1].reshape(N_DEV, DEPTH, CONV_K, WIDTH // N_DEV))

    x2d = x.reshape(t, D_MODEL)
    mem2d = mem.reshape(bsz * MEM_LEN, D_MODEL)
    target2d = loss_target.reshape(t, D_MODEL)

    saved = []
    cur, cur_b = x2d, x2d.astype(BF16)
    for l in range(DEPTH):
        wl = layer_w[l]
        wl["conv"] = conv_full[l]
        hosts = {"in_proj": (gather_of(["w_up", "w_down"], l), put(["w_up", "w_down"], wl))}
        if l + 1 < DEPTH:
            hosts["hgrn_fwd"] = (gather_of(["w_in"], l + 1), put(["w_in"], layer_w[l + 1]))
            rest = ["w_mk", "w_mv", "w_br", "w_o"]
            hosts["mlp_up"] = (gather_of(rest, l + 1), put(rest, layer_w[l + 1]))
        s = _layer_fwd(cur, cur_b, mem2d, wl, bsz=bsz, seq=seq, hosts=hosts)
        saved.append(s)
        cur, cur_b = s["x2"], s["x2_b"]

    dcur, loss_tile = _loss_head(cur, target2d, tm=min(512, seq))
    loss = lax.psum(loss_tile[0, 0], ("x", "y", "c"))

    slicer = dict(w_in=_cols(IN_COLS // N_DEV), w_mk=_rows(D_MODEL // N_DEV), w_mv=_rows(D_MODEL // N_DEV),
                  w_br=_cols(D_MODEL // N_DEV), w_o=_rows(D_MODEL // N_DEV), w_up=_cols(D_FF // N_DEV),
                  w_down=_rows(D_FF // N_DEV))
    recv = [dict() for _ in range(DEPTH)]

    def scatter_of(names, grads_):
        return _Exchange(grads_, [slicer[n] for n in names], [sh[n].shape[1:] for n in names])

    def keep(names, into):
        def hook(recv_):
            into.update(zip(names, recv_))
        return hook

    early_names = ["w_br", "w_o", "w_up", "w_down"]
    late_names = ["w_in", "w_mk", "w_mv"]
    small_rows = [None] * DEPTH
    late = None
    for l in reversed(range(DEPTH)):
        hosts = None
        if late is not None:
            hosts = {"ln2_bwd_down": (scatter_of(late_names, late), keep(late_names, recv[l + 1]))}
        last = (lambda *g: (scatter_of(late_names, list(g)), keep(late_names, recv[0]))) if l == 0 else None
        dcur, big_grads, small_rows[l] = _layer_bwd(
            dcur, mem2d, saved[l], layer_w[l], bsz=bsz, seq=seq, hosts=hosts,
            early=lambda *g, l=l: (scatter_of(early_names, list(g)), keep(early_names, recv[l])), late=last)
        late = big_grads[:3]

    packed = jnp.concatenate(small_rows + [jnp.zeros((SMALL_ROWS * 128 - DEPTH * SMALL_PER_LAYER,), F32)])
    packed = packed.reshape(SMALL_ROWS, 128)
    all_small = _exchange("gather_small_grads", [packed], [_whole], [packed.shape])[0]
    summed = _sum_devices("sum_small_grads", all_small).reshape(-1)[:DEPTH * SMALL_PER_LAYER]
    summed = summed.reshape(DEPTH, SMALL_PER_LAYER)
    small_grads = {}
    off = 0
    for name, n in SMALL:
        small_grads[name] = summed[:, off:off + n]
        off += n
    small_grads["lower_bounds"] = _lb_bwd(lower_bounds, small_grads["lower_bounds"])
    conv_all = small_grads["conv_w"].reshape(DEPTH, CONV_K, WIDTH)
    small_grads["conv_w"] = lax.dynamic_slice_in_dim(conv_all, me * (WIDTH // N_DEV), WIDTH // N_DEV, axis=2)

    grads, deltas, new_m, new_v = {}, {}, {}, {}
    given = dict(lower_bounds=(lower_bounds, m_lower_bounds, v_lower_bounds), conv_w=(conv_w, m_conv_w, v_conv_w),
                 hg_norm_w=(hg_norm_w, m_hg_norm_w, v_hg_norm_w), b_gate=(b_gate, m_b_gate, v_b_gate),
                 ln1_g=(ln1_g, m_ln1_g, v_ln1_g), ln1_b=(ln1_b, m_ln1_b, v_ln1_b),
                 ln2_g=(ln2_g, m_ln2_g, v_ln2_g), ln2_b=(ln2_b, m_ln2_b, v_ln2_b))
    for name, (w_, m_, v_) in given.items():
        g_ = small_grads[name].reshape(w_.shape)
        grads[name] = g_
        deltas[name], new_m[name], new_v[name] = _adam_small("adam_" + name, g_, w_, m_, v_)

    big = dict(w_in=("w_in", w_in, m_w_in, v_w_in, 128), w_mem_k=("w_mk", w_mem_k, m_w_mem_k, v_w_mem_k, 128),
               w_mem_v=("w_mv", w_mem_v, m_w_mem_v, v_w_mem_v, 128),
               w_branch=("w_br", w_branch, m_w_branch, v_w_branch, 512), w_o=("w_o", w_o, m_w_o, v_w_o, 128),
               w_up=("w_up", w_up, m_w_up, v_w_up, 256), w_down=("w_down", w_down, m_w_down, v_w_down, 128))
    for name, (k, w_, m_, v_, tr) in big.items():
        shape = w_.shape
        flat = (DEPTH, -1, shape[-1])
        rc = [recv[l][k].reshape((N_DEV,) + w_.reshape(flat).shape[1:]) for l in range(DEPTH)]
        outs = _adam_shard("adam_" + name, rc, w_.reshape(flat), m_.reshape(flat), v_.reshape(flat), tr=tr)
        grads[name], deltas[name], new_m[name], new_v[name] = (o.reshape(shape) for o in outs)

    order = ["lower_bounds", "w_in", "conv_w", "hg_norm_w", "w_mem_k", "w_mem_v", "w_branch", "b_gate", "w_o",
             "ln1_g", "ln1_b", "w_up", "w_down", "ln2_g", "ln2_b"]
    return (loss, dcur.reshape(x.shape), *[grads[n] for n in order], *[deltas[n] for n in order],
            *[new_m[n] for n in order], *[new_v[n] for n in order])
```

```python
import functools

import jax
import jax.numpy as jnp
from jax import lax
from jax.experimental import pallas as pl
from jax.experimental.pallas import tpu as pltpu

F32 = jnp.float32
BF16 = jnp.bfloat16

N_DEV = 8
D_MODEL = 1024
DEPTH = 4
MEM_LEN = 256
CONV_K = 3
WIDTH = 512
HEADS = 4
HEAD_DIM = 128
CHUNK = 32
D_FF = 4 * D_MODEL
IN_COLS = 7168
ALPHA = (2.0 * DEPTH) ** 0.25
LN_EPS = 1e-5
RMS_EPS = 1e-6
ADAM_LR = 0.001
ADAM_B1 = 0.9
ADAM_B2 = 0.999
ADAM_EPS = 1e-08
ADAM_WD = 0.01
ADAM_STEP = 10

C_CB, C_CC, C_CH, C_HQ, C_HF, C_HI, C_HG, C_MQ, C_GA = 0, 512, 1024, 1536, 2048, 2560, 3072, 3584, 4096

ROWS_HG = 256
NT_DIMS = (((1,), (1,)), ((), ()))
TN_DIMS = (((0,), (0,)), ((), ()))
MESH = pl.DeviceIdType.MESH


def _dot(a, b):
    return jnp.dot(a, b, preferred_element_type=F32)


def _dot_nt(a, b):
    return lax.dot_general(a, b, NT_DIMS, preferred_element_type=F32)


def _dot_tn(a, b):
    return lax.dot_general(a, b, TN_DIMS, preferred_element_type=F32)


def _sigmoid(x):
    return 1.0 / (1.0 + jnp.exp(-x))


def _params(*sem):
    return pltpu.CompilerParams(dimension_semantics=sem)


def _resident(shape):
    nd = len(shape)
    return pl.BlockSpec(shape, lambda *_: (0,) * nd)


def _my_id():
    return 4 * lax.axis_index("x") + 2 * lax.axis_index("y") + lax.axis_index("c")


class _Exchange:
    def __init__(self, srcs, slicers, piece_shapes):
        self.srcs, self.slicers, self.n = list(srcs), list(slicers), len(srcs)
        any_spec = pl.BlockSpec(memory_space=pl.ANY)
        self.in_specs = [any_spec] * self.n
        self.out_specs = [any_spec] * self.n
        self.out_shape = [jax.ShapeDtypeStruct((N_DEV,) + tuple(s), a.dtype) for s, a in zip(piece_shapes, srcs)]
        self.scratch = [pltpu.SemaphoreType.DMA((self.n * N_DEV,)), pltpu.SemaphoreType.DMA((self.n * N_DEV,)),
                        pltpu.SemaphoreType.DMA((self.n,))]

    def _remote(self, ins, outs, sems, k, j, me):
        return pltpu.make_async_remote_copy(
            src_ref=self.slicers[k](ins[k], j), dst_ref=outs[k].at[me],
            send_sem=sems[0].at[k * N_DEV + j], recv_sem=sems[1].at[k * N_DEV + me],
            device_id=(j // 4, (j // 2) % 2, j % 2), device_id_type=MESH)

    def _local(self, ins, outs, sems, k, j, me):
        return pltpu.make_async_copy(self.slicers[k](ins[k], j), outs[k].at[me], sems[2].at[k])

    def start(self, ins, outs, sems):
        me = _my_id()
        for k in range(self.n):
            for j in range(N_DEV):
                @pl.when(j != me)
                def _():
                    self._remote(ins, outs, sems, k, j, me).start()

                @pl.when(j == me)
                def _():
                    self._local(ins, outs, sems, k, j, me).start()

    def wait(self, ins, outs, sems):
        me = _my_id()
        for k in range(self.n):
            for j in range(N_DEV):
                @pl.when(j != me)
                def _():
                    pltpu.make_async_remote_copy(
                        src_ref=self.slicers[k](ins[k], j), dst_ref=outs[k].at[j],
                        send_sem=sems[0].at[k * N_DEV + j], recv_sem=sems[1].at[k * N_DEV + j],
                        device_id=(j // 4, (j // 2) % 2, j % 2), device_id_type=MESH).wait_recv()
                    self._remote(ins, outs, sems, k, j, me).wait_send()

                @pl.when(j == me)
                def _():
                    self._local(ins, outs, sems, k, j, me).wait()


def _exchange(name, srcs, slicers, piece_shapes):
    ex = _Exchange(srcs, slicers, piece_shapes)

    def body(*refs):
        ins, outs, sems = refs[:ex.n], refs[ex.n:2 * ex.n], refs[2 * ex.n:]
        ex.start(ins, outs, sems)
        ex.wait(ins, outs, sems)

    return pl.pallas_call(
        body, name=name, in_specs=ex.in_specs, out_specs=ex.out_specs, out_shape=ex.out_shape,
        scratch_shapes=ex.scratch, compiler_params=pltpu.CompilerParams(has_side_effects=True),
    )(*ex.srcs)


def _call(body, name, grid, in_specs, out_specs, out_shape, args, scratch=(), sem=None, carry=None):
    n_in, n_out, n_scr = len(in_specs), len(out_specs), len(scratch)
    if carry is None:
        outs = pl.pallas_call(body, name=name, grid=grid, in_specs=in_specs, out_specs=out_specs,
                              out_shape=out_shape, scratch_shapes=list(scratch),
                              compiler_params=_params(*sem))(*args)
        return outs, None
    nc = carry.n

    def hosted(*refs):
        ins, c_in = refs[:n_in], refs[n_in:n_in + nc]
        outs = refs[n_in + nc:n_in + nc + n_out]
        c_out = refs[n_in + nc + n_out:n_in + 2 * nc + n_out]
        rest = refs[n_in + 2 * nc + n_out:]
        scr, sems = rest[:n_scr], rest[n_scr:]
        first, last = True, True
        for d, size in enumerate(grid):
            first = first & (pl.program_id(d) == 0)
            last = last & (pl.program_id(d) == size - 1)

        @pl.when(first)
        def _():
            carry.start(c_in, c_out, sems)

        body(*ins, *outs, *scr)

        @pl.when(last)
        def _():
            carry.wait(c_in, c_out, sems)

    outs = pl.pallas_call(
        hosted, name=name + "_x", grid=grid, in_specs=list(in_specs) + carry.in_specs,
        out_specs=list(out_specs) + carry.out_specs, out_shape=list(out_shape) + carry.out_shape,
        scratch_shapes=list(scratch) + carry.scratch,
        compiler_params=_params(*(["arbitrary"] * len(grid))))(*args, *carry.srcs)
    return outs[:n_out], outs[n_out:]


def _whole(ref, j):
    return ref


def _cols(width):
    return lambda ref, j: ref.at[(slice(None),) * (len(ref.shape) - 1) + (pl.ds(j * width, width),)]


def _rows(height):
    return lambda ref, j: ref.at[pl.ds(j * height, height)]


def _mm_nn(name, a, w, *, tm, tn, out_dtype, relu2=False, carry=None):
    t, k = a.shape
    n = w.shape[1]

    def body(a_ref, w_ref, o_ref):
        acc = _dot(a_ref[...].astype(BF16), w_ref[...])
        if relu2:
            r = jnp.maximum(acc, 0.0)
            acc = r * r
        o_ref[...] = acc.astype(out_dtype)

    outs, recv = _call(
        body, name, (t // tm, n // tn),
        [pl.BlockSpec((tm, k), lambda i, j: (i, 0)), pl.BlockSpec((k, tn), lambda i, j: (0, j))],
        [pl.BlockSpec((tm, tn), lambda i, j: (i, j))], [jax.ShapeDtypeStruct((t, n), out_dtype)], (a, w),
        sem=("parallel", "parallel"), carry=carry)
    return outs[0] if carry is None else (outs[0], recv)


def _layer_norm(z, g, b):
    mu = jnp.mean(z, axis=-1, keepdims=True)
    zc = z - mu
    var = jnp.mean(zc * zc, axis=-1, keepdims=True)
    return zc * lax.rsqrt(var + LN_EPS) * g + b


def _linear_ln(name, a, w, resid, g, b, *, tm):
    t, k = a.shape

    def body(a_ref, w_ref, r_ref, g_ref, b_ref, z_ref, x_ref, xb_ref):
        z = ALPHA * r_ref[...] + _dot(a_ref[...], w_ref[...])
        z_ref[...] = z
        y = _layer_norm(z, g_ref[...], b_ref[...])
        x_ref[...] = y
        xb_ref[...] = y.astype(BF16)

    row = pl.BlockSpec((tm, D_MODEL), lambda i: (i, 0))
    return pl.pallas_call(
        body, name=name, grid=(t // tm,),
        in_specs=[pl.BlockSpec((tm, k), lambda i: (i, 0)), _resident((k, D_MODEL)), row,
                  _resident((1, D_MODEL)), _resident((1, D_MODEL))],
        out_specs=[row, row, row],
        out_shape=[jax.ShapeDtypeStruct((t, D_MODEL), F32)] * 2 + [jax.ShapeDtypeStruct((t, D_MODEL), BF16)],
        compiler_params=_params("parallel"),
    )(a, w, resid, g, b)


def _ln_bwd_mm_nt(name, dy, z, g, w, h=None, *, tm, tn, carry=None):
    t = dy.shape[0]
    n = w.shape[0]
    halves = [slice(0, tm // 2), slice(tm // 2, tm)]

    def body(*refs):
        if h is None:
            dy_ref, z_ref, g_ref, w_ref, dz_ref, dzb_ref, o_ref, dg_ref, db_ref = refs
        else:
            dy_ref, z_ref, g_ref, w_ref, h_ref, dz_ref, dzb_ref, o_ref, dg_ref, db_ref = refs

        @pl.when(pl.program_id(0) == 0)
        def _():
            dg_ref[...] = jnp.zeros_like(dg_ref)
            db_ref[...] = jnp.zeros_like(db_ref)

        zv = _Lanes(z_ref[s, :] for s in halves)
        dyv = _Lanes(dy_ref[s, :] for s in halves)
        mu = _mean(zv, axis=-1, keepdims=True)
        zc = zv - mu
        rstd = _rsqrt(_mean(zc * zc, axis=-1, keepdims=True) + LN_EPS)
        xh = zc * rstd
        gdy = dyv * g_ref[...]
        m1 = _mean(gdy, axis=-1, keepdims=True)
        m2 = _mean(gdy * xh, axis=-1, keepdims=True)
        dz = rstd * (gdy - m1 - xh * m2)
        dz_b = dz.astype(BF16)
        for s, a, a_b in zip(halves, dz.xs, dz_b.xs):
            dz_ref[s, :] = a
            dzb_ref[s, :] = a_b
        dg_ref[...] += _sum(dyv * xh, axis=0, keepdims=True).total()
        db_ref[...] += _sum(dyv, axis=0, keepdims=True).total()
        for c in range(n // tn):
            cols = slice(c * tn, (c + 1) * tn)
            acc = _ldot_nt(dz_b, w_ref[cols, :])
            if h is not None:
                acc = acc * (2.0 * _sqrt(_Lanes(h_ref[s, cols] for s in halves).astype(F32)))
            for s, a in zip(halves, acc.xs):
                o_ref[s, cols] = a.astype(BF16)

    row = pl.BlockSpec((tm, D_MODEL), lambda i: (i, 0))
    vec = _resident((1, D_MODEL))
    tile = pl.BlockSpec((tm, n), lambda i: (i, 0))
    in_specs = [row, row, vec, _resident((n, D_MODEL))]
    args = [dy, z, g, w]
    if h is not None:
        in_specs.append(tile)
        args.append(h)
    outs, recv = _call(
        body, name, (t // tm,), in_specs, [row, row, tile, vec, vec],
        [jax.ShapeDtypeStruct((t, D_MODEL), F32), jax.ShapeDtypeStruct((t, D_MODEL), BF16),
         jax.ShapeDtypeStruct((t, n), BF16), jax.ShapeDtypeStruct((1, D_MODEL), F32),
         jax.ShapeDtypeStruct((1, D_MODEL), F32)], args, sem=("arbitrary",), carry=carry)
    return outs if carry is None else (outs, recv)


def _mm_tn(name, a, b, *, tk, tmo, tno):
    t, m = a.shape
    n = b.shape[1]
    nk = t // tk

    def body(a_ref, b_ref, o_ref, acc_ref):
        k = pl.program_id(2)
        p = _dot_tn(a_ref[...].astype(BF16), b_ref[...].astype(BF16))

        @pl.when(k == 0)
        def _():
            acc_ref[...] = p

        @pl.when(k > 0)
        def _():
            acc_ref[...] += p

        @pl.when(k == nk - 1)
        def _():
            o_ref[...] = acc_ref[...].astype(BF16)

    return pl.pallas_call(
        body, name=name, grid=(m // tmo, n // tno, nk),
        in_specs=[pl.BlockSpec((tk, tmo), lambda i, j, k: (k, i)), pl.BlockSpec((tk, tno), lambda i, j, k: (k, j))],
        out_specs=pl.BlockSpec((tmo, tno), lambda i, j, k: (i, j)),
        out_shape=jax.ShapeDtypeStruct((m, n), BF16),
        scratch_shapes=[pltpu.VMEM((tmo, tno), F32)],
        compiler_params=_params("parallel", "parallel", "arbitrary"),
    )(a, b)


def _mm_nt_sum(name, pieces, offsets, w, resid, *, tm, carry=None):
    t = resid.shape[0]
    widths = [p.shape[1] for p in pieces]
    n_p = len(pieces)

    def body(*refs):
        p_refs, w_ref, r_ref, o_ref = refs[:n_p], refs[n_p], refs[n_p + 1], refs[n_p + 2]
        acc = ALPHA * r_ref[...]
        for p_ref, off, wd in zip(p_refs, offsets, widths):
            acc = acc + _dot_nt(p_ref[...], w_ref[:, off:off + wd])
        o_ref[...] = acc

    row = pl.BlockSpec((tm, D_MODEL), lambda i: (i, 0))
    outs, recv = _call(
        body, name, (t // tm,),
        [pl.BlockSpec((tm, wd), lambda i: (i, 0)) for wd in widths] + [_resident(w.shape), row],
        [row], [jax.ShapeDtypeStruct((t, D_MODEL), F32)], (*pieces, w, resid), sem=("parallel",), carry=carry)
    return outs[0] if carry is None else (outs[0], recv)


def _chunk_mask(rows):
    r = lax.broadcasted_iota(jnp.int32, (rows, rows), 0)
    c = lax.broadcasted_iota(jnp.int32, (rows, rows), 1)
    return ((r // CHUNK) == (c // CHUNK)) & (c <= r)


class _Lanes:
    def __init__(self, xs):
        self.xs = list(xs)

    def _with(self, other, f):
        if isinstance(other, _Lanes):
            return _Lanes([f(a, b) for a, b in zip(self.xs, other.xs)])
        return _Lanes([f(a, other) for a in self.xs])

    def __add__(self, o):
        return self._with(o, lambda a, b: a + b)

    def __radd__(self, o):
        return self._with(o, lambda a, b: b + a)

    def __sub__(self, o):
        return self._with(o, lambda a, b: a - b)

    def __rsub__(self, o):
        return self._with(o, lambda a, b: b - a)

    def __mul__(self, o):
        return self._with(o, lambda a, b: a * b)

    def __rmul__(self, o):
        return self._with(o, lambda a, b: b * a)

    def __truediv__(self, o):
        return self._with(o, lambda a, b: a / b)

    def __rtruediv__(self, o):
        return self._with(o, lambda a, b: b / a)

    def __neg__(self):
        return _Lanes([-a for a in self.xs])

    def __ge__(self, o):
        return self._with(o, lambda a, b: a >= b)

    def __getitem__(self, idx):
        return _Lanes([a[idx] for a in self.xs])

    def astype(self, dtype):
        return _Lanes([a.astype(dtype) for a in self.xs])

    def total(self):
        return functools.reduce(lambda a, b: a + b, self.xs)


def _lift(f):
    def g(*args, **kw):
        lanes = [a for a in args if isinstance(a, _Lanes)]
        if not lanes:
            return f(*args, **kw)
        return _Lanes([f(*[a.xs[i] if isinstance(a, _Lanes) else a for a in args], **kw)
                       for i in range(len(lanes[0].xs))])
    return g


def _concat(parts, axis):
    if isinstance(parts[0], _Lanes):
        return _Lanes([jnp.concatenate([p.xs[i] for p in parts], axis=axis) for i in range(len(parts[0].xs))])
    return jnp.concatenate(parts, axis=axis)


_exp, _log, _abs, _sqrt, _where = _lift(jnp.exp), _lift(jnp.log), _lift(jnp.abs), _lift(jnp.sqrt), _lift(jnp.where)
_sum, _mean, _rsqrt, _bcast = _lift(jnp.sum), _lift(jnp.mean), _lift(lax.rsqrt), _lift(jnp.broadcast_to)
_ldot, _ldot_nt, _ldot_tn = _lift(_dot), _lift(_dot_nt), _lift(_dot_tn)
_lsigmoid = _lift(_sigmoid)


def _mask_sum(mask_b, x, transpose=False):
    f = _ldot_tn if transpose else _ldot
    hi = x.astype(BF16)
    lo = (x - hi.astype(F32)).astype(BF16)
    return f(mask_b, hi) + f(mask_b, lo)


def _chunk_row(x, pos, rows):
    nc = rows // CHUNK

    def one(a):
        a3 = a.reshape(nc, CHUNK, HEAD_DIM)
        return jnp.broadcast_to(a3[:, pos:pos + 1, :], (nc, CHUNK, HEAD_DIM)).reshape(rows, HEAD_DIM)

    return _lift(one)(x)


def _chunk_total(x, rows):
    nc = rows // CHUNK

    def one(a):
        tot = jnp.sum(a.reshape(nc, CHUNK, HEAD_DIM), axis=1, keepdims=True)
        return jnp.broadcast_to(tot, (nc, CHUNK, HEAD_DIM)).reshape(rows, HEAD_DIM)

    return _lift(one)(x)


def _sigmoid_pair(x):
    e = _exp(-_abs(x))
    big = 1.0 / (1.0 + e)
    small = e * big
    pos = x >= 0.0
    return _where(pos, big, small), _where(pos, small, big)


def _hg_gates(q_raw, fl, lb, rows, mask):
    tri = mask.astype(BF16)
    sg, sg_neg = _sigmoid_pair(fl)
    forget = lb + (1.0 - lb) * sg
    k = (1.0 - lb) * sg_neg
    sq = _lsigmoid(q_raw)
    qs = q_raw * sq
    bc = _mask_sum(tri, _log(forget))
    bref = _chunk_row(bc, CHUNK // 2 - 1, rows)
    blast = _chunk_row(bc, CHUNK - 1, rows)
    return dict(tri=tri, sg=sg, sg_neg=sg_neg, forget=forget, k=k, sq=sq, qs=qs,
                e_a=_exp(bc - bref), e_b=_exp(bref - bc), e_q=_exp(bc), e_k=_exp(blast - bc),
                dec=_exp(blast))


HG_GROUP = 2


def _hg_lanes(bsz):
    return [(hh, slice(hh * HEAD_DIM, (hh + 1) * HEAD_DIM), b) for hh in range(HG_GROUP) for b in range(bsz)]


def _hg_read(ref, lanes):
    return _Lanes(ref[b, :, cs] for _, cs, b in lanes)


def _hg_write(ref, lanes, val):
    for (_, cs, b), a in zip(lanes, val.xs):
        ref[b, :, cs] = a


def _hgrn_fwd(proj, lb, nw, *, bsz, seq, carry=None):
    rows = min(ROWS_HG, seq)
    nt = seq // rows
    nc = rows // CHUNK
    t = bsz * seq

    lanes = _hg_lanes(bsz)

    def body(q_ref, f_ref, v_ref, g_ref, lb_ref, nw_ref, y_ref, o_ref, st_ref, s_scr):
        @pl.when(pl.program_id(1) == 0)
        def _():
            s_scr[...] = jnp.zeros_like(s_scr)

        mask = _chunk_mask(rows)
        lb_v = _Lanes(lb_ref[:, cs] for _, cs, _ in lanes)
        gt = _hg_gates(_hg_read(q_ref, lanes), _hg_read(f_ref, lanes), lb_v, rows, mask)
        v_b = _hg_read(v_ref, lanes).astype(BF16)
        a_b = (gt["qs"] * gt["e_a"]).astype(BF16)
        b_b = (gt["k"] * gt["e_b"]).astype(BF16)
        qi_b = (gt["qs"] * gt["e_q"]).astype(BF16)
        ko_b = (gt["k"] * gt["e_k"]).astype(BF16)
        scores = _where(mask, _ldot_nt(a_b, b_b), 0.0)
        o_intra = _ldot(scores.astype(BF16), v_b)

        s = _Lanes(s_scr[i] for i in range(len(lanes)))
        parts = []
        for n in range(nc):
            sl = slice(n * CHUNK, (n + 1) * CHUNK)
            s_b = s.astype(BF16)
            for (hh, _, b), a in zip(lanes, s_b.xs):
                st_ref[hh, b, n] = a
            parts.append(_ldot_nt(qi_b[sl], s_b))
            s = s * gt["dec"][n * CHUNK:n * CHUNK + 1] + _ldot_tn(v_b[sl], ko_b[sl])
        for i, a in enumerate(s.xs):
            s_scr[i] = a
        o = o_intra + _concat(parts, 0)
        _hg_write(o_ref, lanes, o)
        r = _rsqrt(_mean(o * o, axis=-1, keepdims=True) + RMS_EPS)
        g = _hg_read(g_ref, lanes)
        _hg_write(y_ref, lanes, (o * r * nw_ref[...] * (g * _lsigmoid(g))).astype(BF16))

    wide = HG_GROUP * HEAD_DIM

    def col(base):
        return pl.BlockSpec((bsz, rows, wide), lambda h, j: (0, j, base // wide + h))

    out_tile = pl.BlockSpec((bsz, rows, wide), lambda h, j: (0, j, h))
    p3 = proj.reshape(bsz, seq, IN_COLS)
    outs, recv = _call(
        body, "hgrn_fwd", (HEADS // HG_GROUP, nt),
        [col(C_HQ), col(C_HF), col(C_HI), col(C_HG),
         pl.BlockSpec((1, wide), lambda h, j: (0, h)), _resident((1, HEAD_DIM))],
        [out_tile, out_tile,
         pl.BlockSpec((HG_GROUP, bsz, nc, HEAD_DIM, HEAD_DIM), lambda h, j: (h, 0, j, 0, 0))],
        [jax.ShapeDtypeStruct((bsz, seq, WIDTH), BF16), jax.ShapeDtypeStruct((bsz, seq, WIDTH), F32),
         jax.ShapeDtypeStruct((HEADS, bsz, seq // CHUNK, HEAD_DIM, HEAD_DIM), BF16)],
        (p3, p3, p3, p3, lb, nw), scratch=[pltpu.VMEM((len(lanes), HEAD_DIM, HEAD_DIM), F32)],
        sem=("parallel", "arbitrary"), carry=carry)
    outs = [outs[0].reshape(t, WIDTH), outs[1].reshape(t, WIDTH), outs[2]]
    return outs if carry is None else (outs, recv)


def _hgrn_bwd(proj, lb, nw, o_pre, states, dy, *, bsz, seq, carry=None):
    rows = min(ROWS_HG, seq)
    nt = seq // rows
    nc = rows // CHUNK
    t = bsz * seq
    lanes = _hg_lanes(bsz)

    def body(q_ref, f_ref, v_ref, g_ref, lb_ref, nw_ref, o_ref, st_ref, dy_ref,
             dq_ref, df_ref, dv_ref, dg_ref, dlb_ref, dnw_ref, ds_scr):
        h, j = pl.program_id(0), pl.program_id(1)

        @pl.when(j == 0)
        def _():
            ds_scr[...] = jnp.zeros_like(ds_scr)
            dlb_ref[...] = jnp.zeros_like(dlb_ref)

        @pl.when((h == 0) & (j == 0))
        def _():
            dnw_ref[...] = jnp.zeros_like(dnw_ref)

        mask = _chunk_mask(rows)
        q_raw = _hg_read(q_ref, lanes)
        lb_v = _Lanes(lb_ref[:, cs] for _, cs, _ in lanes)
        gt = _hg_gates(q_raw, _hg_read(f_ref, lanes), lb_v, rows, mask)
        v_b = _hg_read(v_ref, lanes).astype(BF16)
        a_f = gt["qs"] * gt["e_a"]
        b_f = gt["k"] * gt["e_b"]
        qi_f = gt["qs"] * gt["e_q"]
        ko_f = gt["k"] * gt["e_k"]
        a_b, b_b, qi_b, ko_b = a_f.astype(BF16), b_f.astype(BF16), qi_f.astype(BF16), ko_f.astype(BF16)

        o = _hg_read(o_ref, lanes)
        nw_v = nw_ref[...]
        g = _hg_read(g_ref, lanes)
        dyv = _hg_read(dy_ref, lanes)
        r = _rsqrt(_mean(o * o, axis=-1, keepdims=True) + RMS_EPS)
        sgg = _lsigmoid(g)
        d_g = dyv * (o * r * nw_v) * (sgg * (1.0 + g * (1.0 - sgg)))
        d_on = dyv * (g * sgg)
        dnw_ref[...] += _sum(d_on * o * r, axis=0, keepdims=True).total()
        tt = d_on * nw_v
        d_o = r * tt - o * (r * r * r) * _mean(tt * o, axis=-1, keepdims=True)
        do_b = d_o.astype(BF16)

        sc_b = _where(mask, _ldot_nt(a_b, b_b), 0.0).astype(BF16)
        dsc_b = _where(mask, _ldot_nt(do_b, v_b), 0.0).astype(BF16)
        d_v = _ldot_tn(sc_b, do_b)
        d_a = _ldot(dsc_b, b_b)
        d_bm = _ldot_tn(dsc_b, a_b)

        ds = _Lanes(ds_scr[i] for i in range(len(lanes)))
        dqi_parts, dko_parts, dvi_parts, ddec_parts = [None] * nc, [None] * nc, [None] * nc, [None] * nc
        for n in reversed(range(nc)):
            sl = slice(n * CHUNK, (n + 1) * CHUNK)
            dec_n = gt["dec"][n * CHUNK:n * CHUNK + 1]
            ds_b = ds.astype(BF16)
            s_n = _Lanes(st_ref[hh, b, n] for hh, _, b in lanes)
            dqi_parts[n] = _ldot(do_b[sl], s_n)
            dko_parts[n] = _ldot(v_b[sl], ds_b)
            dvi_parts[n] = _ldot_nt(ko_b[sl], ds_b)
            d_dec = _sum(ds * s_n.astype(F32), axis=0, keepdims=True)
            ddec_parts[n] = _bcast(d_dec * dec_n, (CHUNK, HEAD_DIM))
            ds = ds * dec_n + _ldot_tn(do_b[sl], qi_b[sl])
        for i, a in enumerate(ds.xs):
            ds_scr[i] = a
        d_qi = _concat(dqi_parts, 0)
        d_ko = _concat(dko_parts, 0)
        d_v = d_v + _concat(dvi_parts, 0)

        d_qs = d_a * gt["e_a"] + d_qi * gt["e_q"]
        d_k = d_bm * gt["e_b"] + d_ko * gt["e_k"]
        t_a, t_b, t_q, t_k = d_a * a_f, d_bm * b_f, d_qi * qi_f, d_ko * ko_f
        d_bref = _chunk_total(t_b - t_a, rows)
        d_blast = _chunk_total(t_k, rows) + _concat(ddec_parts, 0)
        pos = lax.broadcasted_iota(jnp.int32, (rows, HEAD_DIM), 0) % CHUNK
        d_bc = (t_a - t_b + t_q - t_k + _where(pos == CHUNK // 2 - 1, d_bref, 0.0)
                + _where(pos == CHUNK - 1, d_blast, 0.0))
        d_logf = _mask_sum(gt["tri"], d_bc, transpose=True)

        sg, sg_neg = gt["sg"], gt["sg_neg"]
        inv_f = 1.0 / gt["forget"]
        common = (1.0 - lb_v) * sg * sg_neg
        d_fl = common * (d_logf * inv_f - d_k)
        d_lb = _sum(sg_neg * (d_logf * inv_f - d_k), axis=0, keepdims=True)
        for (_, cs, _), a in zip(lanes, d_lb.xs):
            dlb_ref[:, cs] += a
        sq = gt["sq"]
        _hg_write(dq_ref, lanes, (d_qs * (sq * (1.0 + q_raw * (1.0 - sq)))).astype(BF16))
        _hg_write(df_ref, lanes, d_fl.astype(BF16))
        _hg_write(dv_ref, lanes, d_v.astype(BF16))
        _hg_write(dg_ref, lanes, d_g.astype(BF16))

    wide = HG_GROUP * HEAD_DIM

    def col(base):
        return pl.BlockSpec((bsz, rows, wide), lambda h, j: (0, nt - 1 - j, base // wide + h))

    tile = pl.BlockSpec((bsz, rows, wide), lambda h, j: (0, nt - 1 - j, h))
    head_vec = pl.BlockSpec((1, wide), lambda h, j: (0, h))
    d_out = jax.ShapeDtypeStruct((bsz, seq, WIDTH), BF16)
    p3 = proj.reshape(bsz, seq, IN_COLS)
    outs, recv = _call(
        body, "hgrn_bwd", (HEADS // HG_GROUP, nt),
        [col(C_HQ), col(C_HF), col(C_HI), col(C_HG), head_vec, _resident((1, HEAD_DIM)), tile,
         pl.BlockSpec((HG_GROUP, bsz, nc, HEAD_DIM, HEAD_DIM), lambda h, j: (h, 0, nt - 1 - j, 0, 0)), tile],
        [tile, tile, tile, tile, head_vec, _resident((1, HEAD_DIM))],
        [d_out, d_out, d_out, d_out, jax.ShapeDtypeStruct((1, WIDTH), F32),
         jax.ShapeDtypeStruct((1, HEAD_DIM), F32)],
        (p3, p3, p3, p3, lb, nw, o_pre.reshape(bsz, seq, WIDTH), states, dy.reshape(bsz, seq, WIDTH)),
        scratch=[pltpu.VMEM((len(lanes), HEAD_DIM, HEAD_DIM), F32)],
        sem=("arbitrary", "arbitrary"), carry=carry)
    outs = [o_.reshape(t, WIDTH) for o_ in outs[:4]] + list(outs[4:])
    return outs if carry is None else (outs, recv)


def _mem_kv(mem2d, w_k, w_v):
    rows = mem2d.shape[0]

    def body(m_ref, wk_ref, wv_ref, k_ref, v_ref):
        m_b = m_ref[...].astype(BF16)
        k_ref[...] = _dot(m_b, wk_ref[...]).astype(BF16)
        v_ref[...] = _dot(m_b, wv_ref[...]).astype(BF16)

    return pl.pallas_call(
        body, name="mem_kv", grid=(rows // MEM_LEN,),
        in_specs=[pl.BlockSpec((MEM_LEN, D_MODEL), lambda i: (i, 0)), _resident((D_MODEL, WIDTH)),
                  _resident((D_MODEL, WIDTH))],
        out_specs=[pl.BlockSpec((MEM_LEN, WIDTH), lambda i: (i, 0))] * 2,
        out_shape=[jax.ShapeDtypeStruct((rows, WIDTH), BF16)] * 2,
        compiler_params=_params("parallel"),
    )(mem2d, w_k, w_v)


def _softmax_rows(s):
    m = jnp.max(s, axis=-1, keepdims=True)
    e = jnp.exp(s - m)
    return e / jnp.sum(e, axis=-1, keepdims=True)


def _attn_fwd(proj, mk, mv, *, tm, seq):
    t = proj.shape[0]
    per_b = seq // tm
    scale = HEAD_DIM ** -0.5

    def body(q_ref, k_ref, v_ref, y_ref):
        outs = []
        for h in range(HEADS):
            sl = slice(h * HEAD_DIM, (h + 1) * HEAD_DIM)
            s = _dot_nt(q_ref[:, sl].astype(BF16), k_ref[:, sl]) * scale
            p = _softmax_rows(s)
            outs.append(_dot(p.astype(BF16), v_ref[:, sl]))
        y_ref[...] = jnp.concatenate(outs, axis=-1).astype(BF16)

    kv = pl.BlockSpec((MEM_LEN, WIDTH), lambda i: (i // per_b, 0))
    return pl.pallas_call(
        body, name="attn_fwd", grid=(t // tm,),
        in_specs=[pl.BlockSpec((tm, WIDTH), lambda i: (i, C_MQ // WIDTH)), kv, kv],
        out_specs=pl.BlockSpec((tm, WIDTH), lambda i: (i, 0)),
        out_shape=jax.ShapeDtypeStruct((t, WIDTH), BF16),
        compiler_params=_params("parallel"),
    )(proj, mk, mv)


def _attn_bwd(proj, mk, mv, dy, *, tm, seq):
    t = proj.shape[0]
    per_b = seq // tm
    scale = HEAD_DIM ** -0.5

    def body(q_ref, k_ref, v_ref, dy_ref, dq_ref, dk_ref, dv_ref):
        i = pl.program_id(0)

        @pl.when(i % per_b == 0)
        def _():
            dk_ref[...] = jnp.zeros_like(dk_ref)
            dv_ref[...] = jnp.zeros_like(dv_ref)

        dqs, dks, dvs = [], [], []
        for h in range(HEADS):
            sl = slice(h * HEAD_DIM, (h + 1) * HEAD_DIM)
            q_b = q_ref[:, sl].astype(BF16)
            k_b, v_b = k_ref[:, sl], v_ref[:, sl]
            p = _softmax_rows(_dot_nt(q_b, k_b) * scale)
            dy_b = dy_ref[:, sl].astype(BF16)
            dp = _dot_nt(dy_b, v_b)
            dvs.append(_dot_tn(p.astype(BF16), dy_b))
            ds_b = (p * (dp - jnp.sum(dp * p, axis=-1, keepdims=True)) * scale).astype(BF16)
            dqs.append(_dot(ds_b, k_b))
            dks.append(_dot_tn(ds_b, q_b))
        dq_ref[...] = jnp.concatenate(dqs, axis=-1).astype(BF16)
        dk_ref[...] += jnp.concatenate(dks, axis=-1)
        dv_ref[...] += jnp.concatenate(dvs, axis=-1)

    kv = pl.BlockSpec((MEM_LEN, WIDTH), lambda i: (i // per_b, 0))
    tile = pl.BlockSpec((tm, WIDTH), lambda i: (i, 0))
    n_mem = mk.shape[0]
    return pl.pallas_call(
        body, name="attn_bwd", grid=(t // tm,),
        in_specs=[pl.BlockSpec((tm, WIDTH), lambda i: (i, C_MQ // WIDTH)), kv, kv, tile],
        out_specs=[tile, kv, kv],
        out_shape=[jax.ShapeDtypeStruct((t, WIDTH), BF16), jax.ShapeDtypeStruct((n_mem, WIDTH), F32),
                   jax.ShapeDtypeStruct((n_mem, WIDTH), F32)],
        compiler_params=_params("arbitrary"),
    )(proj, mk, mv, dy)


HALO = 8


def _shift_down(u, halo, k, row):
    out = pltpu.roll(u, k, 0)
    for m in range(k):
        out = jnp.where(row == m, halo[HALO - k + m:HALO - k + m + 1, :], out)
    return out


def _shift_up(u, halo, k, row, tm):
    out = pltpu.roll(u, tm - k, 0)
    for m in range(k):
        out = jnp.where(row == tm - k + m, halo[m:m + 1, :], out)
    return out


def _merge_fwd(proj, y_b, y_c, conv_w, w_branch, b_gate, *, tm, seq):
    t = proj.shape[0]
    per_b = seq // tm
    hb = tm // HALO

    def body(cb_ref, cc_ref, ch_ref, cch_ref, chh_ref, ga_ref, gb_ref, gc_ref, yb_ref, yc_ref, cw_ref, wb_ref,
             bg_ref, ya_ref, pa_ref, pb_ref, pc_ref, mg_ref, sa_ref, sb_ref, sc_ref):
        i = pl.program_id(0)
        row = lax.broadcasted_iota(jnp.int32, (tm, WIDTH), 0)
        u = cc_ref[...] * ch_ref[...]
        halo = jnp.where(i % per_b == 0, 0.0, cch_ref[...] * chh_ref[...])
        cw = cw_ref[...]
        y = cw[0:1] * _shift_down(u, halo, 2, row) + cw[1:2] * _shift_down(u, halo, 1, row) + cw[2:3] * u
        ya_b = (cb_ref[...] * y).astype(BF16)
        ya_ref[...] = ya_b
        merged = None
        for idx, (y_in, g_ref, p_ref, s_ref) in enumerate(((ya_b, ga_ref, pa_ref, sa_ref),
                                                            (yb_ref[...], gb_ref, pb_ref, sb_ref),
                                                            (yc_ref[...], gc_ref, pc_ref, sc_ref))):
            p = _dot(y_in, wb_ref[idx])
            p_ref[...] = p.astype(BF16)
            sg = _sigmoid(g_ref[...] + bg_ref[:, idx * D_MODEL:(idx + 1) * D_MODEL])
            s_ref[...] = sg.astype(BF16)
            term = sg * p
            merged = term if merged is None else merged + term
        mg_ref[...] = merged.astype(BF16)

    def half(c):
        return pl.BlockSpec((tm, WIDTH), lambda i: (i, c // WIDTH))

    def prev(c):
        return pl.BlockSpec((HALO, WIDTH), lambda i: (jnp.maximum(i * hb - 1, 0), c // WIDTH))

    def gate(k):
        return pl.BlockSpec((tm, D_MODEL), lambda i: (i, C_GA // D_MODEL + k))

    tile512 = pl.BlockSpec((tm, WIDTH), lambda i: (i, 0))
    tile1k = pl.BlockSpec((tm, D_MODEL), lambda i: (i, 0))
    return pl.pallas_call(
        body, name="merge_fwd", grid=(t // tm,),
        in_specs=[half(C_CB), half(C_CC), half(C_CH), prev(C_CC), prev(C_CH), gate(0), gate(1), gate(2),
                  tile512, tile512, _resident((CONV_K, WIDTH)), _resident((3, WIDTH, D_MODEL)),
                  _resident((1, 3 * D_MODEL))],
        out_specs=[tile512] + [tile1k] * 7,
        out_shape=[jax.ShapeDtypeStruct((t, WIDTH), BF16)] + [jax.ShapeDtypeStruct((t, D_MODEL), BF16)] * 7,
        compiler_params=_params("parallel"),
    )(proj, proj, proj, proj, proj, proj, proj, proj, y_b, y_c, conv_w, w_branch, b_gate)


def _merge_bwd(dmerged, projections, gates, w_branch, *, tm):
    t = dmerged.shape[0]

    def body(dm_ref, pa_ref, pb_ref, pc_ref, sa_ref, sb_ref, sc_ref, wb_ref,
             dgt_ref, dpa_ref, dpb_ref, dpc_ref, dya_ref, dyb_ref, dyc_ref, dbg_ref):
        i = pl.program_id(0)

        @pl.when(i == 0)
        def _():
            dbg_ref[...] = jnp.zeros_like(dbg_ref)

        dm = dm_ref[...].astype(F32)
        for idx, (p_ref, s_ref, dp_ref, dy_ref) in enumerate(((pa_ref, sa_ref, dpa_ref, dya_ref),
                                                              (pb_ref, sb_ref, dpb_ref, dyb_ref),
                                                              (pc_ref, sc_ref, dpc_ref, dyc_ref))):
            cols = slice(idx * D_MODEL, (idx + 1) * D_MODEL)
            sg = s_ref[...].astype(F32)
            dp = dm * sg
            dp_b = dp.astype(BF16)
            dp_ref[...] = dp_b
            dgate = dp * p_ref[...].astype(F32) * (1.0 - sg)
            dgt_ref[:, cols] = dgate.astype(BF16)
            dbg_ref[:, cols] += jnp.sum(dgate, axis=0, keepdims=True)
            dy_ref[...] = _dot_nt(dp_b, wb_ref[idx])

    tile512 = pl.BlockSpec((tm, WIDTH), lambda i: (i, 0))
    tile1k = pl.BlockSpec((tm, D_MODEL), lambda i: (i, 0))
    return pl.pallas_call(
        body, name="merge_bwd", grid=(t // tm,),
        in_specs=[tile1k] * 7 + [_resident((3, WIDTH, D_MODEL))],
        out_specs=[pl.BlockSpec((tm, 3 * D_MODEL), lambda i: (i, 0)), tile1k, tile1k, tile1k,
                   tile512, tile512, tile512, _resident((1, 3 * D_MODEL))],
        out_shape=[jax.ShapeDtypeStruct((t, 3 * D_MODEL), BF16)] + [jax.ShapeDtypeStruct((t, D_MODEL), BF16)] * 3
                  + [jax.ShapeDtypeStruct((t, WIDTH), F32)] * 3 + [jax.ShapeDtypeStruct((1, 3 * D_MODEL), F32)],
        compiler_params=_params("arbitrary"),
    )(dmerged, *projections, *gates, w_branch)


def _conv_bwd(proj, dya, conv_w, *, tm, seq):
    t = proj.shape[0]
    per_b = seq // tm
    hb = tm // HALO
    last_blk = t // HALO - 1

    def body(cb_ref, cc_ref, ch_ref, cch_ref, chh_ref, dya_ref, cbn_ref, dyan_ref, cw_ref, d_ref, dcw_ref):
        i = pl.program_id(0)

        @pl.when(i == 0)
        def _():
            dcw_ref[...] = jnp.zeros_like(dcw_ref)

        row = lax.broadcasted_iota(jnp.int32, (tm, WIDTH), 0)
        cb, cc, ch = cb_ref[...], cc_ref[...], ch_ref[...]
        u = cc * ch
        halo = jnp.where(i % per_b == 0, 0.0, cch_ref[...] * chh_ref[...])
        u1 = _shift_down(u, halo, 1, row)
        u2 = _shift_down(u, halo, 2, row)
        cw = cw_ref[...]
        y = cw[0:1] * u2 + cw[1:2] * u1 + cw[2:3] * u
        dya = dya_ref[...]
        dy = dya * cb
        nxt = jnp.where(i % per_b == per_b - 1, 0.0, dyan_ref[...] * cbn_ref[...])
        du = cw[2:3] * dy + cw[1:2] * _shift_up(dy, nxt, 1, row, tm) + cw[0:1] * _shift_up(dy, nxt, 2, row, tm)
        d_ref[:, 0:WIDTH] = (dya * y).astype(BF16)
        d_ref[:, WIDTH:2 * WIDTH] = (du * ch).astype(BF16)
        d_ref[:, 2 * WIDTH:3 * WIDTH] = (du * cc).astype(BF16)
        dcw_ref[0:1, :] += jnp.sum(dy * u2, axis=0, keepdims=True)
        dcw_ref[1:2, :] += jnp.sum(dy * u1, axis=0, keepdims=True)
        dcw_ref[2:3, :] += jnp.sum(dy * u, axis=0, keepdims=True)

    def half(c):
        return pl.BlockSpec((tm, WIDTH), lambda i: (i, c // WIDTH))

    def prev(c):
        return pl.BlockSpec((HALO, WIDTH), lambda i: (jnp.maximum(i * hb - 1, 0), c // WIDTH))

    def nxt(c):
        return pl.BlockSpec((HALO, WIDTH), lambda i: (jnp.minimum((i + 1) * hb, last_blk), c // WIDTH))

    return pl.pallas_call(
        body, name="conv_bwd", grid=(t // tm,),
        in_specs=[half(C_CB), half(C_CC), half(C_CH), prev(C_CC), prev(C_CH),
                  pl.BlockSpec((tm, WIDTH), lambda i: (i, 0)), nxt(C_CB), nxt(0), _resident((CONV_K, WIDTH))],
        out_specs=[pl.BlockSpec((tm, 3 * WIDTH), lambda i: (i, 0)), _resident((CONV_K, WIDTH))],
        out_shape=[jax.ShapeDtypeStruct((t, 3 * WIDTH), BF16), jax.ShapeDtypeStruct((CONV_K, WIDTH), F32)],
        compiler_params=_params("arbitrary"),
    )(proj, proj, proj, proj, proj, dya, proj, dya, conv_w)


def _loss_head(y, target, *, tm):
    t = y.shape[0]

    def body(y_ref, t_ref, dy_ref, l_ref):
        @pl.when(pl.program_id(0) == 0)
        def _():
            l_ref[...] = jnp.zeros_like(l_ref)

        err = y_ref[...] - t_ref[...]
        dy_ref[...] = err * (1.0 / D_MODEL)
        per_row = jnp.sum(err * err, axis=-1, keepdims=True) * (1.0 / D_MODEL)
        l_ref[...] += 0.5 * jnp.sum(per_row, axis=0, keepdims=True)

    row = pl.BlockSpec((tm, D_MODEL), lambda i: (i, 0))
    return pl.pallas_call(
        body, name="loss_head", grid=(t // tm,),
        in_specs=[row, row], out_specs=[row, _resident((8, 128))],
        out_shape=[jax.ShapeDtypeStruct((t, D_MODEL), F32), jax.ShapeDtypeStruct((8, 128), F32)],
        compiler_params=_params("arbitrary"),
    )(y, target)


def _lb_softmax(lower_bounds):
    x = lower_bounds
    e = jnp.exp(x - jnp.max(x, axis=0, keepdims=True))
    return e / jnp.sum(e, axis=0, keepdims=True)


def _lb_fwd(lower_bounds):
    def body(x_ref, o_ref):
        s = _lb_softmax(x_ref[...])
        c = s[0:1]
        o_ref[0:1, :] = c - s[0:1]
        for l in range(1, DEPTH):
            c = c + s[l:l + 1]
            o_ref[l:l + 1, :] = c - s[0:1]

    return pl.pallas_call(body, name="lb_fwd", out_shape=jax.ShapeDtypeStruct(lower_bounds.shape, F32))(lower_bounds)


def _lb_bwd(lower_bounds, d_lb_all):
    def body(x_ref, d_ref, o_ref):
        s = _lb_softmax(x_ref[...])
        d = d_ref[...]
        rows = [jnp.zeros_like(d[0:1])]
        for j in range(1, DEPTH):
            acc = d[j:j + 1]
            for l in range(j + 1, DEPTH):
                acc = acc + d[l:l + 1]
            rows.append(acc)
        inner = rows[0] * s[0:1]
        for j in range(1, DEPTH):
            inner = inner + rows[j] * s[j:j + 1]
        for j in range(DEPTH):
            o_ref[j:j + 1, :] = s[j:j + 1] * (rows[j] - inner)

    return pl.pallas_call(body, name="lb_bwd", out_shape=jax.ShapeDtypeStruct(lower_bounds.shape, F32))(
        lower_bounds, d_lb_all)


def _adamw(w, g, m, v):
    m2 = ADAM_B1 * m + (1.0 - ADAM_B1) * g
    v2 = ADAM_B2 * v + (1.0 - ADAM_B2) * (g * g)
    m_hat = m2 / (1.0 - ADAM_B1 ** ADAM_STEP)
    v_hat = v2 / (1.0 - ADAM_B2 ** ADAM_STEP)
    delta = -ADAM_LR * (m_hat / (jnp.sqrt(v_hat) + ADAM_EPS) + ADAM_WD * w)
    return delta, m2, v2


def _adam_small(name, g, w, m, v):
    shape = w.shape
    flat = (-1, shape[-1])
    g2, w2, m2, v2 = (a.reshape(flat) for a in (g, w, m, v))

    def body(g_ref, w_ref, m_ref, v_ref, d_ref, mo_ref, vo_ref):
        d, mm, vv = _adamw(w_ref[...], g_ref[...], m_ref[...], v_ref[...])
        d_ref[...] = d
        mo_ref[...] = mm
        vo_ref[...] = vv

    outs = pl.pallas_call(body, name=name, out_shape=[jax.ShapeDtypeStruct(w2.shape, F32)] * 3)(g2, w2, m2, v2)
    return [o.reshape(shape) for o in outs]


def _adam_shard(name, recvs, w, m, v, *, tr):
    _, r, c = w.shape

    def body(*refs):
        rc, (w_ref, m_ref, v_ref), (g_ref, d_ref, mo_ref, vo_ref) = refs[:DEPTH], refs[DEPTH:DEPTH + 3], refs[DEPTH + 3:]
        layer = pl.program_id(0)
        for cand in range(DEPTH):
            @pl.when(layer == cand)
            def _():
                g = rc[cand][0].astype(F32)
                for d in range(1, N_DEV):
                    g = g + rc[cand][d].astype(F32)
                dl, mm, vv = _adamw(w_ref[...], g, m_ref[...], v_ref[...])
                g_ref[...] = g
                d_ref[...] = dl
                mo_ref[...] = mm
                vo_ref[...] = vv

    def recv_spec(cand):
        return pl.BlockSpec((N_DEV, tr, c), lambda l, i: (0, jnp.where(l == cand, i, 0), 0))

    tile = pl.BlockSpec((None, tr, c), lambda l, i: (l, i, 0))
    return pl.pallas_call(
        body, name=name, grid=(DEPTH, r // tr),
        in_specs=[recv_spec(cand) for cand in range(DEPTH)] + [tile] * 3,
        out_specs=[tile] * 4,
        out_shape=[jax.ShapeDtypeStruct(w.shape, F32)] * 4,
        compiler_params=_params("parallel", "parallel"),
    )(*recvs, w, m, v)


def _sum_devices(name, x):
    def body(x_ref, o_ref):
        acc = x_ref[0]
        for d in range(1, N_DEV):
            acc = acc + x_ref[d]
        o_ref[...] = acc

    return pl.pallas_call(body, name=name, out_shape=jax.ShapeDtypeStruct(x.shape[1:], x.dtype))(x)


SMALL = (("lower_bounds", 512), ("conv_w", CONV_K * WIDTH), ("hg_norm_w", HEAD_DIM), ("b_gate", 3 * D_MODEL),
         ("ln1_g", D_MODEL), ("ln1_b", D_MODEL), ("ln2_g", D_MODEL), ("ln2_b", D_MODEL))
SMALL_PER_LAYER = sum(n for _, n in SMALL)
SMALL_ROWS = 296


def _natural_cols(g):
    nd = g.ndim
    perm = tuple(range(1, nd - 1)) + (0, nd - 1)
    t = jnp.transpose(g, perm)
    return t.reshape(t.shape[:-2] + (t.shape[-2] * t.shape[-1],))


def _natural_rows(g):
    return g.reshape(g.shape[0] * g.shape[1], g.shape[2])


def _hosted(hosts, key, fn):
    if not hosts or key not in hosts:
        return fn(None)
    ex, hook = hosts[key]
    outs, recv = fn(ex)
    hook(recv)
    return outs


def _layer_fwd(cur, cur_b, mem2d, wl, *, bsz, seq, hosts=None):
    tm = min(512, seq)
    proj = _hosted(hosts, "in_proj", lambda c: _mm_nn("in_proj", cur_b, wl["w_in"], tm=min(1024, seq), tn=1024,
                                                       out_dtype=F32, carry=c))
    y_b, o_pre, states = _hosted(hosts, "hgrn_fwd", lambda c: _hgrn_fwd(proj, wl["lb"], wl["nw"], bsz=bsz, seq=seq,
                                                                         carry=c))
    mk, mv = _mem_kv(mem2d, wl["w_mk"], wl["w_mv"])
    y_c = _attn_fwd(proj, mk, mv, tm=tm, seq=seq)
    y_a, pa, pb, pc, merged, sga, sgb, sgc = _merge_fwd(proj, y_b, y_c, wl["conv"], wl["w_br"], wl["b_gate"],
                                                        tm=tm, seq=seq)
    z1, x1, x1_b = _linear_ln("wo_ln", merged, wl["w_o"], cur, wl["ln1_g"], wl["ln1_b"], tm=tm)
    hid = _hosted(hosts, "mlp_up", lambda c: _mm_nn("mlp_up", x1_b, wl["w_up"], tm=tm, tn=1024, out_dtype=BF16,
                                                     relu2=True, carry=c))
    z2, x2, x2_b = _linear_ln("down_ln", hid, wl["w_down"], x1, wl["ln2_g"], wl["ln2_b"], tm=tm)
    return dict(x_b=cur_b, proj=proj, y_a=y_a, y_b=y_b, y_c=y_c, o_pre=o_pre, states=states, mk=mk, mv=mv,
                proj3=(pa, pb, pc), gates3=(sga, sgb, sgc), merged=merged, z1=z1, x1_b=x1_b, hid=hid, z2=z2, x2=x2,
                x2_b=x2_b)


def _layer_bwd(dcur, mem2d, s, wl, *, bsz, seq, hosts=None, early=None, late=None):
    tm = min(512, seq)
    tk = min(1024, bsz * seq)
    dz2, dz2_b, dhpre, d_ln2g, d_ln2b = _hosted(
        hosts, "ln2_bwd_down", lambda c: _ln_bwd_mm_nt("ln2_bwd_down", dcur, s["z2"], wl["ln2_g"], wl["w_down"],
                                                       s["hid"], tm=tm, tn=1024, carry=c))
    g_down = _mm_tn("grad_w_down", s["hid"], dz2_b, tk=tk, tmo=1024, tno=1024)
    dx1 = _mm_nt_sum("mlp_up_bwd", [dhpre], [0], wl["w_up"], dz2, tm=tm)
    g_up = _mm_tn("grad_w_up", s["x1_b"], dhpre, tk=tk, tmo=1024, tno=1024)
    dz1, dz1_b, dmerged, d_ln1g, d_ln1b = _ln_bwd_mm_nt("ln1_bwd_wo", dx1, s["z1"], wl["ln1_g"], wl["w_o"],
                                                        tm=tm, tn=1024)
    g_o = _mm_tn("grad_w_o", s["merged"], dz1_b, tk=tk, tmo=1024, tno=1024)
    dgate, dpa, dpb, dpc, dya, dyb, dyc, d_bg = _merge_bwd(dmerged, s["proj3"], s["gates3"], wl["w_br"], tm=tm)
    g_br = jnp.stack([_mm_tn("grad_w_branch", yy, dp, tk=tk, tmo=512, tno=1024)
                      for yy, dp in ((s["y_a"], dpa), (s["y_b"], dpb), (s["y_c"], dpc))])
    d_conv, d_cw = _conv_bwd(s["proj"], dya, wl["conv"], tm=tm, seq=seq)
    hg_hosts = {"hgrn_bwd": early(g_br, g_o, g_up, g_down)} if early is not None else None
    dq, df, di, dg, d_lb, d_nw = _hosted(
        hg_hosts, "hgrn_bwd", lambda c: _hgrn_bwd(s["proj"], wl["lb"], wl["nw"], s["o_pre"], s["states"], dyb,
                                                  bsz=bsz, seq=seq, carry=c))
    dmq, dmk, dmv = _attn_bwd(s["proj"], s["mk"], s["mv"], dyc, tm=tm, seq=seq)
    tkm = min(512, mem2d.shape[0])
    g_mk = _mm_tn("grad_w_mem", mem2d, dmk, tk=tkm, tmo=1024, tno=512)
    g_mv = _mm_tn("grad_w_mem", mem2d, dmv, tk=tkm, tmo=1024, tno=512)
    pieces = [d_conv, dq, df, di, dg, dmq, dgate]
    offsets = [C_CB, C_HQ, C_HF, C_HI, C_HG, C_MQ, C_GA]
    g_in = jnp.concatenate(
        [_mm_tn("grad_w_in_%d" % p.shape[1], s["x_b"], p, tk=tk, tmo=1024,
                tno=(1024 if p.shape[1] % 1024 == 0 else 512)) for p in pieces], axis=1)
    dx_hosts = {"in_proj_bwd": late(g_in, g_mk, g_mv)} if late is not None else None
    dx = _hosted(dx_hosts, "in_proj_bwd", lambda c: _mm_nt_sum("in_proj_bwd", pieces, offsets, wl["w_in"], dz1,
                                                               tm=min(256, seq), carry=c))
    small = jnp.concatenate([d_lb[0], d_cw.reshape(-1), d_nw[0], d_bg[0], d_ln1g[0], d_ln1b[0], d_ln2g[0],
                             d_ln2b[0]])
    return dx, [g_in, g_mk, g_mv, g_br, g_o, g_up, g_down], small


def kernel(x, mem, lower_bounds, w_in, conv_w, hg_norm_w, w_mem_k, w_mem_v, w_branch, b_gate, w_o, ln1_g, ln1_b, w_up, w_down, ln2_g, ln2_b, loss_target, m_lower_bounds, m_w_in, m_conv_w, m_hg_norm_w, m_w_mem_k, m_w_mem_v, m_w_branch, m_b_gate, m_w_o, m_ln1_g, m_ln1_b, m_w_up, m_w_down, m_ln2_g, m_ln2_b, v_lower_bounds, v_w_in, v_conv_w, v_hg_norm_w, v_w_mem_k, v_w_mem_v, v_w_branch, v_b_gate, v_w_o, v_ln1_g, v_ln1_b, v_w_up, v_w_down, v_ln2_g, v_ln2_b):
    bsz, seq, _ = x.shape
    t = bsz * seq
    me = _my_id()

    sh = dict(w_in=w_in.astype(BF16), w_mk=w_mem_k.astype(BF16), w_mv=w_mem_v.astype(BF16),
              w_br=w_branch.astype(BF16), w_o=w_o.astype(BF16), w_up=w_up.astype(BF16), w_down=w_down.astype(BF16))
    natural = dict(w_in=_natural_cols, w_mk=_natural_rows, w_mv=_natural_rows, w_br=_natural_cols,
                   w_o=_natural_rows, w_up=_natural_cols, w_down=_natural_rows)

    def gather_of(names, l):
        srcs = [sh[n][l] for n in names]
        return _Exchange(srcs, [_whole] * len(srcs), [s_.shape for s_ in srcs])

    def put(names, into):
        def hook(recv_):
            for n, r in zip(names, recv_):
                into[n] = natural[n](r)
        return hook

    lb_all = _lb_fwd(lower_bounds)
    layer_w = [dict(lb=lb_all[l][None], nw=hg_norm_w[l][None], b_gate=b_gate[l][None], ln1_g=ln1_g[l][None],
                    ln1_b=ln1_b[l][None], ln2_g=ln2_g[l][None], ln2_b=ln2_b[l][None]) for l in range(DEPTH)]
    first = ["w_in", "w_mk", "w_mv", "w_br", "w_o"]
    conv_shard = conv_w.reshape(DEPTH * CONV_K * (WIDTH // N_DEV) // 128, 128)
    ex0 = gather_of(first, 0)
    got = _exchange("gather_first", ex0.srcs + [conv_shard], [_whole] * (len(first) + 1),
                    [s_.shape for s_ in ex0.srcs] + [conv_shard.shape])
    put(first, layer_w[0])(got[:len(first)])
    conv_full = _natural_cols(got[-1].reshape(N_DEV, DEPTH, CONV_K, WIDTH // N_DEV))

    x2d = x.reshape(t, D_MODEL)
    mem2d = mem.reshape(bsz * MEM_LEN, D_MODEL)
    target2d = loss_target.reshape(t, D_MODEL)

    saved = []
    cur, cur_b = x2d, x2d.astype(BF16)
    for l in range(DEPTH):
        wl = layer_w[l]
        wl["conv"] = conv_full[l]
        hosts = {"in_proj": (gather_of(["w_up", "w_down"], l), put(["w_up", "w_down"], wl))}
        if l + 1 < DEPTH:
            hosts["hgrn_fwd"] = (gather_of(["w_in"], l + 1), put(["w_in"], layer_w[l + 1]))
            rest = ["w_mk", "w_mv", "w_br", "w_o"]
            hosts["mlp_up"] = (gather_of(rest, l + 1), put(rest, layer_w[l + 1]))
        s = _layer_fwd(cur, cur_b, mem2d, wl, bsz=bsz, seq=seq, hosts=hosts)
        saved.append(s)
        cur, cur_b = s["x2"], s["x2_b"]

    dcur, loss_tile = _loss_head(cur, target2d, tm=min(512, seq))
    loss = lax.psum(loss_tile[0, 0], ("x", "y", "c"))

    slicer = dict(w_in=_cols(IN_COLS // N_DEV), w_mk=_rows(D_MODEL // N_DEV), w_mv=_rows(D_MODEL // N_DEV),
                  w_br=_cols(D_MODEL // N_DEV), w_o=_rows(D_MODEL // N_DEV), w_up=_cols(D_FF // N_DEV),
                  w_down=_rows(D_FF // N_DEV))
    recv = [dict() for _ in range(DEPTH)]

    def scatter_of(names, grads_):
        return _Exchange(grads_, [slicer[n] for n in names], [sh[n].shape[1:] for n in names])

    def keep(names, into):
        def hook(recv_):
            into.update(zip(names, recv_))
        return hook

    early_names = ["w_br", "w_o", "w_up", "w_down"]
    late_names = ["w_in", "w_mk", "w_mv"]
    small_rows = [None] * DEPTH
    late = None
    for l in reversed(range(DEPTH)):
        hosts = None
        if late is not None:
            hosts = {"ln2_bwd_down": (scatter_of(late_names, late), keep(late_names, recv[l + 1]))}
        last = (lambda *g: (scatter_of(late_names, list(g)), keep(late_names, recv[0]))) if l == 0 else None
        dcur, big_grads, small_rows[l] = _layer_bwd(
            dcur, mem2d, saved[l], layer_w[l], bsz=bsz, seq=seq, hosts=hosts,
            early=lambda *g, l=l: (scatter_of(early_names, list(g)), keep(early_names, recv[l])), late=last)
        late = big_grads[:3]

    packed = jnp.concatenate(small_rows + [jnp.zeros((SMALL_ROWS * 128 - DEPTH * SMALL_PER_LAYER,), F32)])
    packed = packed.reshape(SMALL_ROWS, 128)
    all_small = _exchange("gather_small_grads", [packed], [_whole], [packed.shape])[0]
    summed = _sum_devices("sum_small_grads", all_small).reshape(-1)[:DEPTH * SMALL_PER_LAYER]
    summed = summed.reshape(DEPTH, SMALL_PER_LAYER)
    small_grads = {}
    off = 0
    for name, n in SMALL:
        small_grads[name] = summed[:, off:off + n]
        off += n
    small_grads["lower_bounds"] = _lb_bwd(lower_bounds, small_grads["lower_bounds"])
    conv_all = small_grads["conv_w"].reshape(DEPTH, CONV_K, WIDTH)
    small_grads["conv_w"] = lax.dynamic_slice_in_dim(conv_all, me * (WIDTH // N_DEV), WIDTH // N_DEV, axis=2)

    grads, deltas, new_m, new_v = {}, {}, {}, {}
    given = dict(lower_bounds=(lower_bounds, m_lower_bounds, v_lower_bounds), conv_w=(conv_w, m_conv_w, v_conv_w),
                 hg_norm_w=(hg_norm_w, m_hg_norm_w, v_hg_norm_w), b_gate=(b_gate, m_b_gate, v_b_gate),
                 ln1_g=(ln1_g, m_ln1_g, v_ln1_g), ln1_b=(ln1_b, m_ln1_b, v_ln1_b),
                 ln2_g=(ln2_g, m_ln2_g, v_ln2_g), ln2_b=(ln2_b, m_ln2_b, v_ln2_b))
    for name, (w_, m_, v_) in given.items():
        g_ = small_grads[name].reshape(w_.shape)
        grads[name] = g_
        deltas[name], new_m[name], new_v[name] = _adam_small("adam_" + name, g_, w_, m_, v_)

    big = dict(w_in=("w_in", w_in, m_w_in, v_w_in, 128), w_mem_k=("w_mk", w_mem_k, m_w_mem_k, v_w_mem_k, 128),
               w_mem_v=("w_mv", w_mem_v, m_w_mem_v, v_w_mem_v, 128),
               w_branch=("w_br", w_branch, m_w_branch, v_w_branch, 512), w_o=("w_o", w_o, m_w_o, v_w_o, 128),
               w_up=("w_up", w_up, m_w_up, v_w_up, 256), w_down=("w_down", w_down, m_w_down, v_w_down, 128))
    for name, (k, w_, m_, v_, tr) in big.items():
        shape = w_.shape
        flat = (DEPTH, -1, shape[-1])
        rc = [recv[l][k].reshape((N_DEV,) + w_.reshape(flat).shape[1:]) for l in range(DEPTH)]
        outs = _adam_shard("adam_" + name, rc, w_.reshape(flat), m_.reshape(flat), v_.reshape(flat), tr=tr)
        grads[name], deltas[name], new_m[name], new_v[name] = (o.reshape(shape) for o in outs)

    order = ["lower_bounds", "w_in", "conv_w", "hg_norm_w", "w_mem_k", "w_mem_v", "w_branch", "b_gate", "w_o",
             "ln1_g", "ln1_b", "w_up", "w_down", "ln2_g", "ln2_b"]
    return (loss, dcur.reshape(x.shape), *[grads[n] for n in order], *[deltas[n] for n in order],
            *[new_m[n] for n in order], *[new_v[n] for n in order])
```

```python
import functools

import jax
import jax.numpy as jnp
from jax import lax
from jax.experimental import pallas as pl
from jax.experimental.pallas import tpu as pltpu

F32 = jnp.float32
BF16 = jnp.bfloat16

N_DEV = 8
D_MODEL = 1024
DEPTH = 4
MEM_LEN = 256
CONV_K = 3
WIDTH = 512
HEADS = 4
HEAD_DIM = 128
CHUNK = 32
D_FF = 4 * D_MODEL
IN_COLS = 7168
ALPHA = (2.0 * DEPTH) ** 0.25
LN_EPS = 1e-5
RMS_EPS = 1e-6
ADAM_LR = 0.001
ADAM_B1 = 0.9
ADAM_B2 = 0.999
ADAM_EPS = 1e-08
ADAM_WD = 0.01
ADAM_STEP = 10

C_CB, C_CC, C_CH, C_HQ, C_HF, C_HI, C_HG, C_MQ, C_GA = 0, 512, 1024, 1536, 2048, 2560, 3072, 3584, 4096

ROWS_HG = 256
NT_DIMS = (((1,), (1,)), ((), ()))
TN_DIMS = (((0,), (0,)), ((), ()))
MESH = pl.DeviceIdType.MESH


def _dot(a, b):
    return jnp.dot(a, b, preferred_element_type=F32)


def _dot_nt(a, b):
    return lax.dot_general(a, b, NT_DIMS, preferred_element_type=F32)


def _dot_tn(a, b):
    return lax.dot_general(a, b, TN_DIMS, preferred_element_type=F32)


def _sigmoid(x):
    return 1.0 / (1.0 + jnp.exp(-x))


def _params(*sem):
    return pltpu.CompilerParams(dimension_semantics=sem)


def _resident(shape):
    nd = len(shape)
    return pl.BlockSpec(shape, lambda *_: (0,) * nd)


def _my_id():
    return 4 * lax.axis_index("x") + 2 * lax.axis_index("y") + lax.axis_index("c")


class _Exchange:
    def __init__(self, srcs, slicers, piece_shapes):
        self.srcs, self.slicers, self.n = list(srcs), list(slicers), len(srcs)
        any_spec = pl.BlockSpec(memory_space=pl.ANY)
        self.in_specs = [any_spec] * self.n
        self.out_specs = [any_spec] * self.n
        self.out_shape = [jax.ShapeDtypeStruct((N_DEV,) + tuple(s), a.dtype) for s, a in zip(piece_shapes, srcs)]
        self.scratch = [pltpu.SemaphoreType.DMA((self.n * N_DEV,)), pltpu.SemaphoreType.DMA((self.n * N_DEV,)),
                        pltpu.SemaphoreType.DMA((self.n,))]

    def _remote(self, ins, outs, sems, k, j, me):
        return pltpu.make_async_remote_copy(
            src_ref=self.slicers[k](ins[k], j), dst_ref=outs[k].at[me],
            send_sem=sems[0].at[k * N_DEV + j], recv_sem=sems[1].at[k * N_DEV + me],
            device_id=(j // 4, (j // 2) % 2, j % 2), device_id_type=MESH)

    def _local(self, ins, outs, sems, k, j, me):
        return pltpu.make_async_copy(self.slicers[k](ins[k], j), outs[k].at[me], sems[2].at[k])

    def start(self, ins, outs, sems):
        me = _my_id()
        for k in range(self.n):
            for j in range(N_DEV):
                @pl.when(j != me)
                def _():
                    self._remote(ins, outs, sems, k, j, me).start()

                @pl.when(j == me)
                def _():
                    self._local(ins, outs, sems, k, j, me).start()

    def wait(self, ins, outs, sems):
        me = _my_id()
        for k in range(self.n):
            for j in range(N_DEV):
                @pl.when(j != me)
                def _():
                    pltpu.make_async_remote_copy(
                        src_ref=self.slicers[k](ins[k], j), dst_ref=outs[k].at[j],
                        send_sem=sems[0].at[k * N_DEV + j], recv_sem=sems[1].at[k * N_DEV + j],
                        device_id=(j // 4, (j // 2) % 2, j % 2), device_id_type=MESH).wait_recv()
                    self._remote(ins, outs, sems, k, j, me).wait_send()

                @pl.when(j == me)
                def _():
                    self._local(ins, outs, sems, k, j, me).wait()


def _exchange(name, srcs, slicers, piece_shapes):
    ex = _Exchange(srcs, slicers, piece_shapes)

    def body(*refs):
        ins, outs, sems = refs[:ex.n], refs[ex.n:2 * ex.n], refs[2 * ex.n:]
        ex.start(ins, outs, sems)
        ex.wait(ins, outs, sems)

    return pl.pallas_call(
        body, name=name, in_specs=ex.in_specs, out_specs=ex.out_specs, out_shape=ex.out_shape,
        scratch_shapes=ex.scratch, compiler_params=pltpu.CompilerParams(has_side_effects=True),
    )(*ex.srcs)


def _call(body, name, grid, in_specs, out_specs, out_shape, args, scratch=(), sem=None, carry=None):
    n_in, n_out, n_scr = len(in_specs), len(out_specs), len(scratch)
    if carry is None:
        outs = pl.pallas_call(body, name=name, grid=grid, in_specs=in_specs, out_specs=out_specs,
                              out_shape=out_shape, scratch_shapes=list(scratch),
                              compiler_params=_params(*sem))(*args)
        return outs, None
    nc = carry.n

    def hosted(*refs):
        ins, c_in = refs[:n_in], refs[n_in:n_in + nc]
        outs = refs[n_in + nc:n_in + nc + n_out]
        c_out = refs[n_in + nc + n_out:n_in + 2 * nc + n_out]
        rest = refs[n_in + 2 * nc + n_out:]
        scr, sems = rest[:n_scr], rest[n_scr:]
        first, last = True, True
        for d, size in enumerate(grid):
            first = first & (pl.program_id(d) == 0)
            last = last & (pl.program_id(d) == size - 1)

        @pl.when(first)
        def _():
            carry.start(c_in, c_out, sems)

        body(*ins, *outs, *scr)

        @pl.when(last)
        def _():
            carry.wait(c_in, c_out, sems)

    outs = pl.pallas_call(
        hosted, name=name + "_x", grid=grid, in_specs=list(in_specs) + carry.in_specs,
        out_specs=list(out_specs) + carry.out_specs, out_shape=list(out_shape) + carry.out_shape,
        scratch_shapes=list(scratch) + carry.scratch,
        compiler_params=_params(*(["arbitrary"] * len(grid))))(*args, *carry.srcs)
    return outs[:n_out], outs[n_out:]


def _whole(ref, j):
    return ref


def _cols(width):
    return lambda ref, j: ref.at[(slice(None),) * (len(ref.shape) - 1) + (pl.ds(j * width, width),)]


def _rows(height):
    return lambda ref, j: ref.at[pl.ds(j * height, height)]


def _mm_nn(name, a, w, *, tm, tn, out_dtype, relu2=False, carry=None):
    t, k = a.shape
    n = w.shape[1]

    def body(a_ref, w_ref, o_ref):
        acc = _dot(a_ref[...].astype(BF16), w_ref[...])
        if relu2:
            r = jnp.maximum(acc, 0.0)
            acc = r * r
        o_ref[...] = acc.astype(out_dtype)

    outs, recv = _call(
        body, name, (t // tm, n // tn),
        [pl.BlockSpec((tm, k), lambda i, j: (i, 0)), pl.BlockSpec((k, tn), lambda i, j: (0, j))],
        [pl.BlockSpec((tm, tn), lambda i, j: (i, j))], [jax.ShapeDtypeStruct((t, n), out_dtype)], (a, w),
        sem=("parallel", "parallel"), carry=carry)
    return outs[0] if carry is None else (outs[0], recv)


def _layer_norm(z, g, b):
    mu = jnp.mean(z, axis=-1, keepdims=True)
    zc = z - mu
    var = jnp.mean(zc * zc, axis=-1, keepdims=True)
    return zc * lax.rsqrt(var + LN_EPS) * g + b


def _linear_ln(name, a, w, resid, g, b, *, tm, carry=None):
    t, k = a.shape

    def body(a_ref, w_ref, r_ref, g_ref, b_ref, z_ref, x_ref, xb_ref):
        z = ALPHA * r_ref[...] + _dot(a_ref[...], w_ref[...])
        z_ref[...] = z
        y = _layer_norm(z, g_ref[...], b_ref[...])
        x_ref[...] = y
        xb_ref[...] = y.astype(BF16)

    row = pl.BlockSpec((tm, D_MODEL), lambda i: (i, 0))
    outs, recv = _call(
        body, name, (t // tm,),
        [pl.BlockSpec((tm, k), lambda i: (i, 0)), _resident((k, D_MODEL)), row,
         _resident((1, D_MODEL)), _resident((1, D_MODEL))],
        [row, row, row],
        [jax.ShapeDtypeStruct((t, D_MODEL), F32)] * 2 + [jax.ShapeDtypeStruct((t, D_MODEL), BF16)],
        (a, w, resid, g, b), sem=("parallel",), carry=carry)
    return outs if carry is None else (outs, recv)


def _ln_bwd_mm_nt(name, dy, z, g, w, h=None, *, tm, tn, carry=None):
    t = dy.shape[0]
    n = w.shape[0]
    halves = [slice(0, tm // 2), slice(tm // 2, tm)]

    def body(*refs):
        if h is None:
            dy_ref, z_ref, g_ref, w_ref, dz_ref, dzb_ref, o_ref, dg_ref, db_ref = refs
        else:
            dy_ref, z_ref, g_ref, w_ref, h_ref, dz_ref, dzb_ref, o_ref, dg_ref, db_ref = refs

        @pl.when(pl.program_id(0) == 0)
        def _():
            dg_ref[...] = jnp.zeros_like(dg_ref)
            db_ref[...] = jnp.zeros_like(db_ref)

        zv = _Lanes(z_ref[s, :] for s in halves)
        dyv = _Lanes(dy_ref[s, :] for s in halves)
        mu = _mean(zv, axis=-1, keepdims=True)
        zc = zv - mu
        rstd = _rsqrt(_mean(zc * zc, axis=-1, keepdims=True) + LN_EPS)
        xh = zc * rstd
        gdy = dyv * g_ref[...]
        m1 = _mean(gdy, axis=-1, keepdims=True)
        m2 = _mean(gdy * xh, axis=-1, keepdims=True)
        dz = rstd * (gdy - m1 - xh * m2)
        dz_b = dz.astype(BF16)
        for s, a, a_b in zip(halves, dz.xs, dz_b.xs):
            dz_ref[s, :] = a
            dzb_ref[s, :] = a_b
        dg_ref[...] += _sum(dyv * xh, axis=0, keepdims=True).total()
        db_ref[...] += _sum(dyv, axis=0, keepdims=True).total()
        for c in range(n // tn):
            cols = slice(c * tn, (c + 1) * tn)
            acc = _ldot_nt(dz_b, w_ref[cols, :])
            if h is not None:
                acc = acc * (2.0 * _sqrt(_Lanes(h_ref[s, cols] for s in halves).astype(F32)))
            for s, a in zip(halves, acc.xs):
                o_ref[s, cols] = a.astype(BF16)

    row = pl.BlockSpec((tm, D_MODEL), lambda i: (i, 0))
    vec = _resident((1, D_MODEL))
    tile = pl.BlockSpec((tm, n), lambda i: (i, 0))
    in_specs = [row, row, vec, _resident((n, D_MODEL))]
    args = [dy, z, g, w]
    if h is not None:
        in_specs.append(tile)
        args.append(h)
    outs, recv = _call(
        body, name, (t // tm,), in_specs, [row, row, tile, vec, vec],
        [jax.ShapeDtypeStruct((t, D_MODEL), F32), jax.ShapeDtypeStruct((t, D_MODEL), BF16),
         jax.ShapeDtypeStruct((t, n), BF16), jax.ShapeDtypeStruct((1, D_MODEL), F32),
         jax.ShapeDtypeStruct((1, D_MODEL), F32)], args, sem=("arbitrary",), carry=carry)
    return outs if carry is None else (outs, recv)


def _mm_tn(name, a, b, *, tk, tmo, tno, carry=None):
    t, m = a.shape
    n = b.shape[1]
    nk = t // tk

    def body(a_ref, b_ref, o_ref, acc_ref):
        k = pl.program_id(2)
        p = _dot_tn(a_ref[...].astype(BF16), b_ref[...].astype(BF16))

        @pl.when(k == 0)
        def _():
            acc_ref[...] = p

        @pl.when(k > 0)
        def _():
            acc_ref[...] += p

        @pl.when(k == nk - 1)
        def _():
            o_ref[...] = acc_ref[...].astype(BF16)

    outs, recv = _call(
        body, name, (m // tmo, n // tno, nk),
        [pl.BlockSpec((tk, tmo), lambda i, j, k: (k, i)), pl.BlockSpec((tk, tno), lambda i, j, k: (k, j))],
        [pl.BlockSpec((tmo, tno), lambda i, j, k: (i, j))], [jax.ShapeDtypeStruct((m, n), BF16)], (a, b),
        scratch=[pltpu.VMEM((tmo, tno), F32)], sem=("parallel", "parallel", "arbitrary"), carry=carry)
    return outs[0] if carry is None else (outs[0], recv)


def _mm_nt_sum(name, pieces, offsets, w, resid, *, tm, carry=None):
    t = resid.shape[0]
    widths = [p.shape[1] for p in pieces]
    n_p = len(pieces)

    def body(*refs):
        p_refs, w_ref, r_ref, o_ref = refs[:n_p], refs[n_p], refs[n_p + 1], refs[n_p + 2]
        acc = ALPHA * r_ref[...]
        for p_ref, off, wd in zip(p_refs, offsets, widths):
            acc = acc + _dot_nt(p_ref[...], w_ref[:, off:off + wd])
        o_ref[...] = acc

    row = pl.BlockSpec((tm, D_MODEL), lambda i: (i, 0))
    outs, recv = _call(
        body, name, (t // tm,),
        [pl.BlockSpec((tm, wd), lambda i: (i, 0)) for wd in widths] + [_resident(w.shape), row],
        [row], [jax.ShapeDtypeStruct((t, D_MODEL), F32)], (*pieces, w, resid), sem=("parallel",), carry=carry)
    return outs[0] if carry is None else (outs[0], recv)


def _chunk_mask(rows):
    r = lax.broadcasted_iota(jnp.int32, (rows, rows), 0)
    c = lax.broadcasted_iota(jnp.int32, (rows, rows), 1)
    return ((r // CHUNK) == (c // CHUNK)) & (c <= r)


class _Lanes:
    def __init__(self, xs):
        self.xs = list(xs)

    def _with(self, other, f):
        if isinstance(other, _Lanes):
            return _Lanes([f(a, b) for a, b in zip(self.xs, other.xs)])
        return _Lanes([f(a, other) for a in self.xs])

    def __add__(self, o):
        return self._with(o, lambda a, b: a + b)

    def __radd__(self, o):
        return self._with(o, lambda a, b: b + a)

    def __sub__(self, o):
        return self._with(o, lambda a, b: a - b)

    def __rsub__(self, o):
        return self._with(o, lambda a, b: b - a)

    def __mul__(self, o):
        return self._with(o, lambda a, b: a * b)

    def __rmul__(self, o):
        return self._with(o, lambda a, b: b * a)

    def __truediv__(self, o):
        return self._with(o, lambda a, b: a / b)

    def __rtruediv__(self, o):
        return self._with(o, lambda a, b: b / a)

    def __neg__(self):
        return _Lanes([-a for a in self.xs])

    def __ge__(self, o):
        return self._with(o, lambda a, b: a >= b)

    def __getitem__(self, idx):
        return _Lanes([a[idx] for a in self.xs])

    def astype(self, dtype):
        return _Lanes([a.astype(dtype) for a in self.xs])

    def total(self):
        return functools.reduce(lambda a, b: a + b, self.xs)


def _lift(f):
    def g(*args, **kw):
        lanes = [a for a in args if isinstance(a, _Lanes)]
        if not lanes:
            return f(*args, **kw)
        return _Lanes([f(*[a.xs[i] if isinstance(a, _Lanes) else a for a in args], **kw)
                       for i in range(len(lanes[0].xs))])
    return g


def _concat(parts, axis):
    if isinstance(parts[0], _Lanes):
        return _Lanes([jnp.concatenate([p.xs[i] for p in parts], axis=axis) for i in range(len(parts[0].xs))])
    return jnp.concatenate(parts, axis=axis)


_exp, _log, _abs, _sqrt, _where = _lift(jnp.exp), _lift(jnp.log), _lift(jnp.abs), _lift(jnp.sqrt), _lift(jnp.where)
_sum, _mean, _rsqrt, _bcast = _lift(jnp.sum), _lift(jnp.mean), _lift(lax.rsqrt), _lift(jnp.broadcast_to)
_ldot, _ldot_nt, _ldot_tn = _lift(_dot), _lift(_dot_nt), _lift(_dot_tn)
_lsigmoid = _lift(_sigmoid)


def _mask_sum(mask_b, x, transpose=False):
    f = _ldot_tn if transpose else _ldot
    hi = x.astype(BF16)
    lo = (x - hi.astype(F32)).astype(BF16)
    return f(mask_b, hi) + f(mask_b, lo)


def _chunk_row(x, pos, rows):
    nc = rows // CHUNK

    def one(a):
        a3 = a.reshape(nc, CHUNK, HEAD_DIM)
        return jnp.broadcast_to(a3[:, pos:pos + 1, :], (nc, CHUNK, HEAD_DIM)).reshape(rows, HEAD_DIM)

    return _lift(one)(x)


def _chunk_total(x, rows):
    nc = rows // CHUNK

    def one(a):
        tot = jnp.sum(a.reshape(nc, CHUNK, HEAD_DIM), axis=1, keepdims=True)
        return jnp.broadcast_to(tot, (nc, CHUNK, HEAD_DIM)).reshape(rows, HEAD_DIM)

    return _lift(one)(x)


def _sigmoid_pair(x):
    e = _exp(-_abs(x))
    big = 1.0 / (1.0 + e)
    small = e * big
    pos = x >= 0.0
    return _where(pos, big, small), _where(pos, small, big)


def _hg_gates(q_raw, fl, lb, rows, mask):
    tri = mask.astype(BF16)
    sg, sg_neg = _sigmoid_pair(fl)
    forget = lb + (1.0 - lb) * sg
    k = (1.0 - lb) * sg_neg
    sq = _lsigmoid(q_raw)
    qs = q_raw * sq
    bc = _mask_sum(tri, _log(forget))
    bref = _chunk_row(bc, CHUNK // 2 - 1, rows)
    blast = _chunk_row(bc, CHUNK - 1, rows)
    return dict(tri=tri, sg=sg, sg_neg=sg_neg, forget=forget, k=k, sq=sq, qs=qs,
                e_a=_exp(bc - bref), e_b=_exp(bref - bc), e_q=_exp(bc), e_k=_exp(blast - bc),
                dec=_exp(blast))


HG_GROUP = 2


def _hg_lanes(bsz):
    return [(hh, slice(hh * HEAD_DIM, (hh + 1) * HEAD_DIM), b) for hh in range(HG_GROUP) for b in range(bsz)]


def _hg_read(ref, lanes):
    return _Lanes(ref[b, :, cs] for _, cs, b in lanes)


def _hg_write(ref, lanes, val):
    for (_, cs, b), a in zip(lanes, val.xs):
        ref[b, :, cs] = a


def _hgrn_fwd(proj, lb, nw, *, bsz, seq, carry=None):
    rows = min(ROWS_HG, seq)
    nt = seq // rows
    nc = rows // CHUNK
    t = bsz * seq

    lanes = _hg_lanes(bsz)

    def body(q_ref, f_ref, v_ref, g_ref, lb_ref, nw_ref, y_ref, o_ref, st_ref, s_scr):
        @pl.when(pl.program_id(1) == 0)
        def _():
            s_scr[...] = jnp.zeros_like(s_scr)

        mask = _chunk_mask(rows)
        lb_v = _Lanes(lb_ref[:, cs] for _, cs, _ in lanes)
        gt = _hg_gates(_hg_read(q_ref, lanes), _hg_read(f_ref, lanes), lb_v, rows, mask)
        v_b = _hg_read(v_ref, lanes).astype(BF16)
        a_b = (gt["qs"] * gt["e_a"]).astype(BF16)
        b_b = (gt["k"] * gt["e_b"]).astype(BF16)
        qi_b = (gt["qs"] * gt["e_q"]).astype(BF16)
        ko_b = (gt["k"] * gt["e_k"]).astype(BF16)
        scores = _where(mask, _ldot_nt(a_b, b_b), 0.0)
        o_intra = _ldot(scores.astype(BF16), v_b)

        s = _Lanes(s_scr[i] for i in range(len(lanes)))
        parts = []
        for n in range(nc):
            sl = slice(n * CHUNK, (n + 1) * CHUNK)
            s_b = s.astype(BF16)
            for (hh, _, b), a in zip(lanes, s_b.xs):
                st_ref[hh, b, n] = a
            parts.append(_ldot_nt(qi_b[sl], s_b))
            s = s * gt["dec"][n * CHUNK:n * CHUNK + 1] + _ldot_tn(v_b[sl], ko_b[sl])
        for i, a in enumerate(s.xs):
            s_scr[i] = a
        o = o_intra + _concat(parts, 0)
        _hg_write(o_ref, lanes, o)
        r = _rsqrt(_mean(o * o, axis=-1, keepdims=True) + RMS_EPS)
        g = _hg_read(g_ref, lanes)
        _hg_write(y_ref, lanes, (o * r * nw_ref[...] * (g * _lsigmoid(g))).astype(BF16))

    wide = HG_GROUP * HEAD_DIM

    def col(base):
        return pl.BlockSpec((bsz, rows, wide), lambda h, j: (0, j, base // wide + h))

    out_tile = pl.BlockSpec((bsz, rows, wide), lambda h, j: (0, j, h))
    p3 = proj.reshape(bsz, seq, IN_COLS)
    outs, recv = _call(
        body, "hgrn_fwd", (HEADS // HG_GROUP, nt),
        [col(C_HQ), col(C_HF), col(C_HI), col(C_HG),
         pl.BlockSpec((1, wide), lambda h, j: (0, h)), _resident((1, HEAD_DIM))],
        [out_tile, out_tile,
         pl.BlockSpec((HG_GROUP, bsz, nc, HEAD_DIM, HEAD_DIM), lambda h, j: (h, 0, j, 0, 0))],
        [jax.ShapeDtypeStruct((bsz, seq, WIDTH), BF16), jax.ShapeDtypeStruct((bsz, seq, WIDTH), F32),
         jax.ShapeDtypeStruct((HEADS, bsz, seq // CHUNK, HEAD_DIM, HEAD_DIM), BF16)],
        (p3, p3, p3, p3, lb, nw), scratch=[pltpu.VMEM((len(lanes), HEAD_DIM, HEAD_DIM), F32)],
        sem=("parallel", "arbitrary"), carry=carry)
    outs = [outs[0].reshape(t, WIDTH), outs[1].reshape(t, WIDTH), outs[2]]
    return outs if carry is None else (outs, recv)


def _hgrn_bwd(proj, lb, nw, o_pre, states, dy, *, bsz, seq, carry=None):
    rows = min(ROWS_HG, seq)
    nt = seq // rows
    nc = rows // CHUNK
    t = bsz * seq
    lanes = _hg_lanes(bsz)

    def body(q_ref, f_ref, v_ref, g_ref, lb_ref, nw_ref, o_ref, st_ref, dy_ref,
             dq_ref, df_ref, dv_ref, dg_ref, dlb_ref, dnw_ref, ds_scr):
        h, j = pl.program_id(0), pl.program_id(1)

        @pl.when(j == 0)
        def _():
            ds_scr[...] = jnp.zeros_like(ds_scr)
            dlb_ref[...] = jnp.zeros_like(dlb_ref)

        @pl.when((h == 0) & (j == 0))
        def _():
            dnw_ref[...] = jnp.zeros_like(dnw_ref)

        mask = _chunk_mask(rows)
        q_raw = _hg_read(q_ref, lanes)
        lb_v = _Lanes(lb_ref[:, cs] for _, cs, _ in lanes)
        gt = _hg_gates(q_raw, _hg_read(f_ref, lanes), lb_v, rows, mask)
        v_b = _hg_read(v_ref, lanes).astype(BF16)
        a_f = gt["qs"] * gt["e_a"]
        b_f = gt["k"] * gt["e_b"]
        qi_f = gt["qs"] * gt["e_q"]
        ko_f = gt["k"] * gt["e_k"]
        a_b, b_b, qi_b, ko_b = a_f.astype(BF16), b_f.astype(BF16), qi_f.astype(BF16), ko_f.astype(BF16)

        o = _hg_read(o_ref, lanes)
        nw_v = nw_ref[...]
        g = _hg_read(g_ref, lanes)
        dyv = _hg_read(dy_ref, lanes)
        r = _rsqrt(_mean(o * o, axis=-1, keepdims=True) + RMS_EPS)
        sgg = _lsigmoid(g)
        d_g = dyv * (o * r * nw_v) * (sgg * (1.0 + g * (1.0 - sgg)))
        d_on = dyv * (g * sgg)
        dnw_ref[...] += _sum(d_on * o * r, axis=0, keepdims=True).total()
        tt = d_on * nw_v
        d_o = r * tt - o * (r * r * r) * _mean(tt * o, axis=-1, keepdims=True)
        do_b = d_o.astype(BF16)

        sc_b = _where(mask, _ldot_nt(a_b, b_b), 0.0).astype(BF16)
        dsc_b = _where(mask, _ldot_nt(do_b, v_b), 0.0).astype(BF16)
        d_v = _ldot_tn(sc_b, do_b)
        d_a = _ldot(dsc_b, b_b)
        d_bm = _ldot_tn(dsc_b, a_b)

        ds = _Lanes(ds_scr[i] for i in range(len(lanes)))
        dqi_parts, dko_parts, dvi_parts, ddec_parts = [None] * nc, [None] * nc, [None] * nc, [None] * nc
        for n in reversed(range(nc)):
            sl = slice(n * CHUNK, (n + 1) * CHUNK)
            dec_n = gt["dec"][n * CHUNK:n * CHUNK + 1]
            ds_b = ds.astype(BF16)
            s_n = _Lanes(st_ref[hh, b, n] for hh, _, b in lanes)
            dqi_parts[n] = _ldot(do_b[sl], s_n)
            dko_parts[n] = _ldot(v_b[sl], ds_b)
            dvi_parts[n] = _ldot_nt(ko_b[sl], ds_b)
            d_dec = _sum(ds * s_n.astype(F32), axis=0, keepdims=True)
            ddec_parts[n] = _bcast(d_dec * dec_n, (CHUNK, HEAD_DIM))
            ds = ds * dec_n + _ldot_tn(do_b[sl], qi_b[sl])
        for i, a in enumerate(ds.xs):
            ds_scr[i] = a
        d_qi = _concat(dqi_parts, 0)
        d_ko = _concat(dko_parts, 0)
        d_v = d_v + _concat(dvi_parts, 0)

        d_qs = d_a * gt["e_a"] + d_qi * gt["e_q"]
        d_k = d_bm * gt["e_b"] + d_ko * gt["e_k"]
        t_a, t_b, t_q, t_k = d_a * a_f, d_bm * b_f, d_qi * qi_f, d_ko * ko_f
        d_bref = _chunk_total(t_b - t_a, rows)
        d_blast = _chunk_total(t_k, rows) + _concat(ddec_parts, 0)
        pos = lax.broadcasted_iota(jnp.int32, (rows, HEAD_DIM), 0) % CHUNK
        d_bc = (t_a - t_b + t_q - t_k + _where(pos == CHUNK // 2 - 1, d_bref, 0.0)
                + _where(pos == CHUNK - 1, d_blast, 0.0))
        d_logf = _mask_sum(gt["tri"], d_bc, transpose=True)

        sg, sg_neg = gt["sg"], gt["sg_neg"]
        inv_f = 1.0 / gt["forget"]
        common = (1.0 - lb_v) * sg * sg_neg
        d_fl = common * (d_logf * inv_f - d_k)
        d_lb = _sum(sg_neg * (d_logf * inv_f - d_k), axis=0, keepdims=True)
        for (_, cs, _), a in zip(lanes, d_lb.xs):
            dlb_ref[:, cs] += a
        sq = gt["sq"]
        _hg_write(dq_ref, lanes, (d_qs * (sq * (1.0 + q_raw * (1.0 - sq)))).astype(BF16))
        _hg_write(df_ref, lanes, d_fl.astype(BF16))
        _hg_write(dv_ref, lanes, d_v.astype(BF16))
        _hg_write(dg_ref, lanes, d_g.astype(BF16))

    wide = HG_GROUP * HEAD_DIM

    def col(base):
        return pl.BlockSpec((bsz, rows, wide), lambda h, j: (0, nt - 1 - j, base // wide + h))

    tile = pl.BlockSpec((bsz, rows, wide), lambda h, j: (0, nt - 1 - j, h))
    head_vec = pl.BlockSpec((1, wide), lambda h, j: (0, h))
    d_out = jax.ShapeDtypeStruct((bsz, seq, WIDTH), BF16)
    p3 = proj.reshape(bsz, seq, IN_COLS)
    outs, recv = _call(
        body, "hgrn_bwd", (HEADS // HG_GROUP, nt),
        [col(C_HQ), col(C_HF), col(C_HI), col(C_HG), head_vec, _resident((1, HEAD_DIM)), tile,
         pl.BlockSpec((HG_GROUP, bsz, nc, HEAD_DIM, HEAD_DIM), lambda h, j: (h, 0, nt - 1 - j, 0, 0)), tile],
        [tile, tile, tile, tile, head_vec, _resident((1, HEAD_DIM))],
        [d_out, d_out, d_out, d_out, jax.ShapeDtypeStruct((1, WIDTH), F32),
         jax.ShapeDtypeStruct((1, HEAD_DIM), F32)],
        (p3, p3, p3, p3, lb, nw, o_pre.reshape(bsz, seq, WIDTH), states, dy.reshape(bsz, seq, WIDTH)),
        scratch=[pltpu.VMEM((len(lanes), HEAD_DIM, HEAD_DIM), F32)],
        sem=("arbitrary", "arbitrary"), carry=carry)
    outs = [o_.reshape(t, WIDTH) for o_ in outs[:4]] + list(outs[4:])
    return outs if carry is None else (outs, recv)


def _mem_kv(mem2d, w_k, w_v):
    rows = mem2d.shape[0]

    def body(m_ref, wk_ref, wv_ref, k_ref, v_ref):
        m_b = m_ref[...].astype(BF16)
        k_ref[...] = _dot(m_b, wk_ref[...]).astype(BF16)
        v_ref[...] = _dot(m_b, wv_ref[...]).astype(BF16)

    return pl.pallas_call(
        body, name="mem_kv", grid=(rows // MEM_LEN,),
        in_specs=[pl.BlockSpec((MEM_LEN, D_MODEL), lambda i: (i, 0)), _resident((D_MODEL, WIDTH)),
                  _resident((D_MODEL, WIDTH))],
        out_specs=[pl.BlockSpec((MEM_LEN, WIDTH), lambda i: (i, 0))] * 2,
        out_shape=[jax.ShapeDtypeStruct((rows, WIDTH), BF16)] * 2,
        compiler_params=_params("parallel"),
    )(mem2d, w_k, w_v)


def _softmax_rows(s):
    m = jnp.max(s, axis=-1, keepdims=True)
    e = jnp.exp(s - m)
    return e / jnp.sum(e, axis=-1, keepdims=True)


def _attn_fwd(proj, mk, mv, *, tm, seq):
    t = proj.shape[0]
    per_b = seq // tm
    scale = HEAD_DIM ** -0.5

    def body(q_ref, k_ref, v_ref, y_ref):
        outs = []
        for h in range(HEADS):
            sl = slice(h * HEAD_DIM, (h + 1) * HEAD_DIM)
            s = _dot_nt(q_ref[:, sl].astype(BF16), k_ref[:, sl]) * scale
            p = _softmax_rows(s)
            outs.append(_dot(p.astype(BF16), v_ref[:, sl]))
        y_ref[...] = jnp.concatenate(outs, axis=-1).astype(BF16)

    kv = pl.BlockSpec((MEM_LEN, WIDTH), lambda i: (i // per_b, 0))
    return pl.pallas_call(
        body, name="attn_fwd", grid=(t // tm,),
        in_specs=[pl.BlockSpec((tm, WIDTH), lambda i: (i, C_MQ // WIDTH)), kv, kv],
        out_specs=pl.BlockSpec((tm, WIDTH), lambda i: (i, 0)),
        out_shape=jax.ShapeDtypeStruct((t, WIDTH), BF16),
        compiler_params=_params("parallel"),
    )(proj, mk, mv)


def _attn_bwd(proj, mk, mv, dy, *, tm, seq):
    t = proj.shape[0]
    per_b = seq // tm
    scale = HEAD_DIM ** -0.5

    def body(q_ref, k_ref, v_ref, dy_ref, dq_ref, dk_ref, dv_ref):
        i = pl.program_id(0)

        @pl.when(i % per_b == 0)
        def _():
            dk_ref[...] = jnp.zeros_like(dk_ref)
            dv_ref[...] = jnp.zeros_like(dv_ref)

        dqs, dks, dvs = [], [], []
        for h in range(HEADS):
            sl = slice(h * HEAD_DIM, (h + 1) * HEAD_DIM)
            q_b = q_ref[:, sl].astype(BF16)
            k_b, v_b = k_ref[:, sl], v_ref[:, sl]
            p = _softmax_rows(_dot_nt(q_b, k_b) * scale)
            dy_b = dy_ref[:, sl].astype(BF16)
            dp = _dot_nt(dy_b, v_b)
            dvs.append(_dot_tn(p.astype(BF16), dy_b))
            ds_b = (p * (dp - jnp.sum(dp * p, axis=-1, keepdims=True)) * scale).astype(BF16)
            dqs.append(_dot(ds_b, k_b))
            dks.append(_dot_tn(ds_b, q_b))
        dq_ref[...] = jnp.concatenate(dqs, axis=-1).astype(BF16)
        dk_ref[...] += jnp.concatenate(dks, axis=-1)
        dv_ref[...] += jnp.concatenate(dvs, axis=-1)

    kv = pl.BlockSpec((MEM_LEN, WIDTH), lambda i: (i // per_b, 0))
    tile = pl.BlockSpec((tm, WIDTH), lambda i: (i, 0))
    n_mem = mk.shape[0]
    return pl.pallas_call(
        body, name="attn_bwd", grid=(t // tm,),
        in_specs=[pl.BlockSpec((tm, WIDTH), lambda i: (i, C_MQ // WIDTH)), kv, kv, tile],
        out_specs=[tile, kv, kv],
        out_shape=[jax.ShapeDtypeStruct((t, WIDTH), BF16), jax.ShapeDtypeStruct((n_mem, WIDTH), F32),
                   jax.ShapeDtypeStruct((n_mem, WIDTH), F32)],
        compiler_params=_params("arbitrary"),
    )(proj, mk, mv, dy)


HALO = 8


def _shift_down(u, halo, k, row):
    out = pltpu.roll(u, k, 0)
    for m in range(k):
        out = jnp.where(row == m, halo[HALO - k + m:HALO - k + m + 1, :], out)
    return out


def _shift_up(u, halo, k, row, tm):
    out = pltpu.roll(u, tm - k, 0)
    for m in range(k):
        out = jnp.where(row == tm - k + m, halo[m:m + 1, :], out)
    return out


def _merge_fwd(proj, y_b, y_c, conv_w, w_branch, b_gate, *, tm, seq, carry=None):
    t = proj.shape[0]
    per_b = seq // tm
    hb = tm // HALO

    def body(cb_ref, cc_ref, ch_ref, cch_ref, chh_ref, ga_ref, gb_ref, gc_ref, yb_ref, yc_ref, cw_ref, wb_ref,
             bg_ref, ya_ref, pa_ref, pb_ref, pc_ref, mg_ref, sa_ref, sb_ref, sc_ref):
        i = pl.program_id(0)
        row = lax.broadcasted_iota(jnp.int32, (tm, WIDTH), 0)
        u = cc_ref[...] * ch_ref[...]
        halo = jnp.where(i % per_b == 0, 0.0, cch_ref[...] * chh_ref[...])
        cw = cw_ref[...]
        y = cw[0:1] * _shift_down(u, halo, 2, row) + cw[1:2] * _shift_down(u, halo, 1, row) + cw[2:3] * u
        ya_b = (cb_ref[...] * y).astype(BF16)
        ya_ref[...] = ya_b
        merged = None
        for idx, (y_in, g_ref, p_ref, s_ref) in enumerate(((ya_b, ga_ref, pa_ref, sa_ref),
                                                            (yb_ref[...], gb_ref, pb_ref, sb_ref),
                                                            (yc_ref[...], gc_ref, pc_ref, sc_ref))):
            p = _dot(y_in, wb_ref[idx])
            p_ref[...] = p.astype(BF16)
            sg = _sigmoid(g_ref[...] + bg_ref[:, idx * D_MODEL:(idx + 1) * D_MODEL])
            s_ref[...] = sg.astype(BF16)
            term = sg * p
            merged = term if merged is None else merged + term
        mg_ref[...] = merged.astype(BF16)

    def half(c):
        return pl.BlockSpec((tm, WIDTH), lambda i: (i, c // WIDTH))

    def prev(c):
        return pl.BlockSpec((HALO, WIDTH), lambda i: (jnp.maximum(i * hb - 1, 0), c // WIDTH))

    def gate(k):
        return pl.BlockSpec((tm, D_MODEL), lambda i: (i, C_GA // D_MODEL + k))

    tile512 = pl.BlockSpec((tm, WIDTH), lambda i: (i, 0))
    tile1k = pl.BlockSpec((tm, D_MODEL), lambda i: (i, 0))
    outs, recv = _call(
        body, "merge_fwd", (t // tm,),
        [half(C_CB), half(C_CC), half(C_CH), prev(C_CC), prev(C_CH), gate(0), gate(1), gate(2),
         tile512, tile512, _resident((CONV_K, WIDTH)), _resident((3, WIDTH, D_MODEL)), _resident((1, 3 * D_MODEL))],
        [tile512] + [tile1k] * 7,
        [jax.ShapeDtypeStruct((t, WIDTH), BF16)] + [jax.ShapeDtypeStruct((t, D_MODEL), BF16)] * 7,
        (proj, proj, proj, proj, proj, proj, proj, proj, y_b, y_c, conv_w, w_branch, b_gate),
        sem=("parallel",), carry=carry)
    return outs if carry is None else (outs, recv)


def _merge_bwd(dmerged, projections, gates, w_branch, *, tm):
    t = dmerged.shape[0]

    def body(dm_ref, pa_ref, pb_ref, pc_ref, sa_ref, sb_ref, sc_ref, wb_ref,
             dgt_ref, dpa_ref, dpb_ref, dpc_ref, dya_ref, dyb_ref, dyc_ref, dbg_ref):
        i = pl.program_id(0)

        @pl.when(i == 0)
        def _():
            dbg_ref[...] = jnp.zeros_like(dbg_ref)

        dm = dm_ref[...].astype(F32)
        for idx, (p_ref, s_ref, dp_ref, dy_ref) in enumerate(((pa_ref, sa_ref, dpa_ref, dya_ref),
                                                              (pb_ref, sb_ref, dpb_ref, dyb_ref),
                                                              (pc_ref, sc_ref, dpc_ref, dyc_ref))):
            cols = slice(idx * D_MODEL, (idx + 1) * D_MODEL)
            sg = s_ref[...].astype(F32)
            dp = dm * sg
            dp_b = dp.astype(BF16)
            dp_ref[...] = dp_b
            dgate = dp * p_ref[...].astype(F32) * (1.0 - sg)
            dgt_ref[:, cols] = dgate.astype(BF16)
            dbg_ref[:, cols] += jnp.sum(dgate, axis=0, keepdims=True)
            dy_ref[...] = _dot_nt(dp_b, wb_ref[idx])

    tile512 = pl.BlockSpec((tm, WIDTH), lambda i: (i, 0))
    tile1k = pl.BlockSpec((tm, D_MODEL), lambda i: (i, 0))
    return pl.pallas_call(
        body, name="merge_bwd", grid=(t // tm,),
        in_specs=[tile1k] * 7 + [_resident((3, WIDTH, D_MODEL))],
        out_specs=[pl.BlockSpec((tm, 3 * D_MODEL), lambda i: (i, 0)), tile1k, tile1k, tile1k,
                   tile512, tile512, tile512, _resident((1, 3 * D_MODEL))],
        out_shape=[jax.ShapeDtypeStruct((t, 3 * D_MODEL), BF16)] + [jax.ShapeDtypeStruct((t, D_MODEL), BF16)] * 3
                  + [jax.ShapeDtypeStruct((t, WIDTH), F32)] * 3 + [jax.ShapeDtypeStruct((1, 3 * D_MODEL), F32)],
        compiler_params=_params("arbitrary"),
    )(dmerged, *projections, *gates, w_branch)


def _conv_bwd(proj, dya, conv_w, *, tm, seq):
    t = proj.shape[0]
    per_b = seq // tm
    hb = tm // HALO
    last_blk = t // HALO - 1

    def body(cb_ref, cc_ref, ch_ref, cch_ref, chh_ref, dya_ref, cbn_ref, dyan_ref, cw_ref, d_ref, dcw_ref):
        i = pl.program_id(0)

        @pl.when(i == 0)
        def _():
            dcw_ref[...] = jnp.zeros_like(dcw_ref)

        row = lax.broadcasted_iota(jnp.int32, (tm, WIDTH), 0)
        cb, cc, ch = cb_ref[...], cc_ref[...], ch_ref[...]
        u = cc * ch
        halo = jnp.where(i % per_b == 0, 0.0, cch_ref[...] * chh_ref[...])
        u1 = _shift_down(u, halo, 1, row)
        u2 = _shift_down(u, halo, 2, row)
        cw = cw_ref[...]
        y = cw[0:1] * u2 + cw[1:2] * u1 + cw[2:3] * u
        dya = dya_ref[...]
        dy = dya * cb
        nxt = jnp.where(i % per_b == per_b - 1, 0.0, dyan_ref[...] * cbn_ref[...])
        du = cw[2:3] * dy + cw[1:2] * _shift_up(dy, nxt, 1, row, tm) + cw[0:1] * _shift_up(dy, nxt, 2, row, tm)
        d_ref[:, 0:WIDTH] = (dya * y).astype(BF16)
        d_ref[:, WIDTH:2 * WIDTH] = (du * ch).astype(BF16)
        d_ref[:, 2 * WIDTH:3 * WIDTH] = (du * cc).astype(BF16)
        dcw_ref[0:1, :] += jnp.sum(dy * u2, axis=0, keepdims=True)
        dcw_ref[1:2, :] += jnp.sum(dy * u1, axis=0, keepdims=True)
        dcw_ref[2:3, :] += jnp.sum(dy * u, axis=0, keepdims=True)

    def half(c):
        return pl.BlockSpec((tm, WIDTH), lambda i: (i, c // WIDTH))

    def prev(c):
        return pl.BlockSpec((HALO, WIDTH), lambda i: (jnp.maximum(i * hb - 1, 0), c // WIDTH))

    def nxt(c):
        return pl.BlockSpec((HALO, WIDTH), lambda i: (jnp.minimum((i + 1) * hb, last_blk), c // WIDTH))

    return pl.pallas_call(
        body, name="conv_bwd", grid=(t // tm,),
        in_specs=[half(C_CB), half(C_CC), half(C_CH), prev(C_CC), prev(C_CH),
                  pl.BlockSpec((tm, WIDTH), lambda i: (i, 0)), nxt(C_CB), nxt(0), _resident((CONV_K, WIDTH))],
        out_specs=[pl.BlockSpec((tm, 3 * WIDTH), lambda i: (i, 0)), _resident((CONV_K, WIDTH))],
        out_shape=[jax.ShapeDtypeStruct((t, 3 * WIDTH), BF16), jax.ShapeDtypeStruct((CONV_K, WIDTH), F32)],
        compiler_params=_params("arbitrary"),
    )(proj, proj, proj, proj, proj, dya, proj, dya, conv_w)


def _loss_head(y, target, *, tm):
    t = y.shape[0]

    def body(y_ref, t_ref, dy_ref, l_ref):
        @pl.when(pl.program_id(0) == 0)
        def _():
            l_ref[...] = jnp.zeros_like(l_ref)

        err = y_ref[...] - t_ref[...]
        dy_ref[...] = err * (1.0 / D_MODEL)
        per_row = jnp.sum(err * err, axis=-1, keepdims=True) * (1.0 / D_MODEL)
        l_ref[...] += 0.5 * jnp.sum(per_row, axis=0, keepdims=True)

    row = pl.BlockSpec((tm, D_MODEL), lambda i: (i, 0))
    return pl.pallas_call(
        body, name="loss_head", grid=(t // tm,),
        in_specs=[row, row], out_specs=[row, _resident((8, 128))],
        out_shape=[jax.ShapeDtypeStruct((t, D_MODEL), F32), jax.ShapeDtypeStruct((8, 128), F32)],
        compiler_params=_params("arbitrary"),
    )(y, target)


def _lb_softmax(lower_bounds):
    x = lower_bounds
    e = jnp.exp(x - jnp.max(x, axis=0, keepdims=True))
    return e / jnp.sum(e, axis=0, keepdims=True)


def _lb_fwd(lower_bounds):
    def body(x_ref, o_ref):
        s = _lb_softmax(x_ref[...])
        c = s[0:1]
        o_ref[0:1, :] = c - s[0:1]
        for l in range(1, DEPTH):
            c = c + s[l:l + 1]
            o_ref[l:l + 1, :] = c - s[0:1]

    return pl.pallas_call(body, name="lb_fwd", out_shape=jax.ShapeDtypeStruct(lower_bounds.shape, F32))(lower_bounds)


def _lb_bwd(lower_bounds, d_lb_all):
    def body(x_ref, d_ref, o_ref):
        s = _lb_softmax(x_ref[...])
        d = d_ref[...]
        rows = [jnp.zeros_like(d[0:1])]
        for j in range(1, DEPTH):
            acc = d[j:j + 1]
            for l in range(j + 1, DEPTH):
                acc = acc + d[l:l + 1]
            rows.append(acc)
        inner = rows[0] * s[0:1]
        for j in range(1, DEPTH):
            inner = inner + rows[j] * s[j:j + 1]
        for j in range(DEPTH):
            o_ref[j:j + 1, :] = s[j:j + 1] * (rows[j] - inner)

    return pl.pallas_call(body, name="lb_bwd", out_shape=jax.ShapeDtypeStruct(lower_bounds.shape, F32))(
        lower_bounds, d_lb_all)


def _adamw(w, g, m, v):
    m2 = ADAM_B1 * m + (1.0 - ADAM_B1) * g
    v2 = ADAM_B2 * v + (1.0 - ADAM_B2) * (g * g)
    m_hat = m2 / (1.0 - ADAM_B1 ** ADAM_STEP)
    v_hat = v2 / (1.0 - ADAM_B2 ** ADAM_STEP)
    delta = -ADAM_LR * (m_hat / (jnp.sqrt(v_hat) + ADAM_EPS) + ADAM_WD * w)
    return delta, m2, v2


def _adam_small(name, g, w, m, v):
    shape = w.shape
    flat = (-1, shape[-1])
    g2, w2, m2, v2 = (a.reshape(flat) for a in (g, w, m, v))

    def body(g_ref, w_ref, m_ref, v_ref, d_ref, mo_ref, vo_ref):
        d, mm, vv = _adamw(w_ref[...], g_ref[...], m_ref[...], v_ref[...])
        d_ref[...] = d
        mo_ref[...] = mm
        vo_ref[...] = vv

    outs = pl.pallas_call(body, name=name, out_shape=[jax.ShapeDtypeStruct(w2.shape, F32)] * 3)(g2, w2, m2, v2)
    return [o.reshape(shape) for o in outs]


def _adam_shard(name, recvs, w, m, v, *, tr):
    _, r, c = w.shape

    def body(*refs):
        rc, (w_ref, m_ref, v_ref), (g_ref, d_ref, mo_ref, vo_ref) = refs[:DEPTH], refs[DEPTH:DEPTH + 3], refs[DEPTH + 3:]
        layer = pl.program_id(0)
        for cand in range(DEPTH):
            @pl.when(layer == cand)
            def _():
                g = rc[cand][0].astype(F32)
                for d in range(1, N_DEV):
                    g = g + rc[cand][d].astype(F32)
                dl, mm, vv = _adamw(w_ref[...], g, m_ref[...], v_ref[...])
                g_ref[...] = g
                d_ref[...] = dl
                mo_ref[...] = mm
                vo_ref[...] = vv

    def recv_spec(cand):
        return pl.BlockSpec((N_DEV, tr, c), lambda l, i: (0, jnp.where(l == cand, i, 0), 0))

    tile = pl.BlockSpec((None, tr, c), lambda l, i: (l, i, 0))
    return pl.pallas_call(
        body, name=name, grid=(DEPTH, r // tr),
        in_specs=[recv_spec(cand) for cand in range(DEPTH)] + [tile] * 3,
        out_specs=[tile] * 4,
        out_shape=[jax.ShapeDtypeStruct(w.shape, F32)] * 4,
        compiler_params=_params("parallel", "parallel"),
    )(*recvs, w, m, v)


def _sum_devices(name, x):
    def body(x_ref, o_ref):
        acc = x_ref[0]
        for d in range(1, N_DEV):
            acc = acc + x_ref[d]
        o_ref[...] = acc

    return pl.pallas_call(body, name=name, out_shape=jax.ShapeDtypeStruct(x.shape[1:], x.dtype))(x)


SMALL = (("lower_bounds", 512), ("conv_w", CONV_K * WIDTH), ("hg_norm_w", HEAD_DIM), ("b_gate", 3 * D_MODEL),
         ("ln1_g", D_MODEL), ("ln1_b", D_MODEL), ("ln2_g", D_MODEL), ("ln2_b", D_MODEL))
SMALL_PER_LAYER = sum(n for _, n in SMALL)
SMALL_ROWS = 296


def _natural_cols(g):
    nd = g.ndim
    perm = tuple(range(1, nd - 1)) + (0, nd - 1)
    t = jnp.transpose(g, perm)
    return t.reshape(t.shape[:-2] + (t.shape[-2] * t.shape[-1],))


def _natural_rows(g):
    return g.reshape(g.shape[0] * g.shape[1], g.shape[2])


def _hosted(hosts, key, fn):
    if not hosts or key not in hosts:
        return fn(None)
    ex, hook = hosts[key]
    outs, recv = fn(ex)
    hook(recv)
    return outs


def _layer_fwd(cur, cur_b, mem2d, wl, *, bsz, seq, hosts=None):
    tm = min(512, seq)
    proj = _hosted(hosts, "in_proj", lambda c: _mm_nn("in_proj", cur_b, wl["w_in"], tm=min(1024, seq), tn=1024,
                                                       out_dtype=F32, carry=c))
    y_b, o_pre, states = _hosted(hosts, "hgrn_fwd", lambda c: _hgrn_fwd(proj, wl["lb"], wl["nw"], bsz=bsz, seq=seq,
                                                                         carry=c))
    mk, mv = _mem_kv(mem2d, wl["w_mk"], wl["w_mv"])
    y_c = _attn_fwd(proj, mk, mv, tm=tm, seq=seq)
    y_a, pa, pb, pc, merged, sga, sgb, sgc = _hosted(
        hosts, "merge_fwd", lambda c: _merge_fwd(proj, y_b, y_c, wl["conv"], wl["w_br"], wl["b_gate"], tm=tm,
                                                 seq=seq, carry=c))
    z1, x1, x1_b = _hosted(hosts, "wo_ln", lambda c: _linear_ln("wo_ln", merged, wl["w_o"], cur, wl["ln1_g"],
                                                                  wl["ln1_b"], tm=tm, carry=c))
    hid = _hosted(hosts, "mlp_up", lambda c: _mm_nn("mlp_up", x1_b, wl["w_up"], tm=tm, tn=1024, out_dtype=BF16,
                                                     relu2=True, carry=c))
    z2, x2, x2_b = _linear_ln("down_ln", hid, wl["w_down"], x1, wl["ln2_g"], wl["ln2_b"], tm=tm)
    return dict(x_b=cur_b, proj=proj, y_a=y_a, y_b=y_b, y_c=y_c, o_pre=o_pre, states=states, mk=mk, mv=mv,
                proj3=(pa, pb, pc), gates3=(sga, sgb, sgc), merged=merged, z1=z1, x1_b=x1_b, hid=hid, z2=z2, x2=x2,
                x2_b=x2_b)


def _layer_bwd(dcur, mem2d, s, wl, *, bsz, seq, plan=None):
    tm = min(512, seq)
    tk = min(1024, bsz * seq)
    g = {}

    def run(key, fn):
        made = plan[key](g) if plan and key in plan else None
        return _hosted({key: made} if made else None, key, fn)

    dz2, dz2_b, dhpre, d_ln2g, d_ln2b = run(
        "ln2_bwd_down", lambda c: _ln_bwd_mm_nt("ln2_bwd_down", dcur, s["z2"], wl["ln2_g"], wl["w_down"],
                                                s["hid"], tm=tm, tn=1024, carry=c))
    g["w_down"] = _mm_tn("grad_w_down", s["hid"], dz2_b, tk=tk, tmo=1024, tno=1024)
    dx1 = _mm_nt_sum("mlp_up_bwd", [dhpre], [0], wl["w_up"], dz2, tm=tm)
    g["w_up"] = run("grad_w_up", lambda c: _mm_tn("grad_w_up", s["x1_b"], dhpre, tk=tk, tmo=1024, tno=1024, carry=c))
    dz1, dz1_b, dmerged, d_ln1g, d_ln1b = _ln_bwd_mm_nt("ln1_bwd_wo", dx1, s["z1"], wl["ln1_g"], wl["w_o"],
                                                        tm=tm, tn=1024)
    g["w_o"] = _mm_tn("grad_w_o", s["merged"], dz1_b, tk=tk, tmo=1024, tno=1024)
    dgate, dpa, dpb, dpc, dya, dyb, dyc, d_bg = _merge_bwd(dmerged, s["proj3"], s["gates3"], wl["w_br"], tm=tm)
    g["w_br"] = jnp.stack([_mm_tn("grad_w_branch", yy, dp, tk=tk, tmo=512, tno=1024)
                           for yy, dp in ((s["y_a"], dpa), (s["y_b"], dpb), (s["y_c"], dpc))])
    d_conv, d_cw = _conv_bwd(s["proj"], dya, wl["conv"], tm=tm, seq=seq)
    dq, df, di, dg, d_lb, d_nw = run(
        "hgrn_bwd", lambda c: _hgrn_bwd(s["proj"], wl["lb"], wl["nw"], s["o_pre"], s["states"], dyb,
                                        bsz=bsz, seq=seq, carry=c))
    dmq, dmk, dmv = _attn_bwd(s["proj"], s["mk"], s["mv"], dyc, tm=tm, seq=seq)
    tkm = min(512, mem2d.shape[0])
    g["w_mk"] = _mm_tn("grad_w_mem", mem2d, dmk, tk=tkm, tmo=1024, tno=512)
    g["w_mv"] = _mm_tn("grad_w_mem", mem2d, dmv, tk=tkm, tmo=1024, tno=512)
    pieces = [d_conv, dq, df, di, dg, dmq, dgate]
    offsets = [C_CB, C_HQ, C_HF, C_HI, C_HG, C_MQ, C_GA]
    g["w_in"] = jnp.concatenate(
        [_mm_tn("grad_w_in_%d" % p.shape[1], s["x_b"], p, tk=tk, tmo=1024,
                tno=(1024 if p.shape[1] % 1024 == 0 else 512)) for p in pieces], axis=1)
    dx = run("in_proj_bwd", lambda c: _mm_nt_sum("in_proj_bwd", pieces, offsets, wl["w_in"], dz1,
                                                 tm=min(256, seq), carry=c))
    small = jnp.concatenate([d_lb[0], d_cw.reshape(-1), d_nw[0], d_bg[0], d_ln1g[0], d_ln1b[0], d_ln2g[0],
                             d_ln2b[0]])
    return dx, g, small


def kernel(x, mem, lower_bounds, w_in, conv_w, hg_norm_w, w_mem_k, w_mem_v, w_branch, b_gate, w_o, ln1_g, ln1_b, w_up, w_down, ln2_g, ln2_b, loss_target, m_lower_bounds, m_w_in, m_conv_w, m_hg_norm_w, m_w_mem_k, m_w_mem_v, m_w_branch, m_b_gate, m_w_o, m_ln1_g, m_ln1_b, m_w_up, m_w_down, m_ln2_g, m_ln2_b, v_lower_bounds, v_w_in, v_conv_w, v_hg_norm_w, v_w_mem_k, v_w_mem_v, v_w_branch, v_b_gate, v_w_o, v_ln1_g, v_ln1_b, v_w_up, v_w_down, v_ln2_g, v_ln2_b):
    bsz, seq, _ = x.shape
    t = bsz * seq
    me = _my_id()

    sh = dict(w_in=w_in.astype(BF16), w_mk=w_mem_k.astype(BF16), w_mv=w_mem_v.astype(BF16),
              w_br=w_branch.astype(BF16), w_o=w_o.astype(BF16), w_up=w_up.astype(BF16), w_down=w_down.astype(BF16))
    half_rows = D_MODEL // 2
    sh["w_in_a"], sh["w_in_b"] = sh["w_in"][:, :half_rows], sh["w_in"][:, half_rows:]
    natural = dict(w_in=_natural_cols, w_in_a=_natural_cols, w_in_b=_natural_cols, w_mk=_natural_rows,
                   w_mv=_natural_rows, w_br=_natural_cols, w_o=_natural_rows, w_up=_natural_cols,
                   w_down=_natural_rows)

    def gather_of(names, l):
        srcs = [sh[n][l] for n in names]
        return _Exchange(srcs, [_whole] * len(srcs), [s_.shape for s_ in srcs])

    def put(names, into):
        def hook(recv_):
            for n, r in zip(names, recv_):
                into[n] = natural[n](r)
        return hook

    lb_all = _lb_fwd(lower_bounds)
    layer_w = [dict(lb=lb_all[l][None], nw=hg_norm_w[l][None], b_gate=b_gate[l][None], ln1_g=ln1_g[l][None],
                    ln1_b=ln1_b[l][None], ln2_g=ln2_g[l][None], ln2_b=ln2_b[l][None]) for l in range(DEPTH)]
    first = ["w_in", "w_mk", "w_mv", "w_br", "w_o"]
    conv_shard = conv_w.reshape(DEPTH * CONV_K * (WIDTH // N_DEV) // 128, 128)
    ex0 = gather_of(first, 0)
    got = _exchange("gather_first", ex0.srcs + [conv_shard], [_whole] * (len(first) + 1),
                    [s_.shape for s_ in ex0.srcs] + [conv_shard.shape])
    put(first, layer_w[0])(got[:len(first)])
    conv_full = _natural_cols(got[-1].reshape(N_DEV, DEPTH, CONV_K, WIDTH // N_DEV))

    x2d = x.reshape(t, D_MODEL)
    mem2d = mem.reshape(bsz * MEM_LEN, D_MODEL)
    target2d = loss_target.reshape(t, D_MODEL)

    saved = []
    cur, cur_b = x2d, x2d.astype(BF16)
    for l in range(DEPTH):
        wl = layer_w[l]
        wl["conv"] = conv_full[l]
        if "w_in" not in wl:
            wl["w_in"] = jnp.concatenate([wl.pop("w_in_a"), wl.pop("w_in_b")], axis=0)
        hosts = {"in_proj": (gather_of(["w_up"], l), put(["w_up"], wl)),
                 "hgrn_fwd": (gather_of(["w_down"], l), put(["w_down"], wl))}
        if l + 1 < DEPTH:
            nxt = layer_w[l + 1]
            hosts["merge_fwd"] = (gather_of(["w_in_a"], l + 1), put(["w_in_a"], nxt))
            hosts["wo_ln"] = (gather_of(["w_mk", "w_mv", "w_o"], l + 1), put(["w_mk", "w_mv", "w_o"], nxt))
            hosts["mlp_up"] = (gather_of(["w_in_b", "w_br"], l + 1), put(["w_in_b", "w_br"], nxt))
        s = _layer_fwd(cur, cur_b, mem2d, wl, bsz=bsz, seq=seq, hosts=hosts)
        saved.append(s)
        cur, cur_b = s["x2"], s["x2_b"]

    dcur, loss_tile = _loss_head(cur, target2d, tm=min(512, seq))
    loss = lax.psum(loss_tile[0, 0], ("x", "y", "c"))

    slicer = dict(w_in=_cols(IN_COLS // N_DEV), w_mk=_rows(D_MODEL // N_DEV), w_mv=_rows(D_MODEL // N_DEV),
                  w_br=_cols(D_MODEL // N_DEV), w_o=_rows(D_MODEL // N_DEV), w_up=_cols(D_FF // N_DEV),
                  w_down=_rows(D_FF // N_DEV))
    recv = [dict() for _ in range(DEPTH)]

    def scatter_of(names, g, into):
        ex = _Exchange([g[n] for n in names], [slicer[n] for n in names], [sh[n].shape[1:] for n in names])
        return ex, lambda recv_: into.update(zip(names, recv_))

    small_rows = [None] * DEPTH
    prev = None
    for l in reversed(range(DEPTH)):
        plan = {"grad_w_up": lambda g, l=l: scatter_of(["w_down"], g, recv[l]),
                "hgrn_bwd": lambda g, l=l: scatter_of(["w_up", "w_o", "w_br"], g, recv[l])}
        if l == 0:
            plan["in_proj_bwd"] = lambda g: scatter_of(["w_in", "w_mk", "w_mv"], g, recv[0])
        else:
            plan["in_proj_bwd"] = lambda g, l=l: scatter_of(["w_in"], g, recv[l])
        if prev is not None:
            plan["ln2_bwd_down"] = lambda g, l=l, prev=prev: scatter_of(["w_mk", "w_mv"], prev, recv[l + 1])
        dcur, prev, small_rows[l] = _layer_bwd(dcur, mem2d, saved[l], layer_w[l], bsz=bsz, seq=seq, plan=plan)

    packed = jnp.concatenate(small_rows + [jnp.zeros((SMALL_ROWS * 128 - DEPTH * SMALL_PER_LAYER,), F32)])
    packed = packed.reshape(SMALL_ROWS, 128)
    all_small = _exchange("gather_small_grads", [packed], [_whole], [packed.shape])[0]
    summed = _sum_devices("sum_small_grads", all_small).reshape(-1)[:DEPTH * SMALL_PER_LAYER]
    summed = summed.reshape(DEPTH, SMALL_PER_LAYER)
    small_grads = {}
    off = 0
    for name, n in SMALL:
        small_grads[name] = summed[:, off:off + n]
        off += n
    small_grads["lower_bounds"] = _lb_bwd(lower_bounds, small_grads["lower_bounds"])
    conv_all = small_grads["conv_w"].reshape(DEPTH, CONV_K, WIDTH)
    small_grads["conv_w"] = lax.dynamic_slice_in_dim(conv_all, me * (WIDTH // N_DEV), WIDTH // N_DEV, axis=2)

    grads, deltas, new_m, new_v = {}, {}, {}, {}
    given = dict(lower_bounds=(lower_bounds, m_lower_bounds, v_lower_bounds), conv_w=(conv_w, m_conv_w, v_conv_w),
                 hg_norm_w=(hg_norm_w, m_hg_norm_w, v_hg_norm_w), b_gate=(b_gate, m_b_gate, v_b_gate),
                 ln1_g=(ln1_g, m_ln1_g, v_ln1_g), ln1_b=(ln1_b, m_ln1_b, v_ln1_b),
                 ln2_g=(ln2_g, m_ln2_g, v_ln2_g), ln2_b=(ln2_b, m_ln2_b, v_ln2_b))
    for name, (w_, m_, v_) in given.items():
        g_ = small_grads[name].reshape(w_.shape)
        grads[name] = g_
        deltas[name], new_m[name], new_v[name] = _adam_small("adam_" + name, g_, w_, m_, v_)

    big = dict(w_in=("w_in", w_in, m_w_in, v_w_in, 128), w_mem_k=("w_mk", w_mem_k, m_w_mem_k, v_w_mem_k, 128),
               w_mem_v=("w_mv", w_mem_v, m_w_mem_v, v_w_mem_v, 128),
               w_branch=("w_br", w_branch, m_w_branch, v_w_branch, 512), w_o=("w_o", w_o, m_w_o, v_w_o, 128),
               w_up=("w_up", w_up, m_w_up, v_w_up, 256), w_down=("w_down", w_down, m_w_down, v_w_down, 128))
    for name, (k, w_, m_, v_, tr) in big.items():
        shape = w_.shape
        flat = (DEPTH, -1, shape[-1])
        rc = [recv[l][k].reshape((N_DEV,) + w_.reshape(flat).shape[1:]) for l in range(DEPTH)]
        outs = _adam_shard("adam_" + name, rc, w_.reshape(flat), m_.reshape(flat), v_.reshape(flat), tr=tr)
        grads[name], deltas[name], new_m[name], new_v[name] = (o.reshape(shape) for o in outs)

    order = ["lower_bounds", "w_in", "conv_w", "hg_norm_w", "w_mem_k", "w_mem_v", "w_branch", "b_gate", "w_o",
             "ln1_g", "ln1_b", "w_up", "w_down", "ln2_g", "ln2_b"]
    return (loss, dcur.reshape(x.shape), *[grads[n] for n in order], *[deltas[n] for n in order],
            *[new_m[n] for n in order], *[new_v[n] for n in order])
```

```python
import functools

import jax
import jax.numpy as jnp
from jax import lax
from jax.experimental import pallas as pl
from jax.experimental.pallas import tpu as pltpu

F32 = jnp.float32
BF16 = jnp.bfloat16

N_DEV = 8
D_MODEL = 1024
DEPTH = 4
MEM_LEN = 256
CONV_K = 3
WIDTH = 512
HEADS = 4
HEAD_DIM = 128
CHUNK = 32
D_FF = 4 * D_MODEL
IN_COLS = 7168
ALPHA = (2.0 * DEPTH) ** 0.25
LN_EPS = 1e-5
RMS_EPS = 1e-6
ADAM_LR = 0.001
ADAM_B1 = 0.9
ADAM_B2 = 0.999
ADAM_EPS = 1e-08
ADAM_WD = 0.01
ADAM_STEP = 10

C_CB, C_CC, C_CH, C_HQ, C_HF, C_HI, C_HG, C_MQ, C_GA = 0, 512, 1024, 1536, 2048, 2560, 3072, 3584, 4096

ROWS_HG = 256
NT_DIMS = (((1,), (1,)), ((), ()))
TN_DIMS = (((0,), (0,)), ((), ()))
MESH = pl.DeviceIdType.MESH


def _dot(a, b):
    return jnp.dot(a, b, preferred_element_type=F32)


def _dot_nt(a, b):
    return lax.dot_general(a, b, NT_DIMS, preferred_element_type=F32)


def _dot_tn(a, b):
    return lax.dot_general(a, b, TN_DIMS, preferred_element_type=F32)


def _sigmoid(x):
    return 1.0 / (1.0 + jnp.exp(-x))


def _params(*sem):
    return pltpu.CompilerParams(dimension_semantics=sem)


def _resident(shape):
    nd = len(shape)
    return pl.BlockSpec(shape, lambda *_: (0,) * nd)


def _my_id():
    return 4 * lax.axis_index("x") + 2 * lax.axis_index("y") + lax.axis_index("c")


class _Exchange:
    def __init__(self, srcs, slicers, piece_shapes):
        self.srcs, self.slicers, self.n = list(srcs), list(slicers), len(srcs)
        any_spec = pl.BlockSpec(memory_space=pl.ANY)
        self.in_specs = [any_spec] * self.n
        self.out_specs = [any_spec] * self.n
        self.out_shape = [jax.ShapeDtypeStruct((N_DEV,) + tuple(s), a.dtype) for s, a in zip(piece_shapes, srcs)]
        self.scratch = [pltpu.SemaphoreType.DMA((self.n * N_DEV,)), pltpu.SemaphoreType.DMA((self.n * N_DEV,)),
                        pltpu.SemaphoreType.DMA((self.n,))]

    def _remote(self, ins, outs, sems, k, j, me):
        return pltpu.make_async_remote_copy(
            src_ref=self.slicers[k](ins[k], j), dst_ref=outs[k].at[me],
            send_sem=sems[0].at[k * N_DEV + j], recv_sem=sems[1].at[k * N_DEV + me],
            device_id=(j // 4, (j // 2) % 2, j % 2), device_id_type=MESH)

    def _local(self, ins, outs, sems, k, j, me):
        return pltpu.make_async_copy(self.slicers[k](ins[k], j), outs[k].at[me], sems[2].at[k])

    def start(self, ins, outs, sems):
        me = _my_id()
        for k in range(self.n):
            for j in range(N_DEV):
                @pl.when(j != me)
                def _():
                    self._remote(ins, outs, sems, k, j, me).start()

                @pl.when(j == me)
                def _():
                    self._local(ins, outs, sems, k, j, me).start()

    def wait(self, ins, outs, sems):
        me = _my_id()
        for k in range(self.n):
            for j in range(N_DEV):
                @pl.when(j != me)
                def _():
                    pltpu.make_async_remote_copy(
                        src_ref=self.slicers[k](ins[k], j), dst_ref=outs[k].at[j],
                        send_sem=sems[0].at[k * N_DEV + j], recv_sem=sems[1].at[k * N_DEV + j],
                        device_id=(j // 4, (j // 2) % 2, j % 2), device_id_type=MESH).wait_recv()
                    self._remote(ins, outs, sems, k, j, me).wait_send()

                @pl.when(j == me)
                def _():
                    self._local(ins, outs, sems, k, j, me).wait()


def _exchange(name, srcs, slicers, piece_shapes):
    ex = _Exchange(srcs, slicers, piece_shapes)

    def body(*refs):
        ins, outs, sems = refs[:ex.n], refs[ex.n:2 * ex.n], refs[2 * ex.n:]
        ex.start(ins, outs, sems)
        ex.wait(ins, outs, sems)

    return pl.pallas_call(
        body, name=name, in_specs=ex.in_specs, out_specs=ex.out_specs, out_shape=ex.out_shape,
        scratch_shapes=ex.scratch, compiler_params=pltpu.CompilerParams(has_side_effects=True),
    )(*ex.srcs)


def _call(body, name, grid, in_specs, out_specs, out_shape, args, scratch=(), sem=None, carry=None):
    n_in, n_out, n_scr = len(in_specs), len(out_specs), len(scratch)
    if carry is None:
        outs = pl.pallas_call(body, name=name, grid=grid, in_specs=in_specs, out_specs=out_specs,
                              out_shape=out_shape, scratch_shapes=list(scratch),
                              compiler_params=_params(*sem))(*args)
        return outs, None
    nc = carry.n

    def hosted(*refs):
        ins, c_in = refs[:n_in], refs[n_in:n_in + nc]
        outs = refs[n_in + nc:n_in + nc + n_out]
        c_out = refs[n_in + nc + n_out:n_in + 2 * nc + n_out]
        rest = refs[n_in + 2 * nc + n_out:]
        scr, sems = rest[:n_scr], rest[n_scr:]
        first, last = True, True
        for d, size in enumerate(grid):
            first = first & (pl.program_id(d) == 0)
            last = last & (pl.program_id(d) == size - 1)

        @pl.when(first)
        def _():
            carry.start(c_in, c_out, sems)

        body(*ins, *outs, *scr)

        @pl.when(last)
        def _():
            carry.wait(c_in, c_out, sems)

    outs = pl.pallas_call(
        hosted, name=name + "_x", grid=grid, in_specs=list(in_specs) + carry.in_specs,
        out_specs=list(out_specs) + carry.out_specs, out_shape=list(out_shape) + carry.out_shape,
        scratch_shapes=list(scratch) + carry.scratch,
        compiler_params=_params(*(["arbitrary"] * len(grid))))(*args, *carry.srcs)
    return outs[:n_out], outs[n_out:]


def _whole(ref, j):
    return ref


def _cols(width):
    return lambda ref, j: ref.at[(slice(None),) * (len(ref.shape) - 1) + (pl.ds(j * width, width),)]


def _rows(height):
    return lambda ref, j: ref.at[pl.ds(j * height, height)]


def _mm_nn(name, a, w, *, tm, tn, out_dtype, relu2=False, carry=None):
    t, k = a.shape
    n = w.shape[1]

    def body(a_ref, w_ref, o_ref):
        acc = _dot(a_ref[...].astype(BF16), w_ref[...])
        if relu2:
            r = jnp.maximum(acc, 0.0)
            acc = r * r
        o_ref[...] = acc.astype(out_dtype)

    outs, recv = _call(
        body, name, (t // tm, n // tn),
        [pl.BlockSpec((tm, k), lambda i, j: (i, 0)), pl.BlockSpec((k, tn), lambda i, j: (0, j))],
        [pl.BlockSpec((tm, tn), lambda i, j: (i, j))], [jax.ShapeDtypeStruct((t, n), out_dtype)], (a, w),
        sem=("parallel", "parallel"), carry=carry)
    return outs[0] if carry is None else (outs[0], recv)


def _layer_norm(z, g, b):
    mu = jnp.mean(z, axis=-1, keepdims=True)
    zc = z - mu
    var = jnp.mean(zc * zc, axis=-1, keepdims=True)
    return zc * lax.rsqrt(var + LN_EPS) * g + b


def _linear_ln(name, a, w, resid, g, b, *, tm, carry=None):
    t, k = a.shape

    def body(a_ref, w_ref, r_ref, g_ref, b_ref, z_ref, x_ref, xb_ref):
        z = ALPHA * r_ref[...] + _dot(a_ref[...], w_ref[...])
        z_ref[...] = z
        y = _layer_norm(z, g_ref[...], b_ref[...])
        x_ref[...] = y
        xb_ref[...] = y.astype(BF16)

    row = pl.BlockSpec((tm, D_MODEL), lambda i: (i, 0))
    outs, recv = _call(
        body, name, (t // tm,),
        [pl.BlockSpec((tm, k), lambda i: (i, 0)), _resident((k, D_MODEL)), row,
         _resident((1, D_MODEL)), _resident((1, D_MODEL))],
        [row, row, row],
        [jax.ShapeDtypeStruct((t, D_MODEL), F32)] * 2 + [jax.ShapeDtypeStruct((t, D_MODEL), BF16)],
        (a, w, resid, g, b), sem=("parallel",), carry=carry)
    return outs if carry is None else (outs, recv)


def _ln_bwd_mm_nt(name, dy, z, g, w, h=None, *, tm, tn, carry=None):
    t = dy.shape[0]
    n = w.shape[0]
    halves = [slice(0, tm // 2), slice(tm // 2, tm)]

    def body(*refs):
        if h is None:
            dy_ref, z_ref, g_ref, w_ref, dz_ref, dzb_ref, o_ref, dg_ref, db_ref = refs
        else:
            dy_ref, z_ref, g_ref, w_ref, h_ref, dz_ref, dzb_ref, o_ref, dg_ref, db_ref = refs

        @pl.when(pl.program_id(0) == 0)
        def _():
            dg_ref[...] = jnp.zeros_like(dg_ref)
            db_ref[...] = jnp.zeros_like(db_ref)

        zv = _Lanes(z_ref[s, :] for s in halves)
        dyv = _Lanes(dy_ref[s, :] for s in halves)
        mu = _mean(zv, axis=-1, keepdims=True)
        zc = zv - mu
        rstd = _rsqrt(_mean(zc * zc, axis=-1, keepdims=True) + LN_EPS)
        xh = zc * rstd
        gdy = dyv * g_ref[...]
        m1 = _mean(gdy, axis=-1, keepdims=True)
        m2 = _mean(gdy * xh, axis=-1, keepdims=True)
        dz = rstd * (gdy - m1 - xh * m2)
        dz_b = dz.astype(BF16)
        for s, a, a_b in zip(halves, dz.xs, dz_b.xs):
            dz_ref[s, :] = a
            dzb_ref[s, :] = a_b
        dg_ref[...] += _sum(dyv * xh, axis=0, keepdims=True).total()
        db_ref[...] += _sum(dyv, axis=0, keepdims=True).total()
        for c in range(n // tn):
            cols = slice(c * tn, (c + 1) * tn)
            acc = _ldot_nt(dz_b, w_ref[cols, :])
            if h is not None:
                acc = acc * (2.0 * _sqrt(_Lanes(h_ref[s, cols] for s in halves).astype(F32)))
            for s, a in zip(halves, acc.xs):
                o_ref[s, cols] = a.astype(BF16)

    row = pl.BlockSpec((tm, D_MODEL), lambda i: (i, 0))
    vec = _resident((1, D_MODEL))
    tile = pl.BlockSpec((tm, n), lambda i: (i, 0))
    in_specs = [row, row, vec, _resident((n, D_MODEL))]
    args = [dy, z, g, w]
    if h is not None:
        in_specs.append(tile)
        args.append(h)
    outs, recv = _call(
        body, name, (t // tm,), in_specs, [row, row, tile, vec, vec],
        [jax.ShapeDtypeStruct((t, D_MODEL), F32), jax.ShapeDtypeStruct((t, D_MODEL), BF16),
         jax.ShapeDtypeStruct((t, n), BF16), jax.ShapeDtypeStruct((1, D_MODEL), F32),
         jax.ShapeDtypeStruct((1, D_MODEL), F32)], args, sem=("arbitrary",), carry=carry)
    return outs if carry is None else (outs, recv)


def _mm_tn(name, a, b, *, tk, tmo, tno, carry=None):
    t, m = a.shape
    n = b.shape[1]
    nk = t // tk

    def body(a_ref, b_ref, o_ref, acc_ref):
        k = pl.program_id(2)
        p = _dot_tn(a_ref[...].astype(BF16), b_ref[...].astype(BF16))

        @pl.when(k == 0)
        def _():
            acc_ref[...] = p

        @pl.when(k > 0)
        def _():
            acc_ref[...] += p

        @pl.when(k == nk - 1)
        def _():
            o_ref[...] = acc_ref[...].astype(BF16)

    outs, recv = _call(
        body, name, (m // tmo, n // tno, nk),
        [pl.BlockSpec((tk, tmo), lambda i, j, k: (k, i)), pl.BlockSpec((tk, tno), lambda i, j, k: (k, j))],
        [pl.BlockSpec((tmo, tno), lambda i, j, k: (i, j))], [jax.ShapeDtypeStruct((m, n), BF16)], (a, b),
        scratch=[pltpu.VMEM((tmo, tno), F32)], sem=("parallel", "parallel", "arbitrary"), carry=carry)
    return outs[0] if carry is None else (outs[0], recv)


def _mm_nt_sum(name, pieces, offsets, w, resid, *, tm, carry=None):
    t = resid.shape[0]
    widths = [p.shape[1] for p in pieces]
    n_p = len(pieces)

    def body(*refs):
        p_refs, w_ref, r_ref, o_ref = refs[:n_p], refs[n_p], refs[n_p + 1], refs[n_p + 2]
        acc = ALPHA * r_ref[...]
        for p_ref, off, wd in zip(p_refs, offsets, widths):
            acc = acc + _dot_nt(p_ref[...], w_ref[:, off:off + wd])
        o_ref[...] = acc

    row = pl.BlockSpec((tm, D_MODEL), lambda i: (i, 0))
    outs, recv = _call(
        body, name, (t // tm,),
        [pl.BlockSpec((tm, wd), lambda i: (i, 0)) for wd in widths] + [_resident(w.shape), row],
        [row], [jax.ShapeDtypeStruct((t, D_MODEL), F32)], (*pieces, w, resid), sem=("parallel",), carry=carry)
    return outs[0] if carry is None else (outs[0], recv)


def _chunk_mask(rows):
    r = lax.broadcasted_iota(jnp.int32, (rows, rows), 0)
    c = lax.broadcasted_iota(jnp.int32, (rows, rows), 1)
    return ((r // CHUNK) == (c // CHUNK)) & (c <= r)


class _Lanes:
    def __init__(self, xs):
        self.xs = list(xs)

    def _with(self, other, f):
        if isinstance(other, _Lanes):
            return _Lanes([f(a, b) for a, b in zip(self.xs, other.xs)])
        return _Lanes([f(a, other) for a in self.xs])

    def __add__(self, o):
        return self._with(o, lambda a, b: a + b)

    def __radd__(self, o):
        return self._with(o, lambda a, b: b + a)

    def __sub__(self, o):
        return self._with(o, lambda a, b: a - b)

    def __rsub__(self, o):
        return self._with(o, lambda a, b: b - a)

    def __mul__(self, o):
        return self._with(o, lambda a, b: a * b)

    def __rmul__(self, o):
        return self._with(o, lambda a, b: b * a)

    def __truediv__(self, o):
        return self._with(o, lambda a, b: a / b)

    def __rtruediv__(self, o):
        return self._with(o, lambda a, b: b / a)

    def __neg__(self):
        return _Lanes([-a for a in self.xs])

    def __ge__(self, o):
        return self._with(o, lambda a, b: a >= b)

    def __getitem__(self, idx):
        return _Lanes([a[idx] for a in self.xs])

    def astype(self, dtype):
        return _Lanes([a.astype(dtype) for a in self.xs])

    def total(self):
        return functools.reduce(lambda a, b: a + b, self.xs)


def _lift(f):
    def g(*args, **kw):
        lanes = [a for a in args if isinstance(a, _Lanes)]
        if not lanes:
            return f(*args, **kw)
        return _Lanes([f(*[a.xs[i] if isinstance(a, _Lanes) else a for a in args], **kw)
                       for i in range(len(lanes[0].xs))])
    return g


def _concat(parts, axis):
    if isinstance(parts[0], _Lanes):
        return _Lanes([jnp.concatenate([p.xs[i] for p in parts], axis=axis) for i in range(len(parts[0].xs))])
    return jnp.concatenate(parts, axis=axis)


_exp, _log, _abs, _sqrt, _where = _lift(jnp.exp), _lift(jnp.log), _lift(jnp.abs), _lift(jnp.sqrt), _lift(jnp.where)
_sum, _mean, _rsqrt, _bcast = _lift(jnp.sum), _lift(jnp.mean), _lift(lax.rsqrt), _lift(jnp.broadcast_to)
_ldot, _ldot_nt, _ldot_tn = _lift(_dot), _lift(_dot_nt), _lift(_dot_tn)
_lsigmoid = _lift(_sigmoid)


def _mask_sum(mask_b, x, transpose=False):
    f = _ldot_tn if transpose else _ldot
    hi = x.astype(BF16)
    lo = (x - hi.astype(F32)).astype(BF16)
    return f(mask_b, hi) + f(mask_b, lo)


def _chunk_row(x, pos, rows):
    nc = rows // CHUNK

    def one(a):
        a3 = a.reshape(nc, CHUNK, HEAD_DIM)
        return jnp.broadcast_to(a3[:, pos:pos + 1, :], (nc, CHUNK, HEAD_DIM)).reshape(rows, HEAD_DIM)

    return _lift(one)(x)


def _chunk_total(x, rows):
    nc = rows // CHUNK

    def one(a):
        tot = jnp.sum(a.reshape(nc, CHUNK, HEAD_DIM), axis=1, keepdims=True)
        return jnp.broadcast_to(tot, (nc, CHUNK, HEAD_DIM)).reshape(rows, HEAD_DIM)

    return _lift(one)(x)


def _sigmoid_pair(x):
    e = _exp(-_abs(x))
    big = 1.0 / (1.0 + e)
    small = e * big
    pos = x >= 0.0
    return _where(pos, big, small), _where(pos, small, big)


def _hg_gates(q_raw, fl, lb, rows, mask):
    tri = mask.astype(BF16)
    sg, sg_neg = _sigmoid_pair(fl)
    forget = lb + (1.0 - lb) * sg
    k = (1.0 - lb) * sg_neg
    sq = _lsigmoid(q_raw)
    qs = q_raw * sq
    bc = _mask_sum(tri, _log(forget))
    bref = _chunk_row(bc, CHUNK // 2 - 1, rows)
    blast = _chunk_row(bc, CHUNK - 1, rows)
    return dict(tri=tri, sg=sg, sg_neg=sg_neg, forget=forget, k=k, sq=sq, qs=qs,
                e_a=_exp(bc - bref), e_b=_exp(bref - bc), e_q=_exp(bc), e_k=_exp(blast - bc),
                dec=_exp(blast))


HG_GROUP = 4


def _hg_lanes(bsz):
    return [(hh, slice(hh * HEAD_DIM, (hh + 1) * HEAD_DIM), b) for hh in range(HG_GROUP) for b in range(bsz)]


def _hg_read(ref, lanes):
    return _Lanes(ref[b, :, cs] for _, cs, b in lanes)


def _hg_write(ref, lanes, val):
    for (_, cs, b), a in zip(lanes, val.xs):
        ref[b, :, cs] = a


def _hgrn_fwd(proj, lb, nw, *, bsz, seq, carry=None):
    rows = min(ROWS_HG, seq)
    nt = seq // rows
    nc = rows // CHUNK
    t = bsz * seq

    lanes = _hg_lanes(bsz)

    def body(q_ref, f_ref, v_ref, g_ref, lb_ref, nw_ref, y_ref, o_ref, st_ref, s_scr):
        @pl.when(pl.program_id(1) == 0)
        def _():
            s_scr[...] = jnp.zeros_like(s_scr)

        mask = _chunk_mask(rows)
        lb_v = _Lanes(lb_ref[:, cs] for _, cs, _ in lanes)
        gt = _hg_gates(_hg_read(q_ref, lanes), _hg_read(f_ref, lanes), lb_v, rows, mask)
        v_b = _hg_read(v_ref, lanes).astype(BF16)
        a_b = (gt["qs"] * gt["e_a"]).astype(BF16)
        b_b = (gt["k"] * gt["e_b"]).astype(BF16)
        qi_b = (gt["qs"] * gt["e_q"]).astype(BF16)
        ko_b = (gt["k"] * gt["e_k"]).astype(BF16)
        scores = _where(mask, _ldot_nt(a_b, b_b), 0.0)
        o_intra = _ldot(scores.astype(BF16), v_b)

        s = _Lanes(s_scr[i] for i in range(len(lanes)))
        parts = []
        for n in range(nc):
            sl = slice(n * CHUNK, (n + 1) * CHUNK)
            s_b = s.astype(BF16)
            for (hh, _, b), a in zip(lanes, s_b.xs):
                st_ref[hh, b, n] = a
            parts.append(_ldot_nt(qi_b[sl], s_b))
            s = s * gt["dec"][n * CHUNK:n * CHUNK + 1] + _ldot_tn(v_b[sl], ko_b[sl])
        for i, a in enumerate(s.xs):
            s_scr[i] = a
        o = o_intra + _concat(parts, 0)
        _hg_write(o_ref, lanes, o)
        r = _rsqrt(_mean(o * o, axis=-1, keepdims=True) + RMS_EPS)
        g = _hg_read(g_ref, lanes)
        _hg_write(y_ref, lanes, (o * r * nw_ref[...] * (g * _lsigmoid(g))).astype(BF16))

    wide = HG_GROUP * HEAD_DIM

    def col(base):
        return pl.BlockSpec((bsz, rows, wide), lambda h, j: (0, j, base // wide + h))

    out_tile = pl.BlockSpec((bsz, rows, wide), lambda h, j: (0, j, h))
    p3 = proj.reshape(bsz, seq, IN_COLS)
    outs, recv = _call(
        body, "hgrn_fwd", (HEADS // HG_GROUP, nt),
        [col(C_HQ), col(C_HF), col(C_HI), col(C_HG),
         pl.BlockSpec((1, wide), lambda h, j: (0, h)), _resident((1, HEAD_DIM))],
        [out_tile, out_tile,
         pl.BlockSpec((HG_GROUP, bsz, nc, HEAD_DIM, HEAD_DIM), lambda h, j: (h, 0, j, 0, 0))],
        [jax.ShapeDtypeStruct((bsz, seq, WIDTH), BF16), jax.ShapeDtypeStruct((bsz, seq, WIDTH), F32),
         jax.ShapeDtypeStruct((HEADS, bsz, seq // CHUNK, HEAD_DIM, HEAD_DIM), BF16)],
        (p3, p3, p3, p3, lb, nw), scratch=[pltpu.VMEM((len(lanes), HEAD_DIM, HEAD_DIM), F32)],
        sem=("parallel", "arbitrary"), carry=carry)
    outs = [outs[0].reshape(t, WIDTH), outs[1].reshape(t, WIDTH), outs[2]]
    return outs if carry is None else (outs, recv)


def _hgrn_bwd(proj, lb, nw, o_pre, states, dy, *, bsz, seq, carry=None):
    rows = min(ROWS_HG, seq)
    nt = seq // rows
    nc = rows // CHUNK
    t = bsz * seq
    lanes = _hg_lanes(bsz)

    def body(q_ref, f_ref, v_ref, g_ref, lb_ref, nw_ref, o_ref, st_ref, dy_ref,
             dq_ref, df_ref, dv_ref, dg_ref, dlb_ref, dnw_ref, ds_scr):
        h, j = pl.program_id(0), pl.program_id(1)

        @pl.when(j == 0)
        def _():
            ds_scr[...] = jnp.zeros_like(ds_scr)
            dlb_ref[...] = jnp.zeros_like(dlb_ref)

        @pl.when((h == 0) & (j == 0))
        def _():
            dnw_ref[...] = jnp.zeros_like(dnw_ref)

        mask = _chunk_mask(rows)
        q_raw = _hg_read(q_ref, lanes)
        lb_v = _Lanes(lb_ref[:, cs] for _, cs, _ in lanes)
        gt = _hg_gates(q_raw, _hg_read(f_ref, lanes), lb_v, rows, mask)
        v_b = _hg_read(v_ref, lanes).astype(BF16)
        a_f = gt["qs"] * gt["e_a"]
        b_f = gt["k"] * gt["e_b"]
        qi_f = gt["qs"] * gt["e_q"]
        ko_f = gt["k"] * gt["e_k"]
        a_b, b_b, qi_b, ko_b = a_f.astype(BF16), b_f.astype(BF16), qi_f.astype(BF16), ko_f.astype(BF16)

        o = _hg_read(o_ref, lanes)
        nw_v = nw_ref[...]
        g = _hg_read(g_ref, lanes)
        dyv = _hg_read(dy_ref, lanes)
        r = _rsqrt(_mean(o * o, axis=-1, keepdims=True) + RMS_EPS)
        sgg = _lsigmoid(g)
        d_g = dyv * (o * r * nw_v) * (sgg * (1.0 + g * (1.0 - sgg)))
        d_on = dyv * (g * sgg)
        dnw_ref[...] += _sum(d_on * o * r, axis=0, keepdims=True).total()
        tt = d_on * nw_v
        d_o = r * tt - o * (r * r * r) * _mean(tt * o, axis=-1, keepdims=True)
        do_b = d_o.astype(BF16)

        sc_b = _where(mask, _ldot_nt(a_b, b_b), 0.0).astype(BF16)
        dsc_b = _where(mask, _ldot_nt(do_b, v_b), 0.0).astype(BF16)
        d_v = _ldot_tn(sc_b, do_b)
        d_a = _ldot(dsc_b, b_b)
        d_bm = _ldot_tn(dsc_b, a_b)

        ds = _Lanes(ds_scr[i] for i in range(len(lanes)))
        dqi_parts, dko_parts, dvi_parts, ddec_parts = [None] * nc, [None] * nc, [None] * nc, [None] * nc
        for n in reversed(range(nc)):
            sl = slice(n * CHUNK, (n + 1) * CHUNK)
            dec_n = gt["dec"][n * CHUNK:n * CHUNK + 1]
            ds_b = ds.astype(BF16)
            s_n = _Lanes(st_ref[hh, b, n] for hh, _, b in lanes)
            dqi_parts[n] = _ldot(do_b[sl], s_n)
            dko_parts[n] = _ldot(v_b[sl], ds_b)
            dvi_parts[n] = _ldot_nt(ko_b[sl], ds_b)
            d_dec = _sum(ds * s_n.astype(F32), axis=0, keepdims=True)
            ddec_parts[n] = _bcast(d_dec * dec_n, (CHUNK, HEAD_DIM))
            ds = ds * dec_n + _ldot_tn(do_b[sl], qi_b[sl])
        for i, a in enumerate(ds.xs):
            ds_scr[i] = a
        d_qi = _concat(dqi_parts, 0)
        d_ko = _concat(dko_parts, 0)
        d_v = d_v + _concat(dvi_parts, 0)

        d_qs = d_a * gt["e_a"] + d_qi * gt["e_q"]
        d_k = d_bm * gt["e_b"] + d_ko * gt["e_k"]
        t_a, t_b, t_q, t_k = d_a * a_f, d_bm * b_f, d_qi * qi_f, d_ko * ko_f
        d_bref = _chunk_total(t_b - t_a, rows)
        d_blast = _chunk_total(t_k, rows) + _concat(ddec_parts, 0)
        pos = lax.broadcasted_iota(jnp.int32, (rows, HEAD_DIM), 0) % CHUNK
        d_bc = (t_a - t_b + t_q - t_k + _where(pos == CHUNK // 2 - 1, d_bref, 0.0)
                + _where(pos == CHUNK - 1, d_blast, 0.0))
        d_logf = _mask_sum(gt["tri"], d_bc, transpose=True)

        sg, sg_neg = gt["sg"], gt["sg_neg"]
        inv_f = 1.0 / gt["forget"]
        common = (1.0 - lb_v) * sg * sg_neg
        d_fl = common * (d_logf * inv_f - d_k)
        d_lb = _sum(sg_neg * (d_logf * inv_f - d_k), axis=0, keepdims=True)
        for (_, cs, _), a in zip(lanes, d_lb.xs):
            dlb_ref[:, cs] += a
        sq = gt["sq"]
        _hg_write(dq_ref, lanes, (d_qs * (sq * (1.0 + q_raw * (1.0 - sq)))).astype(BF16))
        _hg_write(df_ref, lanes, d_fl.astype(BF16))
        _hg_write(dv_ref, lanes, d_v.astype(BF16))
        _hg_write(dg_ref, lanes, d_g.astype(BF16))

    wide = HG_GROUP * HEAD_DIM

    def col(base):
        return pl.BlockSpec((bsz, rows, wide), lambda h, j: (0, nt - 1 - j, base // wide + h))

    tile = pl.BlockSpec((bsz, rows, wide), lambda h, j: (0, nt - 1 - j, h))
    head_vec = pl.BlockSpec((1, wide), lambda h, j: (0, h))
    d_out = jax.ShapeDtypeStruct((bsz, seq, WIDTH), BF16)
    p3 = proj.reshape(bsz, seq, IN_COLS)
    outs, recv = _call(
        body, "hgrn_bwd", (HEADS // HG_GROUP, nt),
        [col(C_HQ), col(C_HF), col(C_HI), col(C_HG), head_vec, _resident((1, HEAD_DIM)), tile,
         pl.BlockSpec((HG_GROUP, bsz, nc, HEAD_DIM, HEAD_DIM), lambda h, j: (h, 0, nt - 1 - j, 0, 0)), tile],
        [tile, tile, tile, tile, head_vec, _resident((1, HEAD_DIM))],
        [d_out, d_out, d_out, d_out, jax.ShapeDtypeStruct((1, WIDTH), F32),
         jax.ShapeDtypeStruct((1, HEAD_DIM), F32)],
        (p3, p3, p3, p3, lb, nw, o_pre.reshape(bsz, seq, WIDTH), states, dy.reshape(bsz, seq, WIDTH)),
        scratch=[pltpu.VMEM((len(lanes), HEAD_DIM, HEAD_DIM), F32)],
        sem=("arbitrary", "arbitrary"), carry=carry)
    outs = [o_.reshape(t, WIDTH) for o_ in outs[:4]] + list(outs[4:])
    return outs if carry is None else (outs, recv)


def _mem_kv(mem2d, w_k, w_v):
    rows = mem2d.shape[0]

    def body(m_ref, wk_ref, wv_ref, k_ref, v_ref):
        m_b = m_ref[...].astype(BF16)
        k_ref[...] = _dot(m_b, wk_ref[...]).astype(BF16)
        v_ref[...] = _dot(m_b, wv_ref[...]).astype(BF16)

    return pl.pallas_call(
        body, name="mem_kv", grid=(rows // MEM_LEN,),
        in_specs=[pl.BlockSpec((MEM_LEN, D_MODEL), lambda i: (i, 0)), _resident((D_MODEL, WIDTH)),
                  _resident((D_MODEL, WIDTH))],
        out_specs=[pl.BlockSpec((MEM_LEN, WIDTH), lambda i: (i, 0))] * 2,
        out_shape=[jax.ShapeDtypeStruct((rows, WIDTH), BF16)] * 2,
        compiler_params=_params("parallel"),
    )(mem2d, w_k, w_v)


def _softmax_rows(s):
    m = _lift(jnp.max)(s, axis=-1, keepdims=True)
    e = _exp(s - m)
    return e / _sum(e, axis=-1, keepdims=True)


def _attn_fwd(proj, mk, mv, *, tm, seq):
    t = proj.shape[0]
    per_b = seq // tm
    scale = HEAD_DIM ** -0.5

    def body(q_ref, k_ref, v_ref, y_ref):
        heads = [slice(h * HEAD_DIM, (h + 1) * HEAD_DIM) for h in range(HEADS)]
        q_b = _Lanes(q_ref[:, sl] for sl in heads).astype(BF16)
        p = _softmax_rows(_ldot_nt(q_b, _Lanes(k_ref[:, sl] for sl in heads)) * scale)
        out = _ldot(p.astype(BF16), _Lanes(v_ref[:, sl] for sl in heads))
        y_ref[...] = jnp.concatenate(out.xs, axis=-1).astype(BF16)

    kv = pl.BlockSpec((MEM_LEN, WIDTH), lambda i: (i // per_b, 0))
    return pl.pallas_call(
        body, name="attn_fwd", grid=(t // tm,),
        in_specs=[pl.BlockSpec((tm, WIDTH), lambda i: (i, C_MQ // WIDTH)), kv, kv],
        out_specs=pl.BlockSpec((tm, WIDTH), lambda i: (i, 0)),
        out_shape=jax.ShapeDtypeStruct((t, WIDTH), BF16),
        compiler_params=_params("parallel"),
    )(proj, mk, mv)


def _attn_bwd(proj, mk, mv, dy, *, tm, seq):
    t = proj.shape[0]
    per_b = seq // tm
    scale = HEAD_DIM ** -0.5

    def body(q_ref, k_ref, v_ref, dy_ref, dq_ref, dk_ref, dv_ref):
        i = pl.program_id(0)

        @pl.when(i % per_b == 0)
        def _():
            dk_ref[...] = jnp.zeros_like(dk_ref)
            dv_ref[...] = jnp.zeros_like(dv_ref)

        heads = [slice(h * HEAD_DIM, (h + 1) * HEAD_DIM) for h in range(HEADS)]
        q_b = _Lanes(q_ref[:, sl] for sl in heads).astype(BF16)
        k_b, v_b = _Lanes(k_ref[:, sl] for sl in heads), _Lanes(v_ref[:, sl] for sl in heads)
        p = _softmax_rows(_ldot_nt(q_b, k_b) * scale)
        dy_b = _Lanes(dy_ref[:, sl] for sl in heads).astype(BF16)
        dp = _ldot_nt(dy_b, v_b)
        d_v = _ldot_tn(p.astype(BF16), dy_b)
        ds_b = (p * (dp - _sum(dp * p, axis=-1, keepdims=True)) * scale).astype(BF16)
        dq_ref[...] = jnp.concatenate(_ldot(ds_b, k_b).xs, axis=-1).astype(BF16)
        dk_ref[...] += jnp.concatenate(_ldot_tn(ds_b, q_b).xs, axis=-1)
        dv_ref[...] += jnp.concatenate(d_v.xs, axis=-1)

    kv = pl.BlockSpec((MEM_LEN, WIDTH), lambda i: (i // per_b, 0))
    tile = pl.BlockSpec((tm, WIDTH), lambda i: (i, 0))
    n_mem = mk.shape[0]
    return pl.pallas_call(
        body, name="attn_bwd", grid=(t // tm,),
        in_specs=[pl.BlockSpec((tm, WIDTH), lambda i: (i, C_MQ // WIDTH)), kv, kv, tile],
        out_specs=[tile, kv, kv],
        out_shape=[jax.ShapeDtypeStruct((t, WIDTH), BF16), jax.ShapeDtypeStruct((n_mem, WIDTH), F32),
                   jax.ShapeDtypeStruct((n_mem, WIDTH), F32)],
        compiler_params=_params("arbitrary"),
    )(proj, mk, mv, dy)


HALO = 8


def _shift_down(u, halo, k, row):
    out = pltpu.roll(u, k, 0)
    for m in range(k):
        out = jnp.where(row == m, halo[HALO - k + m:HALO - k + m + 1, :], out)
    return out


def _shift_up(u, halo, k, row, tm):
    out = pltpu.roll(u, tm - k, 0)
    for m in range(k):
        out = jnp.where(row == tm - k + m, halo[m:m + 1, :], out)
    return out


def _merge_fwd(proj, y_b, y_c, conv_w, w_branch, b_gate, *, tm, seq, carry=None):
    t = proj.shape[0]
    per_b = seq // tm
    hb = tm // HALO

    def body(cb_ref, cc_ref, ch_ref, cch_ref, chh_ref, ga_ref, gb_ref, gc_ref, yb_ref, yc_ref, cw_ref, wb_ref,
             bg_ref, ya_ref, pa_ref, pb_ref, pc_ref, mg_ref, sa_ref, sb_ref, sc_ref):
        i = pl.program_id(0)
        row = lax.broadcasted_iota(jnp.int32, (tm, WIDTH), 0)
        u = cc_ref[...] * ch_ref[...]
        halo = jnp.where(i % per_b == 0, 0.0, cch_ref[...] * chh_ref[...])
        cw = cw_ref[...]
        y = cw[0:1] * _shift_down(u, halo, 2, row) + cw[1:2] * _shift_down(u, halo, 1, row) + cw[2:3] * u
        ya_b = (cb_ref[...] * y).astype(BF16)
        ya_ref[...] = ya_b
        merged = None
        for idx, (y_in, g_ref, p_ref, s_ref) in enumerate(((ya_b, ga_ref, pa_ref, sa_ref),
                                                            (yb_ref[...], gb_ref, pb_ref, sb_ref),
                                                            (yc_ref[...], gc_ref, pc_ref, sc_ref))):
            p = _dot(y_in, wb_ref[idx])
            p_ref[...] = p.astype(BF16)
            sg = _sigmoid(g_ref[...] + bg_ref[:, idx * D_MODEL:(idx + 1) * D_MODEL])
            s_ref[...] = sg.astype(BF16)
            term = sg * p
            merged = term if merged is None else merged + term
        mg_ref[...] = merged.astype(BF16)

    def half(c):
        return pl.BlockSpec((tm, WIDTH), lambda i: (i, c // WIDTH))

    def prev(c):
        return pl.BlockSpec((HALO, WIDTH), lambda i: (jnp.maximum(i * hb - 1, 0), c // WIDTH))

    def gate(k):
        return pl.BlockSpec((tm, D_MODEL), lambda i: (i, C_GA // D_MODEL + k))

    tile512 = pl.BlockSpec((tm, WIDTH), lambda i: (i, 0))
    tile1k = pl.BlockSpec((tm, D_MODEL), lambda i: (i, 0))
    outs, recv = _call(
        body, "merge_fwd", (t // tm,),
        [half(C_CB), half(C_CC), half(C_CH), prev(C_CC), prev(C_CH), gate(0), gate(1), gate(2),
         tile512, tile512, _resident((CONV_K, WIDTH)), _resident((3, WIDTH, D_MODEL)), _resident((1, 3 * D_MODEL))],
        [tile512] + [tile1k] * 7,
        [jax.ShapeDtypeStruct((t, WIDTH), BF16)] + [jax.ShapeDtypeStruct((t, D_MODEL), BF16)] * 7,
        (proj, proj, proj, proj, proj, proj, proj, proj, y_b, y_c, conv_w, w_branch, b_gate),
        sem=("parallel",), carry=carry)
    return outs if carry is None else (outs, recv)


def _merge_bwd(dmerged, projections, gates, w_branch, *, tm):
    t = dmerged.shape[0]

    def body(dm_ref, pa_ref, pb_ref, pc_ref, sa_ref, sb_ref, sc_ref, wb_ref,
             dgt_ref, dpa_ref, dpb_ref, dpc_ref, dya_ref, dyb_ref, dyc_ref, dbg_ref):
        i = pl.program_id(0)

        @pl.when(i == 0)
        def _():
            dbg_ref[...] = jnp.zeros_like(dbg_ref)

        dm = dm_ref[...].astype(F32)
        for idx, (p_ref, s_ref, dp_ref, dy_ref) in enumerate(((pa_ref, sa_ref, dpa_ref, dya_ref),
                                                              (pb_ref, sb_ref, dpb_ref, dyb_ref),
                                                              (pc_ref, sc_ref, dpc_ref, dyc_ref))):
            cols = slice(idx * D_MODEL, (idx + 1) * D_MODEL)
            sg = s_ref[...].astype(F32)
            dp = dm * sg
            dp_b = dp.astype(BF16)
            dp_ref[...] = dp_b
            dgate = dp * p_ref[...].astype(F32) * (1.0 - sg)
            dgt_ref[:, cols] = dgate.astype(BF16)
            dbg_ref[:, cols] += jnp.sum(dgate, axis=0, keepdims=True)
            dy_ref[...] = _dot_nt(dp_b, wb_ref[idx])

    tile512 = pl.BlockSpec((tm, WIDTH), lambda i: (i, 0))
    tile1k = pl.BlockSpec((tm, D_MODEL), lambda i: (i, 0))
    return pl.pallas_call(
        body, name="merge_bwd", grid=(t // tm,),
        in_specs=[tile1k] * 7 + [_resident((3, WIDTH, D_MODEL))],
        out_specs=[pl.BlockSpec((tm, 3 * D_MODEL), lambda i: (i, 0)), tile1k, tile1k, tile1k,
                   tile512, tile512, tile512, _resident((1, 3 * D_MODEL))],
        out_shape=[jax.ShapeDtypeStruct((t, 3 * D_MODEL), BF16)] + [jax.ShapeDtypeStruct((t, D_MODEL), BF16)] * 3
                  + [jax.ShapeDtypeStruct((t, WIDTH), F32)] * 3 + [jax.ShapeDtypeStruct((1, 3 * D_MODEL), F32)],
        compiler_params=_params("arbitrary"),
    )(dmerged, *projections, *gates, w_branch)


def _conv_bwd(proj, dya, conv_w, *, tm, seq):
    t = proj.shape[0]
    per_b = seq // tm
    hb = tm // HALO
    last_blk = t // HALO - 1

    def body(cb_ref, cc_ref, ch_ref, cch_ref, chh_ref, dya_ref, cbn_ref, dyan_ref, cw_ref, d_ref, dcw_ref):
        i = pl.program_id(0)

        @pl.when(i == 0)
        def _():
            dcw_ref[...] = jnp.zeros_like(dcw_ref)

        row = lax.broadcasted_iota(jnp.int32, (tm, WIDTH), 0)
        cb, cc, ch = cb_ref[...], cc_ref[...], ch_ref[...]
        u = cc * ch
        halo = jnp.where(i % per_b == 0, 0.0, cch_ref[...] * chh_ref[...])
        u1 = _shift_down(u, halo, 1, row)
        u2 = _shift_down(u, halo, 2, row)
        cw = cw_ref[...]
        y = cw[0:1] * u2 + cw[1:2] * u1 + cw[2:3] * u
        dya = dya_ref[...]
        dy = dya * cb
        nxt = jnp.where(i % per_b == per_b - 1, 0.0, dyan_ref[...] * cbn_ref[...])
        du = cw[2:3] * dy + cw[1:2] * _shift_up(dy, nxt, 1, row, tm) + cw[0:1] * _shift_up(dy, nxt, 2, row, tm)
        d_ref[:, 0:WIDTH] = (dya * y).astype(BF16)
        d_ref[:, WIDTH:2 * WIDTH] = (du * ch).astype(BF16)
        d_ref[:, 2 * WIDTH:3 * WIDTH] = (du * cc).astype(BF16)
        dcw_ref[0:1, :] += jnp.sum(dy * u2, axis=0, keepdims=True)
        dcw_ref[1:2, :] += jnp.sum(dy * u1, axis=0, keepdims=True)
        dcw_ref[2:3, :] += jnp.sum(dy * u, axis=0, keepdims=True)

    def half(c):
        return pl.BlockSpec((tm, WIDTH), lambda i: (i, c // WIDTH))

    def prev(c):
        return pl.BlockSpec((HALO, WIDTH), lambda i: (jnp.maximum(i * hb - 1, 0), c // WIDTH))

    def nxt(c):
        return pl.BlockSpec((HALO, WIDTH), lambda i: (jnp.minimum((i + 1) * hb, last_blk), c // WIDTH))

    return pl.pallas_call(
        body, name="conv_bwd", grid=(t // tm,),
        in_specs=[half(C_CB), half(C_CC), half(C_CH), prev(C_CC), prev(C_CH),
                  pl.BlockSpec((tm, WIDTH), lambda i: (i, 0)), nxt(C_CB), nxt(0), _resident((CONV_K, WIDTH))],
        out_specs=[pl.BlockSpec((tm, 3 * WIDTH), lambda i: (i, 0)), _resident((CONV_K, WIDTH))],
        out_shape=[jax.ShapeDtypeStruct((t, 3 * WIDTH), BF16), jax.ShapeDtypeStruct((CONV_K, WIDTH), F32)],
        compiler_params=_params("arbitrary"),
    )(proj, proj, proj, proj, proj, dya, proj, dya, conv_w)


def _loss_head(y, target, *, tm):
    t = y.shape[0]

    def body(y_ref, t_ref, dy_ref, l_ref):
        @pl.when(pl.program_id(0) == 0)
        def _():
            l_ref[...] = jnp.zeros_like(l_ref)

        err = y_ref[...] - t_ref[...]
        dy_ref[...] = err * (1.0 / D_MODEL)
        per_row = jnp.sum(err * err, axis=-1, keepdims=True) * (1.0 / D_MODEL)
        l_ref[...] += 0.5 * jnp.sum(per_row, axis=0, keepdims=True)

    row = pl.BlockSpec((tm, D_MODEL), lambda i: (i, 0))
    return pl.pallas_call(
        body, name="loss_head", grid=(t // tm,),
        in_specs=[row, row], out_specs=[row, _resident((8, 128))],
        out_shape=[jax.ShapeDtypeStruct((t, D_MODEL), F32), jax.ShapeDtypeStruct((8, 128), F32)],
        compiler_params=_params("arbitrary"),
    )(y, target)


def _lb_softmax(lower_bounds):
    x = lower_bounds
    e = jnp.exp(x - jnp.max(x, axis=0, keepdims=True))
    return e / jnp.sum(e, axis=0, keepdims=True)


def _lb_fwd(lower_bounds):
    def body(x_ref, o_ref):
        s = _lb_softmax(x_ref[...])
        c = s[0:1]
        o_ref[0:1, :] = c - s[0:1]
        for l in range(1, DEPTH):
            c = c + s[l:l + 1]
            o_ref[l:l + 1, :] = c - s[0:1]

    return pl.pallas_call(body, name="lb_fwd", out_shape=jax.ShapeDtypeStruct(lower_bounds.shape, F32))(lower_bounds)


def _lb_bwd(lower_bounds, d_lb_all):
    def body(x_ref, d_ref, o_ref):
        s = _lb_softmax(x_ref[...])
        d = d_ref[...]
        rows = [jnp.zeros_like(d[0:1])]
        for j in range(1, DEPTH):
            acc = d[j:j + 1]
            for l in range(j + 1, DEPTH):
                acc = acc + d[l:l + 1]
            rows.append(acc)
        inner = rows[0] * s[0:1]
        for j in range(1, DEPTH):
            inner = inner + rows[j] * s[j:j + 1]
        for j in range(DEPTH):
            o_ref[j:j + 1, :] = s[j:j + 1] * (rows[j] - inner)

    return pl.pallas_call(body, name="lb_bwd", out_shape=jax.ShapeDtypeStruct(lower_bounds.shape, F32))(
        lower_bounds, d_lb_all)


def _adamw(w, g, m, v):
    m2 = ADAM_B1 * m + (1.0 - ADAM_B1) * g
    v2 = ADAM_B2 * v + (1.0 - ADAM_B2) * (g * g)
    m_hat = m2 / (1.0 - ADAM_B1 ** ADAM_STEP)
    v_hat = v2 / (1.0 - ADAM_B2 ** ADAM_STEP)
    delta = -ADAM_LR * (m_hat / (jnp.sqrt(v_hat) + ADAM_EPS) + ADAM_WD * w)
    return delta, m2, v2


def _adam_small(name, g, w, m, v):
    shape = w.shape
    flat = (-1, shape[-1])
    g2, w2, m2, v2 = (a.reshape(flat) for a in (g, w, m, v))

    def body(g_ref, w_ref, m_ref, v_ref, d_ref, mo_ref, vo_ref):
        d, mm, vv = _adamw(w_ref[...], g_ref[...], m_ref[...], v_ref[...])
        d_ref[...] = d
        mo_ref[...] = mm
        vo_ref[...] = vv

    outs = pl.pallas_call(body, name=name, out_shape=[jax.ShapeDtypeStruct(w2.shape, F32)] * 3)(g2, w2, m2, v2)
    return [o.reshape(shape) for o in outs]


def _adam_shard(name, recvs, w, m, v, *, tr):
    _, r, c = w.shape

    def body(*refs):
        rc, (w_ref, m_ref, v_ref), (g_ref, d_ref, mo_ref, vo_ref) = refs[:DEPTH], refs[DEPTH:DEPTH + 3], refs[DEPTH + 3:]
        layer = pl.program_id(0)
        for cand in range(DEPTH):
            @pl.when(layer == cand)
            def _():
                g = rc[cand][0].astype(F32)
                for d in range(1, N_DEV):
                    g = g + rc[cand][d].astype(F32)
                dl, mm, vv = _adamw(w_ref[...], g, m_ref[...], v_ref[...])
                g_ref[...] = g
                d_ref[...] = dl
                mo_ref[...] = mm
                vo_ref[...] = vv

    def recv_spec(cand):
        return pl.BlockSpec((N_DEV, tr, c), lambda l, i: (0, jnp.where(l == cand, i, 0), 0))

    tile = pl.BlockSpec((None, tr, c), lambda l, i: (l, i, 0))
    return pl.pallas_call(
        body, name=name, grid=(DEPTH, r // tr),
        in_specs=[recv_spec(cand) for cand in range(DEPTH)] + [tile] * 3,
        out_specs=[tile] * 4,
        out_shape=[jax.ShapeDtypeStruct(w.shape, F32)] * 4,
        compiler_params=_params("parallel", "parallel"),
    )(*recvs, w, m, v)


def _sum_devices(name, x):
    def body(x_ref, o_ref):
        acc = x_ref[0]
        for d in range(1, N_DEV):
            acc = acc + x_ref[d]
        o_ref[...] = acc

    return pl.pallas_call(body, name=name, out_shape=jax.ShapeDtypeStruct(x.shape[1:], x.dtype))(x)


SMALL = (("lower_bounds", 512), ("conv_w", CONV_K * WIDTH), ("hg_norm_w", HEAD_DIM), ("b_gate", 3 * D_MODEL),
         ("ln1_g", D_MODEL), ("ln1_b", D_MODEL), ("ln2_g", D_MODEL), ("ln2_b", D_MODEL))
SMALL_PER_LAYER = sum(n for _, n in SMALL)
SMALL_ROWS = 296


def _natural_cols(g):
    nd = g.ndim
    perm = tuple(range(1, nd - 1)) + (0, nd - 1)
    t = jnp.transpose(g, perm)
    return t.reshape(t.shape[:-2] + (t.shape[-2] * t.shape[-1],))


def _natural_rows(g):
    return g.reshape(g.shape[0] * g.shape[1], g.shape[2])


def _hosted(hosts, key, fn):
    if not hosts or key not in hosts:
        return fn(None)
    ex, hook = hosts[key]
    outs, recv = fn(ex)
    hook(recv)
    return outs


def _layer_fwd(cur, cur_b, mem2d, wl, *, bsz, seq, hosts=None):
    tm = min(512, seq)
    proj = _hosted(hosts, "in_proj", lambda c: _mm_nn("in_proj", cur_b, wl["w_in"], tm=min(1024, seq), tn=1024,
                                                       out_dtype=F32, carry=c))
    y_b, o_pre, states = _hosted(hosts, "hgrn_fwd", lambda c: _hgrn_fwd(proj, wl["lb"], wl["nw"], bsz=bsz, seq=seq,
                                                                         carry=c))
    mk, mv = _mem_kv(mem2d, wl["w_mk"], wl["w_mv"])
    y_c = _attn_fwd(proj, mk, mv, tm=tm, seq=seq)
    y_a, pa, pb, pc, merged, sga, sgb, sgc = _hosted(
        hosts, "merge_fwd", lambda c: _merge_fwd(proj, y_b, y_c, wl["conv"], wl["w_br"], wl["b_gate"], tm=tm,
                                                 seq=seq, carry=c))
    z1, x1, x1_b = _hosted(hosts, "wo_ln", lambda c: _linear_ln("wo_ln", merged, wl["w_o"], cur, wl["ln1_g"],
                                                                  wl["ln1_b"], tm=tm, carry=c))
    hid = _hosted(hosts, "mlp_up", lambda c: _mm_nn("mlp_up", x1_b, wl["w_up"], tm=tm, tn=1024, out_dtype=BF16,
                                                     relu2=True, carry=c))
    z2, x2, x2_b = _linear_ln("down_ln", hid, wl["w_down"], x1, wl["ln2_g"], wl["ln2_b"], tm=tm)
    return dict(x_b=cur_b, proj=proj, y_a=y_a, y_b=y_b, y_c=y_c, o_pre=o_pre, states=states, mk=mk, mv=mv,
                proj3=(pa, pb, pc), gates3=(sga, sgb, sgc), merged=merged, z1=z1, x1_b=x1_b, hid=hid, z2=z2, x2=x2,
                x2_b=x2_b)


def _layer_bwd(dcur, mem2d, s, wl, *, bsz, seq, plan=None):
    tm = min(512, seq)
    tk = min(1024, bsz * seq)
    g = {}

    def run(key, fn):
        made = plan[key](g) if plan and key in plan else None
        return _hosted({key: made} if made else None, key, fn)

    dz2, dz2_b, dhpre, d_ln2g, d_ln2b = run(
        "ln2_bwd_down", lambda c: _ln_bwd_mm_nt("ln2_bwd_down", dcur, s["z2"], wl["ln2_g"], wl["w_down"],
                                                s["hid"], tm=tm, tn=1024, carry=c))
    g["w_down"] = _mm_tn("grad_w_down", s["hid"], dz2_b, tk=tk, tmo=1024, tno=1024)
    dx1 = _mm_nt_sum("mlp_up_bwd", [dhpre], [0], wl["w_up"], dz2, tm=tm)
    g["w_up"] = run("grad_w_up", lambda c: _mm_tn("grad_w_up", s["x1_b"], dhpre, tk=tk, tmo=1024, tno=1024, carry=c))
    dz1, dz1_b, dmerged, d_ln1g, d_ln1b = _ln_bwd_mm_nt("ln1_bwd_wo", dx1, s["z1"], wl["ln1_g"], wl["w_o"],
                                                        tm=tm, tn=1024)
    g["w_o"] = _mm_tn("grad_w_o", s["merged"], dz1_b, tk=tk, tmo=1024, tno=1024)
    dgate, dpa, dpb, dpc, dya, dyb, dyc, d_bg = _merge_bwd(dmerged, s["proj3"], s["gates3"], wl["w_br"], tm=tm)
    g["w_br"] = jnp.stack([_mm_tn("grad_w_branch", yy, dp, tk=tk, tmo=512, tno=1024)
                           for yy, dp in ((s["y_a"], dpa), (s["y_b"], dpb), (s["y_c"], dpc))])
    d_conv, d_cw = _conv_bwd(s["proj"], dya, wl["conv"], tm=tm, seq=seq)
    dq, df, di, dg, d_lb, d_nw = run(
        "hgrn_bwd", lambda c: _hgrn_bwd(s["proj"], wl["lb"], wl["nw"], s["o_pre"], s["states"], dyb,
                                        bsz=bsz, seq=seq, carry=c))
    dmq, dmk, dmv = _attn_bwd(s["proj"], s["mk"], s["mv"], dyc, tm=tm, seq=seq)
    tkm = min(512, mem2d.shape[0])
    g["w_mk"] = _mm_tn("grad_w_mem", mem2d, dmk, tk=tkm, tmo=1024, tno=512)
    g["w_mv"] = _mm_tn("grad_w_mem", mem2d, dmv, tk=tkm, tmo=1024, tno=512)
    pieces = [d_conv, dq, df, di, dg, dmq, dgate]
    offsets = [C_CB, C_HQ, C_HF, C_HI, C_HG, C_MQ, C_GA]
    g["w_in"] = jnp.concatenate(
        [_mm_tn("grad_w_in_%d" % p.shape[1], s["x_b"], p, tk=tk, tmo=1024,
                tno=(1024 if p.shape[1] % 1024 == 0 else 512)) for p in pieces], axis=1)
    dx = run("in_proj_bwd", lambda c: _mm_nt_sum("in_proj_bwd", pieces, offsets, wl["w_in"], dz1,
                                                 tm=min(256, seq), carry=c))
    small = jnp.concatenate([d_lb[0], d_cw.reshape(-1), d_nw[0], d_bg[0], d_ln1g[0], d_ln1b[0], d_ln2g[0],
                             d_ln2b[0]])
    return dx, g, small


def kernel(x, mem, lower_bounds, w_in, conv_w, hg_norm_w, w_mem_k, w_mem_v, w_branch, b_gate, w_o, ln1_g, ln1_b, w_up, w_down, ln2_g, ln2_b, loss_target, m_lower_bounds, m_w_in, m_conv_w, m_hg_norm_w, m_w_mem_k, m_w_mem_v, m_w_branch, m_b_gate, m_w_o, m_ln1_g, m_ln1_b, m_w_up, m_w_down, m_ln2_g, m_ln2_b, v_lower_bounds, v_w_in, v_conv_w, v_hg_norm_w, v_w_mem_k, v_w_mem_v, v_w_branch, v_b_gate, v_w_o, v_ln1_g, v_ln1_b, v_w_up, v_w_down, v_ln2_g, v_ln2_b):
    bsz, seq, _ = x.shape
    t = bsz * seq
    me = _my_id()

    sh = dict(w_in=w_in.astype(BF16), w_mk=w_mem_k.astype(BF16), w_mv=w_mem_v.astype(BF16),
              w_br=w_branch.astype(BF16), w_o=w_o.astype(BF16), w_up=w_up.astype(BF16), w_down=w_down.astype(BF16))
    half_rows = D_MODEL // 2
    sh["w_in_a"], sh["w_in_b"] = sh["w_in"][:, :half_rows], sh["w_in"][:, half_rows:]
    natural = dict(w_in=_natural_cols, w_in_a=_natural_cols, w_in_b=_natural_cols, w_mk=_natural_rows,
                   w_mv=_natural_rows, w_br=_natural_cols, w_o=_natural_rows, w_up=_natural_cols,
                   w_down=_natural_rows)

    def gather_of(names, l):
        srcs = [sh[n][l] for n in names]
        return _Exchange(srcs, [_whole] * len(srcs), [s_.shape for s_ in srcs])

    def put(names, into):
        def hook(recv_):
            for n, r in zip(names, recv_):
                into[n] = natural[n](r)
        return hook

    lb_all = _lb_fwd(lower_bounds)
    layer_w = [dict(lb=lb_all[l][None], nw=hg_norm_w[l][None], b_gate=b_gate[l][None], ln1_g=ln1_g[l][None],
                    ln1_b=ln1_b[l][None], ln2_g=ln2_g[l][None], ln2_b=ln2_b[l][None]) for l in range(DEPTH)]
    first = ["w_in"]
    conv_shard = conv_w.reshape(DEPTH * CONV_K * (WIDTH // N_DEV) // 128, 128)
    ex0 = gather_of(first, 0)
    got = _exchange("gather_first", ex0.srcs + [conv_shard], [_whole] * (len(first) + 1),
                    [s_.shape for s_ in ex0.srcs] + [conv_shard.shape])
    put(first, layer_w[0])(got[:len(first)])
    conv_full = _natural_cols(got[-1].reshape(N_DEV, DEPTH, CONV_K, WIDTH // N_DEV))

    x2d = x.reshape(t, D_MODEL)
    mem2d = mem.reshape(bsz * MEM_LEN, D_MODEL)
    target2d = loss_target.reshape(t, D_MODEL)

    saved = []
    cur, cur_b = x2d, x2d.astype(BF16)
    for l in range(DEPTH):
        wl = layer_w[l]
        wl["conv"] = conv_full[l]
        if "w_in" not in wl:
            wl["w_in"] = jnp.concatenate([wl.pop("w_in_a"), wl.pop("w_in_b")], axis=0)
        now = ["w_up"] if l else ["w_up", "w_mk", "w_mv", "w_br", "w_o"]
        hosts = {"in_proj": (gather_of(now, l), put(now, wl)),
                 "hgrn_fwd": (gather_of(["w_down"], l), put(["w_down"], wl))}
        if l + 1 < DEPTH:
            nxt = layer_w[l + 1]
            hosts["merge_fwd"] = (gather_of(["w_in_a"], l + 1), put(["w_in_a"], nxt))
            hosts["wo_ln"] = (gather_of(["w_mk", "w_mv", "w_o"], l + 1), put(["w_mk", "w_mv", "w_o"], nxt))
            hosts["mlp_up"] = (gather_of(["w_in_b", "w_br"], l + 1), put(["w_in_b", "w_br"], nxt))
        s = _layer_fwd(cur, cur_b, mem2d, wl, bsz=bsz, seq=seq, hosts=hosts)
        saved.append(s)
        cur, cur_b = s["x2"], s["x2_b"]

    dcur, loss_tile = _loss_head(cur, target2d, tm=min(512, seq))
    loss = lax.psum(loss_tile[0, 0], ("x", "y", "c"))

    in_w = IN_COLS // N_DEV

    def in_half(r):
        return lambda ref, j: ref.at[pl.ds(r * half_rows, half_rows), pl.ds(j * in_w, in_w)]

    slicer = dict(w_in_a=in_half(0), w_in_b=in_half(1), w_mk=_rows(D_MODEL // N_DEV), w_mv=_rows(D_MODEL // N_DEV),
                  w_br=_cols(D_MODEL // N_DEV), w_o=_rows(D_MODEL // N_DEV), w_up=_cols(D_FF // N_DEV),
                  w_down=_rows(D_FF // N_DEV))
    source = dict(w_in_a="w_in", w_in_b="w_in")
    recv = [dict() for _ in range(DEPTH)]

    def scatter_of(names, g, into):
        ex = _Exchange([g[source.get(n, n)] for n in names], [slicer[n] for n in names],
                       [sh[n].shape[1:] for n in names])
        return ex, lambda recv_: into.update(zip(names, recv_))

    small_rows = [None] * DEPTH
    prev = None
    rest = ["w_in_b", "w_mk", "w_mv"]
    for l in reversed(range(DEPTH)):
        plan = {"grad_w_up": lambda g, l=l: scatter_of(["w_down"], g, recv[l]),
                "hgrn_bwd": lambda g, l=l: scatter_of(["w_up", "w_o", "w_br"], g, recv[l])}
        if l == 0:
            plan["in_proj_bwd"] = lambda g: scatter_of(["w_in_a"] + rest, g, recv[0])
        else:
            plan["in_proj_bwd"] = lambda g, l=l: scatter_of(["w_in_a"], g, recv[l])
        if prev is not None:
            plan["ln2_bwd_down"] = lambda g, l=l, prev=prev: scatter_of(rest, prev, recv[l + 1])
        dcur, prev, small_rows[l] = _layer_bwd(dcur, mem2d, saved[l], layer_w[l], bsz=bsz, seq=seq, plan=plan)
    for r in recv:
        r["w_in"] = jnp.concatenate([r.pop("w_in_a"), r.pop("w_in_b")], axis=1)

    packed = jnp.concatenate(small_rows + [jnp.zeros((SMALL_ROWS * 128 - DEPTH * SMALL_PER_LAYER,), F32)])
    packed = packed.reshape(SMALL_ROWS, 128)
    all_small = _exchange("gather_small_grads", [packed], [_whole], [packed.shape])[0]
    summed = _sum_devices("sum_small_grads", all_small).reshape(-1)[:DEPTH * SMALL_PER_LAYER]
    summed = summed.reshape(DEPTH, SMALL_PER_LAYER)
    small_grads = {}
    off = 0
    for name, n in SMALL:
        small_grads[name] = summed[:, off:off + n]
        off += n
    small_grads["lower_bounds"] = _lb_bwd(lower_bounds, small_grads["lower_bounds"])
    conv_all = small_grads["conv_w"].reshape(DEPTH, CONV_K, WIDTH)
    small_grads["conv_w"] = lax.dynamic_slice_in_dim(conv_all, me * (WIDTH // N_DEV), WIDTH // N_DEV, axis=2)

    grads, deltas, new_m, new_v = {}, {}, {}, {}
    given = dict(lower_bounds=(lower_bounds, m_lower_bounds, v_lower_bounds), conv_w=(conv_w, m_conv_w, v_conv_w),
                 hg_norm_w=(hg_norm_w, m_hg_norm_w, v_hg_norm_w), b_gate=(b_gate, m_b_gate, v_b_gate),
                 ln1_g=(ln1_g, m_ln1_g, v_ln1_g), ln1_b=(ln1_b, m_ln1_b, v_ln1_b),
                 ln2_g=(ln2_g, m_ln2_g, v_ln2_g), ln2_b=(ln2_b, m_ln2_b, v_ln2_b))
    for name, (w_, m_, v_) in given.items():
        g_ = small_grads[name].reshape(w_.shape)
        grads[name] = g_
        deltas[name], new_m[name], new_v[name] = _adam_small("adam_" + name, g_, w_, m_, v_)

    big = dict(w_in=("w_in", w_in, m_w_in, v_w_in, 128), w_mem_k=("w_mk", w_mem_k, m_w_mem_k, v_w_mem_k, 128),
               w_mem_v=("w_mv", w_mem_v, m_w_mem_v, v_w_mem_v, 128),
               w_branch=("w_br", w_branch, m_w_branch, v_w_branch, 512), w_o=("w_o", w_o, m_w_o, v_w_o, 128),
               w_up=("w_up", w_up, m_w_up, v_w_up, 256), w_down=("w_down", w_down, m_w_down, v_w_down, 128))
    for name, (k, w_, m_, v_, tr) in big.items():
        shape = w_.shape
        flat = (DEPTH, -1, shape[-1])
        rc = [recv[l][k].reshape((N_DEV,) + w_.reshape(flat).shape[1:]) for l in range(DEPTH)]
        outs = _adam_shard("adam_" + name, rc, w_.reshape(flat), m_.reshape(flat), v_.reshape(flat), tr=tr)
        grads[name], deltas[name], new_m[name], new_v[name] = (o.reshape(shape) for o in outs)

    order = ["lower_bounds", "w_in", "conv_w", "hg_norm_w", "w_mem_k", "w_mem_v", "w_branch", "b_gate", "w_o",
             "ln1_g", "ln1_b", "w_up", "w_down", "ln2_g", "ln2_b"]
    return (loss, dcur.reshape(x.shape), *[grads[n] for n in order], *[deltas[n] for n in order],
            *[new_m[n] for n in order], *[new_v[n] for n in order])
```

```python
import functools

import jax
import jax.numpy as jnp
from jax import lax
from jax.experimental import pallas as pl
from jax.experimental.pallas import tpu as pltpu

F32 = jnp.float32
BF16 = jnp.bfloat16

N_DEV = 8
D_MODEL = 1024
DEPTH = 4
MEM_LEN = 256
CONV_K = 3
WIDTH = 512
HEADS = 4
HEAD_DIM = 128
CHUNK = 32
D_FF = 4 * D_MODEL
IN_COLS = 7168
ALPHA = (2.0 * DEPTH) ** 0.25
LN_EPS = 1e-5
RMS_EPS = 1e-6
ADAM_LR = 0.001
ADAM_B1 = 0.9
ADAM_B2 = 0.999
ADAM_EPS = 1e-08
ADAM_WD = 0.01
ADAM_STEP = 10

C_CB, C_CC, C_CH, C_HQ, C_HF, C_HI, C_HG, C_MQ, C_GA = 0, 512, 1024, 1536, 2048, 2560, 3072, 3584, 4096

ROWS_HG = 256
NT_DIMS = (((1,), (1,)), ((), ()))
TN_DIMS = (((0,), (0,)), ((), ()))
MESH = pl.DeviceIdType.MESH


def _dot(a, b):
    return jnp.dot(a, b, preferred_element_type=F32)


def _dot_nt(a, b):
    return lax.dot_general(a, b, NT_DIMS, preferred_element_type=F32)


def _dot_tn(a, b):
    return lax.dot_general(a, b, TN_DIMS, preferred_element_type=F32)


def _sigmoid(x):
    return 1.0 / (1.0 + jnp.exp(-x))


def _params(*sem):
    return pltpu.CompilerParams(dimension_semantics=sem)


def _resident(shape):
    nd = len(shape)
    return pl.BlockSpec(shape, lambda *_: (0,) * nd)


def _my_id():
    return 4 * lax.axis_index("x") + 2 * lax.axis_index("y") + lax.axis_index("c")


class _Exchange:
    def __init__(self, srcs, slicers, piece_shapes):
        self.srcs, self.slicers, self.n = list(srcs), list(slicers), len(srcs)
        any_spec = pl.BlockSpec(memory_space=pl.ANY)
        self.in_specs = [any_spec] * self.n
        self.out_specs = [any_spec] * self.n
        self.out_shape = [jax.ShapeDtypeStruct((N_DEV,) + tuple(s), a.dtype) for s, a in zip(piece_shapes, srcs)]
        self.scratch = [pltpu.SemaphoreType.DMA((self.n * N_DEV,)), pltpu.SemaphoreType.DMA((self.n * N_DEV,)),
                        pltpu.SemaphoreType.DMA((self.n,))]

    def _remote(self, ins, outs, sems, k, j, me):
        return pltpu.make_async_remote_copy(
            src_ref=self.slicers[k](ins[k], j), dst_ref=outs[k].at[me],
            send_sem=sems[0].at[k * N_DEV + j], recv_sem=sems[1].at[k * N_DEV + me],
            device_id=(j // 4, (j // 2) % 2, j % 2), device_id_type=MESH)

    def _local(self, ins, outs, sems, k, j, me):
        return pltpu.make_async_copy(self.slicers[k](ins[k], j), outs[k].at[me], sems[2].at[k])

    def start(self, ins, outs, sems):
        me = _my_id()
        for k in range(self.n):
            for j in range(N_DEV):
                @pl.when(j != me)
                def _():
                    self._remote(ins, outs, sems, k, j, me).start()

                @pl.when(j == me)
                def _():
                    self._local(ins, outs, sems, k, j, me).start()

    def wait(self, ins, outs, sems):
        me = _my_id()
        for k in range(self.n):
            for j in range(N_DEV):
                @pl.when(j != me)
                def _():
                    pltpu.make_async_remote_copy(
                        src_ref=self.slicers[k](ins[k], j), dst_ref=outs[k].at[j],
                        send_sem=sems[0].at[k * N_DEV + j], recv_sem=sems[1].at[k * N_DEV + j],
                        device_id=(j // 4, (j // 2) % 2, j % 2), device_id_type=MESH).wait_recv()
                    self._remote(ins, outs, sems, k, j, me).wait_send()

                @pl.when(j == me)
                def _():
                    self._local(ins, outs, sems, k, j, me).wait()


def _exchange(name, srcs, slicers, piece_shapes):
    ex = _Exchange(srcs, slicers, piece_shapes)

    def body(*refs):
        ins, outs, sems = refs[:ex.n], refs[ex.n:2 * ex.n], refs[2 * ex.n:]
        ex.start(ins, outs, sems)
        ex.wait(ins, outs, sems)

    return pl.pallas_call(
        body, name=name, in_specs=ex.in_specs, out_specs=ex.out_specs, out_shape=ex.out_shape,
        scratch_shapes=ex.scratch, compiler_params=pltpu.CompilerParams(has_side_effects=True),
    )(*ex.srcs)


def _call(body, name, grid, in_specs, out_specs, out_shape, args, scratch=(), sem=None, carry=None):
    n_in, n_out, n_scr = len(in_specs), len(out_specs), len(scratch)
    if carry is None:
        outs = pl.pallas_call(body, name=name, grid=grid, in_specs=in_specs, out_specs=out_specs,
                              out_shape=out_shape, scratch_shapes=list(scratch),
                              compiler_params=_params(*sem))(*args)
        return outs, None
    nc = carry.n

    def hosted(*refs):
        ins, c_in = refs[:n_in], refs[n_in:n_in + nc]
        outs = refs[n_in + nc:n_in + nc + n_out]
        c_out = refs[n_in + nc + n_out:n_in + 2 * nc + n_out]
        rest = refs[n_in + 2 * nc + n_out:]
        scr, sems = rest[:n_scr], rest[n_scr:]
        first, last = True, True
        for d, size in enumerate(grid):
            first = first & (pl.program_id(d) == 0)
            last = last & (pl.program_id(d) == size - 1)

        @pl.when(first)
        def _():
            carry.start(c_in, c_out, sems)

        body(*ins, *outs, *scr)

        @pl.when(last)
        def _():
            carry.wait(c_in, c_out, sems)

    outs = pl.pallas_call(
        hosted, name=name + "_x", grid=grid, in_specs=list(in_specs) + carry.in_specs,
        out_specs=list(out_specs) + carry.out_specs, out_shape=list(out_shape) + carry.out_shape,
        scratch_shapes=list(scratch) + carry.scratch,
        compiler_params=_params(*(["arbitrary"] * len(grid))))(*args, *carry.srcs)
    return outs[:n_out], outs[n_out:]


def _whole(ref, j):
    return ref


def _cols(width):
    return lambda ref, j: ref.at[(slice(None),) * (len(ref.shape) - 1) + (pl.ds(j * width, width),)]


def _rows(height):
    return lambda ref, j: ref.at[pl.ds(j * height, height)]


def _mm_nn(name, a, w, *, tm, tn, out_dtype, relu2=False, carry=None):
    t, k = a.shape
    n = w.shape[1]

    def body(a_ref, w_ref, o_ref):
        acc = _dot(a_ref[...].astype(BF16), w_ref[...])
        if relu2:
            r = jnp.maximum(acc, 0.0)
            acc = r * r
        o_ref[...] = acc.astype(out_dtype)

    outs, recv = _call(
        body, name, (t // tm, n // tn),
        [pl.BlockSpec((tm, k), lambda i, j: (i, 0)), pl.BlockSpec((k, tn), lambda i, j: (0, j))],
        [pl.BlockSpec((tm, tn), lambda i, j: (i, j))], [jax.ShapeDtypeStruct((t, n), out_dtype)], (a, w),
        sem=("parallel", "parallel"), carry=carry)
    return outs[0] if carry is None else (outs[0], recv)


def _layer_norm(z, g, b):
    mu = jnp.mean(z, axis=-1, keepdims=True)
    zc = z - mu
    var = jnp.mean(zc * zc, axis=-1, keepdims=True)
    return zc * lax.rsqrt(var + LN_EPS) * g + b


def _linear_ln(name, a, w, resid, g, b, *, tm, carry=None):
    t, k = a.shape

    def body(a_ref, w_ref, r_ref, g_ref, b_ref, z_ref, x_ref, xb_ref):
        z = ALPHA * r_ref[...] + _dot(a_ref[...], w_ref[...])
        z_ref[...] = z
        y = _layer_norm(z, g_ref[...], b_ref[...])
        x_ref[...] = y
        xb_ref[...] = y.astype(BF16)

    row = pl.BlockSpec((tm, D_MODEL), lambda i: (i, 0))
    outs, recv = _call(
        body, name, (t // tm,),
        [pl.BlockSpec((tm, k), lambda i: (i, 0)), _resident((k, D_MODEL)), row,
         _resident((1, D_MODEL)), _resident((1, D_MODEL))],
        [row, row, row],
        [jax.ShapeDtypeStruct((t, D_MODEL), F32)] * 2 + [jax.ShapeDtypeStruct((t, D_MODEL), BF16)],
        (a, w, resid, g, b), sem=("parallel",), carry=carry)
    return outs if carry is None else (outs, recv)


def _ln_bwd_mm_nt(name, dy, z, g, w, h=None, *, tm, tn, carry=None):
    t = dy.shape[0]
    n = w.shape[0]
    halves = [slice(0, tm // 2), slice(tm // 2, tm)]

    def body(*refs):
        if h is None:
            dy_ref, z_ref, g_ref, w_ref, dz_ref, dzb_ref, o_ref, dg_ref, db_ref = refs
        else:
            dy_ref, z_ref, g_ref, w_ref, h_ref, dz_ref, dzb_ref, o_ref, dg_ref, db_ref = refs

        @pl.when(pl.program_id(0) == 0)
        def _():
            dg_ref[...] = jnp.zeros_like(dg_ref)
            db_ref[...] = jnp.zeros_like(db_ref)

        zv = _Lanes(z_ref[s, :] for s in halves)
        dyv = _Lanes(dy_ref[s, :] for s in halves)
        mu = _mean(zv, axis=-1, keepdims=True)
        zc = zv - mu
        rstd = _rsqrt(_mean(zc * zc, axis=-1, keepdims=True) + LN_EPS)
        xh = zc * rstd
        gdy = dyv * g_ref[...]
        m1 = _mean(gdy, axis=-1, keepdims=True)
        m2 = _mean(gdy * xh, axis=-1, keepdims=True)
        dz = rstd * (gdy - m1 - xh * m2)
        dz_b = dz.astype(BF16)
        for s, a, a_b in zip(halves, dz.xs, dz_b.xs):
            dz_ref[s, :] = a
            dzb_ref[s, :] = a_b
        dg_ref[...] += _sum(dyv * xh, axis=0, keepdims=True).total()
        db_ref[...] += _sum(dyv, axis=0, keepdims=True).total()
        for c in range(n // tn):
            cols = slice(c * tn, (c + 1) * tn)
            acc = _ldot_nt(dz_b, w_ref[cols, :])
            if h is not None:
                acc = acc * (2.0 * _sqrt(_Lanes(h_ref[s, cols] for s in halves).astype(F32)))
            for s, a in zip(halves, acc.xs):
                o_ref[s, cols] = a.astype(BF16)

    row = pl.BlockSpec((tm, D_MODEL), lambda i: (i, 0))
    vec = _resident((1, D_MODEL))
    tile = pl.BlockSpec((tm, n), lambda i: (i, 0))
    in_specs = [row, row, vec, _resident((n, D_MODEL))]
    args = [dy, z, g, w]
    if h is not None:
        in_specs.append(tile)
        args.append(h)
    outs, recv = _call(
        body, name, (t // tm,), in_specs, [row, row, tile, vec, vec],
        [jax.ShapeDtypeStruct((t, D_MODEL), F32), jax.ShapeDtypeStruct((t, D_MODEL), BF16),
         jax.ShapeDtypeStruct((t, n), BF16), jax.ShapeDtypeStruct((1, D_MODEL), F32),
         jax.ShapeDtypeStruct((1, D_MODEL), F32)], args, sem=("arbitrary",), carry=carry)
    return outs if carry is None else (outs, recv)


def _mm_tn(name, a, b, *, tk, tmo, tno, carry=None):
    t, m = a.shape
    n = b.shape[1]
    nk = t // tk

    def body(a_ref, b_ref, o_ref, acc_ref):
        k = pl.program_id(2)
        p = _dot_tn(a_ref[...].astype(BF16), b_ref[...].astype(BF16))

        @pl.when(k == 0)
        def _():
            acc_ref[...] = p

        @pl.when(k > 0)
        def _():
            acc_ref[...] += p

        @pl.when(k == nk - 1)
        def _():
            o_ref[...] = acc_ref[...].astype(BF16)

    outs, recv = _call(
        body, name, (m // tmo, n // tno, nk),
        [pl.BlockSpec((tk, tmo), lambda i, j, k: (k, i)), pl.BlockSpec((tk, tno), lambda i, j, k: (k, j))],
        [pl.BlockSpec((tmo, tno), lambda i, j, k: (i, j))], [jax.ShapeDtypeStruct((m, n), BF16)], (a, b),
        scratch=[pltpu.VMEM((tmo, tno), F32)], sem=("parallel", "parallel", "arbitrary"), carry=carry)
    return outs[0] if carry is None else (outs[0], recv)


def _mm_nt_sum(name, pieces, offsets, w, resid, *, tm, carry=None):
    t = resid.shape[0]
    widths = [p.shape[1] for p in pieces]
    n_p = len(pieces)

    def body(*refs):
        p_refs, w_ref, r_ref, o_ref = refs[:n_p], refs[n_p], refs[n_p + 1], refs[n_p + 2]
        acc = ALPHA * r_ref[...]
        for p_ref, off, wd in zip(p_refs, offsets, widths):
            acc = acc + _dot_nt(p_ref[...], w_ref[:, off:off + wd])
        o_ref[...] = acc

    row = pl.BlockSpec((tm, D_MODEL), lambda i: (i, 0))
    outs, recv = _call(
        body, name, (t // tm,),
        [pl.BlockSpec((tm, wd), lambda i: (i, 0)) for wd in widths] + [_resident(w.shape), row],
        [row], [jax.ShapeDtypeStruct((t, D_MODEL), F32)], (*pieces, w, resid), sem=("parallel",), carry=carry)
    return outs[0] if carry is None else (outs[0], recv)


def _chunk_mask(rows):
    r = lax.broadcasted_iota(jnp.int32, (rows, rows), 0)
    c = lax.broadcasted_iota(jnp.int32, (rows, rows), 1)
    return ((r // CHUNK) == (c // CHUNK)) & (c <= r)


class _Lanes:
    def __init__(self, xs):
        self.xs = list(xs)

    def _with(self, other, f):
        if isinstance(other, _Lanes):
            return _Lanes([f(a, b) for a, b in zip(self.xs, other.xs)])
        return _Lanes([f(a, other) for a in self.xs])

    def __add__(self, o):
        return self._with(o, lambda a, b: a + b)

    def __radd__(self, o):
        return self._with(o, lambda a, b: b + a)

    def __sub__(self, o):
        return self._with(o, lambda a, b: a - b)

    def __rsub__(self, o):
        return self._with(o, lambda a, b: b - a)

    def __mul__(self, o):
        return self._with(o, lambda a, b: a * b)

    def __rmul__(self, o):
        return self._with(o, lambda a, b: b * a)

    def __truediv__(self, o):
        return self._with(o, lambda a, b: a / b)

    def __rtruediv__(self, o):
        return self._with(o, lambda a, b: b / a)

    def __neg__(self):
        return _Lanes([-a for a in self.xs])

    def __ge__(self, o):
        return self._with(o, lambda a, b: a >= b)

    def __getitem__(self, idx):
        return _Lanes([a[idx] for a in self.xs])

    def astype(self, dtype):
        return _Lanes([a.astype(dtype) for a in self.xs])

    def total(self):
        return functools.reduce(lambda a, b: a + b, self.xs)


def _lift(f):
    def g(*args, **kw):
        lanes = [a for a in args if isinstance(a, _Lanes)]
        if not lanes:
            return f(*args, **kw)
        return _Lanes([f(*[a.xs[i] if isinstance(a, _Lanes) else a for a in args], **kw)
                       for i in range(len(lanes[0].xs))])
    return g


def _concat(parts, axis):
    if isinstance(parts[0], _Lanes):
        return _Lanes([jnp.concatenate([p.xs[i] for p in parts], axis=axis) for i in range(len(parts[0].xs))])
    return jnp.concatenate(parts, axis=axis)


_exp, _log, _abs, _sqrt, _where = _lift(jnp.exp), _lift(jnp.log), _lift(jnp.abs), _lift(jnp.sqrt), _lift(jnp.where)
_sum, _mean, _rsqrt, _bcast = _lift(jnp.sum), _lift(jnp.mean), _lift(lax.rsqrt), _lift(jnp.broadcast_to)
_ldot, _ldot_nt, _ldot_tn = _lift(_dot), _lift(_dot_nt), _lift(_dot_tn)
_lsigmoid = _lift(_sigmoid)


def _mask_sum(mask_b, x, transpose=False):
    f = _ldot_tn if transpose else _ldot
    hi = x.astype(BF16)
    lo = (x - hi.astype(F32)).astype(BF16)
    return f(mask_b, hi) + f(mask_b, lo)


def _chunk_row(x, pos, rows):
    nc = rows // CHUNK

    def one(a):
        a3 = a.reshape(nc, CHUNK, HEAD_DIM)
        return jnp.broadcast_to(a3[:, pos:pos + 1, :], (nc, CHUNK, HEAD_DIM)).reshape(rows, HEAD_DIM)

    return _lift(one)(x)


def _chunk_total(x, rows):
    nc = rows // CHUNK

    def one(a):
        tot = jnp.sum(a.reshape(nc, CHUNK, HEAD_DIM), axis=1, keepdims=True)
        return jnp.broadcast_to(tot, (nc, CHUNK, HEAD_DIM)).reshape(rows, HEAD_DIM)

    return _lift(one)(x)


def _sigmoid_pair(x):
    e = _exp(-_abs(x))
    big = 1.0 / (1.0 + e)
    small = e * big
    pos = x >= 0.0
    return _where(pos, big, small), _where(pos, small, big)


def _hg_gates(q_raw, fl, lb, rows, mask):
    tri = mask.astype(BF16)
    sg, sg_neg = _sigmoid_pair(fl)
    forget = lb + (1.0 - lb) * sg
    k = (1.0 - lb) * sg_neg
    sq = _lsigmoid(q_raw)
    qs = q_raw * sq
    bc = _mask_sum(tri, _log(forget))
    bref = _chunk_row(bc, CHUNK // 2 - 1, rows)
    blast = _chunk_row(bc, CHUNK - 1, rows)
    return dict(tri=tri, sg=sg, sg_neg=sg_neg, forget=forget, k=k, sq=sq, qs=qs,
                e_a=_exp(bc - bref), e_b=_exp(bref - bc), e_q=_exp(bc), e_k=_exp(blast - bc),
                dec=_exp(blast))


HG_GROUP = 4


def _hg_lanes(bsz):
    return [(hh, slice(hh * HEAD_DIM, (hh + 1) * HEAD_DIM), b) for hh in range(HG_GROUP) for b in range(bsz)]


def _hg_read(ref, lanes):
    return _Lanes(ref[b, :, cs] for _, cs, b in lanes)


def _hg_write(ref, lanes, val, offset=0):
    for (_, cs, b), a in zip(lanes, val.xs):
        ref[b, :, offset + cs.start:offset + cs.stop] = a


def _hgrn_fwd(proj, lb, nw, *, bsz, seq, carry=None):
    rows = min(ROWS_HG, seq)
    nt = seq // rows
    nc = rows // CHUNK
    t = bsz * seq

    lanes = _hg_lanes(bsz)

    def body(q_ref, f_ref, v_ref, g_ref, lb_ref, nw_ref, y_ref, o_ref, st_ref, s_scr):
        @pl.when(pl.program_id(1) == 0)
        def _():
            s_scr[...] = jnp.zeros_like(s_scr)

        mask = _chunk_mask(rows)
        lb_v = _Lanes(lb_ref[:, cs] for _, cs, _ in lanes)
        gt = _hg_gates(_hg_read(q_ref, lanes), _hg_read(f_ref, lanes), lb_v, rows, mask)
        v_b = _hg_read(v_ref, lanes).astype(BF16)
        a_b = (gt["qs"] * gt["e_a"]).astype(BF16)
        b_b = (gt["k"] * gt["e_b"]).astype(BF16)
        qi_b = (gt["qs"] * gt["e_q"]).astype(BF16)
        ko_b = (gt["k"] * gt["e_k"]).astype(BF16)
        scores = _where(mask, _ldot_nt(a_b, b_b), 0.0)
        o_intra = _ldot(scores.astype(BF16), v_b)

        s = _Lanes(s_scr[i] for i in range(len(lanes)))
        parts = []
        for n in range(nc):
            sl = slice(n * CHUNK, (n + 1) * CHUNK)
            s_b = s.astype(BF16)
            for (hh, _, b), a in zip(lanes, s_b.xs):
                st_ref[hh, b, n] = a
            parts.append(_ldot_nt(qi_b[sl], s_b))
            s = s * gt["dec"][n * CHUNK:n * CHUNK + 1] + _ldot_tn(v_b[sl], ko_b[sl])
        for i, a in enumerate(s.xs):
            s_scr[i] = a
        o = o_intra + _concat(parts, 0)
        _hg_write(o_ref, lanes, o)
        r = _rsqrt(_mean(o * o, axis=-1, keepdims=True) + RMS_EPS)
        g = _hg_read(g_ref, lanes)
        _hg_write(y_ref, lanes, (o * r * nw_ref[...] * (g * _lsigmoid(g))).astype(BF16))

    wide = HG_GROUP * HEAD_DIM

    def col(base):
        return pl.BlockSpec((bsz, rows, wide), lambda h, j: (0, j, base // wide + h))

    out_tile = pl.BlockSpec((bsz, rows, wide), lambda h, j: (0, j, h))
    p3 = proj.reshape(bsz, seq, IN_COLS)
    outs, recv = _call(
        body, "hgrn_fwd", (HEADS // HG_GROUP, nt),
        [col(C_HQ), col(C_HF), col(C_HI), col(C_HG),
         pl.BlockSpec((1, wide), lambda h, j: (0, h)), _resident((1, HEAD_DIM))],
        [out_tile, out_tile,
         pl.BlockSpec((HG_GROUP, bsz, nc, HEAD_DIM, HEAD_DIM), lambda h, j: (h, 0, j, 0, 0))],
        [jax.ShapeDtypeStruct((bsz, seq, WIDTH), BF16), jax.ShapeDtypeStruct((bsz, seq, WIDTH), F32),
         jax.ShapeDtypeStruct((HEADS, bsz, seq // CHUNK, HEAD_DIM, HEAD_DIM), BF16)],
        (p3, p3, p3, p3, lb, nw), scratch=[pltpu.VMEM((len(lanes), HEAD_DIM, HEAD_DIM), F32)],
        sem=("parallel", "arbitrary"), carry=carry)
    outs = [outs[0].reshape(t, WIDTH), outs[1].reshape(t, WIDTH), outs[2]]
    return outs if carry is None else (outs, recv)


def _hgrn_bwd(proj, lb, nw, o_pre, states, dy, *, bsz, seq, carry=None):
    rows = min(ROWS_HG, seq)
    nt = seq // rows
    nc = rows // CHUNK
    t = bsz * seq
    lanes = _hg_lanes(bsz)

    def body(q_ref, f_ref, v_ref, g_ref, lb_ref, nw_ref, o_ref, st_ref, dy_ref, dh_ref, dlb_ref, dnw_ref, ds_scr):
        h, j = pl.program_id(0), pl.program_id(1)

        @pl.when(j == 0)
        def _():
            ds_scr[...] = jnp.zeros_like(ds_scr)
            dlb_ref[...] = jnp.zeros_like(dlb_ref)

        @pl.when((h == 0) & (j == 0))
        def _():
            dnw_ref[...] = jnp.zeros_like(dnw_ref)

        mask = _chunk_mask(rows)
        q_raw = _hg_read(q_ref, lanes)
        lb_v = _Lanes(lb_ref[:, cs] for _, cs, _ in lanes)
        gt = _hg_gates(q_raw, _hg_read(f_ref, lanes), lb_v, rows, mask)
        v_b = _hg_read(v_ref, lanes).astype(BF16)
        a_f = gt["qs"] * gt["e_a"]
        b_f = gt["k"] * gt["e_b"]
        qi_f = gt["qs"] * gt["e_q"]
        ko_f = gt["k"] * gt["e_k"]
        a_b, b_b, qi_b, ko_b = a_f.astype(BF16), b_f.astype(BF16), qi_f.astype(BF16), ko_f.astype(BF16)

        o = _hg_read(o_ref, lanes)
        nw_v = nw_ref[...]
        g = _hg_read(g_ref, lanes)
        dyv = _hg_read(dy_ref, lanes)
        r = _rsqrt(_mean(o * o, axis=-1, keepdims=True) + RMS_EPS)
        sgg = _lsigmoid(g)
        d_g = dyv * (o * r * nw_v) * (sgg * (1.0 + g * (1.0 - sgg)))
        d_on = dyv * (g * sgg)
        dnw_ref[...] += _sum(d_on * o * r, axis=0, keepdims=True).total()
        tt = d_on * nw_v
        d_o = r * tt - o * (r * r * r) * _mean(tt * o, axis=-1, keepdims=True)
        do_b = d_o.astype(BF16)

        sc_b = _where(mask, _ldot_nt(a_b, b_b), 0.0).astype(BF16)
        dsc_b = _where(mask, _ldot_nt(do_b, v_b), 0.0).astype(BF16)
        d_v = _ldot_tn(sc_b, do_b)
        d_a = _ldot(dsc_b, b_b)
        d_bm = _ldot_tn(dsc_b, a_b)

        ds = _Lanes(ds_scr[i] for i in range(len(lanes)))
        dqi_parts, dko_parts, dvi_parts, ddec_parts = [None] * nc, [None] * nc, [None] * nc, [None] * nc
        for n in reversed(range(nc)):
            sl = slice(n * CHUNK, (n + 1) * CHUNK)
            dec_n = gt["dec"][n * CHUNK:n * CHUNK + 1]
            ds_b = ds.astype(BF16)
            s_n = _Lanes(st_ref[hh, b, n] for hh, _, b in lanes)
            dqi_parts[n] = _ldot(do_b[sl], s_n)
            dko_parts[n] = _ldot(v_b[sl], ds_b)
            dvi_parts[n] = _ldot_nt(ko_b[sl], ds_b)
            d_dec = _sum(ds * s_n.astype(F32), axis=0, keepdims=True)
            ddec_parts[n] = _bcast(d_dec * dec_n, (CHUNK, HEAD_DIM))
            ds = ds * dec_n + _ldot_tn(do_b[sl], qi_b[sl])
        for i, a in enumerate(ds.xs):
            ds_scr[i] = a
        d_qi = _concat(dqi_parts, 0)
        d_ko = _concat(dko_parts, 0)
        d_v = d_v + _concat(dvi_parts, 0)

        d_qs = d_a * gt["e_a"] + d_qi * gt["e_q"]
        d_k = d_bm * gt["e_b"] + d_ko * gt["e_k"]
        t_a, t_b, t_q, t_k = d_a * a_f, d_bm * b_f, d_qi * qi_f, d_ko * ko_f
        d_bref = _chunk_total(t_b - t_a, rows)
        d_blast = _chunk_total(t_k, rows) + _concat(ddec_parts, 0)
        pos = lax.broadcasted_iota(jnp.int32, (rows, HEAD_DIM), 0) % CHUNK
        d_bc = (t_a - t_b + t_q - t_k + _where(pos == CHUNK // 2 - 1, d_bref, 0.0)
                + _where(pos == CHUNK - 1, d_blast, 0.0))
        d_logf = _mask_sum(gt["tri"], d_bc, transpose=True)

        sg, sg_neg = gt["sg"], gt["sg_neg"]
        inv_f = 1.0 / gt["forget"]
        common = (1.0 - lb_v) * sg * sg_neg
        d_fl = common * (d_logf * inv_f - d_k)
        d_lb = _sum(sg_neg * (d_logf * inv_f - d_k), axis=0, keepdims=True)
        for (_, cs, _), a in zip(lanes, d_lb.xs):
            dlb_ref[:, cs] += a
        sq = gt["sq"]
        _hg_write(dh_ref, lanes, (d_qs * (sq * (1.0 + q_raw * (1.0 - sq)))).astype(BF16), 0)
        _hg_write(dh_ref, lanes, d_fl.astype(BF16), WIDTH)
        _hg_write(dh_ref, lanes, d_v.astype(BF16), 2 * WIDTH)
        _hg_write(dh_ref, lanes, d_g.astype(BF16), 3 * WIDTH)

    assert HG_GROUP == HEADS, "the combined gradient block needs all heads in one grid step"
    wide = HG_GROUP * HEAD_DIM

    def col(base):
        return pl.BlockSpec((bsz, rows, wide), lambda h, j: (0, nt - 1 - j, base // wide + h))

    tile = pl.BlockSpec((bsz, rows, wide), lambda h, j: (0, nt - 1 - j, h))
    head_vec = pl.BlockSpec((1, wide), lambda h, j: (0, h))
    p3 = proj.reshape(bsz, seq, IN_COLS)
    outs, recv = _call(
        body, "hgrn_bwd", (HEADS // HG_GROUP, nt),
        [col(C_HQ), col(C_HF), col(C_HI), col(C_HG), head_vec, _resident((1, HEAD_DIM)), tile,
         pl.BlockSpec((HG_GROUP, bsz, nc, HEAD_DIM, HEAD_DIM), lambda h, j: (h, 0, nt - 1 - j, 0, 0)), tile],
        [pl.BlockSpec((bsz, rows, 4 * WIDTH), lambda h, j: (0, nt - 1 - j, 0)), head_vec, _resident((1, HEAD_DIM))],
        [jax.ShapeDtypeStruct((bsz, seq, 4 * WIDTH), BF16), jax.ShapeDtypeStruct((1, WIDTH), F32),
         jax.ShapeDtypeStruct((1, HEAD_DIM), F32)],
        (p3, p3, p3, p3, lb, nw, o_pre.reshape(bsz, seq, WIDTH), states, dy.reshape(bsz, seq, WIDTH)),
        scratch=[pltpu.VMEM((len(lanes), HEAD_DIM, HEAD_DIM), F32)],
        sem=("arbitrary", "arbitrary"), carry=carry)
    outs = [outs[0].reshape(t, 4 * WIDTH), outs[1], outs[2]]
    return outs if carry is None else (outs, recv)


def _mem_kv(mem2d, w_k, w_v):
    rows = mem2d.shape[0]

    def body(m_ref, wk_ref, wv_ref, k_ref, v_ref):
        m_b = m_ref[...].astype(BF16)
        k_ref[...] = _dot(m_b, wk_ref[...]).astype(BF16)
        v_ref[...] = _dot(m_b, wv_ref[...]).astype(BF16)

    return pl.pallas_call(
        body, name="mem_kv", grid=(rows // MEM_LEN,),
        in_specs=[pl.BlockSpec((MEM_LEN, D_MODEL), lambda i: (i, 0)), _resident((D_MODEL, WIDTH)),
                  _resident((D_MODEL, WIDTH))],
        out_specs=[pl.BlockSpec((MEM_LEN, WIDTH), lambda i: (i, 0))] * 2,
        out_shape=[jax.ShapeDtypeStruct((rows, WIDTH), BF16)] * 2,
        compiler_params=_params("parallel"),
    )(mem2d, w_k, w_v)


def _softmax_rows(s):
    m = _lift(jnp.max)(s, axis=-1, keepdims=True)
    e = _exp(s - m)
    return e / _sum(e, axis=-1, keepdims=True)


def _attn_fwd(proj, mk, mv, *, tm, seq):
    t = proj.shape[0]
    per_b = seq // tm
    scale = HEAD_DIM ** -0.5

    def body(q_ref, k_ref, v_ref, y_ref):
        heads = [slice(h * HEAD_DIM, (h + 1) * HEAD_DIM) for h in range(HEADS)]
        q_b = _Lanes(q_ref[:, sl] for sl in heads).astype(BF16)
        p = _softmax_rows(_ldot_nt(q_b, _Lanes(k_ref[:, sl] for sl in heads)) * scale)
        out = _ldot(p.astype(BF16), _Lanes(v_ref[:, sl] for sl in heads))
        y_ref[...] = jnp.concatenate(out.xs, axis=-1).astype(BF16)

    kv = pl.BlockSpec((MEM_LEN, WIDTH), lambda i: (i // per_b, 0))
    return pl.pallas_call(
        body, name="attn_fwd", grid=(t // tm,),
        in_specs=[pl.BlockSpec((tm, WIDTH), lambda i: (i, C_MQ // WIDTH)), kv, kv],
        out_specs=pl.BlockSpec((tm, WIDTH), lambda i: (i, 0)),
        out_shape=jax.ShapeDtypeStruct((t, WIDTH), BF16),
        compiler_params=_params("parallel"),
    )(proj, mk, mv)


def _attn_bwd(proj, mk, mv, dy, *, tm, seq):
    t = proj.shape[0]
    per_b = seq // tm
    scale = HEAD_DIM ** -0.5

    def body(q_ref, k_ref, v_ref, dy_ref, dq_ref, dk_ref, dv_ref):
        i = pl.program_id(0)

        @pl.when(i % per_b == 0)
        def _():
            dk_ref[...] = jnp.zeros_like(dk_ref)
            dv_ref[...] = jnp.zeros_like(dv_ref)

        heads = [slice(h * HEAD_DIM, (h + 1) * HEAD_DIM) for h in range(HEADS)]
        q_b = _Lanes(q_ref[:, sl] for sl in heads).astype(BF16)
        k_b, v_b = _Lanes(k_ref[:, sl] for sl in heads), _Lanes(v_ref[:, sl] for sl in heads)
        p = _softmax_rows(_ldot_nt(q_b, k_b) * scale)
        dy_b = _Lanes(dy_ref[:, sl] for sl in heads).astype(BF16)
        dp = _ldot_nt(dy_b, v_b)
        d_v = _ldot_tn(p.astype(BF16), dy_b)
        ds_b = (p * (dp - _sum(dp * p, axis=-1, keepdims=True)) * scale).astype(BF16)
        dq_ref[...] = jnp.concatenate(_ldot(ds_b, k_b).xs, axis=-1).astype(BF16)
        dk_ref[...] += jnp.concatenate(_ldot_tn(ds_b, q_b).xs, axis=-1)
        dv_ref[...] += jnp.concatenate(d_v.xs, axis=-1)

    kv = pl.BlockSpec((MEM_LEN, WIDTH), lambda i: (i // per_b, 0))
    tile = pl.BlockSpec((tm, WIDTH), lambda i: (i, 0))
    n_mem = mk.shape[0]
    return pl.pallas_call(
        body, name="attn_bwd", grid=(t // tm,),
        in_specs=[pl.BlockSpec((tm, WIDTH), lambda i: (i, C_MQ // WIDTH)), kv, kv, tile],
        out_specs=[tile, kv, kv],
        out_shape=[jax.ShapeDtypeStruct((t, WIDTH), BF16), jax.ShapeDtypeStruct((n_mem, WIDTH), F32),
                   jax.ShapeDtypeStruct((n_mem, WIDTH), F32)],
        compiler_params=_params("arbitrary"),
    )(proj, mk, mv, dy)


HALO = 8


def _shift_down(u, halo, k, row):
    out = pltpu.roll(u, k, 0)
    for m in range(k):
        out = jnp.where(row == m, halo[HALO - k + m:HALO - k + m + 1, :], out)
    return out


def _shift_up(u, halo, k, row, tm):
    out = pltpu.roll(u, tm - k, 0)
    for m in range(k):
        out = jnp.where(row == tm - k + m, halo[m:m + 1, :], out)
    return out


def _merge_fwd(proj, y_b, y_c, conv_w, w_branch, b_gate, *, tm, seq, carry=None):
    t = proj.shape[0]
    per_b = seq // tm
    hb = tm // HALO

    def body(cb_ref, cc_ref, ch_ref, cch_ref, chh_ref, ga_ref, gb_ref, gc_ref, yb_ref, yc_ref, cw_ref, wb_ref,
             bg_ref, ya_ref, pa_ref, pb_ref, pc_ref, mg_ref, sa_ref, sb_ref, sc_ref):
        i = pl.program_id(0)
        row = lax.broadcasted_iota(jnp.int32, (tm, WIDTH), 0)
        u = cc_ref[...] * ch_ref[...]
        halo = jnp.where(i % per_b == 0, 0.0, cch_ref[...] * chh_ref[...])
        cw = cw_ref[...]
        y = cw[0:1] * _shift_down(u, halo, 2, row) + cw[1:2] * _shift_down(u, halo, 1, row) + cw[2:3] * u
        ya_b = (cb_ref[...] * y).astype(BF16)
        ya_ref[...] = ya_b
        merged = None
        for idx, (y_in, g_ref, p_ref, s_ref) in enumerate(((ya_b, ga_ref, pa_ref, sa_ref),
                                                            (yb_ref[...], gb_ref, pb_ref, sb_ref),
                                                            (yc_ref[...], gc_ref, pc_ref, sc_ref))):
            p = _dot(y_in, wb_ref[idx])
            p_ref[...] = p.astype(BF16)
            sg = _sigmoid(g_ref[...] + bg_ref[:, idx * D_MODEL:(idx + 1) * D_MODEL])
            s_ref[...] = sg.astype(BF16)
            term = sg * p
            merged = term if merged is None else merged + term
        mg_ref[...] = merged.astype(BF16)

    def half(c):
        return pl.BlockSpec((tm, WIDTH), lambda i: (i, c // WIDTH))

    def prev(c):
        return pl.BlockSpec((HALO, WIDTH), lambda i: (jnp.maximum(i * hb - 1, 0), c // WIDTH))

    def gate(k):
        return pl.BlockSpec((tm, D_MODEL), lambda i: (i, C_GA // D_MODEL + k))

    tile512 = pl.BlockSpec((tm, WIDTH), lambda i: (i, 0))
    tile1k = pl.BlockSpec((tm, D_MODEL), lambda i: (i, 0))
    outs, recv = _call(
        body, "merge_fwd", (t // tm,),
        [half(C_CB), half(C_CC), half(C_CH), prev(C_CC), prev(C_CH), gate(0), gate(1), gate(2),
         tile512, tile512, _resident((CONV_K, WIDTH)), _resident((3, WIDTH, D_MODEL)), _resident((1, 3 * D_MODEL))],
        [tile512] + [tile1k] * 7,
        [jax.ShapeDtypeStruct((t, WIDTH), BF16)] + [jax.ShapeDtypeStruct((t, D_MODEL), BF16)] * 7,
        (proj, proj, proj, proj, proj, proj, proj, proj, y_b, y_c, conv_w, w_branch, b_gate),
        sem=("parallel",), carry=carry)
    return outs if carry is None else (outs, recv)


def _merge_bwd(dmerged, projections, gates, w_branch, *, tm):
    t = dmerged.shape[0]

    def body(dm_ref, pa_ref, pb_ref, pc_ref, sa_ref, sb_ref, sc_ref, wb_ref,
             dgt_ref, dpa_ref, dpb_ref, dpc_ref, dya_ref, dyb_ref, dyc_ref, dbg_ref):
        i = pl.program_id(0)

        @pl.when(i == 0)
        def _():
            dbg_ref[...] = jnp.zeros_like(dbg_ref)

        dm = dm_ref[...].astype(F32)
        for idx, (p_ref, s_ref, dp_ref, dy_ref) in enumerate(((pa_ref, sa_ref, dpa_ref, dya_ref),
                                                              (pb_ref, sb_ref, dpb_ref, dyb_ref),
                                                              (pc_ref, sc_ref, dpc_ref, dyc_ref))):
            cols = slice(idx * D_MODEL, (idx + 1) * D_MODEL)
            sg = s_ref[...].astype(F32)
            dp = dm * sg
            dp_b = dp.astype(BF16)
            dp_ref[...] = dp_b
            dgate = dp * p_ref[...].astype(F32) * (1.0 - sg)
            dgt_ref[:, cols] = dgate.astype(BF16)
            dbg_ref[:, cols] += jnp.sum(dgate, axis=0, keepdims=True)
            dy_ref[...] = _dot_nt(dp_b, wb_ref[idx])

    tile512 = pl.BlockSpec((tm, WIDTH), lambda i: (i, 0))
    tile1k = pl.BlockSpec((tm, D_MODEL), lambda i: (i, 0))
    return pl.pallas_call(
        body, name="merge_bwd", grid=(t // tm,),
        in_specs=[tile1k] * 7 + [_resident((3, WIDTH, D_MODEL))],
        out_specs=[pl.BlockSpec((tm, 3 * D_MODEL), lambda i: (i, 0)), tile1k, tile1k, tile1k,
                   tile512, tile512, tile512, _resident((1, 3 * D_MODEL))],
        out_shape=[jax.ShapeDtypeStruct((t, 3 * D_MODEL), BF16)] + [jax.ShapeDtypeStruct((t, D_MODEL), BF16)] * 3
                  + [jax.ShapeDtypeStruct((t, WIDTH), F32)] * 3 + [jax.ShapeDtypeStruct((1, 3 * D_MODEL), F32)],
        compiler_params=_params("arbitrary"),
    )(dmerged, *projections, *gates, w_branch)


def _conv_bwd(proj, dya, conv_w, *, tm, seq):
    t = proj.shape[0]
    per_b = seq // tm
    hb = tm // HALO
    last_blk = t // HALO - 1

    def body(cb_ref, cc_ref, ch_ref, cch_ref, chh_ref, dya_ref, cbn_ref, dyan_ref, cw_ref, d_ref, dcw_ref):
        i = pl.program_id(0)

        @pl.when(i == 0)
        def _():
            dcw_ref[...] = jnp.zeros_like(dcw_ref)

        row = lax.broadcasted_iota(jnp.int32, (tm, WIDTH), 0)
        cb, cc, ch = cb_ref[...], cc_ref[...], ch_ref[...]
        u = cc * ch
        halo = jnp.where(i % per_b == 0, 0.0, cch_ref[...] * chh_ref[...])
        u1 = _shift_down(u, halo, 1, row)
        u2 = _shift_down(u, halo, 2, row)
        cw = cw_ref[...]
        y = cw[0:1] * u2 + cw[1:2] * u1 + cw[2:3] * u
        dya = dya_ref[...]
        dy = dya * cb
        nxt = jnp.where(i % per_b == per_b - 1, 0.0, dyan_ref[...] * cbn_ref[...])
        du = cw[2:3] * dy + cw[1:2] * _shift_up(dy, nxt, 1, row, tm) + cw[0:1] * _shift_up(dy, nxt, 2, row, tm)
        d_ref[:, 0:WIDTH] = (dya * y).astype(BF16)
        d_ref[:, WIDTH:2 * WIDTH] = (du * ch).astype(BF16)
        d_ref[:, 2 * WIDTH:3 * WIDTH] = (du * cc).astype(BF16)
        dcw_ref[0:1, :] += jnp.sum(dy * u2, axis=0, keepdims=True)
        dcw_ref[1:2, :] += jnp.sum(dy * u1, axis=0, keepdims=True)
        dcw_ref[2:3, :] += jnp.sum(dy * u, axis=0, keepdims=True)

    def half(c):
        return pl.BlockSpec((tm, WIDTH), lambda i: (i, c // WIDTH))

    def prev(c):
        return pl.BlockSpec((HALO, WIDTH), lambda i: (jnp.maximum(i * hb - 1, 0), c // WIDTH))

    def nxt(c):
        return pl.BlockSpec((HALO, WIDTH), lambda i: (jnp.minimum((i + 1) * hb, last_blk), c // WIDTH))

    return pl.pallas_call(
        body, name="conv_bwd", grid=(t // tm,),
        in_specs=[half(C_CB), half(C_CC), half(C_CH), prev(C_CC), prev(C_CH),
                  pl.BlockSpec((tm, WIDTH), lambda i: (i, 0)), nxt(C_CB), nxt(0), _resident((CONV_K, WIDTH))],
        out_specs=[pl.BlockSpec((tm, 3 * WIDTH), lambda i: (i, 0)), _resident((CONV_K, WIDTH))],
        out_shape=[jax.ShapeDtypeStruct((t, 3 * WIDTH), BF16), jax.ShapeDtypeStruct((CONV_K, WIDTH), F32)],
        compiler_params=_params("arbitrary"),
    )(proj, proj, proj, proj, proj, dya, proj, dya, conv_w)


def _loss_head(y, target, *, tm):
    t = y.shape[0]

    def body(y_ref, t_ref, dy_ref, l_ref):
        @pl.when(pl.program_id(0) == 0)
        def _():
            l_ref[...] = jnp.zeros_like(l_ref)

        err = y_ref[...] - t_ref[...]
        dy_ref[...] = err * (1.0 / D_MODEL)
        per_row = jnp.sum(err * err, axis=-1, keepdims=True) * (1.0 / D_MODEL)
        l_ref[...] += 0.5 * jnp.sum(per_row, axis=0, keepdims=True)

    row = pl.BlockSpec((tm, D_MODEL), lambda i: (i, 0))
    return pl.pallas_call(
        body, name="loss_head", grid=(t // tm,),
        in_specs=[row, row], out_specs=[row, _resident((8, 128))],
        out_shape=[jax.ShapeDtypeStruct((t, D_MODEL), F32), jax.ShapeDtypeStruct((8, 128), F32)],
        compiler_params=_params("arbitrary"),
    )(y, target)


def _lb_softmax(lower_bounds):
    x = lower_bounds
    e = jnp.exp(x - jnp.max(x, axis=0, keepdims=True))
    return e / jnp.sum(e, axis=0, keepdims=True)


def _lb_fwd(lower_bounds):
    def body(x_ref, o_ref):
        s = _lb_softmax(x_ref[...])
        c = s[0:1]
        o_ref[0:1, :] = c - s[0:1]
        for l in range(1, DEPTH):
            c = c + s[l:l + 1]
            o_ref[l:l + 1, :] = c - s[0:1]

    return pl.pallas_call(body, name="lb_fwd", out_shape=jax.ShapeDtypeStruct(lower_bounds.shape, F32))(lower_bounds)


def _lb_bwd(lower_bounds, d_lb_all):
    def body(x_ref, d_ref, o_ref):
        s = _lb_softmax(x_ref[...])
        d = d_ref[...]
        rows = [jnp.zeros_like(d[0:1])]
        for j in range(1, DEPTH):
            acc = d[j:j + 1]
            for l in range(j + 1, DEPTH):
                acc = acc + d[l:l + 1]
            rows.append(acc)
        inner = rows[0] * s[0:1]
        for j in range(1, DEPTH):
            inner = inner + rows[j] * s[j:j + 1]
        for j in range(DEPTH):
            o_ref[j:j + 1, :] = s[j:j + 1] * (rows[j] - inner)

    return pl.pallas_call(body, name="lb_bwd", out_shape=jax.ShapeDtypeStruct(lower_bounds.shape, F32))(
        lower_bounds, d_lb_all)


def _adamw(w, g, m, v):
    m2 = ADAM_B1 * m + (1.0 - ADAM_B1) * g
    v2 = ADAM_B2 * v + (1.0 - ADAM_B2) * (g * g)
    m_hat = m2 / (1.0 - ADAM_B1 ** ADAM_STEP)
    v_hat = v2 / (1.0 - ADAM_B2 ** ADAM_STEP)
    delta = -ADAM_LR * (m_hat / (jnp.sqrt(v_hat) + ADAM_EPS) + ADAM_WD * w)
    return delta, m2, v2


def _adam_small(name, g, w, m, v):
    shape = w.shape
    flat = (-1, shape[-1])
    g2, w2, m2, v2 = (a.reshape(flat) for a in (g, w, m, v))

    def body(g_ref, w_ref, m_ref, v_ref, d_ref, mo_ref, vo_ref):
        d, mm, vv = _adamw(w_ref[...], g_ref[...], m_ref[...], v_ref[...])
        d_ref[...] = d
        mo_ref[...] = mm
        vo_ref[...] = vv

    outs = pl.pallas_call(body, name=name, out_shape=[jax.ShapeDtypeStruct(w2.shape, F32)] * 3)(g2, w2, m2, v2)
    return [o.reshape(shape) for o in outs]


def _adam_shard(name, recvs, w, m, v, *, tr):
    _, r, c = w.shape

    def body(*refs):
        rc, (w_ref, m_ref, v_ref), (g_ref, d_ref, mo_ref, vo_ref) = refs[:DEPTH], refs[DEPTH:DEPTH + 3], refs[DEPTH + 3:]
        layer = pl.program_id(0)
        for cand in range(DEPTH):
            @pl.when(layer == cand)
            def _():
                g = rc[cand][0].astype(F32)
                for d in range(1, N_DEV):
                    g = g + rc[cand][d].astype(F32)
                dl, mm, vv = _adamw(w_ref[...], g, m_ref[...], v_ref[...])
                g_ref[...] = g
                d_ref[...] = dl
                mo_ref[...] = mm
                vo_ref[...] = vv

    def recv_spec(cand):
        return pl.BlockSpec((N_DEV, tr, c), lambda l, i: (0, jnp.where(l == cand, i, 0), 0))

    tile = pl.BlockSpec((None, tr, c), lambda l, i: (l, i, 0))
    return pl.pallas_call(
        body, name=name, grid=(DEPTH, r // tr),
        in_specs=[recv_spec(cand) for cand in range(DEPTH)] + [tile] * 3,
        out_specs=[tile] * 4,
        out_shape=[jax.ShapeDtypeStruct(w.shape, F32)] * 4,
        compiler_params=_params("parallel", "parallel"),
    )(*recvs, w, m, v)


def _sum_devices(name, x):
    def body(x_ref, o_ref):
        acc = x_ref[0]
        for d in range(1, N_DEV):
            acc = acc + x_ref[d]
        o_ref[...] = acc

    return pl.pallas_call(body, name=name, out_shape=jax.ShapeDtypeStruct(x.shape[1:], x.dtype))(x)


SMALL = (("lower_bounds", 1, 512), ("conv_w", CONV_K, WIDTH), ("hg_norm_w", 1, HEAD_DIM), ("b_gate", 3, D_MODEL),
         ("ln1_g", 1, D_MODEL), ("ln1_b", 1, D_MODEL), ("ln2_g", 1, D_MODEL), ("ln2_b", 1, D_MODEL))
SMALL_ROWS = sum(r for _, r, _ in SMALL)


def _pack_small(per_layer):
    flat = [a for layer in per_layer for a in layer]

    def body(*refs):
        ins, o_ref = refs[:-1], refs[-1]
        o_ref[...] = jnp.zeros_like(o_ref)
        it = iter(ins)
        for l in range(DEPTH):
            row = l * SMALL_ROWS
            for name, nrows, ncols in SMALL:
                ref = next(it)
                if name == "b_gate":
                    for k in range(nrows):
                        o_ref[row + k:row + k + 1, :] = ref[:, k * ncols:(k + 1) * ncols]
                else:
                    o_ref[row:row + nrows, 0:ncols] = ref[...]
                row += nrows

    return pl.pallas_call(body, name="pack_small_grads",
                          out_shape=jax.ShapeDtypeStruct((DEPTH * SMALL_ROWS, D_MODEL), F32))(*flat)


def _unpack_small(summed):
    s3 = summed.reshape(DEPTH, SMALL_ROWS, D_MODEL)
    out, row = {}, 0
    for name, nrows, ncols in SMALL:
        out[name] = s3[:, row:row + nrows, :ncols].reshape(DEPTH, nrows * ncols)
        row += nrows
    return out


def _natural_cols(g):
    nd = g.ndim
    perm = tuple(range(1, nd - 1)) + (0, nd - 1)
    t = jnp.transpose(g, perm)
    return t.reshape(t.shape[:-2] + (t.shape[-2] * t.shape[-1],))


def _natural_rows(g):
    return g.reshape(g.shape[0] * g.shape[1], g.shape[2])


def _hosted(hosts, key, fn):
    if not hosts or key not in hosts:
        return fn(None)
    ex, hook = hosts[key]
    outs, recv = fn(ex)
    hook(recv)
    return outs


def _layer_fwd(cur, cur_b, mem2d, wl, *, bsz, seq, hosts=None):
    tm = min(512, seq)
    proj = _hosted(hosts, "in_proj", lambda c: _mm_nn("in_proj", cur_b, wl["w_in"], tm=min(1024, seq), tn=1024,
                                                       out_dtype=F32, carry=c))
    y_b, o_pre, states = _hosted(hosts, "hgrn_fwd", lambda c: _hgrn_fwd(proj, wl["lb"], wl["nw"], bsz=bsz, seq=seq,
                                                                         carry=c))
    mk, mv = _mem_kv(mem2d, wl["w_mk"], wl["w_mv"])
    y_c = _attn_fwd(proj, mk, mv, tm=tm, seq=seq)
    y_a, pa, pb, pc, merged, sga, sgb, sgc = _hosted(
        hosts, "merge_fwd", lambda c: _merge_fwd(proj, y_b, y_c, wl["conv"], wl["w_br"], wl["b_gate"], tm=tm,
                                                 seq=seq, carry=c))
    z1, x1, x1_b = _hosted(hosts, "wo_ln", lambda c: _linear_ln("wo_ln", merged, wl["w_o"], cur, wl["ln1_g"],
                                                                  wl["ln1_b"], tm=tm, carry=c))
    hid = _hosted(hosts, "mlp_up", lambda c: _mm_nn("mlp_up", x1_b, wl["w_up"], tm=tm, tn=1024, out_dtype=BF16,
                                                     relu2=True, carry=c))
    z2, x2, x2_b = _linear_ln("down_ln", hid, wl["w_down"], x1, wl["ln2_g"], wl["ln2_b"], tm=tm)
    return dict(x_b=cur_b, proj=proj, y_a=y_a, y_b=y_b, y_c=y_c, o_pre=o_pre, states=states, mk=mk, mv=mv,
                proj3=(pa, pb, pc), gates3=(sga, sgb, sgc), merged=merged, z1=z1, x1_b=x1_b, hid=hid, z2=z2, x2=x2,
                x2_b=x2_b)


def _layer_bwd(dcur, mem2d, s, wl, *, bsz, seq, plan=None):
    tm = min(512, seq)
    tk = min(2048, bsz * seq)
    g = {}

    def run(key, fn):
        made = plan[key](g) if plan and key in plan else None
        return _hosted({key: made} if made else None, key, fn)

    dz2, dz2_b, dhpre, d_ln2g, d_ln2b = run(
        "ln2_bwd_down", lambda c: _ln_bwd_mm_nt("ln2_bwd_down", dcur, s["z2"], wl["ln2_g"], wl["w_down"],
                                                s["hid"], tm=tm, tn=1024, carry=c))
    g["w_down"] = _mm_tn("grad_w_down", s["hid"], dz2_b, tk=tk, tmo=1024, tno=1024)
    dx1 = _mm_nt_sum("mlp_up_bwd", [dhpre], [0], wl["w_up"], dz2, tm=tm)
    g["w_up"] = run("grad_w_up", lambda c: _mm_tn("grad_w_up", s["x1_b"], dhpre, tk=tk, tmo=1024, tno=1024, carry=c))
    dz1, dz1_b, dmerged, d_ln1g, d_ln1b = _ln_bwd_mm_nt("ln1_bwd_wo", dx1, s["z1"], wl["ln1_g"], wl["w_o"],
                                                        tm=tm, tn=1024)
    g["w_o"] = _mm_tn("grad_w_o", s["merged"], dz1_b, tk=tk, tmo=1024, tno=1024)
    dgate, dpa, dpb, dpc, dya, dyb, dyc, d_bg = _merge_bwd(dmerged, s["proj3"], s["gates3"], wl["w_br"], tm=tm)
    g["w_br"] = jnp.stack([_mm_tn("grad_w_branch", yy, dp, tk=tk, tmo=512, tno=1024)
                           for yy, dp in ((s["y_a"], dpa), (s["y_b"], dpb), (s["y_c"], dpc))])
    d_conv, d_cw = _conv_bwd(s["proj"], dya, wl["conv"], tm=tm, seq=seq)
    dhg, d_lb, d_nw = run(
        "hgrn_bwd", lambda c: _hgrn_bwd(s["proj"], wl["lb"], wl["nw"], s["o_pre"], s["states"], dyb,
                                        bsz=bsz, seq=seq, carry=c))
    dmq, dmk, dmv = _attn_bwd(s["proj"], s["mk"], s["mv"], dyc, tm=tm, seq=seq)
    tkm = min(512, mem2d.shape[0])
    g["w_mk"] = _mm_tn("grad_w_mem", mem2d, dmk, tk=tkm, tmo=1024, tno=512)
    g["w_mv"] = _mm_tn("grad_w_mem", mem2d, dmv, tk=tkm, tmo=1024, tno=512)
    pieces = [d_conv, dhg, dmq, dgate]
    offsets = [C_CB, C_HQ, C_MQ, C_GA]
    g["w_in"] = jnp.concatenate(
        [_mm_tn("grad_w_in_%d" % p.shape[1], s["x_b"], p, tk=tk, tmo=1024,
                tno=next(w for w in (1024, 768, 512) if p.shape[1] % w == 0)) for p in pieces], axis=1)
    dx = run("in_proj_bwd", lambda c: _mm_nt_sum("in_proj_bwd", pieces, offsets, wl["w_in"], dz1,
                                                 tm=min(256, seq), carry=c))
    return dx, g, [d_lb, d_cw, d_nw, d_bg, d_ln1g, d_ln1b, d_ln2g, d_ln2b]


def kernel(x, mem, lower_bounds, w_in, conv_w, hg_norm_w, w_mem_k, w_mem_v, w_branch, b_gate, w_o, ln1_g, ln1_b, w_up, w_down, ln2_g, ln2_b, loss_target, m_lower_bounds, m_w_in, m_conv_w, m_hg_norm_w, m_w_mem_k, m_w_mem_v, m_w_branch, m_b_gate, m_w_o, m_ln1_g, m_ln1_b, m_w_up, m_w_down, m_ln2_g, m_ln2_b, v_lower_bounds, v_w_in, v_conv_w, v_hg_norm_w, v_w_mem_k, v_w_mem_v, v_w_branch, v_b_gate, v_w_o, v_ln1_g, v_ln1_b, v_w_up, v_w_down, v_ln2_g, v_ln2_b):
    bsz, seq, _ = x.shape
    t = bsz * seq
    me = _my_id()

    sh = dict(w_in=w_in.astype(BF16), w_mk=w_mem_k.astype(BF16), w_mv=w_mem_v.astype(BF16),
              w_br=w_branch.astype(BF16), w_o=w_o.astype(BF16), w_up=w_up.astype(BF16), w_down=w_down.astype(BF16))
    half_rows = D_MODEL // 2
    sh["w_in_a"], sh["w_in_b"] = sh["w_in"][:, :half_rows], sh["w_in"][:, half_rows:]
    natural = dict(w_in=_natural_cols, w_in_a=_natural_cols, w_in_b=_natural_cols, w_mk=_natural_rows,
                   w_mv=_natural_rows, w_br=_natural_cols, w_o=_natural_rows, w_up=_natural_cols,
                   w_down=_natural_rows)

    def gather_of(names, l):
        srcs = [sh[n][l] for n in names]
        return _Exchange(srcs, [_whole] * len(srcs), [s_.shape for s_ in srcs])

    def put(names, into):
        def hook(recv_):
            for n, r in zip(names, recv_):
                into[n] = natural[n](r)
        return hook

    lb_all = _lb_fwd(lower_bounds)
    layer_w = [dict(lb=lb_all[l][None], nw=hg_norm_w[l][None], b_gate=b_gate[l][None], ln1_g=ln1_g[l][None],
                    ln1_b=ln1_b[l][None], ln2_g=ln2_g[l][None], ln2_b=ln2_b[l][None]) for l in range(DEPTH)]
    first = ["w_in"]
    conv_shard = conv_w.reshape(DEPTH * CONV_K * (WIDTH // N_DEV) // 128, 128)
    ex0 = gather_of(first, 0)
    got = _exchange("gather_first", ex0.srcs + [conv_shard], [_whole] * (len(first) + 1),
                    [s_.shape for s_ in ex0.srcs] + [conv_shard.shape])
    put(first, layer_w[0])(got[:len(first)])
    conv_full = _natural_cols(got[-1].reshape(N_DEV, DEPTH, CONV_K, WIDTH // N_DEV))

    x2d = x.reshape(t, D_MODEL)
    mem2d = mem.reshape(bsz * MEM_LEN, D_MODEL)
    target2d = loss_target.reshape(t, D_MODEL)

    saved = []
    cur, cur_b = x2d, x2d.astype(BF16)
    for l in range(DEPTH):
        wl = layer_w[l]
        wl["conv"] = conv_full[l]
        if "w_in" not in wl:
            wl["w_in"] = jnp.concatenate([wl.pop("w_in_a"), wl.pop("w_in_b")], axis=0)
        now = ["w_up"] if l else ["w_up", "w_mk", "w_mv", "w_br", "w_o"]
        hosts = {"in_proj": (gather_of(now, l), put(now, wl)),
                 "hgrn_fwd": (gather_of(["w_down"], l), put(["w_down"], wl))}
        if l + 1 < DEPTH:
            nxt = layer_w[l + 1]
            hosts["merge_fwd"] = (gather_of(["w_in_a"], l + 1), put(["w_in_a"], nxt))
            hosts["wo_ln"] = (gather_of(["w_mk", "w_mv", "w_o"], l + 1), put(["w_mk", "w_mv", "w_o"], nxt))
            hosts["mlp_up"] = (gather_of(["w_in_b", "w_br"], l + 1), put(["w_in_b", "w_br"], nxt))
        s = _layer_fwd(cur, cur_b, mem2d, wl, bsz=bsz, seq=seq, hosts=hosts)
        saved.append(s)
        cur, cur_b = s["x2"], s["x2_b"]

    dcur, loss_tile = _loss_head(cur, target2d, tm=min(512, seq))
    loss = lax.psum(loss_tile[0, 0], ("x", "y", "c"))

    in_w = IN_COLS // N_DEV

    def in_half(r):
        return lambda ref, j: ref.at[pl.ds(r * half_rows, half_rows), pl.ds(j * in_w, in_w)]

    slicer = dict(w_in_a=in_half(0), w_in_b=in_half(1), w_mk=_rows(D_MODEL // N_DEV), w_mv=_rows(D_MODEL // N_DEV),
                  w_br=_cols(D_MODEL // N_DEV), w_o=_rows(D_MODEL // N_DEV), w_up=_cols(D_FF // N_DEV),
                  w_down=_rows(D_FF // N_DEV))
    source = dict(w_in_a="w_in", w_in_b="w_in")
    recv = [dict() for _ in range(DEPTH)]

    def scatter_of(names, g, into):
        ex = _Exchange([g[source.get(n, n)] for n in names], [slicer[n] for n in names],
                       [sh[n].shape[1:] for n in names])
        return ex, lambda recv_: into.update(zip(names, recv_))

    small_rows = [None] * DEPTH
    prev = None
    rest = ["w_in_b", "w_mk", "w_mv"]
    for l in reversed(range(DEPTH)):
        plan = {"grad_w_up": lambda g, l=l: scatter_of(["w_down"], g, recv[l]),
                "hgrn_bwd": lambda g, l=l: scatter_of(["w_up", "w_o", "w_br"], g, recv[l])}
        if l == 0:
            plan["in_proj_bwd"] = lambda g: scatter_of(["w_in_a"] + rest, g, recv[0])
        else:
            plan["in_proj_bwd"] = lambda g, l=l: scatter_of(["w_in_a"], g, recv[l])
        if prev is not None:
            plan["ln2_bwd_down"] = lambda g, l=l, prev=prev: scatter_of(rest, prev, recv[l + 1])
        dcur, prev, small_rows[l] = _layer_bwd(dcur, mem2d, saved[l], layer_w[l], bsz=bsz, seq=seq, plan=plan)
    for r in recv:
        r["w_in"] = jnp.concatenate([r.pop("w_in_a"), r.pop("w_in_b")], axis=1)

    packed = _pack_small(small_rows)
    all_small = _exchange("gather_small_grads", [packed], [_whole], [packed.shape])[0]
    small_grads = _unpack_small(_sum_devices("sum_small_grads", all_small))
    small_grads["lower_bounds"] = _lb_bwd(lower_bounds, small_grads["lower_bounds"])
    conv_all = small_grads["conv_w"].reshape(DEPTH, CONV_K, WIDTH)
    small_grads["conv_w"] = lax.dynamic_slice_in_dim(conv_all, me * (WIDTH // N_DEV), WIDTH // N_DEV, axis=2)

    grads, deltas, new_m, new_v = {}, {}, {}, {}
    given = dict(lower_bounds=(lower_bounds, m_lower_bounds, v_lower_bounds), conv_w=(conv_w, m_conv_w, v_conv_w),
                 hg_norm_w=(hg_norm_w, m_hg_norm_w, v_hg_norm_w), b_gate=(b_gate, m_b_gate, v_b_gate),
                 ln1_g=(ln1_g, m_ln1_g, v_ln1_g), ln1_b=(ln1_b, m_ln1_b, v_ln1_b),
                 ln2_g=(ln2_g, m_ln2_g, v_ln2_g), ln2_b=(ln2_b, m_ln2_b, v_ln2_b))
    for name, (w_, m_, v_) in given.items():
        g_ = small_grads[name].reshape(w_.shape)
        grads[name] = g_
        deltas[name], new_m[name], new_v[name] = _adam_small("adam_" + name, g_, w_, m_, v_)

    big = dict(w_in=("w_in", w_in, m_w_in, v_w_in, 128), w_mem_k=("w_mk", w_mem_k, m_w_mem_k, v_w_mem_k, 128),
               w_mem_v=("w_mv", w_mem_v, m_w_mem_v, v_w_mem_v, 128),
               w_branch=("w_br", w_branch, m_w_branch, v_w_branch, 512), w_o=("w_o", w_o, m_w_o, v_w_o, 128),
               w_up=("w_up", w_up, m_w_up, v_w_up, 256), w_down=("w_down", w_down, m_w_down, v_w_down, 128))
    for name, (k, w_, m_, v_, tr) in big.items():
        shape = w_.shape
        flat = (DEPTH, -1, shape[-1])
        rc = [recv[l][k].reshape((N_DEV,) + w_.reshape(flat).shape[1:]) for l in range(DEPTH)]
        outs = _adam_shard("adam_" + name, rc, w_.reshape(flat), m_.reshape(flat), v_.reshape(flat), tr=tr)
        grads[name], deltas[name], new_m[name], new_v[name] = (o.reshape(shape) for o in outs)

    order = ["lower_bounds", "w_in", "conv_w", "hg_norm_w", "w_mem_k", "w_mem_v", "w_branch", "b_gate", "w_o",
             "ln1_g", "ln1_b", "w_up", "w_down", "ln2_g", "ln2_b"]
    return (loss, dcur.reshape(x.shape), *[grads[n] for n in order], *[deltas[n] for n in order],
            *[new_m[n] for n in order], *[new_v[n] for n in order])
```

```python
import functools

import jax
import jax.numpy as jnp
from jax import lax
from jax.experimental import pallas as pl
from jax.experimental.pallas import tpu as pltpu

F32 = jnp.float32
BF16 = jnp.bfloat16

N_DEV = 8
D_MODEL = 1024
DEPTH = 4
MEM_LEN = 256
CONV_K = 3
WIDTH = 512
HEADS = 4
HEAD_DIM = 128
CHUNK = 32
D_FF = 4 * D_MODEL
IN_COLS = 7168
ALPHA = (2.0 * DEPTH) ** 0.25
LN_EPS = 1e-5
RMS_EPS = 1e-6
ADAM_LR = 0.001
ADAM_B1 = 0.9
ADAM_B2 = 0.999
ADAM_EPS = 1e-08
ADAM_WD = 0.01
ADAM_STEP = 10

C_CB, C_CC, C_CH, C_HQ, C_HF, C_HI, C_HG, C_MQ, C_GA = 0, 512, 1024, 1536, 2048, 2560, 3072, 3584, 4096

ROWS_HG = 256
NT_DIMS = (((1,), (1,)), ((), ()))
TN_DIMS = (((0,), (0,)), ((), ()))
MESH = pl.DeviceIdType.MESH


def _dot(a, b):
    return jnp.dot(a, b, preferred_element_type=F32)


def _dot_nt(a, b):
    return lax.dot_general(a, b, NT_DIMS, preferred_element_type=F32)


def _dot_tn(a, b):
    return lax.dot_general(a, b, TN_DIMS, preferred_element_type=F32)


def _sigmoid(x):
    return 1.0 / (1.0 + jnp.exp(-x))


def _params(*sem):
    return pltpu.CompilerParams(dimension_semantics=sem)


def _resident(shape):
    nd = len(shape)
    return pl.BlockSpec(shape, lambda *_: (0,) * nd)


def _my_id():
    return 4 * lax.axis_index("x") + 2 * lax.axis_index("y") + lax.axis_index("c")


class _Exchange:
    def __init__(self, srcs, slicers, piece_shapes):
        self.srcs, self.slicers, self.n = list(srcs), list(slicers), len(srcs)
        any_spec = pl.BlockSpec(memory_space=pl.ANY)
        self.in_specs = [any_spec] * self.n
        self.out_specs = [any_spec] * self.n
        self.out_shape = [jax.ShapeDtypeStruct((N_DEV,) + tuple(s), a.dtype) for s, a in zip(piece_shapes, srcs)]
        self.scratch = [pltpu.SemaphoreType.DMA((self.n * N_DEV,)), pltpu.SemaphoreType.DMA((self.n * N_DEV,)),
                        pltpu.SemaphoreType.DMA((self.n,))]

    def _remote(self, ins, outs, sems, k, j, me):
        return pltpu.make_async_remote_copy(
            src_ref=self.slicers[k](ins[k], j), dst_ref=outs[k].at[me],
            send_sem=sems[0].at[k * N_DEV + j], recv_sem=sems[1].at[k * N_DEV + me],
            device_id=(j // 4, (j // 2) % 2, j % 2), device_id_type=MESH)

    def _local(self, ins, outs, sems, k, j, me):
        return pltpu.make_async_copy(self.slicers[k](ins[k], j), outs[k].at[me], sems[2].at[k])

    def start(self, ins, outs, sems):
        me = _my_id()
        for k in range(self.n):
            for j in range(N_DEV):
                @pl.when(j != me)
                def _():
                    self._remote(ins, outs, sems, k, j, me).start()

                @pl.when(j == me)
                def _():
                    self._local(ins, outs, sems, k, j, me).start()

    def wait(self, ins, outs, sems):
        me = _my_id()
        for k in range(self.n):
            for j in range(N_DEV):
                @pl.when(j != me)
                def _():
                    pltpu.make_async_remote_copy(
                        src_ref=self.slicers[k](ins[k], j), dst_ref=outs[k].at[j],
                        send_sem=sems[0].at[k * N_DEV + j], recv_sem=sems[1].at[k * N_DEV + j],
                        device_id=(j // 4, (j // 2) % 2, j % 2), device_id_type=MESH).wait_recv()
                    self._remote(ins, outs, sems, k, j, me).wait_send()

                @pl.when(j == me)
                def _():
                    self._local(ins, outs, sems, k, j, me).wait()


def _exchange(name, srcs, slicers, piece_shapes):
    ex = _Exchange(srcs, slicers, piece_shapes)

    def body(*refs):
        ins, outs, sems = refs[:ex.n], refs[ex.n:2 * ex.n], refs[2 * ex.n:]
        ex.start(ins, outs, sems)
        ex.wait(ins, outs, sems)

    return pl.pallas_call(
        body, name=name, in_specs=ex.in_specs, out_specs=ex.out_specs, out_shape=ex.out_shape,
        scratch_shapes=ex.scratch, compiler_params=pltpu.CompilerParams(has_side_effects=True),
    )(*ex.srcs)


def _call(body, name, grid, in_specs, out_specs, out_shape, args, scratch=(), sem=None, carry=None):
    n_in, n_out, n_scr = len(in_specs), len(out_specs), len(scratch)
    if carry is None:
        outs = pl.pallas_call(body, name=name, grid=grid, in_specs=in_specs, out_specs=out_specs,
                              out_shape=out_shape, scratch_shapes=list(scratch),
                              compiler_params=_params(*sem))(*args)
        return outs, None
    nc = carry.n

    def hosted(*refs):
        ins, c_in = refs[:n_in], refs[n_in:n_in + nc]
        outs = refs[n_in + nc:n_in + nc + n_out]
        c_out = refs[n_in + nc + n_out:n_in + 2 * nc + n_out]
        rest = refs[n_in + 2 * nc + n_out:]
        scr, sems = rest[:n_scr], rest[n_scr:]
        first, last = True, True
        for d, size in enumerate(grid):
            first = first & (pl.program_id(d) == 0)
            last = last & (pl.program_id(d) == size - 1)

        @pl.when(first)
        def _():
            carry.start(c_in, c_out, sems)

        body(*ins, *outs, *scr)

        @pl.when(last)
        def _():
            carry.wait(c_in, c_out, sems)

    outs = pl.pallas_call(
        hosted, name=name + "_x", grid=grid, in_specs=list(in_specs) + carry.in_specs,
        out_specs=list(out_specs) + carry.out_specs, out_shape=list(out_shape) + carry.out_shape,
        scratch_shapes=list(scratch) + carry.scratch,
        compiler_params=_params(*(["arbitrary"] * len(grid))))(*args, *carry.srcs)
    return outs[:n_out], outs[n_out:]


def _whole(ref, j):
    return ref


def _cols(width):
    return lambda ref, j: ref.at[(slice(None),) * (len(ref.shape) - 1) + (pl.ds(j * width, width),)]


def _rows(height):
    return lambda ref, j: ref.at[pl.ds(j * height, height)]


def _mm_nn(name, a, w, *, tm, tn, out_dtype, relu2=False, carry=None):
    t, k = a.shape
    n = w.shape[1]

    def body(a_ref, w_ref, o_ref):
        acc = _dot(a_ref[...].astype(BF16), w_ref[...])
        if relu2:
            r = jnp.maximum(acc, 0.0)
            acc = r * r
        o_ref[...] = acc.astype(out_dtype)

    outs, recv = _call(
        body, name, (t // tm, n // tn),
        [pl.BlockSpec((tm, k), lambda i, j: (i, 0)), pl.BlockSpec((k, tn), lambda i, j: (0, j))],
        [pl.BlockSpec((tm, tn), lambda i, j: (i, j))], [jax.ShapeDtypeStruct((t, n), out_dtype)], (a, w),
        sem=("parallel", "parallel"), carry=carry)
    return outs[0] if carry is None else (outs[0], recv)


def _in_proj(a, w, *, tm, carry=None):
    t, k = a.shape
    tn = 1024
    f_tile, f_off = C_HF // tn, C_HF % tn

    def body(a_ref, w_ref, o_ref, f_ref):
        acc = _dot(a_ref[...], w_ref[...])
        o_ref[...] = acc.astype(BF16)

        @pl.when(pl.program_id(1) == f_tile)
        def _():
            f_ref[...] = acc[:, f_off:f_off + WIDTH]

    outs, recv = _call(
        body, "in_proj", (t // tm, IN_COLS // tn),
        [pl.BlockSpec((tm, k), lambda i, j: (i, 0)), pl.BlockSpec((k, tn), lambda i, j: (0, j))],
        [pl.BlockSpec((tm, tn), lambda i, j: (i, j)), pl.BlockSpec((tm, WIDTH), lambda i, j: (i, 0))],
        [jax.ShapeDtypeStruct((t, IN_COLS), BF16), jax.ShapeDtypeStruct((t, WIDTH), F32)], (a, w),
        sem=("parallel", "arbitrary"), carry=carry)
    return outs if carry is None else (outs, recv)


def _layer_norm(z, g, b):
    mu = jnp.mean(z, axis=-1, keepdims=True)
    zc = z - mu
    var = jnp.mean(zc * zc, axis=-1, keepdims=True)
    return zc * lax.rsqrt(var + LN_EPS) * g + b


def _linear_ln(name, a, w, resid, g, b, *, tm, carry=None):
    t, k = a.shape

    def body(a_ref, w_ref, r_ref, g_ref, b_ref, z_ref, x_ref, xb_ref):
        z = ALPHA * r_ref[...] + _dot(a_ref[...], w_ref[...])
        z_ref[...] = z
        y = _layer_norm(z, g_ref[...], b_ref[...])
        x_ref[...] = y
        xb_ref[...] = y.astype(BF16)

    row = pl.BlockSpec((tm, D_MODEL), lambda i: (i, 0))
    outs, recv = _call(
        body, name, (t // tm,),
        [pl.BlockSpec((tm, k), lambda i: (i, 0)), _resident((k, D_MODEL)), row,
         _resident((1, D_MODEL)), _resident((1, D_MODEL))],
        [row, row, row],
        [jax.ShapeDtypeStruct((t, D_MODEL), F32)] * 2 + [jax.ShapeDtypeStruct((t, D_MODEL), BF16)],
        (a, w, resid, g, b), sem=("parallel",), carry=carry)
    return outs if carry is None else (outs, recv)


def _ln_bwd_mm_nt(name, dy, z, g, w, h=None, *, tm, tn, carry=None):
    t = dy.shape[0]
    n = w.shape[0]
    halves = [slice(0, tm // 2), slice(tm // 2, tm)]

    def body(*refs):
        if h is None:
            dy_ref, z_ref, g_ref, w_ref, dz_ref, dzb_ref, o_ref, dg_ref, db_ref = refs
        else:
            dy_ref, z_ref, g_ref, w_ref, h_ref, dz_ref, dzb_ref, o_ref, dg_ref, db_ref = refs

        @pl.when(pl.program_id(0) == 0)
        def _():
            dg_ref[...] = jnp.zeros_like(dg_ref)
            db_ref[...] = jnp.zeros_like(db_ref)

        zv = _Lanes(z_ref[s, :] for s in halves)
        dyv = _Lanes(dy_ref[s, :] for s in halves)
        mu = _mean(zv, axis=-1, keepdims=True)
        zc = zv - mu
        rstd = _rsqrt(_mean(zc * zc, axis=-1, keepdims=True) + LN_EPS)
        xh = zc * rstd
        gdy = dyv * g_ref[...]
        m1 = _mean(gdy, axis=-1, keepdims=True)
        m2 = _mean(gdy * xh, axis=-1, keepdims=True)
        dz = rstd * (gdy - m1 - xh * m2)
        dz_b = dz.astype(BF16)
        for s, a, a_b in zip(halves, dz.xs, dz_b.xs):
            dz_ref[s, :] = a
            dzb_ref[s, :] = a_b
        dg_ref[...] += _sum(dyv * xh, axis=0, keepdims=True).total()
        db_ref[...] += _sum(dyv, axis=0, keepdims=True).total()
        for c in range(n // tn):
            cols = slice(c * tn, (c + 1) * tn)
            acc = _ldot_nt(dz_b, w_ref[cols, :])
            if h is not None:
                acc = acc * (2.0 * _sqrt(_Lanes(h_ref[s, cols] for s in halves).astype(F32)))
            for s, a in zip(halves, acc.xs):
                o_ref[s, cols] = a.astype(BF16)

    row = pl.BlockSpec((tm, D_MODEL), lambda i: (i, 0))
    vec = _resident((1, D_MODEL))
    tile = pl.BlockSpec((tm, n), lambda i: (i, 0))
    in_specs = [row, row, vec, _resident((n, D_MODEL))]
    args = [dy, z, g, w]
    if h is not None:
        in_specs.append(tile)
        args.append(h)
    outs, recv = _call(
        body, name, (t // tm,), in_specs, [row, row, tile, vec, vec],
        [jax.ShapeDtypeStruct((t, D_MODEL), F32), jax.ShapeDtypeStruct((t, D_MODEL), BF16),
         jax.ShapeDtypeStruct((t, n), BF16), jax.ShapeDtypeStruct((1, D_MODEL), F32),
         jax.ShapeDtypeStruct((1, D_MODEL), F32)], args, sem=("arbitrary",), carry=carry)
    return outs if carry is None else (outs, recv)


def _mm_tn(name, a, b, *, tk, tmo, tno, carry=None):
    t, m = a.shape
    n = b.shape[1]
    nk = t // tk

    def body(a_ref, b_ref, o_ref, acc_ref):
        k = pl.program_id(2)
        p = _dot_tn(a_ref[...].astype(BF16), b_ref[...].astype(BF16))

        @pl.when(k == 0)
        def _():
            acc_ref[...] = p

        @pl.when(k > 0)
        def _():
            acc_ref[...] += p

        @pl.when(k == nk - 1)
        def _():
            o_ref[...] = acc_ref[...].astype(BF16)

    outs, recv = _call(
        body, name, (m // tmo, n // tno, nk),
        [pl.BlockSpec((tk, tmo), lambda i, j, k: (k, i)), pl.BlockSpec((tk, tno), lambda i, j, k: (k, j))],
        [pl.BlockSpec((tmo, tno), lambda i, j, k: (i, j))], [jax.ShapeDtypeStruct((m, n), BF16)], (a, b),
        scratch=[pltpu.VMEM((tmo, tno), F32)], sem=("parallel", "parallel", "arbitrary"), carry=carry)
    return outs[0] if carry is None else (outs[0], recv)


def _mm_nt_sum(name, pieces, offsets, w, resid, *, tm, carry=None):
    t = resid.shape[0]
    widths = [p.shape[1] for p in pieces]
    n_p = len(pieces)

    def body(*refs):
        p_refs, w_ref, r_ref, o_ref = refs[:n_p], refs[n_p], refs[n_p + 1], refs[n_p + 2]
        acc = ALPHA * r_ref[...]
        for p_ref, off, wd in zip(p_refs, offsets, widths):
            acc = acc + _dot_nt(p_ref[...], w_ref[:, off:off + wd])
        o_ref[...] = acc

    row = pl.BlockSpec((tm, D_MODEL), lambda i: (i, 0))
    outs, recv = _call(
        body, name, (t // tm,),
        [pl.BlockSpec((tm, wd), lambda i: (i, 0)) for wd in widths] + [_resident(w.shape), row],
        [row], [jax.ShapeDtypeStruct((t, D_MODEL), F32)], (*pieces, w, resid), sem=("parallel",), carry=carry)
    return outs[0] if carry is None else (outs[0], recv)


def _chunk_mask(rows):
    r = lax.broadcasted_iota(jnp.int32, (rows, rows), 0)
    c = lax.broadcasted_iota(jnp.int32, (rows, rows), 1)
    return ((r // CHUNK) == (c // CHUNK)) & (c <= r)


class _Lanes:
    def __init__(self, xs):
        self.xs = list(xs)

    def _with(self, other, f):
        if isinstance(other, _Lanes):
            return _Lanes([f(a, b) for a, b in zip(self.xs, other.xs)])
        return _Lanes([f(a, other) for a in self.xs])

    def __add__(self, o):
        return self._with(o, lambda a, b: a + b)

    def __radd__(self, o):
        return self._with(o, lambda a, b: b + a)

    def __sub__(self, o):
        return self._with(o, lambda a, b: a - b)

    def __rsub__(self, o):
        return self._with(o, lambda a, b: b - a)

    def __mul__(self, o):
        return self._with(o, lambda a, b: a * b)

    def __rmul__(self, o):
        return self._with(o, lambda a, b: b * a)

    def __truediv__(self, o):
        return self._with(o, lambda a, b: a / b)

    def __rtruediv__(self, o):
        return self._with(o, lambda a, b: b / a)

    def __neg__(self):
        return _Lanes([-a for a in self.xs])

    def __ge__(self, o):
        return self._with(o, lambda a, b: a >= b)

    def __getitem__(self, idx):
        return _Lanes([a[idx] for a in self.xs])

    def astype(self, dtype):
        return _Lanes([a.astype(dtype) for a in self.xs])

    def total(self):
        return functools.reduce(lambda a, b: a + b, self.xs)


def _lift(f):
    def g(*args, **kw):
        lanes = [a for a in args if isinstance(a, _Lanes)]
        if not lanes:
            return f(*args, **kw)
        return _Lanes([f(*[a.xs[i] if isinstance(a, _Lanes) else a for a in args], **kw)
                       for i in range(len(lanes[0].xs))])
    return g


def _concat(parts, axis):
    if isinstance(parts[0], _Lanes):
        return _Lanes([jnp.concatenate([p.xs[i] for p in parts], axis=axis) for i in range(len(parts[0].xs))])
    return jnp.concatenate(parts, axis=axis)


_exp, _log, _abs, _sqrt, _where = _lift(jnp.exp), _lift(jnp.log), _lift(jnp.abs), _lift(jnp.sqrt), _lift(jnp.where)
_sum, _mean, _rsqrt, _bcast = _lift(jnp.sum), _lift(jnp.mean), _lift(lax.rsqrt), _lift(jnp.broadcast_to)
_ldot, _ldot_nt, _ldot_tn = _lift(_dot), _lift(_dot_nt), _lift(_dot_tn)
_lsigmoid = _lift(_sigmoid)


def _mask_sum(mask_b, x, transpose=False):
    f = _ldot_tn if transpose else _ldot
    hi = x.astype(BF16)
    lo = (x - hi.astype(F32)).astype(BF16)
    return f(mask_b, hi) + f(mask_b, lo)


def _chunk_row(x, pos, rows):
    nc = rows // CHUNK

    def one(a):
        a3 = a.reshape(nc, CHUNK, HEAD_DIM)
        return jnp.broadcast_to(a3[:, pos:pos + 1, :], (nc, CHUNK, HEAD_DIM)).reshape(rows, HEAD_DIM)

    return _lift(one)(x)


def _chunk_total(x, rows):
    nc = rows // CHUNK

    def one(a):
        tot = jnp.sum(a.reshape(nc, CHUNK, HEAD_DIM), axis=1, keepdims=True)
        return jnp.broadcast_to(tot, (nc, CHUNK, HEAD_DIM)).reshape(rows, HEAD_DIM)

    return _lift(one)(x)


def _sigmoid_pair(x):
    e = _exp(-_abs(x))
    big = 1.0 / (1.0 + e)
    small = e * big
    pos = x >= 0.0
    return _where(pos, big, small), _where(pos, small, big)


def _hg_gates(q_raw, fl, lb, rows, mask):
    tri = mask.astype(BF16)
    sg, sg_neg = _sigmoid_pair(fl)
    forget = lb + (1.0 - lb) * sg
    k = (1.0 - lb) * sg_neg
    sq = _lsigmoid(q_raw)
    qs = q_raw * sq
    bc = _mask_sum(tri, _log(forget))
    bref = _chunk_row(bc, CHUNK // 2 - 1, rows)
    blast = _chunk_row(bc, CHUNK - 1, rows)
    return dict(tri=tri, sg=sg, sg_neg=sg_neg, forget=forget, k=k, sq=sq, qs=qs,
                e_a=_exp(bc - bref), e_b=_exp(bref - bc), e_q=_exp(bc), e_k=_exp(blast - bc),
                dec=_exp(blast))


HG_GROUP = 4


def _hg_lanes(bsz):
    return [(hh, slice(hh * HEAD_DIM, (hh + 1) * HEAD_DIM), b) for hh in range(HG_GROUP) for b in range(bsz)]


def _hg_read(ref, lanes):
    return _Lanes(ref[b, :, cs].astype(F32) for _, cs, b in lanes)


def _hg_write(ref, lanes, val, offset=0):
    for (_, cs, b), a in zip(lanes, val.xs):
        ref[b, :, offset + cs.start:offset + cs.stop] = a


def _hgrn_fwd(proj, hf, lb, nw, *, bsz, seq, carry=None):
    rows = min(ROWS_HG, seq)
    nt = seq // rows
    nc = rows // CHUNK
    t = bsz * seq

    lanes = _hg_lanes(bsz)

    def body(q_ref, f_ref, v_ref, g_ref, lb_ref, nw_ref, y_ref, o_ref, st_ref, s_scr):
        @pl.when(pl.program_id(1) == 0)
        def _():
            s_scr[...] = jnp.zeros_like(s_scr)

        mask = _chunk_mask(rows)
        lb_v = _Lanes(lb_ref[:, cs] for _, cs, _ in lanes)
        gt = _hg_gates(_hg_read(q_ref, lanes), _hg_read(f_ref, lanes), lb_v, rows, mask)
        v_b = _hg_read(v_ref, lanes).astype(BF16)
        a_b = (gt["qs"] * gt["e_a"]).astype(BF16)
        b_b = (gt["k"] * gt["e_b"]).astype(BF16)
        qi_b = (gt["qs"] * gt["e_q"]).astype(BF16)
        ko_b = (gt["k"] * gt["e_k"]).astype(BF16)
        scores = _where(mask, _ldot_nt(a_b, b_b), 0.0)
        o_intra = _ldot(scores.astype(BF16), v_b)

        s = _Lanes(s_scr[i] for i in range(len(lanes)))
        parts = []
        for n in range(nc):
            sl = slice(n * CHUNK, (n + 1) * CHUNK)
            s_b = s.astype(BF16)
            for (hh, _, b), a in zip(lanes, s_b.xs):
                st_ref[hh, b, n] = a
            parts.append(_ldot_nt(qi_b[sl], s_b))
            s = s * gt["dec"][n * CHUNK:n * CHUNK + 1] + _ldot_tn(v_b[sl], ko_b[sl])
        for i, a in enumerate(s.xs):
            s_scr[i] = a
        o = o_intra + _concat(parts, 0)
        _hg_write(o_ref, lanes, o)
        r = _rsqrt(_mean(o * o, axis=-1, keepdims=True) + RMS_EPS)
        g = _hg_read(g_ref, lanes)
        _hg_write(y_ref, lanes, (o * r * nw_ref[...] * (g * _lsigmoid(g))).astype(BF16))

    wide = HG_GROUP * HEAD_DIM

    def col(base):
        return pl.BlockSpec((bsz, rows, wide), lambda h, j: (0, j, base // wide + h))

    out_tile = pl.BlockSpec((bsz, rows, wide), lambda h, j: (0, j, h))
    p3 = proj.reshape(bsz, seq, IN_COLS)
    outs, recv = _call(
        body, "hgrn_fwd", (HEADS // HG_GROUP, nt),
        [col(C_HQ), out_tile, col(C_HI), col(C_HG),
         pl.BlockSpec((1, wide), lambda h, j: (0, h)), _resident((1, HEAD_DIM))],
        [out_tile, out_tile,
         pl.BlockSpec((HG_GROUP, bsz, nc, HEAD_DIM, HEAD_DIM), lambda h, j: (h, 0, j, 0, 0))],
        [jax.ShapeDtypeStruct((bsz, seq, WIDTH), BF16), jax.ShapeDtypeStruct((bsz, seq, WIDTH), F32),
         jax.ShapeDtypeStruct((HEADS, bsz, seq // CHUNK, HEAD_DIM, HEAD_DIM), BF16)],
        (p3, hf.reshape(bsz, seq, WIDTH), p3, p3, lb, nw),
        scratch=[pltpu.VMEM((len(lanes), HEAD_DIM, HEAD_DIM), F32)],
        sem=("parallel", "arbitrary"), carry=carry)
    outs = [outs[0].reshape(t, WIDTH), outs[1].reshape(t, WIDTH), outs[2]]
    return outs if carry is None else (outs, recv)


def _hgrn_bwd(proj, hf, lb, nw, o_pre, states, dy, *, bsz, seq, carry=None):
    rows = min(ROWS_HG, seq)
    nt = seq // rows
    nc = rows // CHUNK
    t = bsz * seq
    lanes = _hg_lanes(bsz)

    def body(q_ref, f_ref, v_ref, g_ref, lb_ref, nw_ref, o_ref, st_ref, dy_ref, dh_ref, dlb_ref, dnw_ref, ds_scr):
        h, j = pl.program_id(0), pl.program_id(1)

        @pl.when(j == 0)
        def _():
            ds_scr[...] = jnp.zeros_like(ds_scr)
            dlb_ref[...] = jnp.zeros_like(dlb_ref)

        @pl.when((h == 0) & (j == 0))
        def _():
            dnw_ref[...] = jnp.zeros_like(dnw_ref)

        mask = _chunk_mask(rows)
        q_raw = _hg_read(q_ref, lanes)
        lb_v = _Lanes(lb_ref[:, cs] for _, cs, _ in lanes)
        gt = _hg_gates(q_raw, _hg_read(f_ref, lanes), lb_v, rows, mask)
        v_b = _hg_read(v_ref, lanes).astype(BF16)
        a_f = gt["qs"] * gt["e_a"]
        b_f = gt["k"] * gt["e_b"]
        qi_f = gt["qs"] * gt["e_q"]
        ko_f = gt["k"] * gt["e_k"]
        a_b, b_b, qi_b, ko_b = a_f.astype(BF16), b_f.astype(BF16), qi_f.astype(BF16), ko_f.astype(BF16)

        o = _hg_read(o_ref, lanes)
        nw_v = nw_ref[...]
        g = _hg_read(g_ref, lanes)
        dyv = _hg_read(dy_ref, lanes)
        r = _rsqrt(_mean(o * o, axis=-1, keepdims=True) + RMS_EPS)
        sgg = _lsigmoid(g)
        d_g = dyv * (o * r * nw_v) * (sgg * (1.0 + g * (1.0 - sgg)))
        d_on = dyv * (g * sgg)
        dnw_ref[...] += _sum(d_on * o * r, axis=0, keepdims=True).total()
        tt = d_on * nw_v
        d_o = r * tt - o * (r * r * r) * _mean(tt * o, axis=-1, keepdims=True)
        do_b = d_o.astype(BF16)

        sc_b = _where(mask, _ldot_nt(a_b, b_b), 0.0).astype(BF16)
        dsc_b = _where(mask, _ldot_nt(do_b, v_b), 0.0).astype(BF16)
        d_v = _ldot_tn(sc_b, do_b)
        d_a = _ldot(dsc_b, b_b)
        d_bm = _ldot_tn(dsc_b, a_b)

        ds = _Lanes(ds_scr[i] for i in range(len(lanes)))
        dqi_parts, dko_parts, dvi_parts, ddec_parts = [None] * nc, [None] * nc, [None] * nc, [None] * nc
        for n in reversed(range(nc)):
            sl = slice(n * CHUNK, (n + 1) * CHUNK)
            dec_n = gt["dec"][n * CHUNK:n * CHUNK + 1]
            ds_b = ds.astype(BF16)
            s_n = _Lanes(st_ref[hh, b, n] for hh, _, b in lanes)
            dqi_parts[n] = _ldot(do_b[sl], s_n)
            dko_parts[n] = _ldot(v_b[sl], ds_b)
            dvi_parts[n] = _ldot_nt(ko_b[sl], ds_b)
            d_dec = _sum(ds * s_n.astype(F32), axis=0, keepdims=True)
            ddec_parts[n] = _bcast(d_dec * dec_n, (CHUNK, HEAD_DIM))
            ds = ds * dec_n + _ldot_tn(do_b[sl], qi_b[sl])
        for i, a in enumerate(ds.xs):
            ds_scr[i] = a
        d_qi = _concat(dqi_parts, 0)
        d_ko = _concat(dko_parts, 0)
        d_v = d_v + _concat(dvi_parts, 0)

        d_qs = d_a * gt["e_a"] + d_qi * gt["e_q"]
        d_k = d_bm * gt["e_b"] + d_ko * gt["e_k"]
        t_a, t_b, t_q, t_k = d_a * a_f, d_bm * b_f, d_qi * qi_f, d_ko * ko_f
        d_bref = _chunk_total(t_b - t_a, rows)
        d_blast = _chunk_total(t_k, rows) + _concat(ddec_parts, 0)
        pos = lax.broadcasted_iota(jnp.int32, (rows, HEAD_DIM), 0) % CHUNK
        d_bc = (t_a - t_b + t_q - t_k + _where(pos == CHUNK // 2 - 1, d_bref, 0.0)
                + _where(pos == CHUNK - 1, d_blast, 0.0))
        d_logf = _mask_sum(gt["tri"], d_bc, transpose=True)

        sg, sg_neg = gt["sg"], gt["sg_neg"]
        inv_f = 1.0 / gt["forget"]
        common = (1.0 - lb_v) * sg * sg_neg
        d_fl = common * (d_logf * inv_f - d_k)
        d_lb = _sum(sg_neg * (d_logf * inv_f - d_k), axis=0, keepdims=True)
        for (_, cs, _), a in zip(lanes, d_lb.xs):
            dlb_ref[:, cs] += a
        sq = gt["sq"]
        _hg_write(dh_ref, lanes, (d_qs * (sq * (1.0 + q_raw * (1.0 - sq)))).astype(BF16), 0)
        _hg_write(dh_ref, lanes, d_fl.astype(BF16), WIDTH)
        _hg_write(dh_ref, lanes, d_v.astype(BF16), 2 * WIDTH)
        _hg_write(dh_ref, lanes, d_g.astype(BF16), 3 * WIDTH)

    assert HG_GROUP == HEADS, "the combined gradient block needs all heads in one grid step"
    wide = HG_GROUP * HEAD_DIM

    def col(base):
        return pl.BlockSpec((bsz, rows, wide), lambda h, j: (0, nt - 1 - j, base // wide + h))

    tile = pl.BlockSpec((bsz, rows, wide), lambda h, j: (0, nt - 1 - j, h))
    head_vec = pl.BlockSpec((1, wide), lambda h, j: (0, h))
    p3 = proj.reshape(bsz, seq, IN_COLS)
    outs, recv = _call(
        body, "hgrn_bwd", (HEADS // HG_GROUP, nt),
        [col(C_HQ), tile, col(C_HI), col(C_HG), head_vec, _resident((1, HEAD_DIM)), tile,
         pl.BlockSpec((HG_GROUP, bsz, nc, HEAD_DIM, HEAD_DIM), lambda h, j: (h, 0, nt - 1 - j, 0, 0)), tile],
        [pl.BlockSpec((bsz, rows, 4 * WIDTH), lambda h, j: (0, nt - 1 - j, 0)), head_vec, _resident((1, HEAD_DIM))],
        [jax.ShapeDtypeStruct((bsz, seq, 4 * WIDTH), BF16), jax.ShapeDtypeStruct((1, WIDTH), F32),
         jax.ShapeDtypeStruct((1, HEAD_DIM), F32)],
        (p3, hf.reshape(bsz, seq, WIDTH), p3, p3, lb, nw, o_pre.reshape(bsz, seq, WIDTH), states,
         dy.reshape(bsz, seq, WIDTH)),
        scratch=[pltpu.VMEM((len(lanes), HEAD_DIM, HEAD_DIM), F32)],
        sem=("arbitrary", "arbitrary"), carry=carry)
    outs = [outs[0].reshape(t, 4 * WIDTH), outs[1], outs[2]]
    return outs if carry is None else (outs, recv)


def _mem_kv(mem2d, w_k, w_v):
    rows = mem2d.shape[0]

    def body(m_ref, wk_ref, wv_ref, k_ref, v_ref):
        m_b = m_ref[...].astype(BF16)
        k_ref[...] = _dot(m_b, wk_ref[...]).astype(BF16)
        v_ref[...] = _dot(m_b, wv_ref[...]).astype(BF16)

    return pl.pallas_call(
        body, name="mem_kv", grid=(rows // MEM_LEN,),
        in_specs=[pl.BlockSpec((MEM_LEN, D_MODEL), lambda i: (i, 0)), _resident((D_MODEL, WIDTH)),
                  _resident((D_MODEL, WIDTH))],
        out_specs=[pl.BlockSpec((MEM_LEN, WIDTH), lambda i: (i, 0))] * 2,
        out_shape=[jax.ShapeDtypeStruct((rows, WIDTH), BF16)] * 2,
        compiler_params=_params("parallel"),
    )(mem2d, w_k, w_v)


def _softmax_rows(s):
    m = _lift(jnp.max)(s, axis=-1, keepdims=True)
    e = _exp(s - m)
    return e / _sum(e, axis=-1, keepdims=True)


def _attn_fwd(proj, mk, mv, *, tm, seq):
    t = proj.shape[0]
    per_b = seq // tm
    scale = HEAD_DIM ** -0.5

    def body(q_ref, k_ref, v_ref, y_ref):
        heads = [slice(h * HEAD_DIM, (h + 1) * HEAD_DIM) for h in range(HEADS)]
        q_b = _Lanes(q_ref[:, sl] for sl in heads).astype(BF16)
        p = _softmax_rows(_ldot_nt(q_b, _Lanes(k_ref[:, sl] for sl in heads)) * scale)
        out = _ldot(p.astype(BF16), _Lanes(v_ref[:, sl] for sl in heads))
        y_ref[...] = jnp.concatenate(out.xs, axis=-1).astype(BF16)

    kv = pl.BlockSpec((MEM_LEN, WIDTH), lambda i: (i // per_b, 0))
    return pl.pallas_call(
        body, name="attn_fwd", grid=(t // tm,),
        in_specs=[pl.BlockSpec((tm, WIDTH), lambda i: (i, C_MQ // WIDTH)), kv, kv],
        out_specs=pl.BlockSpec((tm, WIDTH), lambda i: (i, 0)),
        out_shape=jax.ShapeDtypeStruct((t, WIDTH), BF16),
        compiler_params=_params("parallel"),
    )(proj, mk, mv)


def _attn_bwd(proj, mk, mv, dy, *, tm, seq):
    t = proj.shape[0]
    per_b = seq // tm
    scale = HEAD_DIM ** -0.5

    def body(q_ref, k_ref, v_ref, dy_ref, dq_ref, dk_ref, dv_ref):
        i = pl.program_id(0)

        @pl.when(i % per_b == 0)
        def _():
            dk_ref[...] = jnp.zeros_like(dk_ref)
            dv_ref[...] = jnp.zeros_like(dv_ref)

        heads = [slice(h * HEAD_DIM, (h + 1) * HEAD_DIM) for h in range(HEADS)]
        q_b = _Lanes(q_ref[:, sl] for sl in heads).astype(BF16)
        k_b, v_b = _Lanes(k_ref[:, sl] for sl in heads), _Lanes(v_ref[:, sl] for sl in heads)
        p = _softmax_rows(_ldot_nt(q_b, k_b) * scale)
        dy_b = _Lanes(dy_ref[:, sl] for sl in heads).astype(BF16)
        dp = _ldot_nt(dy_b, v_b)
        d_v = _ldot_tn(p.astype(BF16), dy_b)
        ds_b = (p * (dp - _sum(dp * p, axis=-1, keepdims=True)) * scale).astype(BF16)
        dq_ref[...] = jnp.concatenate(_ldot(ds_b, k_b).xs, axis=-1).astype(BF16)
        dk_ref[...] += jnp.concatenate(_ldot_tn(ds_b, q_b).xs, axis=-1)
        dv_ref[...] += jnp.concatenate(d_v.xs, axis=-1)

    kv = pl.BlockSpec((MEM_LEN, WIDTH), lambda i: (i // per_b, 0))
    tile = pl.BlockSpec((tm, WIDTH), lambda i: (i, 0))
    n_mem = mk.shape[0]
    return pl.pallas_call(
        body, name="attn_bwd", grid=(t // tm,),
        in_specs=[pl.BlockSpec((tm, WIDTH), lambda i: (i, C_MQ // WIDTH)), kv, kv, tile],
        out_specs=[tile, kv, kv],
        out_shape=[jax.ShapeDtypeStruct((t, WIDTH), BF16), jax.ShapeDtypeStruct((n_mem, WIDTH), F32),
                   jax.ShapeDtypeStruct((n_mem, WIDTH), F32)],
        compiler_params=_params("arbitrary"),
    )(proj, mk, mv, dy)


HALO = 16


def _shift_down(u, halo, k, row):
    out = pltpu.roll(u, k, 0)
    for m in range(k):
        out = jnp.where(row == m, halo[HALO - k + m:HALO - k + m + 1, :], out)
    return out


def _shift_up(u, halo, k, row, tm):
    out = pltpu.roll(u, tm - k, 0)
    for m in range(k):
        out = jnp.where(row == tm - k + m, halo[m:m + 1, :], out)
    return out


def _merge_fwd(proj, y_b, y_c, conv_w, w_branch, b_gate, *, tm, seq, carry=None):
    t = proj.shape[0]
    per_b = seq // tm
    hb = tm // HALO

    def body(cb_ref, cc_ref, ch_ref, cch_ref, chh_ref, ga_ref, gb_ref, gc_ref, yb_ref, yc_ref, cw_ref, wb_ref,
             bg_ref, ya_ref, pa_ref, pb_ref, pc_ref, mg_ref, sa_ref, sb_ref, sc_ref):
        i = pl.program_id(0)
        row = lax.broadcasted_iota(jnp.int32, (tm, WIDTH), 0)
        u = cc_ref[...].astype(F32) * ch_ref[...].astype(F32)
        halo = jnp.where(i % per_b == 0, 0.0, cch_ref[...].astype(F32) * chh_ref[...].astype(F32))
        cw = cw_ref[...]
        y = cw[0:1] * _shift_down(u, halo, 2, row) + cw[1:2] * _shift_down(u, halo, 1, row) + cw[2:3] * u
        ya_b = (cb_ref[...].astype(F32) * y).astype(BF16)
        ya_ref[...] = ya_b
        merged = None
        for idx, (y_in, g_ref, p_ref, s_ref) in enumerate(((ya_b, ga_ref, pa_ref, sa_ref),
                                                            (yb_ref[...], gb_ref, pb_ref, sb_ref),
                                                            (yc_ref[...], gc_ref, pc_ref, sc_ref))):
            p = _dot(y_in, wb_ref[idx])
            p_ref[...] = p.astype(BF16)
            sg = _sigmoid(g_ref[...].astype(F32) + bg_ref[:, idx * D_MODEL:(idx + 1) * D_MODEL])
            s_ref[...] = sg.astype(BF16)
            term = sg * p
            merged = term if merged is None else merged + term
        mg_ref[...] = merged.astype(BF16)

    def half(c):
        return pl.BlockSpec((tm, WIDTH), lambda i: (i, c // WIDTH))

    def prev(c):
        return pl.BlockSpec((HALO, WIDTH), lambda i: (jnp.maximum(i * hb - 1, 0), c // WIDTH))

    def gate(k):
        return pl.BlockSpec((tm, D_MODEL), lambda i: (i, C_GA // D_MODEL + k))

    tile512 = pl.BlockSpec((tm, WIDTH), lambda i: (i, 0))
    tile1k = pl.BlockSpec((tm, D_MODEL), lambda i: (i, 0))
    outs, recv = _call(
        body, "merge_fwd", (t // tm,),
        [half(C_CB), half(C_CC), half(C_CH), prev(C_CC), prev(C_CH), gate(0), gate(1), gate(2),
         tile512, tile512, _resident((CONV_K, WIDTH)), _resident((3, WIDTH, D_MODEL)), _resident((1, 3 * D_MODEL))],
        [tile512] + [tile1k] * 7,
        [jax.ShapeDtypeStruct((t, WIDTH), BF16)] + [jax.ShapeDtypeStruct((t, D_MODEL), BF16)] * 7,
        (proj, proj, proj, proj, proj, proj, proj, proj, y_b, y_c, conv_w, w_branch, b_gate),
        sem=("parallel",), carry=carry)
    return outs if carry is None else (outs, recv)


def _merge_bwd(dmerged, projections, gates, w_branch, *, tm):
    t = dmerged.shape[0]

    def body(dm_ref, pa_ref, pb_ref, pc_ref, sa_ref, sb_ref, sc_ref, wb_ref,
             dgt_ref, dpa_ref, dpb_ref, dpc_ref, dya_ref, dyb_ref, dyc_ref, dbg_ref):
        i = pl.program_id(0)

        @pl.when(i == 0)
        def _():
            dbg_ref[...] = jnp.zeros_like(dbg_ref)

        dm = dm_ref[...].astype(F32)
        for idx, (p_ref, s_ref, dp_ref, dy_ref) in enumerate(((pa_ref, sa_ref, dpa_ref, dya_ref),
                                                              (pb_ref, sb_ref, dpb_ref, dyb_ref),
                                                              (pc_ref, sc_ref, dpc_ref, dyc_ref))):
            cols = slice(idx * D_MODEL, (idx + 1) * D_MODEL)
            sg = s_ref[...].astype(F32)
            dp = dm * sg
            dp_b = dp.astype(BF16)
            dp_ref[...] = dp_b
            dgate = dp * p_ref[...].astype(F32) * (1.0 - sg)
            dgt_ref[:, cols] = dgate.astype(BF16)
            dbg_ref[:, cols] += jnp.sum(dgate, axis=0, keepdims=True)
            dy_ref[...] = _dot_nt(dp_b, wb_ref[idx]).astype(BF16)

    tile512 = pl.BlockSpec((tm, WIDTH), lambda i: (i, 0))
    tile1k = pl.BlockSpec((tm, D_MODEL), lambda i: (i, 0))
    return pl.pallas_call(
        body, name="merge_bwd", grid=(t // tm,),
        in_specs=[tile1k] * 7 + [_resident((3, WIDTH, D_MODEL))],
        out_specs=[pl.BlockSpec((tm, 3 * D_MODEL), lambda i: (i, 0)), tile1k, tile1k, tile1k,
                   tile512, tile512, tile512, _resident((1, 3 * D_MODEL))],
        out_shape=[jax.ShapeDtypeStruct((t, 3 * D_MODEL), BF16)] + [jax.ShapeDtypeStruct((t, D_MODEL), BF16)] * 3
                  + [jax.ShapeDtypeStruct((t, WIDTH), BF16)] * 3 + [jax.ShapeDtypeStruct((1, 3 * D_MODEL), F32)],
        compiler_params=_params("arbitrary"),
    )(dmerged, *projections, *gates, w_branch)


def _conv_bwd(proj, dya, conv_w, *, tm, seq):
    t = proj.shape[0]
    per_b = seq // tm
    hb = tm // HALO
    last_blk = t // HALO - 1

    def body(cb_ref, cc_ref, ch_ref, cch_ref, chh_ref, dya_ref, cbn_ref, dyan_ref, cw_ref, d_ref, dcw_ref):
        i = pl.program_id(0)

        @pl.when(i == 0)
        def _():
            dcw_ref[...] = jnp.zeros_like(dcw_ref)

        row = lax.broadcasted_iota(jnp.int32, (tm, WIDTH), 0)
        cb, cc, ch = cb_ref[...].astype(F32), cc_ref[...].astype(F32), ch_ref[...].astype(F32)
        u = cc * ch
        halo = jnp.where(i % per_b == 0, 0.0, cch_ref[...].astype(F32) * chh_ref[...].astype(F32))
        u1 = _shift_down(u, halo, 1, row)
        u2 = _shift_down(u, halo, 2, row)
        cw = cw_ref[...]
        y = cw[0:1] * u2 + cw[1:2] * u1 + cw[2:3] * u
        dya = dya_ref[...].astype(F32)
        dy = dya * cb
        nxt = jnp.where(i % per_b == per_b - 1, 0.0, dyan_ref[...].astype(F32) * cbn_ref[...].astype(F32))
        du = cw[2:3] * dy + cw[1:2] * _shift_up(dy, nxt, 1, row, tm) + cw[0:1] * _shift_up(dy, nxt, 2, row, tm)
        d_ref[:, 0:WIDTH] = (dya * y).astype(BF16)
        d_ref[:, WIDTH:2 * WIDTH] = (du * ch).astype(BF16)
        d_ref[:, 2 * WIDTH:3 * WIDTH] = (du * cc).astype(BF16)
        dcw_ref[0:1, :] += jnp.sum(dy * u2, axis=0, keepdims=True)
        dcw_ref[1:2, :] += jnp.sum(dy * u1, axis=0, keepdims=True)
        dcw_ref[2:3, :] += jnp.sum(dy * u, axis=0, keepdims=True)

    def half(c):
        return pl.BlockSpec((tm, WIDTH), lambda i: (i, c // WIDTH))

    def prev(c):
        return pl.BlockSpec((HALO, WIDTH), lambda i: (jnp.maximum(i * hb - 1, 0), c // WIDTH))

    def nxt(c):
        return pl.BlockSpec((HALO, WIDTH), lambda i: (jnp.minimum((i + 1) * hb, last_blk), c // WIDTH))

    return pl.pallas_call(
        body, name="conv_bwd", grid=(t // tm,),
        in_specs=[half(C_CB), half(C_CC), half(C_CH), prev(C_CC), prev(C_CH),
                  pl.BlockSpec((tm, WIDTH), lambda i: (i, 0)), nxt(C_CB), nxt(0), _resident((CONV_K, WIDTH))],
        out_specs=[pl.BlockSpec((tm, 3 * WIDTH), lambda i: (i, 0)), _resident((CONV_K, WIDTH))],
        out_shape=[jax.ShapeDtypeStruct((t, 3 * WIDTH), BF16), jax.ShapeDtypeStruct((CONV_K, WIDTH), F32)],
        compiler_params=_params("arbitrary"),
    )(proj, proj, proj, proj, proj, dya, proj, dya, conv_w)


def _loss_head(y, target, *, tm):
    t = y.shape[0]

    def body(y_ref, t_ref, dy_ref, l_ref):
        @pl.when(pl.program_id(0) == 0)
        def _():
            l_ref[...] = jnp.zeros_like(l_ref)

        err = y_ref[...] - t_ref[...]
        dy_ref[...] = err * (1.0 / D_MODEL)
        per_row = jnp.sum(err * err, axis=-1, keepdims=True) * (1.0 / D_MODEL)
        l_ref[...] += 0.5 * jnp.sum(per_row, axis=0, keepdims=True)

    row = pl.BlockSpec((tm, D_MODEL), lambda i: (i, 0))
    return pl.pallas_call(
        body, name="loss_head", grid=(t // tm,),
        in_specs=[row, row], out_specs=[row, _resident((8, 128))],
        out_shape=[jax.ShapeDtypeStruct((t, D_MODEL), F32), jax.ShapeDtypeStruct((8, 128), F32)],
        compiler_params=_params("arbitrary"),
    )(y, target)


def _lb_softmax(lower_bounds):
    x = lower_bounds
    e = jnp.exp(x - jnp.max(x, axis=0, keepdims=True))
    return e / jnp.sum(e, axis=0, keepdims=True)


def _lb_fwd(lower_bounds):
    def body(x_ref, o_ref):
        s = _lb_softmax(x_ref[...])
        c = s[0:1]
        o_ref[0:1, :] = c - s[0:1]
        for l in range(1, DEPTH):
            c = c + s[l:l + 1]
            o_ref[l:l + 1, :] = c - s[0:1]

    return pl.pallas_call(body, name="lb_fwd", out_shape=jax.ShapeDtypeStruct(lower_bounds.shape, F32))(lower_bounds)


def _lb_bwd(lower_bounds, d_lb_all):
    def body(x_ref, d_ref, o_ref):
        s = _lb_softmax(x_ref[...])
        d = d_ref[...]
        rows = [jnp.zeros_like(d[0:1])]
        for j in range(1, DEPTH):
            acc = d[j:j + 1]
            for l in range(j + 1, DEPTH):
                acc = acc + d[l:l + 1]
            rows.append(acc)
        inner = rows[0] * s[0:1]
        for j in range(1, DEPTH):
            inner = inner + rows[j] * s[j:j + 1]
        for j in range(DEPTH):
            o_ref[j:j + 1, :] = s[j:j + 1] * (rows[j] - inner)

    return pl.pallas_call(body, name="lb_bwd", out_shape=jax.ShapeDtypeStruct(lower_bounds.shape, F32))(
        lower_bounds, d_lb_all)


def _adamw(w, g, m, v):
    m2 = ADAM_B1 * m + (1.0 - ADAM_B1) * g
    v2 = ADAM_B2 * v + (1.0 - ADAM_B2) * (g * g)
    m_hat = m2 / (1.0 - ADAM_B1 ** ADAM_STEP)
    v_hat = v2 / (1.0 - ADAM_B2 ** ADAM_STEP)
    delta = -ADAM_LR * (m_hat / (jnp.sqrt(v_hat) + ADAM_EPS) + ADAM_WD * w)
    return delta, m2, v2


def _adam_small(name, g, w, m, v):
    shape = w.shape
    flat = (-1, shape[-1])
    g2, w2, m2, v2 = (a.reshape(flat) for a in (g, w, m, v))

    def body(g_ref, w_ref, m_ref, v_ref, d_ref, mo_ref, vo_ref):
        d, mm, vv = _adamw(w_ref[...], g_ref[...], m_ref[...], v_ref[...])
        d_ref[...] = d
        mo_ref[...] = mm
        vo_ref[...] = vv

    outs = pl.pallas_call(body, name=name, out_shape=[jax.ShapeDtypeStruct(w2.shape, F32)] * 3)(g2, w2, m2, v2)
    return [o.reshape(shape) for o in outs]


def _adam_shard(name, recvs, w, m, v, *, tr):
    _, r, c = w.shape

    def body(*refs):
        rc, (w_ref, m_ref, v_ref), (g_ref, d_ref, mo_ref, vo_ref) = refs[:DEPTH], refs[DEPTH:DEPTH + 3], refs[DEPTH + 3:]
        layer = pl.program_id(0)
        for cand in range(DEPTH):
            @pl.when(layer == cand)
            def _():
                g = rc[cand][0].astype(F32)
                for d in range(1, N_DEV):
                    g = g + rc[cand][d].astype(F32)
                dl, mm, vv = _adamw(w_ref[...], g, m_ref[...], v_ref[...])
                g_ref[...] = g
                d_ref[...] = dl
                mo_ref[...] = mm
                vo_ref[...] = vv

    def recv_spec(cand):
        return pl.BlockSpec((N_DEV, tr, c), lambda l, i: (0, jnp.where(l == cand, i, 0), 0))

    tile = pl.BlockSpec((None, tr, c), lambda l, i: (l, i, 0))
    return pl.pallas_call(
        body, name=name, grid=(DEPTH, r // tr),
        in_specs=[recv_spec(cand) for cand in range(DEPTH)] + [tile] * 3,
        out_specs=[tile] * 4,
        out_shape=[jax.ShapeDtypeStruct(w.shape, F32)] * 4,
        compiler_params=_params("parallel", "parallel"),
    )(*recvs, w, m, v)


def _sum_devices(name, x):
    def body(x_ref, o_ref):
        acc = x_ref[0]
        for d in range(1, N_DEV):
            acc = acc + x_ref[d]
        o_ref[...] = acc

    return pl.pallas_call(body, name=name, out_shape=jax.ShapeDtypeStruct(x.shape[1:], x.dtype))(x)


SMALL = (("lower_bounds", 1, 512), ("conv_w", CONV_K, WIDTH), ("hg_norm_w", 1, HEAD_DIM), ("b_gate", 3, D_MODEL),
         ("ln1_g", 1, D_MODEL), ("ln1_b", 1, D_MODEL), ("ln2_g", 1, D_MODEL), ("ln2_b", 1, D_MODEL))
SMALL_ROWS = sum(r for _, r, _ in SMALL)


def _pack_small(per_layer):
    flat = [a for layer in per_layer for a in layer]

    def body(*refs):
        ins, o_ref = refs[:-1], refs[-1]
        o_ref[...] = jnp.zeros_like(o_ref)
        it = iter(ins)
        for l in range(DEPTH):
            row = l * SMALL_ROWS
            for name, nrows, ncols in SMALL:
                ref = next(it)
                if name == "b_gate":
                    for k in range(nrows):
                        o_ref[row + k:row + k + 1, :] = ref[:, k * ncols:(k + 1) * ncols]
                else:
                    o_ref[row:row + nrows, 0:ncols] = ref[...]
                row += nrows

    return pl.pallas_call(body, name="pack_small_grads",
                          out_shape=jax.ShapeDtypeStruct((DEPTH * SMALL_ROWS, D_MODEL), F32))(*flat)


def _unpack_small(summed):
    s3 = summed.reshape(DEPTH, SMALL_ROWS, D_MODEL)
    out, row = {}, 0
    for name, nrows, ncols in SMALL:
        out[name] = s3[:, row:row + nrows, :ncols].reshape(DEPTH, nrows * ncols)
        row += nrows
    return out


def _natural_cols(g):
    nd = g.ndim
    perm = tuple(range(1, nd - 1)) + (0, nd - 1)
    t = jnp.transpose(g, perm)
    return t.reshape(t.shape[:-2] + (t.shape[-2] * t.shape[-1],))


def _natural_rows(g):
    return g.reshape(g.shape[0] * g.shape[1], g.shape[2])


def _hosted(hosts, key, fn):
    if not hosts or key not in hosts:
        return fn(None)
    ex, hook = hosts[key]
    outs, recv = fn(ex)
    hook(recv)
    return outs


def _layer_fwd(cur, cur_b, mem2d, wl, *, bsz, seq, hosts=None):
    tm = min(512, seq)
    proj, hf = _hosted(hosts, "in_proj", lambda c: _in_proj(cur_b, wl["w_in"], tm=min(1024, seq), carry=c))
    y_b, o_pre, states = _hosted(hosts, "hgrn_fwd", lambda c: _hgrn_fwd(proj, hf, wl["lb"], wl["nw"], bsz=bsz,
                                                                         seq=seq, carry=c))
    mk, mv = _mem_kv(mem2d, wl["w_mk"], wl["w_mv"])
    y_c = _attn_fwd(proj, mk, mv, tm=tm, seq=seq)
    y_a, pa, pb, pc, merged, sga, sgb, sgc = _hosted(
        hosts, "merge_fwd", lambda c: _merge_fwd(proj, y_b, y_c, wl["conv"], wl["w_br"], wl["b_gate"], tm=tm,
                                                 seq=seq, carry=c))
    z1, x1, x1_b = _hosted(hosts, "wo_ln", lambda c: _linear_ln("wo_ln", merged, wl["w_o"], cur, wl["ln1_g"],
                                                                  wl["ln1_b"], tm=tm, carry=c))
    hid = _hosted(hosts, "mlp_up", lambda c: _mm_nn("mlp_up", x1_b, wl["w_up"], tm=tm, tn=1024, out_dtype=BF16,
                                                     relu2=True, carry=c))
    z2, x2, x2_b = _linear_ln("down_ln", hid, wl["w_down"], x1, wl["ln2_g"], wl["ln2_b"], tm=tm)
    return dict(x_b=cur_b, proj=proj, hf=hf, y_a=y_a, y_b=y_b, y_c=y_c, o_pre=o_pre, states=states, mk=mk, mv=mv,
                proj3=(pa, pb, pc), gates3=(sga, sgb, sgc), merged=merged, z1=z1, x1_b=x1_b, hid=hid, z2=z2, x2=x2,
                x2_b=x2_b)


def _layer_bwd(dcur, mem2d, s, wl, *, bsz, seq, plan=None):
    tm = min(512, seq)
    tk = min(2048, bsz * seq)
    g = {}

    def run(key, fn):
        made = plan[key](g) if plan and key in plan else None
        return _hosted({key: made} if made else None, key, fn)

    dz2, dz2_b, dhpre, d_ln2g, d_ln2b = run(
        "ln2_bwd_down", lambda c: _ln_bwd_mm_nt("ln2_bwd_down", dcur, s["z2"], wl["ln2_g"], wl["w_down"],
                                                s["hid"], tm=tm, tn=1024, carry=c))
    g["w_down"] = _mm_tn("grad_w_down", s["hid"], dz2_b, tk=tk, tmo=1024, tno=1024)
    dx1 = _mm_nt_sum("mlp_up_bwd", [dhpre], [0], wl["w_up"], dz2, tm=tm)
    g["w_up"] = run("grad_w_up", lambda c: _mm_tn("grad_w_up", s["x1_b"], dhpre, tk=tk, tmo=1024, tno=1024, carry=c))
    dz1, dz1_b, dmerged, d_ln1g, d_ln1b = _ln_bwd_mm_nt("ln1_bwd_wo", dx1, s["z1"], wl["ln1_g"], wl["w_o"],
                                                        tm=tm, tn=1024)
    g["w_o"] = _mm_tn("grad_w_o", s["merged"], dz1_b, tk=tk, tmo=1024, tno=1024)
    dgate, dpa, dpb, dpc, dya, dyb, dyc, d_bg = _merge_bwd(dmerged, s["proj3"], s["gates3"], wl["w_br"], tm=tm)
    g["w_br"] = jnp.stack([_mm_tn("grad_w_branch", yy, dp, tk=tk, tmo=512, tno=1024)
                           for yy, dp in ((s["y_a"], dpa), (s["y_b"], dpb), (s["y_c"], dpc))])
    d_conv, d_cw = _conv_bwd(s["proj"], dya, wl["conv"], tm=tm, seq=seq)
    dhg, d_lb, d_nw = run(
        "hgrn_bwd", lambda c: _hgrn_bwd(s["proj"], s["hf"], wl["lb"], wl["nw"], s["o_pre"], s["states"], dyb,
                                        bsz=bsz, seq=seq, carry=c))
    dmq, dmk, dmv = _attn_bwd(s["proj"], s["mk"], s["mv"], dyc, tm=tm, seq=seq)
    tkm = min(512, mem2d.shape[0])
    g["w_mk"] = _mm_tn("grad_w_mem", mem2d, dmk, tk=tkm, tmo=1024, tno=512)
    g["w_mv"] = _mm_tn("grad_w_mem", mem2d, dmv, tk=tkm, tmo=1024, tno=512)
    pieces = [d_conv, dhg, dmq, dgate]
    offsets = [C_CB, C_HQ, C_MQ, C_GA]
    g["w_in"] = jnp.concatenate(
        [_mm_tn("grad_w_in_%d" % p.shape[1], s["x_b"], p, tk=tk, tmo=1024,
                tno=next(w for w in (1024, 768, 512) if p.shape[1] % w == 0)) for p in pieces], axis=1)
    dx = run("in_proj_bwd", lambda c: _mm_nt_sum("in_proj_bwd", pieces, offsets, wl["w_in"], dz1,
                                                 tm=min(256, seq), carry=c))
    return dx, g, [d_lb, d_cw, d_nw, d_bg, d_ln1g, d_ln1b, d_ln2g, d_ln2b]


def kernel(x, mem, lower_bounds, w_in, conv_w, hg_norm_w, w_mem_k, w_mem_v, w_branch, b_gate, w_o, ln1_g, ln1_b, w_up, w_down, ln2_g, ln2_b, loss_target, m_lower_bounds, m_w_in, m_conv_w, m_hg_norm_w, m_w_mem_k, m_w_mem_v, m_w_branch, m_b_gate, m_w_o, m_ln1_g, m_ln1_b, m_w_up, m_w_down, m_ln2_g, m_ln2_b, v_lower_bounds, v_w_in, v_conv_w, v_hg_norm_w, v_w_mem_k, v_w_mem_v, v_w_branch, v_b_gate, v_w_o, v_ln1_g, v_ln1_b, v_w_up, v_w_down, v_ln2_g, v_ln2_b):
    bsz, seq, _ = x.shape
    t = bsz * seq
    me = _my_id()

    sh = dict(w_in=w_in.astype(BF16), w_mk=w_mem_k.astype(BF16), w_mv=w_mem_v.astype(BF16),
              w_br=w_branch.astype(BF16), w_o=w_o.astype(BF16), w_up=w_up.astype(BF16), w_down=w_down.astype(BF16))
    half_rows = D_MODEL // 2
    sh["w_in_a"], sh["w_in_b"] = sh["w_in"][:, :half_rows], sh["w_in"][:, half_rows:]
    natural = dict(w_in=_natural_cols, w_in_a=_natural_cols, w_in_b=_natural_cols, w_mk=_natural_rows,
                   w_mv=_natural_rows, w_br=_natural_cols, w_o=_natural_rows, w_up=_natural_cols,
                   w_down=_natural_rows)

    def gather_of(names, l):
        srcs = [sh[n][l] for n in names]
        return _Exchange(srcs, [_whole] * len(srcs), [s_.shape for s_ in srcs])

    def put(names, into):
        def hook(recv_):
            for n, r in zip(names, recv_):
                into[n] = natural[n](r)
        return hook

    lb_all = _lb_fwd(lower_bounds)
    layer_w = [dict(lb=lb_all[l][None], nw=hg_norm_w[l][None], b_gate=b_gate[l][None], ln1_g=ln1_g[l][None],
                    ln1_b=ln1_b[l][None], ln2_g=ln2_g[l][None], ln2_b=ln2_b[l][None]) for l in range(DEPTH)]
    first = ["w_in"]
    conv_shard = conv_w.reshape(DEPTH * CONV_K * (WIDTH // N_DEV) // 128, 128)
    ex0 = gather_of(first, 0)
    got = _exchange("gather_first", ex0.srcs + [conv_shard], [_whole] * (len(first) + 1),
                    [s_.shape for s_ in ex0.srcs] + [conv_shard.shape])
    put(first, layer_w[0])(got[:len(first)])
    conv_full = _natural_cols(got[-1].reshape(N_DEV, DEPTH, CONV_K, WIDTH // N_DEV))

    x2d = x.reshape(t, D_MODEL)
    mem2d = mem.reshape(bsz * MEM_LEN, D_MODEL)
    target2d = loss_target.reshape(t, D_MODEL)

    saved = []
    cur, cur_b = x2d, x2d.astype(BF16)
    for l in range(DEPTH):
        wl = layer_w[l]
        wl["conv"] = conv_full[l]
        if "w_in" not in wl:
            wl["w_in"] = jnp.concatenate([wl.pop("w_in_a"), wl.pop("w_in_b")], axis=0)
        now = ["w_up"] if l else ["w_up", "w_mk", "w_mv", "w_br", "w_o"]
        hosts = {"in_proj": (gather_of(now, l), put(now, wl)),
                 "hgrn_fwd": (gather_of(["w_down"], l), put(["w_down"], wl))}
        if l + 1 < DEPTH:
            nxt = layer_w[l + 1]
            hosts["merge_fwd"] = (gather_of(["w_in_a"], l + 1), put(["w_in_a"], nxt))
            hosts["wo_ln"] = (gather_of(["w_mk", "w_mv", "w_o"], l + 1), put(["w_mk", "w_mv", "w_o"], nxt))
            hosts["mlp_up"] = (gather_of(["w_in_b", "w_br"], l + 1), put(["w_in_b", "w_br"], nxt))
        s = _layer_fwd(cur, cur_b, mem2d, wl, bsz=bsz, seq=seq, hosts=hosts)
        saved.append(s)
        cur, cur_b = s["x2"], s["x2_b"]

    dcur, loss_tile = _loss_head(cur, target2d, tm=min(512, seq))
    loss = lax.psum(loss_tile[0, 0], ("x", "y", "c"))

    in_w = IN_COLS // N_DEV

    def in_half(r):
        return lambda ref, j: ref.at[pl.ds(r * half_rows, half_rows), pl.ds(j * in_w, in_w)]

    slicer = dict(w_in_a=in_half(0), w_in_b=in_half(1), w_mk=_rows(D_MODEL // N_DEV), w_mv=_rows(D_MODEL // N_DEV),
                  w_br=_cols(D_MODEL // N_DEV), w_o=_rows(D_MODEL // N_DEV), w_up=_cols(D_FF // N_DEV),
                  w_down=_rows(D_FF // N_DEV))
    source = dict(w_in_a="w_in", w_in_b="w_in")
    recv = [dict() for _ in range(DEPTH)]

    def scatter_of(names, g, into):
        ex = _Exchange([g[source.get(n, n)] for n in names], [slicer[n] for n in names],
                       [sh[n].shape[1:] for n in names])
        return ex, lambda recv_: into.update(zip(names, recv_))

    small_rows = [None] * DEPTH
    prev = None
    rest = ["w_in_b", "w_mk", "w_mv"]
    for l in reversed(range(DEPTH)):
        plan = {"grad_w_up": lambda g, l=l: scatter_of(["w_down"], g, recv[l]),
                "hgrn_bwd": lambda g, l=l: scatter_of(["w_up", "w_o", "w_br"], g, recv[l])}
        if l == 0:
            plan["in_proj_bwd"] = lambda g: scatter_of(["w_in_a"] + rest, g, recv[0])
        else:
            plan["in_proj_bwd"] = lambda g, l=l: scatter_of(["w_in_a"], g, recv[l])
        if prev is not None:
            plan["ln2_bwd_down"] = lambda g, l=l, prev=prev: scatter_of(rest, prev, recv[l + 1])
        dcur, prev, small_rows[l] = _layer_bwd(dcur, mem2d, saved[l], layer_w[l], bsz=bsz, seq=seq, plan=plan)
    for r in recv:
        r["w_in"] = jnp.concatenate([r.pop("w_in_a"), r.pop("w_in_b")], axis=1)

    packed = _pack_small(small_rows)
    all_small = _exchange("gather_small_grads", [packed], [_whole], [packed.shape])[0]
    small_grads = _unpack_small(_sum_devices("sum_small_grads", all_small))
    small_grads["lower_bounds"] = _lb_bwd(lower_bounds, small_grads["lower_bounds"])
    conv_all = small_grads["conv_w"].reshape(DEPTH, CONV_K, WIDTH)
    small_grads["conv_w"] = lax.dynamic_slice_in_dim(conv_all, me * (WIDTH // N_DEV), WIDTH // N_DEV, axis=2)

    grads, deltas, new_m, new_v = {}, {}, {}, {}
    given = dict(lower_bounds=(lower_bounds, m_lower_bounds, v_lower_bounds), conv_w=(conv_w, m_conv_w, v_conv_w),
                 hg_norm_w=(hg_norm_w, m_hg_norm_w, v_hg_norm_w), b_gate=(b_gate, m_b_gate, v_b_gate),
                 ln1_g=(ln1_g, m_ln1_g, v_ln1_g), ln1_b=(ln1_b, m_ln1_b, v_ln1_b),
                 ln2_g=(ln2_g, m_ln2_g, v_ln2_g), ln2_b=(ln2_b, m_ln2_b, v_ln2_b))
    for name, (w_, m_, v_) in given.items():
        g_ = small_grads[name].reshape(w_.shape)
        grads[name] = g_
        deltas[name], new_m[name], new_v[name] = _adam_small("adam_" + name, g_, w_, m_, v_)

    big = dict(w_in=("w_in", w_in, m_w_in, v_w_in, 128), w_mem_k=("w_mk", w_mem_k, m_w_mem_k, v_w_mem_k, 128),
               w_mem_v=("w_mv", w_mem_v, m_w_mem_v, v_w_mem_v, 128),
               w_branch=("w_br", w_branch, m_w_branch, v_w_branch, 512), w_o=("w_o", w_o, m_w_o, v_w_o, 128),
               w_up=("w_up", w_up, m_w_up, v_w_up, 256), w_down=("w_down", w_down, m_w_down, v_w_down, 128))
    for name, (k, w_, m_, v_, tr) in big.items():
        shape = w_.shape
        flat = (DEPTH, -1, shape[-1])
        rc = [recv[l][k].reshape((N_DEV,) + w_.reshape(flat).shape[1:]) for l in range(DEPTH)]
        outs = _adam_shard("adam_" + name, rc, w_.reshape(flat), m_.reshape(flat), v_.reshape(flat), tr=tr)
        grads[name], deltas[name], new_m[name], new_v[name] = (o.reshape(shape) for o in outs)

    order = ["lower_bounds", "w_in", "conv_w", "hg_norm_w", "w_mem_k", "w_mem_v", "w_branch", "b_gate", "w_o",
             "ln1_g", "ln1_b", "w_up", "w_down", "ln2_g", "ln2_b"]
    return (loss, dcur.reshape(x.shape), *[grads[n] for n in order], *[deltas[n] for n in order],
            *[new_m[n] for n in order], *[new_v[n] for n in order])
```

```python
import functools

import jax
import jax.numpy as jnp
from jax import lax
from jax.experimental import pallas as pl
from jax.experimental.pallas import tpu as pltpu

F32 = jnp.float32
BF16 = jnp.bfloat16

N_DEV = 8
D_MODEL = 1024
DEPTH = 4
MEM_LEN = 256
CONV_K = 3
WIDTH = 512
HEADS = 4
HEAD_DIM = 128
CHUNK = 32
D_FF = 4 * D_MODEL
IN_COLS = 7168
ALPHA = (2.0 * DEPTH) ** 0.25
LN_EPS = 1e-5
RMS_EPS = 1e-6
ADAM_LR = 0.001
ADAM_B1 = 0.9
ADAM_B2 = 0.999
ADAM_EPS = 1e-08
ADAM_WD = 0.01
ADAM_STEP = 10

C_CB, C_CC, C_CH, C_HQ, C_HF, C_HI, C_HG, C_MQ, C_GA = 0, 512, 1024, 1536, 2048, 2560, 3072, 3584, 4096

ROWS_HG = 256
NT_DIMS = (((1,), (1,)), ((), ()))
TN_DIMS = (((0,), (0,)), ((), ()))
MESH = pl.DeviceIdType.MESH


def _dot(a, b):
    return jnp.dot(a, b, preferred_element_type=F32)


def _dot_nt(a, b):
    return lax.dot_general(a, b, NT_DIMS, preferred_element_type=F32)


def _dot_tn(a, b):
    return lax.dot_general(a, b, TN_DIMS, preferred_element_type=F32)


def _sigmoid(x):
    return 1.0 / (1.0 + jnp.exp(-x))


def _params(*sem):
    return pltpu.CompilerParams(dimension_semantics=sem)


def _resident(shape):
    nd = len(shape)
    return pl.BlockSpec(shape, lambda *_: (0,) * nd)


def _my_id():
    return 4 * lax.axis_index("x") + 2 * lax.axis_index("y") + lax.axis_index("c")


class _Exchange:
    def __init__(self, srcs, slicers, piece_shapes):
        self.srcs, self.slicers, self.n = list(srcs), list(slicers), len(srcs)
        any_spec = pl.BlockSpec(memory_space=pl.ANY)
        self.in_specs = [any_spec] * self.n
        self.out_specs = [any_spec] * self.n
        self.out_shape = [jax.ShapeDtypeStruct((N_DEV,) + tuple(s), a.dtype) for s, a in zip(piece_shapes, srcs)]
        self.scratch = [pltpu.SemaphoreType.DMA((self.n * N_DEV,)), pltpu.SemaphoreType.DMA((self.n * N_DEV,)),
                        pltpu.SemaphoreType.DMA((self.n,))]

    def _remote(self, ins, outs, sems, k, j, me):
        return pltpu.make_async_remote_copy(
            src_ref=self.slicers[k](ins[k], j), dst_ref=outs[k].at[me],
            send_sem=sems[0].at[k * N_DEV + j], recv_sem=sems[1].at[k * N_DEV + me],
            device_id=(j // 4, (j // 2) % 2, j % 2), device_id_type=MESH)

    def _local(self, ins, outs, sems, k, j, me):
        return pltpu.make_async_copy(self.slicers[k](ins[k], j), outs[k].at[me], sems[2].at[k])

    def start(self, ins, outs, sems):
        me = _my_id()
        for k in range(self.n):
            for j in range(N_DEV):
                @pl.when(j != me)
                def _():
                    self._remote(ins, outs, sems, k, j, me).start()

                @pl.when(j == me)
                def _():
                    self._local(ins, outs, sems, k, j, me).start()

    def wait(self, ins, outs, sems):
        me = _my_id()
        for k in range(self.n):
            for j in range(N_DEV):
                @pl.when(j != me)
                def _():
                    pltpu.make_async_remote_copy(
                        src_ref=self.slicers[k](ins[k], j), dst_ref=outs[k].at[j],
                        send_sem=sems[0].at[k * N_DEV + j], recv_sem=sems[1].at[k * N_DEV + j],
                        device_id=(j // 4, (j // 2) % 2, j % 2), device_id_type=MESH).wait_recv()
                    self._remote(ins, outs, sems, k, j, me).wait_send()

                @pl.when(j == me)
                def _():
                    self._local(ins, outs, sems, k, j, me).wait()


def _exchange(name, srcs, slicers, piece_shapes):
    ex = _Exchange(srcs, slicers, piece_shapes)

    def body(*refs):
        ins, outs, sems = refs[:ex.n], refs[ex.n:2 * ex.n], refs[2 * ex.n:]
        ex.start(ins, outs, sems)
        ex.wait(ins, outs, sems)

    return pl.pallas_call(
        body, name=name, in_specs=ex.in_specs, out_specs=ex.out_specs, out_shape=ex.out_shape,
        scratch_shapes=ex.scratch, compiler_params=pltpu.CompilerParams(has_side_effects=True),
    )(*ex.srcs)


def _call(body, name, grid, in_specs, out_specs, out_shape, args, scratch=(), sem=None, carry=None):
    n_in, n_out, n_scr = len(in_specs), len(out_specs), len(scratch)
    if carry is None:
        outs = pl.pallas_call(body, name=name, grid=grid, in_specs=in_specs, out_specs=out_specs,
                              out_shape=out_shape, scratch_shapes=list(scratch),
                              compiler_params=_params(*sem))(*args)
        return outs, None
    nc = carry.n

    def hosted(*refs):
        ins, c_in = refs[:n_in], refs[n_in:n_in + nc]
        outs = refs[n_in + nc:n_in + nc + n_out]
        c_out = refs[n_in + nc + n_out:n_in + 2 * nc + n_out]
        rest = refs[n_in + 2 * nc + n_out:]
        scr, sems = rest[:n_scr], rest[n_scr:]
        first, last = True, True
        for d, size in enumerate(grid):
            first = first & (pl.program_id(d) == 0)
            last = last & (pl.program_id(d) == size - 1)

        @pl.when(first)
        def _():
            carry.start(c_in, c_out, sems)

        body(*ins, *outs, *scr)

        @pl.when(last)
        def _():
            carry.wait(c_in, c_out, sems)

    outs = pl.pallas_call(
        hosted, name=name + "_x", grid=grid, in_specs=list(in_specs) + carry.in_specs,
        out_specs=list(out_specs) + carry.out_specs, out_shape=list(out_shape) + carry.out_shape,
        scratch_shapes=list(scratch) + carry.scratch,
        compiler_params=_params(*(["arbitrary"] * len(grid))))(*args, *carry.srcs)
    return outs[:n_out], outs[n_out:]


def _whole(ref, j):
    return ref


def _cols(width):
    return lambda ref, j: ref.at[(slice(None),) * (len(ref.shape) - 1) + (pl.ds(j * width, width),)]


def _rows(height):
    return lambda ref, j: ref.at[pl.ds(j * height, height)]


def _mm_nn(name, a, w, *, tm, tn, out_dtype, relu2=False, carry=None):
    t, k = a.shape
    n = w.shape[1]

    def body(a_ref, w_ref, o_ref):
        acc = _dot(a_ref[...].astype(BF16), w_ref[...])
        if relu2:
            r = jnp.maximum(acc, 0.0)
            acc = r * r
        o_ref[...] = acc.astype(out_dtype)

    outs, recv = _call(
        body, name, (t // tm, n // tn),
        [pl.BlockSpec((tm, k), lambda i, j: (i, 0)), pl.BlockSpec((k, tn), lambda i, j: (0, j))],
        [pl.BlockSpec((tm, tn), lambda i, j: (i, j))], [jax.ShapeDtypeStruct((t, n), out_dtype)], (a, w),
        sem=("parallel", "parallel"), carry=carry)
    return outs[0] if carry is None else (outs[0], recv)


def _in_proj(a, w, *, tm, carry=None):
    t, k = a.shape
    tn = IN_COLS // 4
    f_tile, f_off = C_HF // tn, C_HF % tn

    def body(a_ref, w_ref, o_ref, f_ref):
        acc = _dot(a_ref[...], w_ref[...])
        o_ref[...] = acc.astype(BF16)

        @pl.when(pl.program_id(1) == f_tile)
        def _():
            f_ref[...] = acc[:, f_off:f_off + WIDTH]

    outs, recv = _call(
        body, "in_proj", (t // tm, IN_COLS // tn),
        [pl.BlockSpec((tm, k), lambda i, j: (i, 0)), pl.BlockSpec((k, tn), lambda i, j: (0, j))],
        [pl.BlockSpec((tm, tn), lambda i, j: (i, j)), pl.BlockSpec((tm, WIDTH), lambda i, j: (i, 0))],
        [jax.ShapeDtypeStruct((t, IN_COLS), BF16), jax.ShapeDtypeStruct((t, WIDTH), F32)], (a, w),
        sem=("parallel", "arbitrary"), carry=carry)
    return outs if carry is None else (outs, recv)


def _linear_ln(name, a, w, resid, g, b, *, tm, carry=None):
    t, k = a.shape
    halves = [slice(0, tm // 2), slice(tm // 2, tm)]

    def body(a_ref, w_ref, r_ref, g_ref, b_ref, z_ref, x_ref, xb_ref):
        z = ALPHA * _Lanes(r_ref[s, :] for s in halves) + _ldot(_Lanes(a_ref[s, :] for s in halves), w_ref[...])
        zc = z - _mean(z, axis=-1, keepdims=True)
        y = zc * _rsqrt(_mean(zc * zc, axis=-1, keepdims=True) + LN_EPS) * g_ref[...] + b_ref[...]
        for s, zz, yy in zip(halves, z.xs, y.xs):
            z_ref[s, :] = zz
            x_ref[s, :] = yy
            xb_ref[s, :] = yy.astype(BF16)

    row = pl.BlockSpec((tm, D_MODEL), lambda i: (i, 0))
    outs, recv = _call(
        body, name, (t // tm,),
        [pl.BlockSpec((tm, k), lambda i: (i, 0)), _resident((k, D_MODEL)), row,
         _resident((1, D_MODEL)), _resident((1, D_MODEL))],
        [row, row, row],
        [jax.ShapeDtypeStruct((t, D_MODEL), F32)] * 2 + [jax.ShapeDtypeStruct((t, D_MODEL), BF16)],
        (a, w, resid, g, b), sem=("parallel",), carry=carry)
    return outs if carry is None else (outs, recv)


def _ln_bwd_mm_nt(name, dy, z, g, w, h=None, *, tm, tn, carry=None):
    t = dy.shape[0]
    n = w.shape[0]
    halves = [slice(0, tm // 2), slice(tm // 2, tm)]

    def body(*refs):
        if h is None:
            dy_ref, z_ref, g_ref, w_ref, dz_ref, dzb_ref, o_ref, dg_ref, db_ref = refs
        else:
            dy_ref, z_ref, g_ref, w_ref, h_ref, dz_ref, dzb_ref, o_ref, dg_ref, db_ref = refs

        @pl.when(pl.program_id(0) == 0)
        def _():
            dg_ref[...] = jnp.zeros_like(dg_ref)
            db_ref[...] = jnp.zeros_like(db_ref)

        zv = _Lanes(z_ref[s, :] for s in halves)
        dyv = _Lanes(dy_ref[s, :] for s in halves)
        mu = _mean(zv, axis=-1, keepdims=True)
        zc = zv - mu
        rstd = _rsqrt(_mean(zc * zc, axis=-1, keepdims=True) + LN_EPS)
        xh = zc * rstd
        gdy = dyv * g_ref[...]
        m1 = _mean(gdy, axis=-1, keepdims=True)
        m2 = _mean(gdy * xh, axis=-1, keepdims=True)
        dz = rstd * (gdy - m1 - xh * m2)
        dz_b = dz.astype(BF16)
        for s, a, a_b in zip(halves, dz.xs, dz_b.xs):
            dz_ref[s, :] = a
            dzb_ref[s, :] = a_b
        dg_ref[...] += _sum(dyv * xh, axis=0, keepdims=True).total()
        db_ref[...] += _sum(dyv, axis=0, keepdims=True).total()
        for c in range(n // tn):
            cols = slice(c * tn, (c + 1) * tn)
            acc = _ldot_nt(dz_b, w_ref[cols, :])
            if h is not None:
                acc = acc * (2.0 * _sqrt(_Lanes(h_ref[s, cols] for s in halves).astype(F32)))
            for s, a in zip(halves, acc.xs):
                o_ref[s, cols] = a.astype(BF16)

    row = pl.BlockSpec((tm, D_MODEL), lambda i: (i, 0))
    vec = _resident((1, D_MODEL))
    tile = pl.BlockSpec((tm, n), lambda i: (i, 0))
    in_specs = [row, row, vec, _resident((n, D_MODEL))]
    args = [dy, z, g, w]
    if h is not None:
        in_specs.append(tile)
        args.append(h)
    outs, recv = _call(
        body, name, (t // tm,), in_specs, [row, row, tile, vec, vec],
        [jax.ShapeDtypeStruct((t, D_MODEL), F32), jax.ShapeDtypeStruct((t, D_MODEL), BF16),
         jax.ShapeDtypeStruct((t, n), BF16), jax.ShapeDtypeStruct((1, D_MODEL), F32),
         jax.ShapeDtypeStruct((1, D_MODEL), F32)], args, sem=("arbitrary",), carry=carry)
    return outs if carry is None else (outs, recv)


def _mm_tn(name, a, b, *, tk, tmo, tno, carry=None):
    t, m = a.shape
    n = b.shape[1]
    nk = t // tk

    def body(a_ref, b_ref, o_ref, acc_ref):
        k = pl.program_id(2)
        p = _dot_tn(a_ref[...].astype(BF16), b_ref[...].astype(BF16))

        @pl.when(k == 0)
        def _():
            acc_ref[...] = p

        @pl.when(k > 0)
        def _():
            acc_ref[...] += p

        @pl.when(k == nk - 1)
        def _():
            o_ref[...] = acc_ref[...].astype(BF16)

    outs, recv = _call(
        body, name, (m // tmo, n // tno, nk),
        [pl.BlockSpec((tk, tmo), lambda i, j, k: (k, i)), pl.BlockSpec((tk, tno), lambda i, j, k: (k, j))],
        [pl.BlockSpec((tmo, tno), lambda i, j, k: (i, j))], [jax.ShapeDtypeStruct((m, n), BF16)], (a, b),
        scratch=[pltpu.VMEM((tmo, tno), F32)], sem=("parallel", "parallel", "arbitrary"), carry=carry)
    return outs[0] if carry is None else (outs[0], recv)


def _mm_nt_sum(name, pieces, offsets, w, resid, *, tm, carry=None):
    t = resid.shape[0]
    widths = [p.shape[1] for p in pieces]
    n_p = len(pieces)

    def body(*refs):
        p_refs, w_ref, r_ref, o_ref = refs[:n_p], refs[n_p], refs[n_p + 1], refs[n_p + 2]
        acc = ALPHA * r_ref[...]
        for p_ref, off, wd in zip(p_refs, offsets, widths):
            acc = acc + _dot_nt(p_ref[...], w_ref[:, off:off + wd])
        o_ref[...] = acc

    row = pl.BlockSpec((tm, D_MODEL), lambda i: (i, 0))
    outs, recv = _call(
        body, name, (t // tm,),
        [pl.BlockSpec((tm, wd), lambda i: (i, 0)) for wd in widths] + [_resident(w.shape), row],
        [row], [jax.ShapeDtypeStruct((t, D_MODEL), F32)], (*pieces, w, resid), sem=("parallel",), carry=carry)
    return outs[0] if carry is None else (outs[0], recv)


def _chunk_mask(rows):
    r = lax.broadcasted_iota(jnp.int32, (rows, rows), 0)
    c = lax.broadcasted_iota(jnp.int32, (rows, rows), 1)
    return ((r // CHUNK) == (c // CHUNK)) & (c <= r)


class _Lanes:
    def __init__(self, xs):
        self.xs = list(xs)

    def _with(self, other, f):
        if isinstance(other, _Lanes):
            return _Lanes([f(a, b) for a, b in zip(self.xs, other.xs)])
        return _Lanes([f(a, other) for a in self.xs])

    def __add__(self, o):
        return self._with(o, lambda a, b: a + b)

    def __radd__(self, o):
        return self._with(o, lambda a, b: b + a)

    def __sub__(self, o):
        return self._with(o, lambda a, b: a - b)

    def __rsub__(self, o):
        return self._with(o, lambda a, b: b - a)

    def __mul__(self, o):
        return self._with(o, lambda a, b: a * b)

    def __rmul__(self, o):
        return self._with(o, lambda a, b: b * a)

    def __truediv__(self, o):
        return self._with(o, lambda a, b: a / b)

    def __rtruediv__(self, o):
        return self._with(o, lambda a, b: b / a)

    def __neg__(self):
        return _Lanes([-a for a in self.xs])

    def __ge__(self, o):
        return self._with(o, lambda a, b: a >= b)

    def __getitem__(self, idx):
        return _Lanes([a[idx] for a in self.xs])

    def astype(self, dtype):
        return _Lanes([a.astype(dtype) for a in self.xs])

    def total(self):
        return functools.reduce(lambda a, b: a + b, self.xs)


def _lift(f):
    def g(*args, **kw):
        lanes = [a for a in args if isinstance(a, _Lanes)]
        if not lanes:
            return f(*args, **kw)
        return _Lanes([f(*[a.xs[i] if isinstance(a, _Lanes) else a for a in args], **kw)
                       for i in range(len(lanes[0].xs))])
    return g


def _concat(parts, axis):
    if isinstance(parts[0], _Lanes):
        return _Lanes([jnp.concatenate([p.xs[i] for p in parts], axis=axis) for i in range(len(parts[0].xs))])
    return jnp.concatenate(parts, axis=axis)


_exp, _log, _abs, _sqrt, _where = _lift(jnp.exp), _lift(jnp.log), _lift(jnp.abs), _lift(jnp.sqrt), _lift(jnp.where)
_sum, _mean, _rsqrt, _bcast = _lift(jnp.sum), _lift(jnp.mean), _lift(lax.rsqrt), _lift(jnp.broadcast_to)
_ldot, _ldot_nt, _ldot_tn = _lift(_dot), _lift(_dot_nt), _lift(_dot_tn)
_lsigmoid = _lift(_sigmoid)


def _mask_sum(mask_b, x, transpose=False):
    f = _ldot_tn if transpose else _ldot
    hi = x.astype(BF16)
    lo = (x - hi.astype(F32)).astype(BF16)
    return f(mask_b, hi) + f(mask_b, lo)


def _chunk_row(x, pos, rows):
    nc = rows // CHUNK

    def one(a):
        a3 = a.reshape(nc, CHUNK, HEAD_DIM)
        return jnp.broadcast_to(a3[:, pos:pos + 1, :], (nc, CHUNK, HEAD_DIM)).reshape(rows, HEAD_DIM)

    return _lift(one)(x)


def _chunk_total(x, rows):
    nc = rows // CHUNK

    def one(a):
        tot = jnp.sum(a.reshape(nc, CHUNK, HEAD_DIM), axis=1, keepdims=True)
        return jnp.broadcast_to(tot, (nc, CHUNK, HEAD_DIM)).reshape(rows, HEAD_DIM)

    return _lift(one)(x)


def _sigmoid_pair(x):
    e = _exp(-_abs(x))
    big = 1.0 / (1.0 + e)
    small = e * big
    pos = x >= 0.0
    return _where(pos, big, small), _where(pos, small, big)


def _hg_gates(q_raw, fl, lb, rows, mask):
    tri = mask.astype(BF16)
    sg, sg_neg = _sigmoid_pair(fl)
    forget = lb + (1.0 - lb) * sg
    k = (1.0 - lb) * sg_neg
    sq = _lsigmoid(q_raw)
    qs = q_raw * sq
    bc = _mask_sum(tri, _log(forget))
    bref = _chunk_row(bc, CHUNK // 2 - 1, rows)
    blast = _chunk_row(bc, CHUNK - 1, rows)
    return dict(tri=tri, sg=sg, sg_neg=sg_neg, forget=forget, k=k, sq=sq, qs=qs,
                e_a=_exp(bc - bref), e_b=_exp(bref - bc), e_q=_exp(bc), e_k=_exp(blast - bc),
                dec=_exp(blast))


HG_GROUP = 4


def _hg_lanes(bsz):
    return [(hh, slice(hh * HEAD_DIM, (hh + 1) * HEAD_DIM), b) for hh in range(HG_GROUP) for b in range(bsz)]


def _hg_read(ref, lanes):
    return _Lanes(ref[b, :, cs].astype(F32) for _, cs, b in lanes)


def _hg_write(ref, lanes, val, offset=0):
    for (_, cs, b), a in zip(lanes, val.xs):
        ref[b, :, offset + cs.start:offset + cs.stop] = a


def _hgrn_fwd(proj, hf, lb, nw, *, bsz, seq, carry=None):
    rows = min(ROWS_HG, seq)
    nt = seq // rows
    nc = rows // CHUNK
    t = bsz * seq

    lanes = _hg_lanes(bsz)

    def body(q_ref, f_ref, v_ref, g_ref, lb_ref, nw_ref, y_ref, o_ref, st_ref, s_scr):
        @pl.when(pl.program_id(1) == 0)
        def _():
            s_scr[...] = jnp.zeros_like(s_scr)

        mask = _chunk_mask(rows)
        lb_v = _Lanes(lb_ref[:, cs] for _, cs, _ in lanes)
        gt = _hg_gates(_hg_read(q_ref, lanes), _hg_read(f_ref, lanes), lb_v, rows, mask)
        v_b = _hg_read(v_ref, lanes).astype(BF16)
        a_b = (gt["qs"] * gt["e_a"]).astype(BF16)
        b_b = (gt["k"] * gt["e_b"]).astype(BF16)
        qi_b = (gt["qs"] * gt["e_q"]).astype(BF16)
        ko_b = (gt["k"] * gt["e_k"]).astype(BF16)
        scores = _where(mask, _ldot_nt(a_b, b_b), 0.0)
        o_intra = _ldot(scores.astype(BF16), v_b)

        s = _Lanes(s_scr[i] for i in range(len(lanes)))
        parts = []
        for n in range(nc):
            sl = slice(n * CHUNK, (n + 1) * CHUNK)
            s_b = s.astype(BF16)
            for (hh, _, b), a in zip(lanes, s_b.xs):
                st_ref[hh, b, n] = a
            parts.append(_ldot_nt(qi_b[sl], s_b))
            s = s * gt["dec"][n * CHUNK:n * CHUNK + 1] + _ldot_tn(v_b[sl], ko_b[sl])
        for i, a in enumerate(s.xs):
            s_scr[i] = a
        o = o_intra + _concat(parts, 0)
        _hg_write(o_ref, lanes, o)
        r = _rsqrt(_mean(o * o, axis=-1, keepdims=True) + RMS_EPS)
        g = _hg_read(g_ref, lanes)
        _hg_write(y_ref, lanes, (o * r * nw_ref[...] * (g * _lsigmoid(g))).astype(BF16))

    wide = HG_GROUP * HEAD_DIM

    def col(base):
        return pl.BlockSpec((bsz, rows, wide), lambda h, j: (0, j, base // wide + h))

    out_tile = pl.BlockSpec((bsz, rows, wide), lambda h, j: (0, j, h))
    p3 = proj.reshape(bsz, seq, IN_COLS)
    outs, recv = _call(
        body, "hgrn_fwd", (HEADS // HG_GROUP, nt),
        [col(C_HQ), out_tile, col(C_HI), col(C_HG),
         pl.BlockSpec((1, wide), lambda h, j: (0, h)), _resident((1, HEAD_DIM))],
        [out_tile, out_tile,
         pl.BlockSpec((HG_GROUP, bsz, nc, HEAD_DIM, HEAD_DIM), lambda h, j: (h, 0, j, 0, 0))],
        [jax.ShapeDtypeStruct((bsz, seq, WIDTH), BF16), jax.ShapeDtypeStruct((bsz, seq, WIDTH), F32),
         jax.ShapeDtypeStruct((HEADS, bsz, seq // CHUNK, HEAD_DIM, HEAD_DIM), BF16)],
        (p3, hf.reshape(bsz, seq, WIDTH), p3, p3, lb, nw),
        scratch=[pltpu.VMEM((len(lanes), HEAD_DIM, HEAD_DIM), F32)],
        sem=("parallel", "arbitrary"), carry=carry)
    outs = [outs[0].reshape(t, WIDTH), outs[1].reshape(t, WIDTH), outs[2]]
    return outs if carry is None else (outs, recv)


def _hgrn_bwd(proj, hf, lb, nw, o_pre, states, dy, *, bsz, seq, carry=None):
    rows = min(ROWS_HG, seq)
    nt = seq // rows
    nc = rows // CHUNK
    t = bsz * seq
    lanes = _hg_lanes(bsz)

    def body(q_ref, f_ref, v_ref, g_ref, lb_ref, nw_ref, o_ref, st_ref, dy_ref, dh_ref, dlb_ref, dnw_ref, ds_scr):
        h, j = pl.program_id(0), pl.program_id(1)

        @pl.when(j == 0)
        def _():
            ds_scr[...] = jnp.zeros_like(ds_scr)
            dlb_ref[...] = jnp.zeros_like(dlb_ref)

        @pl.when((h == 0) & (j == 0))
        def _():
            dnw_ref[...] = jnp.zeros_like(dnw_ref)

        mask = _chunk_mask(rows)
        q_raw = _hg_read(q_ref, lanes)
        lb_v = _Lanes(lb_ref[:, cs] for _, cs, _ in lanes)
        gt = _hg_gates(q_raw, _hg_read(f_ref, lanes), lb_v, rows, mask)
        v_b = _hg_read(v_ref, lanes).astype(BF16)
        a_f = gt["qs"] * gt["e_a"]
        b_f = gt["k"] * gt["e_b"]
        qi_f = gt["qs"] * gt["e_q"]
        ko_f = gt["k"] * gt["e_k"]
        a_b, b_b, qi_b, ko_b = a_f.astype(BF16), b_f.astype(BF16), qi_f.astype(BF16), ko_f.astype(BF16)

        o = _hg_read(o_ref, lanes)
        nw_v = nw_ref[...]
        g = _hg_read(g_ref, lanes)
        dyv = _hg_read(dy_ref, lanes)
        r = _rsqrt(_mean(o * o, axis=-1, keepdims=True) + RMS_EPS)
        sgg = _lsigmoid(g)
        d_g = dyv * (o * r * nw_v) * (sgg * (1.0 + g * (1.0 - sgg)))
        d_on = dyv * (g * sgg)
        dnw_ref[...] += _sum(d_on * o * r, axis=0, keepdims=True).total()
        tt = d_on * nw_v
        d_o = r * tt - o * (r * r * r) * _mean(tt * o, axis=-1, keepdims=True)
        do_b = d_o.astype(BF16)

        sc_b = _where(mask, _ldot_nt(a_b, b_b), 0.0).astype(BF16)
        dsc_b = _where(mask, _ldot_nt(do_b, v_b), 0.0).astype(BF16)
        d_v = _ldot_tn(sc_b, do_b)
        d_a = _ldot(dsc_b, b_b)
        d_bm = _ldot_tn(dsc_b, a_b)

        ds = _Lanes(ds_scr[i] for i in range(len(lanes)))
        dqi_parts, dko_parts, dvi_parts, ddec_parts = [None] * nc, [None] * nc, [None] * nc, [None] * nc
        for n in reversed(range(nc)):
            sl = slice(n * CHUNK, (n + 1) * CHUNK)
            dec_n = gt["dec"][n * CHUNK:n * CHUNK + 1]
            ds_b = ds.astype(BF16)
            s_n = _Lanes(st_ref[hh, b, n] for hh, _, b in lanes)
            dqi_parts[n] = _ldot(do_b[sl], s_n)
            dko_parts[n] = _ldot(v_b[sl], ds_b)
            dvi_parts[n] = _ldot_nt(ko_b[sl], ds_b)
            d_dec = _sum(ds * s_n.astype(F32), axis=0, keepdims=True)
            ddec_parts[n] = _bcast(d_dec * dec_n, (CHUNK, HEAD_DIM))
            ds = ds * dec_n + _ldot_tn(do_b[sl], qi_b[sl])
        for i, a in enumerate(ds.xs):
            ds_scr[i] = a
        d_qi = _concat(dqi_parts, 0)
        d_ko = _concat(dko_parts, 0)
        d_v = d_v + _concat(dvi_parts, 0)

        d_qs = d_a * gt["e_a"] + d_qi * gt["e_q"]
        d_k = d_bm * gt["e_b"] + d_ko * gt["e_k"]
        t_a, t_b, t_q, t_k = d_a * a_f, d_bm * b_f, d_qi * qi_f, d_ko * ko_f
        d_bref = _chunk_total(t_b - t_a, rows)
        d_blast = _chunk_total(t_k, rows) + _concat(ddec_parts, 0)
        pos = lax.broadcasted_iota(jnp.int32, (rows, HEAD_DIM), 0) % CHUNK
        d_bc = (t_a - t_b + t_q - t_k + _where(pos == CHUNK // 2 - 1, d_bref, 0.0)
                + _where(pos == CHUNK - 1, d_blast, 0.0))
        d_logf = _mask_sum(gt["tri"], d_bc, transpose=True)

        sg, sg_neg = gt["sg"], gt["sg_neg"]
        inv_f = 1.0 / gt["forget"]
        common = (1.0 - lb_v) * sg * sg_neg
        d_fl = common * (d_logf * inv_f - d_k)
        d_lb = _sum(sg_neg * (d_logf * inv_f - d_k), axis=0, keepdims=True)
        for (_, cs, _), a in zip(lanes, d_lb.xs):
            dlb_ref[:, cs] += a
        sq = gt["sq"]
        _hg_write(dh_ref, lanes, (d_qs * (sq * (1.0 + q_raw * (1.0 - sq)))).astype(BF16), 0)
        _hg_write(dh_ref, lanes, d_fl.astype(BF16), WIDTH)
        _hg_write(dh_ref, lanes, d_v.astype(BF16), 2 * WIDTH)
        _hg_write(dh_ref, lanes, d_g.astype(BF16), 3 * WIDTH)

    assert HG_GROUP == HEADS, "the combined gradient block needs all heads in one grid step"
    wide = HG_GROUP * HEAD_DIM

    def col(base):
        return pl.BlockSpec((bsz, rows, wide), lambda h, j: (0, nt - 1 - j, base // wide + h))

    tile = pl.BlockSpec((bsz, rows, wide), lambda h, j: (0, nt - 1 - j, h))
    head_vec = pl.BlockSpec((1, wide), lambda h, j: (0, h))
    p3 = proj.reshape(bsz, seq, IN_COLS)
    outs, recv = _call(
        body, "hgrn_bwd", (HEADS // HG_GROUP, nt),
        [col(C_HQ), tile, col(C_HI), col(C_HG), head_vec, _resident((1, HEAD_DIM)), tile,
         pl.BlockSpec((HG_GROUP, bsz, nc, HEAD_DIM, HEAD_DIM), lambda h, j: (h, 0, nt - 1 - j, 0, 0)), tile],
        [pl.BlockSpec((bsz, rows, 4 * WIDTH), lambda h, j: (0, nt - 1 - j, 0)), head_vec, _resident((1, HEAD_DIM))],
        [jax.ShapeDtypeStruct((bsz, seq, 4 * WIDTH), BF16), jax.ShapeDtypeStruct((1, WIDTH), F32),
         jax.ShapeDtypeStruct((1, HEAD_DIM), F32)],
        (p3, hf.reshape(bsz, seq, WIDTH), p3, p3, lb, nw, o_pre.reshape(bsz, seq, WIDTH), states,
         dy.reshape(bsz, seq, WIDTH)),
        scratch=[pltpu.VMEM((len(lanes), HEAD_DIM, HEAD_DIM), F32)],
        sem=("arbitrary", "arbitrary"), carry=carry)
    outs = [outs[0].reshape(t, 4 * WIDTH), outs[1], outs[2]]
    return outs if carry is None else (outs, recv)


def _mem_kv(mem2d, w_k, w_v):
    rows = mem2d.shape[0]

    def body(m_ref, wk_ref, wv_ref, k_ref, v_ref):
        m_b = m_ref[...].astype(BF16)
        k_ref[...] = _dot(m_b, wk_ref[...]).astype(BF16)
        v_ref[...] = _dot(m_b, wv_ref[...]).astype(BF16)

    return pl.pallas_call(
        body, name="mem_kv", grid=(rows // MEM_LEN,),
        in_specs=[pl.BlockSpec((MEM_LEN, D_MODEL), lambda i: (i, 0)), _resident((D_MODEL, WIDTH)),
                  _resident((D_MODEL, WIDTH))],
        out_specs=[pl.BlockSpec((MEM_LEN, WIDTH), lambda i: (i, 0))] * 2,
        out_shape=[jax.ShapeDtypeStruct((rows, WIDTH), BF16)] * 2,
        compiler_params=_params("parallel"),
    )(mem2d, w_k, w_v)


def _softmax_rows(s):
    m = _lift(jnp.max)(s, axis=-1, keepdims=True)
    e = _exp(s - m)
    return e / _sum(e, axis=-1, keepdims=True)


def _attn_fwd(proj, mk, mv, *, tm, seq):
    t = proj.shape[0]
    per_b = seq // tm
    scale = HEAD_DIM ** -0.5

    def body(q_ref, k_ref, v_ref, y_ref):
        heads = [slice(h * HEAD_DIM, (h + 1) * HEAD_DIM) for h in range(HEADS)]
        q_b = _Lanes(q_ref[:, sl] for sl in heads).astype(BF16)
        p = _softmax_rows(_ldot_nt(q_b, _Lanes(k_ref[:, sl] for sl in heads)) * scale)
        out = _ldot(p.astype(BF16), _Lanes(v_ref[:, sl] for sl in heads))
        y_ref[...] = jnp.concatenate(out.xs, axis=-1).astype(BF16)

    kv = pl.BlockSpec((MEM_LEN, WIDTH), lambda i: (i // per_b, 0))
    return pl.pallas_call(
        body, name="attn_fwd", grid=(t // tm,),
        in_specs=[pl.BlockSpec((tm, WIDTH), lambda i: (i, C_MQ // WIDTH)), kv, kv],
        out_specs=pl.BlockSpec((tm, WIDTH), lambda i: (i, 0)),
        out_shape=jax.ShapeDtypeStruct((t, WIDTH), BF16),
        compiler_params=_params("parallel"),
    )(proj, mk, mv)


def _attn_bwd(proj, mk, mv, dy, *, tm, seq):
    t = proj.shape[0]
    per_b = seq // tm
    scale = HEAD_DIM ** -0.5

    def body(q_ref, k_ref, v_ref, dy_ref, dq_ref, dk_ref, dv_ref):
        i = pl.program_id(0)

        @pl.when(i % per_b == 0)
        def _():
            dk_ref[...] = jnp.zeros_like(dk_ref)
            dv_ref[...] = jnp.zeros_like(dv_ref)

        heads = [slice(h * HEAD_DIM, (h + 1) * HEAD_DIM) for h in range(HEADS)]
        q_b = _Lanes(q_ref[:, sl] for sl in heads).astype(BF16)
        k_b, v_b = _Lanes(k_ref[:, sl] for sl in heads), _Lanes(v_ref[:, sl] for sl in heads)
        p = _softmax_rows(_ldot_nt(q_b, k_b) * scale)
        dy_b = _Lanes(dy_ref[:, sl] for sl in heads).astype(BF16)
        dp = _ldot_nt(dy_b, v_b)
        d_v = _ldot_tn(p.astype(BF16), dy_b)
        ds_b = (p * (dp - _sum(dp * p, axis=-1, keepdims=True)) * scale).astype(BF16)
        dq_ref[...] = jnp.concatenate(_ldot(ds_b, k_b).xs, axis=-1).astype(BF16)
        dk_ref[...] += jnp.concatenate(_ldot_tn(ds_b, q_b).xs, axis=-1)
        dv_ref[...] += jnp.concatenate(d_v.xs, axis=-1)

    kv = pl.BlockSpec((MEM_LEN, WIDTH), lambda i: (i // per_b, 0))
    tile = pl.BlockSpec((tm, WIDTH), lambda i: (i, 0))
    n_mem = mk.shape[0]
    return pl.pallas_call(
        body, name="attn_bwd", grid=(t // tm,),
        in_specs=[pl.BlockSpec((tm, WIDTH), lambda i: (i, C_MQ // WIDTH)), kv, kv, tile],
        out_specs=[tile, kv, kv],
        out_shape=[jax.ShapeDtypeStruct((t, WIDTH), BF16), jax.ShapeDtypeStruct((n_mem, WIDTH), F32),
                   jax.ShapeDtypeStruct((n_mem, WIDTH), F32)],
        compiler_params=_params("arbitrary"),
    )(proj, mk, mv, dy)


HALO = 16


def _shift_down(u, halo, k, row):
    out = pltpu.roll(u, k, 0)
    for m in range(k):
        out = jnp.where(row == m, halo[HALO - k + m:HALO - k + m + 1, :], out)
    return out


def _shift_up(u, halo, k, row, tm):
    out = pltpu.roll(u, tm - k, 0)
    for m in range(k):
        out = jnp.where(row == tm - k + m, halo[m:m + 1, :], out)
    return out


def _merge_fwd(proj, y_b, y_c, conv_w, w_branch, b_gate, *, tm, seq, carry=None):
    t = proj.shape[0]
    per_b = seq // tm
    hb = tm // HALO

    def body(cb_ref, cc_ref, ch_ref, cch_ref, chh_ref, ga_ref, gb_ref, gc_ref, yb_ref, yc_ref, cw_ref, wb_ref,
             bg_ref, ya_ref, pa_ref, pb_ref, pc_ref, mg_ref, sa_ref, sb_ref, sc_ref):
        i = pl.program_id(0)
        row = lax.broadcasted_iota(jnp.int32, (tm, WIDTH), 0)
        u = cc_ref[...].astype(F32) * ch_ref[...].astype(F32)
        halo = jnp.where(i % per_b == 0, 0.0, cch_ref[...].astype(F32) * chh_ref[...].astype(F32))
        cw = cw_ref[...]
        y = cw[0:1] * _shift_down(u, halo, 2, row) + cw[1:2] * _shift_down(u, halo, 1, row) + cw[2:3] * u
        ya_b = (cb_ref[...].astype(F32) * y).astype(BF16)
        ya_ref[...] = ya_b
        merged = None
        for idx, (y_in, g_ref, p_ref, s_ref) in enumerate(((ya_b, ga_ref, pa_ref, sa_ref),
                                                            (yb_ref[...], gb_ref, pb_ref, sb_ref),
                                                            (yc_ref[...], gc_ref, pc_ref, sc_ref))):
            p = _dot(y_in, wb_ref[idx])
            p_ref[...] = p.astype(BF16)
            sg = _sigmoid(g_ref[...].astype(F32) + bg_ref[:, idx * D_MODEL:(idx + 1) * D_MODEL])
            s_ref[...] = sg.astype(BF16)
            term = sg * p
            merged = term if merged is None else merged + term
        mg_ref[...] = merged.astype(BF16)

    def half(c):
        return pl.BlockSpec((tm, WIDTH), lambda i: (i, c // WIDTH))

    def prev(c):
        return pl.BlockSpec((HALO, WIDTH), lambda i: (jnp.maximum(i * hb - 1, 0), c // WIDTH))

    def gate(k):
        return pl.BlockSpec((tm, D_MODEL), lambda i: (i, C_GA // D_MODEL + k))

    tile512 = pl.BlockSpec((tm, WIDTH), lambda i: (i, 0))
    tile1k = pl.BlockSpec((tm, D_MODEL), lambda i: (i, 0))
    outs, recv = _call(
        body, "merge_fwd", (t // tm,),
        [half(C_CB), half(C_CC), half(C_CH), prev(C_CC), prev(C_CH), gate(0), gate(1), gate(2),
         tile512, tile512, _resident((CONV_K, WIDTH)), _resident((3, WIDTH, D_MODEL)), _resident((1, 3 * D_MODEL))],
        [tile512] + [tile1k] * 7,
        [jax.ShapeDtypeStruct((t, WIDTH), BF16)] + [jax.ShapeDtypeStruct((t, D_MODEL), BF16)] * 7,
        (proj, proj, proj, proj, proj, proj, proj, proj, y_b, y_c, conv_w, w_branch, b_gate),
        sem=("parallel",), carry=carry)
    return outs if carry is None else (outs, recv)


def _merge_bwd(dmerged, projections, gates, w_branch, *, tm):
    t = dmerged.shape[0]

    def body(dm_ref, pa_ref, pb_ref, pc_ref, sa_ref, sb_ref, sc_ref, wb_ref,
             dgt_ref, dpa_ref, dpb_ref, dpc_ref, dya_ref, dyb_ref, dyc_ref, dbg_ref):
        i = pl.program_id(0)

        @pl.when(i == 0)
        def _():
            dbg_ref[...] = jnp.zeros_like(dbg_ref)

        dm = dm_ref[...].astype(F32)
        for idx, (p_ref, s_ref, dp_ref, dy_ref) in enumerate(((pa_ref, sa_ref, dpa_ref, dya_ref),
                                                              (pb_ref, sb_ref, dpb_ref, dyb_ref),
                                                              (pc_ref, sc_ref, dpc_ref, dyc_ref))):
            cols = slice(idx * D_MODEL, (idx + 1) * D_MODEL)
            sg = s_ref[...].astype(F32)
            dp = dm * sg
            dp_b = dp.astype(BF16)
            dp_ref[...] = dp_b
            dgate = dp * p_ref[...].astype(F32) * (1.0 - sg)
            dgt_ref[:, cols] = dgate.astype(BF16)
            dbg_ref[:, cols] += jnp.sum(dgate, axis=0, keepdims=True)
            dy_ref[...] = _dot_nt(dp_b, wb_ref[idx]).astype(BF16)

    tile512 = pl.BlockSpec((tm, WIDTH), lambda i: (i, 0))
    tile1k = pl.BlockSpec((tm, D_MODEL), lambda i: (i, 0))
    return pl.pallas_call(
        body, name="merge_bwd", grid=(t // tm,),
        in_specs=[tile1k] * 7 + [_resident((3, WIDTH, D_MODEL))],
        out_specs=[pl.BlockSpec((tm, 3 * D_MODEL), lambda i: (i, 0)), tile1k, tile1k, tile1k,
                   tile512, tile512, tile512, _resident((1, 3 * D_MODEL))],
        out_shape=[jax.ShapeDtypeStruct((t, 3 * D_MODEL), BF16)] + [jax.ShapeDtypeStruct((t, D_MODEL), BF16)] * 3
                  + [jax.ShapeDtypeStruct((t, WIDTH), BF16)] * 3 + [jax.ShapeDtypeStruct((1, 3 * D_MODEL), F32)],
        compiler_params=_params("arbitrary"),
    )(dmerged, *projections, *gates, w_branch)


def _conv_bwd(proj, dya, conv_w, *, tm, seq):
    t = proj.shape[0]
    per_b = seq // tm
    hb = tm // HALO
    last_blk = t // HALO - 1

    def body(cb_ref, cc_ref, ch_ref, cch_ref, chh_ref, dya_ref, cbn_ref, dyan_ref, cw_ref, d_ref, dcw_ref):
        i = pl.program_id(0)

        @pl.when(i == 0)
        def _():
            dcw_ref[...] = jnp.zeros_like(dcw_ref)

        row = lax.broadcasted_iota(jnp.int32, (tm, WIDTH), 0)
        cb, cc, ch = cb_ref[...].astype(F32), cc_ref[...].astype(F32), ch_ref[...].astype(F32)
        u = cc * ch
        halo = jnp.where(i % per_b == 0, 0.0, cch_ref[...].astype(F32) * chh_ref[...].astype(F32))
        u1 = _shift_down(u, halo, 1, row)
        u2 = _shift_down(u, halo, 2, row)
        cw = cw_ref[...]
        y = cw[0:1] * u2 + cw[1:2] * u1 + cw[2:3] * u
        dya = dya_ref[...].astype(F32)
        dy = dya * cb
        nxt = jnp.where(i % per_b == per_b - 1, 0.0, dyan_ref[...].astype(F32) * cbn_ref[...].astype(F32))
        du = cw[2:3] * dy + cw[1:2] * _shift_up(dy, nxt, 1, row, tm) + cw[0:1] * _shift_up(dy, nxt, 2, row, tm)
        d_ref[:, 0:WIDTH] = (dya * y).astype(BF16)
        d_ref[:, WIDTH:2 * WIDTH] = (du * ch).astype(BF16)
        d_ref[:, 2 * WIDTH:3 * WIDTH] = (du * cc).astype(BF16)
        dcw_ref[0:1, :] += jnp.sum(dy * u2, axis=0, keepdims=True)
        dcw_ref[1:2, :] += jnp.sum(dy * u1, axis=0, keepdims=True)
        dcw_ref[2:3, :] += jnp.sum(dy * u, axis=0, keepdims=True)

    def half(c):
        return pl.BlockSpec((tm, WIDTH), lambda i: (i, c // WIDTH))

    def prev(c):
        return pl.BlockSpec((HALO, WIDTH), lambda i: (jnp.maximum(i * hb - 1, 0), c // WIDTH))

    def nxt(c):
        return pl.BlockSpec((HALO, WIDTH), lambda i: (jnp.minimum((i + 1) * hb, last_blk), c // WIDTH))

    return pl.pallas_call(
        body, name="conv_bwd", grid=(t // tm,),
        in_specs=[half(C_CB), half(C_CC), half(C_CH), prev(C_CC), prev(C_CH),
                  pl.BlockSpec((tm, WIDTH), lambda i: (i, 0)), nxt(C_CB), nxt(0), _resident((CONV_K, WIDTH))],
        out_specs=[pl.BlockSpec((tm, 3 * WIDTH), lambda i: (i, 0)), _resident((CONV_K, WIDTH))],
        out_shape=[jax.ShapeDtypeStruct((t, 3 * WIDTH), BF16), jax.ShapeDtypeStruct((CONV_K, WIDTH), F32)],
        compiler_params=_params("arbitrary"),
    )(proj, proj, proj, proj, proj, dya, proj, dya, conv_w)


def _loss_head(y, target, *, tm):
    t = y.shape[0]

    def body(y_ref, t_ref, dy_ref, l_ref):
        @pl.when(pl.program_id(0) == 0)
        def _():
            l_ref[...] = jnp.zeros_like(l_ref)

        err = y_ref[...] - t_ref[...]
        dy_ref[...] = err * (1.0 / D_MODEL)
        per_row = jnp.sum(err * err, axis=-1, keepdims=True) * (1.0 / D_MODEL)
        l_ref[...] += 0.5 * jnp.sum(per_row, axis=0, keepdims=True)

    row = pl.BlockSpec((tm, D_MODEL), lambda i: (i, 0))
    return pl.pallas_call(
        body, name="loss_head", grid=(t // tm,),
        in_specs=[row, row], out_specs=[row, _resident((8, 128))],
        out_shape=[jax.ShapeDtypeStruct((t, D_MODEL), F32), jax.ShapeDtypeStruct((8, 128), F32)],
        compiler_params=_params("arbitrary"),
    )(y, target)


def _lb_softmax(lower_bounds):
    x = lower_bounds
    e = jnp.exp(x - jnp.max(x, axis=0, keepdims=True))
    return e / jnp.sum(e, axis=0, keepdims=True)


def _lb_fwd(lower_bounds):
    def body(x_ref, o_ref):
        s = _lb_softmax(x_ref[...])
        c = s[0:1]
        o_ref[0:1, :] = c - s[0:1]
        for l in range(1, DEPTH):
            c = c + s[l:l + 1]
            o_ref[l:l + 1, :] = c - s[0:1]

    return pl.pallas_call(body, name="lb_fwd", out_shape=jax.ShapeDtypeStruct(lower_bounds.shape, F32))(lower_bounds)


def _lb_bwd(lower_bounds, d_lb_all):
    def body(x_ref, d_ref, o_ref):
        s = _lb_softmax(x_ref[...])
        d = d_ref[...]
        rows = [jnp.zeros_like(d[0:1])]
        for j in range(1, DEPTH):
            acc = d[j:j + 1]
            for l in range(j + 1, DEPTH):
                acc = acc + d[l:l + 1]
            rows.append(acc)
        inner = rows[0] * s[0:1]
        for j in range(1, DEPTH):
            inner = inner + rows[j] * s[j:j + 1]
        for j in range(DEPTH):
            o_ref[j:j + 1, :] = s[j:j + 1] * (rows[j] - inner)

    return pl.pallas_call(body, name="lb_bwd", out_shape=jax.ShapeDtypeStruct(lower_bounds.shape, F32))(
        lower_bounds, d_lb_all)


def _adamw(w, g, m, v):
    m2 = ADAM_B1 * m + (1.0 - ADAM_B1) * g
    v2 = ADAM_B2 * v + (1.0 - ADAM_B2) * (g * g)
    m_hat = m2 / (1.0 - ADAM_B1 ** ADAM_STEP)
    v_hat = v2 / (1.0 - ADAM_B2 ** ADAM_STEP)
    delta = -ADAM_LR * (m_hat / (jnp.sqrt(v_hat) + ADAM_EPS) + ADAM_WD * w)
    return delta, m2, v2


def _adam_small(name, g, w, m, v):
    shape = w.shape
    flat = (-1, shape[-1])
    g2, w2, m2, v2 = (a.reshape(flat) for a in (g, w, m, v))

    def body(g_ref, w_ref, m_ref, v_ref, d_ref, mo_ref, vo_ref):
        d, mm, vv = _adamw(w_ref[...], g_ref[...], m_ref[...], v_ref[...])
        d_ref[...] = d
        mo_ref[...] = mm
        vo_ref[...] = vv

    outs = pl.pallas_call(body, name=name, out_shape=[jax.ShapeDtypeStruct(w2.shape, F32)] * 3)(g2, w2, m2, v2)
    return [o.reshape(shape) for o in outs]


def _adam_shard(name, recvs, w, m, v, *, tr):
    _, r, c = w.shape

    def body(*refs):
        rc, (w_ref, m_ref, v_ref), (g_ref, d_ref, mo_ref, vo_ref) = refs[:DEPTH], refs[DEPTH:DEPTH + 3], refs[DEPTH + 3:]
        layer = pl.program_id(0)
        for cand in range(DEPTH):
            @pl.when(layer == cand)
            def _():
                g = rc[cand][0].astype(F32)
                for d in range(1, N_DEV):
                    g = g + rc[cand][d].astype(F32)
                dl, mm, vv = _adamw(w_ref[...], g, m_ref[...], v_ref[...])
                g_ref[...] = g
                d_ref[...] = dl
                mo_ref[...] = mm
                vo_ref[...] = vv

    def recv_spec(cand):
        return pl.BlockSpec((N_DEV, tr, c), lambda l, i: (0, jnp.where(l == cand, i, 0), 0))

    tile = pl.BlockSpec((None, tr, c), lambda l, i: (l, i, 0))
    return pl.pallas_call(
        body, name=name, grid=(DEPTH, r // tr),
        in_specs=[recv_spec(cand) for cand in range(DEPTH)] + [tile] * 3,
        out_specs=[tile] * 4,
        out_shape=[jax.ShapeDtypeStruct(w.shape, F32)] * 4,
        compiler_params=_params("parallel", "parallel"),
    )(*recvs, w, m, v)


def _sum_devices(name, x):
    def body(x_ref, o_ref):
        acc = x_ref[0]
        for d in range(1, N_DEV):
            acc = acc + x_ref[d]
        o_ref[...] = acc

    return pl.pallas_call(body, name=name, out_shape=jax.ShapeDtypeStruct(x.shape[1:], x.dtype))(x)


SMALL = (("lower_bounds", 1, 512), ("conv_w", CONV_K, WIDTH), ("hg_norm_w", 1, HEAD_DIM), ("b_gate", 3, D_MODEL),
         ("ln1_g", 1, D_MODEL), ("ln1_b", 1, D_MODEL), ("ln2_g", 1, D_MODEL), ("ln2_b", 1, D_MODEL))
SMALL_ROWS = sum(r for _, r, _ in SMALL)


def _pack_small(per_layer):
    flat = [a for layer in per_layer for a in layer]

    def body(*refs):
        ins, o_ref = refs[:-1], refs[-1]
        o_ref[...] = jnp.zeros_like(o_ref)
        it = iter(ins)
        for l in range(DEPTH):
            row = l * SMALL_ROWS
            for name, nrows, ncols in SMALL:
                ref = next(it)
                if name == "b_gate":
                    for k in range(nrows):
                        o_ref[row + k:row + k + 1, :] = ref[:, k * ncols:(k + 1) * ncols]
                else:
                    o_ref[row:row + nrows, 0:ncols] = ref[...]
                row += nrows

    return pl.pallas_call(body, name="pack_small_grads",
                          out_shape=jax.ShapeDtypeStruct((DEPTH * SMALL_ROWS, D_MODEL), F32))(*flat)


def _unpack_small(summed):
    s3 = summed.reshape(DEPTH, SMALL_ROWS, D_MODEL)
    out, row = {}, 0
    for name, nrows, ncols in SMALL:
        out[name] = s3[:, row:row + nrows, :ncols].reshape(DEPTH, nrows * ncols)
        row += nrows
    return out


def _natural_cols(g):
    nd = g.ndim
    perm = tuple(range(1, nd - 1)) + (0, nd - 1)
    t = jnp.transpose(g, perm)
    return t.reshape(t.shape[:-2] + (t.shape[-2] * t.shape[-1],))


def _natural_rows(g):
    return g.reshape(g.shape[0] * g.shape[1], g.shape[2])


def _hosted(hosts, key, fn):
    if not hosts or key not in hosts:
        return fn(None)
    ex, hook = hosts[key]
    outs, recv = fn(ex)
    hook(recv)
    return outs


def _layer_fwd(cur, cur_b, mem2d, wl, *, bsz, seq, hosts=None):
    tm = min(512, seq)
    proj, hf = _hosted(hosts, "in_proj", lambda c: _in_proj(cur_b, wl["w_in"], tm=min(1024, seq), carry=c))
    y_b, o_pre, states = _hosted(hosts, "hgrn_fwd", lambda c: _hgrn_fwd(proj, hf, wl["lb"], wl["nw"], bsz=bsz,
                                                                         seq=seq, carry=c))
    mk, mv = _mem_kv(mem2d, wl["w_mk"], wl["w_mv"])
    y_c = _attn_fwd(proj, mk, mv, tm=tm, seq=seq)
    y_a, pa, pb, pc, merged, sga, sgb, sgc = _hosted(
        hosts, "merge_fwd", lambda c: _merge_fwd(proj, y_b, y_c, wl["conv"], wl["w_br"], wl["b_gate"], tm=tm,
                                                 seq=seq, carry=c))
    z1, x1, x1_b = _hosted(hosts, "wo_ln", lambda c: _linear_ln("wo_ln", merged, wl["w_o"], cur, wl["ln1_g"],
                                                                  wl["ln1_b"], tm=tm, carry=c))
    hid = _hosted(hosts, "mlp_up", lambda c: _mm_nn("mlp_up", x1_b, wl["w_up"], tm=min(1024, seq), tn=2048,
                                                     out_dtype=BF16, relu2=True, carry=c))
    z2, x2, x2_b = _linear_ln("down_ln", hid, wl["w_down"], x1, wl["ln2_g"], wl["ln2_b"], tm=tm)
    return dict(x_b=cur_b, proj=proj, hf=hf, y_a=y_a, y_b=y_b, y_c=y_c, o_pre=o_pre, states=states, mk=mk, mv=mv,
                proj3=(pa, pb, pc), gates3=(sga, sgb, sgc), merged=merged, z1=z1, x1_b=x1_b, hid=hid, z2=z2, x2=x2,
                x2_b=x2_b)


def _layer_bwd(dcur, mem2d, s, wl, *, bsz, seq, plan=None):
    tm = min(512, seq)
    tk = min(2048, bsz * seq)
    g = {}

    def run(key, fn):
        made = plan[key](g) if plan and key in plan else None
        return _hosted({key: made} if made else None, key, fn)

    dz2, dz2_b, dhpre, d_ln2g, d_ln2b = run(
        "ln2_bwd_down", lambda c: _ln_bwd_mm_nt("ln2_bwd_down", dcur, s["z2"], wl["ln2_g"], wl["w_down"],
                                                s["hid"], tm=tm, tn=1024, carry=c))
    g["w_down"] = _mm_tn("grad_w_down", s["hid"], dz2_b, tk=tk, tmo=1024, tno=1024)
    dx1 = _mm_nt_sum("mlp_up_bwd", [dhpre], [0], wl["w_up"], dz2, tm=tm)
    g["w_up"] = run("grad_w_up", lambda c: _mm_tn("grad_w_up", s["x1_b"], dhpre, tk=tk, tmo=1024, tno=1024, carry=c))
    dz1, dz1_b, dmerged, d_ln1g, d_ln1b = _ln_bwd_mm_nt("ln1_bwd_wo", dx1, s["z1"], wl["ln1_g"], wl["w_o"],
                                                        tm=tm, tn=1024)
    g["w_o"] = _mm_tn("grad_w_o", s["merged"], dz1_b, tk=tk, tmo=1024, tno=1024)
    dgate, dpa, dpb, dpc, dya, dyb, dyc, d_bg = _merge_bwd(dmerged, s["proj3"], s["gates3"], wl["w_br"], tm=tm)
    g["w_br"] = jnp.stack([_mm_tn("grad_w_branch", yy, dp, tk=tk, tmo=512, tno=1024)
                           for yy, dp in ((s["y_a"], dpa), (s["y_b"], dpb), (s["y_c"], dpc))])
    d_conv, d_cw = _conv_bwd(s["proj"], dya, wl["conv"], tm=tm, seq=seq)
    dhg, d_lb, d_nw = run(
        "hgrn_bwd", lambda c: _hgrn_bwd(s["proj"], s["hf"], wl["lb"], wl["nw"], s["o_pre"], s["states"], dyb,
                                        bsz=bsz, seq=seq, carry=c))
    dmq, dmk, dmv = _attn_bwd(s["proj"], s["mk"], s["mv"], dyc, tm=tm, seq=seq)
    tkm = min(512, mem2d.shape[0])
    g["w_mk"] = _mm_tn("grad_w_mem", mem2d, dmk, tk=tkm, tmo=1024, tno=512)
    g["w_mv"] = _mm_tn("grad_w_mem", mem2d, dmv, tk=tkm, tmo=1024, tno=512)
    pieces = [d_conv, dhg, dmq, dgate]
    offsets = [C_CB, C_HQ, C_MQ, C_GA]
    g["w_in"] = jnp.concatenate(
        [_mm_tn("grad_w_in_%d" % p.shape[1], s["x_b"], p, tk=tk, tmo=1024,
                tno=next(w for w in (1024, 768, 512) if p.shape[1] % w == 0)) for p in pieces], axis=1)
    dx = run("in_proj_bwd", lambda c: _mm_nt_sum("in_proj_bwd", pieces, offsets, wl["w_in"], dz1,
                                                 tm=min(256, seq), carry=c))
    return dx, g, [d_lb, d_cw, d_nw, d_bg, d_ln1g, d_ln1b, d_ln2g, d_ln2b]


def kernel(x, mem, lower_bounds, w_in, conv_w, hg_norm_w, w_mem_k, w_mem_v, w_branch, b_gate, w_o, ln1_g, ln1_b, w_up, w_down, ln2_g, ln2_b, loss_target, m_lower_bounds, m_w_in, m_conv_w, m_hg_norm_w, m_w_mem_k, m_w_mem_v, m_w_branch, m_b_gate, m_w_o, m_ln1_g, m_ln1_b, m_w_up, m_w_down, m_ln2_g, m_ln2_b, v_lower_bounds, v_w_in, v_conv_w, v_hg_norm_w, v_w_mem_k, v_w_mem_v, v_w_branch, v_b_gate, v_w_o, v_ln1_g, v_ln1_b, v_w_up, v_w_down, v_ln2_g, v_ln2_b):
    bsz, seq, _ = x.shape
    t = bsz * seq
    me = _my_id()

    sh = dict(w_in=w_in.astype(BF16), w_mk=w_mem_k.astype(BF16), w_mv=w_mem_v.astype(BF16),
              w_br=w_branch.astype(BF16), w_o=w_o.astype(BF16), w_up=w_up.astype(BF16), w_down=w_down.astype(BF16))
    half_rows = D_MODEL // 2
    sh["w_in_a"], sh["w_in_b"] = sh["w_in"][:, :half_rows], sh["w_in"][:, half_rows:]
    natural = dict(w_in=_natural_cols, w_in_a=_natural_cols, w_in_b=_natural_cols, w_mk=_natural_rows,
                   w_mv=_natural_rows, w_br=_natural_cols, w_o=_natural_rows, w_up=_natural_cols,
                   w_down=_natural_rows)

    def gather_of(names, l):
        srcs = [sh[n][l] for n in names]
        return _Exchange(srcs, [_whole] * len(srcs), [s_.shape for s_ in srcs])

    def put(names, into):
        def hook(recv_):
            for n, r in zip(names, recv_):
                into[n] = natural[n](r)
        return hook

    lb_all = _lb_fwd(lower_bounds)
    layer_w = [dict(lb=lb_all[l][None], nw=hg_norm_w[l][None], b_gate=b_gate[l][None], ln1_g=ln1_g[l][None],
                    ln1_b=ln1_b[l][None], ln2_g=ln2_g[l][None], ln2_b=ln2_b[l][None]) for l in range(DEPTH)]
    first = ["w_in"]
    conv_shard = conv_w.reshape(DEPTH * CONV_K * (WIDTH // N_DEV) // 128, 128)
    ex0 = gather_of(first, 0)
    got = _exchange("gather_first", ex0.srcs + [conv_shard], [_whole] * (len(first) + 1),
                    [s_.shape for s_ in ex0.srcs] + [conv_shard.shape])
    put(first, layer_w[0])(got[:len(first)])
    conv_full = _natural_cols(got[-1].reshape(N_DEV, DEPTH, CONV_K, WIDTH // N_DEV))

    x2d = x.reshape(t, D_MODEL)
    mem2d = mem.reshape(bsz * MEM_LEN, D_MODEL)
    target2d = loss_target.reshape(t, D_MODEL)

    saved = []
    cur, cur_b = x2d, x2d.astype(BF16)
    for l in range(DEPTH):
        wl = layer_w[l]
        wl["conv"] = conv_full[l]
        if "w_in" not in wl:
            wl["w_in"] = jnp.concatenate([wl.pop("w_in_a"), wl.pop("w_in_b")], axis=0)
        now = ["w_up"] if l else ["w_up", "w_mk", "w_mv", "w_br", "w_o"]
        hosts = {"in_proj": (gather_of(now, l), put(now, wl)),
                 "hgrn_fwd": (gather_of(["w_down"], l), put(["w_down"], wl))}
        if l + 1 < DEPTH:
            nxt = layer_w[l + 1]
            hosts["merge_fwd"] = (gather_of(["w_in_a"], l + 1), put(["w_in_a"], nxt))
            hosts["wo_ln"] = (gather_of(["w_mk", "w_mv", "w_o"], l + 1), put(["w_mk", "w_mv", "w_o"], nxt))
            hosts["mlp_up"] = (gather_of(["w_in_b", "w_br"], l + 1), put(["w_in_b", "w_br"], nxt))
        s = _layer_fwd(cur, cur_b, mem2d, wl, bsz=bsz, seq=seq, hosts=hosts)
        saved.append(s)
        cur, cur_b = s["x2"], s["x2_b"]

    dcur, loss_tile = _loss_head(cur, target2d, tm=min(512, seq))
    loss = lax.psum(loss_tile[0, 0], ("x", "y", "c"))

    in_w = IN_COLS // N_DEV

    def in_half(r):
        return lambda ref, j: ref.at[pl.ds(r * half_rows, half_rows), pl.ds(j * in_w, in_w)]

    slicer = dict(w_in_a=in_half(0), w_in_b=in_half(1), w_mk=_rows(D_MODEL // N_DEV), w_mv=_rows(D_MODEL // N_DEV),
                  w_br=_cols(D_MODEL // N_DEV), w_o=_rows(D_MODEL // N_DEV), w_up=_cols(D_FF // N_DEV),
                  w_down=_rows(D_FF // N_DEV))
    source = dict(w_in_a="w_in", w_in_b="w_in")
    recv = [dict() for _ in range(DEPTH)]

    def scatter_of(names, g, into):
        ex = _Exchange([g[source.get(n, n)] for n in names], [slicer[n] for n in names],
                       [sh[n].shape[1:] for n in names])
        return ex, lambda recv_: into.update(zip(names, recv_))

    small_rows = [None] * DEPTH
    prev = None
    rest = ["w_in_b", "w_mk", "w_mv"]
    for l in reversed(range(DEPTH)):
        plan = {"grad_w_up": lambda g, l=l: scatter_of(["w_down"], g, recv[l]),
                "hgrn_bwd": lambda g, l=l: scatter_of(["w_up", "w_o", "w_br"], g, recv[l])}
        if l == 0:
            plan["in_proj_bwd"] = lambda g: scatter_of(["w_in_a"] + rest, g, recv[0])
        else:
            plan["in_proj_bwd"] = lambda g, l=l: scatter_of(["w_in_a"], g, recv[l])
        if prev is not None:
            plan["ln2_bwd_down"] = lambda g, l=l, prev=prev: scatter_of(rest, prev, recv[l + 1])
        dcur, prev, small_rows[l] = _layer_bwd(dcur, mem2d, saved[l], layer_w[l], bsz=bsz, seq=seq, plan=plan)
    for r in recv:
        r["w_in"] = jnp.concatenate([r.pop("w_in_a"), r.pop("w_in_b")], axis=1)

    packed = _pack_small(small_rows)
    all_small = _exchange("gather_small_grads", [packed], [_whole], [packed.shape])[0]
    small_grads = _unpack_small(_sum_devices("sum_small_grads", all_small))
    small_grads["lower_bounds"] = _lb_bwd(lower_bounds, small_grads["lower_bounds"])
    conv_all = small_grads["conv_w"].reshape(DEPTH, CONV_K, WIDTH)
    small_grads["conv_w"] = lax.dynamic_slice_in_dim(conv_all, me * (WIDTH // N_DEV), WIDTH // N_DEV, axis=2)

    grads, deltas, new_m, new_v = {}, {}, {}, {}
    given = dict(lower_bounds=(lower_bounds, m_lower_bounds, v_lower_bounds), conv_w=(conv_w, m_conv_w, v_conv_w),
                 hg_norm_w=(hg_norm_w, m_hg_norm_w, v_hg_norm_w), b_gate=(b_gate, m_b_gate, v_b_gate),
                 ln1_g=(ln1_g, m_ln1_g, v_ln1_g), ln1_b=(ln1_b, m_ln1_b, v_ln1_b),
                 ln2_g=(ln2_g, m_ln2_g, v_ln2_g), ln2_b=(ln2_b, m_ln2_b, v_ln2_b))
    for name, (w_, m_, v_) in given.items():
        g_ = small_grads[name].reshape(w_.shape)
        grads[name] = g_
        deltas[name], new_m[name], new_v[name] = _adam_small("adam_" + name, g_, w_, m_, v_)

    big = dict(w_in=("w_in", w_in, m_w_in, v_w_in, 128), w_mem_k=("w_mk", w_mem_k, m_w_mem_k, v_w_mem_k, 128),
               w_mem_v=("w_mv", w_mem_v, m_w_mem_v, v_w_mem_v, 128),
               w_branch=("w_br", w_branch, m_w_branch, v_w_branch, 512), w_o=("w_o", w_o, m_w_o, v_w_o, 128),
               w_up=("w_up", w_up, m_w_up, v_w_up, 256), w_down=("w_down", w_down, m_w_down, v_w_down, 128))
    for name, (k, w_, m_, v_, tr) in big.items():
        shape = w_.shape
        flat = (DEPTH, -1, shape[-1])
        rc = [recv[l][k].reshape((N_DEV,) + w_.reshape(flat).shape[1:]) for l in range(DEPTH)]
        outs = _adam_shard("adam_" + name, rc, w_.reshape(flat), m_.reshape(flat), v_.reshape(flat), tr=tr)
        grads[name], deltas[name], new_m[name], new_v[name] = (o.reshape(shape) for o in outs)

    order = ["lower_bounds", "w_in", "conv_w", "hg_norm_w", "w_mem_k", "w_mem_v", "w_branch", "b_gate", "w_o",
             "ln1_g", "ln1_b", "w_up", "w_down", "ln2_g", "ln2_b"]
    return (loss, dcur.reshape(x.shape), *[grads[n] for n in order], *[deltas[n] for n in order],
            *[new_m[n] for n in order], *[new_v[n] for n in order])
```

```python
import functools

import jax
import jax.numpy as jnp
from jax import lax
from jax.experimental import pallas as pl
from jax.experimental.pallas import tpu as pltpu

F32 = jnp.float32
BF16 = jnp.bfloat16

N_DEV = 8
D_MODEL = 1024
DEPTH = 4
MEM_LEN = 256
CONV_K = 3
WIDTH = 512
HEADS = 4
HEAD_DIM = 128
CHUNK = 32
D_FF = 4 * D_MODEL
IN_COLS = 7168
ALPHA = (2.0 * DEPTH) ** 0.25
LN_EPS = 1e-5
RMS_EPS = 1e-6
ADAM_LR = 0.001
ADAM_B1 = 0.9
ADAM_B2 = 0.999
ADAM_EPS = 1e-08
ADAM_WD = 0.01
ADAM_STEP = 10

C_CB, C_CC, C_CH, C_HQ, C_HF, C_HI, C_HG, C_MQ, C_GA = 0, 512, 1024, 1536, 2048, 2560, 3072, 3584, 4096

ROWS_HG = 256
NT_DIMS = (((1,), (1,)), ((), ()))
TN_DIMS = (((0,), (0,)), ((), ()))
MESH = pl.DeviceIdType.MESH


def _dot(a, b):
    return jnp.dot(a, b, preferred_element_type=F32)


def _dot_nt(a, b):
    return lax.dot_general(a, b, NT_DIMS, preferred_element_type=F32)


def _dot_tn(a, b):
    return lax.dot_general(a, b, TN_DIMS, preferred_element_type=F32)


def _sigmoid(x):
    return 1.0 / (1.0 + jnp.exp(-x))


def _params(*sem):
    return pltpu.CompilerParams(dimension_semantics=sem)


def _resident(shape):
    nd = len(shape)
    return pl.BlockSpec(shape, lambda *_: (0,) * nd)


def _my_id():
    return 4 * lax.axis_index("x") + 2 * lax.axis_index("y") + lax.axis_index("c")


class _Exchange:
    def __init__(self, srcs, slicers=None, piece_shapes=None, route="all"):
        self.srcs, self.n, self.route = list(srcs), len(srcs), route
        self.slicers = list(slicers) if slicers else [_whole] * self.n
        any_spec = pl.BlockSpec(memory_space=pl.ANY)
        self.in_specs = [any_spec] * self.n
        self.out_specs = [any_spec] * self.n
        if route == "relay":
            self.out_shape = [jax.ShapeDtypeStruct(a.shape, a.dtype) for a in srcs]
        else:
            self.out_shape = [jax.ShapeDtypeStruct((N_DEV,) + tuple(s), a.dtype) for s, a in zip(piece_shapes, srcs)]
        self.aliased = route == "relay"
        self.scratch = [pltpu.SemaphoreType.DMA((self.n * N_DEV,)), pltpu.SemaphoreType.DMA((self.n * N_DEV,)),
                        pltpu.SemaphoreType.DMA((self.n,))]

    def _peer(self, j, me):
        if self.route == "all":
            return j != me
        return (j != me) & ((j % 2 == lax.axis_index("c")) | (j // 2 == me // 2))

    def _remote(self, ins, outs, sems, k, j, me):
        return pltpu.make_async_remote_copy(
            src_ref=self.slicers[k](ins[k], j), dst_ref=outs[k].at[me],
            send_sem=sems[0].at[k * N_DEV + j], recv_sem=sems[1].at[k * N_DEV + me],
            device_id=(j // 4, (j // 2) % 2, j % 2), device_id_type=MESH)

    def _local(self, ins, outs, sems, k, j, me):
        return pltpu.make_async_copy(self.slicers[k](ins[k], j), outs[k].at[me], sems[2].at[k])

    def _relay(self, outs, sems, k, j):
        sibling = (lax.axis_index("x"), lax.axis_index("y"), 1 - lax.axis_index("c"))
        return pltpu.make_async_remote_copy(
            src_ref=outs[k].at[j], dst_ref=outs[k].at[j], send_sem=sems[0].at[k * N_DEV + j],
            recv_sem=sems[1].at[k * N_DEV + j], device_id=sibling, device_id_type=MESH)

    def _other_chip(self, j, same_core):
        on_my_core = j % 2 == lax.axis_index("c")
        return (on_my_core if same_core else ~on_my_core) & (j // 2 != _my_id() // 2)

    def start(self, ins, outs, sems):
        me = _my_id()
        for k in range(self.n):
            for j in range(N_DEV):
                if self.route == "relay":
                    @pl.when(self._other_chip(j, True))
                    def _():
                        self._relay(outs, sems, k, j).start()
                    continue

                @pl.when(self._peer(j, me))
                def _():
                    self._remote(ins, outs, sems, k, j, me).start()

                @pl.when(j == me)
                def _():
                    self._local(ins, outs, sems, k, j, me).start()

    def wait(self, ins, outs, sems):
        me = _my_id()
        for k in range(self.n):
            for j in range(N_DEV):
                if self.route == "relay":
                    @pl.when(self._other_chip(j, False))
                    def _():
                        self._relay(outs, sems, k, j).wait_recv()

                    @pl.when(self._other_chip(j, True))
                    def _():
                        self._relay(outs, sems, k, j).wait_send()
                    continue

                @pl.when(self._peer(j, me))
                def _():
                    pltpu.make_async_remote_copy(
                        src_ref=self.slicers[k](ins[k], j), dst_ref=outs[k].at[j],
                        send_sem=sems[0].at[k * N_DEV + j], recv_sem=sems[1].at[k * N_DEV + j],
                        device_id=(j // 4, (j // 2) % 2, j % 2), device_id_type=MESH).wait_recv()
                    self._remote(ins, outs, sems, k, j, me).wait_send()

                @pl.when(j == me)
                def _():
                    self._local(ins, outs, sems, k, j, me).wait()


def _carried(exchanges, n_in, n_out):
    c_in = [s for ex in exchanges for s in ex.in_specs]
    c_out = [s for ex in exchanges for s in ex.out_specs]
    shapes = [s for ex in exchanges for s in ex.out_shape]
    sems = [s for ex in exchanges for s in ex.scratch]
    srcs = [a for ex in exchanges for a in ex.srcs]
    aliases, off = {}, 0
    for ex in exchanges:
        if ex.aliased:
            aliases.update({n_in + off + k: n_out + off + k for k in range(ex.n)})
        off += ex.n
    total = off

    def split(refs, n_scr):
        ins, outs = refs[:n_in], refs[n_in + total:n_in + total + n_out]
        rest = refs[n_in + 2 * total + n_out:]
        scr, sem_refs = rest[:n_scr], rest[n_scr:]
        parts, off_ = [], 0
        for i, ex in enumerate(exchanges):
            parts.append((refs[n_in + off_:n_in + off_ + ex.n],
                          refs[n_in + total + n_out + off_:n_in + total + n_out + off_ + ex.n],
                          sem_refs[3 * i:3 * i + 3]))
            off_ += ex.n
        return ins, outs, scr, parts

    return c_in, c_out, shapes, sems, srcs, aliases, split


def _exchange(name, exchanges):
    c_in, c_out, shapes, sems, srcs, aliases, split = _carried(exchanges, 0, 0)

    def body(*refs):
        _, _, _, parts = split(refs, 0)
        for ex, part in zip(exchanges, parts):
            ex.start(*part)
        for ex, part in zip(exchanges, parts):
            ex.wait(*part)

    outs = pl.pallas_call(
        body, name=name, in_specs=c_in, out_specs=c_out, out_shape=shapes, scratch_shapes=sems,
        input_output_aliases=aliases, compiler_params=pltpu.CompilerParams(has_side_effects=True))(*srcs)
    return _per_exchange(exchanges, outs)


def _per_exchange(exchanges, flat):
    out, off = [], 0
    for ex in exchanges:
        out.append(flat[off:off + ex.n])
        off += ex.n
    return out


def _call(body, name, grid, in_specs, out_specs, out_shape, args, scratch=(), sem=None, carry=None):
    n_in, n_out, n_scr = len(in_specs), len(out_specs), len(scratch)
    if not carry:
        outs = pl.pallas_call(body, name=name, grid=grid, in_specs=in_specs, out_specs=out_specs,
                              out_shape=out_shape, scratch_shapes=list(scratch),
                              compiler_params=_params(*sem))(*args)
        return outs, None
    c_in, c_out, shapes, sems, srcs, aliases, split = _carried(carry, n_in, n_out)

    def hosted(*refs):
        ins, outs, scr, parts = split(refs, n_scr)
        first, last = True, True
        for d, size in enumerate(grid):
            first = first & (pl.program_id(d) == 0)
            last = last & (pl.program_id(d) == size - 1)

        @pl.when(first)
        def _():
            for ex, part in zip(carry, parts):
                ex.start(*part)

        body(*ins, *outs, *scr)

        @pl.when(last)
        def _():
            for ex, part in zip(carry, parts):
                ex.wait(*part)

    outs = pl.pallas_call(
        hosted, name=name + "_x", grid=grid, in_specs=list(in_specs) + c_in,
        out_specs=list(out_specs) + c_out, out_shape=list(out_shape) + shapes,
        scratch_shapes=list(scratch) + sems, input_output_aliases=aliases,
        compiler_params=_params(*(["arbitrary"] * len(grid))))(*args, *srcs)
    return outs[:n_out], _per_exchange(carry, outs[n_out:])


def _whole(ref, j):
    return ref


def _cols(width):
    return lambda ref, j: ref.at[(slice(None),) * (len(ref.shape) - 1) + (pl.ds(j * width, width),)]


def _rows(height):
    return lambda ref, j: ref.at[pl.ds(j * height, height)]


def _mm_nn(name, a, w, *, tm, tn, out_dtype, relu2=False, carry=None):
    t, k = a.shape
    n = w.shape[1]

    def body(a_ref, w_ref, o_ref):
        acc = _dot(a_ref[...].astype(BF16), w_ref[...])
        if relu2:
            r = jnp.maximum(acc, 0.0)
            acc = r * r
        o_ref[...] = acc.astype(out_dtype)

    outs, recv = _call(
        body, name, (t // tm, n // tn),
        [pl.BlockSpec((tm, k), lambda i, j: (i, 0)), pl.BlockSpec((k, tn), lambda i, j: (0, j))],
        [pl.BlockSpec((tm, tn), lambda i, j: (i, j))], [jax.ShapeDtypeStruct((t, n), out_dtype)], (a, w),
        sem=("parallel", "parallel"), carry=carry)
    return outs[0] if carry is None else (outs[0], recv)


def _in_proj(a, w, *, tm, carry=None):
    t, k = a.shape
    tn = IN_COLS // 4
    f_tile, f_off = C_HF // tn, C_HF % tn

    def body(a_ref, w_ref, o_ref, f_ref):
        acc = _dot(a_ref[...], w_ref[...])
        o_ref[...] = acc.astype(BF16)

        @pl.when(pl.program_id(1) == f_tile)
        def _():
            f_ref[...] = acc[:, f_off:f_off + WIDTH]

    outs, recv = _call(
        body, "in_proj", (t // tm, IN_COLS // tn),
        [pl.BlockSpec((tm, k), lambda i, j: (i, 0)), pl.BlockSpec((k, tn), lambda i, j: (0, j))],
        [pl.BlockSpec((tm, tn), lambda i, j: (i, j)), pl.BlockSpec((tm, WIDTH), lambda i, j: (i, 0))],
        [jax.ShapeDtypeStruct((t, IN_COLS), BF16), jax.ShapeDtypeStruct((t, WIDTH), F32)], (a, w),
        sem=("parallel", "arbitrary"), carry=carry)
    return outs if carry is None else (outs, recv)


def _linear_ln(name, a, w, resid, g, b, *, tm, carry=None):
    t, k = a.shape
    halves = [slice(0, tm // 2), slice(tm // 2, tm)]

    def body(a_ref, w_ref, r_ref, g_ref, b_ref, z_ref, x_ref, xb_ref):
        z = ALPHA * _Lanes(r_ref[s, :] for s in halves) + _ldot(_Lanes(a_ref[s, :] for s in halves), w_ref[...])
        zc = z - _mean(z, axis=-1, keepdims=True)
        y = zc * _rsqrt(_mean(zc * zc, axis=-1, keepdims=True) + LN_EPS) * g_ref[...] + b_ref[...]
        for s, zz, yy in zip(halves, z.xs, y.xs):
            z_ref[s, :] = zz
            x_ref[s, :] = yy
            xb_ref[s, :] = yy.astype(BF16)

    row = pl.BlockSpec((tm, D_MODEL), lambda i: (i, 0))
    outs, recv = _call(
        body, name, (t // tm,),
        [pl.BlockSpec((tm, k), lambda i: (i, 0)), _resident((k, D_MODEL)), row,
         _resident((1, D_MODEL)), _resident((1, D_MODEL))],
        [row, row, row],
        [jax.ShapeDtypeStruct((t, D_MODEL), F32)] * 2 + [jax.ShapeDtypeStruct((t, D_MODEL), BF16)],
        (a, w, resid, g, b), sem=("parallel",), carry=carry)
    return outs if carry is None else (outs, recv)


def _ln_bwd_mm_nt(name, dy, z, g, w, h=None, *, tm, tn, carry=None):
    t = dy.shape[0]
    n = w.shape[0]
    halves = [slice(0, tm // 2), slice(tm // 2, tm)]

    def body(*refs):
        if h is None:
            dy_ref, z_ref, g_ref, w_ref, dz_ref, dzb_ref, o_ref, dg_ref, db_ref = refs
        else:
            dy_ref, z_ref, g_ref, w_ref, h_ref, dz_ref, dzb_ref, o_ref, dg_ref, db_ref = refs

        @pl.when(pl.program_id(0) == 0)
        def _():
            dg_ref[...] = jnp.zeros_like(dg_ref)
            db_ref[...] = jnp.zeros_like(db_ref)

        zv = _Lanes(z_ref[s, :] for s in halves)
        dyv = _Lanes(dy_ref[s, :] for s in halves)
        mu = _mean(zv, axis=-1, keepdims=True)
        zc = zv - mu
        rstd = _rsqrt(_mean(zc * zc, axis=-1, keepdims=True) + LN_EPS)
        xh = zc * rstd
        gdy = dyv * g_ref[...]
        m1 = _mean(gdy, axis=-1, keepdims=True)
        m2 = _mean(gdy * xh, axis=-1, keepdims=True)
        dz = rstd * (gdy - m1 - xh * m2)
        dz_b = dz.astype(BF16)
        for s, a, a_b in zip(halves, dz.xs, dz_b.xs):
            dz_ref[s, :] = a
            dzb_ref[s, :] = a_b
        dg_ref[...] += _sum(dyv * xh, axis=0, keepdims=True).total()
        db_ref[...] += _sum(dyv, axis=0, keepdims=True).total()
        for c in range(n // tn):
            cols = slice(c * tn, (c + 1) * tn)
            acc = _ldot_nt(dz_b, w_ref[cols, :])
            if h is not None:
                acc = acc * (2.0 * _sqrt(_Lanes(h_ref[s, cols] for s in halves).astype(F32)))
            for s, a in zip(halves, acc.xs):
                o_ref[s, cols] = a.astype(BF16)

    row = pl.BlockSpec((tm, D_MODEL), lambda i: (i, 0))
    vec = _resident((1, D_MODEL))
    tile = pl.BlockSpec((tm, n), lambda i: (i, 0))
    in_specs = [row, row, vec, _resident((n, D_MODEL))]
    args = [dy, z, g, w]
    if h is not None:
        in_specs.append(tile)
        args.append(h)
    outs, recv = _call(
        body, name, (t // tm,), in_specs, [row, row, tile, vec, vec],
        [jax.ShapeDtypeStruct((t, D_MODEL), F32), jax.ShapeDtypeStruct((t, D_MODEL), BF16),
         jax.ShapeDtypeStruct((t, n), BF16), jax.ShapeDtypeStruct((1, D_MODEL), F32),
         jax.ShapeDtypeStruct((1, D_MODEL), F32)], args, sem=("arbitrary",), carry=carry)
    return outs if carry is None else (outs, recv)


def _mm_tn(name, a, b, *, tk, tmo, tno, carry=None):
    t, m = a.shape
    n = b.shape[1]
    nk = t // tk

    def body(a_ref, b_ref, o_ref, acc_ref):
        k = pl.program_id(2)
        p = _dot_tn(a_ref[...].astype(BF16), b_ref[...].astype(BF16))

        @pl.when(k == 0)
        def _():
            acc_ref[...] = p

        @pl.when(k > 0)
        def _():
            acc_ref[...] += p

        @pl.when(k == nk - 1)
        def _():
            o_ref[...] = acc_ref[...].astype(BF16)

    outs, recv = _call(
        body, name, (m // tmo, n // tno, nk),
        [pl.BlockSpec((tk, tmo), lambda i, j, k: (k, i)), pl.BlockSpec((tk, tno), lambda i, j, k: (k, j))],
        [pl.BlockSpec((tmo, tno), lambda i, j, k: (i, j))], [jax.ShapeDtypeStruct((m, n), BF16)], (a, b),
        scratch=[pltpu.VMEM((tmo, tno), F32)], sem=("parallel", "parallel", "arbitrary"), carry=carry)
    return outs[0] if carry is None else (outs[0], recv)


def _mm_nt_sum(name, pieces, offsets, w, resid, *, tm, carry=None):
    t = resid.shape[0]
    widths = [p.shape[1] for p in pieces]
    n_p = len(pieces)

    def body(*refs):
        p_refs, w_ref, r_ref, o_ref = refs[:n_p], refs[n_p], refs[n_p + 1], refs[n_p + 2]
        acc = ALPHA * r_ref[...]
        for p_ref, off, wd in zip(p_refs, offsets, widths):
            acc = acc + _dot_nt(p_ref[...], w_ref[:, off:off + wd])
        o_ref[...] = acc

    row = pl.BlockSpec((tm, D_MODEL), lambda i: (i, 0))
    outs, recv = _call(
        body, name, (t // tm,),
        [pl.BlockSpec((tm, wd), lambda i: (i, 0)) for wd in widths] + [_resident(w.shape), row],
        [row], [jax.ShapeDtypeStruct((t, D_MODEL), F32)], (*pieces, w, resid), sem=("parallel",), carry=carry)
    return outs[0] if carry is None else (outs[0], recv)


def _chunk_mask(rows):
    r = lax.broadcasted_iota(jnp.int32, (rows, rows), 0)
    c = lax.broadcasted_iota(jnp.int32, (rows, rows), 1)
    return ((r // CHUNK) == (c // CHUNK)) & (c <= r)


class _Lanes:
    def __init__(self, xs):
        self.xs = list(xs)

    def _with(self, other, f):
        if isinstance(other, _Lanes):
            return _Lanes([f(a, b) for a, b in zip(self.xs, other.xs)])
        return _Lanes([f(a, other) for a in self.xs])

    def __add__(self, o):
        return self._with(o, lambda a, b: a + b)

    def __radd__(self, o):
        return self._with(o, lambda a, b: b + a)

    def __sub__(self, o):
        return self._with(o, lambda a, b: a - b)

    def __rsub__(self, o):
        return self._with(o, lambda a, b: b - a)

    def __mul__(self, o):
        return self._with(o, lambda a, b: a * b)

    def __rmul__(self, o):
        return self._with(o, lambda a, b: b * a)

    def __truediv__(self, o):
        return self._with(o, lambda a, b: a / b)

    def __rtruediv__(self, o):
        return self._with(o, lambda a, b: b / a)

    def __neg__(self):
        return _Lanes([-a for a in self.xs])

    def __ge__(self, o):
        return self._with(o, lambda a, b: a >= b)

    def __getitem__(self, idx):
        return _Lanes([a[idx] for a in self.xs])

    def astype(self, dtype):
        return _Lanes([a.astype(dtype) for a in self.xs])

    def total(self):
        return functools.reduce(lambda a, b: a + b, self.xs)


def _lift(f):
    def g(*args, **kw):
        lanes = [a for a in args if isinstance(a, _Lanes)]
        if not lanes:
            return f(*args, **kw)
        return _Lanes([f(*[a.xs[i] if isinstance(a, _Lanes) else a for a in args], **kw)
                       for i in range(len(lanes[0].xs))])
    return g


def _concat(parts, axis):
    if isinstance(parts[0], _Lanes):
        return _Lanes([jnp.concatenate([p.xs[i] for p in parts], axis=axis) for i in range(len(parts[0].xs))])
    return jnp.concatenate(parts, axis=axis)


_exp, _log, _abs, _sqrt, _where = _lift(jnp.exp), _lift(jnp.log), _lift(jnp.abs), _lift(jnp.sqrt), _lift(jnp.where)
_sum, _mean, _rsqrt, _bcast = _lift(jnp.sum), _lift(jnp.mean), _lift(lax.rsqrt), _lift(jnp.broadcast_to)
_ldot, _ldot_nt, _ldot_tn = _lift(_dot), _lift(_dot_nt), _lift(_dot_tn)
_lsigmoid = _lift(_sigmoid)


def _mask_sum(mask_b, x, transpose=False):
    f = _ldot_tn if transpose else _ldot
    hi = x.astype(BF16)
    lo = (x - hi.astype(F32)).astype(BF16)
    return f(mask_b, hi) + f(mask_b, lo)


def _chunk_row(x, pos, rows):
    nc = rows // CHUNK

    def one(a):
        a3 = a.reshape(nc, CHUNK, HEAD_DIM)
        return jnp.broadcast_to(a3[:, pos:pos + 1, :], (nc, CHUNK, HEAD_DIM)).reshape(rows, HEAD_DIM)

    return _lift(one)(x)


def _chunk_total(x, rows):
    nc = rows // CHUNK

    def one(a):
        tot = jnp.sum(a.reshape(nc, CHUNK, HEAD_DIM), axis=1, keepdims=True)
        return jnp.broadcast_to(tot, (nc, CHUNK, HEAD_DIM)).reshape(rows, HEAD_DIM)

    return _lift(one)(x)


def _sigmoid_pair(x):
    e = _exp(-_abs(x))
    big = 1.0 / (1.0 + e)
    small = e * big
    pos = x >= 0.0
    return _where(pos, big, small), _where(pos, small, big)


def _hg_gates(q_raw, fl, lb, rows, mask):
    tri = mask.astype(BF16)
    sg, sg_neg = _sigmoid_pair(fl)
    forget = lb + (1.0 - lb) * sg
    k = (1.0 - lb) * sg_neg
    sq = _lsigmoid(q_raw)
    qs = q_raw * sq
    bc = _mask_sum(tri, _log(forget))
    bref = _chunk_row(bc, CHUNK // 2 - 1, rows)
    blast = _chunk_row(bc, CHUNK - 1, rows)
    return dict(tri=tri, sg=sg, sg_neg=sg_neg, forget=forget, k=k, sq=sq, qs=qs,
                e_a=_exp(bc - bref), e_b=_exp(bref - bc), e_q=_exp(bc), e_k=_exp(blast - bc),
                dec=_exp(blast))


HG_GROUP = 4


def _hg_lanes(bsz):
    return [(hh, slice(hh * HEAD_DIM, (hh + 1) * HEAD_DIM), b) for hh in range(HG_GROUP) for b in range(bsz)]


def _hg_read(ref, lanes):
    return _Lanes(ref[b, :, cs].astype(F32) for _, cs, b in lanes)


def _hg_write(ref, lanes, val, offset=0):
    for (_, cs, b), a in zip(lanes, val.xs):
        ref[b, :, offset + cs.start:offset + cs.stop] = a


def _hgrn_fwd(proj, hf, lb, nw, *, bsz, seq, carry=None):
    rows = min(ROWS_HG, seq)
    nt = seq // rows
    nc = rows // CHUNK
    t = bsz * seq

    lanes = _hg_lanes(bsz)

    def body(q_ref, f_ref, v_ref, g_ref, lb_ref, nw_ref, y_ref, o_ref, st_ref, s_scr):
        @pl.when(pl.program_id(1) == 0)
        def _():
            s_scr[...] = jnp.zeros_like(s_scr)

        mask = _chunk_mask(rows)
        lb_v = _Lanes(lb_ref[:, cs] for _, cs, _ in lanes)
        gt = _hg_gates(_hg_read(q_ref, lanes), _hg_read(f_ref, lanes), lb_v, rows, mask)
        v_b = _hg_read(v_ref, lanes).astype(BF16)
        a_b = (gt["qs"] * gt["e_a"]).astype(BF16)
        b_b = (gt["k"] * gt["e_b"]).astype(BF16)
        qi_b = (gt["qs"] * gt["e_q"]).astype(BF16)
        ko_b = (gt["k"] * gt["e_k"]).astype(BF16)
        scores = _where(mask, _ldot_nt(a_b, b_b), 0.0)
        o_intra = _ldot(scores.astype(BF16), v_b)

        s = _Lanes(s_scr[i] for i in range(len(lanes)))
        parts = []
        for n in range(nc):
            sl = slice(n * CHUNK, (n + 1) * CHUNK)
            s_b = s.astype(BF16)
            for (hh, _, b), a in zip(lanes, s_b.xs):
                st_ref[hh, b, n] = a
            parts.append(_ldot_nt(qi_b[sl], s_b))
            s = s * gt["dec"][n * CHUNK:n * CHUNK + 1] + _ldot_tn(v_b[sl], ko_b[sl])
        for i, a in enumerate(s.xs):
            s_scr[i] = a
        o = o_intra + _concat(parts, 0)
        _hg_write(o_ref, lanes, o)
        r = _rsqrt(_mean(o * o, axis=-1, keepdims=True) + RMS_EPS)
        g = _hg_read(g_ref, lanes)
        _hg_write(y_ref, lanes, (o * r * nw_ref[...] * (g * _lsigmoid(g))).astype(BF16))

    wide = HG_GROUP * HEAD_DIM

    def col(base):
        return pl.BlockSpec((bsz, rows, wide), lambda h, j: (0, j, base // wide + h))

    out_tile = pl.BlockSpec((bsz, rows, wide), lambda h, j: (0, j, h))
    p3 = proj.reshape(bsz, seq, IN_COLS)
    outs, recv = _call(
        body, "hgrn_fwd", (HEADS // HG_GROUP, nt),
        [col(C_HQ), out_tile, col(C_HI), col(C_HG),
         pl.BlockSpec((1, wide), lambda h, j: (0, h)), _resident((1, HEAD_DIM))],
        [out_tile, out_tile,
         pl.BlockSpec((HG_GROUP, bsz, nc, HEAD_DIM, HEAD_DIM), lambda h, j: (h, 0, j, 0, 0))],
        [jax.ShapeDtypeStruct((bsz, seq, WIDTH), BF16), jax.ShapeDtypeStruct((bsz, seq, WIDTH), F32),
         jax.ShapeDtypeStruct((HEADS, bsz, seq // CHUNK, HEAD_DIM, HEAD_DIM), BF16)],
        (p3, hf.reshape(bsz, seq, WIDTH), p3, p3, lb, nw),
        scratch=[pltpu.VMEM((len(lanes), HEAD_DIM, HEAD_DIM), F32)],
        sem=("parallel", "arbitrary"), carry=carry)
    outs = [outs[0].reshape(t, WIDTH), outs[1].reshape(t, WIDTH), outs[2]]
    return outs if carry is None else (outs, recv)


def _hgrn_bwd(proj, hf, lb, nw, o_pre, states, dy, *, bsz, seq, carry=None):
    rows = min(ROWS_HG, seq)
    nt = seq // rows
    nc = rows // CHUNK
    t = bsz * seq
    lanes = _hg_lanes(bsz)

    def body(q_ref, f_ref, v_ref, g_ref, lb_ref, nw_ref, o_ref, st_ref, dy_ref, dh_ref, dlb_ref, dnw_ref, ds_scr):
        h, j = pl.program_id(0), pl.program_id(1)

        @pl.when(j == 0)
        def _():
            ds_scr[...] = jnp.zeros_like(ds_scr)
            dlb_ref[...] = jnp.zeros_like(dlb_ref)

        @pl.when((h == 0) & (j == 0))
        def _():
            dnw_ref[...] = jnp.zeros_like(dnw_ref)

        mask = _chunk_mask(rows)
        q_raw = _hg_read(q_ref, lanes)
        lb_v = _Lanes(lb_ref[:, cs] for _, cs, _ in lanes)
        gt = _hg_gates(q_raw, _hg_read(f_ref, lanes), lb_v, rows, mask)
        v_b = _hg_read(v_ref, lanes).astype(BF16)
        a_f = gt["qs"] * gt["e_a"]
        b_f = gt["k"] * gt["e_b"]
        qi_f = gt["qs"] * gt["e_q"]
        ko_f = gt["k"] * gt["e_k"]
        a_b, b_b, qi_b, ko_b = a_f.astype(BF16), b_f.astype(BF16), qi_f.astype(BF16), ko_f.astype(BF16)

        o = _hg_read(o_ref, lanes)
        nw_v = nw_ref[...]
        g = _hg_read(g_ref, lanes)
        dyv = _hg_read(dy_ref, lanes)
        r = _rsqrt(_mean(o * o, axis=-1, keepdims=True) + RMS_EPS)
        sgg = _lsigmoid(g)
        d_g = dyv * (o * r * nw_v) * (sgg * (1.0 + g * (1.0 - sgg)))
        d_on = dyv * (g * sgg)
        dnw_ref[...] += _sum(d_on * o * r, axis=0, keepdims=True).total()
        tt = d_on * nw_v
        d_o = r * tt - o * (r * r * r) * _mean(tt * o, axis=-1, keepdims=True)
        do_b = d_o.astype(BF16)

        sc_b = _where(mask, _ldot_nt(a_b, b_b), 0.0).astype(BF16)
        dsc_b = _where(mask, _ldot_nt(do_b, v_b), 0.0).astype(BF16)
        d_v = _ldot_tn(sc_b, do_b)
        d_a = _ldot(dsc_b, b_b)
        d_bm = _ldot_tn(dsc_b, a_b)

        ds = _Lanes(ds_scr[i] for i in range(len(lanes)))
        dqi_parts, dko_parts, dvi_parts, ddec_parts = [None] * nc, [None] * nc, [None] * nc, [None] * nc
        for n in reversed(range(nc)):
            sl = slice(n * CHUNK, (n + 1) * CHUNK)
            dec_n = gt["dec"][n * CHUNK:n * CHUNK + 1]
            ds_b = ds.astype(BF16)
            s_n = _Lanes(st_ref[hh, b, n] for hh, _, b in lanes)
            dqi_parts[n] = _ldot(do_b[sl], s_n)
            dko_parts[n] = _ldot(v_b[sl], ds_b)
            dvi_parts[n] = _ldot_nt(ko_b[sl], ds_b)
            d_dec = _sum(ds * s_n.astype(F32), axis=0, keepdims=True)
            ddec_parts[n] = _bcast(d_dec * dec_n, (CHUNK, HEAD_DIM))
            ds = ds * dec_n + _ldot_tn(do_b[sl], qi_b[sl])
        for i, a in enumerate(ds.xs):
            ds_scr[i] = a
        d_qi = _concat(dqi_parts, 0)
        d_ko = _concat(dko_parts, 0)
        d_v = d_v + _concat(dvi_parts, 0)

        d_qs = d_a * gt["e_a"] + d_qi * gt["e_q"]
        d_k = d_bm * gt["e_b"] + d_ko * gt["e_k"]
        t_a, t_b, t_q, t_k = d_a * a_f, d_bm * b_f, d_qi * qi_f, d_ko * ko_f
        d_bref = _chunk_total(t_b - t_a, rows)
        d_blast = _chunk_total(t_k, rows) + _concat(ddec_parts, 0)
        pos = lax.broadcasted_iota(jnp.int32, (rows, HEAD_DIM), 0) % CHUNK
        d_bc = (t_a - t_b + t_q - t_k + _where(pos == CHUNK // 2 - 1, d_bref, 0.0)
                + _where(pos == CHUNK - 1, d_blast, 0.0))
        d_logf = _mask_sum(gt["tri"], d_bc, transpose=True)

        sg, sg_neg = gt["sg"], gt["sg_neg"]
        inv_f = 1.0 / gt["forget"]
        common = (1.0 - lb_v) * sg * sg_neg
        d_fl = common * (d_logf * inv_f - d_k)
        d_lb = _sum(sg_neg * (d_logf * inv_f - d_k), axis=0, keepdims=True)
        for (_, cs, _), a in zip(lanes, d_lb.xs):
            dlb_ref[:, cs] += a
        sq = gt["sq"]
        _hg_write(dh_ref, lanes, (d_qs * (sq * (1.0 + q_raw * (1.0 - sq)))).astype(BF16), 0)
        _hg_write(dh_ref, lanes, d_fl.astype(BF16), WIDTH)
        _hg_write(dh_ref, lanes, d_v.astype(BF16), 2 * WIDTH)
        _hg_write(dh_ref, lanes, d_g.astype(BF16), 3 * WIDTH)

    assert HG_GROUP == HEADS, "the combined gradient block needs all heads in one grid step"
    wide = HG_GROUP * HEAD_DIM

    def col(base):
        return pl.BlockSpec((bsz, rows, wide), lambda h, j: (0, nt - 1 - j, base // wide + h))

    tile = pl.BlockSpec((bsz, rows, wide), lambda h, j: (0, nt - 1 - j, h))
    head_vec = pl.BlockSpec((1, wide), lambda h, j: (0, h))
    p3 = proj.reshape(bsz, seq, IN_COLS)
    outs, recv = _call(
        body, "hgrn_bwd", (HEADS // HG_GROUP, nt),
        [col(C_HQ), tile, col(C_HI), col(C_HG), head_vec, _resident((1, HEAD_DIM)), tile,
         pl.BlockSpec((HG_GROUP, bsz, nc, HEAD_DIM, HEAD_DIM), lambda h, j: (h, 0, nt - 1 - j, 0, 0)), tile],
        [pl.BlockSpec((bsz, rows, 4 * WIDTH), lambda h, j: (0, nt - 1 - j, 0)), head_vec, _resident((1, HEAD_DIM))],
        [jax.ShapeDtypeStruct((bsz, seq, 4 * WIDTH), BF16), jax.ShapeDtypeStruct((1, WIDTH), F32),
         jax.ShapeDtypeStruct((1, HEAD_DIM), F32)],
        (p3, hf.reshape(bsz, seq, WIDTH), p3, p3, lb, nw, o_pre.reshape(bsz, seq, WIDTH), states,
         dy.reshape(bsz, seq, WIDTH)),
        scratch=[pltpu.VMEM((len(lanes), HEAD_DIM, HEAD_DIM), F32)],
        sem=("arbitrary", "arbitrary"), carry=carry)
    outs = [outs[0].reshape(t, 4 * WIDTH), outs[1], outs[2]]
    return outs if carry is None else (outs, recv)


def _mem_kv(mem2d, w_k, w_v):
    rows = mem2d.shape[0]

    def body(m_ref, wk_ref, wv_ref, k_ref, v_ref):
        m_b = m_ref[...].astype(BF16)
        k_ref[...] = _dot(m_b, wk_ref[...]).astype(BF16)
        v_ref[...] = _dot(m_b, wv_ref[...]).astype(BF16)

    return pl.pallas_call(
        body, name="mem_kv", grid=(rows // MEM_LEN,),
        in_specs=[pl.BlockSpec((MEM_LEN, D_MODEL), lambda i: (i, 0)), _resident((D_MODEL, WIDTH)),
                  _resident((D_MODEL, WIDTH))],
        out_specs=[pl.BlockSpec((MEM_LEN, WIDTH), lambda i: (i, 0))] * 2,
        out_shape=[jax.ShapeDtypeStruct((rows, WIDTH), BF16)] * 2,
        compiler_params=_params("parallel"),
    )(mem2d, w_k, w_v)


def _softmax_rows(s):
    m = _lift(jnp.max)(s, axis=-1, keepdims=True)
    e = _exp(s - m)
    return e / _sum(e, axis=-1, keepdims=True)


def _attn_fwd(proj, mk, mv, *, tm, seq):
    t = proj.shape[0]
    per_b = seq // tm
    scale = HEAD_DIM ** -0.5

    def body(q_ref, k_ref, v_ref, y_ref):
        heads = [slice(h * HEAD_DIM, (h + 1) * HEAD_DIM) for h in range(HEADS)]
        q_b = _Lanes(q_ref[:, sl] for sl in heads).astype(BF16)
        p = _softmax_rows(_ldot_nt(q_b, _Lanes(k_ref[:, sl] for sl in heads)) * scale)
        out = _ldot(p.astype(BF16), _Lanes(v_ref[:, sl] for sl in heads))
        y_ref[...] = jnp.concatenate(out.xs, axis=-1).astype(BF16)

    kv = pl.BlockSpec((MEM_LEN, WIDTH), lambda i: (i // per_b, 0))
    return pl.pallas_call(
        body, name="attn_fwd", grid=(t // tm,),
        in_specs=[pl.BlockSpec((tm, WIDTH), lambda i: (i, C_MQ // WIDTH)), kv, kv],
        out_specs=pl.BlockSpec((tm, WIDTH), lambda i: (i, 0)),
        out_shape=jax.ShapeDtypeStruct((t, WIDTH), BF16),
        compiler_params=_params("parallel"),
    )(proj, mk, mv)


def _attn_bwd(proj, mk, mv, dy, *, tm, seq):
    t = proj.shape[0]
    per_b = seq // tm
    scale = HEAD_DIM ** -0.5

    def body(q_ref, k_ref, v_ref, dy_ref, dq_ref, dk_ref, dv_ref):
        i = pl.program_id(0)

        @pl.when(i % per_b == 0)
        def _():
            dk_ref[...] = jnp.zeros_like(dk_ref)
            dv_ref[...] = jnp.zeros_like(dv_ref)

        heads = [slice(h * HEAD_DIM, (h + 1) * HEAD_DIM) for h in range(HEADS)]
        q_b = _Lanes(q_ref[:, sl] for sl in heads).astype(BF16)
        k_b, v_b = _Lanes(k_ref[:, sl] for sl in heads), _Lanes(v_ref[:, sl] for sl in heads)
        p = _softmax_rows(_ldot_nt(q_b, k_b) * scale)
        dy_b = _Lanes(dy_ref[:, sl] for sl in heads).astype(BF16)
        dp = _ldot_nt(dy_b, v_b)
        d_v = _ldot_tn(p.astype(BF16), dy_b)
        ds_b = (p * (dp - _sum(dp * p, axis=-1, keepdims=True)) * scale).astype(BF16)
        dq_ref[...] = jnp.concatenate(_ldot(ds_b, k_b).xs, axis=-1).astype(BF16)
        dk_ref[...] += jnp.concatenate(_ldot_tn(ds_b, q_b).xs, axis=-1)
        dv_ref[...] += jnp.concatenate(d_v.xs, axis=-1)

    kv = pl.BlockSpec((MEM_LEN, WIDTH), lambda i: (i // per_b, 0))
    tile = pl.BlockSpec((tm, WIDTH), lambda i: (i, 0))
    n_mem = mk.shape[0]
    return pl.pallas_call(
        body, name="attn_bwd", grid=(t // tm,),
        in_specs=[pl.BlockSpec((tm, WIDTH), lambda i: (i, C_MQ // WIDTH)), kv, kv, tile],
        out_specs=[tile, kv, kv],
        out_shape=[jax.ShapeDtypeStruct((t, WIDTH), BF16), jax.ShapeDtypeStruct((n_mem, WIDTH), F32),
                   jax.ShapeDtypeStruct((n_mem, WIDTH), F32)],
        compiler_params=_params("arbitrary"),
    )(proj, mk, mv, dy)


HALO = 16


def _shift_down(u, halo, k, row):
    out = pltpu.roll(u, k, 0)
    for m in range(k):
        out = jnp.where(row == m, halo[HALO - k + m:HALO - k + m + 1, :], out)
    return out


def _shift_up(u, halo, k, row, tm):
    out = pltpu.roll(u, tm - k, 0)
    for m in range(k):
        out = jnp.where(row == tm - k + m, halo[m:m + 1, :], out)
    return out


def _merge_fwd(proj, y_b, y_c, conv_w, w_branch, b_gate, *, tm, seq, carry=None):
    t = proj.shape[0]
    per_b = seq // tm
    hb = tm // HALO

    def body(cb_ref, cc_ref, ch_ref, cch_ref, chh_ref, ga_ref, gb_ref, gc_ref, yb_ref, yc_ref, cw_ref, wb_ref,
             bg_ref, ya_ref, pa_ref, pb_ref, pc_ref, mg_ref, sa_ref, sb_ref, sc_ref):
        i = pl.program_id(0)
        row = lax.broadcasted_iota(jnp.int32, (tm, WIDTH), 0)
        u = cc_ref[...].astype(F32) * ch_ref[...].astype(F32)
        halo = jnp.where(i % per_b == 0, 0.0, cch_ref[...].astype(F32) * chh_ref[...].astype(F32))
        cw = cw_ref[...]
        y = cw[0:1] * _shift_down(u, halo, 2, row) + cw[1:2] * _shift_down(u, halo, 1, row) + cw[2:3] * u
        ya_b = (cb_ref[...].astype(F32) * y).astype(BF16)
        ya_ref[...] = ya_b
        merged = None
        for idx, (y_in, g_ref, p_ref, s_ref) in enumerate(((ya_b, ga_ref, pa_ref, sa_ref),
                                                            (yb_ref[...], gb_ref, pb_ref, sb_ref),
                                                            (yc_ref[...], gc_ref, pc_ref, sc_ref))):
            p = _dot(y_in, wb_ref[idx])
            p_ref[...] = p.astype(BF16)
            sg = _sigmoid(g_ref[...].astype(F32) + bg_ref[:, idx * D_MODEL:(idx + 1) * D_MODEL])
            s_ref[...] = sg.astype(BF16)
            term = sg * p
            merged = term if merged is None else merged + term
        mg_ref[...] = merged.astype(BF16)

    def half(c):
        return pl.BlockSpec((tm, WIDTH), lambda i: (i, c // WIDTH))

    def prev(c):
        return pl.BlockSpec((HALO, WIDTH), lambda i: (jnp.maximum(i * hb - 1, 0), c // WIDTH))

    def gate(k):
        return pl.BlockSpec((tm, D_MODEL), lambda i: (i, C_GA // D_MODEL + k))

    tile512 = pl.BlockSpec((tm, WIDTH), lambda i: (i, 0))
    tile1k = pl.BlockSpec((tm, D_MODEL), lambda i: (i, 0))
    outs, recv = _call(
        body, "merge_fwd", (t // tm,),
        [half(C_CB), half(C_CC), half(C_CH), prev(C_CC), prev(C_CH), gate(0), gate(1), gate(2),
         tile512, tile512, _resident((CONV_K, WIDTH)), _resident((3, WIDTH, D_MODEL)), _resident((1, 3 * D_MODEL))],
        [tile512] + [tile1k] * 7,
        [jax.ShapeDtypeStruct((t, WIDTH), BF16)] + [jax.ShapeDtypeStruct((t, D_MODEL), BF16)] * 7,
        (proj, proj, proj, proj, proj, proj, proj, proj, y_b, y_c, conv_w, w_branch, b_gate),
        sem=("parallel",), carry=carry)
    return outs if carry is None else (outs, recv)


def _merge_bwd(dmerged, projections, gates, w_branch, *, tm):
    t = dmerged.shape[0]

    def body(dm_ref, pa_ref, pb_ref, pc_ref, sa_ref, sb_ref, sc_ref, wb_ref,
             dgt_ref, dpa_ref, dpb_ref, dpc_ref, dya_ref, dyb_ref, dyc_ref, dbg_ref):
        i = pl.program_id(0)

        @pl.when(i == 0)
        def _():
            dbg_ref[...] = jnp.zeros_like(dbg_ref)

        dm = dm_ref[...].astype(F32)
        for idx, (p_ref, s_ref, dp_ref, dy_ref) in enumerate(((pa_ref, sa_ref, dpa_ref, dya_ref),
                                                              (pb_ref, sb_ref, dpb_ref, dyb_ref),
                                                              (pc_ref, sc_ref, dpc_ref, dyc_ref))):
            cols = slice(idx * D_MODEL, (idx + 1) * D_MODEL)
            sg = s_ref[...].astype(F32)
            dp = dm * sg
            dp_b = dp.astype(BF16)
            dp_ref[...] = dp_b
            dgate = dp * p_ref[...].astype(F32) * (1.0 - sg)
            dgt_ref[:, cols] = dgate.astype(BF16)
            dbg_ref[:, cols] += jnp.sum(dgate, axis=0, keepdims=True)
            dy_ref[...] = _dot_nt(dp_b, wb_ref[idx]).astype(BF16)

    tile512 = pl.BlockSpec((tm, WIDTH), lambda i: (i, 0))
    tile1k = pl.BlockSpec((tm, D_MODEL), lambda i: (i, 0))
    return pl.pallas_call(
        body, name="merge_bwd", grid=(t // tm,),
        in_specs=[tile1k] * 7 + [_resident((3, WIDTH, D_MODEL))],
        out_specs=[pl.BlockSpec((tm, 3 * D_MODEL), lambda i: (i, 0)), tile1k, tile1k, tile1k,
                   tile512, tile512, tile512, _resident((1, 3 * D_MODEL))],
        out_shape=[jax.ShapeDtypeStruct((t, 3 * D_MODEL), BF16)] + [jax.ShapeDtypeStruct((t, D_MODEL), BF16)] * 3
                  + [jax.ShapeDtypeStruct((t, WIDTH), BF16)] * 3 + [jax.ShapeDtypeStruct((1, 3 * D_MODEL), F32)],
        compiler_params=_params("arbitrary"),
    )(dmerged, *projections, *gates, w_branch)


def _conv_bwd(proj, dya, conv_w, *, tm, seq):
    t = proj.shape[0]
    per_b = seq // tm
    hb = tm // HALO
    last_blk = t // HALO - 1

    def body(cb_ref, cc_ref, ch_ref, cch_ref, chh_ref, dya_ref, cbn_ref, dyan_ref, cw_ref, d_ref, dcw_ref):
        i = pl.program_id(0)

        @pl.when(i == 0)
        def _():
            dcw_ref[...] = jnp.zeros_like(dcw_ref)

        row = lax.broadcasted_iota(jnp.int32, (tm, WIDTH), 0)
        cb, cc, ch = cb_ref[...].astype(F32), cc_ref[...].astype(F32), ch_ref[...].astype(F32)
        u = cc * ch
        halo = jnp.where(i % per_b == 0, 0.0, cch_ref[...].astype(F32) * chh_ref[...].astype(F32))
        u1 = _shift_down(u, halo, 1, row)
        u2 = _shift_down(u, halo, 2, row)
        cw = cw_ref[...]
        y = cw[0:1] * u2 + cw[1:2] * u1 + cw[2:3] * u
        dya = dya_ref[...].astype(F32)
        dy = dya * cb
        nxt = jnp.where(i % per_b == per_b - 1, 0.0, dyan_ref[...].astype(F32) * cbn_ref[...].astype(F32))
        du = cw[2:3] * dy + cw[1:2] * _shift_up(dy, nxt, 1, row, tm) + cw[0:1] * _shift_up(dy, nxt, 2, row, tm)
        d_ref[:, 0:WIDTH] = (dya * y).astype(BF16)
        d_ref[:, WIDTH:2 * WIDTH] = (du * ch).astype(BF16)
        d_ref[:, 2 * WIDTH:3 * WIDTH] = (du * cc).astype(BF16)
        dcw_ref[0:1, :] += jnp.sum(dy * u2, axis=0, keepdims=True)
        dcw_ref[1:2, :] += jnp.sum(dy * u1, axis=0, keepdims=True)
        dcw_ref[2:3, :] += jnp.sum(dy * u, axis=0, keepdims=True)

    def half(c):
        return pl.BlockSpec((tm, WIDTH), lambda i: (i, c // WIDTH))

    def prev(c):
        return pl.BlockSpec((HALO, WIDTH), lambda i: (jnp.maximum(i * hb - 1, 0), c // WIDTH))

    def nxt(c):
        return pl.BlockSpec((HALO, WIDTH), lambda i: (jnp.minimum((i + 1) * hb, last_blk), c // WIDTH))

    return pl.pallas_call(
        body, name="conv_bwd", grid=(t // tm,),
        in_specs=[half(C_CB), half(C_CC), half(C_CH), prev(C_CC), prev(C_CH),
                  pl.BlockSpec((tm, WIDTH), lambda i: (i, 0)), nxt(C_CB), nxt(0), _resident((CONV_K, WIDTH))],
        out_specs=[pl.BlockSpec((tm, 3 * WIDTH), lambda i: (i, 0)), _resident((CONV_K, WIDTH))],
        out_shape=[jax.ShapeDtypeStruct((t, 3 * WIDTH), BF16), jax.ShapeDtypeStruct((CONV_K, WIDTH), F32)],
        compiler_params=_params("arbitrary"),
    )(proj, proj, proj, proj, proj, dya, proj, dya, conv_w)


def _loss_head(y, target, *, tm):
    t = y.shape[0]

    def body(y_ref, t_ref, dy_ref, l_ref):
        @pl.when(pl.program_id(0) == 0)
        def _():
            l_ref[...] = jnp.zeros_like(l_ref)

        err = y_ref[...] - t_ref[...]
        dy_ref[...] = err * (1.0 / D_MODEL)
        per_row = jnp.sum(err * err, axis=-1, keepdims=True) * (1.0 / D_MODEL)
        l_ref[...] += 0.5 * jnp.sum(per_row, axis=0, keepdims=True)

    row = pl.BlockSpec((tm, D_MODEL), lambda i: (i, 0))
    return pl.pallas_call(
        body, name="loss_head", grid=(t // tm,),
        in_specs=[row, row], out_specs=[row, _resident((8, 128))],
        out_shape=[jax.ShapeDtypeStruct((t, D_MODEL), F32), jax.ShapeDtypeStruct((8, 128), F32)],
        compiler_params=_params("arbitrary"),
    )(y, target)


def _lb_softmax(lower_bounds):
    x = lower_bounds
    e = jnp.exp(x - jnp.max(x, axis=0, keepdims=True))
    return e / jnp.sum(e, axis=0, keepdims=True)


def _lb_fwd(lower_bounds):
    def body(x_ref, o_ref):
        s = _lb_softmax(x_ref[...])
        c = s[0:1]
        o_ref[0:1, :] = c - s[0:1]
        for l in range(1, DEPTH):
            c = c + s[l:l + 1]
            o_ref[l:l + 1, :] = c - s[0:1]

    return pl.pallas_call(body, name="lb_fwd", out_shape=jax.ShapeDtypeStruct(lower_bounds.shape, F32))(lower_bounds)


def _lb_bwd(lower_bounds, d_lb_all):
    def body(x_ref, d_ref, o_ref):
        s = _lb_softmax(x_ref[...])
        d = d_ref[...]
        rows = [jnp.zeros_like(d[0:1])]
        for j in range(1, DEPTH):
            acc = d[j:j + 1]
            for l in range(j + 1, DEPTH):
                acc = acc + d[l:l + 1]
            rows.append(acc)
        inner = rows[0] * s[0:1]
        for j in range(1, DEPTH):
            inner = inner + rows[j] * s[j:j + 1]
        for j in range(DEPTH):
            o_ref[j:j + 1, :] = s[j:j + 1] * (rows[j] - inner)

    return pl.pallas_call(body, name="lb_bwd", out_shape=jax.ShapeDtypeStruct(lower_bounds.shape, F32))(
        lower_bounds, d_lb_all)


def _adamw(w, g, m, v):
    m2 = ADAM_B1 * m + (1.0 - ADAM_B1) * g
    v2 = ADAM_B2 * v + (1.0 - ADAM_B2) * (g * g)
    m_hat = m2 / (1.0 - ADAM_B1 ** ADAM_STEP)
    v_hat = v2 / (1.0 - ADAM_B2 ** ADAM_STEP)
    delta = -ADAM_LR * (m_hat / (jnp.sqrt(v_hat) + ADAM_EPS) + ADAM_WD * w)
    return delta, m2, v2


def _adam_small(name, g, w, m, v):
    shape = w.shape
    flat = (-1, shape[-1])
    g2, w2, m2, v2 = (a.reshape(flat) for a in (g, w, m, v))

    def body(g_ref, w_ref, m_ref, v_ref, d_ref, mo_ref, vo_ref):
        d, mm, vv = _adamw(w_ref[...], g_ref[...], m_ref[...], v_ref[...])
        d_ref[...] = d
        mo_ref[...] = mm
        vo_ref[...] = vv

    outs = pl.pallas_call(body, name=name, out_shape=[jax.ShapeDtypeStruct(w2.shape, F32)] * 3)(g2, w2, m2, v2)
    return [o.reshape(shape) for o in outs]


def _adam_shard(name, recvs, w, m, v, *, tr):
    _, r, c = w.shape

    def body(*refs):
        rc, (w_ref, m_ref, v_ref), (g_ref, d_ref, mo_ref, vo_ref) = refs[:DEPTH], refs[DEPTH:DEPTH + 3], refs[DEPTH + 3:]
        layer = pl.program_id(0)
        for cand in range(DEPTH):
            @pl.when(layer == cand)
            def _():
                g = rc[cand][0].astype(F32)
                for d in range(1, N_DEV):
                    g = g + rc[cand][d].astype(F32)
                dl, mm, vv = _adamw(w_ref[...], g, m_ref[...], v_ref[...])
                g_ref[...] = g
                d_ref[...] = dl
                mo_ref[...] = mm
                vo_ref[...] = vv

    def recv_spec(cand):
        return pl.BlockSpec((N_DEV, tr, c), lambda l, i: (0, jnp.where(l == cand, i, 0), 0))

    tile = pl.BlockSpec((None, tr, c), lambda l, i: (l, i, 0))
    return pl.pallas_call(
        body, name=name, grid=(DEPTH, r // tr),
        in_specs=[recv_spec(cand) for cand in range(DEPTH)] + [tile] * 3,
        out_specs=[tile] * 4,
        out_shape=[jax.ShapeDtypeStruct(w.shape, F32)] * 4,
        compiler_params=_params("parallel", "parallel"),
    )(*recvs, w, m, v)


def _sum_devices(name, x):
    def body(x_ref, o_ref):
        acc = x_ref[0]
        for d in range(1, N_DEV):
            acc = acc + x_ref[d]
        o_ref[...] = acc

    return pl.pallas_call(body, name=name, out_shape=jax.ShapeDtypeStruct(x.shape[1:], x.dtype))(x)


SMALL = (("lower_bounds", 1, 512), ("conv_w", CONV_K, WIDTH), ("hg_norm_w", 1, HEAD_DIM), ("b_gate", 3, D_MODEL),
         ("ln1_g", 1, D_MODEL), ("ln1_b", 1, D_MODEL), ("ln2_g", 1, D_MODEL), ("ln2_b", 1, D_MODEL))
SMALL_ROWS = sum(r for _, r, _ in SMALL)


def _pack_small(per_layer):
    flat = [a for layer in per_layer for a in layer]

    def body(*refs):
        ins, o_ref = refs[:-1], refs[-1]
        o_ref[...] = jnp.zeros_like(o_ref)
        it = iter(ins)
        for l in range(DEPTH):
            row = l * SMALL_ROWS
            for name, nrows, ncols in SMALL:
                ref = next(it)
                if name == "b_gate":
                    for k in range(nrows):
                        o_ref[row + k:row + k + 1, :] = ref[:, k * ncols:(k + 1) * ncols]
                else:
                    o_ref[row:row + nrows, 0:ncols] = ref[...]
                row += nrows

    return pl.pallas_call(body, name="pack_small_grads",
                          out_shape=jax.ShapeDtypeStruct((DEPTH * SMALL_ROWS, D_MODEL), F32))(*flat)


def _unpack_small(summed):
    s3 = summed.reshape(DEPTH, SMALL_ROWS, D_MODEL)
    out, row = {}, 0
    for name, nrows, ncols in SMALL:
        out[name] = s3[:, row:row + nrows, :ncols].reshape(DEPTH, nrows * ncols)
        row += nrows
    return out


def _natural_cols(g):
    nd = g.ndim
    perm = tuple(range(1, nd - 1)) + (0, nd - 1)
    t = jnp.transpose(g, perm)
    return t.reshape(t.shape[:-2] + (t.shape[-2] * t.shape[-1],))


def _natural_rows(g):
    return g.reshape(g.shape[0] * g.shape[1], g.shape[2])


def _hosted(hosts, key, fn):
    pairs = hosts.get(key) if hosts else None
    if callable(pairs):
        pairs = pairs()
    if not pairs:
        return fn(None)
    outs, recvs = fn([ex for ex, _ in pairs])
    for (_, hook), recv in zip(pairs, recvs):
        hook(recv)
    return outs


def _layer_fwd(cur, cur_b, mem2d, wl, *, bsz, seq, hosts=None):
    tm = min(512, seq)
    proj, hf = _hosted(hosts, "in_proj", lambda c: _in_proj(cur_b, wl["w_in"], tm=min(1024, seq), carry=c))
    y_b, o_pre, states = _hosted(hosts, "hgrn_fwd", lambda c: _hgrn_fwd(proj, hf, wl["lb"], wl["nw"], bsz=bsz,
                                                                         seq=seq, carry=c))
    mk, mv = _mem_kv(mem2d, wl["w_mk"], wl["w_mv"])
    y_c = _attn_fwd(proj, mk, mv, tm=tm, seq=seq)
    y_a, pa, pb, pc, merged, sga, sgb, sgc = _hosted(
        hosts, "merge_fwd", lambda c: _merge_fwd(proj, y_b, y_c, wl["conv"], wl["w_br"], wl["b_gate"], tm=tm,
                                                 seq=seq, carry=c))
    z1, x1, x1_b = _hosted(hosts, "wo_ln", lambda c: _linear_ln("wo_ln", merged, wl["w_o"], cur, wl["ln1_g"],
                                                                  wl["ln1_b"], tm=tm, carry=c))
    hid = _hosted(hosts, "mlp_up", lambda c: _mm_nn("mlp_up", x1_b, wl["w_up"], tm=min(1024, seq), tn=2048,
                                                     out_dtype=BF16, relu2=True, carry=c))
    z2, x2, x2_b = _linear_ln("down_ln", hid, wl["w_down"], x1, wl["ln2_g"], wl["ln2_b"], tm=tm)
    return dict(x_b=cur_b, proj=proj, hf=hf, y_a=y_a, y_b=y_b, y_c=y_c, o_pre=o_pre, states=states, mk=mk, mv=mv,
                proj3=(pa, pb, pc), gates3=(sga, sgb, sgc), merged=merged, z1=z1, x1_b=x1_b, hid=hid, z2=z2, x2=x2,
                x2_b=x2_b)


def _layer_bwd(dcur, mem2d, s, wl, *, bsz, seq, plan=None):
    tm = min(512, seq)
    tk = min(2048, bsz * seq)
    g = {}

    def run(key, fn):
        made = plan[key](g) if plan and key in plan else None
        return _hosted({key: [made]} if made else None, key, fn)

    dz2, dz2_b, dhpre, d_ln2g, d_ln2b = run(
        "ln2_bwd_down", lambda c: _ln_bwd_mm_nt("ln2_bwd_down", dcur, s["z2"], wl["ln2_g"], wl["w_down"],
                                                s["hid"], tm=tm, tn=1024, carry=c))
    g["w_down"] = _mm_tn("grad_w_down", s["hid"], dz2_b, tk=tk, tmo=1024, tno=1024)
    dx1 = _mm_nt_sum("mlp_up_bwd", [dhpre], [0], wl["w_up"], dz2, tm=tm)
    g["w_up"] = run("grad_w_up", lambda c: _mm_tn("grad_w_up", s["x1_b"], dhpre, tk=tk, tmo=1024, tno=1024, carry=c))
    dz1, dz1_b, dmerged, d_ln1g, d_ln1b = _ln_bwd_mm_nt("ln1_bwd_wo", dx1, s["z1"], wl["ln1_g"], wl["w_o"],
                                                        tm=tm, tn=1024)
    g["w_o"] = _mm_tn("grad_w_o", s["merged"], dz1_b, tk=tk, tmo=1024, tno=1024)
    dgate, dpa, dpb, dpc, dya, dyb, dyc, d_bg = _merge_bwd(dmerged, s["proj3"], s["gates3"], wl["w_br"], tm=tm)
    g["w_br"] = jnp.stack([_mm_tn("grad_w_branch", yy, dp, tk=tk, tmo=512, tno=1024)
                           for yy, dp in ((s["y_a"], dpa), (s["y_b"], dpb), (s["y_c"], dpc))])
    d_conv, d_cw = _conv_bwd(s["proj"], dya, wl["conv"], tm=tm, seq=seq)
    dhg, d_lb, d_nw = run(
        "hgrn_bwd", lambda c: _hgrn_bwd(s["proj"], s["hf"], wl["lb"], wl["nw"], s["o_pre"], s["states"], dyb,
                                        bsz=bsz, seq=seq, carry=c))
    dmq, dmk, dmv = _attn_bwd(s["proj"], s["mk"], s["mv"], dyc, tm=tm, seq=seq)
    tkm = min(512, mem2d.shape[0])
    g["w_mk"] = _mm_tn("grad_w_mem", mem2d, dmk, tk=tkm, tmo=1024, tno=512)
    g["w_mv"] = _mm_tn("grad_w_mem", mem2d, dmv, tk=tkm, tmo=1024, tno=512)
    pieces = [d_conv, dhg, dmq, dgate]
    offsets = [C_CB, C_HQ, C_MQ, C_GA]
    g["w_in"] = jnp.concatenate(
        [_mm_tn("grad_w_in_%d" % p.shape[1], s["x_b"], p, tk=tk, tmo=1024,
                tno=next(w for w in (1024, 768, 512) if p.shape[1] % w == 0)) for p in pieces], axis=1)
    dx = run("in_proj_bwd", lambda c: _mm_nt_sum("in_proj_bwd", pieces, offsets, wl["w_in"], dz1,
                                                 tm=min(256, seq), carry=c))
    return dx, g, [d_lb, d_cw, d_nw, d_bg, d_ln1g, d_ln1b, d_ln2g, d_ln2b]


def kernel(x, mem, lower_bounds, w_in, conv_w, hg_norm_w, w_mem_k, w_mem_v, w_branch, b_gate, w_o, ln1_g, ln1_b, w_up, w_down, ln2_g, ln2_b, loss_target, m_lower_bounds, m_w_in, m_conv_w, m_hg_norm_w, m_w_mem_k, m_w_mem_v, m_w_branch, m_b_gate, m_w_o, m_ln1_g, m_ln1_b, m_w_up, m_w_down, m_ln2_g, m_ln2_b, v_lower_bounds, v_w_in, v_conv_w, v_hg_norm_w, v_w_mem_k, v_w_mem_v, v_w_branch, v_b_gate, v_w_o, v_ln1_g, v_ln1_b, v_w_up, v_w_down, v_ln2_g, v_ln2_b):
    bsz, seq, _ = x.shape
    t = bsz * seq
    me = _my_id()

    sh = dict(w_in=w_in.astype(BF16), w_mk=w_mem_k.astype(BF16), w_mv=w_mem_v.astype(BF16),
              w_br=w_branch.astype(BF16), w_o=w_o.astype(BF16), w_up=w_up.astype(BF16), w_down=w_down.astype(BF16))
    half_rows = D_MODEL // 2
    sh["w_in_a"], sh["w_in_b"] = sh["w_in"][:, :half_rows], sh["w_in"][:, half_rows:]
    natural = dict(w_in=_natural_cols, w_in_a=_natural_cols, w_in_b=_natural_cols, w_mk=_natural_rows,
                   w_mv=_natural_rows, w_br=_natural_cols, w_o=_natural_rows, w_up=_natural_cols,
                   w_down=_natural_rows)

    lb_all = _lb_fwd(lower_bounds)
    layer_w = [dict(lb=lb_all[l][None], nw=hg_norm_w[l][None], b_gate=b_gate[l][None], ln1_g=ln1_g[l][None],
                    ln1_b=ln1_b[l][None], ln2_g=ln2_g[l][None], ln2_b=ln2_b[l][None]) for l in range(DEPTH)]
    half_full = {}

    def near(names, l):
        srcs = [sh[n][l] for n in names]
        ex = _Exchange(srcs, piece_shapes=[s_.shape for s_ in srcs], route="near")
        return ex, lambda recv_: half_full.update({(n, l): r for n, r in zip(names, recv_)})

    def relay(names, l):
        ex = _Exchange([half_full.pop((n, l)) for n in names], route="relay")

        def hook(recv_):
            for n, r in zip(names, recv_):
                layer_w[l][n] = natural[n](r)
        return ex, hook

    small4 = ["w_mk", "w_mv", "w_br", "w_o"]
    conv_shard = conv_w.reshape(DEPTH * CONV_K * (WIDTH // N_DEV) // 128, 128)
    first = near(["w_in"] + small4, 0)
    conv_ex = _Exchange([conv_shard], piece_shapes=[conv_shard.shape])
    got = _exchange("gather_first", [first[0], conv_ex])
    first[1](got[0])
    conv_full = _natural_cols(got[1][0].reshape(N_DEV, DEPTH, CONV_K, WIDTH // N_DEV))
    second = relay(["w_in"] + small4, 0)
    second[1](_exchange("relay_first", [second[0]])[0])

    x2d = x.reshape(t, D_MODEL)
    mem2d = mem.reshape(bsz * MEM_LEN, D_MODEL)
    target2d = loss_target.reshape(t, D_MODEL)

    saved = []
    cur, cur_b = x2d, x2d.astype(BF16)
    for l in range(DEPTH):
        wl = layer_w[l]
        wl["conv"] = conv_full[l]
        more = l + 1 < DEPTH
        hosts = {"in_proj": [near(["w_up", "w_down"], l)],
                 "hgrn_fwd": lambda l=l, more=more: [relay(["w_up", "w_down"], l)] + ([near(["w_in"], l + 1)] if more else [])}
        if more:
            hosts["merge_fwd"] = lambda l=l: [relay(["w_in"], l + 1), near(small4, l + 1)]
            hosts["mlp_up"] = lambda l=l: [relay(small4, l + 1)]
        s = _layer_fwd(cur, cur_b, mem2d, wl, bsz=bsz, seq=seq, hosts=hosts)
        saved.append(s)
        cur, cur_b = s["x2"], s["x2_b"]

    dcur, loss_tile = _loss_head(cur, target2d, tm=min(512, seq))
    loss = lax.psum(loss_tile[0, 0], ("x", "y", "c"))

    in_w = IN_COLS // N_DEV

    def in_half(r):
        return lambda ref, j: ref.at[pl.ds(r * half_rows, half_rows), pl.ds(j * in_w, in_w)]

    slicer = dict(w_in_a=in_half(0), w_in_b=in_half(1), w_mk=_rows(D_MODEL // N_DEV), w_mv=_rows(D_MODEL // N_DEV),
                  w_br=_cols(D_MODEL // N_DEV), w_o=_rows(D_MODEL // N_DEV), w_up=_cols(D_FF // N_DEV),
                  w_down=_rows(D_FF // N_DEV))
    source = dict(w_in_a="w_in", w_in_b="w_in")
    recv = [dict() for _ in range(DEPTH)]

    def scatter_of(names, g, into):
        ex = _Exchange([g[source.get(n, n)] for n in names], [slicer[n] for n in names],
                       [sh[n].shape[1:] for n in names])
        return ex, lambda recv_: into.update(zip(names, recv_))

    small_rows = [None] * DEPTH
    prev = None
    rest = ["w_in_b", "w_mk", "w_mv"]
    for l in reversed(range(DEPTH)):
        plan = {"grad_w_up": lambda g, l=l: scatter_of(["w_down"], g, recv[l]),
                "hgrn_bwd": lambda g, l=l: scatter_of(["w_up", "w_o", "w_br"], g, recv[l])}
        if l == 0:
            plan["in_proj_bwd"] = lambda g: scatter_of(["w_in_a"] + rest, g, recv[0])
        else:
            plan["in_proj_bwd"] = lambda g, l=l: scatter_of(["w_in_a"], g, recv[l])
        if prev is not None:
            plan["ln2_bwd_down"] = lambda g, l=l, prev=prev: scatter_of(rest, prev, recv[l + 1])
        dcur, prev, small_rows[l] = _layer_bwd(dcur, mem2d, saved[l], layer_w[l], bsz=bsz, seq=seq, plan=plan)
    for r in recv:
        r["w_in"] = jnp.concatenate([r.pop("w_in_a"), r.pop("w_in_b")], axis=1)

    packed = _pack_small(small_rows)
    all_small = _exchange("gather_small_grads", [_Exchange([packed], piece_shapes=[packed.shape])])[0][0]
    small_grads = _unpack_small(_sum_devices("sum_small_grads", all_small))
    small_grads["lower_bounds"] = _lb_bwd(lower_bounds, small_grads["lower_bounds"])
    conv_all = small_grads["conv_w"].reshape(DEPTH, CONV_K, WIDTH)
    small_grads["conv_w"] = lax.dynamic_slice_in_dim(conv_all, me * (WIDTH // N_DEV), WIDTH // N_DEV, axis=2)

    grads, deltas, new_m, new_v = {}, {}, {}, {}
    given = dict(lower_bounds=(lower_bounds, m_lower_bounds, v_lower_bounds), conv_w=(conv_w, m_conv_w, v_conv_w),
                 hg_norm_w=(hg_norm_w, m_hg_norm_w, v_hg_norm_w), b_gate=(b_gate, m_b_gate, v_b_gate),
                 ln1_g=(ln1_g, m_ln1_g, v_ln1_g), ln1_b=(ln1_b, m_ln1_b, v_ln1_b),
                 ln2_g=(ln2_g, m_ln2_g, v_ln2_g), ln2_b=(ln2_b, m_ln2_b, v_ln2_b))
    for name, (w_, m_, v_) in given.items():
        g_ = small_grads[name].reshape(w_.shape)
        grads[name] = g_
        deltas[name], new_m[name], new_v[name] = _adam_small("adam_" + name, g_, w_, m_, v_)

    big = dict(w_in=("w_in", w_in, m_w_in, v_w_in, 128), w_mem_k=("w_mk", w_mem_k, m_w_mem_k, v_w_mem_k, 128),
               w_mem_v=("w_mv", w_mem_v, m_w_mem_v, v_w_mem_v, 128),
               w_branch=("w_br", w_branch, m_w_branch, v_w_branch, 512), w_o=("w_o", w_o, m_w_o, v_w_o, 128),
               w_up=("w_up", w_up, m_w_up, v_w_up, 256), w_down=("w_down", w_down, m_w_down, v_w_down, 128))
    for name, (k, w_, m_, v_, tr) in big.items():
        shape = w_.shape
        flat = (DEPTH, -1, shape[-1])
        rc = [recv[l][k].reshape((N_DEV,) + w_.reshape(flat).shape[1:]) for l in range(DEPTH)]
        outs = _adam_shard("adam_" + name, rc, w_.reshape(flat), m_.reshape(flat), v_.reshape(flat), tr=tr)
        grads[name], deltas[name], new_m[name], new_v[name] = (o.reshape(shape) for o in outs)

    order = ["lower_bounds", "w_in", "conv_w", "hg_norm_w", "w_mem_k", "w_mem_v", "w_branch", "b_gate", "w_o",
             "ln1_g", "ln1_b", "w_up", "w_down", "ln2_g", "ln2_b"]
    return (loss, dcur.reshape(x.shape), *[grads[n] for n in order], *[deltas[n] for n in order],
            *[new_m[n] for n in order], *[new_v[n] for n in order])
```

```python
import functools

import jax
import jax.numpy as jnp
from jax import lax
from jax.experimental import pallas as pl
from jax.experimental.pallas import tpu as pltpu

F32 = jnp.float32
BF16 = jnp.bfloat16

N_DEV = 8
D_MODEL = 1024
DEPTH = 4
MEM_LEN = 256
CONV_K = 3
WIDTH = 512
HEADS = 4
HEAD_DIM = 128
CHUNK = 32
D_FF = 4 * D_MODEL
IN_COLS = 7168
ALPHA = (2.0 * DEPTH) ** 0.25
LN_EPS = 1e-5
RMS_EPS = 1e-6
ADAM_LR = 0.001
ADAM_B1 = 0.9
ADAM_B2 = 0.999
ADAM_EPS = 1e-08
ADAM_WD = 0.01
ADAM_STEP = 10

C_CB, C_CC, C_CH, C_HQ, C_HF, C_HI, C_HG, C_MQ, C_GA = 0, 512, 1024, 1536, 2048, 2560, 3072, 3584, 4096

ROWS_HG = 256
NT_DIMS = (((1,), (1,)), ((), ()))
TN_DIMS = (((0,), (0,)), ((), ()))
MESH = pl.DeviceIdType.MESH


def _dot(a, b):
    return jnp.dot(a, b, preferred_element_type=F32)


def _dot_nt(a, b):
    return lax.dot_general(a, b, NT_DIMS, preferred_element_type=F32)


def _dot_tn(a, b):
    return lax.dot_general(a, b, TN_DIMS, preferred_element_type=F32)


def _sigmoid(x):
    return 1.0 / (1.0 + jnp.exp(-x))


def _params(*sem):
    return pltpu.CompilerParams(dimension_semantics=sem)


def _resident(shape, single=False):
    nd = len(shape)
    if single:
        return pl.BlockSpec(shape, lambda *_: (0,) * nd, pipeline_mode=pl.Buffered(1))
    return pl.BlockSpec(shape, lambda *_: (0,) * nd)


def _my_id():
    return 4 * lax.axis_index("x") + 2 * lax.axis_index("y") + lax.axis_index("c")


class _Exchange:
    def __init__(self, srcs, slicers=None, piece_shapes=None, route="all"):
        self.srcs, self.n, self.route = list(srcs), len(srcs), route
        self.slicers = list(slicers) if slicers else [_whole] * self.n
        any_spec = pl.BlockSpec(memory_space=pl.ANY)
        self.in_specs = [any_spec] * self.n
        self.out_specs = [any_spec] * self.n
        if route == "relay":
            self.out_shape = [jax.ShapeDtypeStruct(a.shape, a.dtype) for a in srcs]
        else:
            self.out_shape = [jax.ShapeDtypeStruct((N_DEV,) + tuple(s), a.dtype) for s, a in zip(piece_shapes, srcs)]
        self.aliased = route == "relay"
        self.scratch = [pltpu.SemaphoreType.DMA((self.n * N_DEV,)), pltpu.SemaphoreType.DMA((self.n * N_DEV,)),
                        pltpu.SemaphoreType.DMA((self.n,))]

    def _peer(self, j, me):
        if self.route == "all":
            return j != me
        return (j != me) & ((j % 2 == lax.axis_index("c")) | (j // 2 == me // 2))

    def _remote(self, ins, outs, sems, k, j, me):
        return pltpu.make_async_remote_copy(
            src_ref=self.slicers[k](ins[k], j), dst_ref=outs[k].at[me],
            send_sem=sems[0].at[k * N_DEV + j], recv_sem=sems[1].at[k * N_DEV + me],
            device_id=(j // 4, (j // 2) % 2, j % 2), device_id_type=MESH)

    def _local(self, ins, outs, sems, k, j, me):
        return pltpu.make_async_copy(self.slicers[k](ins[k], j), outs[k].at[me], sems[2].at[k])

    def _relay(self, outs, sems, k, j):
        sibling = (lax.axis_index("x"), lax.axis_index("y"), 1 - lax.axis_index("c"))
        return pltpu.make_async_remote_copy(
            src_ref=outs[k].at[j], dst_ref=outs[k].at[j], send_sem=sems[0].at[k * N_DEV + j],
            recv_sem=sems[1].at[k * N_DEV + j], device_id=sibling, device_id_type=MESH)

    def _other_chip(self, j, same_core):
        on_my_core = j % 2 == lax.axis_index("c")
        return (on_my_core if same_core else ~on_my_core) & (j // 2 != _my_id() // 2)

    def start(self, ins, outs, sems):
        me = _my_id()
        for k in range(self.n):
            for j in range(N_DEV):
                if self.route == "relay":
                    @pl.when(self._other_chip(j, True))
                    def _():
                        self._relay(outs, sems, k, j).start()
                    continue

                @pl.when(self._peer(j, me))
                def _():
                    self._remote(ins, outs, sems, k, j, me).start()

                @pl.when(j == me)
                def _():
                    self._local(ins, outs, sems, k, j, me).start()

    def wait(self, ins, outs, sems):
        me = _my_id()
        for k in range(self.n):
            for j in range(N_DEV):
                if self.route == "relay":
                    @pl.when(self._other_chip(j, False))
                    def _():
                        self._relay(outs, sems, k, j).wait_recv()

                    @pl.when(self._other_chip(j, True))
                    def _():
                        self._relay(outs, sems, k, j).wait_send()
                    continue

                @pl.when(self._peer(j, me))
                def _():
                    pltpu.make_async_remote_copy(
                        src_ref=self.slicers[k](ins[k], j), dst_ref=outs[k].at[j],
                        send_sem=sems[0].at[k * N_DEV + j], recv_sem=sems[1].at[k * N_DEV + j],
                        device_id=(j // 4, (j // 2) % 2, j % 2), device_id_type=MESH).wait_recv()
                    self._remote(ins, outs, sems, k, j, me).wait_send()

                @pl.when(j == me)
                def _():
                    self._local(ins, outs, sems, k, j, me).wait()


def _carried(exchanges, n_in, n_out):
    c_in = [s for ex in exchanges for s in ex.in_specs]
    c_out = [s for ex in exchanges for s in ex.out_specs]
    shapes = [s for ex in exchanges for s in ex.out_shape]
    sems = [s for ex in exchanges for s in ex.scratch]
    srcs = [a for ex in exchanges for a in ex.srcs]
    aliases, off = {}, 0
    for ex in exchanges:
        if ex.aliased:
            aliases.update({n_in + off + k: n_out + off + k for k in range(ex.n)})
        off += ex.n
    total = off

    def split(refs, n_scr):
        ins, outs = refs[:n_in], refs[n_in + total:n_in + total + n_out]
        rest = refs[n_in + 2 * total + n_out:]
        scr, sem_refs = rest[:n_scr], rest[n_scr:]
        parts, off_ = [], 0
        for i, ex in enumerate(exchanges):
            parts.append((refs[n_in + off_:n_in + off_ + ex.n],
                          refs[n_in + total + n_out + off_:n_in + total + n_out + off_ + ex.n],
                          sem_refs[3 * i:3 * i + 3]))
            off_ += ex.n
        return ins, outs, scr, parts

    return c_in, c_out, shapes, sems, srcs, aliases, split


def _exchange(name, exchanges):
    c_in, c_out, shapes, sems, srcs, aliases, split = _carried(exchanges, 0, 0)

    def body(*refs):
        _, _, _, parts = split(refs, 0)
        for ex, part in zip(exchanges, parts):
            ex.start(*part)
        for ex, part in zip(exchanges, parts):
            ex.wait(*part)

    outs = pl.pallas_call(
        body, name=name, in_specs=c_in, out_specs=c_out, out_shape=shapes, scratch_shapes=sems,
        input_output_aliases=aliases, compiler_params=pltpu.CompilerParams(has_side_effects=True))(*srcs)
    return _per_exchange(exchanges, outs)


def _per_exchange(exchanges, flat):
    out, off = [], 0
    for ex in exchanges:
        out.append(flat[off:off + ex.n])
        off += ex.n
    return out


def _call(body, name, grid, in_specs, out_specs, out_shape, args, scratch=(), sem=None, carry=None):
    n_in, n_out, n_scr = len(in_specs), len(out_specs), len(scratch)
    if not carry:
        outs = pl.pallas_call(body, name=name, grid=grid, in_specs=in_specs, out_specs=out_specs,
                              out_shape=out_shape, scratch_shapes=list(scratch),
                              compiler_params=_params(*sem))(*args)
        return outs, None
    c_in, c_out, shapes, sems, srcs, aliases, split = _carried(carry, n_in, n_out)

    def hosted(*refs):
        ins, outs, scr, parts = split(refs, n_scr)
        first, last = True, True
        for d, size in enumerate(grid):
            first = first & (pl.program_id(d) == 0)
            last = last & (pl.program_id(d) == size - 1)

        @pl.when(first)
        def _():
            for ex, part in zip(carry, parts):
                ex.start(*part)

        body(*ins, *outs, *scr)

        @pl.when(last)
        def _():
            for ex, part in zip(carry, parts):
                ex.wait(*part)

    outs = pl.pallas_call(
        hosted, name=name + "_x", grid=grid, in_specs=list(in_specs) + c_in,
        out_specs=list(out_specs) + c_out, out_shape=list(out_shape) + shapes,
        scratch_shapes=list(scratch) + sems, input_output_aliases=aliases,
        compiler_params=_params(*(["arbitrary"] * len(grid))))(*args, *srcs)
    return outs[:n_out], _per_exchange(carry, outs[n_out:])


def _whole(ref, j):
    return ref


def _cols(width):
    return lambda ref, j: ref.at[(slice(None),) * (len(ref.shape) - 1) + (pl.ds(j * width, width),)]


def _rows(height):
    return lambda ref, j: ref.at[pl.ds(j * height, height)]


def _mm_nn(name, a, w, *, tm, tn, out_dtype, relu2=False, carry=None):
    t, k = a.shape
    n = w.shape[1]

    def body(a_ref, w_ref, o_ref):
        acc = _dot(a_ref[...].astype(BF16), w_ref[...])
        if relu2:
            r = jnp.maximum(acc, 0.0)
            acc = r * r
        o_ref[...] = acc.astype(out_dtype)

    outs, recv = _call(
        body, name, (t // tm, n // tn),
        [pl.BlockSpec((tm, k), lambda i, j: (i, 0)), pl.BlockSpec((k, tn), lambda i, j: (0, j))],
        [pl.BlockSpec((tm, tn), lambda i, j: (i, j))], [jax.ShapeDtypeStruct((t, n), out_dtype)], (a, w),
        sem=("parallel", "parallel"), carry=carry)
    return outs[0] if carry is None else (outs[0], recv)


def _in_proj(a, w, *, tm, carry=None):
    t, k = a.shape
    tn = IN_COLS // 4
    f_tile, f_off = C_HF // tn, C_HF % tn

    def body(a_ref, w_ref, o_ref, f_ref):
        acc = _dot(a_ref[...], w_ref[...])
        o_ref[...] = acc.astype(BF16)

        @pl.when(pl.program_id(1) == f_tile)
        def _():
            f_ref[...] = acc[:, f_off:f_off + WIDTH]

    outs, recv = _call(
        body, "in_proj", (t // tm, IN_COLS // tn),
        [pl.BlockSpec((tm, k), lambda i, j: (i, 0)), pl.BlockSpec((k, tn), lambda i, j: (0, j))],
        [pl.BlockSpec((tm, tn), lambda i, j: (i, j)), pl.BlockSpec((tm, WIDTH), lambda i, j: (i, 0))],
        [jax.ShapeDtypeStruct((t, IN_COLS), BF16), jax.ShapeDtypeStruct((t, WIDTH), F32)], (a, w),
        sem=("parallel", "arbitrary"), carry=carry)
    return outs if carry is None else (outs, recv)


def _linear_ln(name, a, w, resid, g, b, *, tm, carry=None):
    t, k = a.shape
    halves = [slice(0, tm // 2), slice(tm // 2, tm)] if k > D_MODEL else [slice(0, tm)]

    def body(a_ref, w_ref, r_ref, g_ref, b_ref, z_ref, x_ref, xb_ref):
        z = ALPHA * _Lanes(r_ref[s, :] for s in halves) + _ldot(_Lanes(a_ref[s, :] for s in halves), w_ref[...])
        zc = z - _mean(z, axis=-1, keepdims=True)
        y = zc * _rsqrt(_mean(zc * zc, axis=-1, keepdims=True) + LN_EPS) * g_ref[...] + b_ref[...]
        for s, zz, yy in zip(halves, z.xs, y.xs):
            z_ref[s, :] = zz
            x_ref[s, :] = yy
            xb_ref[s, :] = yy.astype(BF16)

    row = pl.BlockSpec((tm, D_MODEL), lambda i: (i, 0))
    outs, recv = _call(
        body, name, (t // tm,),
        [pl.BlockSpec((tm, k), lambda i: (i, 0)), _resident((k, D_MODEL)), row,
         _resident((1, D_MODEL)), _resident((1, D_MODEL))],
        [row, row, row],
        [jax.ShapeDtypeStruct((t, D_MODEL), F32)] * 2 + [jax.ShapeDtypeStruct((t, D_MODEL), BF16)],
        (a, w, resid, g, b), sem=("parallel",), carry=carry)
    return outs if carry is None else (outs, recv)


def _ln_bwd_mm_nt(name, dy, z, g, w, h=None, *, tm, tn, carry=None):
    t = dy.shape[0]
    n = w.shape[0]
    halves = [slice(0, tm // 2), slice(tm // 2, tm)]

    def body(*refs):
        if h is None:
            dy_ref, z_ref, g_ref, w_ref, dz_ref, dzb_ref, o_ref, dg_ref, db_ref = refs
        else:
            dy_ref, z_ref, g_ref, w_ref, h_ref, dz_ref, dzb_ref, o_ref, dg_ref, db_ref = refs

        @pl.when(pl.program_id(0) == 0)
        def _():
            dg_ref[...] = jnp.zeros_like(dg_ref)
            db_ref[...] = jnp.zeros_like(db_ref)

        zv = _Lanes(z_ref[s, :] for s in halves)
        dyv = _Lanes(dy_ref[s, :] for s in halves)
        mu = _mean(zv, axis=-1, keepdims=True)
        zc = zv - mu
        rstd = _rsqrt(_mean(zc * zc, axis=-1, keepdims=True) + LN_EPS)
        xh = zc * rstd
        gdy = dyv * g_ref[...]
        m1 = _mean(gdy, axis=-1, keepdims=True)
        m2 = _mean(gdy * xh, axis=-1, keepdims=True)
        dz = rstd * (gdy - m1 - xh * m2)
        dz_b = dz.astype(BF16)
        for s, a, a_b in zip(halves, dz.xs, dz_b.xs):
            dz_ref[s, :] = a
            dzb_ref[s, :] = a_b
        dg_ref[...] += _sum(dyv * xh, axis=0, keepdims=True).total()
        db_ref[...] += _sum(dyv, axis=0, keepdims=True).total()
        for c in range(n // tn):
            cols = slice(c * tn, (c + 1) * tn)
            acc = _ldot_nt(dz_b, w_ref[cols, :])
            if h is not None:
                acc = acc * (2.0 * _sqrt(_Lanes(h_ref[s, cols] for s in halves).astype(F32)))
            for s, a in zip(halves, acc.xs):
                o_ref[s, cols] = a.astype(BF16)

    row = pl.BlockSpec((tm, D_MODEL), lambda i: (i, 0))
    vec = _resident((1, D_MODEL))
    tile = pl.BlockSpec((tm, n), lambda i: (i, 0))
    in_specs = [row, row, vec, _resident((n, D_MODEL))]
    args = [dy, z, g, w]
    if h is not None:
        in_specs.append(tile)
        args.append(h)
    outs, recv = _call(
        body, name, (t // tm,), in_specs, [row, row, tile, vec, vec],
        [jax.ShapeDtypeStruct((t, D_MODEL), F32), jax.ShapeDtypeStruct((t, D_MODEL), BF16),
         jax.ShapeDtypeStruct((t, n), BF16), jax.ShapeDtypeStruct((1, D_MODEL), F32),
         jax.ShapeDtypeStruct((1, D_MODEL), F32)], args, sem=("arbitrary",), carry=carry)
    return outs if carry is None else (outs, recv)


def _mm_tn(name, a, b, *, tk, tmo, tno, carry=None):
    t, m = a.shape
    n = b.shape[1]
    nk = t // tk

    def body(a_ref, b_ref, o_ref, acc_ref):
        k = pl.program_id(2)
        p = _dot_tn(a_ref[...].astype(BF16), b_ref[...].astype(BF16))

        @pl.when(k == 0)
        def _():
            acc_ref[...] = p

        @pl.when(k > 0)
        def _():
            acc_ref[...] += p

        @pl.when(k == nk - 1)
        def _():
            o_ref[...] = acc_ref[...].astype(BF16)

    outs, recv = _call(
        body, name, (m // tmo, n // tno, nk),
        [pl.BlockSpec((tk, tmo), lambda i, j, k: (k, i)), pl.BlockSpec((tk, tno), lambda i, j, k: (k, j))],
        [pl.BlockSpec((tmo, tno), lambda i, j, k: (i, j))], [jax.ShapeDtypeStruct((m, n), BF16)], (a, b),
        scratch=[pltpu.VMEM((tmo, tno), F32)], sem=("parallel", "parallel", "arbitrary"), carry=carry)
    return outs[0] if carry is None else (outs[0], recv)


def _mm_nt_sum(name, pieces, offsets, w, resid, *, tm, carry=None):
    t = resid.shape[0]
    widths = [p.shape[1] for p in pieces]
    n_p = len(pieces)

    def body(*refs):
        p_refs, w_ref, r_ref, o_ref = refs[:n_p], refs[n_p], refs[n_p + 1], refs[n_p + 2]
        acc = ALPHA * r_ref[...]
        for p_ref, off, wd in zip(p_refs, offsets, widths):
            acc = acc + _dot_nt(p_ref[...], w_ref[:, off:off + wd])
        o_ref[...] = acc

    row = pl.BlockSpec((tm, D_MODEL), lambda i: (i, 0))
    outs, recv = _call(
        body, name, (t // tm,),
        [pl.BlockSpec((tm, wd), lambda i: (i, 0)) for wd in widths] + [_resident(w.shape, single=True), row],
        [row], [jax.ShapeDtypeStruct((t, D_MODEL), F32)], (*pieces, w, resid), sem=("parallel",), carry=carry)
    return outs[0] if carry is None else (outs[0], recv)


def _chunk_mask(rows):
    r = lax.broadcasted_iota(jnp.int32, (rows, rows), 0)
    c = lax.broadcasted_iota(jnp.int32, (rows, rows), 1)
    return ((r // CHUNK) == (c // CHUNK)) & (c <= r)


class _Lanes:
    def __init__(self, xs):
        self.xs = list(xs)

    def _with(self, other, f):
        if isinstance(other, _Lanes):
            return _Lanes([f(a, b) for a, b in zip(self.xs, other.xs)])
        return _Lanes([f(a, other) for a in self.xs])

    def __add__(self, o):
        return self._with(o, lambda a, b: a + b)

    def __radd__(self, o):
        return self._with(o, lambda a, b: b + a)

    def __sub__(self, o):
        return self._with(o, lambda a, b: a - b)

    def __rsub__(self, o):
        return self._with(o, lambda a, b: b - a)

    def __mul__(self, o):
        return self._with(o, lambda a, b: a * b)

    def __rmul__(self, o):
        return self._with(o, lambda a, b: b * a)

    def __truediv__(self, o):
        return self._with(o, lambda a, b: a / b)

    def __rtruediv__(self, o):
        return self._with(o, lambda a, b: b / a)

    def __neg__(self):
        return _Lanes([-a for a in self.xs])

    def __ge__(self, o):
        return self._with(o, lambda a, b: a >= b)

    def __getitem__(self, idx):
        return _Lanes([a[idx] for a in self.xs])

    def astype(self, dtype):
        return _Lanes([a.astype(dtype) for a in self.xs])

    def total(self):
        return functools.reduce(lambda a, b: a + b, self.xs)


def _lift(f):
    def g(*args, **kw):
        lanes = [a for a in args if isinstance(a, _Lanes)]
        if not lanes:
            return f(*args, **kw)
        return _Lanes([f(*[a.xs[i] if isinstance(a, _Lanes) else a for a in args], **kw)
                       for i in range(len(lanes[0].xs))])
    return g


def _concat(parts, axis):
    if isinstance(parts[0], _Lanes):
        return _Lanes([jnp.concatenate([p.xs[i] for p in parts], axis=axis) for i in range(len(parts[0].xs))])
    return jnp.concatenate(parts, axis=axis)


_exp, _log, _abs, _sqrt, _where = _lift(jnp.exp), _lift(jnp.log), _lift(jnp.abs), _lift(jnp.sqrt), _lift(jnp.where)
_sum, _mean, _rsqrt, _bcast = _lift(jnp.sum), _lift(jnp.mean), _lift(lax.rsqrt), _lift(jnp.broadcast_to)
_ldot, _ldot_nt, _ldot_tn = _lift(_dot), _lift(_dot_nt), _lift(_dot_tn)
_lsigmoid = _lift(_sigmoid)


def _mask_sum(mask_b, x, transpose=False):
    f = _ldot_tn if transpose else _ldot
    hi = x.astype(BF16)
    lo = (x - hi.astype(F32)).astype(BF16)
    return f(mask_b, hi) + f(mask_b, lo)


def _chunk_row(x, pos, rows):
    nc = rows // CHUNK

    def one(a):
        a3 = a.reshape(nc, CHUNK, HEAD_DIM)
        return jnp.broadcast_to(a3[:, pos:pos + 1, :], (nc, CHUNK, HEAD_DIM)).reshape(rows, HEAD_DIM)

    return _lift(one)(x)


def _chunk_total(x, rows):
    nc = rows // CHUNK

    def one(a):
        tot = jnp.sum(a.reshape(nc, CHUNK, HEAD_DIM), axis=1, keepdims=True)
        return jnp.broadcast_to(tot, (nc, CHUNK, HEAD_DIM)).reshape(rows, HEAD_DIM)

    return _lift(one)(x)


def _sigmoid_pair(x):
    e = _exp(-_abs(x))
    big = 1.0 / (1.0 + e)
    small = e * big
    pos = x >= 0.0
    return _where(pos, big, small), _where(pos, small, big)


def _hg_gates(q_raw, fl, lb, rows, mask):
    tri = mask.astype(BF16)
    sg, sg_neg = _sigmoid_pair(fl)
    forget = lb + (1.0 - lb) * sg
    k = (1.0 - lb) * sg_neg
    sq = _lsigmoid(q_raw)
    qs = q_raw * sq
    bc = _mask_sum(tri, _log(forget))
    bref = _chunk_row(bc, CHUNK // 2 - 1, rows)
    blast = _chunk_row(bc, CHUNK - 1, rows)
    return dict(tri=tri, sg=sg, sg_neg=sg_neg, forget=forget, k=k, sq=sq, qs=qs,
                e_a=_exp(bc - bref), e_b=_exp(bref - bc), e_q=_exp(bc), e_k=_exp(blast - bc),
                dec=_exp(blast))


HG_GROUP = 4


def _hg_lanes(bsz):
    return [(hh, slice(hh * HEAD_DIM, (hh + 1) * HEAD_DIM), b) for hh in range(HG_GROUP) for b in range(bsz)]


def _hg_read(ref, lanes):
    return _Lanes(ref[b, :, cs].astype(F32) for _, cs, b in lanes)


def _hg_write(ref, lanes, val, offset=0):
    for (_, cs, b), a in zip(lanes, val.xs):
        ref[b, :, offset + cs.start:offset + cs.stop] = a


def _hgrn_fwd(proj, hf, lb, nw, *, bsz, seq, carry=None):
    rows = min(ROWS_HG, seq)
    nt = seq // rows
    nc = rows // CHUNK
    t = bsz * seq

    lanes = _hg_lanes(bsz)

    def body(q_ref, f_ref, v_ref, g_ref, lb_ref, nw_ref, y_ref, o_ref, st_ref, s_scr):
        @pl.when(pl.program_id(1) == 0)
        def _():
            s_scr[...] = jnp.zeros_like(s_scr)

        mask = _chunk_mask(rows)
        lb_v = _Lanes(lb_ref[:, cs] for _, cs, _ in lanes)
        gt = _hg_gates(_hg_read(q_ref, lanes), _hg_read(f_ref, lanes), lb_v, rows, mask)
        v_b = _hg_read(v_ref, lanes).astype(BF16)
        a_b = (gt["qs"] * gt["e_a"]).astype(BF16)
        b_b = (gt["k"] * gt["e_b"]).astype(BF16)
        qi_b = (gt["qs"] * gt["e_q"]).astype(BF16)
        ko_b = (gt["k"] * gt["e_k"]).astype(BF16)
        scores = _where(mask, _ldot_nt(a_b, b_b), 0.0)
        o_intra = _ldot(scores.astype(BF16), v_b)

        s = _Lanes(s_scr[i] for i in range(len(lanes)))
        parts = []
        for n in range(nc):
            sl = slice(n * CHUNK, (n + 1) * CHUNK)
            s_b = s.astype(BF16)
            for (hh, _, b), a in zip(lanes, s_b.xs):
                st_ref[hh, b, n] = a
            parts.append(_ldot_nt(qi_b[sl], s_b))
            s = s * gt["dec"][n * CHUNK:n * CHUNK + 1] + _ldot_tn(v_b[sl], ko_b[sl])
        for i, a in enumerate(s.xs):
            s_scr[i] = a
        o = o_intra + _concat(parts, 0)
        _hg_write(o_ref, lanes, o)
        r = _rsqrt(_mean(o * o, axis=-1, keepdims=True) + RMS_EPS)
        g = _hg_read(g_ref, lanes)
        _hg_write(y_ref, lanes, (o * r * nw_ref[...] * (g * _lsigmoid(g))).astype(BF16))

    wide = HG_GROUP * HEAD_DIM

    def col(base):
        return pl.BlockSpec((bsz, rows, wide), lambda h, j: (0, j, base // wide + h))

    out_tile = pl.BlockSpec((bsz, rows, wide), lambda h, j: (0, j, h))
    p3 = proj.reshape(bsz, seq, IN_COLS)
    outs, recv = _call(
        body, "hgrn_fwd", (HEADS // HG_GROUP, nt),
        [col(C_HQ), out_tile, col(C_HI), col(C_HG),
         pl.BlockSpec((1, wide), lambda h, j: (0, h)), _resident((1, HEAD_DIM))],
        [out_tile, out_tile,
         pl.BlockSpec((HG_GROUP, bsz, nc, HEAD_DIM, HEAD_DIM), lambda h, j: (h, 0, j, 0, 0))],
        [jax.ShapeDtypeStruct((bsz, seq, WIDTH), BF16), jax.ShapeDtypeStruct((bsz, seq, WIDTH), F32),
         jax.ShapeDtypeStruct((HEADS, bsz, seq // CHUNK, HEAD_DIM, HEAD_DIM), BF16)],
        (p3, hf.reshape(bsz, seq, WIDTH), p3, p3, lb, nw),
        scratch=[pltpu.VMEM((len(lanes), HEAD_DIM, HEAD_DIM), F32)],
        sem=("parallel", "arbitrary"), carry=carry)
    outs = [outs[0].reshape(t, WIDTH), outs[1].reshape(t, WIDTH), outs[2]]
    return outs if carry is None else (outs, recv)


def _hgrn_bwd(proj, hf, lb, nw, o_pre, states, dy, *, bsz, seq, carry=None):
    rows = min(ROWS_HG, seq)
    nt = seq // rows
    nc = rows // CHUNK
    t = bsz * seq
    lanes = _hg_lanes(bsz)

    def body(q_ref, f_ref, v_ref, g_ref, lb_ref, nw_ref, o_ref, st_ref, dy_ref, dh_ref, dlb_ref, dnw_ref, ds_scr):
        h, j = pl.program_id(0), pl.program_id(1)

        @pl.when(j == 0)
        def _():
            ds_scr[...] = jnp.zeros_like(ds_scr)
            dlb_ref[...] = jnp.zeros_like(dlb_ref)

        @pl.when((h == 0) & (j == 0))
        def _():
            dnw_ref[...] = jnp.zeros_like(dnw_ref)

        mask = _chunk_mask(rows)
        q_raw = _hg_read(q_ref, lanes)
        lb_v = _Lanes(lb_ref[:, cs] for _, cs, _ in lanes)
        gt = _hg_gates(q_raw, _hg_read(f_ref, lanes), lb_v, rows, mask)
        v_b = _hg_read(v_ref, lanes).astype(BF16)
        a_f = gt["qs"] * gt["e_a"]
        b_f = gt["k"] * gt["e_b"]
        qi_f = gt["qs"] * gt["e_q"]
        ko_f = gt["k"] * gt["e_k"]
        a_b, b_b, qi_b, ko_b = a_f.astype(BF16), b_f.astype(BF16), qi_f.astype(BF16), ko_f.astype(BF16)

        o = _hg_read(o_ref, lanes)
        nw_v = nw_ref[...]
        g = _hg_read(g_ref, lanes)
        dyv = _hg_read(dy_ref, lanes)
        r = _rsqrt(_mean(o * o, axis=-1, keepdims=True) + RMS_EPS)
        sgg = _lsigmoid(g)
        d_g = dyv * (o * r * nw_v) * (sgg * (1.0 + g * (1.0 - sgg)))
        d_on = dyv * (g * sgg)
        dnw_ref[...] += _sum(d_on * o * r, axis=0, keepdims=True).total()
        tt = d_on * nw_v
        d_o = r * tt - o * (r * r * r) * _mean(tt * o, axis=-1, keepdims=True)
        do_b = d_o.astype(BF16)

        sc_b = _where(mask, _ldot_nt(a_b, b_b), 0.0).astype(BF16)
        dsc_b = _where(mask, _ldot_nt(do_b, v_b), 0.0).astype(BF16)
        d_v = _ldot_tn(sc_b, do_b)
        d_a = _ldot(dsc_b, b_b)
        d_bm = _ldot_tn(dsc_b, a_b)

        ds = _Lanes(ds_scr[i] for i in range(len(lanes)))
        dqi_parts, dko_parts, dvi_parts, ddec_parts = [None] * nc, [None] * nc, [None] * nc, [None] * nc
        for n in reversed(range(nc)):
            sl = slice(n * CHUNK, (n + 1) * CHUNK)
            dec_n = gt["dec"][n * CHUNK:n * CHUNK + 1]
            ds_b = ds.astype(BF16)
            s_n = _Lanes(st_ref[hh, b, n] for hh, _, b in lanes)
            dqi_parts[n] = _ldot(do_b[sl], s_n)
            dko_parts[n] = _ldot(v_b[sl], ds_b)
            dvi_parts[n] = _ldot_nt(ko_b[sl], ds_b)
            d_dec = _sum(ds * s_n.astype(F32), axis=0, keepdims=True)
            ddec_parts[n] = _bcast(d_dec * dec_n, (CHUNK, HEAD_DIM))
            ds = ds * dec_n + _ldot_tn(do_b[sl], qi_b[sl])
        for i, a in enumerate(ds.xs):
            ds_scr[i] = a
        d_qi = _concat(dqi_parts, 0)
        d_ko = _concat(dko_parts, 0)
        d_v = d_v + _concat(dvi_parts, 0)

        d_qs = d_a * gt["e_a"] + d_qi * gt["e_q"]
        d_k = d_bm * gt["e_b"] + d_ko * gt["e_k"]
        t_a, t_b, t_q, t_k = d_a * a_f, d_bm * b_f, d_qi * qi_f, d_ko * ko_f
        d_bref = _chunk_total(t_b - t_a, rows)
        d_blast = _chunk_total(t_k, rows) + _concat(ddec_parts, 0)
        pos = lax.broadcasted_iota(jnp.int32, (rows, HEAD_DIM), 0) % CHUNK
        d_bc = (t_a - t_b + t_q - t_k + _where(pos == CHUNK // 2 - 1, d_bref, 0.0)
                + _where(pos == CHUNK - 1, d_blast, 0.0))
        d_logf = _mask_sum(gt["tri"], d_bc, transpose=True)

        sg, sg_neg = gt["sg"], gt["sg_neg"]
        inv_f = 1.0 / gt["forget"]
        common = (1.0 - lb_v) * sg * sg_neg
        d_fl = common * (d_logf * inv_f - d_k)
        d_lb = _sum(sg_neg * (d_logf * inv_f - d_k), axis=0, keepdims=True)
        for (_, cs, _), a in zip(lanes, d_lb.xs):
            dlb_ref[:, cs] += a
        sq = gt["sq"]
        _hg_write(dh_ref, lanes, (d_qs * (sq * (1.0 + q_raw * (1.0 - sq)))).astype(BF16), 0)
        _hg_write(dh_ref, lanes, d_fl.astype(BF16), WIDTH)
        _hg_write(dh_ref, lanes, d_v.astype(BF16), 2 * WIDTH)
        _hg_write(dh_ref, lanes, d_g.astype(BF16), 3 * WIDTH)

    assert HG_GROUP == HEADS, "the combined gradient block needs all heads in one grid step"
    wide = HG_GROUP * HEAD_DIM

    def col(base):
        return pl.BlockSpec((bsz, rows, wide), lambda h, j: (0, nt - 1 - j, base // wide + h))

    tile = pl.BlockSpec((bsz, rows, wide), lambda h, j: (0, nt - 1 - j, h))
    head_vec = pl.BlockSpec((1, wide), lambda h, j: (0, h))
    p3 = proj.reshape(bsz, seq, IN_COLS)
    outs, recv = _call(
        body, "hgrn_bwd", (HEADS // HG_GROUP, nt),
        [col(C_HQ), tile, col(C_HI), col(C_HG), head_vec, _resident((1, HEAD_DIM)), tile,
         pl.BlockSpec((HG_GROUP, bsz, nc, HEAD_DIM, HEAD_DIM), lambda h, j: (h, 0, nt - 1 - j, 0, 0)), tile],
        [pl.BlockSpec((bsz, rows, 4 * WIDTH), lambda h, j: (0, nt - 1 - j, 0)), head_vec, _resident((1, HEAD_DIM))],
        [jax.ShapeDtypeStruct((bsz, seq, 4 * WIDTH), BF16), jax.ShapeDtypeStruct((1, WIDTH), F32),
         jax.ShapeDtypeStruct((1, HEAD_DIM), F32)],
        (p3, hf.reshape(bsz, seq, WIDTH), p3, p3, lb, nw, o_pre.reshape(bsz, seq, WIDTH), states,
         dy.reshape(bsz, seq, WIDTH)),
        scratch=[pltpu.VMEM((len(lanes), HEAD_DIM, HEAD_DIM), F32)],
        sem=("arbitrary", "arbitrary"), carry=carry)
    outs = [outs[0].reshape(t, 4 * WIDTH), outs[1], outs[2]]
    return outs if carry is None else (outs, recv)


def _mem_kv(mem2d, w_k, w_v):
    rows = mem2d.shape[0]

    def body(m_ref, wk_ref, wv_ref, k_ref, v_ref):
        m_b = m_ref[...].astype(BF16)
        k_ref[...] = _dot(m_b, wk_ref[...]).astype(BF16)
        v_ref[...] = _dot(m_b, wv_ref[...]).astype(BF16)

    return pl.pallas_call(
        body, name="mem_kv", grid=(rows // MEM_LEN,),
        in_specs=[pl.BlockSpec((MEM_LEN, D_MODEL), lambda i: (i, 0)), _resident((D_MODEL, WIDTH)),
                  _resident((D_MODEL, WIDTH))],
        out_specs=[pl.BlockSpec((MEM_LEN, WIDTH), lambda i: (i, 0))] * 2,
        out_shape=[jax.ShapeDtypeStruct((rows, WIDTH), BF16)] * 2,
        compiler_params=_params("parallel"),
    )(mem2d, w_k, w_v)


def _softmax_rows(s):
    m = _lift(jnp.max)(s, axis=-1, keepdims=True)
    e = _exp(s - m)
    return e / _sum(e, axis=-1, keepdims=True)


def _attn_fwd(proj, mk, mv, *, tm, seq):
    t = proj.shape[0]
    per_b = seq // tm
    scale = HEAD_DIM ** -0.5

    def body(q_ref, k_ref, v_ref, y_ref):
        heads = [slice(h * HEAD_DIM, (h + 1) * HEAD_DIM) for h in range(HEADS)]
        q_b = _Lanes(q_ref[:, sl] for sl in heads).astype(BF16)
        p = _softmax_rows(_ldot_nt(q_b, _Lanes(k_ref[:, sl] for sl in heads)) * scale)
        out = _ldot(p.astype(BF16), _Lanes(v_ref[:, sl] for sl in heads))
        y_ref[...] = jnp.concatenate(out.xs, axis=-1).astype(BF16)

    kv = pl.BlockSpec((MEM_LEN, WIDTH), lambda i: (i // per_b, 0))
    return pl.pallas_call(
        body, name="attn_fwd", grid=(t // tm,),
        in_specs=[pl.BlockSpec((tm, WIDTH), lambda i: (i, C_MQ // WIDTH)), kv, kv],
        out_specs=pl.BlockSpec((tm, WIDTH), lambda i: (i, 0)),
        out_shape=jax.ShapeDtypeStruct((t, WIDTH), BF16),
        compiler_params=_params("parallel"),
    )(proj, mk, mv)


def _attn_bwd(proj, mk, mv, dy, *, tm, seq):
    t = proj.shape[0]
    per_b = seq // tm
    scale = HEAD_DIM ** -0.5

    def body(q_ref, k_ref, v_ref, dy_ref, dq_ref, dk_ref, dv_ref):
        i = pl.program_id(0)

        @pl.when(i % per_b == 0)
        def _():
            dk_ref[...] = jnp.zeros_like(dk_ref)
            dv_ref[...] = jnp.zeros_like(dv_ref)

        heads = [slice(h * HEAD_DIM, (h + 1) * HEAD_DIM) for h in range(HEADS)]
        q_b = _Lanes(q_ref[:, sl] for sl in heads).astype(BF16)
        k_b, v_b = _Lanes(k_ref[:, sl] for sl in heads), _Lanes(v_ref[:, sl] for sl in heads)
        p = _softmax_rows(_ldot_nt(q_b, k_b) * scale)
        dy_b = _Lanes(dy_ref[:, sl] for sl in heads).astype(BF16)
        dp = _ldot_nt(dy_b, v_b)
        d_v = _ldot_tn(p.astype(BF16), dy_b)
        ds_b = (p * (dp - _sum(dp * p, axis=-1, keepdims=True)) * scale).astype(BF16)
        dq_ref[...] = jnp.concatenate(_ldot(ds_b, k_b).xs, axis=-1).astype(BF16)
        dk_ref[...] += jnp.concatenate(_ldot_tn(ds_b, q_b).xs, axis=-1)
        dv_ref[...] += jnp.concatenate(d_v.xs, axis=-1)

    kv = pl.BlockSpec((MEM_LEN, WIDTH), lambda i: (i // per_b, 0))
    tile = pl.BlockSpec((tm, WIDTH), lambda i: (i, 0))
    n_mem = mk.shape[0]
    return pl.pallas_call(
        body, name="attn_bwd", grid=(t // tm,),
        in_specs=[pl.BlockSpec((tm, WIDTH), lambda i: (i, C_MQ // WIDTH)), kv, kv, tile],
        out_specs=[tile, kv, kv],
        out_shape=[jax.ShapeDtypeStruct((t, WIDTH), BF16), jax.ShapeDtypeStruct((n_mem, WIDTH), F32),
                   jax.ShapeDtypeStruct((n_mem, WIDTH), F32)],
        compiler_params=_params("arbitrary"),
    )(proj, mk, mv, dy)


HALO = 16


def _shift_down(u, halo, k, row):
    out = pltpu.roll(u, k, 0)
    for m in range(k):
        out = jnp.where(row == m, halo[HALO - k + m:HALO - k + m + 1, :], out)
    return out


def _shift_up(u, halo, k, row, tm):
    out = pltpu.roll(u, tm - k, 0)
    for m in range(k):
        out = jnp.where(row == tm - k + m, halo[m:m + 1, :], out)
    return out


def _merge_fwd(proj, y_b, y_c, conv_w, w_branch, b_gate, *, tm, seq, carry=None):
    t = proj.shape[0]
    per_b = seq // tm
    hb = tm // HALO

    def body(cb_ref, cc_ref, ch_ref, cch_ref, chh_ref, ga_ref, gb_ref, gc_ref, yb_ref, yc_ref, cw_ref, wb_ref,
             bg_ref, ya_ref, pa_ref, pb_ref, pc_ref, mg_ref, sa_ref, sb_ref, sc_ref):
        i = pl.program_id(0)
        row = lax.broadcasted_iota(jnp.int32, (tm, WIDTH), 0)
        u = cc_ref[...].astype(F32) * ch_ref[...].astype(F32)
        halo = jnp.where(i % per_b == 0, 0.0, cch_ref[...].astype(F32) * chh_ref[...].astype(F32))
        cw = cw_ref[...]
        y = cw[0:1] * _shift_down(u, halo, 2, row) + cw[1:2] * _shift_down(u, halo, 1, row) + cw[2:3] * u
        ya_b = (cb_ref[...].astype(F32) * y).astype(BF16)
        ya_ref[...] = ya_b
        merged = None
        for idx, (y_in, g_ref, p_ref, s_ref) in enumerate(((ya_b, ga_ref, pa_ref, sa_ref),
                                                            (yb_ref[...], gb_ref, pb_ref, sb_ref),
                                                            (yc_ref[...], gc_ref, pc_ref, sc_ref))):
            p = _dot(y_in, wb_ref[idx])
            p_ref[...] = p.astype(BF16)
            sg = _sigmoid(g_ref[...].astype(F32) + bg_ref[:, idx * D_MODEL:(idx + 1) * D_MODEL])
            s_ref[...] = sg.astype(BF16)
            term = sg * p
            merged = term if merged is None else merged + term
        mg_ref[...] = merged.astype(BF16)

    def half(c):
        return pl.BlockSpec((tm, WIDTH), lambda i: (i, c // WIDTH))

    def prev(c):
        return pl.BlockSpec((HALO, WIDTH), lambda i: (jnp.maximum(i * hb - 1, 0), c // WIDTH))

    def gate(k):
        return pl.BlockSpec((tm, D_MODEL), lambda i: (i, C_GA // D_MODEL + k))

    tile512 = pl.BlockSpec((tm, WIDTH), lambda i: (i, 0))
    tile1k = pl.BlockSpec((tm, D_MODEL), lambda i: (i, 0))
    outs, recv = _call(
        body, "merge_fwd", (t // tm,),
        [half(C_CB), half(C_CC), half(C_CH), prev(C_CC), prev(C_CH), gate(0), gate(1), gate(2),
         tile512, tile512, _resident((CONV_K, WIDTH)), _resident((3, WIDTH, D_MODEL)), _resident((1, 3 * D_MODEL))],
        [tile512] + [tile1k] * 7,
        [jax.ShapeDtypeStruct((t, WIDTH), BF16)] + [jax.ShapeDtypeStruct((t, D_MODEL), BF16)] * 7,
        (proj, proj, proj, proj, proj, proj, proj, proj, y_b, y_c, conv_w, w_branch, b_gate),
        sem=("parallel",), carry=carry)
    return outs if carry is None else (outs, recv)


def _merge_bwd(dmerged, projections, gates, w_branch, *, tm):
    t = dmerged.shape[0]

    def body(dm_ref, pa_ref, pb_ref, pc_ref, sa_ref, sb_ref, sc_ref, wb_ref,
             dgt_ref, dpa_ref, dpb_ref, dpc_ref, dya_ref, dyb_ref, dyc_ref, dbg_ref):
        i = pl.program_id(0)

        @pl.when(i == 0)
        def _():
            dbg_ref[...] = jnp.zeros_like(dbg_ref)

        dm = dm_ref[...].astype(F32)
        for idx, (p_ref, s_ref, dp_ref, dy_ref) in enumerate(((pa_ref, sa_ref, dpa_ref, dya_ref),
                                                              (pb_ref, sb_ref, dpb_ref, dyb_ref),
                                                              (pc_ref, sc_ref, dpc_ref, dyc_ref))):
            cols = slice(idx * D_MODEL, (idx + 1) * D_MODEL)
            sg = s_ref[...].astype(F32)
            dp = dm * sg
            dp_b = dp.astype(BF16)
            dp_ref[...] = dp_b
            dgate = dp * p_ref[...].astype(F32) * (1.0 - sg)
            dgt_ref[:, cols] = dgate.astype(BF16)
            dbg_ref[:, cols] += jnp.sum(dgate, axis=0, keepdims=True)
            dy_ref[...] = _dot_nt(dp_b, wb_ref[idx]).astype(BF16)

    tile512 = pl.BlockSpec((tm, WIDTH), lambda i: (i, 0))
    tile1k = pl.BlockSpec((tm, D_MODEL), lambda i: (i, 0))
    return pl.pallas_call(
        body, name="merge_bwd", grid=(t // tm,),
        in_specs=[tile1k] * 7 + [_resident((3, WIDTH, D_MODEL))],
        out_specs=[pl.BlockSpec((tm, 3 * D_MODEL), lambda i: (i, 0)), tile1k, tile1k, tile1k,
                   tile512, tile512, tile512, _resident((1, 3 * D_MODEL))],
        out_shape=[jax.ShapeDtypeStruct((t, 3 * D_MODEL), BF16)] + [jax.ShapeDtypeStruct((t, D_MODEL), BF16)] * 3
                  + [jax.ShapeDtypeStruct((t, WIDTH), BF16)] * 3 + [jax.ShapeDtypeStruct((1, 3 * D_MODEL), F32)],
        compiler_params=_params("arbitrary"),
    )(dmerged, *projections, *gates, w_branch)


def _conv_bwd(proj, dya, conv_w, *, tm, seq):
    t = proj.shape[0]
    per_b = seq // tm
    hb = tm // HALO
    last_blk = t // HALO - 1

    def body(cb_ref, cc_ref, ch_ref, cch_ref, chh_ref, dya_ref, cbn_ref, dyan_ref, cw_ref, d_ref, dcw_ref):
        i = pl.program_id(0)

        @pl.when(i == 0)
        def _():
            dcw_ref[...] = jnp.zeros_like(dcw_ref)

        row = lax.broadcasted_iota(jnp.int32, (tm, WIDTH), 0)
        cb, cc, ch = cb_ref[...].astype(F32), cc_ref[...].astype(F32), ch_ref[...].astype(F32)
        u = cc * ch
        halo = jnp.where(i % per_b == 0, 0.0, cch_ref[...].astype(F32) * chh_ref[...].astype(F32))
        u1 = _shift_down(u, halo, 1, row)
        u2 = _shift_down(u, halo, 2, row)
        cw = cw_ref[...]
        y = cw[0:1] * u2 + cw[1:2] * u1 + cw[2:3] * u
        dya = dya_ref[...].astype(F32)
        dy = dya * cb
        nxt = jnp.where(i % per_b == per_b - 1, 0.0, dyan_ref[...].astype(F32) * cbn_ref[...].astype(F32))
        du = cw[2:3] * dy + cw[1:2] * _shift_up(dy, nxt, 1, row, tm) + cw[0:1] * _shift_up(dy, nxt, 2, row, tm)
        d_ref[:, 0:WIDTH] = (dya * y).astype(BF16)
        d_ref[:, WIDTH:2 * WIDTH] = (du * ch).astype(BF16)
        d_ref[:, 2 * WIDTH:3 * WIDTH] = (du * cc).astype(BF16)
        dcw_ref[0:1, :] += jnp.sum(dy * u2, axis=0, keepdims=True)
        dcw_ref[1:2, :] += jnp.sum(dy * u1, axis=0, keepdims=True)
        dcw_ref[2:3, :] += jnp.sum(dy * u, axis=0, keepdims=True)

    def half(c):
        return pl.BlockSpec((tm, WIDTH), lambda i: (i, c // WIDTH))

    def prev(c):
        return pl.BlockSpec((HALO, WIDTH), lambda i: (jnp.maximum(i * hb - 1, 0), c // WIDTH))

    def nxt(c):
        return pl.BlockSpec((HALO, WIDTH), lambda i: (jnp.minimum((i + 1) * hb, last_blk), c // WIDTH))

    return pl.pallas_call(
        body, name="conv_bwd", grid=(t // tm,),
        in_specs=[half(C_CB), half(C_CC), half(C_CH), prev(C_CC), prev(C_CH),
                  pl.BlockSpec((tm, WIDTH), lambda i: (i, 0)), nxt(C_CB), nxt(0), _resident((CONV_K, WIDTH))],
        out_specs=[pl.BlockSpec((tm, 3 * WIDTH), lambda i: (i, 0)), _resident((CONV_K, WIDTH))],
        out_shape=[jax.ShapeDtypeStruct((t, 3 * WIDTH), BF16), jax.ShapeDtypeStruct((CONV_K, WIDTH), F32)],
        compiler_params=_params("arbitrary"),
    )(proj, proj, proj, proj, proj, dya, proj, dya, conv_w)


def _loss_head(y, target, *, tm):
    t = y.shape[0]

    def body(y_ref, t_ref, dy_ref, l_ref):
        @pl.when(pl.program_id(0) == 0)
        def _():
            l_ref[...] = jnp.zeros_like(l_ref)

        err = y_ref[...] - t_ref[...]
        dy_ref[...] = err * (1.0 / D_MODEL)
        per_row = jnp.sum(err * err, axis=-1, keepdims=True) * (1.0 / D_MODEL)
        l_ref[...] += 0.5 * jnp.sum(per_row, axis=0, keepdims=True)

    row = pl.BlockSpec((tm, D_MODEL), lambda i: (i, 0))
    return pl.pallas_call(
        body, name="loss_head", grid=(t // tm,),
        in_specs=[row, row], out_specs=[row, _resident((8, 128))],
        out_shape=[jax.ShapeDtypeStruct((t, D_MODEL), F32), jax.ShapeDtypeStruct((8, 128), F32)],
        compiler_params=_params("arbitrary"),
    )(y, target)


def _lb_softmax(lower_bounds):
    x = lower_bounds
    e = jnp.exp(x - jnp.max(x, axis=0, keepdims=True))
    return e / jnp.sum(e, axis=0, keepdims=True)


def _lb_fwd(lower_bounds):
    def body(x_ref, o_ref):
        s = _lb_softmax(x_ref[...])
        c = s[0:1]
        o_ref[0:1, :] = c - s[0:1]
        for l in range(1, DEPTH):
            c = c + s[l:l + 1]
            o_ref[l:l + 1, :] = c - s[0:1]

    return pl.pallas_call(body, name="lb_fwd", out_shape=jax.ShapeDtypeStruct(lower_bounds.shape, F32))(lower_bounds)


def _lb_bwd(lower_bounds, d_lb_all):
    def body(x_ref, d_ref, o_ref):
        s = _lb_softmax(x_ref[...])
        d = d_ref[...]
        rows = [jnp.zeros_like(d[0:1])]
        for j in range(1, DEPTH):
            acc = d[j:j + 1]
            for l in range(j + 1, DEPTH):
                acc = acc + d[l:l + 1]
            rows.append(acc)
        inner = rows[0] * s[0:1]
        for j in range(1, DEPTH):
            inner = inner + rows[j] * s[j:j + 1]
        for j in range(DEPTH):
            o_ref[j:j + 1, :] = s[j:j + 1] * (rows[j] - inner)

    return pl.pallas_call(body, name="lb_bwd", out_shape=jax.ShapeDtypeStruct(lower_bounds.shape, F32))(
        lower_bounds, d_lb_all)


def _adamw(w, g, m, v):
    m2 = ADAM_B1 * m + (1.0 - ADAM_B1) * g
    v2 = ADAM_B2 * v + (1.0 - ADAM_B2) * (g * g)
    m_hat = m2 / (1.0 - ADAM_B1 ** ADAM_STEP)
    v_hat = v2 / (1.0 - ADAM_B2 ** ADAM_STEP)
    delta = -ADAM_LR * (m_hat / (jnp.sqrt(v_hat) + ADAM_EPS) + ADAM_WD * w)
    return delta, m2, v2


def _adam_small(name, g, w, m, v):
    shape = w.shape
    flat = (-1, shape[-1])
    g2, w2, m2, v2 = (a.reshape(flat) for a in (g, w, m, v))

    def body(g_ref, w_ref, m_ref, v_ref, d_ref, mo_ref, vo_ref):
        d, mm, vv = _adamw(w_ref[...], g_ref[...], m_ref[...], v_ref[...])
        d_ref[...] = d
        mo_ref[...] = mm
        vo_ref[...] = vv

    outs = pl.pallas_call(body, name=name, out_shape=[jax.ShapeDtypeStruct(w2.shape, F32)] * 3)(g2, w2, m2, v2)
    return [o.reshape(shape) for o in outs]


def _adam_shard(name, recvs, w, m, v, *, tr):
    _, r, c = w.shape

    def body(*refs):
        rc, (w_ref, m_ref, v_ref), (g_ref, d_ref, mo_ref, vo_ref) = refs[:DEPTH], refs[DEPTH:DEPTH + 3], refs[DEPTH + 3:]
        layer = pl.program_id(0)
        for cand in range(DEPTH):
            @pl.when(layer == cand)
            def _():
                g = rc[cand][0].astype(F32)
                for d in range(1, N_DEV):
                    g = g + rc[cand][d].astype(F32)
                dl, mm, vv = _adamw(w_ref[...], g, m_ref[...], v_ref[...])
                g_ref[...] = g
                d_ref[...] = dl
                mo_ref[...] = mm
                vo_ref[...] = vv

    def recv_spec(cand):
        return pl.BlockSpec((N_DEV, tr, c), lambda l, i: (0, jnp.where(l == cand, i, 0), 0))

    tile = pl.BlockSpec((None, tr, c), lambda l, i: (l, i, 0))
    return pl.pallas_call(
        body, name=name, grid=(DEPTH, r // tr),
        in_specs=[recv_spec(cand) for cand in range(DEPTH)] + [tile] * 3,
        out_specs=[tile] * 4,
        out_shape=[jax.ShapeDtypeStruct(w.shape, F32)] * 4,
        compiler_params=_params("parallel", "parallel"),
    )(*recvs, w, m, v)


def _sum_devices(name, x):
    def body(x_ref, o_ref):
        acc = x_ref[0]
        for d in range(1, N_DEV):
            acc = acc + x_ref[d]
        o_ref[...] = acc

    return pl.pallas_call(body, name=name, out_shape=jax.ShapeDtypeStruct(x.shape[1:], x.dtype))(x)


SMALL = (("lower_bounds", 1, 512), ("conv_w", CONV_K, WIDTH), ("hg_norm_w", 1, HEAD_DIM), ("b_gate", 3, D_MODEL),
         ("ln1_g", 1, D_MODEL), ("ln1_b", 1, D_MODEL), ("ln2_g", 1, D_MODEL), ("ln2_b", 1, D_MODEL))
SMALL_ROWS = sum(r for _, r, _ in SMALL)


def _pack_small(per_layer):
    flat = [a for layer in per_layer for a in layer]

    def body(*refs):
        ins, o_ref = refs[:-1], refs[-1]
        o_ref[...] = jnp.zeros_like(o_ref)
        it = iter(ins)
        for l in range(DEPTH):
            row = l * SMALL_ROWS
            for name, nrows, ncols in SMALL:
                ref = next(it)
                if name == "b_gate":
                    for k in range(nrows):
                        o_ref[row + k:row + k + 1, :] = ref[:, k * ncols:(k + 1) * ncols]
                else:
                    o_ref[row:row + nrows, 0:ncols] = ref[...]
                row += nrows

    return pl.pallas_call(body, name="pack_small_grads",
                          out_shape=jax.ShapeDtypeStruct((DEPTH * SMALL_ROWS, D_MODEL), F32))(*flat)


def _unpack_small(summed):
    s3 = summed.reshape(DEPTH, SMALL_ROWS, D_MODEL)
    out, row = {}, 0
    for name, nrows, ncols in SMALL:
        out[name] = s3[:, row:row + nrows, :ncols].reshape(DEPTH, nrows * ncols)
        row += nrows
    return out


def _natural_cols(g):
    nd = g.ndim
    perm = tuple(range(1, nd - 1)) + (0, nd - 1)
    t = jnp.transpose(g, perm)
    return t.reshape(t.shape[:-2] + (t.shape[-2] * t.shape[-1],))


def _natural_rows(g):
    return g.reshape(g.shape[0] * g.shape[1], g.shape[2])


def _hosted(hosts, key, fn):
    pairs = hosts.get(key) if hosts else None
    if callable(pairs):
        pairs = pairs()
    if not pairs:
        return fn(None)
    outs, recvs = fn([ex for ex, _ in pairs])
    for (_, hook), recv in zip(pairs, recvs):
        hook(recv)
    return outs


def _layer_fwd(cur, cur_b, mem2d, wl, *, bsz, seq, hosts=None):
    tm = min(512, seq)
    proj, hf = _hosted(hosts, "in_proj", lambda c: _in_proj(cur_b, wl["w_in"], tm=min(1024, seq), carry=c))
    y_b, o_pre, states = _hosted(hosts, "hgrn_fwd", lambda c: _hgrn_fwd(proj, hf, wl["lb"], wl["nw"], bsz=bsz,
                                                                         seq=seq, carry=c))
    mk, mv = _mem_kv(mem2d, wl["w_mk"], wl["w_mv"])
    y_c = _attn_fwd(proj, mk, mv, tm=tm, seq=seq)
    y_a, pa, pb, pc, merged, sga, sgb, sgc = _hosted(
        hosts, "merge_fwd", lambda c: _merge_fwd(proj, y_b, y_c, wl["conv"], wl["w_br"], wl["b_gate"], tm=tm,
                                                 seq=seq, carry=c))
    z1, x1, x1_b = _hosted(hosts, "wo_ln", lambda c: _linear_ln("wo_ln", merged, wl["w_o"], cur, wl["ln1_g"],
                                                                  wl["ln1_b"], tm=tm, carry=c))
    hid = _hosted(hosts, "mlp_up", lambda c: _mm_nn("mlp_up", x1_b, wl["w_up"], tm=min(1024, seq), tn=2048,
                                                     out_dtype=BF16, relu2=True, carry=c))
    z2, x2, x2_b = _linear_ln("down_ln", hid, wl["w_down"], x1, wl["ln2_g"], wl["ln2_b"], tm=tm)
    return dict(x_b=cur_b, proj=proj, hf=hf, y_a=y_a, y_b=y_b, y_c=y_c, o_pre=o_pre, states=states, mk=mk, mv=mv,
                proj3=(pa, pb, pc), gates3=(sga, sgb, sgc), merged=merged, z1=z1, x1_b=x1_b, hid=hid, z2=z2, x2=x2,
                x2_b=x2_b)


def _layer_bwd(dcur, mem2d, s, wl, *, bsz, seq, plan=None):
    tm = min(512, seq)
    tk = min(2048, bsz * seq)
    g = {}

    def run(key, fn):
        made = plan[key](g) if plan and key in plan else None
        return _hosted({key: [made]} if made else None, key, fn)

    dz2, dz2_b, dhpre, d_ln2g, d_ln2b = run(
        "ln2_bwd_down", lambda c: _ln_bwd_mm_nt("ln2_bwd_down", dcur, s["z2"], wl["ln2_g"], wl["w_down"],
                                                s["hid"], tm=tm, tn=1024, carry=c))
    g["w_down"] = _mm_tn("grad_w_down", s["hid"], dz2_b, tk=tk, tmo=1024, tno=1024)
    dx1 = _mm_nt_sum("mlp_up_bwd", [dhpre], [0], wl["w_up"], dz2, tm=tm)
    g["w_up"] = run("grad_w_up", lambda c: _mm_tn("grad_w_up", s["x1_b"], dhpre, tk=tk, tmo=1024, tno=2048, carry=c))
    dz1, dz1_b, dmerged, d_ln1g, d_ln1b = _ln_bwd_mm_nt("ln1_bwd_wo", dx1, s["z1"], wl["ln1_g"], wl["w_o"],
                                                        tm=tm, tn=1024)
    g["w_o"] = _mm_tn("grad_w_o", s["merged"], dz1_b, tk=tk, tmo=1024, tno=1024)
    dgate, dpa, dpb, dpc, dya, dyb, dyc, d_bg = _merge_bwd(dmerged, s["proj3"], s["gates3"], wl["w_br"], tm=tm)
    g["w_br"] = jnp.stack([_mm_tn("grad_w_branch", yy, dp, tk=tk, tmo=512, tno=1024)
                           for yy, dp in ((s["y_a"], dpa), (s["y_b"], dpb), (s["y_c"], dpc))])
    d_conv, d_cw = _conv_bwd(s["proj"], dya, wl["conv"], tm=tm, seq=seq)
    dhg, d_lb, d_nw = run(
        "hgrn_bwd", lambda c: _hgrn_bwd(s["proj"], s["hf"], wl["lb"], wl["nw"], s["o_pre"], s["states"], dyb,
                                        bsz=bsz, seq=seq, carry=c))
    dmq, dmk, dmv = _attn_bwd(s["proj"], s["mk"], s["mv"], dyc, tm=tm, seq=seq)
    tkm = min(512, mem2d.shape[0])
    g["w_mk"] = _mm_tn("grad_w_mem", mem2d, dmk, tk=tkm, tmo=1024, tno=512)
    g["w_mv"] = _mm_tn("grad_w_mem", mem2d, dmv, tk=tkm, tmo=1024, tno=512)
    pieces = [d_conv, dhg, dmq, dgate]
    offsets = [C_CB, C_HQ, C_MQ, C_GA]
    g["w_in"] = jnp.concatenate(
        [_mm_tn("grad_w_in_%d" % p.shape[1], s["x_b"], p, tk=tk, tmo=1024,
                tno=next(w for w in (2048, 1536, 512) if p.shape[1] % w == 0)) for p in pieces], axis=1)
    dx = run("in_proj_bwd", lambda c: _mm_nt_sum("in_proj_bwd", pieces, offsets, wl["w_in"], dz1,
                                                 tm=min(512, seq), carry=c))
    return dx, g, [d_lb, d_cw, d_nw, d_bg, d_ln1g, d_ln1b, d_ln2g, d_ln2b]


def kernel(x, mem, lower_bounds, w_in, conv_w, hg_norm_w, w_mem_k, w_mem_v, w_branch, b_gate, w_o, ln1_g, ln1_b, w_up, w_down, ln2_g, ln2_b, loss_target, m_lower_bounds, m_w_in, m_conv_w, m_hg_norm_w, m_w_mem_k, m_w_mem_v, m_w_branch, m_b_gate, m_w_o, m_ln1_g, m_ln1_b, m_w_up, m_w_down, m_ln2_g, m_ln2_b, v_lower_bounds, v_w_in, v_conv_w, v_hg_norm_w, v_w_mem_k, v_w_mem_v, v_w_branch, v_b_gate, v_w_o, v_ln1_g, v_ln1_b, v_w_up, v_w_down, v_ln2_g, v_ln2_b):
    bsz, seq, _ = x.shape
    t = bsz * seq
    me = _my_id()

    sh = dict(w_in=w_in.astype(BF16), w_mk=w_mem_k.astype(BF16), w_mv=w_mem_v.astype(BF16),
              w_br=w_branch.astype(BF16), w_o=w_o.astype(BF16), w_up=w_up.astype(BF16), w_down=w_down.astype(BF16))
    half_rows = D_MODEL // 2
    sh["w_in_a"], sh["w_in_b"] = sh["w_in"][:, :half_rows], sh["w_in"][:, half_rows:]
    natural = dict(w_in=_natural_cols, w_in_a=_natural_cols, w_in_b=_natural_cols, w_mk=_natural_rows,
                   w_mv=_natural_rows, w_br=_natural_cols, w_o=_natural_rows, w_up=_natural_cols,
                   w_down=_natural_rows)

    lb_all = _lb_fwd(lower_bounds)
    layer_w = [dict(lb=lb_all[l][None], nw=hg_norm_w[l][None], b_gate=b_gate[l][None], ln1_g=ln1_g[l][None],
                    ln1_b=ln1_b[l][None], ln2_g=ln2_g[l][None], ln2_b=ln2_b[l][None]) for l in range(DEPTH)]
    half_full = {}

    def near(names, l):
        srcs = [sh[n][l] for n in names]
        ex = _Exchange(srcs, piece_shapes=[s_.shape for s_ in srcs], route="near")
        return ex, lambda recv_: half_full.update({(n, l): r for n, r in zip(names, recv_)})

    def relay(names, l):
        ex = _Exchange([half_full.pop((n, l)) for n in names], route="relay")

        def hook(recv_):
            for n, r in zip(names, recv_):
                layer_w[l][n] = natural[n](r)
        return ex, hook

    small4 = ["w_mk", "w_mv", "w_br", "w_o"]
    conv_shard = conv_w.reshape(DEPTH * CONV_K * (WIDTH // N_DEV) // 128, 128)
    first = near(["w_in"], 0)
    conv_ex = _Exchange([conv_shard], piece_shapes=[conv_shard.shape])
    got = _exchange("gather_first", [first[0], conv_ex])
    first[1](got[0])
    conv_full = _natural_cols(got[1][0].reshape(N_DEV, DEPTH, CONV_K, WIDTH // N_DEV))
    second = relay(["w_in"], 0)
    second[1](_exchange("relay_first", [second[0]])[0])

    x2d = x.reshape(t, D_MODEL)
    mem2d = mem.reshape(bsz * MEM_LEN, D_MODEL)
    target2d = loss_target.reshape(t, D_MODEL)

    saved = []
    cur, cur_b = x2d, x2d.astype(BF16)
    for l in range(DEPTH):
        wl = layer_w[l]
        wl["conv"] = conv_full[l]
        more = l + 1 < DEPTH
        now = ["w_up", "w_down"] + ([] if l else small4)
        hosts = {"in_proj": [near(now, l)],
                 "hgrn_fwd": lambda l=l, more=more, now=now: [relay(now, l)] + ([near(["w_in"], l + 1)] if more else [])}
        if more:
            hosts["merge_fwd"] = lambda l=l: [relay(["w_in"], l + 1), near(small4, l + 1)]
            hosts["mlp_up"] = lambda l=l: [relay(small4, l + 1)]
        s = _layer_fwd(cur, cur_b, mem2d, wl, bsz=bsz, seq=seq, hosts=hosts)
        saved.append(s)
        cur, cur_b = s["x2"], s["x2_b"]

    dcur, loss_tile = _loss_head(cur, target2d, tm=min(512, seq))
    loss = lax.psum(loss_tile[0, 0], ("x", "y", "c"))

    in_w = IN_COLS // N_DEV

    def in_half(r):
        return lambda ref, j: ref.at[pl.ds(r * half_rows, half_rows), pl.ds(j * in_w, in_w)]

    slicer = dict(w_in_a=in_half(0), w_in_b=in_half(1), w_mk=_rows(D_MODEL // N_DEV), w_mv=_rows(D_MODEL // N_DEV),
                  w_br=_cols(D_MODEL // N_DEV), w_o=_rows(D_MODEL // N_DEV), w_up=_cols(D_FF // N_DEV),
                  w_down=_rows(D_FF // N_DEV))
    source = dict(w_in_a="w_in", w_in_b="w_in")
    recv = [dict() for _ in range(DEPTH)]

    def scatter_of(names, g, into):
        ex = _Exchange([g[source.get(n, n)] for n in names], [slicer[n] for n in names],
                       [sh[n].shape[1:] for n in names])
        return ex, lambda recv_: into.update(zip(names, recv_))

    small_rows = [None] * DEPTH
    prev = None
    rest = ["w_in_b", "w_mk", "w_mv"]
    for l in reversed(range(DEPTH)):
        plan = {"grad_w_up": lambda g, l=l: scatter_of(["w_down"], g, recv[l]),
                "hgrn_bwd": lambda g, l=l: scatter_of(["w_up", "w_o", "w_br"], g, recv[l])}
        if l == 0:
            plan["in_proj_bwd"] = lambda g: scatter_of(["w_in_a"] + rest, g, recv[0])
        else:
            plan["in_proj_bwd"] = lambda g, l=l: scatter_of(["w_in_a"], g, recv[l])
        if prev is not None:
            plan["ln2_bwd_down"] = lambda g, l=l, prev=prev: scatter_of(rest, prev, recv[l + 1])
        dcur, prev, small_rows[l] = _layer_bwd(dcur, mem2d, saved[l], layer_w[l], bsz=bsz, seq=seq, plan=plan)
    for r in recv:
        r["w_in"] = jnp.concatenate([r.pop("w_in_a"), r.pop("w_in_b")], axis=1)

    packed = _pack_small(small_rows)
    all_small = _exchange("gather_small_grads", [_Exchange([packed], piece_shapes=[packed.shape])])[0][0]
    small_grads = _unpack_small(_sum_devices("sum_small_grads", all_small))
    small_grads["lower_bounds"] = _lb_bwd(lower_bounds, small_grads["lower_bounds"])
    conv_all = small_grads["conv_w"].reshape(DEPTH, CONV_K, WIDTH)
    small_grads["conv_w"] = lax.dynamic_slice_in_dim(conv_all, me * (WIDTH // N_DEV), WIDTH // N_DEV, axis=2)

    grads, deltas, new_m, new_v = {}, {}, {}, {}
    given = dict(lower_bounds=(lower_bounds, m_lower_bounds, v_lower_bounds), conv_w=(conv_w, m_conv_w, v_conv_w),
                 hg_norm_w=(hg_norm_w, m_hg_norm_w, v_hg_norm_w), b_gate=(b_gate, m_b_gate, v_b_gate),
                 ln1_g=(ln1_g, m_ln1_g, v_ln1_g), ln1_b=(ln1_b, m_ln1_b, v_ln1_b),
                 ln2_g=(ln2_g, m_ln2_g, v_ln2_g), ln2_b=(ln2_b, m_ln2_b, v_ln2_b))
    for name, (w_, m_, v_) in given.items():
        g_ = small_grads[name].reshape(w_.shape)
        grads[name] = g_
        deltas[name], new_m[name], new_v[name] = _adam_small("adam_" + name, g_, w_, m_, v_)

    big = dict(w_in=("w_in", w_in, m_w_in, v_w_in, 128), w_mem_k=("w_mk", w_mem_k, m_w_mem_k, v_w_mem_k, 128),
               w_mem_v=("w_mv", w_mem_v, m_w_mem_v, v_w_mem_v, 128),
               w_branch=("w_br", w_branch, m_w_branch, v_w_branch, 512), w_o=("w_o", w_o, m_w_o, v_w_o, 128),
               w_up=("w_up", w_up, m_w_up, v_w_up, 256), w_down=("w_down", w_down, m_w_down, v_w_down, 128))
    for name, (k, w_, m_, v_, tr) in big.items():
        shape = w_.shape
        flat = (DEPTH, -1, shape[-1])
        rc = [recv[l][k].reshape((N_DEV,) + w_.reshape(flat).shape[1:]) for l in range(DEPTH)]
        outs = _adam_shard("adam_" + name, rc, w_.reshape(flat), m_.reshape(flat), v_.reshape(flat), tr=tr)
        grads[name], deltas[name], new_m[name], new_v[name] = (o.reshape(shape) for o in outs)

    order = ["lower_bounds", "w_in", "conv_w", "hg_norm_w", "w_mem_k", "w_mem_v", "w_branch", "b_gate", "w_o",
             "ln1_g", "ln1_b", "w_up", "w_down", "ln2_g", "ln2_b"]
    return (loss, dcur.reshape(x.shape), *[grads[n] for n in order], *[deltas[n] for n in order],
            *[new_m[n] for n in order], *[new_v[n] for n in order])
```

```python
import functools

import jax
import jax.numpy as jnp
from jax import lax
from jax.experimental import pallas as pl
from jax.experimental.pallas import tpu as pltpu

F32 = jnp.float32
BF16 = jnp.bfloat16

N_DEV = 8
D_MODEL = 1024
DEPTH = 4
MEM_LEN = 256
CONV_K = 3
WIDTH = 512
HEADS = 4
HEAD_DIM = 128
CHUNK = 32
D_FF = 4 * D_MODEL
IN_COLS = 7168
ALPHA = (2.0 * DEPTH) ** 0.25
LN_EPS = 1e-5
RMS_EPS = 1e-6
ADAM_LR = 0.001
ADAM_B1 = 0.9
ADAM_B2 = 0.999
ADAM_EPS = 1e-08
ADAM_WD = 0.01
ADAM_STEP = 10

C_CB, C_CC, C_CH, C_HQ, C_HF, C_HI, C_HG, C_MQ, C_GA = 0, 512, 1024, 1536, 2048, 2560, 3072, 3584, 4096

ROWS_HG = 256
NT_DIMS = (((1,), (1,)), ((), ()))
TN_DIMS = (((0,), (0,)), ((), ()))
MESH = pl.DeviceIdType.MESH


def _dot(a, b):
    return jnp.dot(a, b, preferred_element_type=F32)


def _dot_nt(a, b):
    return lax.dot_general(a, b, NT_DIMS, preferred_element_type=F32)


def _dot_tn(a, b):
    return lax.dot_general(a, b, TN_DIMS, preferred_element_type=F32)


def _sigmoid(x):
    return 1.0 / (1.0 + jnp.exp(-x))


def _params(*sem):
    return pltpu.CompilerParams(dimension_semantics=sem)


def _resident(shape, single=False):
    nd = len(shape)
    if single:
        return pl.BlockSpec(shape, lambda *_: (0,) * nd, pipeline_mode=pl.Buffered(1))
    return pl.BlockSpec(shape, lambda *_: (0,) * nd)


def _my_id():
    return 4 * lax.axis_index("x") + 2 * lax.axis_index("y") + lax.axis_index("c")


class _Exchange:
    def __init__(self, srcs, slicers=None, piece_shapes=None, route="all"):
        self.srcs, self.n, self.route = list(srcs), len(srcs), route
        self.slicers = list(slicers) if slicers else [_whole] * self.n
        any_spec = pl.BlockSpec(memory_space=pl.ANY)
        self.in_specs = [any_spec] * self.n
        self.out_specs = [any_spec] * self.n
        if route == "relay":
            self.out_shape = [jax.ShapeDtypeStruct(a.shape, a.dtype) for a in srcs]
        else:
            self.out_shape = [jax.ShapeDtypeStruct((N_DEV,) + tuple(s), a.dtype) for s, a in zip(piece_shapes, srcs)]
        self.aliased = route == "relay"
        self.scratch = [pltpu.SemaphoreType.DMA((self.n * N_DEV,)), pltpu.SemaphoreType.DMA((self.n * N_DEV,)),
                        pltpu.SemaphoreType.DMA((self.n,))]

    def _peer(self, j, me):
        if self.route == "all":
            return j != me
        return (j != me) & ((j % 2 == lax.axis_index("c")) | (j // 2 == me // 2))

    def _remote(self, ins, outs, sems, k, j, me):
        return pltpu.make_async_remote_copy(
            src_ref=self.slicers[k](ins[k], j), dst_ref=outs[k].at[me],
            send_sem=sems[0].at[k * N_DEV + j], recv_sem=sems[1].at[k * N_DEV + me],
            device_id=(j // 4, (j // 2) % 2, j % 2), device_id_type=MESH)

    def _local(self, ins, outs, sems, k, j, me):
        return pltpu.make_async_copy(self.slicers[k](ins[k], j), outs[k].at[me], sems[2].at[k])

    def _relay(self, outs, sems, k, j):
        sibling = (lax.axis_index("x"), lax.axis_index("y"), 1 - lax.axis_index("c"))
        return pltpu.make_async_remote_copy(
            src_ref=outs[k].at[j], dst_ref=outs[k].at[j], send_sem=sems[0].at[k * N_DEV + j],
            recv_sem=sems[1].at[k * N_DEV + j], device_id=sibling, device_id_type=MESH)

    def _other_chip(self, j, same_core):
        on_my_core = j % 2 == lax.axis_index("c")
        return (on_my_core if same_core else ~on_my_core) & (j // 2 != _my_id() // 2)

    def start(self, ins, outs, sems):
        me = _my_id()
        for k in range(self.n):
            for j in range(N_DEV):
                if self.route == "relay":
                    @pl.when(self._other_chip(j, True))
                    def _():
                        self._relay(outs, sems, k, j).start()
                    continue

                @pl.when(self._peer(j, me))
                def _():
                    self._remote(ins, outs, sems, k, j, me).start()

                @pl.when(j == me)
                def _():
                    self._local(ins, outs, sems, k, j, me).start()

    def wait(self, ins, outs, sems):
        me = _my_id()
        for k in range(self.n):
            for j in range(N_DEV):
                if self.route == "relay":
                    @pl.when(self._other_chip(j, False))
                    def _():
                        self._relay(outs, sems, k, j).wait_recv()

                    @pl.when(self._other_chip(j, True))
                    def _():
                        self._relay(outs, sems, k, j).wait_send()
                    continue

                @pl.when(self._peer(j, me))
                def _():
                    pltpu.make_async_remote_copy(
                        src_ref=self.slicers[k](ins[k], j), dst_ref=outs[k].at[j],
                        send_sem=sems[0].at[k * N_DEV + j], recv_sem=sems[1].at[k * N_DEV + j],
                        device_id=(j // 4, (j // 2) % 2, j % 2), device_id_type=MESH).wait_recv()
                    self._remote(ins, outs, sems, k, j, me).wait_send()

                @pl.when(j == me)
                def _():
                    self._local(ins, outs, sems, k, j, me).wait()


def _carried(exchanges, n_in, n_out):
    c_in = [s for ex in exchanges for s in ex.in_specs]
    c_out = [s for ex in exchanges for s in ex.out_specs]
    shapes = [s for ex in exchanges for s in ex.out_shape]
    sems = [s for ex in exchanges for s in ex.scratch]
    srcs = [a for ex in exchanges for a in ex.srcs]
    aliases, off = {}, 0
    for ex in exchanges:
        if ex.aliased:
            aliases.update({n_in + off + k: n_out + off + k for k in range(ex.n)})
        off += ex.n
    total = off

    def split(refs, n_scr):
        ins, outs = refs[:n_in], refs[n_in + total:n_in + total + n_out]
        rest = refs[n_in + 2 * total + n_out:]
        scr, sem_refs = rest[:n_scr], rest[n_scr:]
        parts, off_ = [], 0
        for i, ex in enumerate(exchanges):
            parts.append((refs[n_in + off_:n_in + off_ + ex.n],
                          refs[n_in + total + n_out + off_:n_in + total + n_out + off_ + ex.n],
                          sem_refs[3 * i:3 * i + 3]))
            off_ += ex.n
        return ins, outs, scr, parts

    return c_in, c_out, shapes, sems, srcs, aliases, split


def _exchange(name, exchanges):
    c_in, c_out, shapes, sems, srcs, aliases, split = _carried(exchanges, 0, 0)

    def body(*refs):
        _, _, _, parts = split(refs, 0)
        for ex, part in zip(exchanges, parts):
            ex.start(*part)
        for ex, part in zip(exchanges, parts):
            ex.wait(*part)

    outs = pl.pallas_call(
        body, name=name, in_specs=c_in, out_specs=c_out, out_shape=shapes, scratch_shapes=sems,
        input_output_aliases=aliases, compiler_params=pltpu.CompilerParams(has_side_effects=True))(*srcs)
    return _per_exchange(exchanges, outs)


def _per_exchange(exchanges, flat):
    out, off = [], 0
    for ex in exchanges:
        out.append(flat[off:off + ex.n])
        off += ex.n
    return out


def _call(body, name, grid, in_specs, out_specs, out_shape, args, scratch=(), sem=None, carry=None):
    n_in, n_out, n_scr = len(in_specs), len(out_specs), len(scratch)
    if not carry:
        outs = pl.pallas_call(body, name=name, grid=grid, in_specs=in_specs, out_specs=out_specs,
                              out_shape=out_shape, scratch_shapes=list(scratch),
                              compiler_params=_params(*sem))(*args)
        return outs, None
    c_in, c_out, shapes, sems, srcs, aliases, split = _carried(carry, n_in, n_out)

    def hosted(*refs):
        ins, outs, scr, parts = split(refs, n_scr)
        first, last = True, True
        for d, size in enumerate(grid):
            first = first & (pl.program_id(d) == 0)
            last = last & (pl.program_id(d) == size - 1)

        @pl.when(first)
        def _():
            for ex, part in zip(carry, parts):
                ex.start(*part)

        body(*ins, *outs, *scr)

        @pl.when(last)
        def _():
            for ex, part in zip(carry, parts):
                ex.wait(*part)

    outs = pl.pallas_call(
        hosted, name=name + "_x", grid=grid, in_specs=list(in_specs) + c_in,
        out_specs=list(out_specs) + c_out, out_shape=list(out_shape) + shapes,
        scratch_shapes=list(scratch) + sems, input_output_aliases=aliases,
        compiler_params=_params(*(["arbitrary"] * len(grid))))(*args, *srcs)
    return outs[:n_out], _per_exchange(carry, outs[n_out:])


def _whole(ref, j):
    return ref


def _cols(width):
    return lambda ref, j: ref.at[(slice(None),) * (len(ref.shape) - 1) + (pl.ds(j * width, width),)]


def _rows(height):
    return lambda ref, j: ref.at[pl.ds(j * height, height)]


def _mm_nn(name, a, w, *, tm, tn, out_dtype, relu2=False, carry=None):
    t, k = a.shape
    n = w.shape[1]

    def body(a_ref, w_ref, o_ref):
        acc = _dot(a_ref[...].astype(BF16), w_ref[...])
        if relu2:
            r = jnp.maximum(acc, 0.0)
            acc = r * r
        o_ref[...] = acc.astype(out_dtype)

    outs, recv = _call(
        body, name, (t // tm, n // tn),
        [pl.BlockSpec((tm, k), lambda i, j: (i, 0)), pl.BlockSpec((k, tn), lambda i, j: (0, j))],
        [pl.BlockSpec((tm, tn), lambda i, j: (i, j))], [jax.ShapeDtypeStruct((t, n), out_dtype)], (a, w),
        sem=("parallel", "parallel"), carry=carry)
    return outs[0] if carry is None else (outs[0], recv)


def _in_proj(a, w, *, tm, carry=None):
    t, k = a.shape
    tn = IN_COLS // 4
    f_tile, f_off = C_HF // tn, C_HF % tn

    def body(a_ref, w_ref, o_ref, f_ref):
        acc = _dot(a_ref[...], w_ref[...])
        o_ref[...] = acc.astype(BF16)

        @pl.when(pl.program_id(1) == f_tile)
        def _():
            f_ref[...] = acc[:, f_off:f_off + WIDTH]

    outs, recv = _call(
        body, "in_proj", (t // tm, IN_COLS // tn),
        [pl.BlockSpec((tm, k), lambda i, j: (i, 0)), pl.BlockSpec((k, tn), lambda i, j: (0, j))],
        [pl.BlockSpec((tm, tn), lambda i, j: (i, j)), pl.BlockSpec((tm, WIDTH), lambda i, j: (i, 0))],
        [jax.ShapeDtypeStruct((t, IN_COLS), BF16), jax.ShapeDtypeStruct((t, WIDTH), F32)], (a, w),
        sem=("parallel", "arbitrary"), carry=carry)
    return outs if carry is None else (outs, recv)


def _linear_ln(name, a, w, resid, g, b, *, tm, carry=None):
    t, k = a.shape
    halves = [slice(0, tm // 2), slice(tm // 2, tm)] if k > D_MODEL else [slice(0, tm)]

    def body(a_ref, w_ref, r_ref, g_ref, b_ref, z_ref, x_ref, xb_ref):
        z = ALPHA * _Lanes(r_ref[s, :] for s in halves) + _ldot(_Lanes(a_ref[s, :] for s in halves), w_ref[...])
        zc = z - _mean(z, axis=-1, keepdims=True)
        y = zc * _rsqrt(_mean(zc * zc, axis=-1, keepdims=True) + LN_EPS) * g_ref[...] + b_ref[...]
        for s, zz, yy in zip(halves, z.xs, y.xs):
            z_ref[s, :] = zz
            x_ref[s, :] = yy
            xb_ref[s, :] = yy.astype(BF16)

    row = pl.BlockSpec((tm, D_MODEL), lambda i: (i, 0))
    outs, recv = _call(
        body, name, (t // tm,),
        [pl.BlockSpec((tm, k), lambda i: (i, 0)), _resident((k, D_MODEL)), row,
         _resident((1, D_MODEL)), _resident((1, D_MODEL))],
        [row, row, row],
        [jax.ShapeDtypeStruct((t, D_MODEL), F32)] * 2 + [jax.ShapeDtypeStruct((t, D_MODEL), BF16)],
        (a, w, resid, g, b), sem=("parallel",), carry=carry)
    return outs if carry is None else (outs, recv)


def _ln_bwd_mm_nt(name, dy, z, g, w, h=None, *, tm, tn, carry=None):
    t = dy.shape[0]
    n = w.shape[0]
    halves = [slice(0, tm // 2), slice(tm // 2, tm)]

    def body(*refs):
        if h is None:
            dy_ref, z_ref, g_ref, w_ref, dz_ref, dzb_ref, o_ref, dg_ref, db_ref = refs
        else:
            dy_ref, z_ref, g_ref, w_ref, h_ref, dz_ref, dzb_ref, o_ref, dg_ref, db_ref = refs

        @pl.when(pl.program_id(0) == 0)
        def _():
            dg_ref[...] = jnp.zeros_like(dg_ref)
            db_ref[...] = jnp.zeros_like(db_ref)

        zv = _Lanes(z_ref[s, :] for s in halves)
        dyv = _Lanes(dy_ref[s, :] for s in halves)
        mu = _mean(zv, axis=-1, keepdims=True)
        zc = zv - mu
        rstd = _rsqrt(_mean(zc * zc, axis=-1, keepdims=True) + LN_EPS)
        xh = zc * rstd
        gdy = dyv * g_ref[...]
        m1 = _mean(gdy, axis=-1, keepdims=True)
        m2 = _mean(gdy * xh, axis=-1, keepdims=True)
        dz = rstd * (gdy - m1 - xh * m2)
        dz_b = dz.astype(BF16)
        for s, a, a_b in zip(halves, dz.xs, dz_b.xs):
            dz_ref[s, :] = a
            dzb_ref[s, :] = a_b
        dg_ref[...] += _sum(dyv * xh, axis=0, keepdims=True).total()
        db_ref[...] += _sum(dyv, axis=0, keepdims=True).total()
        for c in range(n // tn):
            cols = slice(c * tn, (c + 1) * tn)
            acc = _ldot_nt(dz_b, w_ref[cols, :])
            if h is not None:
                acc = acc * (2.0 * _sqrt(_Lanes(h_ref[s, cols] for s in halves).astype(F32)))
            for s, a in zip(halves, acc.xs):
                o_ref[s, cols] = a.astype(BF16)

    row = pl.BlockSpec((tm, D_MODEL), lambda i: (i, 0))
    vec = _resident((1, D_MODEL))
    tile = pl.BlockSpec((tm, n), lambda i: (i, 0))
    in_specs = [row, row, vec, _resident((n, D_MODEL))]
    args = [dy, z, g, w]
    if h is not None:
        in_specs.append(tile)
        args.append(h)
    outs, recv = _call(
        body, name, (t // tm,), in_specs, [row, row, tile, vec, vec],
        [jax.ShapeDtypeStruct((t, D_MODEL), F32), jax.ShapeDtypeStruct((t, D_MODEL), BF16),
         jax.ShapeDtypeStruct((t, n), BF16), jax.ShapeDtypeStruct((1, D_MODEL), F32),
         jax.ShapeDtypeStruct((1, D_MODEL), F32)], args, sem=("arbitrary",), carry=carry)
    return outs if carry is None else (outs, recv)


def _mm_tn(name, a, b, *, tk, tmo, tno, carry=None):
    t, m = a.shape
    n = b.shape[1]
    nk = t // tk

    def body(a_ref, b_ref, o_ref, acc_ref):
        k = pl.program_id(2)
        p = _dot_tn(a_ref[...].astype(BF16), b_ref[...].astype(BF16))

        @pl.when(k == 0)
        def _():
            acc_ref[...] = p

        @pl.when(k > 0)
        def _():
            acc_ref[...] += p

        @pl.when(k == nk - 1)
        def _():
            o_ref[...] = acc_ref[...].astype(BF16)

    outs, recv = _call(
        body, name, (m // tmo, n // tno, nk),
        [pl.BlockSpec((tk, tmo), lambda i, j, k: (k, i)), pl.BlockSpec((tk, tno), lambda i, j, k: (k, j))],
        [pl.BlockSpec((tmo, tno), lambda i, j, k: (i, j))], [jax.ShapeDtypeStruct((m, n), BF16)], (a, b),
        scratch=[pltpu.VMEM((tmo, tno), F32)], sem=("parallel", "parallel", "arbitrary"), carry=carry)
    return outs[0] if carry is None else (outs[0], recv)


def _mm_nt_sum(name, pieces, offsets, w, resid, *, tm, carry=None):
    t = resid.shape[0]
    widths = [p.shape[1] for p in pieces]
    n_p = len(pieces)

    def body(*refs):
        p_refs, w_ref, r_ref, o_ref = refs[:n_p], refs[n_p], refs[n_p + 1], refs[n_p + 2]
        acc = ALPHA * r_ref[...]
        for p_ref, off, wd in zip(p_refs, offsets, widths):
            acc = acc + _dot_nt(p_ref[...], w_ref[:, off:off + wd])
        o_ref[...] = acc

    row = pl.BlockSpec((tm, D_MODEL), lambda i: (i, 0))
    outs, recv = _call(
        body, name, (t // tm,),
        [pl.BlockSpec((tm, wd), lambda i: (i, 0)) for wd in widths] + [_resident(w.shape, single=True), row],
        [row], [jax.ShapeDtypeStruct((t, D_MODEL), F32)], (*pieces, w, resid), sem=("parallel",), carry=carry)
    return outs[0] if carry is None else (outs[0], recv)


def _chunk_mask(rows):
    r = lax.broadcasted_iota(jnp.int32, (rows, rows), 0)
    c = lax.broadcasted_iota(jnp.int32, (rows, rows), 1)
    return ((r // CHUNK) == (c // CHUNK)) & (c <= r)


class _Lanes:
    def __init__(self, xs):
        self.xs = list(xs)

    def _with(self, other, f):
        if isinstance(other, _Lanes):
            return _Lanes([f(a, b) for a, b in zip(self.xs, other.xs)])
        return _Lanes([f(a, other) for a in self.xs])

    def __add__(self, o):
        return self._with(o, lambda a, b: a + b)

    def __radd__(self, o):
        return self._with(o, lambda a, b: b + a)

    def __sub__(self, o):
        return self._with(o, lambda a, b: a - b)

    def __rsub__(self, o):
        return self._with(o, lambda a, b: b - a)

    def __mul__(self, o):
        return self._with(o, lambda a, b: a * b)

    def __rmul__(self, o):
        return self._with(o, lambda a, b: b * a)

    def __truediv__(self, o):
        return self._with(o, lambda a, b: a / b)

    def __rtruediv__(self, o):
        return self._with(o, lambda a, b: b / a)

    def __neg__(self):
        return _Lanes([-a for a in self.xs])

    def __ge__(self, o):
        return self._with(o, lambda a, b: a >= b)

    def __getitem__(self, idx):
        return _Lanes([a[idx] for a in self.xs])

    def astype(self, dtype):
        return _Lanes([a.astype(dtype) for a in self.xs])

    def total(self):
        return functools.reduce(lambda a, b: a + b, self.xs)


def _lift(f):
    def g(*args, **kw):
        lanes = [a for a in args if isinstance(a, _Lanes)]
        if not lanes:
            return f(*args, **kw)
        return _Lanes([f(*[a.xs[i] if isinstance(a, _Lanes) else a for a in args], **kw)
                       for i in range(len(lanes[0].xs))])
    return g


def _concat(parts, axis):
    if isinstance(parts[0], _Lanes):
        return _Lanes([jnp.concatenate([p.xs[i] for p in parts], axis=axis) for i in range(len(parts[0].xs))])
    return jnp.concatenate(parts, axis=axis)


_exp, _log, _abs, _sqrt, _where = _lift(jnp.exp), _lift(jnp.log), _lift(jnp.abs), _lift(jnp.sqrt), _lift(jnp.where)
_sum, _mean, _rsqrt, _bcast = _lift(jnp.sum), _lift(jnp.mean), _lift(lax.rsqrt), _lift(jnp.broadcast_to)
_ldot, _ldot_nt, _ldot_tn = _lift(_dot), _lift(_dot_nt), _lift(_dot_tn)
_lsigmoid = _lift(_sigmoid)


def _mask_sum(mask_b, x, transpose=False):
    f = _ldot_tn if transpose else _ldot
    hi = x.astype(BF16)
    lo = (x - hi.astype(F32)).astype(BF16)
    return f(mask_b, hi) + f(mask_b, lo)


def _chunk_row(x, pos, rows):
    nc = rows // CHUNK

    def one(a):
        a3 = a.reshape(nc, CHUNK, HEAD_DIM)
        return jnp.broadcast_to(a3[:, pos:pos + 1, :], (nc, CHUNK, HEAD_DIM)).reshape(rows, HEAD_DIM)

    return _lift(one)(x)


def _chunk_total(x, rows):
    nc = rows // CHUNK

    def one(a):
        tot = jnp.sum(a.reshape(nc, CHUNK, HEAD_DIM), axis=1, keepdims=True)
        return jnp.broadcast_to(tot, (nc, CHUNK, HEAD_DIM)).reshape(rows, HEAD_DIM)

    return _lift(one)(x)


def _sigmoid_pair(x):
    e = _exp(-_abs(x))
    big = 1.0 / (1.0 + e)
    small = e * big
    pos = x >= 0.0
    return _where(pos, big, small), _where(pos, small, big)


def _hg_gates(q_raw, fl, lb, rows, mask):
    tri = mask.astype(BF16)
    sg, sg_neg = _sigmoid_pair(fl)
    forget = lb + (1.0 - lb) * sg
    k = (1.0 - lb) * sg_neg
    sq = _lsigmoid(q_raw)
    qs = q_raw * sq
    bc = _mask_sum(tri, _log(forget))
    bref = _chunk_row(bc, CHUNK // 2 - 1, rows)
    blast = _chunk_row(bc, CHUNK - 1, rows)
    return dict(tri=tri, sg=sg, sg_neg=sg_neg, forget=forget, k=k, sq=sq, qs=qs,
                e_a=_exp(bc - bref), e_b=_exp(bref - bc), e_q=_exp(bc), e_k=_exp(blast - bc),
                dec=_exp(blast))


HG_GROUP = 4


def _hg_lanes(bsz):
    return [(hh, slice(hh * HEAD_DIM, (hh + 1) * HEAD_DIM), b) for hh in range(HG_GROUP) for b in range(bsz)]


def _hg_read(ref, lanes):
    return _Lanes(ref[b, :, cs].astype(F32) for _, cs, b in lanes)


def _hg_write(ref, lanes, val, offset=0):
    for (_, cs, b), a in zip(lanes, val.xs):
        ref[b, :, offset + cs.start:offset + cs.stop] = a


def _hgrn_fwd(proj, hf, lb, nw, *, bsz, seq, carry=None):
    rows = min(ROWS_HG, seq)
    nt = seq // rows
    nc = rows // CHUNK
    t = bsz * seq

    lanes = _hg_lanes(bsz)

    def body(q_ref, f_ref, v_ref, g_ref, lb_ref, nw_ref, y_ref, o_ref, st_ref, s_scr):
        @pl.when(pl.program_id(1) == 0)
        def _():
            s_scr[...] = jnp.zeros_like(s_scr)

        mask = _chunk_mask(rows)
        lb_v = _Lanes(lb_ref[:, cs] for _, cs, _ in lanes)
        gt = _hg_gates(_hg_read(q_ref, lanes), _hg_read(f_ref, lanes), lb_v, rows, mask)
        v_b = _hg_read(v_ref, lanes).astype(BF16)
        a_b = (gt["qs"] * gt["e_a"]).astype(BF16)
        b_b = (gt["k"] * gt["e_b"]).astype(BF16)
        qi_b = (gt["qs"] * gt["e_q"]).astype(BF16)
        ko_b = (gt["k"] * gt["e_k"]).astype(BF16)
        scores = _where(mask, _ldot_nt(a_b, b_b), 0.0)
        o_intra = _ldot(scores.astype(BF16), v_b)

        s = _Lanes(s_scr[i] for i in range(len(lanes)))
        parts = []
        for n in range(nc):
            sl = slice(n * CHUNK, (n + 1) * CHUNK)
            s_b = s.astype(BF16)
            for (hh, _, b), a in zip(lanes, s_b.xs):
                st_ref[hh, b, n] = a
            parts.append(_ldot_nt(qi_b[sl], s_b))
            s = s * gt["dec"][n * CHUNK:n * CHUNK + 1] + _ldot_tn(v_b[sl], ko_b[sl])
        for i, a in enumerate(s.xs):
            s_scr[i] = a
        o = o_intra + _concat(parts, 0)
        _hg_write(o_ref, lanes, o)
        r = _rsqrt(_mean(o * o, axis=-1, keepdims=True) + RMS_EPS)
        g = _hg_read(g_ref, lanes)
        _hg_write(y_ref, lanes, (o * r * nw_ref[...] * (g * _lsigmoid(g))).astype(BF16))

    wide = HG_GROUP * HEAD_DIM

    def col(base):
        return pl.BlockSpec((bsz, rows, wide), lambda h, j: (0, j, base // wide + h))

    out_tile = pl.BlockSpec((bsz, rows, wide), lambda h, j: (0, j, h))
    p3 = proj.reshape(bsz, seq, IN_COLS)
    outs, recv = _call(
        body, "hgrn_fwd", (HEADS // HG_GROUP, nt),
        [col(C_HQ), out_tile, col(C_HI), col(C_HG),
         pl.BlockSpec((1, wide), lambda h, j: (0, h)), _resident((1, HEAD_DIM))],
        [out_tile, out_tile,
         pl.BlockSpec((HG_GROUP, bsz, nc, HEAD_DIM, HEAD_DIM), lambda h, j: (h, 0, j, 0, 0))],
        [jax.ShapeDtypeStruct((bsz, seq, WIDTH), BF16), jax.ShapeDtypeStruct((bsz, seq, WIDTH), F32),
         jax.ShapeDtypeStruct((HEADS, bsz, seq // CHUNK, HEAD_DIM, HEAD_DIM), BF16)],
        (p3, hf.reshape(bsz, seq, WIDTH), p3, p3, lb, nw),
        scratch=[pltpu.VMEM((len(lanes), HEAD_DIM, HEAD_DIM), F32)],
        sem=("parallel", "arbitrary"), carry=carry)
    outs = [outs[0].reshape(t, WIDTH), outs[1].reshape(t, WIDTH), outs[2]]
    return outs if carry is None else (outs, recv)


def _hgrn_bwd(proj, hf, lb, nw, o_pre, states, dy, *, bsz, seq, carry=None):
    rows = min(ROWS_HG, seq)
    nt = seq // rows
    nc = rows // CHUNK
    t = bsz * seq
    lanes = _hg_lanes(bsz)

    def body(q_ref, f_ref, v_ref, g_ref, lb_ref, nw_ref, o_ref, st_ref, dy_ref, dh_ref, dlb_ref, dnw_ref, ds_scr):
        h, j = pl.program_id(0), pl.program_id(1)

        @pl.when(j == 0)
        def _():
            ds_scr[...] = jnp.zeros_like(ds_scr)
            dlb_ref[...] = jnp.zeros_like(dlb_ref)

        @pl.when((h == 0) & (j == 0))
        def _():
            dnw_ref[...] = jnp.zeros_like(dnw_ref)

        mask = _chunk_mask(rows)
        q_raw = _hg_read(q_ref, lanes)
        lb_v = _Lanes(lb_ref[:, cs] for _, cs, _ in lanes)
        gt = _hg_gates(q_raw, _hg_read(f_ref, lanes), lb_v, rows, mask)
        v_b = _hg_read(v_ref, lanes).astype(BF16)
        a_f = gt["qs"] * gt["e_a"]
        b_f = gt["k"] * gt["e_b"]
        qi_f = gt["qs"] * gt["e_q"]
        ko_f = gt["k"] * gt["e_k"]
        a_b, b_b, qi_b, ko_b = a_f.astype(BF16), b_f.astype(BF16), qi_f.astype(BF16), ko_f.astype(BF16)

        o = _hg_read(o_ref, lanes)
        nw_v = nw_ref[...]
        g = _hg_read(g_ref, lanes)
        dyv = _hg_read(dy_ref, lanes)
        r = _rsqrt(_mean(o * o, axis=-1, keepdims=True) + RMS_EPS)
        sgg = _lsigmoid(g)
        d_g = dyv * (o * r * nw_v) * (sgg * (1.0 + g * (1.0 - sgg)))
        d_on = dyv * (g * sgg)
        dnw_ref[...] += _sum(d_on * o * r, axis=0, keepdims=True).total()
        tt = d_on * nw_v
        d_o = r * tt - o * (r * r * r) * _mean(tt * o, axis=-1, keepdims=True)
        do_b = d_o.astype(BF16)

        sc_b = _where(mask, _ldot_nt(a_b, b_b), 0.0).astype(BF16)
        dsc_b = _where(mask, _ldot_nt(do_b, v_b), 0.0).astype(BF16)
        d_v = _ldot_tn(sc_b, do_b)
        d_a = _ldot(dsc_b, b_b)
        d_bm = _ldot_tn(dsc_b, a_b)

        ds = _Lanes(ds_scr[i] for i in range(len(lanes)))
        dqi_parts, dko_parts, dvi_parts, ddec_parts = [None] * nc, [None] * nc, [None] * nc, [None] * nc
        for n in reversed(range(nc)):
            sl = slice(n * CHUNK, (n + 1) * CHUNK)
            dec_n = gt["dec"][n * CHUNK:n * CHUNK + 1]
            ds_b = ds.astype(BF16)
            s_n = _Lanes(st_ref[hh, b, n] for hh, _, b in lanes)
            dqi_parts[n] = _ldot(do_b[sl], s_n)
            dko_parts[n] = _ldot(v_b[sl], ds_b)
            dvi_parts[n] = _ldot_nt(ko_b[sl], ds_b)
            d_dec = _sum(ds * s_n.astype(F32), axis=0, keepdims=True)
            ddec_parts[n] = _bcast(d_dec * dec_n, (CHUNK, HEAD_DIM))
            ds = ds * dec_n + _ldot_tn(do_b[sl], qi_b[sl])
        for i, a in enumerate(ds.xs):
            ds_scr[i] = a
        d_qi = _concat(dqi_parts, 0)
        d_ko = _concat(dko_parts, 0)
        d_v = d_v + _concat(dvi_parts, 0)

        d_qs = d_a * gt["e_a"] + d_qi * gt["e_q"]
        d_k = d_bm * gt["e_b"] + d_ko * gt["e_k"]
        t_a, t_b, t_q, t_k = d_a * a_f, d_bm * b_f, d_qi * qi_f, d_ko * ko_f
        d_bref = _chunk_total(t_b - t_a, rows)
        d_blast = _chunk_total(t_k, rows) + _concat(ddec_parts, 0)
        pos = lax.broadcasted_iota(jnp.int32, (rows, HEAD_DIM), 0) % CHUNK
        d_bc = (t_a - t_b + t_q - t_k + _where(pos == CHUNK // 2 - 1, d_bref, 0.0)
                + _where(pos == CHUNK - 1, d_blast, 0.0))
        d_logf = _mask_sum(gt["tri"], d_bc, transpose=True)

        sg, sg_neg = gt["sg"], gt["sg_neg"]
        inv_f = 1.0 / gt["forget"]
        common = (1.0 - lb_v) * sg * sg_neg
        d_fl = common * (d_logf * inv_f - d_k)
        d_lb = _sum(sg_neg * (d_logf * inv_f - d_k), axis=0, keepdims=True)
        for (_, cs, _), a in zip(lanes, d_lb.xs):
            dlb_ref[:, cs] += a
        sq = gt["sq"]
        _hg_write(dh_ref, lanes, (d_qs * (sq * (1.0 + q_raw * (1.0 - sq)))).astype(BF16), 0)
        _hg_write(dh_ref, lanes, d_fl.astype(BF16), WIDTH)
        _hg_write(dh_ref, lanes, d_v.astype(BF16), 2 * WIDTH)
        _hg_write(dh_ref, lanes, d_g.astype(BF16), 3 * WIDTH)

    assert HG_GROUP == HEADS, "the combined gradient block needs all heads in one grid step"
    wide = HG_GROUP * HEAD_DIM

    def col(base):
        return pl.BlockSpec((bsz, rows, wide), lambda h, j: (0, nt - 1 - j, base // wide + h))

    tile = pl.BlockSpec((bsz, rows, wide), lambda h, j: (0, nt - 1 - j, h))
    head_vec = pl.BlockSpec((1, wide), lambda h, j: (0, h))
    p3 = proj.reshape(bsz, seq, IN_COLS)
    outs, recv = _call(
        body, "hgrn_bwd", (HEADS // HG_GROUP, nt),
        [col(C_HQ), tile, col(C_HI), col(C_HG), head_vec, _resident((1, HEAD_DIM)), tile,
         pl.BlockSpec((HG_GROUP, bsz, nc, HEAD_DIM, HEAD_DIM), lambda h, j: (h, 0, nt - 1 - j, 0, 0)), tile],
        [pl.BlockSpec((bsz, rows, 4 * WIDTH), lambda h, j: (0, nt - 1 - j, 0)), head_vec, _resident((1, HEAD_DIM))],
        [jax.ShapeDtypeStruct((bsz, seq, 4 * WIDTH), BF16), jax.ShapeDtypeStruct((1, WIDTH), F32),
         jax.ShapeDtypeStruct((1, HEAD_DIM), F32)],
        (p3, hf.reshape(bsz, seq, WIDTH), p3, p3, lb, nw, o_pre.reshape(bsz, seq, WIDTH), states,
         dy.reshape(bsz, seq, WIDTH)),
        scratch=[pltpu.VMEM((len(lanes), HEAD_DIM, HEAD_DIM), F32)],
        sem=("arbitrary", "arbitrary"), carry=carry)
    outs = [outs[0].reshape(t, 4 * WIDTH), outs[1], outs[2]]
    return outs if carry is None else (outs, recv)


def _mem_kv(mem2d, w_k, w_v):
    rows = mem2d.shape[0]

    def body(m_ref, wk_ref, wv_ref, k_ref, v_ref):
        m_b = m_ref[...].astype(BF16)
        k_ref[...] = _dot(m_b, wk_ref[...]).astype(BF16)
        v_ref[...] = _dot(m_b, wv_ref[...]).astype(BF16)

    return pl.pallas_call(
        body, name="mem_kv", grid=(rows // MEM_LEN,),
        in_specs=[pl.BlockSpec((MEM_LEN, D_MODEL), lambda i: (i, 0)), _resident((D_MODEL, WIDTH)),
                  _resident((D_MODEL, WIDTH))],
        out_specs=[pl.BlockSpec((MEM_LEN, WIDTH), lambda i: (i, 0))] * 2,
        out_shape=[jax.ShapeDtypeStruct((rows, WIDTH), BF16)] * 2,
        compiler_params=_params("parallel"),
    )(mem2d, w_k, w_v)


def _softmax_rows(s):
    m = _lift(jnp.max)(s, axis=-1, keepdims=True)
    e = _exp(s - m)
    return e / _sum(e, axis=-1, keepdims=True)


def _attn_fwd(proj, mk, mv, *, tm, seq):
    t = proj.shape[0]
    per_b = seq // tm
    scale = HEAD_DIM ** -0.5

    def body(q_ref, k_ref, v_ref, y_ref):
        heads = [slice(h * HEAD_DIM, (h + 1) * HEAD_DIM) for h in range(HEADS)]
        q_b = _Lanes(q_ref[:, sl] for sl in heads).astype(BF16)
        p = _softmax_rows(_ldot_nt(q_b, _Lanes(k_ref[:, sl] for sl in heads)) * scale)
        out = _ldot(p.astype(BF16), _Lanes(v_ref[:, sl] for sl in heads))
        y_ref[...] = jnp.concatenate(out.xs, axis=-1).astype(BF16)

    kv = pl.BlockSpec((MEM_LEN, WIDTH), lambda i: (i // per_b, 0))
    return pl.pallas_call(
        body, name="attn_fwd", grid=(t // tm,),
        in_specs=[pl.BlockSpec((tm, WIDTH), lambda i: (i, C_MQ // WIDTH)), kv, kv],
        out_specs=pl.BlockSpec((tm, WIDTH), lambda i: (i, 0)),
        out_shape=jax.ShapeDtypeStruct((t, WIDTH), BF16),
        compiler_params=_params("parallel"),
    )(proj, mk, mv)


def _attn_bwd(proj, mk, mv, dy, *, tm, seq):
    t = proj.shape[0]
    per_b = seq // tm
    scale = HEAD_DIM ** -0.5

    def body(q_ref, k_ref, v_ref, dy_ref, dq_ref, dk_ref, dv_ref):
        i = pl.program_id(0)

        @pl.when(i % per_b == 0)
        def _():
            dk_ref[...] = jnp.zeros_like(dk_ref)
            dv_ref[...] = jnp.zeros_like(dv_ref)

        heads = [slice(h * HEAD_DIM, (h + 1) * HEAD_DIM) for h in range(HEADS)]
        q_b = _Lanes(q_ref[:, sl] for sl in heads).astype(BF16)
        k_b, v_b = _Lanes(k_ref[:, sl] for sl in heads), _Lanes(v_ref[:, sl] for sl in heads)
        p = _softmax_rows(_ldot_nt(q_b, k_b) * scale)
        dy_b = _Lanes(dy_ref[:, sl] for sl in heads).astype(BF16)
        dp = _ldot_nt(dy_b, v_b)
        d_v = _ldot_tn(p.astype(BF16), dy_b)
        ds_b = (p * (dp - _sum(dp * p, axis=-1, keepdims=True)) * scale).astype(BF16)
        dq_ref[...] = jnp.concatenate(_ldot(ds_b, k_b).xs, axis=-1).astype(BF16)
        dk_ref[...] += jnp.concatenate(_ldot_tn(ds_b, q_b).xs, axis=-1)
        dv_ref[...] += jnp.concatenate(d_v.xs, axis=-1)

    kv = pl.BlockSpec((MEM_LEN, WIDTH), lambda i: (i // per_b, 0))
    tile = pl.BlockSpec((tm, WIDTH), lambda i: (i, 0))
    n_mem = mk.shape[0]
    return pl.pallas_call(
        body, name="attn_bwd", grid=(t // tm,),
        in_specs=[pl.BlockSpec((tm, WIDTH), lambda i: (i, C_MQ // WIDTH)), kv, kv, tile],
        out_specs=[tile, kv, kv],
        out_shape=[jax.ShapeDtypeStruct((t, WIDTH), BF16), jax.ShapeDtypeStruct((n_mem, WIDTH), F32),
                   jax.ShapeDtypeStruct((n_mem, WIDTH), F32)],
        compiler_params=_params("arbitrary"),
    )(proj, mk, mv, dy)


HALO = 16


def _shift_down(u, halo, k, row):
    out = pltpu.roll(u, k, 0)
    for m in range(k):
        out = jnp.where(row == m, halo[HALO - k + m:HALO - k + m + 1, :], out)
    return out


def _shift_up(u, halo, k, row, tm):
    out = pltpu.roll(u, tm - k, 0)
    for m in range(k):
        out = jnp.where(row == tm - k + m, halo[m:m + 1, :], out)
    return out


def _merge_fwd(proj, y_b, y_c, conv_w, w_branch, b_gate, *, tm, seq, carry=None):
    t = proj.shape[0]
    per_b = seq // tm
    hb = tm // HALO

    def body(cb_ref, cc_ref, ch_ref, cch_ref, chh_ref, ga_ref, gb_ref, gc_ref, yb_ref, yc_ref, cw_ref, wb_ref,
             bg_ref, ya_ref, pa_ref, pb_ref, pc_ref, mg_ref, sa_ref, sb_ref, sc_ref):
        i = pl.program_id(0)
        row = lax.broadcasted_iota(jnp.int32, (tm, WIDTH), 0)
        u = cc_ref[...].astype(F32) * ch_ref[...].astype(F32)
        halo = jnp.where(i % per_b == 0, 0.0, cch_ref[...].astype(F32) * chh_ref[...].astype(F32))
        cw = cw_ref[...]
        y = cw[0:1] * _shift_down(u, halo, 2, row) + cw[1:2] * _shift_down(u, halo, 1, row) + cw[2:3] * u
        ya_b = (cb_ref[...].astype(F32) * y).astype(BF16)
        ya_ref[...] = ya_b
        merged = None
        for idx, (y_in, g_ref, p_ref, s_ref) in enumerate(((ya_b, ga_ref, pa_ref, sa_ref),
                                                            (yb_ref[...], gb_ref, pb_ref, sb_ref),
                                                            (yc_ref[...], gc_ref, pc_ref, sc_ref))):
            p = _dot(y_in, wb_ref[idx])
            p_ref[...] = p.astype(BF16)
            sg = _sigmoid(g_ref[...].astype(F32) + bg_ref[:, idx * D_MODEL:(idx + 1) * D_MODEL])
            s_ref[...] = sg.astype(BF16)
            term = sg * p
            merged = term if merged is None else merged + term
        mg_ref[...] = merged.astype(BF16)

    def half(c):
        return pl.BlockSpec((tm, WIDTH), lambda i: (i, c // WIDTH))

    def prev(c):
        return pl.BlockSpec((HALO, WIDTH), lambda i: (jnp.maximum(i * hb - 1, 0), c // WIDTH))

    def gate(k):
        return pl.BlockSpec((tm, D_MODEL), lambda i: (i, C_GA // D_MODEL + k))

    tile512 = pl.BlockSpec((tm, WIDTH), lambda i: (i, 0))
    tile1k = pl.BlockSpec((tm, D_MODEL), lambda i: (i, 0))
    outs, recv = _call(
        body, "merge_fwd", (t // tm,),
        [half(C_CB), half(C_CC), half(C_CH), prev(C_CC), prev(C_CH), gate(0), gate(1), gate(2),
         tile512, tile512, _resident((CONV_K, WIDTH)), _resident((3, WIDTH, D_MODEL)), _resident((1, 3 * D_MODEL))],
        [tile512] + [tile1k] * 7,
        [jax.ShapeDtypeStruct((t, WIDTH), BF16)] + [jax.ShapeDtypeStruct((t, D_MODEL), BF16)] * 7,
        (proj, proj, proj, proj, proj, proj, proj, proj, y_b, y_c, conv_w, w_branch, b_gate),
        sem=("parallel",), carry=carry)
    return outs if carry is None else (outs, recv)


def _merge_bwd(dmerged, projections, gates, branch_in, w_branch, *, tm):
    t = dmerged.shape[0]
    last = t // tm - 1

    def body(dm_ref, pa_ref, pb_ref, pc_ref, sa_ref, sb_ref, sc_ref, ya_ref, yb_ref, yc_ref, wb_ref,
             dgt_ref, dya_ref, dyb_ref, dyc_ref, dbg_ref, gw_ref, acc_ref):
        i = pl.program_id(0)

        @pl.when(i == 0)
        def _():
            dbg_ref[...] = jnp.zeros_like(dbg_ref)
            acc_ref[...] = jnp.zeros_like(acc_ref)

        dm = dm_ref[...].astype(F32)
        for idx, (p_ref, s_ref, y_ref, dy_ref) in enumerate(((pa_ref, sa_ref, ya_ref, dya_ref),
                                                             (pb_ref, sb_ref, yb_ref, dyb_ref),
                                                             (pc_ref, sc_ref, yc_ref, dyc_ref))):
            cols = slice(idx * D_MODEL, (idx + 1) * D_MODEL)
            sg = s_ref[...].astype(F32)
            dp = dm * sg
            dp_b = dp.astype(BF16)
            dgate = dp * p_ref[...].astype(F32) * (1.0 - sg)
            dgt_ref[:, cols] = dgate.astype(BF16)
            dbg_ref[:, cols] += jnp.sum(dgate, axis=0, keepdims=True)
            dy_ref[...] = _dot_nt(dp_b, wb_ref[idx]).astype(BF16)
            acc_ref[idx] += _dot_tn(y_ref[...], dp_b)

        @pl.when(i == last)
        def _():
            gw_ref[...] = acc_ref[...].astype(BF16)

    tile512 = pl.BlockSpec((tm, WIDTH), lambda i: (i, 0))
    tile1k = pl.BlockSpec((tm, D_MODEL), lambda i: (i, 0))
    return pl.pallas_call(
        body, name="merge_bwd", grid=(t // tm,),
        in_specs=[tile1k] * 7 + [tile512] * 3 + [_resident((3, WIDTH, D_MODEL))],
        out_specs=[pl.BlockSpec((tm, 3 * D_MODEL), lambda i: (i, 0)), tile512, tile512, tile512,
                   _resident((1, 3 * D_MODEL)), _resident((3, WIDTH, D_MODEL))],
        out_shape=[jax.ShapeDtypeStruct((t, 3 * D_MODEL), BF16)] + [jax.ShapeDtypeStruct((t, WIDTH), BF16)] * 3
                  + [jax.ShapeDtypeStruct((1, 3 * D_MODEL), F32), jax.ShapeDtypeStruct((3, WIDTH, D_MODEL), BF16)],
        scratch_shapes=[pltpu.VMEM((3, WIDTH, D_MODEL), F32)],
        compiler_params=_params("arbitrary"),
    )(dmerged, *projections, *gates, *branch_in, w_branch)


def _conv_bwd(proj, dya, conv_w, *, tm, seq):
    t = proj.shape[0]
    per_b = seq // tm
    hb = tm // HALO
    last_blk = t // HALO - 1

    def body(cb_ref, cc_ref, ch_ref, cch_ref, chh_ref, dya_ref, cbn_ref, dyan_ref, cw_ref, d_ref, dcw_ref):
        i = pl.program_id(0)

        @pl.when(i == 0)
        def _():
            dcw_ref[...] = jnp.zeros_like(dcw_ref)

        row = lax.broadcasted_iota(jnp.int32, (tm, WIDTH), 0)
        cb, cc, ch = cb_ref[...].astype(F32), cc_ref[...].astype(F32), ch_ref[...].astype(F32)
        u = cc * ch
        halo = jnp.where(i % per_b == 0, 0.0, cch_ref[...].astype(F32) * chh_ref[...].astype(F32))
        u1 = _shift_down(u, halo, 1, row)
        u2 = _shift_down(u, halo, 2, row)
        cw = cw_ref[...]
        y = cw[0:1] * u2 + cw[1:2] * u1 + cw[2:3] * u
        dya = dya_ref[...].astype(F32)
        dy = dya * cb
        nxt = jnp.where(i % per_b == per_b - 1, 0.0, dyan_ref[...].astype(F32) * cbn_ref[...].astype(F32))
        du = cw[2:3] * dy + cw[1:2] * _shift_up(dy, nxt, 1, row, tm) + cw[0:1] * _shift_up(dy, nxt, 2, row, tm)
        d_ref[:, 0:WIDTH] = (dya * y).astype(BF16)
        d_ref[:, WIDTH:2 * WIDTH] = (du * ch).astype(BF16)
        d_ref[:, 2 * WIDTH:3 * WIDTH] = (du * cc).astype(BF16)
        dcw_ref[0:1, :] += jnp.sum(dy * u2, axis=0, keepdims=True)
        dcw_ref[1:2, :] += jnp.sum(dy * u1, axis=0, keepdims=True)
        dcw_ref[2:3, :] += jnp.sum(dy * u, axis=0, keepdims=True)

    def half(c):
        return pl.BlockSpec((tm, WIDTH), lambda i: (i, c // WIDTH))

    def prev(c):
        return pl.BlockSpec((HALO, WIDTH), lambda i: (jnp.maximum(i * hb - 1, 0), c // WIDTH))

    def nxt(c):
        return pl.BlockSpec((HALO, WIDTH), lambda i: (jnp.minimum((i + 1) * hb, last_blk), c // WIDTH))

    return pl.pallas_call(
        body, name="conv_bwd", grid=(t // tm,),
        in_specs=[half(C_CB), half(C_CC), half(C_CH), prev(C_CC), prev(C_CH),
                  pl.BlockSpec((tm, WIDTH), lambda i: (i, 0)), nxt(C_CB), nxt(0), _resident((CONV_K, WIDTH))],
        out_specs=[pl.BlockSpec((tm, 3 * WIDTH), lambda i: (i, 0)), _resident((CONV_K, WIDTH))],
        out_shape=[jax.ShapeDtypeStruct((t, 3 * WIDTH), BF16), jax.ShapeDtypeStruct((CONV_K, WIDTH), F32)],
        compiler_params=_params("arbitrary"),
    )(proj, proj, proj, proj, proj, dya, proj, dya, conv_w)


def _loss_head(y, target, *, tm):
    t = y.shape[0]

    def body(y_ref, t_ref, dy_ref, l_ref):
        @pl.when(pl.program_id(0) == 0)
        def _():
            l_ref[...] = jnp.zeros_like(l_ref)

        err = y_ref[...] - t_ref[...]
        dy_ref[...] = err * (1.0 / D_MODEL)
        per_row = jnp.sum(err * err, axis=-1, keepdims=True) * (1.0 / D_MODEL)
        l_ref[...] += 0.5 * jnp.sum(per_row, axis=0, keepdims=True)

    row = pl.BlockSpec((tm, D_MODEL), lambda i: (i, 0))
    return pl.pallas_call(
        body, name="loss_head", grid=(t // tm,),
        in_specs=[row, row], out_specs=[row, _resident((8, 128))],
        out_shape=[jax.ShapeDtypeStruct((t, D_MODEL), F32), jax.ShapeDtypeStruct((8, 128), F32)],
        compiler_params=_params("arbitrary"),
    )(y, target)


def _lb_softmax(lower_bounds):
    x = lower_bounds
    e = jnp.exp(x - jnp.max(x, axis=0, keepdims=True))
    return e / jnp.sum(e, axis=0, keepdims=True)


def _lb_fwd(lower_bounds):
    def body(x_ref, o_ref):
        s = _lb_softmax(x_ref[...])
        c = s[0:1]
        o_ref[0:1, :] = c - s[0:1]
        for l in range(1, DEPTH):
            c = c + s[l:l + 1]
            o_ref[l:l + 1, :] = c - s[0:1]

    return pl.pallas_call(body, name="lb_fwd", out_shape=jax.ShapeDtypeStruct(lower_bounds.shape, F32))(lower_bounds)


def _lb_bwd(lower_bounds, d_lb_all):
    def body(x_ref, d_ref, o_ref):
        s = _lb_softmax(x_ref[...])
        d = d_ref[...]
        rows = [jnp.zeros_like(d[0:1])]
        for j in range(1, DEPTH):
            acc = d[j:j + 1]
            for l in range(j + 1, DEPTH):
                acc = acc + d[l:l + 1]
            rows.append(acc)
        inner = rows[0] * s[0:1]
        for j in range(1, DEPTH):
            inner = inner + rows[j] * s[j:j + 1]
        for j in range(DEPTH):
            o_ref[j:j + 1, :] = s[j:j + 1] * (rows[j] - inner)

    return pl.pallas_call(body, name="lb_bwd", out_shape=jax.ShapeDtypeStruct(lower_bounds.shape, F32))(
        lower_bounds, d_lb_all)


def _adamw(w, g, m, v):
    m2 = ADAM_B1 * m + (1.0 - ADAM_B1) * g
    v2 = ADAM_B2 * v + (1.0 - ADAM_B2) * (g * g)
    m_hat = m2 / (1.0 - ADAM_B1 ** ADAM_STEP)
    v_hat = v2 / (1.0 - ADAM_B2 ** ADAM_STEP)
    delta = -ADAM_LR * (m_hat / (jnp.sqrt(v_hat) + ADAM_EPS) + ADAM_WD * w)
    return delta, m2, v2


def _adam_small(name, g, w, m, v):
    shape = w.shape
    flat = (-1, shape[-1])
    g2, w2, m2, v2 = (a.reshape(flat) for a in (g, w, m, v))

    def body(g_ref, w_ref, m_ref, v_ref, d_ref, mo_ref, vo_ref):
        d, mm, vv = _adamw(w_ref[...], g_ref[...], m_ref[...], v_ref[...])
        d_ref[...] = d
        mo_ref[...] = mm
        vo_ref[...] = vv

    outs = pl.pallas_call(body, name=name, out_shape=[jax.ShapeDtypeStruct(w2.shape, F32)] * 3)(g2, w2, m2, v2)
    return [o.reshape(shape) for o in outs]


def _adam_shard(name, recvs, w, m, v, *, tr):
    _, r, c = w.shape

    def body(*refs):
        rc, (w_ref, m_ref, v_ref), (g_ref, d_ref, mo_ref, vo_ref) = refs[:DEPTH], refs[DEPTH:DEPTH + 3], refs[DEPTH + 3:]
        layer = pl.program_id(0)
        for cand in range(DEPTH):
            @pl.when(layer == cand)
            def _():
                g = rc[cand][0].astype(F32)
                for d in range(1, N_DEV):
                    g = g + rc[cand][d].astype(F32)
                dl, mm, vv = _adamw(w_ref[...], g, m_ref[...], v_ref[...])
                g_ref[...] = g
                d_ref[...] = dl
                mo_ref[...] = mm
                vo_ref[...] = vv

    def recv_spec(cand):
        return pl.BlockSpec((N_DEV, tr, c), lambda l, i: (0, jnp.where(l == cand, i, 0), 0))

    tile = pl.BlockSpec((None, tr, c), lambda l, i: (l, i, 0))
    return pl.pallas_call(
        body, name=name, grid=(DEPTH, r // tr),
        in_specs=[recv_spec(cand) for cand in range(DEPTH)] + [tile] * 3,
        out_specs=[tile] * 4,
        out_shape=[jax.ShapeDtypeStruct(w.shape, F32)] * 4,
        compiler_params=_params("parallel", "parallel"),
    )(*recvs, w, m, v)


def _sum_devices(name, x):
    def body(x_ref, o_ref):
        acc = x_ref[0]
        for d in range(1, N_DEV):
            acc = acc + x_ref[d]
        o_ref[...] = acc

    return pl.pallas_call(body, name=name, out_shape=jax.ShapeDtypeStruct(x.shape[1:], x.dtype))(x)


SMALL = (("lower_bounds", 1, 512), ("conv_w", CONV_K, WIDTH), ("hg_norm_w", 1, HEAD_DIM), ("b_gate", 3, D_MODEL),
         ("ln1_g", 1, D_MODEL), ("ln1_b", 1, D_MODEL), ("ln2_g", 1, D_MODEL), ("ln2_b", 1, D_MODEL))
SMALL_ROWS = sum(r for _, r, _ in SMALL)


def _pack_small(per_layer):
    flat = [a for layer in per_layer for a in layer]

    def body(*refs):
        ins, o_ref = refs[:-1], refs[-1]
        o_ref[...] = jnp.zeros_like(o_ref)
        it = iter(ins)
        for l in range(DEPTH):
            row = l * SMALL_ROWS
            for name, nrows, ncols in SMALL:
                ref = next(it)
                if name == "b_gate":
                    for k in range(nrows):
                        o_ref[row + k:row + k + 1, :] = ref[:, k * ncols:(k + 1) * ncols]
                else:
                    o_ref[row:row + nrows, 0:ncols] = ref[...]
                row += nrows

    return pl.pallas_call(body, name="pack_small_grads",
                          out_shape=jax.ShapeDtypeStruct((DEPTH * SMALL_ROWS, D_MODEL), F32))(*flat)


def _unpack_small(summed):
    s3 = summed.reshape(DEPTH, SMALL_ROWS, D_MODEL)
    out, row = {}, 0
    for name, nrows, ncols in SMALL:
        out[name] = s3[:, row:row + nrows, :ncols].reshape(DEPTH, nrows * ncols)
        row += nrows
    return out


def _natural_cols(g):
    nd = g.ndim
    perm = tuple(range(1, nd - 1)) + (0, nd - 1)
    t = jnp.transpose(g, perm)
    return t.reshape(t.shape[:-2] + (t.shape[-2] * t.shape[-1],))


def _natural_rows(g):
    return g.reshape(g.shape[0] * g.shape[1], g.shape[2])


def _hosted(hosts, key, fn):
    pairs = hosts.get(key) if hosts else None
    if callable(pairs):
        pairs = pairs()
    if not pairs:
        return fn(None)
    outs, recvs = fn([ex for ex, _ in pairs])
    for (_, hook), recv in zip(pairs, recvs):
        hook(recv)
    return outs


def _layer_fwd(cur, cur_b, mem2d, wl, *, bsz, seq, hosts=None):
    tm = min(512, seq)
    proj, hf = _hosted(hosts, "in_proj", lambda c: _in_proj(cur_b, wl["w_in"], tm=min(1024, seq), carry=c))
    y_b, o_pre, states = _hosted(hosts, "hgrn_fwd", lambda c: _hgrn_fwd(proj, hf, wl["lb"], wl["nw"], bsz=bsz,
                                                                         seq=seq, carry=c))
    mk, mv = _mem_kv(mem2d, wl["w_mk"], wl["w_mv"])
    y_c = _attn_fwd(proj, mk, mv, tm=tm, seq=seq)
    y_a, pa, pb, pc, merged, sga, sgb, sgc = _hosted(
        hosts, "merge_fwd", lambda c: _merge_fwd(proj, y_b, y_c, wl["conv"], wl["w_br"], wl["b_gate"], tm=tm,
                                                 seq=seq, carry=c))
    z1, x1, x1_b = _hosted(hosts, "wo_ln", lambda c: _linear_ln("wo_ln", merged, wl["w_o"], cur, wl["ln1_g"],
                                                                  wl["ln1_b"], tm=tm, carry=c))
    hid = _hosted(hosts, "mlp_up", lambda c: _mm_nn("mlp_up", x1_b, wl["w_up"], tm=min(1024, seq), tn=2048,
                                                     out_dtype=BF16, relu2=True, carry=c))
    z2, x2, x2_b = _linear_ln("down_ln", hid, wl["w_down"], x1, wl["ln2_g"], wl["ln2_b"], tm=tm)
    return dict(x_b=cur_b, proj=proj, hf=hf, y_a=y_a, y_b=y_b, y_c=y_c, o_pre=o_pre, states=states, mk=mk, mv=mv,
                proj3=(pa, pb, pc), gates3=(sga, sgb, sgc), merged=merged, z1=z1, x1_b=x1_b, hid=hid, z2=z2, x2=x2,
                x2_b=x2_b)


def _layer_bwd(dcur, mem2d, s, wl, *, bsz, seq, plan=None):
    tm = min(512, seq)
    tk = min(2048, bsz * seq)
    g = {}

    def run(key, fn):
        made = plan[key](g) if plan and key in plan else None
        return _hosted({key: [made]} if made else None, key, fn)

    dz2, dz2_b, dhpre, d_ln2g, d_ln2b = run(
        "ln2_bwd_down", lambda c: _ln_bwd_mm_nt("ln2_bwd_down", dcur, s["z2"], wl["ln2_g"], wl["w_down"],
                                                s["hid"], tm=tm, tn=1024, carry=c))
    g["w_down"] = _mm_tn("grad_w_down", s["hid"], dz2_b, tk=tk, tmo=1024, tno=1024)
    dx1 = _mm_nt_sum("mlp_up_bwd", [dhpre], [0], wl["w_up"], dz2, tm=tm)
    g["w_up"] = run("grad_w_up", lambda c: _mm_tn("grad_w_up", s["x1_b"], dhpre, tk=tk, tmo=1024, tno=2048, carry=c))
    dz1, dz1_b, dmerged, d_ln1g, d_ln1b = _ln_bwd_mm_nt("ln1_bwd_wo", dx1, s["z1"], wl["ln1_g"], wl["w_o"],
                                                        tm=tm, tn=1024)
    g["w_o"] = _mm_tn("grad_w_o", s["merged"], dz1_b, tk=tk, tmo=1024, tno=1024)
    dgate, dya, dyb, dyc, d_bg, g["w_br"] = _merge_bwd(dmerged, s["proj3"], s["gates3"],
                                                       (s["y_a"], s["y_b"], s["y_c"]), wl["w_br"], tm=tm)
    d_conv, d_cw = _conv_bwd(s["proj"], dya, wl["conv"], tm=tm, seq=seq)
    dhg, d_lb, d_nw = run(
        "hgrn_bwd", lambda c: _hgrn_bwd(s["proj"], s["hf"], wl["lb"], wl["nw"], s["o_pre"], s["states"], dyb,
                                        bsz=bsz, seq=seq, carry=c))
    dmq, dmk, dmv = _attn_bwd(s["proj"], s["mk"], s["mv"], dyc, tm=tm, seq=seq)
    tkm = min(512, mem2d.shape[0])
    g["w_mk"] = _mm_tn("grad_w_mem", mem2d, dmk, tk=tkm, tmo=1024, tno=512)
    g["w_mv"] = _mm_tn("grad_w_mem", mem2d, dmv, tk=tkm, tmo=1024, tno=512)
    pieces = [d_conv, dhg, dmq, dgate]
    offsets = [C_CB, C_HQ, C_MQ, C_GA]
    g["w_in"] = jnp.concatenate(
        [_mm_tn("grad_w_in_%d" % p.shape[1], s["x_b"], p, tk=tk, tmo=1024,
                tno=next(w for w in (2048, 1536, 512) if p.shape[1] % w == 0)) for p in pieces], axis=1)
    dx = run("in_proj_bwd", lambda c: _mm_nt_sum("in_proj_bwd", pieces, offsets, wl["w_in"], dz1,
                                                 tm=min(512, seq), carry=c))
    return dx, g, [d_lb, d_cw, d_nw, d_bg, d_ln1g, d_ln1b, d_ln2g, d_ln2b]


def kernel(x, mem, lower_bounds, w_in, conv_w, hg_norm_w, w_mem_k, w_mem_v, w_branch, b_gate, w_o, ln1_g, ln1_b, w_up, w_down, ln2_g, ln2_b, loss_target, m_lower_bounds, m_w_in, m_conv_w, m_hg_norm_w, m_w_mem_k, m_w_mem_v, m_w_branch, m_b_gate, m_w_o, m_ln1_g, m_ln1_b, m_w_up, m_w_down, m_ln2_g, m_ln2_b, v_lower_bounds, v_w_in, v_conv_w, v_hg_norm_w, v_w_mem_k, v_w_mem_v, v_w_branch, v_b_gate, v_w_o, v_ln1_g, v_ln1_b, v_w_up, v_w_down, v_ln2_g, v_ln2_b):
    bsz, seq, _ = x.shape
    t = bsz * seq
    me = _my_id()

    sh = dict(w_in=w_in.astype(BF16), w_mk=w_mem_k.astype(BF16), w_mv=w_mem_v.astype(BF16),
              w_br=w_branch.astype(BF16), w_o=w_o.astype(BF16), w_up=w_up.astype(BF16), w_down=w_down.astype(BF16))
    half_rows = D_MODEL // 2
    sh["w_in_a"], sh["w_in_b"] = sh["w_in"][:, :half_rows], sh["w_in"][:, half_rows:]
    natural = dict(w_in=_natural_cols, w_in_a=_natural_cols, w_in_b=_natural_cols, w_mk=_natural_rows,
                   w_mv=_natural_rows, w_br=_natural_cols, w_o=_natural_rows, w_up=_natural_cols,
                   w_down=_natural_rows)

    lb_all = _lb_fwd(lower_bounds)
    layer_w = [dict(lb=lb_all[l][None], nw=hg_norm_w[l][None], b_gate=b_gate[l][None], ln1_g=ln1_g[l][None],
                    ln1_b=ln1_b[l][None], ln2_g=ln2_g[l][None], ln2_b=ln2_b[l][None]) for l in range(DEPTH)]
    half_full = {}

    def near(names, l):
        srcs = [sh[n][l] for n in names]
        ex = _Exchange(srcs, piece_shapes=[s_.shape for s_ in srcs], route="near")
        return ex, lambda recv_: half_full.update({(n, l): r for n, r in zip(names, recv_)})

    def relay(names, l):
        ex = _Exchange([half_full.pop((n, l)) for n in names], route="relay")

        def hook(recv_):
            for n, r in zip(names, recv_):
                layer_w[l][n] = natural[n](r)
        return ex, hook

    small4 = ["w_mk", "w_mv", "w_br", "w_o"]
    conv_shard = conv_w.reshape(DEPTH * CONV_K * (WIDTH // N_DEV) // 128, 128)
    first = near(["w_in"], 0)
    conv_ex = _Exchange([conv_shard], piece_shapes=[conv_shard.shape])
    got = _exchange("gather_first", [first[0], conv_ex])
    first[1](got[0])
    conv_full = _natural_cols(got[1][0].reshape(N_DEV, DEPTH, CONV_K, WIDTH // N_DEV))
    second = relay(["w_in"], 0)
    second[1](_exchange("relay_first", [second[0]])[0])

    x2d = x.reshape(t, D_MODEL)
    mem2d = mem.reshape(bsz * MEM_LEN, D_MODEL)
    target2d = loss_target.reshape(t, D_MODEL)

    saved = []
    cur, cur_b = x2d, x2d.astype(BF16)
    for l in range(DEPTH):
        wl = layer_w[l]
        wl["conv"] = conv_full[l]
        more = l + 1 < DEPTH
        now = ["w_up", "w_down"] + ([] if l else small4)
        hosts = {"in_proj": [near(now, l)],
                 "hgrn_fwd": lambda l=l, more=more, now=now: [relay(now, l)] + ([near(["w_in"], l + 1)] if more else [])}
        if more:
            hosts["merge_fwd"] = lambda l=l: [relay(["w_in"], l + 1), near(small4, l + 1)]
            hosts["mlp_up"] = lambda l=l: [relay(small4, l + 1)]
        s = _layer_fwd(cur, cur_b, mem2d, wl, bsz=bsz, seq=seq, hosts=hosts)
        saved.append(s)
        cur, cur_b = s["x2"], s["x2_b"]

    dcur, loss_tile = _loss_head(cur, target2d, tm=min(512, seq))
    loss = lax.psum(loss_tile[0, 0], ("x", "y", "c"))

    in_w = IN_COLS // N_DEV

    def in_half(r):
        return lambda ref, j: ref.at[pl.ds(r * half_rows, half_rows), pl.ds(j * in_w, in_w)]

    slicer = dict(w_in_a=in_half(0), w_in_b=in_half(1), w_mk=_rows(D_MODEL // N_DEV), w_mv=_rows(D_MODEL // N_DEV),
                  w_br=_cols(D_MODEL // N_DEV), w_o=_rows(D_MODEL // N_DEV), w_up=_cols(D_FF // N_DEV),
                  w_down=_rows(D_FF // N_DEV))
    source = dict(w_in_a="w_in", w_in_b="w_in")
    recv = [dict() for _ in range(DEPTH)]

    def scatter_of(names, g, into):
        ex = _Exchange([g[source.get(n, n)] for n in names], [slicer[n] for n in names],
                       [sh[n].shape[1:] for n in names])
        return ex, lambda recv_: into.update(zip(names, recv_))

    small_rows = [None] * DEPTH
    prev = None
    rest = ["w_in_b", "w_mk", "w_mv"]
    for l in reversed(range(DEPTH)):
        plan = {"grad_w_up": lambda g, l=l: scatter_of(["w_down"], g, recv[l]),
                "hgrn_bwd": lambda g, l=l: scatter_of(["w_up", "w_o", "w_br"], g, recv[l])}
        if l == 0:
            plan["in_proj_bwd"] = lambda g: scatter_of(["w_in_a"] + rest, g, recv[0])
        else:
            plan["in_proj_bwd"] = lambda g, l=l: scatter_of(["w_in_a"], g, recv[l])
        if prev is not None:
            plan["ln2_bwd_down"] = lambda g, l=l, prev=prev: scatter_of(rest, prev, recv[l + 1])
        dcur, prev, small_rows[l] = _layer_bwd(dcur, mem2d, saved[l], layer_w[l], bsz=bsz, seq=seq, plan=plan)
    for r in recv:
        r["w_in"] = jnp.concatenate([r.pop("w_in_a"), r.pop("w_in_b")], axis=1)

    packed = _pack_small(small_rows)
    all_small = _exchange("gather_small_grads", [_Exchange([packed], piece_shapes=[packed.shape])])[0][0]
    small_grads = _unpack_small(_sum_devices("sum_small_grads", all_small))
    small_grads["lower_bounds"] = _lb_bwd(lower_bounds, small_grads["lower_bounds"])
    conv_all = small_grads["conv_w"].reshape(DEPTH, CONV_K, WIDTH)
    small_grads["conv_w"] = lax.dynamic_slice_in_dim(conv_all, me * (WIDTH // N_DEV), WIDTH // N_DEV, axis=2)

    grads, deltas, new_m, new_v = {}, {}, {}, {}
    given = dict(lower_bounds=(lower_bounds, m_lower_bounds, v_lower_bounds), conv_w=(conv_w, m_conv_w, v_conv_w),
                 hg_norm_w=(hg_norm_w, m_hg_norm_w, v_hg_norm_w), b_gate=(b_gate, m_b_gate, v_b_gate),
                 ln1_g=(ln1_g, m_ln1_g, v_ln1_g), ln1_b=(ln1_b, m_ln1_b, v_ln1_b),
                 ln2_g=(ln2_g, m_ln2_g, v_ln2_g), ln2_b=(ln2_b, m_ln2_b, v_ln2_b))
    for name, (w_, m_, v_) in given.items():
        g_ = small_grads[name].reshape(w_.shape)
        grads[name] = g_
        deltas[name], new_m[name], new_v[name] = _adam_small("adam_" + name, g_, w_, m_, v_)

    big = dict(w_in=("w_in", w_in, m_w_in, v_w_in, 128), w_mem_k=("w_mk", w_mem_k, m_w_mem_k, v_w_mem_k, 128),
               w_mem_v=("w_mv", w_mem_v, m_w_mem_v, v_w_mem_v, 128),
               w_branch=("w_br", w_branch, m_w_branch, v_w_branch, 512), w_o=("w_o", w_o, m_w_o, v_w_o, 128),
               w_up=("w_up", w_up, m_w_up, v_w_up, 256), w_down=("w_down", w_down, m_w_down, v_w_down, 128))
    for name, (k, w_, m_, v_, tr) in big.items():
        shape = w_.shape
        flat = (DEPTH, -1, shape[-1])
        rc = [recv[l][k].reshape((N_DEV,) + w_.reshape(flat).shape[1:]) for l in range(DEPTH)]
        outs = _adam_shard("adam_" + name, rc, w_.reshape(flat), m_.reshape(flat), v_.reshape(flat), tr=tr)
        grads[name], deltas[name], new_m[name], new_v[name] = (o.reshape(shape) for o in outs)

    order = ["lower_bounds", "w_in", "conv_w", "hg_norm_w", "w_mem_k", "w_mem_v", "w_branch", "b_gate", "w_o",
             "ln1_g", "ln1_b", "w_up", "w_down", "ln2_g", "ln2_b"]
    return (loss, dcur.reshape(x.shape), *[grads[n] for n in order], *[deltas[n] for n in order],
            *[new_m[n] for n in order], *[new_v[n] for n in order])
```

```python
import functools

import jax
import jax.numpy as jnp
from jax import lax
from jax.experimental import pallas as pl
from jax.experimental.pallas import tpu as pltpu

F32 = jnp.float32
BF16 = jnp.bfloat16

N_DEV = 8
D_MODEL = 1024
DEPTH = 4
MEM_LEN = 256
CONV_K = 3
WIDTH = 512
HEADS = 4
HEAD_DIM = 128
CHUNK = 32
D_FF = 4 * D_MODEL
IN_COLS = 7168
ALPHA = (2.0 * DEPTH) ** 0.25
LN_EPS = 1e-5
RMS_EPS = 1e-6
ADAM_LR = 0.001
ADAM_B1 = 0.9
ADAM_B2 = 0.999
ADAM_EPS = 1e-08
ADAM_WD = 0.01
ADAM_STEP = 10

C_CB, C_CC, C_CH, C_HQ, C_HF, C_HI, C_HG, C_MQ, C_GA = 0, 512, 1024, 1536, 2048, 2560, 3072, 3584, 4096

ROWS_HG = 256
NT_DIMS = (((1,), (1,)), ((), ()))
TN_DIMS = (((0,), (0,)), ((), ()))
MESH = pl.DeviceIdType.MESH


def _dot(a, b):
    return jnp.dot(a, b, preferred_element_type=F32)


def _dot_nt(a, b):
    return lax.dot_general(a, b, NT_DIMS, preferred_element_type=F32)


def _dot_tn(a, b):
    return lax.dot_general(a, b, TN_DIMS, preferred_element_type=F32)


def _sigmoid(x):
    return 1.0 / (1.0 + jnp.exp(-x))


def _params(*sem):
    return pltpu.CompilerParams(dimension_semantics=sem)


def _resident(shape, single=False):
    nd = len(shape)
    if single:
        return pl.BlockSpec(shape, lambda *_: (0,) * nd, pipeline_mode=pl.Buffered(1))
    return pl.BlockSpec(shape, lambda *_: (0,) * nd)


def _my_id():
    return 4 * lax.axis_index("x") + 2 * lax.axis_index("y") + lax.axis_index("c")


class _Exchange:
    def __init__(self, srcs, slicers=None, piece_shapes=None, route="all"):
        self.srcs, self.n, self.route = list(srcs), len(srcs), route
        self.slicers = list(slicers) if slicers else [_whole] * self.n
        any_spec = pl.BlockSpec(memory_space=pl.ANY)
        self.in_specs = [any_spec] * self.n
        self.out_specs = [any_spec] * self.n
        if route == "relay":
            self.out_shape = [jax.ShapeDtypeStruct(a.shape, a.dtype) for a in srcs]
        else:
            self.out_shape = [jax.ShapeDtypeStruct((N_DEV,) + tuple(s), a.dtype) for s, a in zip(piece_shapes, srcs)]
        self.aliased = route == "relay"
        self.scratch = [pltpu.SemaphoreType.DMA((self.n * N_DEV,)), pltpu.SemaphoreType.DMA((self.n * N_DEV,)),
                        pltpu.SemaphoreType.DMA((self.n,))]

    def _peer(self, j, me):
        if self.route == "all":
            return j != me
        return (j != me) & ((j % 2 == lax.axis_index("c")) | (j // 2 == me // 2))

    def _remote(self, ins, outs, sems, k, j, me):
        return pltpu.make_async_remote_copy(
            src_ref=self.slicers[k](ins[k], j), dst_ref=outs[k].at[me],
            send_sem=sems[0].at[k * N_DEV + j], recv_sem=sems[1].at[k * N_DEV + me],
            device_id=(j // 4, (j // 2) % 2, j % 2), device_id_type=MESH)

    def _local(self, ins, outs, sems, k, j, me):
        return pltpu.make_async_copy(self.slicers[k](ins[k], j), outs[k].at[me], sems[2].at[k])

    def _relay(self, outs, sems, k, j):
        sibling = (lax.axis_index("x"), lax.axis_index("y"), 1 - lax.axis_index("c"))
        return pltpu.make_async_remote_copy(
            src_ref=outs[k].at[j], dst_ref=outs[k].at[j], send_sem=sems[0].at[k * N_DEV + j],
            recv_sem=sems[1].at[k * N_DEV + j], device_id=sibling, device_id_type=MESH)

    def _other_chip(self, j, same_core):
        on_my_core = j % 2 == lax.axis_index("c")
        return (on_my_core if same_core else ~on_my_core) & (j // 2 != _my_id() // 2)

    def start(self, ins, outs, sems):
        me = _my_id()
        for k in range(self.n):
            for j in range(N_DEV):
                if self.route == "relay":
                    @pl.when(self._other_chip(j, True))
                    def _():
                        self._relay(outs, sems, k, j).start()
                    continue

                @pl.when(self._peer(j, me))
                def _():
                    self._remote(ins, outs, sems, k, j, me).start()

                @pl.when(j == me)
                def _():
                    self._local(ins, outs, sems, k, j, me).start()

    def wait(self, ins, outs, sems):
        me = _my_id()
        for k in range(self.n):
            for j in range(N_DEV):
                if self.route == "relay":
                    @pl.when(self._other_chip(j, False))
                    def _():
                        self._relay(outs, sems, k, j).wait_recv()

                    @pl.when(self._other_chip(j, True))
                    def _():
                        self._relay(outs, sems, k, j).wait_send()
                    continue

                @pl.when(self._peer(j, me))
                def _():
                    pltpu.make_async_remote_copy(
                        src_ref=self.slicers[k](ins[k], j), dst_ref=outs[k].at[j],
                        send_sem=sems[0].at[k * N_DEV + j], recv_sem=sems[1].at[k * N_DEV + j],
                        device_id=(j // 4, (j // 2) % 2, j % 2), device_id_type=MESH).wait_recv()
                    self._remote(ins, outs, sems, k, j, me).wait_send()

                @pl.when(j == me)
                def _():
                    self._local(ins, outs, sems, k, j, me).wait()


def _carried(exchanges, n_in, n_out):
    c_in = [s for ex in exchanges for s in ex.in_specs]
    c_out = [s for ex in exchanges for s in ex.out_specs]
    shapes = [s for ex in exchanges for s in ex.out_shape]
    sems = [s for ex in exchanges for s in ex.scratch]
    srcs = [a for ex in exchanges for a in ex.srcs]
    aliases, off = {}, 0
    for ex in exchanges:
        if ex.aliased:
            aliases.update({n_in + off + k: n_out + off + k for k in range(ex.n)})
        off += ex.n
    total = off

    def split(refs, n_scr):
        ins, outs = refs[:n_in], refs[n_in + total:n_in + total + n_out]
        rest = refs[n_in + 2 * total + n_out:]
        scr, sem_refs = rest[:n_scr], rest[n_scr:]
        parts, off_ = [], 0
        for i, ex in enumerate(exchanges):
            parts.append((refs[n_in + off_:n_in + off_ + ex.n],
                          refs[n_in + total + n_out + off_:n_in + total + n_out + off_ + ex.n],
                          sem_refs[3 * i:3 * i + 3]))
            off_ += ex.n
        return ins, outs, scr, parts

    return c_in, c_out, shapes, sems, srcs, aliases, split


def _exchange(name, exchanges):
    c_in, c_out, shapes, sems, srcs, aliases, split = _carried(exchanges, 0, 0)

    def body(*refs):
        _, _, _, parts = split(refs, 0)
        for ex, part in zip(exchanges, parts):
            ex.start(*part)
        for ex, part in zip(exchanges, parts):
            ex.wait(*part)

    outs = pl.pallas_call(
        body, name=name, in_specs=c_in, out_specs=c_out, out_shape=shapes, scratch_shapes=sems,
        input_output_aliases=aliases, compiler_params=pltpu.CompilerParams(has_side_effects=True))(*srcs)
    return _per_exchange(exchanges, outs)


def _per_exchange(exchanges, flat):
    out, off = [], 0
    for ex in exchanges:
        out.append(flat[off:off + ex.n])
        off += ex.n
    return out


def _call(body, name, grid, in_specs, out_specs, out_shape, args, scratch=(), sem=None, carry=None):
    n_in, n_out, n_scr = len(in_specs), len(out_specs), len(scratch)
    if not carry:
        outs = pl.pallas_call(body, name=name, grid=grid, in_specs=in_specs, out_specs=out_specs,
                              out_shape=out_shape, scratch_shapes=list(scratch),
                              compiler_params=_params(*sem))(*args)
        return outs, None
    c_in, c_out, shapes, sems, srcs, aliases, split = _carried(carry, n_in, n_out)

    def hosted(*refs):
        ins, outs, scr, parts = split(refs, n_scr)
        first, last = True, True
        for d, size in enumerate(grid):
            first = first & (pl.program_id(d) == 0)
            last = last & (pl.program_id(d) == size - 1)

        @pl.when(first)
        def _():
            for ex, part in zip(carry, parts):
                ex.start(*part)

        body(*ins, *outs, *scr)

        @pl.when(last)
        def _():
            for ex, part in zip(carry, parts):
                ex.wait(*part)

    outs = pl.pallas_call(
        hosted, name=name + "_x", grid=grid, in_specs=list(in_specs) + c_in,
        out_specs=list(out_specs) + c_out, out_shape=list(out_shape) + shapes,
        scratch_shapes=list(scratch) + sems, input_output_aliases=aliases,
        compiler_params=_params(*(["arbitrary"] * len(grid))))(*args, *srcs)
    return outs[:n_out], _per_exchange(carry, outs[n_out:])


def _whole(ref, j):
    return ref


def _cols(width):
    return lambda ref, j: ref.at[(slice(None),) * (len(ref.shape) - 1) + (pl.ds(j * width, width),)]


def _rows(height):
    return lambda ref, j: ref.at[pl.ds(j * height, height)]


def _mm_nn(name, a, w, *, tm, tn, out_dtype, relu2=False, carry=None):
    t, k = a.shape
    n = w.shape[1]

    def body(a_ref, w_ref, o_ref):
        acc = _dot(a_ref[...].astype(BF16), w_ref[...])
        if relu2:
            r = jnp.maximum(acc, 0.0)
            acc = r * r
        o_ref[...] = acc.astype(out_dtype)

    outs, recv = _call(
        body, name, (t // tm, n // tn),
        [pl.BlockSpec((tm, k), lambda i, j: (i, 0)), pl.BlockSpec((k, tn), lambda i, j: (0, j))],
        [pl.BlockSpec((tm, tn), lambda i, j: (i, j))], [jax.ShapeDtypeStruct((t, n), out_dtype)], (a, w),
        sem=("parallel", "parallel"), carry=carry)
    return outs[0] if carry is None else (outs[0], recv)


def _in_proj(a, w, *, tm, carry=None):
    t, k = a.shape
    tn = IN_COLS // 4
    f_tile, f_off = C_HF // tn, C_HF % tn

    def body(a_ref, w_ref, o_ref, f_ref):
        acc = _dot(a_ref[...], w_ref[...])
        o_ref[...] = acc.astype(BF16)

        @pl.when(pl.program_id(1) == f_tile)
        def _():
            f_ref[...] = acc[:, f_off:f_off + WIDTH]

    outs, recv = _call(
        body, "in_proj", (t // tm, IN_COLS // tn),
        [pl.BlockSpec((tm, k), lambda i, j: (i, 0)), pl.BlockSpec((k, tn), lambda i, j: (0, j))],
        [pl.BlockSpec((tm, tn), lambda i, j: (i, j)), pl.BlockSpec((tm, WIDTH), lambda i, j: (i, 0))],
        [jax.ShapeDtypeStruct((t, IN_COLS), BF16), jax.ShapeDtypeStruct((t, WIDTH), F32)], (a, w),
        sem=("parallel", "arbitrary"), carry=carry)
    return outs if carry is None else (outs, recv)


def _linear_ln(name, a, w, resid, g, b, *, tm, carry=None):
    t, k = a.shape
    halves = [slice(0, tm // 2), slice(tm // 2, tm)] if k > D_MODEL else [slice(0, tm)]

    def body(a_ref, w_ref, r_ref, g_ref, b_ref, z_ref, x_ref, xb_ref):
        z = ALPHA * _Lanes(r_ref[s, :] for s in halves) + _ldot(_Lanes(a_ref[s, :] for s in halves), w_ref[...])
        zc = z - _mean(z, axis=-1, keepdims=True)
        y = zc * _rsqrt(_mean(zc * zc, axis=-1, keepdims=True) + LN_EPS) * g_ref[...] + b_ref[...]
        for s, zz, yy in zip(halves, z.xs, y.xs):
            z_ref[s, :] = zz
            x_ref[s, :] = yy
            xb_ref[s, :] = yy.astype(BF16)

    row = pl.BlockSpec((tm, D_MODEL), lambda i: (i, 0))
    outs, recv = _call(
        body, name, (t // tm,),
        [pl.BlockSpec((tm, k), lambda i: (i, 0)), _resident((k, D_MODEL)), row,
         _resident((1, D_MODEL)), _resident((1, D_MODEL))],
        [row, row, row],
        [jax.ShapeDtypeStruct((t, D_MODEL), F32)] * 2 + [jax.ShapeDtypeStruct((t, D_MODEL), BF16)],
        (a, w, resid, g, b), sem=("parallel",), carry=carry)
    return outs if carry is None else (outs, recv)


def _ln_bwd_mm_nt(name, dy, z, g, w, h=None, *, tm, tn, carry=None):
    t = dy.shape[0]
    n = w.shape[0]
    halves = [slice(0, tm // 2), slice(tm // 2, tm)]

    def body(*refs):
        if h is None:
            dy_ref, z_ref, g_ref, w_ref, dz_ref, dzb_ref, o_ref, dg_ref, db_ref = refs
        else:
            dy_ref, z_ref, g_ref, w_ref, h_ref, dz_ref, dzb_ref, o_ref, dg_ref, db_ref = refs

        @pl.when(pl.program_id(0) == 0)
        def _():
            dg_ref[...] = jnp.zeros_like(dg_ref)
            db_ref[...] = jnp.zeros_like(db_ref)

        zv = _Lanes(z_ref[s, :] for s in halves)
        dyv = _Lanes(dy_ref[s, :] for s in halves)
        mu = _mean(zv, axis=-1, keepdims=True)
        zc = zv - mu
        rstd = _rsqrt(_mean(zc * zc, axis=-1, keepdims=True) + LN_EPS)
        xh = zc * rstd
        gdy = dyv * g_ref[...]
        m1 = _mean(gdy, axis=-1, keepdims=True)
        m2 = _mean(gdy * xh, axis=-1, keepdims=True)
        dz = rstd * (gdy - m1 - xh * m2)
        dz_b = dz.astype(BF16)
        for s, a, a_b in zip(halves, dz.xs, dz_b.xs):
            dz_ref[s, :] = a
            dzb_ref[s, :] = a_b
        dg_ref[...] += _sum(dyv * xh, axis=0, keepdims=True).total()
        db_ref[...] += _sum(dyv, axis=0, keepdims=True).total()
        for c in range(n // tn):
            cols = slice(c * tn, (c + 1) * tn)
            acc = _ldot_nt(dz_b, w_ref[cols, :])
            if h is not None:
                acc = acc * (2.0 * _sqrt(_Lanes(h_ref[s, cols] for s in halves).astype(F32)))
            for s, a in zip(halves, acc.xs):
                o_ref[s, cols] = a.astype(BF16)

    row = pl.BlockSpec((tm, D_MODEL), lambda i: (i, 0))
    vec = _resident((1, D_MODEL))
    tile = pl.BlockSpec((tm, n), lambda i: (i, 0))
    in_specs = [row, row, vec, _resident((n, D_MODEL))]
    args = [dy, z, g, w]
    if h is not None:
        in_specs.append(tile)
        args.append(h)
    outs, recv = _call(
        body, name, (t // tm,), in_specs, [row, row, tile, vec, vec],
        [jax.ShapeDtypeStruct((t, D_MODEL), F32), jax.ShapeDtypeStruct((t, D_MODEL), BF16),
         jax.ShapeDtypeStruct((t, n), BF16), jax.ShapeDtypeStruct((1, D_MODEL), F32),
         jax.ShapeDtypeStruct((1, D_MODEL), F32)], args, sem=("arbitrary",), carry=carry)
    return outs if carry is None else (outs, recv)


def _mm_tn(name, a, b, *, tk, tmo, tno, carry=None):
    t, m = a.shape
    n = b.shape[1]
    nk = t // tk

    def body(a_ref, b_ref, o_ref, acc_ref):
        k = pl.program_id(2)
        p = _dot_tn(a_ref[...].astype(BF16), b_ref[...].astype(BF16))

        @pl.when(k == 0)
        def _():
            acc_ref[...] = p

        @pl.when(k > 0)
        def _():
            acc_ref[...] += p

        @pl.when(k == nk - 1)
        def _():
            o_ref[...] = acc_ref[...].astype(BF16)

    outs, recv = _call(
        body, name, (m // tmo, n // tno, nk),
        [pl.BlockSpec((tk, tmo), lambda i, j, k: (k, i)), pl.BlockSpec((tk, tno), lambda i, j, k: (k, j))],
        [pl.BlockSpec((tmo, tno), lambda i, j, k: (i, j))], [jax.ShapeDtypeStruct((m, n), BF16)], (a, b),
        scratch=[pltpu.VMEM((tmo, tno), F32)], sem=("parallel", "parallel", "arbitrary"), carry=carry)
    return outs[0] if carry is None else (outs[0], recv)


def _mm_nt_sum(name, pieces, offsets, w, resid, *, tm, carry=None):
    t = resid.shape[0]
    widths = [p.shape[1] for p in pieces]
    n_p = len(pieces)

    def body(*refs):
        p_refs, w_ref, r_ref, o_ref = refs[:n_p], refs[n_p], refs[n_p + 1], refs[n_p + 2]
        acc = ALPHA * r_ref[...]
        for p_ref, off, wd in zip(p_refs, offsets, widths):
            acc = acc + _dot_nt(p_ref[...], w_ref[:, off:off + wd])
        o_ref[...] = acc

    row = pl.BlockSpec((tm, D_MODEL), lambda i: (i, 0))
    outs, recv = _call(
        body, name, (t // tm,),
        [pl.BlockSpec((tm, wd), lambda i: (i, 0)) for wd in widths] + [_resident(w.shape, single=True), row],
        [row], [jax.ShapeDtypeStruct((t, D_MODEL), F32)], (*pieces, w, resid), sem=("parallel",), carry=carry)
    return outs[0] if carry is None else (outs[0], recv)


def _chunk_mask(rows):
    r = lax.broadcasted_iota(jnp.int32, (rows, rows), 0)
    c = lax.broadcasted_iota(jnp.int32, (rows, rows), 1)
    return ((r // CHUNK) == (c // CHUNK)) & (c <= r)


class _Lanes:
    def __init__(self, xs):
        self.xs = list(xs)

    def _with(self, other, f):
        if isinstance(other, _Lanes):
            return _Lanes([f(a, b) for a, b in zip(self.xs, other.xs)])
        return _Lanes([f(a, other) for a in self.xs])

    def __add__(self, o):
        return self._with(o, lambda a, b: a + b)

    def __radd__(self, o):
        return self._with(o, lambda a, b: b + a)

    def __sub__(self, o):
        return self._with(o, lambda a, b: a - b)

    def __rsub__(self, o):
        return self._with(o, lambda a, b: b - a)

    def __mul__(self, o):
        return self._with(o, lambda a, b: a * b)

    def __rmul__(self, o):
        return self._with(o, lambda a, b: b * a)

    def __truediv__(self, o):
        return self._with(o, lambda a, b: a / b)

    def __rtruediv__(self, o):
        return self._with(o, lambda a, b: b / a)

    def __neg__(self):
        return _Lanes([-a for a in self.xs])

    def __ge__(self, o):
        return self._with(o, lambda a, b: a >= b)

    def __getitem__(self, idx):
        return _Lanes([a[idx] for a in self.xs])

    def astype(self, dtype):
        return _Lanes([a.astype(dtype) for a in self.xs])

    def total(self):
        return functools.reduce(lambda a, b: a + b, self.xs)


def _lift(f):
    def g(*args, **kw):
        lanes = [a for a in args if isinstance(a, _Lanes)]
        if not lanes:
            return f(*args, **kw)
        return _Lanes([f(*[a.xs[i] if isinstance(a, _Lanes) else a for a in args], **kw)
                       for i in range(len(lanes[0].xs))])
    return g


def _concat(parts, axis):
    if isinstance(parts[0], _Lanes):
        return _Lanes([jnp.concatenate([p.xs[i] for p in parts], axis=axis) for i in range(len(parts[0].xs))])
    return jnp.concatenate(parts, axis=axis)


_exp, _log, _abs, _sqrt, _where = _lift(jnp.exp), _lift(jnp.log), _lift(jnp.abs), _lift(jnp.sqrt), _lift(jnp.where)
_sum, _mean, _rsqrt, _bcast = _lift(jnp.sum), _lift(jnp.mean), _lift(lax.rsqrt), _lift(jnp.broadcast_to)
_ldot, _ldot_nt, _ldot_tn = _lift(_dot), _lift(_dot_nt), _lift(_dot_tn)
_lsigmoid = _lift(_sigmoid)


def _mask_sum(mask_b, x, transpose=False):
    f = _ldot_tn if transpose else _ldot
    hi = x.astype(BF16)
    lo = (x - hi.astype(F32)).astype(BF16)
    return f(mask_b, hi) + f(mask_b, lo)


def _chunk_row(x, pos, rows):
    nc = rows // CHUNK

    def one(a):
        a3 = a.reshape(nc, CHUNK, HEAD_DIM)
        return jnp.broadcast_to(a3[:, pos:pos + 1, :], (nc, CHUNK, HEAD_DIM)).reshape(rows, HEAD_DIM)

    return _lift(one)(x)


def _chunk_total(x, rows):
    nc = rows // CHUNK

    def one(a):
        tot = jnp.sum(a.reshape(nc, CHUNK, HEAD_DIM), axis=1, keepdims=True)
        return jnp.broadcast_to(tot, (nc, CHUNK, HEAD_DIM)).reshape(rows, HEAD_DIM)

    return _lift(one)(x)


def _sigmoid_pair(x):
    e = _exp(-_abs(x))
    big = 1.0 / (1.0 + e)
    small = e * big
    pos = x >= 0.0
    return _where(pos, big, small), _where(pos, small, big)


def _hg_gates(q_raw, fl, lb, rows, mask):
    tri = mask.astype(BF16)
    sg, sg_neg = _sigmoid_pair(fl)
    forget = lb + (1.0 - lb) * sg
    k = (1.0 - lb) * sg_neg
    sq = _lsigmoid(q_raw)
    qs = q_raw * sq
    bc = _mask_sum(tri, _log(forget))
    bref = _chunk_row(bc, CHUNK // 2 - 1, rows)
    blast = _chunk_row(bc, CHUNK - 1, rows)
    return dict(tri=tri, sg=sg, sg_neg=sg_neg, forget=forget, k=k, sq=sq, qs=qs,
                e_a=_exp(bc - bref), e_b=_exp(bref - bc), e_q=_exp(bc), e_k=_exp(blast - bc),
                dec=_exp(blast))


HG_GROUP = 4


def _hg_lanes(bsz):
    return [(hh, slice(hh * HEAD_DIM, (hh + 1) * HEAD_DIM), b) for hh in range(HG_GROUP) for b in range(bsz)]


def _hg_read(ref, lanes):
    return _Lanes(ref[b, :, cs].astype(F32) for _, cs, b in lanes)


def _hg_write(ref, lanes, val, offset=0):
    for (_, cs, b), a in zip(lanes, val.xs):
        ref[b, :, offset + cs.start:offset + cs.stop] = a


def _hgrn_fwd(proj, hf, lb, nw, *, bsz, seq, carry=None):
    rows = min(ROWS_HG, seq)
    nt = seq // rows
    nc = rows // CHUNK
    t = bsz * seq

    lanes = _hg_lanes(bsz)

    def body(q_ref, f_ref, v_ref, g_ref, lb_ref, nw_ref, y_ref, o_ref, st_ref, s_scr):
        @pl.when(pl.program_id(1) == 0)
        def _():
            s_scr[...] = jnp.zeros_like(s_scr)

        mask = _chunk_mask(rows)
        lb_v = _Lanes(lb_ref[:, cs] for _, cs, _ in lanes)
        gt = _hg_gates(_hg_read(q_ref, lanes), _hg_read(f_ref, lanes), lb_v, rows, mask)
        v_b = _hg_read(v_ref, lanes).astype(BF16)
        a_b = (gt["qs"] * gt["e_a"]).astype(BF16)
        b_b = (gt["k"] * gt["e_b"]).astype(BF16)
        qi_b = (gt["qs"] * gt["e_q"]).astype(BF16)
        ko_b = (gt["k"] * gt["e_k"]).astype(BF16)
        scores = _where(mask, _ldot_nt(a_b, b_b), 0.0)
        o_intra = _ldot(scores.astype(BF16), v_b)

        s = _Lanes(s_scr[i] for i in range(len(lanes)))
        parts = []
        for n in range(nc):
            sl = slice(n * CHUNK, (n + 1) * CHUNK)
            s_b = s.astype(BF16)
            for (hh, _, b), a in zip(lanes, s_b.xs):
                st_ref[hh, b, n] = a
            parts.append(_ldot_nt(qi_b[sl], s_b))
            s = s * gt["dec"][n * CHUNK:n * CHUNK + 1] + _ldot_tn(v_b[sl], ko_b[sl])
        for i, a in enumerate(s.xs):
            s_scr[i] = a
        o = o_intra + _concat(parts, 0)
        _hg_write(o_ref, lanes, o)
        r = _rsqrt(_mean(o * o, axis=-1, keepdims=True) + RMS_EPS)
        g = _hg_read(g_ref, lanes)
        _hg_write(y_ref, lanes, (o * r * nw_ref[...] * (g * _lsigmoid(g))).astype(BF16))

    wide = HG_GROUP * HEAD_DIM

    def col(base):
        return pl.BlockSpec((bsz, rows, wide), lambda h, j: (0, j, base // wide + h))

    out_tile = pl.BlockSpec((bsz, rows, wide), lambda h, j: (0, j, h))
    p3 = proj.reshape(bsz, seq, IN_COLS)
    outs, recv = _call(
        body, "hgrn_fwd", (HEADS // HG_GROUP, nt),
        [col(C_HQ), out_tile, col(C_HI), col(C_HG),
         pl.BlockSpec((1, wide), lambda h, j: (0, h)), _resident((1, HEAD_DIM))],
        [out_tile, out_tile,
         pl.BlockSpec((HG_GROUP, bsz, nc, HEAD_DIM, HEAD_DIM), lambda h, j: (h, 0, j, 0, 0))],
        [jax.ShapeDtypeStruct((bsz, seq, WIDTH), BF16), jax.ShapeDtypeStruct((bsz, seq, WIDTH), F32),
         jax.ShapeDtypeStruct((HEADS, bsz, seq // CHUNK, HEAD_DIM, HEAD_DIM), BF16)],
        (p3, hf.reshape(bsz, seq, WIDTH), p3, p3, lb, nw),
        scratch=[pltpu.VMEM((len(lanes), HEAD_DIM, HEAD_DIM), F32)],
        sem=("parallel", "arbitrary"), carry=carry)
    outs = [outs[0].reshape(t, WIDTH), outs[1].reshape(t, WIDTH), outs[2]]
    return outs if carry is None else (outs, recv)


def _hgrn_bwd(proj, hf, lb, nw, o_pre, states, dy, *, bsz, seq, carry=None):
    rows = min(ROWS_HG, seq)
    nt = seq // rows
    nc = rows // CHUNK
    t = bsz * seq
    lanes = _hg_lanes(bsz)

    def body(q_ref, f_ref, v_ref, g_ref, lb_ref, nw_ref, o_ref, st_ref, dy_ref, dh_ref, dlb_ref, dnw_ref, ds_scr):
        h, j = pl.program_id(0), pl.program_id(1)

        @pl.when(j == 0)
        def _():
            ds_scr[...] = jnp.zeros_like(ds_scr)
            dlb_ref[...] = jnp.zeros_like(dlb_ref)

        @pl.when((h == 0) & (j == 0))
        def _():
            dnw_ref[...] = jnp.zeros_like(dnw_ref)

        mask = _chunk_mask(rows)
        q_raw = _hg_read(q_ref, lanes)
        lb_v = _Lanes(lb_ref[:, cs] for _, cs, _ in lanes)
        gt = _hg_gates(q_raw, _hg_read(f_ref, lanes), lb_v, rows, mask)
        v_b = _hg_read(v_ref, lanes).astype(BF16)
        a_f = gt["qs"] * gt["e_a"]
        b_f = gt["k"] * gt["e_b"]
        qi_f = gt["qs"] * gt["e_q"]
        ko_f = gt["k"] * gt["e_k"]
        a_b, b_b, qi_b, ko_b = a_f.astype(BF16), b_f.astype(BF16), qi_f.astype(BF16), ko_f.astype(BF16)

        o = _hg_read(o_ref, lanes)
        nw_v = nw_ref[...]
        g = _hg_read(g_ref, lanes)
        dyv = _hg_read(dy_ref, lanes)
        r = _rsqrt(_mean(o * o, axis=-1, keepdims=True) + RMS_EPS)
        sgg = _lsigmoid(g)
        d_g = dyv * (o * r * nw_v) * (sgg * (1.0 + g * (1.0 - sgg)))
        d_on = dyv * (g * sgg)
        dnw_ref[...] += _sum(d_on * o * r, axis=0, keepdims=True).total()
        tt = d_on * nw_v
        d_o = r * tt - o * (r * r * r) * _mean(tt * o, axis=-1, keepdims=True)
        do_b = d_o.astype(BF16)

        sc_b = _where(mask, _ldot_nt(a_b, b_b), 0.0).astype(BF16)
        dsc_b = _where(mask, _ldot_nt(do_b, v_b), 0.0).astype(BF16)
        d_v = _ldot_tn(sc_b, do_b)
        d_a = _ldot(dsc_b, b_b)
        d_bm = _ldot_tn(dsc_b, a_b)

        ds = _Lanes(ds_scr[i] for i in range(len(lanes)))
        dqi_parts, dko_parts, dvi_parts, ddec_parts = [None] * nc, [None] * nc, [None] * nc, [None] * nc
        for n in reversed(range(nc)):
            sl = slice(n * CHUNK, (n + 1) * CHUNK)
            dec_n = gt["dec"][n * CHUNK:n * CHUNK + 1]
            ds_b = ds.astype(BF16)
            s_n = _Lanes(st_ref[hh, b, n] for hh, _, b in lanes)
            dqi_parts[n] = _ldot(do_b[sl], s_n)
            dko_parts[n] = _ldot(v_b[sl], ds_b)
            dvi_parts[n] = _ldot_nt(ko_b[sl], ds_b)
            d_dec = _sum(ds * s_n.astype(F32), axis=0, keepdims=True)
            ddec_parts[n] = _bcast(d_dec * dec_n, (CHUNK, HEAD_DIM))
            ds = ds * dec_n + _ldot_tn(do_b[sl], qi_b[sl])
        for i, a in enumerate(ds.xs):
            ds_scr[i] = a
        d_qi = _concat(dqi_parts, 0)
        d_ko = _concat(dko_parts, 0)
        d_v = d_v + _concat(dvi_parts, 0)

        d_qs = d_a * gt["e_a"] + d_qi * gt["e_q"]
        d_k = d_bm * gt["e_b"] + d_ko * gt["e_k"]
        t_a, t_b, t_q, t_k = d_a * a_f, d_bm * b_f, d_qi * qi_f, d_ko * ko_f
        d_bref = _chunk_total(t_b - t_a, rows)
        d_blast = _chunk_total(t_k, rows) + _concat(ddec_parts, 0)
        pos = lax.broadcasted_iota(jnp.int32, (rows, HEAD_DIM), 0) % CHUNK
        d_bc = (t_a - t_b + t_q - t_k + _where(pos == CHUNK // 2 - 1, d_bref, 0.0)
                + _where(pos == CHUNK - 1, d_blast, 0.0))
        d_logf = _mask_sum(gt["tri"], d_bc, transpose=True)

        sg, sg_neg = gt["sg"], gt["sg_neg"]
        inv_f = 1.0 / gt["forget"]
        common = (1.0 - lb_v) * sg * sg_neg
        d_fl = common * (d_logf * inv_f - d_k)
        d_lb = _sum(sg_neg * (d_logf * inv_f - d_k), axis=0, keepdims=True)
        for (_, cs, _), a in zip(lanes, d_lb.xs):
            dlb_ref[:, cs] += a
        sq = gt["sq"]
        _hg_write(dh_ref, lanes, (d_qs * (sq * (1.0 + q_raw * (1.0 - sq)))).astype(BF16), 0)
        _hg_write(dh_ref, lanes, d_fl.astype(BF16), WIDTH)
        _hg_write(dh_ref, lanes, d_v.astype(BF16), 2 * WIDTH)
        _hg_write(dh_ref, lanes, d_g.astype(BF16), 3 * WIDTH)

    assert HG_GROUP == HEADS, "the combined gradient block needs all heads in one grid step"
    wide = HG_GROUP * HEAD_DIM

    def col(base):
        return pl.BlockSpec((bsz, rows, wide), lambda h, j: (0, nt - 1 - j, base // wide + h))

    tile = pl.BlockSpec((bsz, rows, wide), lambda h, j: (0, nt - 1 - j, h))
    head_vec = pl.BlockSpec((1, wide), lambda h, j: (0, h))
    p3 = proj.reshape(bsz, seq, IN_COLS)
    outs, recv = _call(
        body, "hgrn_bwd", (HEADS // HG_GROUP, nt),
        [col(C_HQ), tile, col(C_HI), col(C_HG), head_vec, _resident((1, HEAD_DIM)), tile,
         pl.BlockSpec((HG_GROUP, bsz, nc, HEAD_DIM, HEAD_DIM), lambda h, j: (h, 0, nt - 1 - j, 0, 0)), tile],
        [pl.BlockSpec((bsz, rows, 4 * WIDTH), lambda h, j: (0, nt - 1 - j, 0)), head_vec, _resident((1, HEAD_DIM))],
        [jax.ShapeDtypeStruct((bsz, seq, 4 * WIDTH), BF16), jax.ShapeDtypeStruct((1, WIDTH), F32),
         jax.ShapeDtypeStruct((1, HEAD_DIM), F32)],
        (p3, hf.reshape(bsz, seq, WIDTH), p3, p3, lb, nw, o_pre.reshape(bsz, seq, WIDTH), states,
         dy.reshape(bsz, seq, WIDTH)),
        scratch=[pltpu.VMEM((len(lanes), HEAD_DIM, HEAD_DIM), F32)],
        sem=("arbitrary", "arbitrary"), carry=carry)
    outs = [outs[0].reshape(t, 4 * WIDTH), outs[1], outs[2]]
    return outs if carry is None else (outs, recv)


def _mem_kv(mem2d, w_k, w_v):
    rows = mem2d.shape[0]

    def body(m_ref, wk_ref, wv_ref, k_ref, v_ref):
        m_b = m_ref[...].astype(BF16)
        k_ref[...] = _dot(m_b, wk_ref[...]).astype(BF16)
        v_ref[...] = _dot(m_b, wv_ref[...]).astype(BF16)

    return pl.pallas_call(
        body, name="mem_kv", grid=(rows // MEM_LEN,),
        in_specs=[pl.BlockSpec((MEM_LEN, D_MODEL), lambda i: (i, 0)), _resident((D_MODEL, WIDTH)),
                  _resident((D_MODEL, WIDTH))],
        out_specs=[pl.BlockSpec((MEM_LEN, WIDTH), lambda i: (i, 0))] * 2,
        out_shape=[jax.ShapeDtypeStruct((rows, WIDTH), BF16)] * 2,
        compiler_params=_params("parallel"),
    )(mem2d, w_k, w_v)


def _softmax_rows(s):
    m = _lift(jnp.max)(s, axis=-1, keepdims=True)
    e = _exp(s - m)
    return e / _sum(e, axis=-1, keepdims=True)


def _attn_fwd(proj, mk, mv, *, tm, seq):
    t = proj.shape[0]
    per_b = seq // tm
    scale = HEAD_DIM ** -0.5

    def body(q_ref, k_ref, v_ref, y_ref):
        heads = [slice(h * HEAD_DIM, (h + 1) * HEAD_DIM) for h in range(HEADS)]
        q_b = _Lanes(q_ref[:, sl] for sl in heads).astype(BF16)
        p = _softmax_rows(_ldot_nt(q_b, _Lanes(k_ref[:, sl] for sl in heads)) * scale)
        out = _ldot(p.astype(BF16), _Lanes(v_ref[:, sl] for sl in heads))
        y_ref[...] = jnp.concatenate(out.xs, axis=-1).astype(BF16)

    kv = pl.BlockSpec((MEM_LEN, WIDTH), lambda i: (i // per_b, 0))
    return pl.pallas_call(
        body, name="attn_fwd", grid=(t // tm,),
        in_specs=[pl.BlockSpec((tm, WIDTH), lambda i: (i, C_MQ // WIDTH)), kv, kv],
        out_specs=pl.BlockSpec((tm, WIDTH), lambda i: (i, 0)),
        out_shape=jax.ShapeDtypeStruct((t, WIDTH), BF16),
        compiler_params=_params("parallel"),
    )(proj, mk, mv)


def _attn_bwd(proj, mk, mv, dy, x_b, *, tm, seq):
    t = proj.shape[0]
    per_b = seq // tm
    scale = HEAD_DIM ** -0.5
    last = t // tm - 1

    def body(q_ref, k_ref, v_ref, dy_ref, x_ref, dq_ref, dk_ref, dv_ref, gw_ref, acc_ref):
        i = pl.program_id(0)

        @pl.when(i % per_b == 0)
        def _():
            dk_ref[...] = jnp.zeros_like(dk_ref)
            dv_ref[...] = jnp.zeros_like(dv_ref)

        @pl.when(i == 0)
        def _():
            acc_ref[...] = jnp.zeros_like(acc_ref)

        heads = [slice(h * HEAD_DIM, (h + 1) * HEAD_DIM) for h in range(HEADS)]
        q_b = _Lanes(q_ref[:, sl] for sl in heads).astype(BF16)
        k_b, v_b = _Lanes(k_ref[:, sl] for sl in heads), _Lanes(v_ref[:, sl] for sl in heads)
        p = _softmax_rows(_ldot_nt(q_b, k_b) * scale)
        dy_b = _Lanes(dy_ref[:, sl] for sl in heads).astype(BF16)
        dp = _ldot_nt(dy_b, v_b)
        d_v = _ldot_tn(p.astype(BF16), dy_b)
        ds_b = (p * (dp - _sum(dp * p, axis=-1, keepdims=True)) * scale).astype(BF16)
        dq_b = jnp.concatenate(_ldot(ds_b, k_b).xs, axis=-1).astype(BF16)
        dq_ref[...] = dq_b
        dk_ref[...] += jnp.concatenate(_ldot_tn(ds_b, q_b).xs, axis=-1)
        dv_ref[...] += jnp.concatenate(d_v.xs, axis=-1)
        acc_ref[...] += _dot_tn(x_ref[...], dq_b)

        @pl.when(i == last)
        def _():
            gw_ref[...] = acc_ref[...].astype(BF16)

    kv = pl.BlockSpec((MEM_LEN, WIDTH), lambda i: (i // per_b, 0))
    tile = pl.BlockSpec((tm, WIDTH), lambda i: (i, 0))
    n_mem = mk.shape[0]
    return pl.pallas_call(
        body, name="attn_bwd", grid=(t // tm,),
        in_specs=[pl.BlockSpec((tm, WIDTH), lambda i: (i, C_MQ // WIDTH)), kv, kv, tile,
                  pl.BlockSpec((tm, D_MODEL), lambda i: (i, 0))],
        out_specs=[tile, kv, kv, _resident((D_MODEL, WIDTH))],
        out_shape=[jax.ShapeDtypeStruct((t, WIDTH), BF16), jax.ShapeDtypeStruct((n_mem, WIDTH), F32),
                   jax.ShapeDtypeStruct((n_mem, WIDTH), F32), jax.ShapeDtypeStruct((D_MODEL, WIDTH), BF16)],
        scratch_shapes=[pltpu.VMEM((D_MODEL, WIDTH), F32)],
        compiler_params=_params("arbitrary"),
    )(proj, mk, mv, dy, x_b)


HALO = 16


def _shift_down(u, halo, k, row):
    out = pltpu.roll(u, k, 0)
    for m in range(k):
        out = jnp.where(row == m, halo[HALO - k + m:HALO - k + m + 1, :], out)
    return out


def _shift_up(u, halo, k, row, tm):
    out = pltpu.roll(u, tm - k, 0)
    for m in range(k):
        out = jnp.where(row == tm - k + m, halo[m:m + 1, :], out)
    return out


def _merge_fwd(proj, y_b, y_c, conv_w, w_branch, b_gate, *, tm, seq, carry=None):
    t = proj.shape[0]
    per_b = seq // tm
    hb = tm // HALO

    def body(cb_ref, cc_ref, ch_ref, cch_ref, chh_ref, ga_ref, gb_ref, gc_ref, yb_ref, yc_ref, cw_ref, wb_ref,
             bg_ref, ya_ref, pa_ref, pb_ref, pc_ref, mg_ref, sa_ref, sb_ref, sc_ref):
        i = pl.program_id(0)
        row = lax.broadcasted_iota(jnp.int32, (tm, WIDTH), 0)
        u = cc_ref[...].astype(F32) * ch_ref[...].astype(F32)
        halo = jnp.where(i % per_b == 0, 0.0, cch_ref[...].astype(F32) * chh_ref[...].astype(F32))
        cw = cw_ref[...]
        y = cw[0:1] * _shift_down(u, halo, 2, row) + cw[1:2] * _shift_down(u, halo, 1, row) + cw[2:3] * u
        ya_b = (cb_ref[...].astype(F32) * y).astype(BF16)
        ya_ref[...] = ya_b
        merged = None
        for idx, (y_in, g_ref, p_ref, s_ref) in enumerate(((ya_b, ga_ref, pa_ref, sa_ref),
                                                            (yb_ref[...], gb_ref, pb_ref, sb_ref),
                                                            (yc_ref[...], gc_ref, pc_ref, sc_ref))):
            p = _dot(y_in, wb_ref[idx])
            p_ref[...] = p.astype(BF16)
            sg = _sigmoid(g_ref[...].astype(F32) + bg_ref[:, idx * D_MODEL:(idx + 1) * D_MODEL])
            s_ref[...] = sg.astype(BF16)
            term = sg * p
            merged = term if merged is None else merged + term
        mg_ref[...] = merged.astype(BF16)

    def half(c):
        return pl.BlockSpec((tm, WIDTH), lambda i: (i, c // WIDTH))

    def prev(c):
        return pl.BlockSpec((HALO, WIDTH), lambda i: (jnp.maximum(i * hb - 1, 0), c // WIDTH))

    def gate(k):
        return pl.BlockSpec((tm, D_MODEL), lambda i: (i, C_GA // D_MODEL + k))

    tile512 = pl.BlockSpec((tm, WIDTH), lambda i: (i, 0))
    tile1k = pl.BlockSpec((tm, D_MODEL), lambda i: (i, 0))
    outs, recv = _call(
        body, "merge_fwd", (t // tm,),
        [half(C_CB), half(C_CC), half(C_CH), prev(C_CC), prev(C_CH), gate(0), gate(1), gate(2),
         tile512, tile512, _resident((CONV_K, WIDTH)), _resident((3, WIDTH, D_MODEL)), _resident((1, 3 * D_MODEL))],
        [tile512] + [tile1k] * 7,
        [jax.ShapeDtypeStruct((t, WIDTH), BF16)] + [jax.ShapeDtypeStruct((t, D_MODEL), BF16)] * 7,
        (proj, proj, proj, proj, proj, proj, proj, proj, y_b, y_c, conv_w, w_branch, b_gate),
        sem=("parallel",), carry=carry)
    return outs if carry is None else (outs, recv)


def _merge_bwd(dmerged, projections, gates, branch_in, w_branch, *, tm):
    t = dmerged.shape[0]
    last = t // tm - 1

    def body(dm_ref, pa_ref, pb_ref, pc_ref, sa_ref, sb_ref, sc_ref, ya_ref, yb_ref, yc_ref, wb_ref,
             dgt_ref, dya_ref, dyb_ref, dyc_ref, dbg_ref, gw_ref, acc_ref):
        i = pl.program_id(0)

        @pl.when(i == 0)
        def _():
            dbg_ref[...] = jnp.zeros_like(dbg_ref)
            acc_ref[...] = jnp.zeros_like(acc_ref)

        dm = dm_ref[...].astype(F32)
        for idx, (p_ref, s_ref, y_ref, dy_ref) in enumerate(((pa_ref, sa_ref, ya_ref, dya_ref),
                                                             (pb_ref, sb_ref, yb_ref, dyb_ref),
                                                             (pc_ref, sc_ref, yc_ref, dyc_ref))):
            cols = slice(idx * D_MODEL, (idx + 1) * D_MODEL)
            sg = s_ref[...].astype(F32)
            dp = dm * sg
            dp_b = dp.astype(BF16)
            dgate = dp * p_ref[...].astype(F32) * (1.0 - sg)
            dgt_ref[:, cols] = dgate.astype(BF16)
            dbg_ref[:, cols] += jnp.sum(dgate, axis=0, keepdims=True)
            dy_ref[...] = _dot_nt(dp_b, wb_ref[idx]).astype(BF16)
            acc_ref[idx] += _dot_tn(y_ref[...], dp_b)

        @pl.when(i == last)
        def _():
            gw_ref[...] = acc_ref[...].astype(BF16)

    tile512 = pl.BlockSpec((tm, WIDTH), lambda i: (i, 0))
    tile1k = pl.BlockSpec((tm, D_MODEL), lambda i: (i, 0))
    return pl.pallas_call(
        body, name="merge_bwd", grid=(t // tm,),
        in_specs=[tile1k] * 7 + [tile512] * 3 + [_resident((3, WIDTH, D_MODEL))],
        out_specs=[pl.BlockSpec((tm, 3 * D_MODEL), lambda i: (i, 0)), tile512, tile512, tile512,
                   _resident((1, 3 * D_MODEL)), _resident((3, WIDTH, D_MODEL))],
        out_shape=[jax.ShapeDtypeStruct((t, 3 * D_MODEL), BF16)] + [jax.ShapeDtypeStruct((t, WIDTH), BF16)] * 3
                  + [jax.ShapeDtypeStruct((1, 3 * D_MODEL), F32), jax.ShapeDtypeStruct((3, WIDTH, D_MODEL), BF16)],
        scratch_shapes=[pltpu.VMEM((3, WIDTH, D_MODEL), F32)],
        compiler_params=_params("arbitrary"),
    )(dmerged, *projections, *gates, *branch_in, w_branch)


def _conv_bwd(proj, dya, conv_w, x_b, *, tm, seq):
    t = proj.shape[0]
    per_b = seq // tm
    hb = tm // HALO
    last_blk = t // HALO - 1
    last = t // tm - 1

    def body(cb_ref, cc_ref, ch_ref, cch_ref, chh_ref, dya_ref, cbn_ref, dyan_ref, cw_ref, x_ref,
             d_ref, dcw_ref, gw_ref, acc_ref):
        i = pl.program_id(0)

        @pl.when(i == 0)
        def _():
            dcw_ref[...] = jnp.zeros_like(dcw_ref)
            acc_ref[...] = jnp.zeros_like(acc_ref)

        row = lax.broadcasted_iota(jnp.int32, (tm, WIDTH), 0)
        cb, cc, ch = cb_ref[...].astype(F32), cc_ref[...].astype(F32), ch_ref[...].astype(F32)
        u = cc * ch
        halo = jnp.where(i % per_b == 0, 0.0, cch_ref[...].astype(F32) * chh_ref[...].astype(F32))
        u1 = _shift_down(u, halo, 1, row)
        u2 = _shift_down(u, halo, 2, row)
        cw = cw_ref[...]
        y = cw[0:1] * u2 + cw[1:2] * u1 + cw[2:3] * u
        dya = dya_ref[...].astype(F32)
        dy = dya * cb
        nxt = jnp.where(i % per_b == per_b - 1, 0.0, dyan_ref[...].astype(F32) * cbn_ref[...].astype(F32))
        du = cw[2:3] * dy + cw[1:2] * _shift_up(dy, nxt, 1, row, tm) + cw[0:1] * _shift_up(dy, nxt, 2, row, tm)
        d_ref[:, 0:WIDTH] = (dya * y).astype(BF16)
        d_ref[:, WIDTH:2 * WIDTH] = (du * ch).astype(BF16)
        d_ref[:, 2 * WIDTH:3 * WIDTH] = (du * cc).astype(BF16)
        dcw_ref[0:1, :] += jnp.sum(dy * u2, axis=0, keepdims=True)
        dcw_ref[1:2, :] += jnp.sum(dy * u1, axis=0, keepdims=True)
        dcw_ref[2:3, :] += jnp.sum(dy * u, axis=0, keepdims=True)
        acc_ref[...] += _dot_tn(x_ref[...], d_ref[...])

        @pl.when(i == last)
        def _():
            gw_ref[...] = acc_ref[...].astype(BF16)

    def half(c):
        return pl.BlockSpec((tm, WIDTH), lambda i: (i, c // WIDTH))

    def prev(c):
        return pl.BlockSpec((HALO, WIDTH), lambda i: (jnp.maximum(i * hb - 1, 0), c // WIDTH))

    def nxt(c):
        return pl.BlockSpec((HALO, WIDTH), lambda i: (jnp.minimum((i + 1) * hb, last_blk), c // WIDTH))

    return pl.pallas_call(
        body, name="conv_bwd", grid=(t // tm,),
        in_specs=[half(C_CB), half(C_CC), half(C_CH), prev(C_CC), prev(C_CH),
                  pl.BlockSpec((tm, WIDTH), lambda i: (i, 0)), nxt(C_CB), nxt(0), _resident((CONV_K, WIDTH)),
                  pl.BlockSpec((tm, D_MODEL), lambda i: (i, 0))],
        out_specs=[pl.BlockSpec((tm, 3 * WIDTH), lambda i: (i, 0)), _resident((CONV_K, WIDTH)),
                   _resident((D_MODEL, 3 * WIDTH))],
        out_shape=[jax.ShapeDtypeStruct((t, 3 * WIDTH), BF16), jax.ShapeDtypeStruct((CONV_K, WIDTH), F32),
                   jax.ShapeDtypeStruct((D_MODEL, 3 * WIDTH), BF16)],
        scratch_shapes=[pltpu.VMEM((D_MODEL, 3 * WIDTH), F32)],
        compiler_params=_params("arbitrary"),
    )(proj, proj, proj, proj, proj, dya, proj, dya, conv_w, x_b)


def _loss_head(y, target, *, tm):
    t = y.shape[0]

    def body(y_ref, t_ref, dy_ref, l_ref):
        @pl.when(pl.program_id(0) == 0)
        def _():
            l_ref[...] = jnp.zeros_like(l_ref)

        err = y_ref[...] - t_ref[...]
        dy_ref[...] = err * (1.0 / D_MODEL)
        per_row = jnp.sum(err * err, axis=-1, keepdims=True) * (1.0 / D_MODEL)
        l_ref[...] += 0.5 * jnp.sum(per_row, axis=0, keepdims=True)

    row = pl.BlockSpec((tm, D_MODEL), lambda i: (i, 0))
    return pl.pallas_call(
        body, name="loss_head", grid=(t // tm,),
        in_specs=[row, row], out_specs=[row, _resident((8, 128))],
        out_shape=[jax.ShapeDtypeStruct((t, D_MODEL), F32), jax.ShapeDtypeStruct((8, 128), F32)],
        compiler_params=_params("arbitrary"),
    )(y, target)


def _lb_softmax(lower_bounds):
    x = lower_bounds
    e = jnp.exp(x - jnp.max(x, axis=0, keepdims=True))
    return e / jnp.sum(e, axis=0, keepdims=True)


def _lb_fwd(lower_bounds):
    def body(x_ref, o_ref):
        s = _lb_softmax(x_ref[...])
        c = s[0:1]
        o_ref[0:1, :] = c - s[0:1]
        for l in range(1, DEPTH):
            c = c + s[l:l + 1]
            o_ref[l:l + 1, :] = c - s[0:1]

    return pl.pallas_call(body, name="lb_fwd", out_shape=jax.ShapeDtypeStruct(lower_bounds.shape, F32))(lower_bounds)


def _lb_bwd(lower_bounds, d_lb_all):
    def body(x_ref, d_ref, o_ref):
        s = _lb_softmax(x_ref[...])
        d = d_ref[...]
        rows = [jnp.zeros_like(d[0:1])]
        for j in range(1, DEPTH):
            acc = d[j:j + 1]
            for l in range(j + 1, DEPTH):
                acc = acc + d[l:l + 1]
            rows.append(acc)
        inner = rows[0] * s[0:1]
        for j in range(1, DEPTH):
            inner = inner + rows[j] * s[j:j + 1]
        for j in range(DEPTH):
            o_ref[j:j + 1, :] = s[j:j + 1] * (rows[j] - inner)

    return pl.pallas_call(body, name="lb_bwd", out_shape=jax.ShapeDtypeStruct(lower_bounds.shape, F32))(
        lower_bounds, d_lb_all)


def _adamw(w, g, m, v):
    m2 = ADAM_B1 * m + (1.0 - ADAM_B1) * g
    v2 = ADAM_B2 * v + (1.0 - ADAM_B2) * (g * g)
    m_hat = m2 / (1.0 - ADAM_B1 ** ADAM_STEP)
    v_hat = v2 / (1.0 - ADAM_B2 ** ADAM_STEP)
    delta = -ADAM_LR * (m_hat / (jnp.sqrt(v_hat) + ADAM_EPS) + ADAM_WD * w)
    return delta, m2, v2


def _adam_small(name, g, w, m, v):
    shape = w.shape
    flat = (-1, shape[-1])
    g2, w2, m2, v2 = (a.reshape(flat) for a in (g, w, m, v))

    def body(g_ref, w_ref, m_ref, v_ref, d_ref, mo_ref, vo_ref):
        d, mm, vv = _adamw(w_ref[...], g_ref[...], m_ref[...], v_ref[...])
        d_ref[...] = d
        mo_ref[...] = mm
        vo_ref[...] = vv

    outs = pl.pallas_call(body, name=name, out_shape=[jax.ShapeDtypeStruct(w2.shape, F32)] * 3)(g2, w2, m2, v2)
    return [o.reshape(shape) for o in outs]


def _adam_shard(name, recvs, w, m, v, *, tr):
    _, r, c = w.shape

    def body(*refs):
        rc, (w_ref, m_ref, v_ref), (g_ref, d_ref, mo_ref, vo_ref) = refs[:DEPTH], refs[DEPTH:DEPTH + 3], refs[DEPTH + 3:]
        layer = pl.program_id(0)
        for cand in range(DEPTH):
            @pl.when(layer == cand)
            def _():
                g = rc[cand][0].astype(F32)
                for d in range(1, N_DEV):
                    g = g + rc[cand][d].astype(F32)
                dl, mm, vv = _adamw(w_ref[...], g, m_ref[...], v_ref[...])
                g_ref[...] = g
                d_ref[...] = dl
                mo_ref[...] = mm
                vo_ref[...] = vv

    def recv_spec(cand):
        return pl.BlockSpec((N_DEV, tr, c), lambda l, i: (0, jnp.where(l == cand, i, 0), 0))

    tile = pl.BlockSpec((None, tr, c), lambda l, i: (l, i, 0))
    return pl.pallas_call(
        body, name=name, grid=(DEPTH, r // tr),
        in_specs=[recv_spec(cand) for cand in range(DEPTH)] + [tile] * 3,
        out_specs=[tile] * 4,
        out_shape=[jax.ShapeDtypeStruct(w.shape, F32)] * 4,
        compiler_params=_params("parallel", "parallel"),
    )(*recvs, w, m, v)


def _sum_devices(name, x):
    def body(x_ref, o_ref):
        acc = x_ref[0]
        for d in range(1, N_DEV):
            acc = acc + x_ref[d]
        o_ref[...] = acc

    return pl.pallas_call(body, name=name, out_shape=jax.ShapeDtypeStruct(x.shape[1:], x.dtype))(x)


SMALL = (("lower_bounds", 1, 512), ("conv_w", CONV_K, WIDTH), ("hg_norm_w", 1, HEAD_DIM), ("b_gate", 3, D_MODEL),
         ("ln1_g", 1, D_MODEL), ("ln1_b", 1, D_MODEL), ("ln2_g", 1, D_MODEL), ("ln2_b", 1, D_MODEL))
SMALL_ROWS = sum(r for _, r, _ in SMALL)


def _pack_small(per_layer):
    flat = [a for layer in per_layer for a in layer]

    def body(*refs):
        ins, o_ref = refs[:-1], refs[-1]
        o_ref[...] = jnp.zeros_like(o_ref)
        it = iter(ins)
        for l in range(DEPTH):
            row = l * SMALL_ROWS
            for name, nrows, ncols in SMALL:
                ref = next(it)
                if name == "b_gate":
                    for k in range(nrows):
                        o_ref[row + k:row + k + 1, :] = ref[:, k * ncols:(k + 1) * ncols]
                else:
                    o_ref[row:row + nrows, 0:ncols] = ref[...]
                row += nrows

    return pl.pallas_call(body, name="pack_small_grads",
                          out_shape=jax.ShapeDtypeStruct((DEPTH * SMALL_ROWS, D_MODEL), F32))(*flat)


def _unpack_small(summed):
    s3 = summed.reshape(DEPTH, SMALL_ROWS, D_MODEL)
    out, row = {}, 0
    for name, nrows, ncols in SMALL:
        out[name] = s3[:, row:row + nrows, :ncols].reshape(DEPTH, nrows * ncols)
        row += nrows
    return out


def _natural_cols(g):
    nd = g.ndim
    perm = tuple(range(1, nd - 1)) + (0, nd - 1)
    t = jnp.transpose(g, perm)
    return t.reshape(t.shape[:-2] + (t.shape[-2] * t.shape[-1],))


def _natural_rows(g):
    return g.reshape(g.shape[0] * g.shape[1], g.shape[2])


def _hosted(hosts, key, fn):
    pairs = hosts.get(key) if hosts else None
    if callable(pairs):
        pairs = pairs()
    if not pairs:
        return fn(None)
    outs, recvs = fn([ex for ex, _ in pairs])
    for (_, hook), recv in zip(pairs, recvs):
        hook(recv)
    return outs


def _layer_fwd(cur, cur_b, mem2d, wl, *, bsz, seq, hosts=None):
    tm = min(512, seq)
    proj, hf = _hosted(hosts, "in_proj", lambda c: _in_proj(cur_b, wl["w_in"], tm=min(1024, seq), carry=c))
    y_b, o_pre, states = _hosted(hosts, "hgrn_fwd", lambda c: _hgrn_fwd(proj, hf, wl["lb"], wl["nw"], bsz=bsz,
                                                                         seq=seq, carry=c))
    mk, mv = _mem_kv(mem2d, wl["w_mk"], wl["w_mv"])
    y_c = _attn_fwd(proj, mk, mv, tm=tm, seq=seq)
    y_a, pa, pb, pc, merged, sga, sgb, sgc = _hosted(
        hosts, "merge_fwd", lambda c: _merge_fwd(proj, y_b, y_c, wl["conv"], wl["w_br"], wl["b_gate"], tm=tm,
                                                 seq=seq, carry=c))
    z1, x1, x1_b = _hosted(hosts, "wo_ln", lambda c: _linear_ln("wo_ln", merged, wl["w_o"], cur, wl["ln1_g"],
                                                                  wl["ln1_b"], tm=tm, carry=c))
    hid = _hosted(hosts, "mlp_up", lambda c: _mm_nn("mlp_up", x1_b, wl["w_up"], tm=min(1024, seq), tn=2048,
                                                     out_dtype=BF16, relu2=True, carry=c))
    z2, x2, x2_b = _linear_ln("down_ln", hid, wl["w_down"], x1, wl["ln2_g"], wl["ln2_b"], tm=tm)
    return dict(x_b=cur_b, proj=proj, hf=hf, y_a=y_a, y_b=y_b, y_c=y_c, o_pre=o_pre, states=states, mk=mk, mv=mv,
                proj3=(pa, pb, pc), gates3=(sga, sgb, sgc), merged=merged, z1=z1, x1_b=x1_b, hid=hid, z2=z2, x2=x2,
                x2_b=x2_b)


def _layer_bwd(dcur, mem2d, s, wl, *, bsz, seq, plan=None):
    tm = min(512, seq)
    tk = min(2048, bsz * seq)
    g = {}

    def run(key, fn):
        made = plan[key](g) if plan and key in plan else None
        return _hosted({key: [made]} if made else None, key, fn)

    dz2, dz2_b, dhpre, d_ln2g, d_ln2b = run(
        "ln2_bwd_down", lambda c: _ln_bwd_mm_nt("ln2_bwd_down", dcur, s["z2"], wl["ln2_g"], wl["w_down"],
                                                s["hid"], tm=tm, tn=1024, carry=c))
    g["w_down"] = _mm_tn("grad_w_down", s["hid"], dz2_b, tk=tk, tmo=1024, tno=1024)
    dx1 = _mm_nt_sum("mlp_up_bwd", [dhpre], [0], wl["w_up"], dz2, tm=tm)
    g["w_up"] = run("grad_w_up", lambda c: _mm_tn("grad_w_up", s["x1_b"], dhpre, tk=tk, tmo=1024, tno=2048, carry=c))
    dz1, dz1_b, dmerged, d_ln1g, d_ln1b = _ln_bwd_mm_nt("ln1_bwd_wo", dx1, s["z1"], wl["ln1_g"], wl["w_o"],
                                                        tm=tm, tn=1024)
    g["w_o"] = _mm_tn("grad_w_o", s["merged"], dz1_b, tk=tk, tmo=1024, tno=1024)
    dgate, dya, dyb, dyc, d_bg, g["w_br"] = _merge_bwd(dmerged, s["proj3"], s["gates3"],
                                                       (s["y_a"], s["y_b"], s["y_c"]), wl["w_br"], tm=tm)
    d_conv, d_cw, gw_conv = _conv_bwd(s["proj"], dya, wl["conv"], s["x_b"], tm=tm, seq=seq)
    dhg, d_lb, d_nw = run(
        "hgrn_bwd", lambda c: _hgrn_bwd(s["proj"], s["hf"], wl["lb"], wl["nw"], s["o_pre"], s["states"], dyb,
                                        bsz=bsz, seq=seq, carry=c))
    dmq, dmk, dmv, gw_mq = _attn_bwd(s["proj"], s["mk"], s["mv"], dyc, s["x_b"], tm=tm, seq=seq)
    tkm = min(512, mem2d.shape[0])
    g["w_mk"] = _mm_tn("grad_w_mem", mem2d, dmk, tk=tkm, tmo=1024, tno=512)
    g["w_mv"] = _mm_tn("grad_w_mem", mem2d, dmv, tk=tkm, tmo=1024, tno=512)
    pieces = [d_conv, dhg, dmq, dgate]
    offsets = [C_CB, C_HQ, C_MQ, C_GA]
    g["w_in"] = jnp.concatenate(
        [gw_conv, _mm_tn("grad_w_in_hgrn", s["x_b"], dhg, tk=tk, tmo=1024, tno=2048), gw_mq,
         _mm_tn("grad_w_in_gates", s["x_b"], dgate, tk=tk, tmo=1024, tno=1536)], axis=1)
    dx = run("in_proj_bwd", lambda c: _mm_nt_sum("in_proj_bwd", pieces, offsets, wl["w_in"], dz1,
                                                 tm=min(512, seq), carry=c))
    return dx, g, [d_lb, d_cw, d_nw, d_bg, d_ln1g, d_ln1b, d_ln2g, d_ln2b]


def kernel(x, mem, lower_bounds, w_in, conv_w, hg_norm_w, w_mem_k, w_mem_v, w_branch, b_gate, w_o, ln1_g, ln1_b, w_up, w_down, ln2_g, ln2_b, loss_target, m_lower_bounds, m_w_in, m_conv_w, m_hg_norm_w, m_w_mem_k, m_w_mem_v, m_w_branch, m_b_gate, m_w_o, m_ln1_g, m_ln1_b, m_w_up, m_w_down, m_ln2_g, m_ln2_b, v_lower_bounds, v_w_in, v_conv_w, v_hg_norm_w, v_w_mem_k, v_w_mem_v, v_w_branch, v_b_gate, v_w_o, v_ln1_g, v_ln1_b, v_w_up, v_w_down, v_ln2_g, v_ln2_b):
    bsz, seq, _ = x.shape
    t = bsz * seq
    me = _my_id()

    sh = dict(w_in=w_in.astype(BF16), w_mk=w_mem_k.astype(BF16), w_mv=w_mem_v.astype(BF16),
              w_br=w_branch.astype(BF16), w_o=w_o.astype(BF16), w_up=w_up.astype(BF16), w_down=w_down.astype(BF16))
    half_rows = D_MODEL // 2
    sh["w_in_a"], sh["w_in_b"] = sh["w_in"][:, :half_rows], sh["w_in"][:, half_rows:]
    natural = dict(w_in=_natural_cols, w_in_a=_natural_cols, w_in_b=_natural_cols, w_mk=_natural_rows,
                   w_mv=_natural_rows, w_br=_natural_cols, w_o=_natural_rows, w_up=_natural_cols,
                   w_down=_natural_rows)

    lb_all = _lb_fwd(lower_bounds)
    layer_w = [dict(lb=lb_all[l][None], nw=hg_norm_w[l][None], b_gate=b_gate[l][None], ln1_g=ln1_g[l][None],
                    ln1_b=ln1_b[l][None], ln2_g=ln2_g[l][None], ln2_b=ln2_b[l][None]) for l in range(DEPTH)]
    half_full = {}

    def near(names, l):
        srcs = [sh[n][l] for n in names]
        ex = _Exchange(srcs, piece_shapes=[s_.shape for s_ in srcs], route="near")
        return ex, lambda recv_: half_full.update({(n, l): r for n, r in zip(names, recv_)})

    def relay(names, l):
        ex = _Exchange([half_full.pop((n, l)) for n in names], route="relay")

        def hook(recv_):
            for n, r in zip(names, recv_):
                layer_w[l][n] = natural[n](r)
        return ex, hook

    small4 = ["w_mk", "w_mv", "w_br", "w_o"]
    conv_shard = conv_w.reshape(DEPTH * CONV_K * (WIDTH // N_DEV) // 128, 128)
    first = near(["w_in"], 0)
    conv_ex = _Exchange([conv_shard], piece_shapes=[conv_shard.shape])
    got = _exchange("gather_first", [first[0], conv_ex])
    first[1](got[0])
    conv_full = _natural_cols(got[1][0].reshape(N_DEV, DEPTH, CONV_K, WIDTH // N_DEV))
    second = relay(["w_in"], 0)
    second[1](_exchange("relay_first", [second[0]])[0])

    x2d = x.reshape(t, D_MODEL)
    mem2d = mem.reshape(bsz * MEM_LEN, D_MODEL)
    target2d = loss_target.reshape(t, D_MODEL)

    saved = []
    cur, cur_b = x2d, x2d.astype(BF16)
    for l in range(DEPTH):
        wl = layer_w[l]
        wl["conv"] = conv_full[l]
        more = l + 1 < DEPTH
        now = ["w_up", "w_down"] + ([] if l else small4)
        hosts = {"in_proj": [near(now, l)],
                 "hgrn_fwd": lambda l=l, more=more, now=now: [relay(now, l)] + ([near(["w_in"], l + 1)] if more else [])}
        if more:
            hosts["merge_fwd"] = lambda l=l: [relay(["w_in"], l + 1), near(small4, l + 1)]
            hosts["mlp_up"] = lambda l=l: [relay(small4, l + 1)]
        s = _layer_fwd(cur, cur_b, mem2d, wl, bsz=bsz, seq=seq, hosts=hosts)
        saved.append(s)
        cur, cur_b = s["x2"], s["x2_b"]

    dcur, loss_tile = _loss_head(cur, target2d, tm=min(512, seq))
    loss = lax.psum(loss_tile[0, 0], ("x", "y", "c"))

    in_w = IN_COLS // N_DEV

    def in_half(r):
        return lambda ref, j: ref.at[pl.ds(r * half_rows, half_rows), pl.ds(j * in_w, in_w)]

    slicer = dict(w_in_a=in_half(0), w_in_b=in_half(1), w_mk=_rows(D_MODEL // N_DEV), w_mv=_rows(D_MODEL // N_DEV),
                  w_br=_cols(D_MODEL // N_DEV), w_o=_rows(D_MODEL // N_DEV), w_up=_cols(D_FF // N_DEV),
                  w_down=_rows(D_FF // N_DEV))
    source = dict(w_in_a="w_in", w_in_b="w_in")
    recv = [dict() for _ in range(DEPTH)]

    def scatter_of(names, g, into):
        ex = _Exchange([g[source.get(n, n)] for n in names], [slicer[n] for n in names],
                       [sh[n].shape[1:] for n in names])
        return ex, lambda recv_: into.update(zip(names, recv_))

    small_rows = [None] * DEPTH
    prev = None
    rest = ["w_in_b", "w_mk", "w_mv"]
    for l in reversed(range(DEPTH)):
        plan = {"grad_w_up": lambda g, l=l: scatter_of(["w_down"], g, recv[l]),
                "hgrn_bwd": lambda g, l=l: scatter_of(["w_up", "w_o", "w_br"], g, recv[l])}
        if l == 0:
            plan["in_proj_bwd"] = lambda g: scatter_of(["w_in_a"] + rest, g, recv[0])
        else:
            plan["in_proj_bwd"] = lambda g, l=l: scatter_of(["w_in_a"], g, recv[l])
        if prev is not None:
            plan["ln2_bwd_down"] = lambda g, l=l, prev=prev: scatter_of(rest, prev, recv[l + 1])
        dcur, prev, small_rows[l] = _layer_bwd(dcur, mem2d, saved[l], layer_w[l], bsz=bsz, seq=seq, plan=plan)
    for r in recv:
        r["w_in"] = jnp.concatenate([r.pop("w_in_a"), r.pop("w_in_b")], axis=1)

    packed = _pack_small(small_rows)
    all_small = _exchange("gather_small_grads", [_Exchange([packed], piece_shapes=[packed.shape])])[0][0]
    small_grads = _unpack_small(_sum_devices("sum_small_grads", all_small))
    small_grads["lower_bounds"] = _lb_bwd(lower_bounds, small_grads["lower_bounds"])
    conv_all = small_grads["conv_w"].reshape(DEPTH, CONV_K, WIDTH)
    small_grads["conv_w"] = lax.dynamic_slice_in_dim(conv_all, me * (WIDTH // N_DEV), WIDTH // N_DEV, axis=2)

    grads, deltas, new_m, new_v = {}, {}, {}, {}
    given = dict(lower_bounds=(lower_bounds, m_lower_bounds, v_lower_bounds), conv_w=(conv_w, m_conv_w, v_conv_w),
                 hg_norm_w=(hg_norm_w, m_hg_norm_w, v_hg_norm_w), b_gate=(b_gate, m_b_gate, v_b_gate),
                 ln1_g=(ln1_g, m_ln1_g, v_ln1_g), ln1_b=(ln1_b, m_ln1_b, v_ln1_b),
                 ln2_g=(ln2_g, m_ln2_g, v_ln2_g), ln2_b=(ln2_b, m_ln2_b, v_ln2_b))
    for name, (w_, m_, v_) in given.items():
        g_ = small_grads[name].reshape(w_.shape)
        grads[name] = g_
        deltas[name], new_m[name], new_v[name] = _adam_small("adam_" + name, g_, w_, m_, v_)

    big = dict(w_in=("w_in", w_in, m_w_in, v_w_in, 128), w_mem_k=("w_mk", w_mem_k, m_w_mem_k, v_w_mem_k, 128),
               w_mem_v=("w_mv", w_mem_v, m_w_mem_v, v_w_mem_v, 128),
               w_branch=("w_br", w_branch, m_w_branch, v_w_branch, 512), w_o=("w_o", w_o, m_w_o, v_w_o, 128),
               w_up=("w_up", w_up, m_w_up, v_w_up, 256), w_down=("w_down", w_down, m_w_down, v_w_down, 128))
    for name, (k, w_, m_, v_, tr) in big.items():
        shape = w_.shape
        flat = (DEPTH, -1, shape[-1])
        rc = [recv[l][k].reshape((N_DEV,) + w_.reshape(flat).shape[1:]) for l in range(DEPTH)]
        outs = _adam_shard("adam_" + name, rc, w_.reshape(flat), m_.reshape(flat), v_.reshape(flat), tr=tr)
        grads[name], deltas[name], new_m[name], new_v[name] = (o.reshape(shape) for o in outs)

    order = ["lower_bounds", "w_in", "conv_w", "hg_norm_w", "w_mem_k", "w_mem_v", "w_branch", "b_gate", "w_o",
             "ln1_g", "ln1_b", "w_up", "w_down", "ln2_g", "ln2_b"]
    return (loss, dcur.reshape(x.shape), *[grads[n] for n in order], *[deltas[n] for n in order],
            *[new_m[n] for n in order], *[new_v[n] for n in order])
```

```python
import functools

import jax
import jax.numpy as jnp
from jax import lax
from jax.experimental import pallas as pl
from jax.experimental.pallas import tpu as pltpu

F32 = jnp.float32
BF16 = jnp.bfloat16

N_DEV = 8
D_MODEL = 1024
DEPTH = 4
MEM_LEN = 256
CONV_K = 3
WIDTH = 512
HEADS = 4
HEAD_DIM = 128
CHUNK = 32
D_FF = 4 * D_MODEL
IN_COLS = 7168
ALPHA = (2.0 * DEPTH) ** 0.25
LN_EPS = 1e-5
RMS_EPS = 1e-6
ADAM_LR = 0.001
ADAM_B1 = 0.9
ADAM_B2 = 0.999
ADAM_EPS = 1e-08
ADAM_WD = 0.01
ADAM_STEP = 10

C_CB, C_CC, C_CH, C_HQ, C_HF, C_HI, C_HG, C_MQ, C_GA = 0, 512, 1024, 1536, 2048, 2560, 3072, 3584, 4096

ROWS_HG = 256
NT_DIMS = (((1,), (1,)), ((), ()))
TN_DIMS = (((0,), (0,)), ((), ()))
MESH = pl.DeviceIdType.MESH


def _dot(a, b):
    return jnp.dot(a, b, preferred_element_type=F32)


def _dot_nt(a, b):
    return lax.dot_general(a, b, NT_DIMS, preferred_element_type=F32)


def _dot_tn(a, b):
    return lax.dot_general(a, b, TN_DIMS, preferred_element_type=F32)


def _sigmoid(x):
    return 1.0 / (1.0 + jnp.exp(-x))


def _params(*sem):
    return pltpu.CompilerParams(dimension_semantics=sem)


def _resident(shape, single=False):
    nd = len(shape)
    if single:
        return pl.BlockSpec(shape, lambda *_: (0,) * nd, pipeline_mode=pl.Buffered(1))
    return pl.BlockSpec(shape, lambda *_: (0,) * nd)


def _my_id():
    return 4 * lax.axis_index("x") + 2 * lax.axis_index("y") + lax.axis_index("c")


class _Exchange:
    def __init__(self, srcs, slicers=None, piece_shapes=None, route="all"):
        self.srcs, self.n, self.route = list(srcs), len(srcs), route
        self.slicers = list(slicers) if slicers else [_whole] * self.n
        any_spec = pl.BlockSpec(memory_space=pl.ANY)
        self.in_specs = [any_spec] * self.n
        self.out_specs = [any_spec] * self.n
        if route == "relay":
            self.out_shape = [jax.ShapeDtypeStruct(a.shape, a.dtype) for a in srcs]
        else:
            self.out_shape = [jax.ShapeDtypeStruct((N_DEV,) + tuple(s), a.dtype) for s, a in zip(piece_shapes, srcs)]
        self.aliased = route == "relay"
        self.scratch = [pltpu.SemaphoreType.DMA((self.n * N_DEV,)), pltpu.SemaphoreType.DMA((self.n * N_DEV,)),
                        pltpu.SemaphoreType.DMA((self.n,))]

    def _peer(self, j, me):
        if self.route == "all":
            return j != me
        return (j != me) & ((j % 2 == lax.axis_index("c")) | (j // 2 == me // 2))

    def _remote(self, ins, outs, sems, k, j, me):
        return pltpu.make_async_remote_copy(
            src_ref=self.slicers[k](ins[k], j), dst_ref=outs[k].at[me],
            send_sem=sems[0].at[k * N_DEV + j], recv_sem=sems[1].at[k * N_DEV + me],
            device_id=(j // 4, (j // 2) % 2, j % 2), device_id_type=MESH)

    def _local(self, ins, outs, sems, k, j, me):
        return pltpu.make_async_copy(self.slicers[k](ins[k], j), outs[k].at[me], sems[2].at[k])

    def _relay(self, outs, sems, k, j):
        sibling = (lax.axis_index("x"), lax.axis_index("y"), 1 - lax.axis_index("c"))
        return pltpu.make_async_remote_copy(
            src_ref=outs[k].at[j], dst_ref=outs[k].at[j], send_sem=sems[0].at[k * N_DEV + j],
            recv_sem=sems[1].at[k * N_DEV + j], device_id=sibling, device_id_type=MESH)

    def _other_chip(self, j, same_core):
        on_my_core = j % 2 == lax.axis_index("c")
        return (on_my_core if same_core else ~on_my_core) & (j // 2 != _my_id() // 2)

    def start(self, ins, outs, sems):
        me = _my_id()
        for k in range(self.n):
            for j in range(N_DEV):
                if self.route == "relay":
                    @pl.when(self._other_chip(j, True))
                    def _():
                        self._relay(outs, sems, k, j).start()
                    continue

                @pl.when(self._peer(j, me))
                def _():
                    self._remote(ins, outs, sems, k, j, me).start()

                @pl.when(j == me)
                def _():
                    self._local(ins, outs, sems, k, j, me).start()

    def wait(self, ins, outs, sems):
        me = _my_id()
        for k in range(self.n):
            for j in range(N_DEV):
                if self.route == "relay":
                    @pl.when(self._other_chip(j, False))
                    def _():
                        self._relay(outs, sems, k, j).wait_recv()

                    @pl.when(self._other_chip(j, True))
                    def _():
                        self._relay(outs, sems, k, j).wait_send()
                    continue

                @pl.when(self._peer(j, me))
                def _():
                    pltpu.make_async_remote_copy(
                        src_ref=self.slicers[k](ins[k], j), dst_ref=outs[k].at[j],
                        send_sem=sems[0].at[k * N_DEV + j], recv_sem=sems[1].at[k * N_DEV + j],
                        device_id=(j // 4, (j // 2) % 2, j % 2), device_id_type=MESH).wait_recv()
                    self._remote(ins, outs, sems, k, j, me).wait_send()

                @pl.when(j == me)
                def _():
                    self._local(ins, outs, sems, k, j, me).wait()


def _carried(exchanges, n_in, n_out):
    c_in = [s for ex in exchanges for s in ex.in_specs]
    c_out = [s for ex in exchanges for s in ex.out_specs]
    shapes = [s for ex in exchanges for s in ex.out_shape]
    sems = [s for ex in exchanges for s in ex.scratch]
    srcs = [a for ex in exchanges for a in ex.srcs]
    aliases, off = {}, 0
    for ex in exchanges:
        if ex.aliased:
            aliases.update({n_in + off + k: n_out + off + k for k in range(ex.n)})
        off += ex.n
    total = off

    def split(refs, n_scr):
        ins, outs = refs[:n_in], refs[n_in + total:n_in + total + n_out]
        rest = refs[n_in + 2 * total + n_out:]
        scr, sem_refs = rest[:n_scr], rest[n_scr:]
        parts, off_ = [], 0
        for i, ex in enumerate(exchanges):
            parts.append((refs[n_in + off_:n_in + off_ + ex.n],
                          refs[n_in + total + n_out + off_:n_in + total + n_out + off_ + ex.n],
                          sem_refs[3 * i:3 * i + 3]))
            off_ += ex.n
        return ins, outs, scr, parts

    return c_in, c_out, shapes, sems, srcs, aliases, split


def _exchange(name, exchanges):
    c_in, c_out, shapes, sems, srcs, aliases, split = _carried(exchanges, 0, 0)

    def body(*refs):
        _, _, _, parts = split(refs, 0)
        for ex, part in zip(exchanges, parts):
            ex.start(*part)
        for ex, part in zip(exchanges, parts):
            ex.wait(*part)

    outs = pl.pallas_call(
        body, name=name, in_specs=c_in, out_specs=c_out, out_shape=shapes, scratch_shapes=sems,
        input_output_aliases=aliases, compiler_params=pltpu.CompilerParams(has_side_effects=True))(*srcs)
    return _per_exchange(exchanges, outs)


def _per_exchange(exchanges, flat):
    out, off = [], 0
    for ex in exchanges:
        out.append(flat[off:off + ex.n])
        off += ex.n
    return out


def _call(body, name, grid, in_specs, out_specs, out_shape, args, scratch=(), sem=None, carry=None):
    n_in, n_out, n_scr = len(in_specs), len(out_specs), len(scratch)
    if not carry:
        outs = pl.pallas_call(body, name=name, grid=grid, in_specs=in_specs, out_specs=out_specs,
                              out_shape=out_shape, scratch_shapes=list(scratch),
                              compiler_params=_params(*sem))(*args)
        return outs, None
    c_in, c_out, shapes, sems, srcs, aliases, split = _carried(carry, n_in, n_out)

    def hosted(*refs):
        ins, outs, scr, parts = split(refs, n_scr)
        first, last = True, True
        for d, size in enumerate(grid):
            first = first & (pl.program_id(d) == 0)
            last = last & (pl.program_id(d) == size - 1)

        @pl.when(first)
        def _():
            for ex, part in zip(carry, parts):
                ex.start(*part)

        body(*ins, *outs, *scr)

        @pl.when(last)
        def _():
            for ex, part in zip(carry, parts):
                ex.wait(*part)

    outs = pl.pallas_call(
        hosted, name=name + "_x", grid=grid, in_specs=list(in_specs) + c_in,
        out_specs=list(out_specs) + c_out, out_shape=list(out_shape) + shapes,
        scratch_shapes=list(scratch) + sems, input_output_aliases=aliases,
        compiler_params=_params(*(["arbitrary"] * len(grid))))(*args, *srcs)
    return outs[:n_out], _per_exchange(carry, outs[n_out:])


def _whole(ref, j):
    return ref


def _cols(width):
    return lambda ref, j: ref.at[(slice(None),) * (len(ref.shape) - 1) + (pl.ds(j * width, width),)]


def _rows(height):
    return lambda ref, j: ref.at[pl.ds(j * height, height)]


def _mm_nn(name, a, w, *, tm, tn, out_dtype, relu2=False, carry=None):
    t, k = a.shape
    n = w.shape[1]

    def body(a_ref, w_ref, o_ref):
        acc = _dot(a_ref[...].astype(BF16), w_ref[...])
        if relu2:
            r = jnp.maximum(acc, 0.0)
            acc = r * r
        o_ref[...] = acc.astype(out_dtype)

    outs, recv = _call(
        body, name, (t // tm, n // tn),
        [pl.BlockSpec((tm, k), lambda i, j: (i, 0)), pl.BlockSpec((k, tn), lambda i, j: (0, j))],
        [pl.BlockSpec((tm, tn), lambda i, j: (i, j))], [jax.ShapeDtypeStruct((t, n), out_dtype)], (a, w),
        sem=("parallel", "parallel"), carry=carry)
    return outs[0] if carry is None else (outs[0], recv)


def _in_proj(a, w, *, tm, carry=None):
    t, k = a.shape
    tn = IN_COLS // 4
    f_tile, f_off = C_HF // tn, C_HF % tn

    def body(a_ref, w_ref, o_ref, f_ref):
        acc = _dot(a_ref[...], w_ref[...])
        o_ref[...] = acc.astype(BF16)

        @pl.when(pl.program_id(1) == f_tile)
        def _():
            f_ref[...] = acc[:, f_off:f_off + WIDTH]

    outs, recv = _call(
        body, "in_proj", (t // tm, IN_COLS // tn),
        [pl.BlockSpec((tm, k), lambda i, j: (i, 0)), pl.BlockSpec((k, tn), lambda i, j: (0, j))],
        [pl.BlockSpec((tm, tn), lambda i, j: (i, j)), pl.BlockSpec((tm, WIDTH), lambda i, j: (i, 0))],
        [jax.ShapeDtypeStruct((t, IN_COLS), BF16), jax.ShapeDtypeStruct((t, WIDTH), F32)], (a, w),
        sem=("parallel", "arbitrary"), carry=carry)
    return outs if carry is None else (outs, recv)


def _linear_ln(name, a, w, resid, g, b, *, tm, carry=None):
    t, k = a.shape
    halves = [slice(0, tm // 2), slice(tm // 2, tm)] if k > D_MODEL else [slice(0, tm)]

    def body(a_ref, w_ref, r_ref, g_ref, b_ref, z_ref, x_ref, xb_ref):
        z = ALPHA * _Lanes(r_ref[s, :] for s in halves) + _ldot(_Lanes(a_ref[s, :] for s in halves), w_ref[...])
        zc = z - _mean(z, axis=-1, keepdims=True)
        y = zc * _rsqrt(_mean(zc * zc, axis=-1, keepdims=True) + LN_EPS) * g_ref[...] + b_ref[...]
        for s, zz, yy in zip(halves, z.xs, y.xs):
            z_ref[s, :] = zz
            x_ref[s, :] = yy
            xb_ref[s, :] = yy.astype(BF16)

    row = pl.BlockSpec((tm, D_MODEL), lambda i: (i, 0))
    outs, recv = _call(
        body, name, (t // tm,),
        [pl.BlockSpec((tm, k), lambda i: (i, 0)), _resident((k, D_MODEL)), row,
         _resident((1, D_MODEL)), _resident((1, D_MODEL))],
        [row, row, row],
        [jax.ShapeDtypeStruct((t, D_MODEL), F32)] * 2 + [jax.ShapeDtypeStruct((t, D_MODEL), BF16)],
        (a, w, resid, g, b), sem=("parallel",), carry=carry)
    return outs if carry is None else (outs, recv)


def _ln_bwd_mm_nt(name, dy, z, g, w, h=None, *, tm, tn, carry=None):
    t = dy.shape[0]
    n = w.shape[0]
    halves = [slice(0, tm // 2), slice(tm // 2, tm)]

    def body(*refs):
        if h is None:
            dy_ref, z_ref, g_ref, w_ref, dz_ref, dzb_ref, o_ref, dg_ref, db_ref = refs
        else:
            dy_ref, z_ref, g_ref, w_ref, h_ref, dz_ref, dzb_ref, o_ref, dg_ref, db_ref = refs

        @pl.when(pl.program_id(0) == 0)
        def _():
            dg_ref[...] = jnp.zeros_like(dg_ref)
            db_ref[...] = jnp.zeros_like(db_ref)

        zv = _Lanes(z_ref[s, :] for s in halves)
        dyv = _Lanes(dy_ref[s, :] for s in halves)
        mu = _mean(zv, axis=-1, keepdims=True)
        zc = zv - mu
        rstd = _rsqrt(_mean(zc * zc, axis=-1, keepdims=True) + LN_EPS)
        xh = zc * rstd
        gdy = dyv * g_ref[...]
        m1 = _mean(gdy, axis=-1, keepdims=True)
        m2 = _mean(gdy * xh, axis=-1, keepdims=True)
        dz = rstd * (gdy - m1 - xh * m2)
        dz_b = dz.astype(BF16)
        for s, a, a_b in zip(halves, dz.xs, dz_b.xs):
            dz_ref[s, :] = a
            dzb_ref[s, :] = a_b
        dg_ref[...] += _sum(dyv * xh, axis=0, keepdims=True).total()
        db_ref[...] += _sum(dyv, axis=0, keepdims=True).total()
        for c in range(n // tn):
            cols = slice(c * tn, (c + 1) * tn)
            acc = _ldot_nt(dz_b, w_ref[cols, :])
            if h is not None:
                acc = acc * (2.0 * _sqrt(_Lanes(h_ref[s, cols] for s in halves).astype(F32)))
            for s, a in zip(halves, acc.xs):
                o_ref[s, cols] = a.astype(BF16)

    row = pl.BlockSpec((tm, D_MODEL), lambda i: (i, 0))
    vec = _resident((1, D_MODEL))
    tile = pl.BlockSpec((tm, n), lambda i: (i, 0))
    in_specs = [row, row, vec, _resident((n, D_MODEL))]
    args = [dy, z, g, w]
    if h is not None:
        in_specs.append(tile)
        args.append(h)
    outs, recv = _call(
        body, name, (t // tm,), in_specs, [row, row, tile, vec, vec],
        [jax.ShapeDtypeStruct((t, D_MODEL), F32), jax.ShapeDtypeStruct((t, D_MODEL), BF16),
         jax.ShapeDtypeStruct((t, n), BF16), jax.ShapeDtypeStruct((1, D_MODEL), F32),
         jax.ShapeDtypeStruct((1, D_MODEL), F32)], args, sem=("arbitrary",), carry=carry)
    return outs if carry is None else (outs, recv)


def _mm_tn(name, a, b, *, tk, tmo, tno, carry=None):
    t, m = a.shape
    n = b.shape[1]
    nk = t // tk

    def body(a_ref, b_ref, o_ref, acc_ref):
        k = pl.program_id(2)
        p = _dot_tn(a_ref[...].astype(BF16), b_ref[...].astype(BF16))

        @pl.when(k == 0)
        def _():
            acc_ref[...] = p

        @pl.when(k > 0)
        def _():
            acc_ref[...] += p

        @pl.when(k == nk - 1)
        def _():
            o_ref[...] = acc_ref[...].astype(BF16)

    outs, recv = _call(
        body, name, (m // tmo, n // tno, nk),
        [pl.BlockSpec((tk, tmo), lambda i, j, k: (k, i)), pl.BlockSpec((tk, tno), lambda i, j, k: (k, j))],
        [pl.BlockSpec((tmo, tno), lambda i, j, k: (i, j))], [jax.ShapeDtypeStruct((m, n), BF16)], (a, b),
        scratch=[pltpu.VMEM((tmo, tno), F32)], sem=("parallel", "parallel", "arbitrary"), carry=carry)
    return outs[0] if carry is None else (outs[0], recv)


def _mm_nt_sum(name, pieces, offsets, w, resid, *, tm, carry=None):
    t = resid.shape[0]
    widths = [p.shape[1] for p in pieces]
    n_p = len(pieces)

    def body(*refs):
        p_refs, w_ref, r_ref, o_ref = refs[:n_p], refs[n_p], refs[n_p + 1], refs[n_p + 2]
        acc = ALPHA * r_ref[...]
        for p_ref, off, wd in zip(p_refs, offsets, widths):
            acc = acc + _dot_nt(p_ref[...], w_ref[:, off:off + wd])
        o_ref[...] = acc

    row = pl.BlockSpec((tm, D_MODEL), lambda i: (i, 0))
    outs, recv = _call(
        body, name, (t // tm,),
        [pl.BlockSpec((tm, wd), lambda i: (i, 0)) for wd in widths] + [_resident(w.shape, single=True), row],
        [row], [jax.ShapeDtypeStruct((t, D_MODEL), F32)], (*pieces, w, resid), sem=("parallel",), carry=carry)
    return outs[0] if carry is None else (outs[0], recv)


def _chunk_mask(rows):
    r = lax.broadcasted_iota(jnp.int32, (rows, rows), 0)
    c = lax.broadcasted_iota(jnp.int32, (rows, rows), 1)
    return ((r // CHUNK) == (c // CHUNK)) & (c <= r)


class _Lanes:
    def __init__(self, xs):
        self.xs = list(xs)

    def _with(self, other, f):
        if isinstance(other, _Lanes):
            return _Lanes([f(a, b) for a, b in zip(self.xs, other.xs)])
        return _Lanes([f(a, other) for a in self.xs])

    def __add__(self, o):
        return self._with(o, lambda a, b: a + b)

    def __radd__(self, o):
        return self._with(o, lambda a, b: b + a)

    def __sub__(self, o):
        return self._with(o, lambda a, b: a - b)

    def __rsub__(self, o):
        return self._with(o, lambda a, b: b - a)

    def __mul__(self, o):
        return self._with(o, lambda a, b: a * b)

    def __rmul__(self, o):
        return self._with(o, lambda a, b: b * a)

    def __truediv__(self, o):
        return self._with(o, lambda a, b: a / b)

    def __rtruediv__(self, o):
        return self._with(o, lambda a, b: b / a)

    def __neg__(self):
        return _Lanes([-a for a in self.xs])

    def __ge__(self, o):
        return self._with(o, lambda a, b: a >= b)

    def __getitem__(self, idx):
        return _Lanes([a[idx] for a in self.xs])

    def astype(self, dtype):
        return _Lanes([a.astype(dtype) for a in self.xs])

    def total(self):
        return functools.reduce(lambda a, b: a + b, self.xs)


def _lift(f):
    def g(*args, **kw):
        lanes = [a for a in args if isinstance(a, _Lanes)]
        if not lanes:
            return f(*args, **kw)
        return _Lanes([f(*[a.xs[i] if isinstance(a, _Lanes) else a for a in args], **kw)
                       for i in range(len(lanes[0].xs))])
    return g


def _concat(parts, axis):
    if isinstance(parts[0], _Lanes):
        return _Lanes([jnp.concatenate([p.xs[i] for p in parts], axis=axis) for i in range(len(parts[0].xs))])
    return jnp.concatenate(parts, axis=axis)


_exp, _log, _abs, _sqrt, _where = _lift(jnp.exp), _lift(jnp.log), _lift(jnp.abs), _lift(jnp.sqrt), _lift(jnp.where)
_sum, _mean, _rsqrt, _bcast = _lift(jnp.sum), _lift(jnp.mean), _lift(lax.rsqrt), _lift(jnp.broadcast_to)
_ldot, _ldot_nt, _ldot_tn = _lift(_dot), _lift(_dot_nt), _lift(_dot_tn)
_lsigmoid = _lift(_sigmoid)


def _mask_sum(mask_b, x, transpose=False):
    f = _ldot_tn if transpose else _ldot
    hi = x.astype(BF16)
    lo = (x - hi.astype(F32)).astype(BF16)
    return f(mask_b, hi) + f(mask_b, lo)


def _chunk_row(x, pos, rows):
    nc = rows // CHUNK

    def one(a):
        a3 = a.reshape(nc, CHUNK, HEAD_DIM)
        return jnp.broadcast_to(a3[:, pos:pos + 1, :], (nc, CHUNK, HEAD_DIM)).reshape(rows, HEAD_DIM)

    return _lift(one)(x)


def _chunk_total(x, rows):
    nc = rows // CHUNK

    def one(a):
        tot = jnp.sum(a.reshape(nc, CHUNK, HEAD_DIM), axis=1, keepdims=True)
        return jnp.broadcast_to(tot, (nc, CHUNK, HEAD_DIM)).reshape(rows, HEAD_DIM)

    return _lift(one)(x)


def _sigmoid_pair(x):
    e = _exp(-_abs(x))
    big = 1.0 / (1.0 + e)
    small = e * big
    pos = x >= 0.0
    return _where(pos, big, small), _where(pos, small, big)


def _hg_gates(q_raw, fl, lb, rows, mask):
    tri = mask.astype(BF16)
    sg, sg_neg = _sigmoid_pair(fl)
    forget = lb + (1.0 - lb) * sg
    k = (1.0 - lb) * sg_neg
    sq = _lsigmoid(q_raw)
    qs = q_raw * sq
    bc = _mask_sum(tri, _log(forget))
    bref = _chunk_row(bc, CHUNK // 2 - 1, rows)
    blast = _chunk_row(bc, CHUNK - 1, rows)
    return dict(tri=tri, sg=sg, sg_neg=sg_neg, forget=forget, k=k, sq=sq, qs=qs,
                e_a=_exp(bc - bref), e_b=_exp(bref - bc), e_q=_exp(bc), e_k=_exp(blast - bc),
                dec=_exp(blast))


HG_GROUP = 4


def _hg_lanes(bsz):
    return [(hh, slice(hh * HEAD_DIM, (hh + 1) * HEAD_DIM), b) for hh in range(HG_GROUP) for b in range(bsz)]


def _hg_read(ref, lanes):
    return _Lanes(ref[b, :, cs].astype(F32) for _, cs, b in lanes)


def _hg_write(ref, lanes, val, offset=0):
    for (_, cs, b), a in zip(lanes, val.xs):
        ref[b, :, offset + cs.start:offset + cs.stop] = a


def _hgrn_fwd(proj, hf, lb, nw, *, bsz, seq, carry=None):
    rows = min(ROWS_HG, seq)
    nt = seq // rows
    nc = rows // CHUNK
    t = bsz * seq

    lanes = _hg_lanes(bsz)

    def body(q_ref, f_ref, v_ref, g_ref, lb_ref, nw_ref, y_ref, o_ref, st_ref, s_scr):
        @pl.when(pl.program_id(1) == 0)
        def _():
            s_scr[...] = jnp.zeros_like(s_scr)

        mask = _chunk_mask(rows)
        lb_v = _Lanes(lb_ref[:, cs] for _, cs, _ in lanes)
        gt = _hg_gates(_hg_read(q_ref, lanes), _hg_read(f_ref, lanes), lb_v, rows, mask)
        v_b = _hg_read(v_ref, lanes).astype(BF16)
        a_b = (gt["qs"] * gt["e_a"]).astype(BF16)
        b_b = (gt["k"] * gt["e_b"]).astype(BF16)
        qi_b = (gt["qs"] * gt["e_q"]).astype(BF16)
        ko_b = (gt["k"] * gt["e_k"]).astype(BF16)
        scores = _where(mask, _ldot_nt(a_b, b_b), 0.0)
        o_intra = _ldot(scores.astype(BF16), v_b)

        s = _Lanes(s_scr[i] for i in range(len(lanes)))
        parts = []
        for n in range(nc):
            sl = slice(n * CHUNK, (n + 1) * CHUNK)
            s_b = s.astype(BF16)
            for (hh, _, b), a in zip(lanes, s_b.xs):
                st_ref[hh, b, n] = a
            parts.append(_ldot_nt(qi_b[sl], s_b))
            s = s * gt["dec"][n * CHUNK:n * CHUNK + 1] + _ldot_tn(v_b[sl], ko_b[sl])
        for i, a in enumerate(s.xs):
            s_scr[i] = a
        o = o_intra + _concat(parts, 0)
        _hg_write(o_ref, lanes, o)
        r = _rsqrt(_mean(o * o, axis=-1, keepdims=True) + RMS_EPS)
        g = _hg_read(g_ref, lanes)
        _hg_write(y_ref, lanes, (o * r * nw_ref[...] * (g * _lsigmoid(g))).astype(BF16))

    wide = HG_GROUP * HEAD_DIM

    def col(base):
        return pl.BlockSpec((bsz, rows, wide), lambda h, j: (0, j, base // wide + h))

    out_tile = pl.BlockSpec((bsz, rows, wide), lambda h, j: (0, j, h))
    p3 = proj.reshape(bsz, seq, IN_COLS)
    outs, recv = _call(
        body, "hgrn_fwd", (HEADS // HG_GROUP, nt),
        [col(C_HQ), out_tile, col(C_HI), col(C_HG),
         pl.BlockSpec((1, wide), lambda h, j: (0, h)), _resident((1, HEAD_DIM))],
        [out_tile, out_tile,
         pl.BlockSpec((HG_GROUP, bsz, nc, HEAD_DIM, HEAD_DIM), lambda h, j: (h, 0, j, 0, 0))],
        [jax.ShapeDtypeStruct((bsz, seq, WIDTH), BF16), jax.ShapeDtypeStruct((bsz, seq, WIDTH), F32),
         jax.ShapeDtypeStruct((HEADS, bsz, seq // CHUNK, HEAD_DIM, HEAD_DIM), BF16)],
        (p3, hf.reshape(bsz, seq, WIDTH), p3, p3, lb, nw),
        scratch=[pltpu.VMEM((len(lanes), HEAD_DIM, HEAD_DIM), F32)],
        sem=("parallel", "arbitrary"), carry=carry)
    outs = [outs[0].reshape(t, WIDTH), outs[1].reshape(t, WIDTH), outs[2]]
    return outs if carry is None else (outs, recv)


def _hgrn_bwd(proj, hf, lb, nw, o_pre, states, dy, *, bsz, seq, carry=None):
    rows = min(ROWS_HG, seq)
    nt = seq // rows
    nc = rows // CHUNK
    t = bsz * seq
    lanes = _hg_lanes(bsz)

    def body(q_ref, f_ref, v_ref, g_ref, lb_ref, nw_ref, o_ref, st_ref, dy_ref, dh_ref, dlb_ref, dnw_ref, ds_scr):
        h, j = pl.program_id(0), pl.program_id(1)

        @pl.when(j == 0)
        def _():
            ds_scr[...] = jnp.zeros_like(ds_scr)
            dlb_ref[...] = jnp.zeros_like(dlb_ref)

        @pl.when((h == 0) & (j == 0))
        def _():
            dnw_ref[...] = jnp.zeros_like(dnw_ref)

        mask = _chunk_mask(rows)
        q_raw = _hg_read(q_ref, lanes)
        lb_v = _Lanes(lb_ref[:, cs] for _, cs, _ in lanes)
        gt = _hg_gates(q_raw, _hg_read(f_ref, lanes), lb_v, rows, mask)
        v_b = _hg_read(v_ref, lanes).astype(BF16)
        a_f = gt["qs"] * gt["e_a"]
        b_f = gt["k"] * gt["e_b"]
        qi_f = gt["qs"] * gt["e_q"]
        ko_f = gt["k"] * gt["e_k"]
        a_b, b_b, qi_b, ko_b = a_f.astype(BF16), b_f.astype(BF16), qi_f.astype(BF16), ko_f.astype(BF16)

        o = _hg_read(o_ref, lanes)
        nw_v = nw_ref[...]
        g = _hg_read(g_ref, lanes)
        dyv = _hg_read(dy_ref, lanes)
        r = _rsqrt(_mean(o * o, axis=-1, keepdims=True) + RMS_EPS)
        sgg = _lsigmoid(g)
        d_g = dyv * (o * r * nw_v) * (sgg * (1.0 + g * (1.0 - sgg)))
        d_on = dyv * (g * sgg)
        dnw_ref[...] += _sum(d_on * o * r, axis=0, keepdims=True).total()
        tt = d_on * nw_v
        d_o = r * tt - o * (r * r * r) * _mean(tt * o, axis=-1, keepdims=True)
        do_b = d_o.astype(BF16)

        sc_b = _where(mask, _ldot_nt(a_b, b_b), 0.0).astype(BF16)
        dsc_b = _where(mask, _ldot_nt(do_b, v_b), 0.0).astype(BF16)
        d_v = _ldot_tn(sc_b, do_b)
        d_a = _ldot(dsc_b, b_b)
        d_bm = _ldot_tn(dsc_b, a_b)

        ds = _Lanes(ds_scr[i] for i in range(len(lanes)))
        dqi_parts, dko_parts, dvi_parts, ddec_parts = [None] * nc, [None] * nc, [None] * nc, [None] * nc
        for n in reversed(range(nc)):
            sl = slice(n * CHUNK, (n + 1) * CHUNK)
            dec_n = gt["dec"][n * CHUNK:n * CHUNK + 1]
            ds_b = ds.astype(BF16)
            s_n = _Lanes(st_ref[hh, b, n] for hh, _, b in lanes)
            dqi_parts[n] = _ldot(do_b[sl], s_n)
            dko_parts[n] = _ldot(v_b[sl], ds_b)
            dvi_parts[n] = _ldot_nt(ko_b[sl], ds_b)
            d_dec = _sum(ds * s_n.astype(F32), axis=0, keepdims=True)
            ddec_parts[n] = _bcast(d_dec * dec_n, (CHUNK, HEAD_DIM))
            ds = ds * dec_n + _ldot_tn(do_b[sl], qi_b[sl])
        for i, a in enumerate(ds.xs):
            ds_scr[i] = a
        d_qi = _concat(dqi_parts, 0)
        d_ko = _concat(dko_parts, 0)
        d_v = d_v + _concat(dvi_parts, 0)

        d_qs = d_a * gt["e_a"] + d_qi * gt["e_q"]
        d_k = d_bm * gt["e_b"] + d_ko * gt["e_k"]
        t_a, t_b, t_q, t_k = d_a * a_f, d_bm * b_f, d_qi * qi_f, d_ko * ko_f
        d_bref = _chunk_total(t_b - t_a, rows)
        d_blast = _chunk_total(t_k, rows) + _concat(ddec_parts, 0)
        pos = lax.broadcasted_iota(jnp.int32, (rows, HEAD_DIM), 0) % CHUNK
        d_bc = (t_a - t_b + t_q - t_k + _where(pos == CHUNK // 2 - 1, d_bref, 0.0)
                + _where(pos == CHUNK - 1, d_blast, 0.0))
        d_logf = _mask_sum(gt["tri"], d_bc, transpose=True)

        sg, sg_neg = gt["sg"], gt["sg_neg"]
        inv_f = 1.0 / gt["forget"]
        common = (1.0 - lb_v) * sg * sg_neg
        d_fl = common * (d_logf * inv_f - d_k)
        d_lb = _sum(sg_neg * (d_logf * inv_f - d_k), axis=0, keepdims=True)
        for (_, cs, _), a in zip(lanes, d_lb.xs):
            dlb_ref[:, cs] += a
        sq = gt["sq"]
        _hg_write(dh_ref, lanes, (d_qs * (sq * (1.0 + q_raw * (1.0 - sq)))).astype(BF16), 0)
        _hg_write(dh_ref, lanes, d_fl.astype(BF16), WIDTH)
        _hg_write(dh_ref, lanes, d_v.astype(BF16), 2 * WIDTH)
        _hg_write(dh_ref, lanes, d_g.astype(BF16), 3 * WIDTH)

    assert HG_GROUP == HEADS, "the combined gradient block needs all heads in one grid step"
    wide = HG_GROUP * HEAD_DIM

    def col(base):
        return pl.BlockSpec((bsz, rows, wide), lambda h, j: (0, nt - 1 - j, base // wide + h))

    tile = pl.BlockSpec((bsz, rows, wide), lambda h, j: (0, nt - 1 - j, h))
    head_vec = pl.BlockSpec((1, wide), lambda h, j: (0, h))
    p3 = proj.reshape(bsz, seq, IN_COLS)
    outs, recv = _call(
        body, "hgrn_bwd", (HEADS // HG_GROUP, nt),
        [col(C_HQ), tile, col(C_HI), col(C_HG), head_vec, _resident((1, HEAD_DIM)), tile,
         pl.BlockSpec((HG_GROUP, bsz, nc, HEAD_DIM, HEAD_DIM), lambda h, j: (h, 0, nt - 1 - j, 0, 0)), tile],
        [pl.BlockSpec((bsz, rows, 4 * WIDTH), lambda h, j: (0, nt - 1 - j, 0)), head_vec, _resident((1, HEAD_DIM))],
        [jax.ShapeDtypeStruct((bsz, seq, 4 * WIDTH), BF16), jax.ShapeDtypeStruct((1, WIDTH), F32),
         jax.ShapeDtypeStruct((1, HEAD_DIM), F32)],
        (p3, hf.reshape(bsz, seq, WIDTH), p3, p3, lb, nw, o_pre.reshape(bsz, seq, WIDTH), states,
         dy.reshape(bsz, seq, WIDTH)),
        scratch=[pltpu.VMEM((len(lanes), HEAD_DIM, HEAD_DIM), F32)],
        sem=("arbitrary", "arbitrary"), carry=carry)
    outs = [outs[0].reshape(t, 4 * WIDTH), outs[1], outs[2]]
    return outs if carry is None else (outs, recv)


def _mem_kv(mem2d, w_k, w_v):
    rows = mem2d.shape[0]

    def body(m_ref, wk_ref, wv_ref, k_ref, v_ref):
        m_b = m_ref[...].astype(BF16)
        k_ref[...] = _dot(m_b, wk_ref[...]).astype(BF16)
        v_ref[...] = _dot(m_b, wv_ref[...]).astype(BF16)

    return pl.pallas_call(
        body, name="mem_kv", grid=(rows // MEM_LEN,),
        in_specs=[pl.BlockSpec((MEM_LEN, D_MODEL), lambda i: (i, 0)), _resident((D_MODEL, WIDTH)),
                  _resident((D_MODEL, WIDTH))],
        out_specs=[pl.BlockSpec((MEM_LEN, WIDTH), lambda i: (i, 0))] * 2,
        out_shape=[jax.ShapeDtypeStruct((rows, WIDTH), BF16)] * 2,
        compiler_params=_params("parallel"),
    )(mem2d, w_k, w_v)


def _softmax_rows(s):
    m = _lift(jnp.max)(s, axis=-1, keepdims=True)
    e = _exp(s - m)
    return e / _sum(e, axis=-1, keepdims=True)


def _attn_fwd(proj, mk, mv, *, tm, seq):
    t = proj.shape[0]
    per_b = seq // tm
    scale = HEAD_DIM ** -0.5

    def body(q_ref, k_ref, v_ref, y_ref):
        heads = [slice(h * HEAD_DIM, (h + 1) * HEAD_DIM) for h in range(HEADS)]
        q_b = _Lanes(q_ref[:, sl] for sl in heads).astype(BF16)
        p = _softmax_rows(_ldot_nt(q_b, _Lanes(k_ref[:, sl] for sl in heads)) * scale)
        out = _ldot(p.astype(BF16), _Lanes(v_ref[:, sl] for sl in heads))
        y_ref[...] = jnp.concatenate(out.xs, axis=-1).astype(BF16)

    kv = pl.BlockSpec((MEM_LEN, WIDTH), lambda i: (i // per_b, 0))
    return pl.pallas_call(
        body, name="attn_fwd", grid=(t // tm,),
        in_specs=[pl.BlockSpec((tm, WIDTH), lambda i: (i, C_MQ // WIDTH)), kv, kv],
        out_specs=pl.BlockSpec((tm, WIDTH), lambda i: (i, 0)),
        out_shape=jax.ShapeDtypeStruct((t, WIDTH), BF16),
        compiler_params=_params("parallel"),
    )(proj, mk, mv)


def _attn_bwd(proj, mk, mv, dy, x_b, *, tm, seq):
    t = proj.shape[0]
    per_b = seq // tm
    scale = HEAD_DIM ** -0.5
    last = t // tm - 1

    def body(q_ref, k_ref, v_ref, dy_ref, x_ref, dq_ref, dk_ref, dv_ref, gw_ref, acc_ref):
        i = pl.program_id(0)

        @pl.when(i % per_b == 0)
        def _():
            dk_ref[...] = jnp.zeros_like(dk_ref)
            dv_ref[...] = jnp.zeros_like(dv_ref)

        @pl.when(i == 0)
        def _():
            acc_ref[...] = jnp.zeros_like(acc_ref)

        heads = [slice(h * HEAD_DIM, (h + 1) * HEAD_DIM) for h in range(HEADS)]
        q_b = _Lanes(q_ref[:, sl] for sl in heads).astype(BF16)
        k_b, v_b = _Lanes(k_ref[:, sl] for sl in heads), _Lanes(v_ref[:, sl] for sl in heads)
        p = _softmax_rows(_ldot_nt(q_b, k_b) * scale)
        dy_b = _Lanes(dy_ref[:, sl] for sl in heads).astype(BF16)
        dp = _ldot_nt(dy_b, v_b)
        d_v = _ldot_tn(p.astype(BF16), dy_b)
        ds_b = (p * (dp - _sum(dp * p, axis=-1, keepdims=True)) * scale).astype(BF16)
        dq_b = jnp.concatenate(_ldot(ds_b, k_b).xs, axis=-1).astype(BF16)
        dq_ref[...] = dq_b
        dk_ref[...] += jnp.concatenate(_ldot_tn(ds_b, q_b).xs, axis=-1)
        dv_ref[...] += jnp.concatenate(d_v.xs, axis=-1)
        acc_ref[...] += _dot_tn(x_ref[...], dq_b)

        @pl.when(i == last)
        def _():
            gw_ref[...] = acc_ref[...].astype(BF16)

    kv = pl.BlockSpec((MEM_LEN, WIDTH), lambda i: (i // per_b, 0))
    tile = pl.BlockSpec((tm, WIDTH), lambda i: (i, 0))
    n_mem = mk.shape[0]
    return pl.pallas_call(
        body, name="attn_bwd", grid=(t // tm,),
        in_specs=[pl.BlockSpec((tm, WIDTH), lambda i: (i, C_MQ // WIDTH)), kv, kv, tile,
                  pl.BlockSpec((tm, D_MODEL), lambda i: (i, 0))],
        out_specs=[tile, kv, kv, _resident((D_MODEL, WIDTH))],
        out_shape=[jax.ShapeDtypeStruct((t, WIDTH), BF16), jax.ShapeDtypeStruct((n_mem, WIDTH), F32),
                   jax.ShapeDtypeStruct((n_mem, WIDTH), F32), jax.ShapeDtypeStruct((D_MODEL, WIDTH), BF16)],
        scratch_shapes=[pltpu.VMEM((D_MODEL, WIDTH), F32)],
        compiler_params=_params("arbitrary"),
    )(proj, mk, mv, dy, x_b)


HALO = 16


def _shift_down(u, halo, k, row):
    out = pltpu.roll(u, k, 0)
    for m in range(k):
        out = jnp.where(row == m, halo[HALO - k + m:HALO - k + m + 1, :], out)
    return out


def _shift_up(u, halo, k, row, tm):
    out = pltpu.roll(u, tm - k, 0)
    for m in range(k):
        out = jnp.where(row == tm - k + m, halo[m:m + 1, :], out)
    return out


def _merge_fwd(proj, y_b, y_c, conv_w, w_branch, b_gate, *, tm, seq, carry=None):
    t = proj.shape[0]
    per_b = seq // tm
    hb = tm // HALO

    def body(cb_ref, cc_ref, ch_ref, cch_ref, chh_ref, ga_ref, gb_ref, gc_ref, yb_ref, yc_ref, cw_ref, wb_ref,
             bg_ref, ya_ref, pa_ref, pb_ref, pc_ref, mg_ref, sa_ref, sb_ref, sc_ref):
        i = pl.program_id(0)
        row = lax.broadcasted_iota(jnp.int32, (tm, WIDTH), 0)
        u = cc_ref[...].astype(F32) * ch_ref[...].astype(F32)
        halo = jnp.where(i % per_b == 0, 0.0, cch_ref[...].astype(F32) * chh_ref[...].astype(F32))
        cw = cw_ref[...]
        y = cw[0:1] * _shift_down(u, halo, 2, row) + cw[1:2] * _shift_down(u, halo, 1, row) + cw[2:3] * u
        ya_b = (cb_ref[...].astype(F32) * y).astype(BF16)
        ya_ref[...] = ya_b
        merged = None
        for idx, (y_in, g_ref, p_ref, s_ref) in enumerate(((ya_b, ga_ref, pa_ref, sa_ref),
                                                            (yb_ref[...], gb_ref, pb_ref, sb_ref),
                                                            (yc_ref[...], gc_ref, pc_ref, sc_ref))):
            p = _dot(y_in, wb_ref[idx])
            p_ref[...] = p.astype(BF16)
            sg = _sigmoid(g_ref[...].astype(F32) + bg_ref[:, idx * D_MODEL:(idx + 1) * D_MODEL])
            s_ref[...] = sg.astype(BF16)
            term = sg * p
            merged = term if merged is None else merged + term
        mg_ref[...] = merged.astype(BF16)

    def half(c):
        return pl.BlockSpec((tm, WIDTH), lambda i: (i, c // WIDTH))

    def prev(c):
        return pl.BlockSpec((HALO, WIDTH), lambda i: (jnp.maximum(i * hb - 1, 0), c // WIDTH))

    def gate(k):
        return pl.BlockSpec((tm, D_MODEL), lambda i: (i, C_GA // D_MODEL + k))

    tile512 = pl.BlockSpec((tm, WIDTH), lambda i: (i, 0))
    tile1k = pl.BlockSpec((tm, D_MODEL), lambda i: (i, 0))
    outs, recv = _call(
        body, "merge_fwd", (t // tm,),
        [half(C_CB), half(C_CC), half(C_CH), prev(C_CC), prev(C_CH), gate(0), gate(1), gate(2),
         tile512, tile512, _resident((CONV_K, WIDTH)), _resident((3, WIDTH, D_MODEL)), _resident((1, 3 * D_MODEL))],
        [tile512] + [tile1k] * 7,
        [jax.ShapeDtypeStruct((t, WIDTH), BF16)] + [jax.ShapeDtypeStruct((t, D_MODEL), BF16)] * 7,
        (proj, proj, proj, proj, proj, proj, proj, proj, y_b, y_c, conv_w, w_branch, b_gate),
        sem=("parallel",), carry=carry)
    return outs if carry is None else (outs, recv)


def _merge_bwd(dmerged, projections, gates, branch_in, w_branch, *, tm):
    t = dmerged.shape[0]
    last = t // tm - 1

    def body(dm_ref, pa_ref, pb_ref, pc_ref, sa_ref, sb_ref, sc_ref, ya_ref, yb_ref, yc_ref, wb_ref,
             dgt_ref, dya_ref, dyb_ref, dyc_ref, dbg_ref, gw_ref, acc_ref):
        i = pl.program_id(0)

        @pl.when(i == 0)
        def _():
            dbg_ref[...] = jnp.zeros_like(dbg_ref)
            acc_ref[...] = jnp.zeros_like(acc_ref)

        dm = dm_ref[...].astype(F32)
        for idx, (p_ref, s_ref, y_ref, dy_ref) in enumerate(((pa_ref, sa_ref, ya_ref, dya_ref),
                                                             (pb_ref, sb_ref, yb_ref, dyb_ref),
                                                             (pc_ref, sc_ref, yc_ref, dyc_ref))):
            cols = slice(idx * D_MODEL, (idx + 1) * D_MODEL)
            sg = s_ref[...].astype(F32)
            dp = dm * sg
            dp_b = dp.astype(BF16)
            dgate = dp * p_ref[...].astype(F32) * (1.0 - sg)
            dgt_ref[:, cols] = dgate.astype(BF16)
            dbg_ref[:, cols] += jnp.sum(dgate, axis=0, keepdims=True)
            dy_ref[...] = _dot_nt(dp_b, wb_ref[idx]).astype(BF16)
            acc_ref[idx] += _dot_tn(y_ref[...], dp_b)

        @pl.when(i == last)
        def _():
            gw_ref[...] = acc_ref[...].astype(BF16)

    tile512 = pl.BlockSpec((tm, WIDTH), lambda i: (i, 0))
    tile1k = pl.BlockSpec((tm, D_MODEL), lambda i: (i, 0))
    return pl.pallas_call(
        body, name="merge_bwd", grid=(t // tm,),
        in_specs=[tile1k] * 7 + [tile512] * 3 + [_resident((3, WIDTH, D_MODEL))],
        out_specs=[pl.BlockSpec((tm, 3 * D_MODEL), lambda i: (i, 0)), tile512, tile512, tile512,
                   _resident((1, 3 * D_MODEL)), _resident((3, WIDTH, D_MODEL))],
        out_shape=[jax.ShapeDtypeStruct((t, 3 * D_MODEL), BF16)] + [jax.ShapeDtypeStruct((t, WIDTH), BF16)] * 3
                  + [jax.ShapeDtypeStruct((1, 3 * D_MODEL), F32), jax.ShapeDtypeStruct((3, WIDTH, D_MODEL), BF16)],
        scratch_shapes=[pltpu.VMEM((3, WIDTH, D_MODEL), F32)],
        compiler_params=_params("arbitrary"),
    )(dmerged, *projections, *gates, *branch_in, w_branch)


def _conv_bwd(proj, dya, conv_w, x_b, *, tm, seq):
    t = proj.shape[0]
    per_b = seq // tm
    hb = tm // HALO
    last_blk = t // HALO - 1
    last = t // tm - 1

    def body(cb_ref, cc_ref, ch_ref, cch_ref, chh_ref, dya_ref, cbn_ref, dyan_ref, cw_ref, x_ref,
             d_ref, dcw_ref, gw_ref, acc_ref):
        i = pl.program_id(0)

        @pl.when(i == 0)
        def _():
            dcw_ref[...] = jnp.zeros_like(dcw_ref)
            acc_ref[...] = jnp.zeros_like(acc_ref)

        row = lax.broadcasted_iota(jnp.int32, (tm, WIDTH), 0)
        cb, cc, ch = cb_ref[...].astype(F32), cc_ref[...].astype(F32), ch_ref[...].astype(F32)
        u = cc * ch
        halo = jnp.where(i % per_b == 0, 0.0, cch_ref[...].astype(F32) * chh_ref[...].astype(F32))
        u1 = _shift_down(u, halo, 1, row)
        u2 = _shift_down(u, halo, 2, row)
        cw = cw_ref[...]
        y = cw[0:1] * u2 + cw[1:2] * u1 + cw[2:3] * u
        dya = dya_ref[...].astype(F32)
        dy = dya * cb
        nxt = jnp.where(i % per_b == per_b - 1, 0.0, dyan_ref[...].astype(F32) * cbn_ref[...].astype(F32))
        du = cw[2:3] * dy + cw[1:2] * _shift_up(dy, nxt, 1, row, tm) + cw[0:1] * _shift_up(dy, nxt, 2, row, tm)
        d_ref[:, 0:WIDTH] = (dya * y).astype(BF16)
        d_ref[:, WIDTH:2 * WIDTH] = (du * ch).astype(BF16)
        d_ref[:, 2 * WIDTH:3 * WIDTH] = (du * cc).astype(BF16)
        dcw_ref[0:1, :] += jnp.sum(dy * u2, axis=0, keepdims=True)
        dcw_ref[1:2, :] += jnp.sum(dy * u1, axis=0, keepdims=True)
        dcw_ref[2:3, :] += jnp.sum(dy * u, axis=0, keepdims=True)
        acc_ref[...] += _dot_tn(x_ref[...], d_ref[...])

        @pl.when(i == last)
        def _():
            gw_ref[...] = acc_ref[...].astype(BF16)

    def half(c):
        return pl.BlockSpec((tm, WIDTH), lambda i: (i, c // WIDTH))

    def prev(c):
        return pl.BlockSpec((HALO, WIDTH), lambda i: (jnp.maximum(i * hb - 1, 0), c // WIDTH))

    def nxt(c):
        return pl.BlockSpec((HALO, WIDTH), lambda i: (jnp.minimum((i + 1) * hb, last_blk), c // WIDTH))

    return pl.pallas_call(
        body, name="conv_bwd", grid=(t // tm,),
        in_specs=[half(C_CB), half(C_CC), half(C_CH), prev(C_CC), prev(C_CH),
                  pl.BlockSpec((tm, WIDTH), lambda i: (i, 0)), nxt(C_CB), nxt(0), _resident((CONV_K, WIDTH)),
                  pl.BlockSpec((tm, D_MODEL), lambda i: (i, 0))],
        out_specs=[pl.BlockSpec((tm, 3 * WIDTH), lambda i: (i, 0)), _resident((CONV_K, WIDTH)),
                   _resident((D_MODEL, 3 * WIDTH))],
        out_shape=[jax.ShapeDtypeStruct((t, 3 * WIDTH), BF16), jax.ShapeDtypeStruct((CONV_K, WIDTH), F32),
                   jax.ShapeDtypeStruct((D_MODEL, 3 * WIDTH), BF16)],
        scratch_shapes=[pltpu.VMEM((D_MODEL, 3 * WIDTH), F32)],
        compiler_params=_params("arbitrary"),
    )(proj, proj, proj, proj, proj, dya, proj, dya, conv_w, x_b)


def _loss_head(y, target, *, tm):
    t = y.shape[0]

    def body(y_ref, t_ref, dy_ref, l_ref):
        @pl.when(pl.program_id(0) == 0)
        def _():
            l_ref[...] = jnp.zeros_like(l_ref)

        err = y_ref[...] - t_ref[...]
        dy_ref[...] = err * (1.0 / D_MODEL)
        per_row = jnp.sum(err * err, axis=-1, keepdims=True) * (1.0 / D_MODEL)
        l_ref[...] += 0.5 * jnp.sum(per_row, axis=0, keepdims=True)

    row = pl.BlockSpec((tm, D_MODEL), lambda i: (i, 0))
    return pl.pallas_call(
        body, name="loss_head", grid=(t // tm,),
        in_specs=[row, row], out_specs=[row, _resident((8, 128))],
        out_shape=[jax.ShapeDtypeStruct((t, D_MODEL), F32), jax.ShapeDtypeStruct((8, 128), F32)],
        compiler_params=_params("arbitrary"),
    )(y, target)


def _lb_softmax(lower_bounds):
    x = lower_bounds
    e = jnp.exp(x - jnp.max(x, axis=0, keepdims=True))
    return e / jnp.sum(e, axis=0, keepdims=True)


def _lb_fwd(lower_bounds):
    def body(x_ref, o_ref):
        s = _lb_softmax(x_ref[...])
        c = s[0:1]
        o_ref[0:1, :] = c - s[0:1]
        for l in range(1, DEPTH):
            c = c + s[l:l + 1]
            o_ref[l:l + 1, :] = c - s[0:1]

    return pl.pallas_call(body, name="lb_fwd", out_shape=jax.ShapeDtypeStruct(lower_bounds.shape, F32))(lower_bounds)


def _lb_bwd(lower_bounds, d_lb_all):
    def body(x_ref, d_ref, o_ref):
        s = _lb_softmax(x_ref[...])
        d = d_ref[...]
        rows = [jnp.zeros_like(d[0:1])]
        for j in range(1, DEPTH):
            acc = d[j:j + 1]
            for l in range(j + 1, DEPTH):
                acc = acc + d[l:l + 1]
            rows.append(acc)
        inner = rows[0] * s[0:1]
        for j in range(1, DEPTH):
            inner = inner + rows[j] * s[j:j + 1]
        for j in range(DEPTH):
            o_ref[j:j + 1, :] = s[j:j + 1] * (rows[j] - inner)

    return pl.pallas_call(body, name="lb_bwd", out_shape=jax.ShapeDtypeStruct(lower_bounds.shape, F32))(
        lower_bounds, d_lb_all)


def _adamw(w, g, m, v):
    m2 = ADAM_B1 * m + (1.0 - ADAM_B1) * g
    v2 = ADAM_B2 * v + (1.0 - ADAM_B2) * (g * g)
    m_hat = m2 / (1.0 - ADAM_B1 ** ADAM_STEP)
    v_hat = v2 / (1.0 - ADAM_B2 ** ADAM_STEP)
    delta = -ADAM_LR * (m_hat / (jnp.sqrt(v_hat) + ADAM_EPS) + ADAM_WD * w)
    return delta, m2, v2


def _adam_small(name, g, w, m, v):
    shape = w.shape
    flat = (-1, shape[-1])
    g2, w2, m2, v2 = (a.reshape(flat) for a in (g, w, m, v))

    def body(g_ref, w_ref, m_ref, v_ref, d_ref, mo_ref, vo_ref):
        d, mm, vv = _adamw(w_ref[...], g_ref[...], m_ref[...], v_ref[...])
        d_ref[...] = d
        mo_ref[...] = mm
        vo_ref[...] = vv

    outs = pl.pallas_call(body, name=name, out_shape=[jax.ShapeDtypeStruct(w2.shape, F32)] * 3)(g2, w2, m2, v2)
    return [o.reshape(shape) for o in outs]


def _adam_shard(name, recvs, w, m, v, *, tr):
    _, r, c = w.shape

    def body(*refs):
        rc, (w_ref, m_ref, v_ref), (g_ref, d_ref, mo_ref, vo_ref) = refs[:DEPTH], refs[DEPTH:DEPTH + 3], refs[DEPTH + 3:]
        layer = pl.program_id(0)
        for cand in range(DEPTH):
            @pl.when(layer == cand)
            def _():
                g = rc[cand][0].astype(F32)
                for d in range(1, N_DEV):
                    g = g + rc[cand][d].astype(F32)
                dl, mm, vv = _adamw(w_ref[...], g, m_ref[...], v_ref[...])
                g_ref[...] = g
                d_ref[...] = dl
                mo_ref[...] = mm
                vo_ref[...] = vv

    def recv_spec(cand):
        return pl.BlockSpec((N_DEV, tr, c), lambda l, i: (0, jnp.where(l == cand, i, 0), 0))

    tile = pl.BlockSpec((None, tr, c), lambda l, i: (l, i, 0))
    return pl.pallas_call(
        body, name=name, grid=(DEPTH, r // tr),
        in_specs=[recv_spec(cand) for cand in range(DEPTH)] + [tile] * 3,
        out_specs=[tile] * 4,
        out_shape=[jax.ShapeDtypeStruct(w.shape, F32)] * 4,
        compiler_params=_params("parallel", "parallel"),
    )(*recvs, w, m, v)


def _sum_devices(name, x):
    def body(x_ref, o_ref):
        acc = x_ref[0]
        for d in range(1, N_DEV):
            acc = acc + x_ref[d]
        o_ref[...] = acc

    return pl.pallas_call(body, name=name, out_shape=jax.ShapeDtypeStruct(x.shape[1:], x.dtype))(x)


SMALL = (("lower_bounds", 1, 512), ("conv_w", CONV_K, WIDTH), ("hg_norm_w", 1, HEAD_DIM), ("b_gate", 3, D_MODEL),
         ("ln1_g", 1, D_MODEL), ("ln1_b", 1, D_MODEL), ("ln2_g", 1, D_MODEL), ("ln2_b", 1, D_MODEL))
SMALL_ROWS = sum(r for _, r, _ in SMALL)


def _pack_small(per_layer):
    flat = [a for layer in per_layer for a in layer]

    def body(*refs):
        ins, o_ref = refs[:-1], refs[-1]
        o_ref[...] = jnp.zeros_like(o_ref)
        it = iter(ins)
        for l in range(DEPTH):
            row = l * SMALL_ROWS
            for name, nrows, ncols in SMALL:
                ref = next(it)
                if name == "b_gate":
                    for k in range(nrows):
                        o_ref[row + k:row + k + 1, :] = ref[:, k * ncols:(k + 1) * ncols]
                else:
                    o_ref[row:row + nrows, 0:ncols] = ref[...]
                row += nrows

    return pl.pallas_call(body, name="pack_small_grads",
                          out_shape=jax.ShapeDtypeStruct((DEPTH * SMALL_ROWS, D_MODEL), F32))(*flat)


def _unpack_small(summed):
    s3 = summed.reshape(DEPTH, SMALL_ROWS, D_MODEL)
    out, row = {}, 0
    for name, nrows, ncols in SMALL:
        out[name] = s3[:, row:row + nrows, :ncols].reshape(DEPTH, nrows * ncols)
        row += nrows
    return out


def _natural_cols(g):
    nd = g.ndim
    perm = tuple(range(1, nd - 1)) + (0, nd - 1)
    t = jnp.transpose(g, perm)
    return t.reshape(t.shape[:-2] + (t.shape[-2] * t.shape[-1],))


def _natural_rows(g):
    return g.reshape(g.shape[0] * g.shape[1], g.shape[2])


def _hosted(hosts, key, fn):
    pairs = hosts.get(key) if hosts else None
    if callable(pairs):
        pairs = pairs()
    if not pairs:
        return fn(None)
    outs, recvs = fn([ex for ex, _ in pairs])
    for (_, hook), recv in zip(pairs, recvs):
        hook(recv)
    return outs


def _layer_fwd(cur, cur_b, mem2d, wl, *, bsz, seq, hosts=None):
    tm = min(512, seq)
    proj, hf = _hosted(hosts, "in_proj", lambda c: _in_proj(cur_b, wl["w_in"], tm=min(1024, seq), carry=c))
    y_b, o_pre, states = _hosted(hosts, "hgrn_fwd", lambda c: _hgrn_fwd(proj, hf, wl["lb"], wl["nw"], bsz=bsz,
                                                                         seq=seq, carry=c))
    mk, mv = _mem_kv(mem2d, wl["w_mk"], wl["w_mv"])
    y_c = _attn_fwd(proj, mk, mv, tm=tm, seq=seq)
    y_a, pa, pb, pc, merged, sga, sgb, sgc = _hosted(
        hosts, "merge_fwd", lambda c: _merge_fwd(proj, y_b, y_c, wl["conv"], wl["w_br"], wl["b_gate"], tm=tm,
                                                 seq=seq, carry=c))
    z1, x1, x1_b = _hosted(hosts, "wo_ln", lambda c: _linear_ln("wo_ln", merged, wl["w_o"], cur, wl["ln1_g"],
                                                                  wl["ln1_b"], tm=tm, carry=c))
    hid = _hosted(hosts, "mlp_up", lambda c: _mm_nn("mlp_up", x1_b, wl["w_up"], tm=min(1024, seq), tn=2048,
                                                     out_dtype=BF16, relu2=True, carry=c))
    z2, x2, x2_b = _linear_ln("down_ln", hid, wl["w_down"], x1, wl["ln2_g"], wl["ln2_b"], tm=tm)
    return dict(x_b=cur_b, proj=proj, hf=hf, y_a=y_a, y_b=y_b, y_c=y_c, o_pre=o_pre, states=states, mk=mk, mv=mv,
                proj3=(pa, pb, pc), gates3=(sga, sgb, sgc), merged=merged, z1=z1, x1_b=x1_b, hid=hid, z2=z2, x2=x2,
                x2_b=x2_b)


def _layer_bwd(dcur, mem2d, s, wl, *, bsz, seq, plan=None):
    tm = min(512, seq)
    tk = min(2048, bsz * seq)
    g = {}

    def run(key, fn):
        made = plan[key](g) if plan and key in plan else None
        return _hosted({key: [made]} if made else None, key, fn)

    dz2, dz2_b, dhpre, d_ln2g, d_ln2b = run(
        "ln2_bwd_down", lambda c: _ln_bwd_mm_nt("ln2_bwd_down", dcur, s["z2"], wl["ln2_g"], wl["w_down"],
                                                s["hid"], tm=tm, tn=1024, carry=c))
    g["w_down"] = _mm_tn("grad_w_down", s["hid"], dz2_b, tk=tk, tmo=1024, tno=1024)
    dx1 = _mm_nt_sum("mlp_up_bwd", [dhpre], [0], wl["w_up"], dz2, tm=tm)
    g["w_up"] = run("grad_w_up", lambda c: _mm_tn("grad_w_up", dhpre, s["x1_b"], tk=tk, tmo=1024, tno=1024,
                                                   carry=c)).T
    dz1, dz1_b, dmerged, d_ln1g, d_ln1b = _ln_bwd_mm_nt("ln1_bwd_wo", dx1, s["z1"], wl["ln1_g"], wl["w_o"],
                                                        tm=tm, tn=1024)
    g["w_o"] = _mm_tn("grad_w_o", s["merged"], dz1_b, tk=tk, tmo=1024, tno=1024)
    dgate, dya, dyb, dyc, d_bg, g["w_br"] = _merge_bwd(dmerged, s["proj3"], s["gates3"],
                                                       (s["y_a"], s["y_b"], s["y_c"]), wl["w_br"], tm=tm)
    d_conv, d_cw, gw_conv = _conv_bwd(s["proj"], dya, wl["conv"], s["x_b"], tm=tm, seq=seq)
    dhg, d_lb, d_nw = run(
        "hgrn_bwd", lambda c: _hgrn_bwd(s["proj"], s["hf"], wl["lb"], wl["nw"], s["o_pre"], s["states"], dyb,
                                        bsz=bsz, seq=seq, carry=c))
    dmq, dmk, dmv, gw_mq = _attn_bwd(s["proj"], s["mk"], s["mv"], dyc, s["x_b"], tm=tm, seq=seq)
    tkm = min(512, mem2d.shape[0])
    g["w_mk"] = _mm_tn("grad_w_mem", mem2d, dmk, tk=tkm, tmo=1024, tno=512)
    g["w_mv"] = _mm_tn("grad_w_mem", mem2d, dmv, tk=tkm, tmo=1024, tno=512)
    pieces = [d_conv, dhg, dmq, dgate]
    offsets = [C_CB, C_HQ, C_MQ, C_GA]
    g["w_in"] = jnp.concatenate(
        [gw_conv, _mm_tn("grad_w_in_hgrn", s["x_b"], dhg, tk=tk, tmo=1024, tno=2048), gw_mq,
         _mm_tn("grad_w_in_gates", s["x_b"], dgate, tk=tk, tmo=1024, tno=1536)], axis=1)
    dx = run("in_proj_bwd", lambda c: _mm_nt_sum("in_proj_bwd", pieces, offsets, wl["w_in"], dz1,
                                                 tm=min(512, seq), carry=c))
    return dx, g, [d_lb, d_cw, d_nw, d_bg, d_ln1g, d_ln1b, d_ln2g, d_ln2b]


def kernel(x, mem, lower_bounds, w_in, conv_w, hg_norm_w, w_mem_k, w_mem_v, w_branch, b_gate, w_o, ln1_g, ln1_b, w_up, w_down, ln2_g, ln2_b, loss_target, m_lower_bounds, m_w_in, m_conv_w, m_hg_norm_w, m_w_mem_k, m_w_mem_v, m_w_branch, m_b_gate, m_w_o, m_ln1_g, m_ln1_b, m_w_up, m_w_down, m_ln2_g, m_ln2_b, v_lower_bounds, v_w_in, v_conv_w, v_hg_norm_w, v_w_mem_k, v_w_mem_v, v_w_branch, v_b_gate, v_w_o, v_ln1_g, v_ln1_b, v_w_up, v_w_down, v_ln2_g, v_ln2_b):
    bsz, seq, _ = x.shape
    t = bsz * seq
    me = _my_id()

    sh = dict(w_in=w_in.astype(BF16), w_mk=w_mem_k.astype(BF16), w_mv=w_mem_v.astype(BF16),
              w_br=w_branch.astype(BF16), w_o=w_o.astype(BF16), w_up=w_up.astype(BF16), w_down=w_down.astype(BF16))
    half_rows = D_MODEL // 2
    sh["w_in_a"], sh["w_in_b"] = sh["w_in"][:, :half_rows], sh["w_in"][:, half_rows:]
    natural = dict(w_in=_natural_cols, w_in_a=_natural_cols, w_in_b=_natural_cols, w_mk=_natural_rows,
                   w_mv=_natural_rows, w_br=_natural_cols, w_o=_natural_rows, w_up=_natural_cols,
                   w_down=_natural_rows)

    lb_all = _lb_fwd(lower_bounds)
    layer_w = [dict(lb=lb_all[l][None], nw=hg_norm_w[l][None], b_gate=b_gate[l][None], ln1_g=ln1_g[l][None],
                    ln1_b=ln1_b[l][None], ln2_g=ln2_g[l][None], ln2_b=ln2_b[l][None]) for l in range(DEPTH)]
    half_full = {}

    def near(names, l):
        srcs = [sh[n][l] for n in names]
        ex = _Exchange(srcs, piece_shapes=[s_.shape for s_ in srcs], route="near")
        return ex, lambda recv_: half_full.update({(n, l): r for n, r in zip(names, recv_)})

    def relay(names, l):
        ex = _Exchange([half_full.pop((n, l)) for n in names], route="relay")

        def hook(recv_):
            for n, r in zip(names, recv_):
                layer_w[l][n] = natural[n](r)
        return ex, hook

    small4 = ["w_mk", "w_mv", "w_br", "w_o"]
    conv_shard = conv_w.reshape(DEPTH * CONV_K * (WIDTH // N_DEV) // 128, 128)
    first = near(["w_in"], 0)
    conv_ex = _Exchange([conv_shard], piece_shapes=[conv_shard.shape])
    got = _exchange("gather_first", [first[0], conv_ex])
    first[1](got[0])
    conv_full = _natural_cols(got[1][0].reshape(N_DEV, DEPTH, CONV_K, WIDTH // N_DEV))
    second = relay(["w_in"], 0)
    second[1](_exchange("relay_first", [second[0]])[0])

    x2d = x.reshape(t, D_MODEL)
    mem2d = mem.reshape(bsz * MEM_LEN, D_MODEL)
    target2d = loss_target.reshape(t, D_MODEL)

    saved = []
    cur, cur_b = x2d, x2d.astype(BF16)
    for l in range(DEPTH):
        wl = layer_w[l]
        wl["conv"] = conv_full[l]
        more = l + 1 < DEPTH
        now = ["w_up", "w_down"] + ([] if l else small4)
        hosts = {"in_proj": [near(now, l)],
                 "hgrn_fwd": lambda l=l, more=more, now=now: [relay(now, l)] + ([near(["w_in"], l + 1)] if more else [])}
        if more:
            hosts["merge_fwd"] = lambda l=l: [relay(["w_in"], l + 1), near(small4, l + 1)]
            hosts["mlp_up"] = lambda l=l: [relay(small4, l + 1)]
        s = _layer_fwd(cur, cur_b, mem2d, wl, bsz=bsz, seq=seq, hosts=hosts)
        saved.append(s)
        cur, cur_b = s["x2"], s["x2_b"]

    dcur, loss_tile = _loss_head(cur, target2d, tm=min(512, seq))
    loss = lax.psum(loss_tile[0, 0], ("x", "y", "c"))

    in_w = IN_COLS // N_DEV

    def in_half(r):
        return lambda ref, j: ref.at[pl.ds(r * half_rows, half_rows), pl.ds(j * in_w, in_w)]

    slicer = dict(w_in_a=in_half(0), w_in_b=in_half(1), w_mk=_rows(D_MODEL // N_DEV), w_mv=_rows(D_MODEL // N_DEV),
                  w_br=_cols(D_MODEL // N_DEV), w_o=_rows(D_MODEL // N_DEV), w_up=_cols(D_FF // N_DEV),
                  w_down=_rows(D_FF // N_DEV))
    source = dict(w_in_a="w_in", w_in_b="w_in")
    recv = [dict() for _ in range(DEPTH)]

    def scatter_of(names, g, into):
        ex = _Exchange([g[source.get(n, n)] for n in names], [slicer[n] for n in names],
                       [sh[n].shape[1:] for n in names])
        return ex, lambda recv_: into.update(zip(names, recv_))

    small_rows = [None] * DEPTH
    prev = None
    rest = ["w_in_b", "w_mk", "w_mv"]
    for l in reversed(range(DEPTH)):
        plan = {"grad_w_up": lambda g, l=l: scatter_of(["w_down"], g, recv[l]),
                "hgrn_bwd": lambda g, l=l: scatter_of(["w_up", "w_o", "w_br"], g, recv[l])}
        if l == 0:
            plan["in_proj_bwd"] = lambda g: scatter_of(["w_in_a"] + rest, g, recv[0])
        else:
            plan["in_proj_bwd"] = lambda g, l=l: scatter_of(["w_in_a"], g, recv[l])
        if prev is not None:
            plan["ln2_bwd_down"] = lambda g, l=l, prev=prev: scatter_of(rest, prev, recv[l + 1])
        dcur, prev, small_rows[l] = _layer_bwd(dcur, mem2d, saved[l], layer_w[l], bsz=bsz, seq=seq, plan=plan)
    for r in recv:
        r["w_in"] = jnp.concatenate([r.pop("w_in_a"), r.pop("w_in_b")], axis=1)

    packed = _pack_small(small_rows)
    all_small = _exchange("gather_small_grads", [_Exchange([packed], piece_shapes=[packed.shape])])[0][0]
    small_grads = _unpack_small(_sum_devices("sum_small_grads", all_small))
    small_grads["lower_bounds"] = _lb_bwd(lower_bounds, small_grads["lower_bounds"])
    conv_all = small_grads["conv_w"].reshape(DEPTH, CONV_K, WIDTH)
    small_grads["conv_w"] = lax.dynamic_slice_in_dim(conv_all, me * (WIDTH // N_DEV), WIDTH // N_DEV, axis=2)

    grads, deltas, new_m, new_v = {}, {}, {}, {}
    given = dict(lower_bounds=(lower_bounds, m_lower_bounds, v_lower_bounds), conv_w=(conv_w, m_conv_w, v_conv_w),
                 hg_norm_w=(hg_norm_w, m_hg_norm_w, v_hg_norm_w), b_gate=(b_gate, m_b_gate, v_b_gate),
                 ln1_g=(ln1_g, m_ln1_g, v_ln1_g), ln1_b=(ln1_b, m_ln1_b, v_ln1_b),
                 ln2_g=(ln2_g, m_ln2_g, v_ln2_g), ln2_b=(ln2_b, m_ln2_b, v_ln2_b))
    for name, (w_, m_, v_) in given.items():
        g_ = small_grads[name].reshape(w_.shape)
        grads[name] = g_
        deltas[name], new_m[name], new_v[name] = _adam_small("adam_" + name, g_, w_, m_, v_)

    big = dict(w_in=("w_in", w_in, m_w_in, v_w_in, 128), w_mem_k=("w_mk", w_mem_k, m_w_mem_k, v_w_mem_k, 128),
               w_mem_v=("w_mv", w_mem_v, m_w_mem_v, v_w_mem_v, 128),
               w_branch=("w_br", w_branch, m_w_branch, v_w_branch, 512), w_o=("w_o", w_o, m_w_o, v_w_o, 128),
               w_up=("w_up", w_up, m_w_up, v_w_up, 256), w_down=("w_down", w_down, m_w_down, v_w_down, 128))
    for name, (k, w_, m_, v_, tr) in big.items():
        shape = w_.shape
        flat = (DEPTH, -1, shape[-1])
        rc = [recv[l][k].reshape((N_DEV,) + w_.reshape(flat).shape[1:]) for l in range(DEPTH)]
        outs = _adam_shard("adam_" + name, rc, w_.reshape(flat), m_.reshape(flat), v_.reshape(flat), tr=tr)
        grads[name], deltas[name], new_m[name], new_v[name] = (o.reshape(shape) for o in outs)

    order = ["lower_bounds", "w_in", "conv_w", "hg_norm_w", "w_mem_k", "w_mem_v", "w_branch", "b_gate", "w_o",
             "ln1_g", "ln1_b", "w_up", "w_down", "ln2_g", "ln2_b"]
    return (loss, dcur.reshape(x.shape), *[grads[n] for n in order], *[deltas[n] for n in order],
            *[new_m[n] for n in order], *[new_v[n] for n in order])
```

```python
import functools

import jax
import jax.numpy as jnp
from jax import lax
from jax.experimental import pallas as pl
from jax.experimental.pallas import tpu as pltpu

F32 = jnp.float32
BF16 = jnp.bfloat16

N_DEV = 8
D_MODEL = 1024
DEPTH = 4
MEM_LEN = 256
CONV_K = 3
WIDTH = 512
HEADS = 4
HEAD_DIM = 128
CHUNK = 32
D_FF = 4 * D_MODEL
IN_COLS = 7168
ALPHA = (2.0 * DEPTH) ** 0.25
LN_EPS = 1e-5
RMS_EPS = 1e-6
ADAM_LR = 0.001
ADAM_B1 = 0.9
ADAM_B2 = 0.999
ADAM_EPS = 1e-08
ADAM_WD = 0.01
ADAM_STEP = 10

C_CB, C_CC, C_CH, C_HQ, C_HF, C_HI, C_HG, C_MQ, C_GA = 0, 512, 1024, 1536, 2048, 2560, 3072, 3584, 4096

ROWS_HG = 256
NT_DIMS = (((1,), (1,)), ((), ()))
TN_DIMS = (((0,), (0,)), ((), ()))
MESH = pl.DeviceIdType.MESH


def _dot(a, b):
    return jnp.dot(a, b, preferred_element_type=F32)


def _dot_nt(a, b):
    return lax.dot_general(a, b, NT_DIMS, preferred_element_type=F32)


def _dot_tn(a, b):
    return lax.dot_general(a, b, TN_DIMS, preferred_element_type=F32)


def _sigmoid(x):
    return 1.0 / (1.0 + jnp.exp(-x))


def _params(*sem):
    return pltpu.CompilerParams(dimension_semantics=sem)


def _resident(shape, single=False):
    nd = len(shape)
    if single:
        return pl.BlockSpec(shape, lambda *_: (0,) * nd, pipeline_mode=pl.Buffered(1))
    return pl.BlockSpec(shape, lambda *_: (0,) * nd)


def _my_id():
    return 4 * lax.axis_index("x") + 2 * lax.axis_index("y") + lax.axis_index("c")


class _Exchange:
    def __init__(self, srcs, slicers=None, piece_shapes=None, route="all"):
        self.srcs, self.n, self.route = list(srcs), len(srcs), route
        self.slicers = list(slicers) if slicers else [_whole] * self.n
        any_spec = pl.BlockSpec(memory_space=pl.ANY)
        self.in_specs = [any_spec] * self.n
        self.out_specs = [any_spec] * self.n
        if route == "relay":
            self.out_shape = [jax.ShapeDtypeStruct(a.shape, a.dtype) for a in srcs]
        else:
            self.out_shape = [jax.ShapeDtypeStruct((N_DEV,) + tuple(s), a.dtype) for s, a in zip(piece_shapes, srcs)]
        self.aliased = route == "relay"
        self.scratch = [pltpu.SemaphoreType.DMA((self.n * N_DEV,)), pltpu.SemaphoreType.DMA((self.n * N_DEV,)),
                        pltpu.SemaphoreType.DMA((self.n,))]

    def _peer(self, j, me):
        if self.route == "all":
            return j != me
        return (j != me) & ((j % 2 == lax.axis_index("c")) | (j // 2 == me // 2))

    def _remote(self, ins, outs, sems, k, j, me):
        return pltpu.make_async_remote_copy(
            src_ref=self.slicers[k](ins[k], j), dst_ref=outs[k].at[me],
            send_sem=sems[0].at[k * N_DEV + j], recv_sem=sems[1].at[k * N_DEV + me],
            device_id=(j // 4, (j // 2) % 2, j % 2), device_id_type=MESH)

    def _local(self, ins, outs, sems, k, j, me):
        return pltpu.make_async_copy(self.slicers[k](ins[k], j), outs[k].at[me], sems[2].at[k])

    def _relay(self, outs, sems, k, j):
        sibling = (lax.axis_index("x"), lax.axis_index("y"), 1 - lax.axis_index("c"))
        return pltpu.make_async_remote_copy(
            src_ref=outs[k].at[j], dst_ref=outs[k].at[j], send_sem=sems[0].at[k * N_DEV + j],
            recv_sem=sems[1].at[k * N_DEV + j], device_id=sibling, device_id_type=MESH)

    def _other_chip(self, j, same_core):
        on_my_core = j % 2 == lax.axis_index("c")
        return (on_my_core if same_core else ~on_my_core) & (j // 2 != _my_id() // 2)

    def start(self, ins, outs, sems):
        me = _my_id()
        for k in range(self.n):
            for j in range(N_DEV):
                if self.route == "relay":
                    @pl.when(self._other_chip(j, True))
                    def _():
                        self._relay(outs, sems, k, j).start()
                    continue

                @pl.when(self._peer(j, me))
                def _():
                    self._remote(ins, outs, sems, k, j, me).start()

                @pl.when(j == me)
                def _():
                    self._local(ins, outs, sems, k, j, me).start()

    def wait(self, ins, outs, sems):
        me = _my_id()
        for k in range(self.n):
            for j in range(N_DEV):
                if self.route == "relay":
                    @pl.when(self._other_chip(j, False))
                    def _():
                        self._relay(outs, sems, k, j).wait_recv()

                    @pl.when(self._other_chip(j, True))
                    def _():
                        self._relay(outs, sems, k, j).wait_send()
                    continue

                @pl.when(self._peer(j, me))
                def _():
                    pltpu.make_async_remote_copy(
                        src_ref=self.slicers[k](ins[k], j), dst_ref=outs[k].at[j],
                        send_sem=sems[0].at[k * N_DEV + j], recv_sem=sems[1].at[k * N_DEV + j],
                        device_id=(j // 4, (j // 2) % 2, j % 2), device_id_type=MESH).wait_recv()
                    self._remote(ins, outs, sems, k, j, me).wait_send()

                @pl.when(j == me)
                def _():
                    self._local(ins, outs, sems, k, j, me).wait()


def _carried(exchanges, n_in, n_out):
    c_in = [s for ex in exchanges for s in ex.in_specs]
    c_out = [s for ex in exchanges for s in ex.out_specs]
    shapes = [s for ex in exchanges for s in ex.out_shape]
    sems = [s for ex in exchanges for s in ex.scratch]
    srcs = [a for ex in exchanges for a in ex.srcs]
    aliases, off = {}, 0
    for ex in exchanges:
        if ex.aliased:
            aliases.update({n_in + off + k: n_out + off + k for k in range(ex.n)})
        off += ex.n
    total = off

    def split(refs, n_scr):
        ins, outs = refs[:n_in], refs[n_in + total:n_in + total + n_out]
        rest = refs[n_in + 2 * total + n_out:]
        scr, sem_refs = rest[:n_scr], rest[n_scr:]
        parts, off_ = [], 0
        for i, ex in enumerate(exchanges):
            parts.append((refs[n_in + off_:n_in + off_ + ex.n],
                          refs[n_in + total + n_out + off_:n_in + total + n_out + off_ + ex.n],
                          sem_refs[3 * i:3 * i + 3]))
            off_ += ex.n
        return ins, outs, scr, parts

    return c_in, c_out, shapes, sems, srcs, aliases, split


def _exchange(name, exchanges):
    c_in, c_out, shapes, sems, srcs, aliases, split = _carried(exchanges, 0, 0)

    def body(*refs):
        _, _, _, parts = split(refs, 0)
        for ex, part in zip(exchanges, parts):
            ex.start(*part)
        for ex, part in zip(exchanges, parts):
            ex.wait(*part)

    outs = pl.pallas_call(
        body, name=name, in_specs=c_in, out_specs=c_out, out_shape=shapes, scratch_shapes=sems,
        input_output_aliases=aliases, compiler_params=pltpu.CompilerParams(has_side_effects=True))(*srcs)
    return _per_exchange(exchanges, outs)


def _per_exchange(exchanges, flat):
    out, off = [], 0
    for ex in exchanges:
        out.append(flat[off:off + ex.n])
        off += ex.n
    return out


def _call(body, name, grid, in_specs, out_specs, out_shape, args, scratch=(), sem=None, carry=None):
    n_in, n_out, n_scr = len(in_specs), len(out_specs), len(scratch)
    if not carry:
        outs = pl.pallas_call(body, name=name, grid=grid, in_specs=in_specs, out_specs=out_specs,
                              out_shape=out_shape, scratch_shapes=list(scratch),
                              compiler_params=_params(*sem))(*args)
        return outs, None
    c_in, c_out, shapes, sems, srcs, aliases, split = _carried(carry, n_in, n_out)

    def hosted(*refs):
        ins, outs, scr, parts = split(refs, n_scr)
        first, last = True, True
        for d, size in enumerate(grid):
            first = first & (pl.program_id(d) == 0)
            last = last & (pl.program_id(d) == size - 1)

        @pl.when(first)
        def _():
            for ex, part in zip(carry, parts):
                ex.start(*part)

        body(*ins, *outs, *scr)

        @pl.when(last)
        def _():
            for ex, part in zip(carry, parts):
                ex.wait(*part)

    outs = pl.pallas_call(
        hosted, name=name + "_x", grid=grid, in_specs=list(in_specs) + c_in,
        out_specs=list(out_specs) + c_out, out_shape=list(out_shape) + shapes,
        scratch_shapes=list(scratch) + sems, input_output_aliases=aliases,
        compiler_params=_params(*(["arbitrary"] * len(grid))))(*args, *srcs)
    return outs[:n_out], _per_exchange(carry, outs[n_out:])


def _whole(ref, j):
    return ref


def _cols(width):
    return lambda ref, j: ref.at[(slice(None),) * (len(ref.shape) - 1) + (pl.ds(j * width, width),)]


def _rows(height):
    return lambda ref, j: ref.at[pl.ds(j * height, height)]


def _mm_nn(name, a, w, *, tm, tn, out_dtype, relu2=False, carry=None):
    t, k = a.shape
    n = w.shape[1]

    def body(a_ref, w_ref, o_ref):
        acc = _dot(a_ref[...].astype(BF16), w_ref[...])
        if relu2:
            r = jnp.maximum(acc, 0.0)
            acc = r * r
        o_ref[...] = acc.astype(out_dtype)

    outs, recv = _call(
        body, name, (t // tm, n // tn),
        [pl.BlockSpec((tm, k), lambda i, j: (i, 0)), pl.BlockSpec((k, tn), lambda i, j: (0, j))],
        [pl.BlockSpec((tm, tn), lambda i, j: (i, j))], [jax.ShapeDtypeStruct((t, n), out_dtype)], (a, w),
        sem=("parallel", "parallel"), carry=carry)
    return outs[0] if carry is None else (outs[0], recv)


def _in_proj(a, w, *, tm, carry=None):
    t, k = a.shape
    tn = IN_COLS // 4
    f_tile, f_off = C_HF // tn, C_HF % tn

    def body(a_ref, w_ref, o_ref, f_ref):
        acc = _dot(a_ref[...], w_ref[...])
        o_ref[...] = acc.astype(BF16)

        @pl.when(pl.program_id(1) == f_tile)
        def _():
            f_ref[...] = acc[:, f_off:f_off + WIDTH]

    outs, recv = _call(
        body, "in_proj", (t // tm, IN_COLS // tn),
        [pl.BlockSpec((tm, k), lambda i, j: (i, 0)), pl.BlockSpec((k, tn), lambda i, j: (0, j))],
        [pl.BlockSpec((tm, tn), lambda i, j: (i, j)), pl.BlockSpec((tm, WIDTH), lambda i, j: (i, 0))],
        [jax.ShapeDtypeStruct((t, IN_COLS), BF16), jax.ShapeDtypeStruct((t, WIDTH), F32)], (a, w),
        sem=("parallel", "arbitrary"), carry=carry)
    return outs if carry is None else (outs, recv)


def _linear_ln(name, a, w, resid, g, b, *, tm, carry=None):
    t, k = a.shape
    halves = [slice(0, tm // 2), slice(tm // 2, tm)] if k > D_MODEL else [slice(0, tm)]

    def body(a_ref, w_ref, r_ref, g_ref, b_ref, z_ref, x_ref, xb_ref):
        z = ALPHA * _Lanes(r_ref[s, :] for s in halves) + _ldot(_Lanes(a_ref[s, :] for s in halves), w_ref[...])
        zc = z - _mean(z, axis=-1, keepdims=True)
        y = zc * _rsqrt(_mean(zc * zc, axis=-1, keepdims=True) + LN_EPS) * g_ref[...] + b_ref[...]
        for s, zz, yy in zip(halves, z.xs, y.xs):
            z_ref[s, :] = zz
            x_ref[s, :] = yy
            xb_ref[s, :] = yy.astype(BF16)

    row = pl.BlockSpec((tm, D_MODEL), lambda i: (i, 0))
    outs, recv = _call(
        body, name, (t // tm,),
        [pl.BlockSpec((tm, k), lambda i: (i, 0)), _resident((k, D_MODEL)), row,
         _resident((1, D_MODEL)), _resident((1, D_MODEL))],
        [row, row, row],
        [jax.ShapeDtypeStruct((t, D_MODEL), F32)] * 2 + [jax.ShapeDtypeStruct((t, D_MODEL), BF16)],
        (a, w, resid, g, b), sem=("parallel",), carry=carry)
    return outs if carry is None else (outs, recv)


def _ln_bwd_mm_nt(name, dy, z, g, w, h=None, left=None, *, tm, tn, carry=None):
    t = dy.shape[0]
    n = w.shape[0]
    halves = [slice(0, tm // 2), slice(tm // 2, tm)]
    last = t // tm - 1

    def body(*refs):
        refs = list(refs)
        dy_ref, z_ref, g_ref, w_ref = refs[:4]
        del refs[:4]
        h_ref = refs.pop(0) if h is not None else None
        left_ref = refs.pop(0) if left is not None else None
        dz_ref, dzb_ref, o_ref, dg_ref, db_ref = refs[:5]
        gw_ref, acc_ref = (refs[5], refs[6]) if left is not None else (None, None)

        @pl.when(pl.program_id(0) == 0)
        def _():
            dg_ref[...] = jnp.zeros_like(dg_ref)
            db_ref[...] = jnp.zeros_like(db_ref)
            if left is not None:
                acc_ref[...] = jnp.zeros_like(acc_ref)

        zv = _Lanes(z_ref[s, :] for s in halves)
        dyv = _Lanes(dy_ref[s, :] for s in halves)
        mu = _mean(zv, axis=-1, keepdims=True)
        zc = zv - mu
        rstd = _rsqrt(_mean(zc * zc, axis=-1, keepdims=True) + LN_EPS)
        xh = zc * rstd
        gdy = dyv * g_ref[...]
        m1 = _mean(gdy, axis=-1, keepdims=True)
        m2 = _mean(gdy * xh, axis=-1, keepdims=True)
        dz = rstd * (gdy - m1 - xh * m2)
        dz_b = dz.astype(BF16)
        for s, a, a_b in zip(halves, dz.xs, dz_b.xs):
            dz_ref[s, :] = a
            dzb_ref[s, :] = a_b
        dg_ref[...] += _sum(dyv * xh, axis=0, keepdims=True).total()
        db_ref[...] += _sum(dyv, axis=0, keepdims=True).total()
        for c in range(n // tn):
            cols = slice(c * tn, (c + 1) * tn)
            acc = _ldot_nt(dz_b, w_ref[cols, :])
            if h is not None:
                acc = acc * (2.0 * _sqrt(_Lanes(h_ref[s, cols] for s in halves).astype(F32)))
            for s, a in zip(halves, acc.xs):
                o_ref[s, cols] = a.astype(BF16)
        if left is not None:
            acc_ref[...] += _ldot_tn(_Lanes(left_ref[s, :] for s in halves), dz_b).total()

            @pl.when(pl.program_id(0) == last)
            def _():
                gw_ref[...] = acc_ref[...].astype(BF16)

    row = pl.BlockSpec((tm, D_MODEL), lambda i: (i, 0))
    vec = _resident((1, D_MODEL))
    tile = pl.BlockSpec((tm, n), lambda i: (i, 0))
    in_specs = [row, row, vec, _resident((n, D_MODEL))]
    args = [dy, z, g, w]
    out_specs = [row, row, tile, vec, vec]
    out_shape = [jax.ShapeDtypeStruct((t, D_MODEL), F32), jax.ShapeDtypeStruct((t, D_MODEL), BF16),
                 jax.ShapeDtypeStruct((t, n), BF16), jax.ShapeDtypeStruct((1, D_MODEL), F32),
                 jax.ShapeDtypeStruct((1, D_MODEL), F32)]
    scratch = []
    if h is not None:
        in_specs.append(tile)
        args.append(h)
    if left is not None:
        m = left.shape[1]
        in_specs.append(pl.BlockSpec((tm, m), lambda i: (i, 0)))
        args.append(left)
        out_specs.append(_resident((m, D_MODEL)))
        out_shape.append(jax.ShapeDtypeStruct((m, D_MODEL), BF16))
        scratch.append(pltpu.VMEM((m, D_MODEL), F32))
    outs, recv = _call(body, name, (t // tm,), in_specs, out_specs, out_shape, args, scratch=scratch,
                       sem=("arbitrary",), carry=carry)
    return outs if carry is None else (outs, recv)


def _mm_tn(name, a, b, *, tk, tmo, tno, carry=None):
    t, m = a.shape
    n = b.shape[1]
    nk = t // tk

    def body(a_ref, b_ref, o_ref, acc_ref):
        k = pl.program_id(2)
        p = _dot_tn(a_ref[...].astype(BF16), b_ref[...].astype(BF16))

        @pl.when(k == 0)
        def _():
            acc_ref[...] = p

        @pl.when(k > 0)
        def _():
            acc_ref[...] += p

        @pl.when(k == nk - 1)
        def _():
            o_ref[...] = acc_ref[...].astype(BF16)

    outs, recv = _call(
        body, name, (m // tmo, n // tno, nk),
        [pl.BlockSpec((tk, tmo), lambda i, j, k: (k, i)), pl.BlockSpec((tk, tno), lambda i, j, k: (k, j))],
        [pl.BlockSpec((tmo, tno), lambda i, j, k: (i, j))], [jax.ShapeDtypeStruct((m, n), BF16)], (a, b),
        scratch=[pltpu.VMEM((tmo, tno), F32)], sem=("parallel", "parallel", "arbitrary"), carry=carry)
    return outs[0] if carry is None else (outs[0], recv)


def _mm_nt_sum(name, pieces, offsets, w, resid, *, tm, carry=None):
    t = resid.shape[0]
    widths = [p.shape[1] for p in pieces]
    n_p = len(pieces)

    def body(*refs):
        p_refs, w_ref, r_ref, o_ref = refs[:n_p], refs[n_p], refs[n_p + 1], refs[n_p + 2]
        acc = ALPHA * r_ref[...]
        for p_ref, off, wd in zip(p_refs, offsets, widths):
            acc = acc + _dot_nt(p_ref[...], w_ref[:, off:off + wd])
        o_ref[...] = acc

    row = pl.BlockSpec((tm, D_MODEL), lambda i: (i, 0))
    outs, recv = _call(
        body, name, (t // tm,),
        [pl.BlockSpec((tm, wd), lambda i: (i, 0)) for wd in widths] + [_resident(w.shape, single=True), row],
        [row], [jax.ShapeDtypeStruct((t, D_MODEL), F32)], (*pieces, w, resid), sem=("parallel",), carry=carry)
    return outs[0] if carry is None else (outs[0], recv)


def _chunk_mask(rows):
    r = lax.broadcasted_iota(jnp.int32, (rows, rows), 0)
    c = lax.broadcasted_iota(jnp.int32, (rows, rows), 1)
    return ((r // CHUNK) == (c // CHUNK)) & (c <= r)


class _Lanes:
    def __init__(self, xs):
        self.xs = list(xs)

    def _with(self, other, f):
        if isinstance(other, _Lanes):
            return _Lanes([f(a, b) for a, b in zip(self.xs, other.xs)])
        return _Lanes([f(a, other) for a in self.xs])

    def __add__(self, o):
        return self._with(o, lambda a, b: a + b)

    def __radd__(self, o):
        return self._with(o, lambda a, b: b + a)

    def __sub__(self, o):
        return self._with(o, lambda a, b: a - b)

    def __rsub__(self, o):
        return self._with(o, lambda a, b: b - a)

    def __mul__(self, o):
        return self._with(o, lambda a, b: a * b)

    def __rmul__(self, o):
        return self._with(o, lambda a, b: b * a)

    def __truediv__(self, o):
        return self._with(o, lambda a, b: a / b)

    def __rtruediv__(self, o):
        return self._with(o, lambda a, b: b / a)

    def __neg__(self):
        return _Lanes([-a for a in self.xs])

    def __ge__(self, o):
        return self._with(o, lambda a, b: a >= b)

    def __getitem__(self, idx):
        return _Lanes([a[idx] for a in self.xs])

    def astype(self, dtype):
        return _Lanes([a.astype(dtype) for a in self.xs])

    def total(self):
        return functools.reduce(lambda a, b: a + b, self.xs)


def _lift(f):
    def g(*args, **kw):
        lanes = [a for a in args if isinstance(a, _Lanes)]
        if not lanes:
            return f(*args, **kw)
        return _Lanes([f(*[a.xs[i] if isinstance(a, _Lanes) else a for a in args], **kw)
                       for i in range(len(lanes[0].xs))])
    return g


def _concat(parts, axis):
    if isinstance(parts[0], _Lanes):
        return _Lanes([jnp.concatenate([p.xs[i] for p in parts], axis=axis) for i in range(len(parts[0].xs))])
    return jnp.concatenate(parts, axis=axis)


_exp, _log, _abs, _sqrt, _where = _lift(jnp.exp), _lift(jnp.log), _lift(jnp.abs), _lift(jnp.sqrt), _lift(jnp.where)
_sum, _mean, _rsqrt, _bcast = _lift(jnp.sum), _lift(jnp.mean), _lift(lax.rsqrt), _lift(jnp.broadcast_to)
_ldot, _ldot_nt, _ldot_tn = _lift(_dot), _lift(_dot_nt), _lift(_dot_tn)
_lsigmoid = _lift(_sigmoid)


def _mask_sum(mask_b, x, transpose=False):
    f = _ldot_tn if transpose else _ldot
    hi = x.astype(BF16)
    lo = (x - hi.astype(F32)).astype(BF16)
    return f(mask_b, hi) + f(mask_b, lo)


def _chunk_row(x, pos, rows):
    nc = rows // CHUNK

    def one(a):
        a3 = a.reshape(nc, CHUNK, HEAD_DIM)
        return jnp.broadcast_to(a3[:, pos:pos + 1, :], (nc, CHUNK, HEAD_DIM)).reshape(rows, HEAD_DIM)

    return _lift(one)(x)


def _chunk_total(x, rows):
    nc = rows // CHUNK

    def one(a):
        tot = jnp.sum(a.reshape(nc, CHUNK, HEAD_DIM), axis=1, keepdims=True)
        return jnp.broadcast_to(tot, (nc, CHUNK, HEAD_DIM)).reshape(rows, HEAD_DIM)

    return _lift(one)(x)


def _sigmoid_pair(x):
    e = _exp(-_abs(x))
    big = 1.0 / (1.0 + e)
    small = e * big
    pos = x >= 0.0
    return _where(pos, big, small), _where(pos, small, big)


def _hg_gates(q_raw, fl, lb, rows, mask):
    tri = mask.astype(BF16)
    sg, sg_neg = _sigmoid_pair(fl)
    forget = lb + (1.0 - lb) * sg
    k = (1.0 - lb) * sg_neg
    sq = _lsigmoid(q_raw)
    qs = q_raw * sq
    bc = _mask_sum(tri, _log(forget))
    bref = _chunk_row(bc, CHUNK // 2 - 1, rows)
    blast = _chunk_row(bc, CHUNK - 1, rows)
    return dict(tri=tri, sg=sg, sg_neg=sg_neg, forget=forget, k=k, sq=sq, qs=qs,
                e_a=_exp(bc - bref), e_b=_exp(bref - bc), e_q=_exp(bc), e_k=_exp(blast - bc),
                dec=_exp(blast))


HG_GROUP = 4


def _hg_lanes(bsz):
    return [(hh, slice(hh * HEAD_DIM, (hh + 1) * HEAD_DIM), b) for hh in range(HG_GROUP) for b in range(bsz)]


def _hg_read(ref, lanes):
    return _Lanes(ref[b, :, cs].astype(F32) for _, cs, b in lanes)


def _hg_write(ref, lanes, val, offset=0):
    for (_, cs, b), a in zip(lanes, val.xs):
        ref[b, :, offset + cs.start:offset + cs.stop] = a


def _hgrn_fwd(proj, hf, lb, nw, *, bsz, seq, carry=None):
    rows = min(ROWS_HG, seq)
    nt = seq // rows
    nc = rows // CHUNK
    t = bsz * seq

    lanes = _hg_lanes(bsz)

    def body(q_ref, f_ref, v_ref, g_ref, lb_ref, nw_ref, y_ref, o_ref, st_ref, s_scr):
        @pl.when(pl.program_id(1) == 0)
        def _():
            s_scr[...] = jnp.zeros_like(s_scr)

        mask = _chunk_mask(rows)
        lb_v = _Lanes(lb_ref[:, cs] for _, cs, _ in lanes)
        gt = _hg_gates(_hg_read(q_ref, lanes), _hg_read(f_ref, lanes), lb_v, rows, mask)
        v_b = _hg_read(v_ref, lanes).astype(BF16)
        a_b = (gt["qs"] * gt["e_a"]).astype(BF16)
        b_b = (gt["k"] * gt["e_b"]).astype(BF16)
        qi_b = (gt["qs"] * gt["e_q"]).astype(BF16)
        ko_b = (gt["k"] * gt["e_k"]).astype(BF16)
        scores = _where(mask, _ldot_nt(a_b, b_b), 0.0)
        o_intra = _ldot(scores.astype(BF16), v_b)

        s = _Lanes(s_scr[i] for i in range(len(lanes)))
        parts = []
        for n in range(nc):
            sl = slice(n * CHUNK, (n + 1) * CHUNK)
            s_b = s.astype(BF16)
            for (hh, _, b), a in zip(lanes, s_b.xs):
                st_ref[hh, b, n] = a
            parts.append(_ldot_nt(qi_b[sl], s_b))
            s = s * gt["dec"][n * CHUNK:n * CHUNK + 1] + _ldot_tn(v_b[sl], ko_b[sl])
        for i, a in enumerate(s.xs):
            s_scr[i] = a
        o = o_intra + _concat(parts, 0)
        _hg_write(o_ref, lanes, o)
        r = _rsqrt(_mean(o * o, axis=-1, keepdims=True) + RMS_EPS)
        g = _hg_read(g_ref, lanes)
        _hg_write(y_ref, lanes, (o * r * nw_ref[...] * (g * _lsigmoid(g))).astype(BF16))

    wide = HG_GROUP * HEAD_DIM

    def col(base):
        return pl.BlockSpec((bsz, rows, wide), lambda h, j: (0, j, base // wide + h))

    out_tile = pl.BlockSpec((bsz, rows, wide), lambda h, j: (0, j, h))
    p3 = proj.reshape(bsz, seq, IN_COLS)
    outs, recv = _call(
        body, "hgrn_fwd", (HEADS // HG_GROUP, nt),
        [col(C_HQ), out_tile, col(C_HI), col(C_HG),
         pl.BlockSpec((1, wide), lambda h, j: (0, h)), _resident((1, HEAD_DIM))],
        [out_tile, out_tile,
         pl.BlockSpec((HG_GROUP, bsz, nc, HEAD_DIM, HEAD_DIM), lambda h, j: (h, 0, j, 0, 0))],
        [jax.ShapeDtypeStruct((bsz, seq, WIDTH), BF16), jax.ShapeDtypeStruct((bsz, seq, WIDTH), F32),
         jax.ShapeDtypeStruct((HEADS, bsz, seq // CHUNK, HEAD_DIM, HEAD_DIM), BF16)],
        (p3, hf.reshape(bsz, seq, WIDTH), p3, p3, lb, nw),
        scratch=[pltpu.VMEM((len(lanes), HEAD_DIM, HEAD_DIM), F32)],
        sem=("parallel", "arbitrary"), carry=carry)
    outs = [outs[0].reshape(t, WIDTH), outs[1].reshape(t, WIDTH), outs[2]]
    return outs if carry is None else (outs, recv)


def _hgrn_bwd(proj, hf, lb, nw, o_pre, states, dy, *, bsz, seq, carry=None):
    rows = min(ROWS_HG, seq)
    nt = seq // rows
    nc = rows // CHUNK
    t = bsz * seq
    lanes = _hg_lanes(bsz)

    def body(q_ref, f_ref, v_ref, g_ref, lb_ref, nw_ref, o_ref, st_ref, dy_ref, dh_ref, dlb_ref, dnw_ref, ds_scr):
        h, j = pl.program_id(0), pl.program_id(1)

        @pl.when(j == 0)
        def _():
            ds_scr[...] = jnp.zeros_like(ds_scr)
            dlb_ref[...] = jnp.zeros_like(dlb_ref)

        @pl.when((h == 0) & (j == 0))
        def _():
            dnw_ref[...] = jnp.zeros_like(dnw_ref)

        mask = _chunk_mask(rows)
        q_raw = _hg_read(q_ref, lanes)
        lb_v = _Lanes(lb_ref[:, cs] for _, cs, _ in lanes)
        gt = _hg_gates(q_raw, _hg_read(f_ref, lanes), lb_v, rows, mask)
        v_b = _hg_read(v_ref, lanes).astype(BF16)
        a_f = gt["qs"] * gt["e_a"]
        b_f = gt["k"] * gt["e_b"]
        qi_f = gt["qs"] * gt["e_q"]
        ko_f = gt["k"] * gt["e_k"]
        a_b, b_b, qi_b, ko_b = a_f.astype(BF16), b_f.astype(BF16), qi_f.astype(BF16), ko_f.astype(BF16)

        o = _hg_read(o_ref, lanes)
        nw_v = nw_ref[...]
        g = _hg_read(g_ref, lanes)
        dyv = _hg_read(dy_ref, lanes)
        r = _rsqrt(_mean(o * o, axis=-1, keepdims=True) + RMS_EPS)
        sgg = _lsigmoid(g)
        d_g = dyv * (o * r * nw_v) * (sgg * (1.0 + g * (1.0 - sgg)))
        d_on = dyv * (g * sgg)
        dnw_ref[...] += _sum(d_on * o * r, axis=0, keepdims=True).total()
        tt = d_on * nw_v
        d_o = r * tt - o * (r * r * r) * _mean(tt * o, axis=-1, keepdims=True)
        do_b = d_o.astype(BF16)

        sc_b = _where(mask, _ldot_nt(a_b, b_b), 0.0).astype(BF16)
        dsc_b = _where(mask, _ldot_nt(do_b, v_b), 0.0).astype(BF16)
        d_v = _ldot_tn(sc_b, do_b)
        d_a = _ldot(dsc_b, b_b)
        d_bm = _ldot_tn(dsc_b, a_b)

        ds = _Lanes(ds_scr[i] for i in range(len(lanes)))
        dqi_parts, dko_parts, dvi_parts, ddec_parts = [None] * nc, [None] * nc, [None] * nc, [None] * nc
        for n in reversed(range(nc)):
            sl = slice(n * CHUNK, (n + 1) * CHUNK)
            dec_n = gt["dec"][n * CHUNK:n * CHUNK + 1]
            ds_b = ds.astype(BF16)
            s_n = _Lanes(st_ref[hh, b, n] for hh, _, b in lanes)
            dqi_parts[n] = _ldot(do_b[sl], s_n)
            dko_parts[n] = _ldot(v_b[sl], ds_b)
            dvi_parts[n] = _ldot_nt(ko_b[sl], ds_b)
            d_dec = _sum(ds * s_n.astype(F32), axis=0, keepdims=True)
            ddec_parts[n] = _bcast(d_dec * dec_n, (CHUNK, HEAD_DIM))
            ds = ds * dec_n + _ldot_tn(do_b[sl], qi_b[sl])
        for i, a in enumerate(ds.xs):
            ds_scr[i] = a
        d_qi = _concat(dqi_parts, 0)
        d_ko = _concat(dko_parts, 0)
        d_v = d_v + _concat(dvi_parts, 0)

        d_qs = d_a * gt["e_a"] + d_qi * gt["e_q"]
        d_k = d_bm * gt["e_b"] + d_ko * gt["e_k"]
        t_a, t_b, t_q, t_k = d_a * a_f, d_bm * b_f, d_qi * qi_f, d_ko * ko_f
        d_bref = _chunk_total(t_b - t_a, rows)
        d_blast = _chunk_total(t_k, rows) + _concat(ddec_parts, 0)
        pos = lax.broadcasted_iota(jnp.int32, (rows, HEAD_DIM), 0) % CHUNK
        d_bc = (t_a - t_b + t_q - t_k + _where(pos == CHUNK // 2 - 1, d_bref, 0.0)
                + _where(pos == CHUNK - 1, d_blast, 0.0))
        d_logf = _mask_sum(gt["tri"], d_bc, transpose=True)

        sg, sg_neg = gt["sg"], gt["sg_neg"]
        inv_f = 1.0 / gt["forget"]
        common = (1.0 - lb_v) * sg * sg_neg
        d_fl = common * (d_logf * inv_f - d_k)
        d_lb = _sum(sg_neg * (d_logf * inv_f - d_k), axis=0, keepdims=True)
        for (_, cs, _), a in zip(lanes, d_lb.xs):
            dlb_ref[:, cs] += a
        sq = gt["sq"]
        _hg_write(dh_ref, lanes, (d_qs * (sq * (1.0 + q_raw * (1.0 - sq)))).astype(BF16), 0)
        _hg_write(dh_ref, lanes, d_fl.astype(BF16), WIDTH)
        _hg_write(dh_ref, lanes, d_v.astype(BF16), 2 * WIDTH)
        _hg_write(dh_ref, lanes, d_g.astype(BF16), 3 * WIDTH)

    assert HG_GROUP == HEADS, "the combined gradient block needs all heads in one grid step"
    wide = HG_GROUP * HEAD_DIM

    def col(base):
        return pl.BlockSpec((bsz, rows, wide), lambda h, j: (0, nt - 1 - j, base // wide + h))

    tile = pl.BlockSpec((bsz, rows, wide), lambda h, j: (0, nt - 1 - j, h))
    head_vec = pl.BlockSpec((1, wide), lambda h, j: (0, h))
    p3 = proj.reshape(bsz, seq, IN_COLS)
    outs, recv = _call(
        body, "hgrn_bwd", (HEADS // HG_GROUP, nt),
        [col(C_HQ), tile, col(C_HI), col(C_HG), head_vec, _resident((1, HEAD_DIM)), tile,
         pl.BlockSpec((HG_GROUP, bsz, nc, HEAD_DIM, HEAD_DIM), lambda h, j: (h, 0, nt - 1 - j, 0, 0)), tile],
        [pl.BlockSpec((bsz, rows, 4 * WIDTH), lambda h, j: (0, nt - 1 - j, 0)), head_vec, _resident((1, HEAD_DIM))],
        [jax.ShapeDtypeStruct((bsz, seq, 4 * WIDTH), BF16), jax.ShapeDtypeStruct((1, WIDTH), F32),
         jax.ShapeDtypeStruct((1, HEAD_DIM), F32)],
        (p3, hf.reshape(bsz, seq, WIDTH), p3, p3, lb, nw, o_pre.reshape(bsz, seq, WIDTH), states,
         dy.reshape(bsz, seq, WIDTH)),
        scratch=[pltpu.VMEM((len(lanes), HEAD_DIM, HEAD_DIM), F32)],
        sem=("arbitrary", "arbitrary"), carry=carry)
    outs = [outs[0].reshape(t, 4 * WIDTH), outs[1], outs[2]]
    return outs if carry is None else (outs, recv)


def _mem_kv(mem2d, w_k, w_v):
    rows = mem2d.shape[0]

    def body(m_ref, wk_ref, wv_ref, k_ref, v_ref):
        m_b = m_ref[...].astype(BF16)
        k_ref[...] = _dot(m_b, wk_ref[...]).astype(BF16)
        v_ref[...] = _dot(m_b, wv_ref[...]).astype(BF16)

    return pl.pallas_call(
        body, name="mem_kv", grid=(rows // MEM_LEN,),
        in_specs=[pl.BlockSpec((MEM_LEN, D_MODEL), lambda i: (i, 0)), _resident((D_MODEL, WIDTH)),
                  _resident((D_MODEL, WIDTH))],
        out_specs=[pl.BlockSpec((MEM_LEN, WIDTH), lambda i: (i, 0))] * 2,
        out_shape=[jax.ShapeDtypeStruct((rows, WIDTH), BF16)] * 2,
        compiler_params=_params("parallel"),
    )(mem2d, w_k, w_v)


def _softmax_rows(s):
    m = _lift(jnp.max)(s, axis=-1, keepdims=True)
    e = _exp(s - m)
    return e / _sum(e, axis=-1, keepdims=True)


def _attn_fwd(proj, mk, mv, *, tm, seq):
    t = proj.shape[0]
    per_b = seq // tm
    scale = HEAD_DIM ** -0.5

    def body(q_ref, k_ref, v_ref, y_ref):
        heads = [slice(h * HEAD_DIM, (h + 1) * HEAD_DIM) for h in range(HEADS)]
        q_b = _Lanes(q_ref[:, sl] for sl in heads).astype(BF16)
        p = _softmax_rows(_ldot_nt(q_b, _Lanes(k_ref[:, sl] for sl in heads)) * scale)
        out = _ldot(p.astype(BF16), _Lanes(v_ref[:, sl] for sl in heads))
        y_ref[...] = jnp.concatenate(out.xs, axis=-1).astype(BF16)

    kv = pl.BlockSpec((MEM_LEN, WIDTH), lambda i: (i // per_b, 0))
    return pl.pallas_call(
        body, name="attn_fwd", grid=(t // tm,),
        in_specs=[pl.BlockSpec((tm, WIDTH), lambda i: (i, C_MQ // WIDTH)), kv, kv],
        out_specs=pl.BlockSpec((tm, WIDTH), lambda i: (i, 0)),
        out_shape=jax.ShapeDtypeStruct((t, WIDTH), BF16),
        compiler_params=_params("parallel"),
    )(proj, mk, mv)


def _attn_bwd(proj, mk, mv, dy, x_b, *, tm, seq):
    t = proj.shape[0]
    per_b = seq // tm
    scale = HEAD_DIM ** -0.5
    last = t // tm - 1

    def body(q_ref, k_ref, v_ref, dy_ref, x_ref, dq_ref, dk_ref, dv_ref, gw_ref, acc_ref):
        i = pl.program_id(0)

        @pl.when(i % per_b == 0)
        def _():
            dk_ref[...] = jnp.zeros_like(dk_ref)
            dv_ref[...] = jnp.zeros_like(dv_ref)

        @pl.when(i == 0)
        def _():
            acc_ref[...] = jnp.zeros_like(acc_ref)

        heads = [slice(h * HEAD_DIM, (h + 1) * HEAD_DIM) for h in range(HEADS)]
        q_b = _Lanes(q_ref[:, sl] for sl in heads).astype(BF16)
        k_b, v_b = _Lanes(k_ref[:, sl] for sl in heads), _Lanes(v_ref[:, sl] for sl in heads)
        p = _softmax_rows(_ldot_nt(q_b, k_b) * scale)
        dy_b = _Lanes(dy_ref[:, sl] for sl in heads).astype(BF16)
        dp = _ldot_nt(dy_b, v_b)
        d_v = _ldot_tn(p.astype(BF16), dy_b)
        ds_b = (p * (dp - _sum(dp * p, axis=-1, keepdims=True)) * scale).astype(BF16)
        dq_b = jnp.concatenate(_ldot(ds_b, k_b).xs, axis=-1).astype(BF16)
        dq_ref[...] = dq_b
        dk_ref[...] += jnp.concatenate(_ldot_tn(ds_b, q_b).xs, axis=-1)
        dv_ref[...] += jnp.concatenate(d_v.xs, axis=-1)
        acc_ref[...] += _dot_tn(x_ref[...], dq_b)

        @pl.when(i == last)
        def _():
            gw_ref[...] = acc_ref[...].astype(BF16)

    kv = pl.BlockSpec((MEM_LEN, WIDTH), lambda i: (i // per_b, 0))
    tile = pl.BlockSpec((tm, WIDTH), lambda i: (i, 0))
    n_mem = mk.shape[0]
    return pl.pallas_call(
        body, name="attn_bwd", grid=(t // tm,),
        in_specs=[pl.BlockSpec((tm, WIDTH), lambda i: (i, C_MQ // WIDTH)), kv, kv, tile,
                  pl.BlockSpec((tm, D_MODEL), lambda i: (i, 0))],
        out_specs=[tile, kv, kv, _resident((D_MODEL, WIDTH))],
        out_shape=[jax.ShapeDtypeStruct((t, WIDTH), BF16), jax.ShapeDtypeStruct((n_mem, WIDTH), F32),
                   jax.ShapeDtypeStruct((n_mem, WIDTH), F32), jax.ShapeDtypeStruct((D_MODEL, WIDTH), BF16)],
        scratch_shapes=[pltpu.VMEM((D_MODEL, WIDTH), F32)],
        compiler_params=_params("arbitrary"),
    )(proj, mk, mv, dy, x_b)


HALO = 16


def _shift_down(u, halo, k, row):
    out = pltpu.roll(u, k, 0)
    for m in range(k):
        out = jnp.where(row == m, halo[HALO - k + m:HALO - k + m + 1, :], out)
    return out


def _shift_up(u, halo, k, row, tm):
    out = pltpu.roll(u, tm - k, 0)
    for m in range(k):
        out = jnp.where(row == tm - k + m, halo[m:m + 1, :], out)
    return out


def _merge_fwd(proj, y_b, y_c, conv_w, w_branch, b_gate, *, tm, seq, carry=None):
    t = proj.shape[0]
    per_b = seq // tm
    hb = tm // HALO

    def body(cb_ref, cc_ref, ch_ref, cch_ref, chh_ref, ga_ref, gb_ref, gc_ref, yb_ref, yc_ref, cw_ref, wb_ref,
             bg_ref, ya_ref, pa_ref, pb_ref, pc_ref, mg_ref, sa_ref, sb_ref, sc_ref):
        i = pl.program_id(0)
        row = lax.broadcasted_iota(jnp.int32, (tm, WIDTH), 0)
        u = cc_ref[...].astype(F32) * ch_ref[...].astype(F32)
        halo = jnp.where(i % per_b == 0, 0.0, cch_ref[...].astype(F32) * chh_ref[...].astype(F32))
        cw = cw_ref[...]
        y = cw[0:1] * _shift_down(u, halo, 2, row) + cw[1:2] * _shift_down(u, halo, 1, row) + cw[2:3] * u
        ya_b = (cb_ref[...].astype(F32) * y).astype(BF16)
        ya_ref[...] = ya_b
        merged = None
        for idx, (y_in, g_ref, p_ref, s_ref) in enumerate(((ya_b, ga_ref, pa_ref, sa_ref),
                                                            (yb_ref[...], gb_ref, pb_ref, sb_ref),
                                                            (yc_ref[...], gc_ref, pc_ref, sc_ref))):
            p = _dot(y_in, wb_ref[idx])
            p_ref[...] = p.astype(BF16)
            sg = _sigmoid(g_ref[...].astype(F32) + bg_ref[:, idx * D_MODEL:(idx + 1) * D_MODEL])
            s_ref[...] = sg.astype(BF16)
            term = sg * p
            merged = term if merged is None else merged + term
        mg_ref[...] = merged.astype(BF16)

    def half(c):
        return pl.BlockSpec((tm, WIDTH), lambda i: (i, c // WIDTH))

    def prev(c):
        return pl.BlockSpec((HALO, WIDTH), lambda i: (jnp.maximum(i * hb - 1, 0), c // WIDTH))

    def gate(k):
        return pl.BlockSpec((tm, D_MODEL), lambda i: (i, C_GA // D_MODEL + k))

    tile512 = pl.BlockSpec((tm, WIDTH), lambda i: (i, 0))
    tile1k = pl.BlockSpec((tm, D_MODEL), lambda i: (i, 0))
    outs, recv = _call(
        body, "merge_fwd", (t // tm,),
        [half(C_CB), half(C_CC), half(C_CH), prev(C_CC), prev(C_CH), gate(0), gate(1), gate(2),
         tile512, tile512, _resident((CONV_K, WIDTH)), _resident((3, WIDTH, D_MODEL)), _resident((1, 3 * D_MODEL))],
        [tile512] + [tile1k] * 7,
        [jax.ShapeDtypeStruct((t, WIDTH), BF16)] + [jax.ShapeDtypeStruct((t, D_MODEL), BF16)] * 7,
        (proj, proj, proj, proj, proj, proj, proj, proj, y_b, y_c, conv_w, w_branch, b_gate),
        sem=("parallel",), carry=carry)
    return outs if carry is None else (outs, recv)


def _merge_bwd(dmerged, projections, gates, branch_in, w_branch, *, tm):
    t = dmerged.shape[0]
    last = t // tm - 1

    def body(dm_ref, pa_ref, pb_ref, pc_ref, sa_ref, sb_ref, sc_ref, ya_ref, yb_ref, yc_ref, wb_ref,
             dgt_ref, dya_ref, dyb_ref, dyc_ref, dbg_ref, gw_ref, acc_ref):
        i = pl.program_id(0)

        @pl.when(i == 0)
        def _():
            dbg_ref[...] = jnp.zeros_like(dbg_ref)
            acc_ref[...] = jnp.zeros_like(acc_ref)

        dm = dm_ref[...].astype(F32)
        for idx, (p_ref, s_ref, y_ref, dy_ref) in enumerate(((pa_ref, sa_ref, ya_ref, dya_ref),
                                                             (pb_ref, sb_ref, yb_ref, dyb_ref),
                                                             (pc_ref, sc_ref, yc_ref, dyc_ref))):
            cols = slice(idx * D_MODEL, (idx + 1) * D_MODEL)
            sg = s_ref[...].astype(F32)
            dp = dm * sg
            dp_b = dp.astype(BF16)
            dgate = dp * p_ref[...].astype(F32) * (1.0 - sg)
            dgt_ref[:, cols] = dgate.astype(BF16)
            dbg_ref[:, cols] += jnp.sum(dgate, axis=0, keepdims=True)
            dy_ref[...] = _dot_nt(dp_b, wb_ref[idx]).astype(BF16)
            acc_ref[idx] += _dot_tn(y_ref[...], dp_b)

        @pl.when(i == last)
        def _():
            gw_ref[...] = acc_ref[...].astype(BF16)

    tile512 = pl.BlockSpec((tm, WIDTH), lambda i: (i, 0))
    tile1k = pl.BlockSpec((tm, D_MODEL), lambda i: (i, 0))
    return pl.pallas_call(
        body, name="merge_bwd", grid=(t // tm,),
        in_specs=[tile1k] * 7 + [tile512] * 3 + [_resident((3, WIDTH, D_MODEL))],
        out_specs=[pl.BlockSpec((tm, 3 * D_MODEL), lambda i: (i, 0)), tile512, tile512, tile512,
                   _resident((1, 3 * D_MODEL)), _resident((3, WIDTH, D_MODEL))],
        out_shape=[jax.ShapeDtypeStruct((t, 3 * D_MODEL), BF16)] + [jax.ShapeDtypeStruct((t, WIDTH), BF16)] * 3
                  + [jax.ShapeDtypeStruct((1, 3 * D_MODEL), F32), jax.ShapeDtypeStruct((3, WIDTH, D_MODEL), BF16)],
        scratch_shapes=[pltpu.VMEM((3, WIDTH, D_MODEL), F32)],
        compiler_params=_params("arbitrary"),
    )(dmerged, *projections, *gates, *branch_in, w_branch)


def _conv_bwd(proj, dya, conv_w, x_b, *, tm, seq):
    t = proj.shape[0]
    per_b = seq // tm
    hb = tm // HALO
    last_blk = t // HALO - 1
    last = t // tm - 1

    def body(cb_ref, cc_ref, ch_ref, cch_ref, chh_ref, dya_ref, cbn_ref, dyan_ref, cw_ref, x_ref,
             d_ref, dcw_ref, gw_ref, acc_ref):
        i = pl.program_id(0)

        @pl.when(i == 0)
        def _():
            dcw_ref[...] = jnp.zeros_like(dcw_ref)
            acc_ref[...] = jnp.zeros_like(acc_ref)

        row = lax.broadcasted_iota(jnp.int32, (tm, WIDTH), 0)
        cb, cc, ch = cb_ref[...].astype(F32), cc_ref[...].astype(F32), ch_ref[...].astype(F32)
        u = cc * ch
        halo = jnp.where(i % per_b == 0, 0.0, cch_ref[...].astype(F32) * chh_ref[...].astype(F32))
        u1 = _shift_down(u, halo, 1, row)
        u2 = _shift_down(u, halo, 2, row)
        cw = cw_ref[...]
        y = cw[0:1] * u2 + cw[1:2] * u1 + cw[2:3] * u
        dya = dya_ref[...].astype(F32)
        dy = dya * cb
        nxt = jnp.where(i % per_b == per_b - 1, 0.0, dyan_ref[...].astype(F32) * cbn_ref[...].astype(F32))
        du = cw[2:3] * dy + cw[1:2] * _shift_up(dy, nxt, 1, row, tm) + cw[0:1] * _shift_up(dy, nxt, 2, row, tm)
        d_ref[:, 0:WIDTH] = (dya * y).astype(BF16)
        d_ref[:, WIDTH:2 * WIDTH] = (du * ch).astype(BF16)
        d_ref[:, 2 * WIDTH:3 * WIDTH] = (du * cc).astype(BF16)
        dcw_ref[0:1, :] += jnp.sum(dy * u2, axis=0, keepdims=True)
        dcw_ref[1:2, :] += jnp.sum(dy * u1, axis=0, keepdims=True)
        dcw_ref[2:3, :] += jnp.sum(dy * u, axis=0, keepdims=True)
        acc_ref[...] += _dot_tn(x_ref[...], d_ref[...])

        @pl.when(i == last)
        def _():
            gw_ref[...] = acc_ref[...].astype(BF16)

    def half(c):
        return pl.BlockSpec((tm, WIDTH), lambda i: (i, c // WIDTH))

    def prev(c):
        return pl.BlockSpec((HALO, WIDTH), lambda i: (jnp.maximum(i * hb - 1, 0), c // WIDTH))

    def nxt(c):
        return pl.BlockSpec((HALO, WIDTH), lambda i: (jnp.minimum((i + 1) * hb, last_blk), c // WIDTH))

    return pl.pallas_call(
        body, name="conv_bwd", grid=(t // tm,),
        in_specs=[half(C_CB), half(C_CC), half(C_CH), prev(C_CC), prev(C_CH),
                  pl.BlockSpec((tm, WIDTH), lambda i: (i, 0)), nxt(C_CB), nxt(0), _resident((CONV_K, WIDTH)),
                  pl.BlockSpec((tm, D_MODEL), lambda i: (i, 0))],
        out_specs=[pl.BlockSpec((tm, 3 * WIDTH), lambda i: (i, 0)), _resident((CONV_K, WIDTH)),
                   _resident((D_MODEL, 3 * WIDTH))],
        out_shape=[jax.ShapeDtypeStruct((t, 3 * WIDTH), BF16), jax.ShapeDtypeStruct((CONV_K, WIDTH), F32),
                   jax.ShapeDtypeStruct((D_MODEL, 3 * WIDTH), BF16)],
        scratch_shapes=[pltpu.VMEM((D_MODEL, 3 * WIDTH), F32)],
        compiler_params=_params("arbitrary"),
    )(proj, proj, proj, proj, proj, dya, proj, dya, conv_w, x_b)


def _loss_head(y, target, *, tm):
    t = y.shape[0]

    def body(y_ref, t_ref, dy_ref, l_ref):
        @pl.when(pl.program_id(0) == 0)
        def _():
            l_ref[...] = jnp.zeros_like(l_ref)

        err = y_ref[...] - t_ref[...]
        dy_ref[...] = err * (1.0 / D_MODEL)
        per_row = jnp.sum(err * err, axis=-1, keepdims=True) * (1.0 / D_MODEL)
        l_ref[...] += 0.5 * jnp.sum(per_row, axis=0, keepdims=True)

    row = pl.BlockSpec((tm, D_MODEL), lambda i: (i, 0))
    return pl.pallas_call(
        body, name="loss_head", grid=(t // tm,),
        in_specs=[row, row], out_specs=[row, _resident((8, 128))],
        out_shape=[jax.ShapeDtypeStruct((t, D_MODEL), F32), jax.ShapeDtypeStruct((8, 128), F32)],
        compiler_params=_params("arbitrary"),
    )(y, target)


def _lb_softmax(lower_bounds):
    x = lower_bounds
    e = jnp.exp(x - jnp.max(x, axis=0, keepdims=True))
    return e / jnp.sum(e, axis=0, keepdims=True)


def _lb_fwd(lower_bounds):
    def body(x_ref, o_ref):
        s = _lb_softmax(x_ref[...])
        c = s[0:1]
        o_ref[0:1, :] = c - s[0:1]
        for l in range(1, DEPTH):
            c = c + s[l:l + 1]
            o_ref[l:l + 1, :] = c - s[0:1]

    return pl.pallas_call(body, name="lb_fwd", out_shape=jax.ShapeDtypeStruct(lower_bounds.shape, F32))(lower_bounds)


def _lb_bwd(lower_bounds, d_lb_all):
    def body(x_ref, d_ref, o_ref):
        s = _lb_softmax(x_ref[...])
        d = d_ref[...]
        rows = [jnp.zeros_like(d[0:1])]
        for j in range(1, DEPTH):
            acc = d[j:j + 1]
            for l in range(j + 1, DEPTH):
                acc = acc + d[l:l + 1]
            rows.append(acc)
        inner = rows[0] * s[0:1]
        for j in range(1, DEPTH):
            inner = inner + rows[j] * s[j:j + 1]
        for j in range(DEPTH):
            o_ref[j:j + 1, :] = s[j:j + 1] * (rows[j] - inner)

    return pl.pallas_call(body, name="lb_bwd", out_shape=jax.ShapeDtypeStruct(lower_bounds.shape, F32))(
        lower_bounds, d_lb_all)


def _adamw(w, g, m, v):
    m2 = ADAM_B1 * m + (1.0 - ADAM_B1) * g
    v2 = ADAM_B2 * v + (1.0 - ADAM_B2) * (g * g)
    m_hat = m2 / (1.0 - ADAM_B1 ** ADAM_STEP)
    v_hat = v2 / (1.0 - ADAM_B2 ** ADAM_STEP)
    delta = -ADAM_LR * (m_hat / (jnp.sqrt(v_hat) + ADAM_EPS) + ADAM_WD * w)
    return delta, m2, v2


def _adam_small(name, g, w, m, v):
    shape = w.shape
    flat = (-1, shape[-1])
    g2, w2, m2, v2 = (a.reshape(flat) for a in (g, w, m, v))

    def body(g_ref, w_ref, m_ref, v_ref, d_ref, mo_ref, vo_ref):
        d, mm, vv = _adamw(w_ref[...], g_ref[...], m_ref[...], v_ref[...])
        d_ref[...] = d
        mo_ref[...] = mm
        vo_ref[...] = vv

    outs = pl.pallas_call(body, name=name, out_shape=[jax.ShapeDtypeStruct(w2.shape, F32)] * 3)(g2, w2, m2, v2)
    return [o.reshape(shape) for o in outs]


def _adam_shard(name, recvs, w, m, v, *, tr):
    _, r, c = w.shape

    def body(*refs):
        rc, (w_ref, m_ref, v_ref), (g_ref, d_ref, mo_ref, vo_ref) = refs[:DEPTH], refs[DEPTH:DEPTH + 3], refs[DEPTH + 3:]
        layer = pl.program_id(0)
        for cand in range(DEPTH):
            @pl.when(layer == cand)
            def _():
                g = rc[cand][0].astype(F32)
                for d in range(1, N_DEV):
                    g = g + rc[cand][d].astype(F32)
                dl, mm, vv = _adamw(w_ref[...], g, m_ref[...], v_ref[...])
                g_ref[...] = g
                d_ref[...] = dl
                mo_ref[...] = mm
                vo_ref[...] = vv

    def recv_spec(cand):
        return pl.BlockSpec((N_DEV, tr, c), lambda l, i: (0, jnp.where(l == cand, i, 0), 0))

    tile = pl.BlockSpec((None, tr, c), lambda l, i: (l, i, 0))
    return pl.pallas_call(
        body, name=name, grid=(DEPTH, r // tr),
        in_specs=[recv_spec(cand) for cand in range(DEPTH)] + [tile] * 3,
        out_specs=[tile] * 4,
        out_shape=[jax.ShapeDtypeStruct(w.shape, F32)] * 4,
        compiler_params=_params("parallel", "parallel"),
    )(*recvs, w, m, v)


def _sum_devices(name, x):
    def body(x_ref, o_ref):
        acc = x_ref[0]
        for d in range(1, N_DEV):
            acc = acc + x_ref[d]
        o_ref[...] = acc

    return pl.pallas_call(body, name=name, out_shape=jax.ShapeDtypeStruct(x.shape[1:], x.dtype))(x)


SMALL = (("lower_bounds", 1, 512), ("conv_w", CONV_K, WIDTH), ("hg_norm_w", 1, HEAD_DIM), ("b_gate", 3, D_MODEL),
         ("ln1_g", 1, D_MODEL), ("ln1_b", 1, D_MODEL), ("ln2_g", 1, D_MODEL), ("ln2_b", 1, D_MODEL))
SMALL_ROWS = sum(r for _, r, _ in SMALL)


def _pack_small(per_layer):
    flat = [a for layer in per_layer for a in layer]

    def body(*refs):
        ins, o_ref = refs[:-1], refs[-1]
        o_ref[...] = jnp.zeros_like(o_ref)
        it = iter(ins)
        for l in range(DEPTH):
            row = l * SMALL_ROWS
            for name, nrows, ncols in SMALL:
                ref = next(it)
                if name == "b_gate":
                    for k in range(nrows):
                        o_ref[row + k:row + k + 1, :] = ref[:, k * ncols:(k + 1) * ncols]
                else:
                    o_ref[row:row + nrows, 0:ncols] = ref[...]
                row += nrows

    return pl.pallas_call(body, name="pack_small_grads",
                          out_shape=jax.ShapeDtypeStruct((DEPTH * SMALL_ROWS, D_MODEL), F32))(*flat)


def _unpack_small(summed):
    s3 = summed.reshape(DEPTH, SMALL_ROWS, D_MODEL)
    out, row = {}, 0
    for name, nrows, ncols in SMALL:
        out[name] = s3[:, row:row + nrows, :ncols].reshape(DEPTH, nrows * ncols)
        row += nrows
    return out


def _natural_cols(g):
    nd = g.ndim
    perm = tuple(range(1, nd - 1)) + (0, nd - 1)
    t = jnp.transpose(g, perm)
    return t.reshape(t.shape[:-2] + (t.shape[-2] * t.shape[-1],))


def _natural_rows(g):
    return g.reshape(g.shape[0] * g.shape[1], g.shape[2])


def _hosted(hosts, key, fn):
    pairs = hosts.get(key) if hosts else None
    if callable(pairs):
        pairs = pairs()
    if not pairs:
        return fn(None)
    outs, recvs = fn([ex for ex, _ in pairs])
    for (_, hook), recv in zip(pairs, recvs):
        hook(recv)
    return outs


def _layer_fwd(cur, cur_b, mem2d, wl, *, bsz, seq, hosts=None):
    tm = min(512, seq)
    proj, hf = _hosted(hosts, "in_proj", lambda c: _in_proj(cur_b, wl["w_in"], tm=min(1024, seq), carry=c))
    y_b, o_pre, states = _hosted(hosts, "hgrn_fwd", lambda c: _hgrn_fwd(proj, hf, wl["lb"], wl["nw"], bsz=bsz,
                                                                         seq=seq, carry=c))
    mk, mv = _mem_kv(mem2d, wl["w_mk"], wl["w_mv"])
    y_c = _attn_fwd(proj, mk, mv, tm=tm, seq=seq)
    y_a, pa, pb, pc, merged, sga, sgb, sgc = _hosted(
        hosts, "merge_fwd", lambda c: _merge_fwd(proj, y_b, y_c, wl["conv"], wl["w_br"], wl["b_gate"], tm=tm,
                                                 seq=seq, carry=c))
    z1, x1, x1_b = _hosted(hosts, "wo_ln", lambda c: _linear_ln("wo_ln", merged, wl["w_o"], cur, wl["ln1_g"],
                                                                  wl["ln1_b"], tm=tm, carry=c))
    hid = _hosted(hosts, "mlp_up", lambda c: _mm_nn("mlp_up", x1_b, wl["w_up"], tm=min(1024, seq), tn=2048,
                                                     out_dtype=BF16, relu2=True, carry=c))
    z2, x2, x2_b = _linear_ln("down_ln", hid, wl["w_down"], x1, wl["ln2_g"], wl["ln2_b"], tm=tm)
    return dict(x_b=cur_b, proj=proj, hf=hf, y_a=y_a, y_b=y_b, y_c=y_c, o_pre=o_pre, states=states, mk=mk, mv=mv,
                proj3=(pa, pb, pc), gates3=(sga, sgb, sgc), merged=merged, z1=z1, x1_b=x1_b, hid=hid, z2=z2, x2=x2,
                x2_b=x2_b)


def _layer_bwd(dcur, mem2d, s, wl, *, bsz, seq, plan=None):
    tm = min(512, seq)
    tk = min(2048, bsz * seq)
    g = {}

    def run(key, fn):
        made = plan[key](g) if plan and key in plan else None
        return _hosted({key: [made]} if made else None, key, fn)

    dz2, dz2_b, dhpre, d_ln2g, d_ln2b = run(
        "ln2_bwd_down", lambda c: _ln_bwd_mm_nt("ln2_bwd_down", dcur, s["z2"], wl["ln2_g"], wl["w_down"],
                                                s["hid"], tm=tm, tn=1024, carry=c))
    g["w_down"] = _mm_tn("grad_w_down", s["hid"], dz2_b, tk=tk, tmo=1024, tno=1024)
    dx1 = _mm_nt_sum("mlp_up_bwd", [dhpre], [0], wl["w_up"], dz2, tm=tm)
    g["w_up"] = run("grad_w_up", lambda c: _mm_tn("grad_w_up", s["x1_b"], dhpre, tk=tk, tmo=1024, tno=2048, carry=c))
    dz1, dz1_b, dmerged, d_ln1g, d_ln1b, g["w_o"] = _ln_bwd_mm_nt("ln1_bwd_wo", dx1, s["z1"], wl["ln1_g"], wl["w_o"],
                                                                  left=s["merged"], tm=tm, tn=1024)
    dgate, dya, dyb, dyc, d_bg, g["w_br"] = _merge_bwd(dmerged, s["proj3"], s["gates3"],
                                                       (s["y_a"], s["y_b"], s["y_c"]), wl["w_br"], tm=tm)
    d_conv, d_cw, gw_conv = _conv_bwd(s["proj"], dya, wl["conv"], s["x_b"], tm=tm, seq=seq)
    dhg, d_lb, d_nw = run(
        "hgrn_bwd", lambda c: _hgrn_bwd(s["proj"], s["hf"], wl["lb"], wl["nw"], s["o_pre"], s["states"], dyb,
                                        bsz=bsz, seq=seq, carry=c))
    dmq, dmk, dmv, gw_mq = _attn_bwd(s["proj"], s["mk"], s["mv"], dyc, s["x_b"], tm=tm, seq=seq)
    tkm = min(512, mem2d.shape[0])
    g["w_mk"] = _mm_tn("grad_w_mem", mem2d, dmk, tk=tkm, tmo=1024, tno=512)
    g["w_mv"] = _mm_tn("grad_w_mem", mem2d, dmv, tk=tkm, tmo=1024, tno=512)
    pieces = [d_conv, dhg, dmq, dgate]
    offsets = [C_CB, C_HQ, C_MQ, C_GA]
    g["w_in"] = jnp.concatenate(
        [gw_conv, _mm_tn("grad_w_in_hgrn", s["x_b"], dhg, tk=tk, tmo=1024, tno=2048), gw_mq,
         _mm_tn("grad_w_in_gates", s["x_b"], dgate, tk=tk, tmo=1024, tno=1536)], axis=1)
    dx = run("in_proj_bwd", lambda c: _mm_nt_sum("in_proj_bwd", pieces, offsets, wl["w_in"], dz1,
                                                 tm=min(512, seq), carry=c))
    return dx, g, [d_lb, d_cw, d_nw, d_bg, d_ln1g, d_ln1b, d_ln2g, d_ln2b]


def kernel(x, mem, lower_bounds, w_in, conv_w, hg_norm_w, w_mem_k, w_mem_v, w_branch, b_gate, w_o, ln1_g, ln1_b, w_up, w_down, ln2_g, ln2_b, loss_target, m_lower_bounds, m_w_in, m_conv_w, m_hg_norm_w, m_w_mem_k, m_w_mem_v, m_w_branch, m_b_gate, m_w_o, m_ln1_g, m_ln1_b, m_w_up, m_w_down, m_ln2_g, m_ln2_b, v_lower_bounds, v_w_in, v_conv_w, v_hg_norm_w, v_w_mem_k, v_w_mem_v, v_w_branch, v_b_gate, v_w_o, v_ln1_g, v_ln1_b, v_w_up, v_w_down, v_ln2_g, v_ln2_b):
    bsz, seq, _ = x.shape
    t = bsz * seq
    me = _my_id()

    sh = dict(w_in=w_in.astype(BF16), w_mk=w_mem_k.astype(BF16), w_mv=w_mem_v.astype(BF16),
              w_br=w_branch.astype(BF16), w_o=w_o.astype(BF16), w_up=w_up.astype(BF16), w_down=w_down.astype(BF16))
    half_rows = D_MODEL // 2
    sh["w_in_a"], sh["w_in_b"] = sh["w_in"][:, :half_rows], sh["w_in"][:, half_rows:]
    natural = dict(w_in=_natural_cols, w_in_a=_natural_cols, w_in_b=_natural_cols, w_mk=_natural_rows,
                   w_mv=_natural_rows, w_br=_natural_cols, w_o=_natural_rows, w_up=_natural_cols,
                   w_down=_natural_rows)

    lb_all = _lb_fwd(lower_bounds)
    layer_w = [dict(lb=lb_all[l][None], nw=hg_norm_w[l][None], b_gate=b_gate[l][None], ln1_g=ln1_g[l][None],
                    ln1_b=ln1_b[l][None], ln2_g=ln2_g[l][None], ln2_b=ln2_b[l][None]) for l in range(DEPTH)]
    half_full = {}

    def near(names, l):
        srcs = [sh[n][l] for n in names]
        ex = _Exchange(srcs, piece_shapes=[s_.shape for s_ in srcs], route="near")
        return ex, lambda recv_: half_full.update({(n, l): r for n, r in zip(names, recv_)})

    def relay(names, l):
        ex = _Exchange([half_full.pop((n, l)) for n in names], route="relay")

        def hook(recv_):
            for n, r in zip(names, recv_):
                layer_w[l][n] = natural[n](r)
        return ex, hook

    small4 = ["w_mk", "w_mv", "w_br", "w_o"]
    conv_shard = conv_w.reshape(DEPTH * CONV_K * (WIDTH // N_DEV) // 128, 128)
    first = near(["w_in"], 0)
    conv_ex = _Exchange([conv_shard], piece_shapes=[conv_shard.shape])
    got = _exchange("gather_first", [first[0], conv_ex])
    first[1](got[0])
    conv_full = _natural_cols(got[1][0].reshape(N_DEV, DEPTH, CONV_K, WIDTH // N_DEV))
    second = relay(["w_in"], 0)
    second[1](_exchange("relay_first", [second[0]])[0])

    x2d = x.reshape(t, D_MODEL)
    mem2d = mem.reshape(bsz * MEM_LEN, D_MODEL)
    target2d = loss_target.reshape(t, D_MODEL)

    saved = []
    cur, cur_b = x2d, x2d.astype(BF16)
    for l in range(DEPTH):
        wl = layer_w[l]
        wl["conv"] = conv_full[l]
        more = l + 1 < DEPTH
        now = ["w_up", "w_down"] + ([] if l else small4)
        hosts = {"in_proj": [near(now, l)],
                 "hgrn_fwd": lambda l=l, more=more, now=now: [relay(now, l)] + ([near(["w_in"], l + 1)] if more else [])}
        if more:
            hosts["merge_fwd"] = lambda l=l: [relay(["w_in"], l + 1), near(small4, l + 1)]
            hosts["mlp_up"] = lambda l=l: [relay(small4, l + 1)]
        s = _layer_fwd(cur, cur_b, mem2d, wl, bsz=bsz, seq=seq, hosts=hosts)
        saved.append(s)
        cur, cur_b = s["x2"], s["x2_b"]

    dcur, loss_tile = _loss_head(cur, target2d, tm=min(512, seq))
    loss = lax.psum(loss_tile[0, 0], ("x", "y", "c"))

    in_w = IN_COLS // N_DEV

    def in_half(r):
        return lambda ref, j: ref.at[pl.ds(r * half_rows, half_rows), pl.ds(j * in_w, in_w)]

    slicer = dict(w_in_a=in_half(0), w_in_b=in_half(1), w_mk=_rows(D_MODEL // N_DEV), w_mv=_rows(D_MODEL // N_DEV),
                  w_br=_cols(D_MODEL // N_DEV), w_o=_rows(D_MODEL // N_DEV), w_up=_cols(D_FF // N_DEV),
                  w_down=_rows(D_FF // N_DEV))
    source = dict(w_in_a="w_in", w_in_b="w_in")
    recv = [dict() for _ in range(DEPTH)]

    def scatter_of(names, g, into):
        ex = _Exchange([g[source.get(n, n)] for n in names], [slicer[n] for n in names],
                       [sh[n].shape[1:] for n in names])
        return ex, lambda recv_: into.update(zip(names, recv_))

    small_rows = [None] * DEPTH
    prev = None
    rest = ["w_in_b", "w_mk", "w_mv"]
    for l in reversed(range(DEPTH)):
        plan = {"grad_w_up": lambda g, l=l: scatter_of(["w_down"], g, recv[l]),
                "hgrn_bwd": lambda g, l=l: scatter_of(["w_up", "w_o", "w_br"], g, recv[l])}
        if l == 0:
            plan["in_proj_bwd"] = lambda g: scatter_of(["w_in_a"] + rest, g, recv[0])
        else:
            plan["in_proj_bwd"] = lambda g, l=l: scatter_of(["w_in_a"], g, recv[l])
        if prev is not None:
            plan["ln2_bwd_down"] = lambda g, l=l, prev=prev: scatter_of(rest, prev, recv[l + 1])
        dcur, prev, small_rows[l] = _layer_bwd(dcur, mem2d, saved[l], layer_w[l], bsz=bsz, seq=seq, plan=plan)
    for r in recv:
        r["w_in"] = jnp.concatenate([r.pop("w_in_a"), r.pop("w_in_b")], axis=1)

    packed = _pack_small(small_rows)
    all_small = _exchange("gather_small_grads", [_Exchange([packed], piece_shapes=[packed.shape])])[0][0]
    small_grads = _unpack_small(_sum_devices("sum_small_grads", all_small))
    small_grads["lower_bounds"] = _lb_bwd(lower_bounds, small_grads["lower_bounds"])
    conv_all = small_grads["conv_w"].reshape(DEPTH, CONV_K, WIDTH)
    small_grads["conv_w"] = lax.dynamic_slice_in_dim(conv_all, me * (WIDTH // N_DEV), WIDTH // N_DEV, axis=2)

    grads, deltas, new_m, new_v = {}, {}, {}, {}
    given = dict(lower_bounds=(lower_bounds, m_lower_bounds, v_lower_bounds), conv_w=(conv_w, m_conv_w, v_conv_w),
                 hg_norm_w=(hg_norm_w, m_hg_norm_w, v_hg_norm_w), b_gate=(b_gate, m_b_gate, v_b_gate),
                 ln1_g=(ln1_g, m_ln1_g, v_ln1_g), ln1_b=(ln1_b, m_ln1_b, v_ln1_b),
                 ln2_g=(ln2_g, m_ln2_g, v_ln2_g), ln2_b=(ln2_b, m_ln2_b, v_ln2_b))
    for name, (w_, m_, v_) in given.items():
        g_ = small_grads[name].reshape(w_.shape)
        grads[name] = g_
        deltas[name], new_m[name], new_v[name] = _adam_small("adam_" + name, g_, w_, m_, v_)

    big = dict(w_in=("w_in", w_in, m_w_in, v_w_in, 128), w_mem_k=("w_mk", w_mem_k, m_w_mem_k, v_w_mem_k, 128),
               w_mem_v=("w_mv", w_mem_v, m_w_mem_v, v_w_mem_v, 128),
               w_branch=("w_br", w_branch, m_w_branch, v_w_branch, 512), w_o=("w_o", w_o, m_w_o, v_w_o, 128),
               w_up=("w_up", w_up, m_w_up, v_w_up, 256), w_down=("w_down", w_down, m_w_down, v_w_down, 128))
    for name, (k, w_, m_, v_, tr) in big.items():
        shape = w_.shape
        flat = (DEPTH, -1, shape[-1])
        rc = [recv[l][k].reshape((N_DEV,) + w_.reshape(flat).shape[1:]) for l in range(DEPTH)]
        outs = _adam_shard("adam_" + name, rc, w_.reshape(flat), m_.reshape(flat), v_.reshape(flat), tr=tr)
        grads[name], deltas[name], new_m[name], new_v[name] = (o.reshape(shape) for o in outs)

    order = ["lower_bounds", "w_in", "conv_w", "hg_norm_w", "w_mem_k", "w_mem_v", "w_branch", "b_gate", "w_o",
             "ln1_g", "ln1_b", "w_up", "w_down", "ln2_g", "ln2_b"]
    return (loss, dcur.reshape(x.shape), *[grads[n] for n in order], *[deltas[n] for n in order],
            *[new_m[n] for n in order], *[new_v[n] for n in order])
```

```python
import functools

import jax
import jax.numpy as jnp
from jax import lax
from jax.experimental import pallas as pl
from jax.experimental.pallas import tpu as pltpu

F32 = jnp.float32
BF16 = jnp.bfloat16

N_DEV = 8
D_MODEL = 1024
DEPTH = 4
MEM_LEN = 256
CONV_K = 3
WIDTH = 512
HEADS = 4
HEAD_DIM = 128
CHUNK = 32
D_FF = 4 * D_MODEL
IN_COLS = 7168
ALPHA = (2.0 * DEPTH) ** 0.25
LN_EPS = 1e-5
RMS_EPS = 1e-6
ADAM_LR = 0.001
ADAM_B1 = 0.9
ADAM_B2 = 0.999
ADAM_EPS = 1e-08
ADAM_WD = 0.01
ADAM_STEP = 10

C_CB, C_CC, C_CH, C_HQ, C_HF, C_HI, C_HG, C_MQ, C_GA = 0, 512, 1024, 1536, 2048, 2560, 3072, 3584, 4096

ROWS_HG = 256
NT_DIMS = (((1,), (1,)), ((), ()))
TN_DIMS = (((0,), (0,)), ((), ()))
MESH = pl.DeviceIdType.MESH


def _dot(a, b):
    return jnp.dot(a, b, preferred_element_type=F32)


def _dot_nt(a, b):
    return lax.dot_general(a, b, NT_DIMS, preferred_element_type=F32)


def _dot_tn(a, b):
    return lax.dot_general(a, b, TN_DIMS, preferred_element_type=F32)


def _sigmoid(x):
    return 1.0 / (1.0 + jnp.exp(-x))


def _params(*sem):
    return pltpu.CompilerParams(dimension_semantics=sem)


def _resident(shape, single=False):
    nd = len(shape)
    if single:
        return pl.BlockSpec(shape, lambda *_: (0,) * nd, pipeline_mode=pl.Buffered(1))
    return pl.BlockSpec(shape, lambda *_: (0,) * nd)


def _my_id():
    return 4 * lax.axis_index("x") + 2 * lax.axis_index("y") + lax.axis_index("c")


class _Exchange:
    def __init__(self, srcs, slicers=None, piece_shapes=None, route="all"):
        self.srcs, self.n, self.route = list(srcs), len(srcs), route
        self.slicers = list(slicers) if slicers else [_whole] * self.n
        any_spec = pl.BlockSpec(memory_space=pl.ANY)
        self.in_specs = [any_spec] * self.n
        self.out_specs = [any_spec] * self.n
        if route == "relay":
            self.out_shape = [jax.ShapeDtypeStruct(a.shape, a.dtype) for a in srcs]
        else:
            self.out_shape = [jax.ShapeDtypeStruct((N_DEV,) + tuple(s), a.dtype) for s, a in zip(piece_shapes, srcs)]
        self.aliased = route == "relay"
        self.scratch = [pltpu.SemaphoreType.DMA((self.n * N_DEV,)), pltpu.SemaphoreType.DMA((self.n * N_DEV,)),
                        pltpu.SemaphoreType.DMA((self.n,))]

    def _peer(self, j, me):
        if self.route == "all":
            return j != me
        return (j != me) & ((j % 2 == lax.axis_index("c")) | (j // 2 == me // 2))

    def _remote(self, ins, outs, sems, k, j, me):
        return pltpu.make_async_remote_copy(
            src_ref=self.slicers[k](ins[k], j), dst_ref=outs[k].at[me],
            send_sem=sems[0].at[k * N_DEV + j], recv_sem=sems[1].at[k * N_DEV + me],
            device_id=(j // 4, (j // 2) % 2, j % 2), device_id_type=MESH)

    def _local(self, ins, outs, sems, k, j, me):
        return pltpu.make_async_copy(self.slicers[k](ins[k], j), outs[k].at[me], sems[2].at[k])

    def _relay(self, outs, sems, k, j):
        sibling = (lax.axis_index("x"), lax.axis_index("y"), 1 - lax.axis_index("c"))
        return pltpu.make_async_remote_copy(
            src_ref=outs[k].at[j], dst_ref=outs[k].at[j], send_sem=sems[0].at[k * N_DEV + j],
            recv_sem=sems[1].at[k * N_DEV + j], device_id=sibling, device_id_type=MESH)

    def _other_chip(self, j, same_core):
        on_my_core = j % 2 == lax.axis_index("c")
        return (on_my_core if same_core else ~on_my_core) & (j // 2 != _my_id() // 2)

    def start(self, ins, outs, sems):
        me = _my_id()
        for k in range(self.n):
            for j in range(N_DEV):
                if self.route == "relay":
                    @pl.when(self._other_chip(j, True))
                    def _():
                        self._relay(outs, sems, k, j).start()
                    continue

                @pl.when(self._peer(j, me))
                def _():
                    self._remote(ins, outs, sems, k, j, me).start()

                @pl.when(j == me)
                def _():
                    self._local(ins, outs, sems, k, j, me).start()

    def wait(self, ins, outs, sems):
        me = _my_id()
        for k in range(self.n):
            for j in range(N_DEV):
                if self.route == "relay":
                    @pl.when(self._other_chip(j, False))
                    def _():
                        self._relay(outs, sems, k, j).wait_recv()

                    @pl.when(self._other_chip(j, True))
                    def _():
                        self._relay(outs, sems, k, j).wait_send()
                    continue

                @pl.when(self._peer(j, me))
                def _():
                    pltpu.make_async_remote_copy(
                        src_ref=self.slicers[k](ins[k], j), dst_ref=outs[k].at[j],
                        send_sem=sems[0].at[k * N_DEV + j], recv_sem=sems[1].at[k * N_DEV + j],
                        device_id=(j // 4, (j // 2) % 2, j % 2), device_id_type=MESH).wait_recv()
                    self._remote(ins, outs, sems, k, j, me).wait_send()

                @pl.when(j == me)
                def _():
                    self._local(ins, outs, sems, k, j, me).wait()


def _carried(exchanges, n_in, n_out):
    c_in = [s for ex in exchanges for s in ex.in_specs]
    c_out = [s for ex in exchanges for s in ex.out_specs]
    shapes = [s for ex in exchanges for s in ex.out_shape]
    sems = [s for ex in exchanges for s in ex.scratch]
    srcs = [a for ex in exchanges for a in ex.srcs]
    aliases, off = {}, 0
    for ex in exchanges:
        if ex.aliased:
            aliases.update({n_in + off + k: n_out + off + k for k in range(ex.n)})
        off += ex.n
    total = off

    def split(refs, n_scr):
        ins, outs = refs[:n_in], refs[n_in + total:n_in + total + n_out]
        rest = refs[n_in + 2 * total + n_out:]
        scr, sem_refs = rest[:n_scr], rest[n_scr:]
        parts, off_ = [], 0
        for i, ex in enumerate(exchanges):
            parts.append((refs[n_in + off_:n_in + off_ + ex.n],
                          refs[n_in + total + n_out + off_:n_in + total + n_out + off_ + ex.n],
                          sem_refs[3 * i:3 * i + 3]))
            off_ += ex.n
        return ins, outs, scr, parts

    return c_in, c_out, shapes, sems, srcs, aliases, split


def _exchange(name, exchanges):
    c_in, c_out, shapes, sems, srcs, aliases, split = _carried(exchanges, 0, 0)

    def body(*refs):
        _, _, _, parts = split(refs, 0)
        for ex, part in zip(exchanges, parts):
            ex.start(*part)
        for ex, part in zip(exchanges, parts):
            ex.wait(*part)

    outs = pl.pallas_call(
        body, name=name, in_specs=c_in, out_specs=c_out, out_shape=shapes, scratch_shapes=sems,
        input_output_aliases=aliases, compiler_params=pltpu.CompilerParams(has_side_effects=True))(*srcs)
    return _per_exchange(exchanges, outs)


def _per_exchange(exchanges, flat):
    out, off = [], 0
    for ex in exchanges:
        out.append(flat[off:off + ex.n])
        off += ex.n
    return out


def _call(body, name, grid, in_specs, out_specs, out_shape, args, scratch=(), sem=None, carry=None):
    n_in, n_out, n_scr = len(in_specs), len(out_specs), len(scratch)
    if not carry:
        outs = pl.pallas_call(body, name=name, grid=grid, in_specs=in_specs, out_specs=out_specs,
                              out_shape=out_shape, scratch_shapes=list(scratch),
                              compiler_params=_params(*sem))(*args)
        return outs, None
    c_in, c_out, shapes, sems, srcs, aliases, split = _carried(carry, n_in, n_out)

    def hosted(*refs):
        ins, outs, scr, parts = split(refs, n_scr)
        first, last = True, True
        for d, size in enumerate(grid):
            first = first & (pl.program_id(d) == 0)
            last = last & (pl.program_id(d) == size - 1)

        @pl.when(first)
        def _():
            for ex, part in zip(carry, parts):
                ex.start(*part)

        body(*ins, *outs, *scr)

        @pl.when(last)
        def _():
            for ex, part in zip(carry, parts):
                ex.wait(*part)

    outs = pl.pallas_call(
        hosted, name=name + "_x", grid=grid, in_specs=list(in_specs) + c_in,
        out_specs=list(out_specs) + c_out, out_shape=list(out_shape) + shapes,
        scratch_shapes=list(scratch) + sems, input_output_aliases=aliases,
        compiler_params=_params(*(["arbitrary"] * len(grid))))(*args, *srcs)
    return outs[:n_out], _per_exchange(carry, outs[n_out:])


def _whole(ref, j):
    return ref


def _cols(width):
    return lambda ref, j: ref.at[(slice(None),) * (len(ref.shape) - 1) + (pl.ds(j * width, width),)]


def _rows(height):
    return lambda ref, j: ref.at[pl.ds(j * height, height)]


def _mm_nn(name, a, w, *, tm, tn, out_dtype, relu2=False, carry=None):
    t, k = a.shape
    n = w.shape[1]

    def body(a_ref, w_ref, o_ref):
        acc = _dot(a_ref[...].astype(BF16), w_ref[...])
        if relu2:
            r = jnp.maximum(acc, 0.0)
            acc = r * r
        o_ref[...] = acc.astype(out_dtype)

    outs, recv = _call(
        body, name, (t // tm, n // tn),
        [pl.BlockSpec((tm, k), lambda i, j: (i, 0)), pl.BlockSpec((k, tn), lambda i, j: (0, j))],
        [pl.BlockSpec((tm, tn), lambda i, j: (i, j))], [jax.ShapeDtypeStruct((t, n), out_dtype)], (a, w),
        sem=("parallel", "parallel"), carry=carry)
    return outs[0] if carry is None else (outs[0], recv)


def _in_proj(a, w, *, tm, carry=None):
    t, k = a.shape
    tn = IN_COLS // 4
    f_tile, f_off = C_HF // tn, C_HF % tn

    def body(a_ref, w_ref, o_ref, f_ref):
        acc = _dot(a_ref[...], w_ref[...])
        o_ref[...] = acc.astype(BF16)

        @pl.when(pl.program_id(1) == f_tile)
        def _():
            f_ref[...] = acc[:, f_off:f_off + WIDTH]

    outs, recv = _call(
        body, "in_proj", (t // tm, IN_COLS // tn),
        [pl.BlockSpec((tm, k), lambda i, j: (i, 0)), pl.BlockSpec((k, tn), lambda i, j: (0, j))],
        [pl.BlockSpec((tm, tn), lambda i, j: (i, j)), pl.BlockSpec((tm, WIDTH), lambda i, j: (i, 0))],
        [jax.ShapeDtypeStruct((t, IN_COLS), BF16), jax.ShapeDtypeStruct((t, WIDTH), F32)], (a, w),
        sem=("parallel", "arbitrary"), carry=carry)
    return outs if carry is None else (outs, recv)


def _linear_ln(name, a, w, resid, g, b, *, tm, carry=None):
    t, k = a.shape
    halves = [slice(0, tm // 2), slice(tm // 2, tm)] if k > D_MODEL else [slice(0, tm)]

    def body(a_ref, w_ref, r_ref, g_ref, b_ref, z_ref, x_ref, xb_ref):
        z = ALPHA * _Lanes(r_ref[s, :] for s in halves) + _ldot(_Lanes(a_ref[s, :] for s in halves), w_ref[...])
        zc = z - _mean(z, axis=-1, keepdims=True)
        y = zc * _rsqrt(_mean(zc * zc, axis=-1, keepdims=True) + LN_EPS) * g_ref[...] + b_ref[...]
        for s, zz, yy in zip(halves, z.xs, y.xs):
            z_ref[s, :] = zz
            x_ref[s, :] = yy
            xb_ref[s, :] = yy.astype(BF16)

    row = pl.BlockSpec((tm, D_MODEL), lambda i: (i, 0))
    outs, recv = _call(
        body, name, (t // tm,),
        [pl.BlockSpec((tm, k), lambda i: (i, 0)), _resident((k, D_MODEL)), row,
         _resident((1, D_MODEL)), _resident((1, D_MODEL))],
        [row, row, row],
        [jax.ShapeDtypeStruct((t, D_MODEL), F32)] * 2 + [jax.ShapeDtypeStruct((t, D_MODEL), BF16)],
        (a, w, resid, g, b), sem=("parallel",), carry=carry)
    return outs if carry is None else (outs, recv)


def _ln_bwd_mm_nt(name, dy, z, g, w, h=None, left=None, *, tm, tn, carry=None):
    t = dy.shape[0]
    n = w.shape[0]
    halves = [slice(0, tm // 2), slice(tm // 2, tm)]
    last = t // tm - 1

    def body(*refs):
        refs = list(refs)
        dy_ref, z_ref, g_ref, w_ref = refs[:4]
        del refs[:4]
        h_ref = refs.pop(0) if h is not None else None
        left_ref = refs.pop(0) if left is not None else None
        dz_ref, dzb_ref, o_ref, dg_ref, db_ref = refs[:5]
        gw_ref, acc_ref = (refs[5], refs[6]) if left is not None else (None, None)

        @pl.when(pl.program_id(0) == 0)
        def _():
            dg_ref[...] = jnp.zeros_like(dg_ref)
            db_ref[...] = jnp.zeros_like(db_ref)
            if left is not None:
                acc_ref[...] = jnp.zeros_like(acc_ref)

        zv = _Lanes(z_ref[s, :] for s in halves)
        dyv = _Lanes(dy_ref[s, :] for s in halves)
        mu = _mean(zv, axis=-1, keepdims=True)
        zc = zv - mu
        rstd = _rsqrt(_mean(zc * zc, axis=-1, keepdims=True) + LN_EPS)
        xh = zc * rstd
        gdy = dyv * g_ref[...]
        m1 = _mean(gdy, axis=-1, keepdims=True)
        m2 = _mean(gdy * xh, axis=-1, keepdims=True)
        dz = rstd * (gdy - m1 - xh * m2)
        dz_b = dz.astype(BF16)
        for s, a, a_b in zip(halves, dz.xs, dz_b.xs):
            dz_ref[s, :] = a
            dzb_ref[s, :] = a_b
        dg_ref[...] += _sum(dyv * xh, axis=0, keepdims=True).total()
        db_ref[...] += _sum(dyv, axis=0, keepdims=True).total()
        for c in range(n // tn):
            cols = slice(c * tn, (c + 1) * tn)
            acc = _ldot_nt(dz_b, w_ref[cols, :])
            if h is not None:
                acc = acc * (2.0 * _sqrt(_Lanes(h_ref[s, cols] for s in halves).astype(F32)))
            for s, a in zip(halves, acc.xs):
                o_ref[s, cols] = a.astype(BF16)
        if left is not None:
            acc_ref[...] += _ldot_tn(_Lanes(left_ref[s, :] for s in halves), dz_b).total()

            @pl.when(pl.program_id(0) == last)
            def _():
                gw_ref[...] = acc_ref[...].astype(BF16)

    row = pl.BlockSpec((tm, D_MODEL), lambda i: (i, 0))
    vec = _resident((1, D_MODEL))
    tile = pl.BlockSpec((tm, n), lambda i: (i, 0))
    in_specs = [row, row, vec, _resident((n, D_MODEL))]
    args = [dy, z, g, w]
    out_specs = [row, row, tile, vec, vec]
    out_shape = [jax.ShapeDtypeStruct((t, D_MODEL), F32), jax.ShapeDtypeStruct((t, D_MODEL), BF16),
                 jax.ShapeDtypeStruct((t, n), BF16), jax.ShapeDtypeStruct((1, D_MODEL), F32),
                 jax.ShapeDtypeStruct((1, D_MODEL), F32)]
    scratch = []
    if h is not None:
        in_specs.append(tile)
        args.append(h)
    if left is not None:
        m = left.shape[1]
        in_specs.append(pl.BlockSpec((tm, m), lambda i: (i, 0)))
        args.append(left)
        out_specs.append(_resident((m, D_MODEL)))
        out_shape.append(jax.ShapeDtypeStruct((m, D_MODEL), BF16))
        scratch.append(pltpu.VMEM((m, D_MODEL), F32))
    outs, recv = _call(body, name, (t // tm,), in_specs, out_specs, out_shape, args, scratch=scratch,
                       sem=("arbitrary",), carry=carry)
    return outs if carry is None else (outs, recv)


def _mm_tn(name, a, b, *, tk, tmo, tno, carry=None):
    t, m = a.shape
    n = b.shape[1]
    nk = t // tk

    def body(a_ref, b_ref, o_ref, acc_ref):
        k = pl.program_id(2)
        p = _dot_tn(a_ref[...].astype(BF16), b_ref[...].astype(BF16))

        @pl.when(k == 0)
        def _():
            acc_ref[...] = p

        @pl.when(k > 0)
        def _():
            acc_ref[...] += p

        @pl.when(k == nk - 1)
        def _():
            o_ref[...] = acc_ref[...].astype(BF16)

    outs, recv = _call(
        body, name, (m // tmo, n // tno, nk),
        [pl.BlockSpec((tk, tmo), lambda i, j, k: (k, i)), pl.BlockSpec((tk, tno), lambda i, j, k: (k, j))],
        [pl.BlockSpec((tmo, tno), lambda i, j, k: (i, j))], [jax.ShapeDtypeStruct((m, n), BF16)], (a, b),
        scratch=[pltpu.VMEM((tmo, tno), F32)], sem=("parallel", "parallel", "arbitrary"), carry=carry)
    return outs[0] if carry is None else (outs[0], recv)


def _mm_nt_sum(name, pieces, offsets, w, resid, *, tm, carry=None):
    t = resid.shape[0]
    widths = [p.shape[1] for p in pieces]
    n_p = len(pieces)

    def body(*refs):
        p_refs, w_ref, r_ref, o_ref = refs[:n_p], refs[n_p], refs[n_p + 1], refs[n_p + 2]
        acc = ALPHA * r_ref[...]
        for p_ref, off, wd in zip(p_refs, offsets, widths):
            acc = acc + _dot_nt(p_ref[...], w_ref[:, off:off + wd])
        o_ref[...] = acc

    row = pl.BlockSpec((tm, D_MODEL), lambda i: (i, 0))
    outs, recv = _call(
        body, name, (t // tm,),
        [pl.BlockSpec((tm, wd), lambda i: (i, 0)) for wd in widths] + [_resident(w.shape, single=True), row],
        [row], [jax.ShapeDtypeStruct((t, D_MODEL), F32)], (*pieces, w, resid), sem=("parallel",), carry=carry)
    return outs[0] if carry is None else (outs[0], recv)


def _chunk_mask(rows):
    r = lax.broadcasted_iota(jnp.int32, (rows, rows), 0)
    c = lax.broadcasted_iota(jnp.int32, (rows, rows), 1)
    return ((r // CHUNK) == (c // CHUNK)) & (c <= r)


class _Lanes:
    def __init__(self, xs):
        self.xs = list(xs)

    def _with(self, other, f):
        if isinstance(other, _Lanes):
            return _Lanes([f(a, b) for a, b in zip(self.xs, other.xs)])
        return _Lanes([f(a, other) for a in self.xs])

    def __add__(self, o):
        return self._with(o, lambda a, b: a + b)

    def __radd__(self, o):
        return self._with(o, lambda a, b: b + a)

    def __sub__(self, o):
        return self._with(o, lambda a, b: a - b)

    def __rsub__(self, o):
        return self._with(o, lambda a, b: b - a)

    def __mul__(self, o):
        return self._with(o, lambda a, b: a * b)

    def __rmul__(self, o):
        return self._with(o, lambda a, b: b * a)

    def __truediv__(self, o):
        return self._with(o, lambda a, b: a / b)

    def __rtruediv__(self, o):
        return self._with(o, lambda a, b: b / a)

    def __neg__(self):
        return _Lanes([-a for a in self.xs])

    def __ge__(self, o):
        return self._with(o, lambda a, b: a >= b)

    def __getitem__(self, idx):
        return _Lanes([a[idx] for a in self.xs])

    def astype(self, dtype):
        return _Lanes([a.astype(dtype) for a in self.xs])

    def total(self):
        return functools.reduce(lambda a, b: a + b, self.xs)


def _lift(f):
    def g(*args, **kw):
        lanes = [a for a in args if isinstance(a, _Lanes)]
        if not lanes:
            return f(*args, **kw)
        return _Lanes([f(*[a.xs[i] if isinstance(a, _Lanes) else a for a in args], **kw)
                       for i in range(len(lanes[0].xs))])
    return g


def _concat(parts, axis):
    if isinstance(parts[0], _Lanes):
        return _Lanes([jnp.concatenate([p.xs[i] for p in parts], axis=axis) for i in range(len(parts[0].xs))])
    return jnp.concatenate(parts, axis=axis)


_exp, _log, _abs, _sqrt, _where = _lift(jnp.exp), _lift(jnp.log), _lift(jnp.abs), _lift(jnp.sqrt), _lift(jnp.where)
_sum, _mean, _rsqrt, _bcast = _lift(jnp.sum), _lift(jnp.mean), _lift(lax.rsqrt), _lift(jnp.broadcast_to)
_ldot, _ldot_nt, _ldot_tn = _lift(_dot), _lift(_dot_nt), _lift(_dot_tn)
_lsigmoid = _lift(_sigmoid)


def _mask_sum(mask_b, x, transpose=False):
    f = _ldot_tn if transpose else _ldot
    hi = x.astype(BF16)
    lo = (x - hi.astype(F32)).astype(BF16)
    return f(mask_b, hi) + f(mask_b, lo)


def _chunk_row(x, pos, rows):
    nc = rows // CHUNK

    def one(a):
        a3 = a.reshape(nc, CHUNK, HEAD_DIM)
        return jnp.broadcast_to(a3[:, pos:pos + 1, :], (nc, CHUNK, HEAD_DIM)).reshape(rows, HEAD_DIM)

    return _lift(one)(x)


def _chunk_total(x, rows):
    nc = rows // CHUNK

    def one(a):
        tot = jnp.sum(a.reshape(nc, CHUNK, HEAD_DIM), axis=1, keepdims=True)
        return jnp.broadcast_to(tot, (nc, CHUNK, HEAD_DIM)).reshape(rows, HEAD_DIM)

    return _lift(one)(x)


def _sigmoid_pair(x):
    e = _exp(-_abs(x))
    big = 1.0 / (1.0 + e)
    small = e * big
    pos = x >= 0.0
    return _where(pos, big, small), _where(pos, small, big)


def _hg_gates(q_raw, fl, lb, rows, mask):
    tri = mask.astype(BF16)
    sg, sg_neg = _sigmoid_pair(fl)
    forget = lb + (1.0 - lb) * sg
    k = (1.0 - lb) * sg_neg
    sq = _lsigmoid(q_raw)
    qs = q_raw * sq
    bc = _mask_sum(tri, _log(forget))
    bref = _chunk_row(bc, CHUNK // 2 - 1, rows)
    blast = _chunk_row(bc, CHUNK - 1, rows)
    return dict(tri=tri, sg=sg, sg_neg=sg_neg, forget=forget, k=k, sq=sq, qs=qs,
                e_a=_exp(bc - bref), e_b=_exp(bref - bc), e_q=_exp(bc), e_k=_exp(blast - bc),
                dec=_exp(blast))


HG_GROUP = 4


def _hg_lanes(bsz):
    return [(hh, slice(hh * HEAD_DIM, (hh + 1) * HEAD_DIM), b) for hh in range(HG_GROUP) for b in range(bsz)]


def _hg_read(ref, lanes):
    return _Lanes(ref[b, :, cs].astype(F32) for _, cs, b in lanes)


def _hg_write(ref, lanes, val, offset=0):
    for (_, cs, b), a in zip(lanes, val.xs):
        ref[b, :, offset + cs.start:offset + cs.stop] = a


def _hgrn_fwd(proj, hf, lb, nw, *, bsz, seq, carry=None):
    rows = min(ROWS_HG, seq)
    nt = seq // rows
    nc = rows // CHUNK
    t = bsz * seq

    lanes = _hg_lanes(bsz)

    def body(q_ref, f_ref, v_ref, g_ref, lb_ref, nw_ref, y_ref, o_ref, st_ref, s_scr):
        @pl.when(pl.program_id(1) == 0)
        def _():
            s_scr[...] = jnp.zeros_like(s_scr)

        mask = _chunk_mask(rows)
        lb_v = _Lanes(lb_ref[:, cs] for _, cs, _ in lanes)
        gt = _hg_gates(_hg_read(q_ref, lanes), _hg_read(f_ref, lanes), lb_v, rows, mask)
        v_b = _hg_read(v_ref, lanes).astype(BF16)
        a_b = (gt["qs"] * gt["e_a"]).astype(BF16)
        b_b = (gt["k"] * gt["e_b"]).astype(BF16)
        qi_b = (gt["qs"] * gt["e_q"]).astype(BF16)
        ko_b = (gt["k"] * gt["e_k"]).astype(BF16)
        scores = _where(mask, _ldot_nt(a_b, b_b), 0.0)
        o_intra = _ldot(scores.astype(BF16), v_b)

        s = _Lanes(s_scr[i] for i in range(len(lanes)))
        parts = []
        for n in range(nc):
            sl = slice(n * CHUNK, (n + 1) * CHUNK)
            s_b = s.astype(BF16)
            for (hh, _, b), a in zip(lanes, s_b.xs):
                st_ref[hh, b, n] = a
            parts.append(_ldot_nt(qi_b[sl], s_b))
            s = s * gt["dec"][n * CHUNK:n * CHUNK + 1] + _ldot_tn(v_b[sl], ko_b[sl])
        for i, a in enumerate(s.xs):
            s_scr[i] = a
        o = o_intra + _concat(parts, 0)
        _hg_write(o_ref, lanes, o)
        r = _rsqrt(_mean(o * o, axis=-1, keepdims=True) + RMS_EPS)
        g = _hg_read(g_ref, lanes)
        _hg_write(y_ref, lanes, (o * r * nw_ref[...] * (g * _lsigmoid(g))).astype(BF16))

    wide = HG_GROUP * HEAD_DIM

    def col(base):
        return pl.BlockSpec((bsz, rows, wide), lambda h, j: (0, j, base // wide + h))

    out_tile = pl.BlockSpec((bsz, rows, wide), lambda h, j: (0, j, h))
    p3 = proj.reshape(bsz, seq, IN_COLS)
    outs, recv = _call(
        body, "hgrn_fwd", (HEADS // HG_GROUP, nt),
        [col(C_HQ), out_tile, col(C_HI), col(C_HG),
         pl.BlockSpec((1, wide), lambda h, j: (0, h)), _resident((1, HEAD_DIM))],
        [out_tile, out_tile,
         pl.BlockSpec((HG_GROUP, bsz, nc, HEAD_DIM, HEAD_DIM), lambda h, j: (h, 0, j, 0, 0))],
        [jax.ShapeDtypeStruct((bsz, seq, WIDTH), BF16), jax.ShapeDtypeStruct((bsz, seq, WIDTH), F32),
         jax.ShapeDtypeStruct((HEADS, bsz, seq // CHUNK, HEAD_DIM, HEAD_DIM), BF16)],
        (p3, hf.reshape(bsz, seq, WIDTH), p3, p3, lb, nw),
        scratch=[pltpu.VMEM((len(lanes), HEAD_DIM, HEAD_DIM), F32)],
        sem=("parallel", "arbitrary"), carry=carry)
    outs = [outs[0].reshape(t, WIDTH), outs[1].reshape(t, WIDTH), outs[2]]
    return outs if carry is None else (outs, recv)


def _hgrn_bwd(proj, hf, lb, nw, o_pre, states, dy, *, bsz, seq, carry=None):
    rows = min(ROWS_HG, seq)
    nt = seq // rows
    nc = rows // CHUNK
    t = bsz * seq
    lanes = _hg_lanes(bsz)

    def body(q_ref, f_ref, v_ref, g_ref, lb_ref, nw_ref, o_ref, st_ref, dy_ref, dh_ref, dlb_ref, dnw_ref, ds_scr):
        h, j = pl.program_id(0), pl.program_id(1)

        @pl.when(j == 0)
        def _():
            ds_scr[...] = jnp.zeros_like(ds_scr)
            dlb_ref[...] = jnp.zeros_like(dlb_ref)

        @pl.when((h == 0) & (j == 0))
        def _():
            dnw_ref[...] = jnp.zeros_like(dnw_ref)

        mask = _chunk_mask(rows)
        q_raw = _hg_read(q_ref, lanes)
        lb_v = _Lanes(lb_ref[:, cs] for _, cs, _ in lanes)
        gt = _hg_gates(q_raw, _hg_read(f_ref, lanes), lb_v, rows, mask)
        v_b = _hg_read(v_ref, lanes).astype(BF16)
        a_f = gt["qs"] * gt["e_a"]
        b_f = gt["k"] * gt["e_b"]
        qi_f = gt["qs"] * gt["e_q"]
        ko_f = gt["k"] * gt["e_k"]
        a_b, b_b, qi_b, ko_b = a_f.astype(BF16), b_f.astype(BF16), qi_f.astype(BF16), ko_f.astype(BF16)

        o = _hg_read(o_ref, lanes)
        nw_v = nw_ref[...]
        g = _hg_read(g_ref, lanes)
        dyv = _hg_read(dy_ref, lanes)
        r = _rsqrt(_mean(o * o, axis=-1, keepdims=True) + RMS_EPS)
        sgg = _lsigmoid(g)
        d_g = dyv * (o * r * nw_v) * (sgg * (1.0 + g * (1.0 - sgg)))
        d_on = dyv * (g * sgg)
        dnw_ref[...] += _sum(d_on * o * r, axis=0, keepdims=True).total()
        tt = d_on * nw_v
        d_o = r * tt - o * (r * r * r) * _mean(tt * o, axis=-1, keepdims=True)
        do_b = d_o.astype(BF16)

        sc_b = _where(mask, _ldot_nt(a_b, b_b), 0.0).astype(BF16)
        dsc_b = _where(mask, _ldot_nt(do_b, v_b), 0.0).astype(BF16)
        d_v = _ldot_tn(sc_b, do_b)
        d_a = _ldot(dsc_b, b_b)
        d_bm = _ldot_tn(dsc_b, a_b)

        ds = _Lanes(ds_scr[i] for i in range(len(lanes)))
        dqi_parts, dko_parts, dvi_parts, ddec_parts = [None] * nc, [None] * nc, [None] * nc, [None] * nc
        for n in reversed(range(nc)):
            sl = slice(n * CHUNK, (n + 1) * CHUNK)
            dec_n = gt["dec"][n * CHUNK:n * CHUNK + 1]
            ds_b = ds.astype(BF16)
            s_n = _Lanes(st_ref[hh, b, n] for hh, _, b in lanes)
            dqi_parts[n] = _ldot(do_b[sl], s_n)
            dko_parts[n] = _ldot(v_b[sl], ds_b)
            dvi_parts[n] = _ldot_nt(ko_b[sl], ds_b)
            d_dec = _sum(ds * s_n.astype(F32), axis=0, keepdims=True)
            ddec_parts[n] = _bcast(d_dec * dec_n, (CHUNK, HEAD_DIM))
            ds = ds * dec_n + _ldot_tn(do_b[sl], qi_b[sl])
        for i, a in enumerate(ds.xs):
            ds_scr[i] = a
        d_qi = _concat(dqi_parts, 0)
        d_ko = _concat(dko_parts, 0)
        d_v = d_v + _concat(dvi_parts, 0)

        d_qs = d_a * gt["e_a"] + d_qi * gt["e_q"]
        d_k = d_bm * gt["e_b"] + d_ko * gt["e_k"]
        t_a, t_b, t_q, t_k = d_a * a_f, d_bm * b_f, d_qi * qi_f, d_ko * ko_f
        d_bref = _chunk_total(t_b - t_a, rows)
        d_blast = _chunk_total(t_k, rows) + _concat(ddec_parts, 0)
        pos = lax.broadcasted_iota(jnp.int32, (rows, HEAD_DIM), 0) % CHUNK
        d_bc = (t_a - t_b + t_q - t_k + _where(pos == CHUNK // 2 - 1, d_bref, 0.0)
                + _where(pos == CHUNK - 1, d_blast, 0.0))
        d_logf = _mask_sum(gt["tri"], d_bc, transpose=True)

        sg, sg_neg = gt["sg"], gt["sg_neg"]
        inv_f = 1.0 / gt["forget"]
        common = (1.0 - lb_v) * sg * sg_neg
        d_fl = common * (d_logf * inv_f - d_k)
        d_lb = _sum(sg_neg * (d_logf * inv_f - d_k), axis=0, keepdims=True)
        for (_, cs, _), a in zip(lanes, d_lb.xs):
            dlb_ref[:, cs] += a
        sq = gt["sq"]
        _hg_write(dh_ref, lanes, (d_qs * (sq * (1.0 + q_raw * (1.0 - sq)))).astype(BF16), 0)
        _hg_write(dh_ref, lanes, d_fl.astype(BF16), WIDTH)
        _hg_write(dh_ref, lanes, d_v.astype(BF16), 2 * WIDTH)
        _hg_write(dh_ref, lanes, d_g.astype(BF16), 3 * WIDTH)

    assert HG_GROUP == HEADS, "the combined gradient block needs all heads in one grid step"
    wide = HG_GROUP * HEAD_DIM

    def col(base):
        return pl.BlockSpec((bsz, rows, wide), lambda h, j: (0, nt - 1 - j, base // wide + h))

    tile = pl.BlockSpec((bsz, rows, wide), lambda h, j: (0, nt - 1 - j, h))
    head_vec = pl.BlockSpec((1, wide), lambda h, j: (0, h))
    p3 = proj.reshape(bsz, seq, IN_COLS)
    outs, recv = _call(
        body, "hgrn_bwd", (HEADS // HG_GROUP, nt),
        [col(C_HQ), tile, col(C_HI), col(C_HG), head_vec, _resident((1, HEAD_DIM)), tile,
         pl.BlockSpec((HG_GROUP, bsz, nc, HEAD_DIM, HEAD_DIM), lambda h, j: (h, 0, nt - 1 - j, 0, 0)), tile],
        [pl.BlockSpec((bsz, rows, 4 * WIDTH), lambda h, j: (0, nt - 1 - j, 0)), head_vec, _resident((1, HEAD_DIM))],
        [jax.ShapeDtypeStruct((bsz, seq, 4 * WIDTH), BF16), jax.ShapeDtypeStruct((1, WIDTH), F32),
         jax.ShapeDtypeStruct((1, HEAD_DIM), F32)],
        (p3, hf.reshape(bsz, seq, WIDTH), p3, p3, lb, nw, o_pre.reshape(bsz, seq, WIDTH), states,
         dy.reshape(bsz, seq, WIDTH)),
        scratch=[pltpu.VMEM((len(lanes), HEAD_DIM, HEAD_DIM), F32)],
        sem=("arbitrary", "arbitrary"), carry=carry)
    outs = [outs[0].reshape(t, 4 * WIDTH), outs[1], outs[2]]
    return outs if carry is None else (outs, recv)


def _mem_kv(mem2d, w_k, w_v):
    rows = mem2d.shape[0]

    def body(m_ref, wk_ref, wv_ref, k_ref, v_ref):
        m_b = m_ref[...].astype(BF16)
        k_ref[...] = _dot(m_b, wk_ref[...]).astype(BF16)
        v_ref[...] = _dot(m_b, wv_ref[...]).astype(BF16)

    return pl.pallas_call(
        body, name="mem_kv", grid=(rows // MEM_LEN,),
        in_specs=[pl.BlockSpec((MEM_LEN, D_MODEL), lambda i: (i, 0)), _resident((D_MODEL, WIDTH)),
                  _resident((D_MODEL, WIDTH))],
        out_specs=[pl.BlockSpec((MEM_LEN, WIDTH), lambda i: (i, 0))] * 2,
        out_shape=[jax.ShapeDtypeStruct((rows, WIDTH), BF16)] * 2,
        compiler_params=_params("parallel"),
    )(mem2d, w_k, w_v)


def _softmax_rows(s):
    m = _lift(jnp.max)(s, axis=-1, keepdims=True)
    e = _exp(s - m)
    return e / _sum(e, axis=-1, keepdims=True)


def _attn_fwd(proj, mk, mv, *, tm, seq):
    t = proj.shape[0]
    per_b = seq // tm
    scale = HEAD_DIM ** -0.5

    def body(q_ref, k_ref, v_ref, y_ref):
        heads = [slice(h * HEAD_DIM, (h + 1) * HEAD_DIM) for h in range(HEADS)]
        q_b = _Lanes(q_ref[:, sl] for sl in heads).astype(BF16)
        p = _softmax_rows(_ldot_nt(q_b, _Lanes(k_ref[:, sl] for sl in heads)) * scale)
        out = _ldot(p.astype(BF16), _Lanes(v_ref[:, sl] for sl in heads))
        y_ref[...] = jnp.concatenate(out.xs, axis=-1).astype(BF16)

    kv = pl.BlockSpec((MEM_LEN, WIDTH), lambda i: (i // per_b, 0))
    return pl.pallas_call(
        body, name="attn_fwd", grid=(t // tm,),
        in_specs=[pl.BlockSpec((tm, WIDTH), lambda i: (i, C_MQ // WIDTH)), kv, kv],
        out_specs=pl.BlockSpec((tm, WIDTH), lambda i: (i, 0)),
        out_shape=jax.ShapeDtypeStruct((t, WIDTH), BF16),
        compiler_params=_params("parallel"),
    )(proj, mk, mv)


def _attn_bwd(proj, mk, mv, dy, x_b, *, tm, seq):
    t = proj.shape[0]
    per_b = seq // tm
    scale = HEAD_DIM ** -0.5
    last = t // tm - 1

    def body(q_ref, k_ref, v_ref, dy_ref, x_ref, dq_ref, dk_ref, dv_ref, gw_ref, acc_ref):
        i = pl.program_id(0)

        @pl.when(i % per_b == 0)
        def _():
            dk_ref[...] = jnp.zeros_like(dk_ref)
            dv_ref[...] = jnp.zeros_like(dv_ref)

        @pl.when(i == 0)
        def _():
            acc_ref[...] = jnp.zeros_like(acc_ref)

        heads = [slice(h * HEAD_DIM, (h + 1) * HEAD_DIM) for h in range(HEADS)]
        q_b = _Lanes(q_ref[:, sl] for sl in heads).astype(BF16)
        k_b, v_b = _Lanes(k_ref[:, sl] for sl in heads), _Lanes(v_ref[:, sl] for sl in heads)
        p = _softmax_rows(_ldot_nt(q_b, k_b) * scale)
        dy_b = _Lanes(dy_ref[:, sl] for sl in heads).astype(BF16)
        dp = _ldot_nt(dy_b, v_b)
        d_v = _ldot_tn(p.astype(BF16), dy_b)
        ds_b = (p * (dp - _sum(dp * p, axis=-1, keepdims=True)) * scale).astype(BF16)
        dq_b = jnp.concatenate(_ldot(ds_b, k_b).xs, axis=-1).astype(BF16)
        dq_ref[...] = dq_b
        dk_ref[...] += jnp.concatenate(_ldot_tn(ds_b, q_b).xs, axis=-1)
        dv_ref[...] += jnp.concatenate(d_v.xs, axis=-1)
        acc_ref[...] += _dot_tn(x_ref[...], dq_b)

        @pl.when(i == last)
        def _():
            gw_ref[...] = acc_ref[...].astype(BF16)

    kv = pl.BlockSpec((MEM_LEN, WIDTH), lambda i: (i // per_b, 0))
    tile = pl.BlockSpec((tm, WIDTH), lambda i: (i, 0))
    n_mem = mk.shape[0]
    return pl.pallas_call(
        body, name="attn_bwd", grid=(t // tm,),
        in_specs=[pl.BlockSpec((tm, WIDTH), lambda i: (i, C_MQ // WIDTH)), kv, kv, tile,
                  pl.BlockSpec((tm, D_MODEL), lambda i: (i, 0))],
        out_specs=[tile, kv, kv, _resident((D_MODEL, WIDTH))],
        out_shape=[jax.ShapeDtypeStruct((t, WIDTH), BF16), jax.ShapeDtypeStruct((n_mem, WIDTH), F32),
                   jax.ShapeDtypeStruct((n_mem, WIDTH), F32), jax.ShapeDtypeStruct((D_MODEL, WIDTH), BF16)],
        scratch_shapes=[pltpu.VMEM((D_MODEL, WIDTH), F32)],
        compiler_params=_params("arbitrary"),
    )(proj, mk, mv, dy, x_b)


HALO = 16


def _shift_down(u, halo, k, row):
    out = pltpu.roll(u, k, 0)
    for m in range(k):
        out = jnp.where(row == m, halo[HALO - k + m:HALO - k + m + 1, :], out)
    return out


def _shift_up(u, halo, k, row, tm):
    out = pltpu.roll(u, tm - k, 0)
    for m in range(k):
        out = jnp.where(row == tm - k + m, halo[m:m + 1, :], out)
    return out


def _merge_fwd(proj, y_b, y_c, conv_w, w_branch, b_gate, *, tm, seq, carry=None):
    t = proj.shape[0]
    per_b = seq // tm
    hb = tm // HALO

    def body(cb_ref, cc_ref, ch_ref, cch_ref, chh_ref, ga_ref, gb_ref, gc_ref, yb_ref, yc_ref, cw_ref, wb_ref,
             bg_ref, ya_ref, pa_ref, pb_ref, pc_ref, mg_ref, sa_ref, sb_ref, sc_ref):
        i = pl.program_id(0)
        row = lax.broadcasted_iota(jnp.int32, (tm, WIDTH), 0)
        u = cc_ref[...].astype(F32) * ch_ref[...].astype(F32)
        halo = jnp.where(i % per_b == 0, 0.0, cch_ref[...].astype(F32) * chh_ref[...].astype(F32))
        cw = cw_ref[...]
        y = cw[0:1] * _shift_down(u, halo, 2, row) + cw[1:2] * _shift_down(u, halo, 1, row) + cw[2:3] * u
        ya_b = (cb_ref[...].astype(F32) * y).astype(BF16)
        ya_ref[...] = ya_b
        merged = None
        for idx, (y_in, g_ref, p_ref, s_ref) in enumerate(((ya_b, ga_ref, pa_ref, sa_ref),
                                                            (yb_ref[...], gb_ref, pb_ref, sb_ref),
                                                            (yc_ref[...], gc_ref, pc_ref, sc_ref))):
            p = _dot(y_in, wb_ref[idx])
            p_ref[...] = p.astype(BF16)
            sg = _sigmoid(g_ref[...].astype(F32) + bg_ref[:, idx * D_MODEL:(idx + 1) * D_MODEL])
            s_ref[...] = sg.astype(BF16)
            term = sg * p
            merged = term if merged is None else merged + term
        mg_ref[...] = merged.astype(BF16)

    def half(c):
        return pl.BlockSpec((tm, WIDTH), lambda i: (i, c // WIDTH))

    def prev(c):
        return pl.BlockSpec((HALO, WIDTH), lambda i: (jnp.maximum(i * hb - 1, 0), c // WIDTH))

    def gate(k):
        return pl.BlockSpec((tm, D_MODEL), lambda i: (i, C_GA // D_MODEL + k))

    tile512 = pl.BlockSpec((tm, WIDTH), lambda i: (i, 0))
    tile1k = pl.BlockSpec((tm, D_MODEL), lambda i: (i, 0))
    outs, recv = _call(
        body, "merge_fwd", (t // tm,),
        [half(C_CB), half(C_CC), half(C_CH), prev(C_CC), prev(C_CH), gate(0), gate(1), gate(2),
         tile512, tile512, _resident((CONV_K, WIDTH)), _resident((3, WIDTH, D_MODEL)), _resident((1, 3 * D_MODEL))],
        [tile512] + [tile1k] * 7,
        [jax.ShapeDtypeStruct((t, WIDTH), BF16)] + [jax.ShapeDtypeStruct((t, D_MODEL), BF16)] * 7,
        (proj, proj, proj, proj, proj, proj, proj, proj, y_b, y_c, conv_w, w_branch, b_gate),
        sem=("parallel",), carry=carry)
    return outs if carry is None else (outs, recv)


def _merge_bwd(dmerged, projections, gates, branch_in, w_branch, *, tm):
    t = dmerged.shape[0]
    last = t // tm - 1

    def body(dm_ref, pa_ref, pb_ref, pc_ref, sa_ref, sb_ref, sc_ref, ya_ref, yb_ref, yc_ref, wb_ref,
             dgt_ref, dya_ref, dyb_ref, dyc_ref, dbg_ref, gw_ref, acc_ref):
        i = pl.program_id(0)

        @pl.when(i == 0)
        def _():
            dbg_ref[...] = jnp.zeros_like(dbg_ref)
            acc_ref[...] = jnp.zeros_like(acc_ref)

        dm = dm_ref[...].astype(F32)
        for idx, (p_ref, s_ref, y_ref, dy_ref) in enumerate(((pa_ref, sa_ref, ya_ref, dya_ref),
                                                             (pb_ref, sb_ref, yb_ref, dyb_ref),
                                                             (pc_ref, sc_ref, yc_ref, dyc_ref))):
            cols = slice(idx * D_MODEL, (idx + 1) * D_MODEL)
            sg = s_ref[...].astype(F32)
            dp = dm * sg
            dp_b = dp.astype(BF16)
            dgate = dp * p_ref[...].astype(F32) * (1.0 - sg)
            dgt_ref[:, cols] = dgate.astype(BF16)
            dbg_ref[:, cols] += jnp.sum(dgate, axis=0, keepdims=True)
            dy_ref[...] = _dot_nt(dp_b, wb_ref[idx]).astype(BF16)
            acc_ref[idx] += _dot_tn(y_ref[...], dp_b)

        @pl.when(i == last)
        def _():
            gw_ref[...] = acc_ref[...].astype(BF16)

    tile512 = pl.BlockSpec((tm, WIDTH), lambda i: (i, 0))
    tile1k = pl.BlockSpec((tm, D_MODEL), lambda i: (i, 0))
    return pl.pallas_call(
        body, name="merge_bwd", grid=(t // tm,),
        in_specs=[tile1k] * 7 + [tile512] * 3 + [_resident((3, WIDTH, D_MODEL))],
        out_specs=[pl.BlockSpec((tm, 3 * D_MODEL), lambda i: (i, 0)), tile512, tile512, tile512,
                   _resident((1, 3 * D_MODEL)), _resident((3, WIDTH, D_MODEL))],
        out_shape=[jax.ShapeDtypeStruct((t, 3 * D_MODEL), BF16)] + [jax.ShapeDtypeStruct((t, WIDTH), BF16)] * 3
                  + [jax.ShapeDtypeStruct((1, 3 * D_MODEL), F32), jax.ShapeDtypeStruct((3, WIDTH, D_MODEL), BF16)],
        scratch_shapes=[pltpu.VMEM((3, WIDTH, D_MODEL), F32)],
        compiler_params=_params("arbitrary"),
    )(dmerged, *projections, *gates, *branch_in, w_branch)


def _conv_bwd(proj, dya, conv_w, x_b, *, tm, seq):
    t = proj.shape[0]
    per_b = seq // tm
    hb = tm // HALO
    last_blk = t // HALO - 1
    last = t // tm - 1

    def body(cb_ref, cc_ref, ch_ref, cch_ref, chh_ref, dya_ref, cbn_ref, dyan_ref, cw_ref, x_ref,
             d_ref, dcw_ref, gw_ref, acc_ref):
        i = pl.program_id(0)

        @pl.when(i == 0)
        def _():
            dcw_ref[...] = jnp.zeros_like(dcw_ref)
            acc_ref[...] = jnp.zeros_like(acc_ref)

        row = lax.broadcasted_iota(jnp.int32, (tm, WIDTH), 0)
        cb, cc, ch = cb_ref[...].astype(F32), cc_ref[...].astype(F32), ch_ref[...].astype(F32)
        u = cc * ch
        halo = jnp.where(i % per_b == 0, 0.0, cch_ref[...].astype(F32) * chh_ref[...].astype(F32))
        u1 = _shift_down(u, halo, 1, row)
        u2 = _shift_down(u, halo, 2, row)
        cw = cw_ref[...]
        y = cw[0:1] * u2 + cw[1:2] * u1 + cw[2:3] * u
        dya = dya_ref[...].astype(F32)
        dy = dya * cb
        nxt = jnp.where(i % per_b == per_b - 1, 0.0, dyan_ref[...].astype(F32) * cbn_ref[...].astype(F32))
        du = cw[2:3] * dy + cw[1:2] * _shift_up(dy, nxt, 1, row, tm) + cw[0:1] * _shift_up(dy, nxt, 2, row, tm)
        d_ref[:, 0:WIDTH] = (dya * y).astype(BF16)
        d_ref[:, WIDTH:2 * WIDTH] = (du * ch).astype(BF16)
        d_ref[:, 2 * WIDTH:3 * WIDTH] = (du * cc).astype(BF16)
        dcw_ref[0:1, :] += jnp.sum(dy * u2, axis=0, keepdims=True)
        dcw_ref[1:2, :] += jnp.sum(dy * u1, axis=0, keepdims=True)
        dcw_ref[2:3, :] += jnp.sum(dy * u, axis=0, keepdims=True)
        acc_ref[...] += _dot_tn(x_ref[...], d_ref[...])

        @pl.when(i == last)
        def _():
            gw_ref[...] = acc_ref[...].astype(BF16)

    def half(c):
        return pl.BlockSpec((tm, WIDTH), lambda i: (i, c // WIDTH))

    def prev(c):
        return pl.BlockSpec((HALO, WIDTH), lambda i: (jnp.maximum(i * hb - 1, 0), c // WIDTH))

    def nxt(c):
        return pl.BlockSpec((HALO, WIDTH), lambda i: (jnp.minimum((i + 1) * hb, last_blk), c // WIDTH))

    return pl.pallas_call(
        body, name="conv_bwd", grid=(t // tm,),
        in_specs=[half(C_CB), half(C_CC), half(C_CH), prev(C_CC), prev(C_CH),
                  pl.BlockSpec((tm, WIDTH), lambda i: (i, 0)), nxt(C_CB), nxt(0), _resident((CONV_K, WIDTH)),
                  pl.BlockSpec((tm, D_MODEL), lambda i: (i, 0))],
        out_specs=[pl.BlockSpec((tm, 3 * WIDTH), lambda i: (i, 0)), _resident((CONV_K, WIDTH)),
                   _resident((D_MODEL, 3 * WIDTH))],
        out_shape=[jax.ShapeDtypeStruct((t, 3 * WIDTH), BF16), jax.ShapeDtypeStruct((CONV_K, WIDTH), F32),
                   jax.ShapeDtypeStruct((D_MODEL, 3 * WIDTH), BF16)],
        scratch_shapes=[pltpu.VMEM((D_MODEL, 3 * WIDTH), F32)],
        compiler_params=_params("arbitrary"),
    )(proj, proj, proj, proj, proj, dya, proj, dya, conv_w, x_b)


def _loss_head(y, target, *, tm):
    t = y.shape[0]

    def body(y_ref, t_ref, dy_ref, l_ref):
        @pl.when(pl.program_id(0) == 0)
        def _():
            l_ref[...] = jnp.zeros_like(l_ref)

        err = y_ref[...] - t_ref[...]
        dy_ref[...] = err * (1.0 / D_MODEL)
        per_row = jnp.sum(err * err, axis=-1, keepdims=True) * (1.0 / D_MODEL)
        l_ref[...] += 0.5 * jnp.sum(per_row, axis=0, keepdims=True)

    row = pl.BlockSpec((tm, D_MODEL), lambda i: (i, 0))
    return pl.pallas_call(
        body, name="loss_head", grid=(t // tm,),
        in_specs=[row, row], out_specs=[row, _resident((8, 128))],
        out_shape=[jax.ShapeDtypeStruct((t, D_MODEL), F32), jax.ShapeDtypeStruct((8, 128), F32)],
        compiler_params=_params("arbitrary"),
    )(y, target)


def _lb_softmax(lower_bounds):
    x = lower_bounds
    e = jnp.exp(x - jnp.max(x, axis=0, keepdims=True))
    return e / jnp.sum(e, axis=0, keepdims=True)


def _lb_fwd(lower_bounds):
    def body(x_ref, o_ref):
        s = _lb_softmax(x_ref[...])
        c = s[0:1]
        o_ref[0:1, :] = c - s[0:1]
        for l in range(1, DEPTH):
            c = c + s[l:l + 1]
            o_ref[l:l + 1, :] = c - s[0:1]

    return pl.pallas_call(body, name="lb_fwd", out_shape=jax.ShapeDtypeStruct(lower_bounds.shape, F32))(lower_bounds)


def _lb_bwd(lower_bounds, d_lb_all):
    def body(x_ref, d_ref, o_ref):
        s = _lb_softmax(x_ref[...])
        d = d_ref[...]
        rows = [jnp.zeros_like(d[0:1])]
        for j in range(1, DEPTH):
            acc = d[j:j + 1]
            for l in range(j + 1, DEPTH):
                acc = acc + d[l:l + 1]
            rows.append(acc)
        inner = rows[0] * s[0:1]
        for j in range(1, DEPTH):
            inner = inner + rows[j] * s[j:j + 1]
        for j in range(DEPTH):
            o_ref[j:j + 1, :] = s[j:j + 1] * (rows[j] - inner)

    return pl.pallas_call(body, name="lb_bwd", out_shape=jax.ShapeDtypeStruct(lower_bounds.shape, F32))(
        lower_bounds, d_lb_all)


def _adamw(w, g, m, v):
    m2 = ADAM_B1 * m + (1.0 - ADAM_B1) * g
    v2 = ADAM_B2 * v + (1.0 - ADAM_B2) * (g * g)
    m_hat = m2 / (1.0 - ADAM_B1 ** ADAM_STEP)
    v_hat = v2 / (1.0 - ADAM_B2 ** ADAM_STEP)
    delta = -ADAM_LR * (m_hat / (jnp.sqrt(v_hat) + ADAM_EPS) + ADAM_WD * w)
    return delta, m2, v2


def _adam_small(name, g, w, m, v):
    shape = w.shape
    flat = (-1, shape[-1])
    g2, w2, m2, v2 = (a.reshape(flat) for a in (g, w, m, v))

    def body(g_ref, w_ref, m_ref, v_ref, d_ref, mo_ref, vo_ref):
        d, mm, vv = _adamw(w_ref[...], g_ref[...], m_ref[...], v_ref[...])
        d_ref[...] = d
        mo_ref[...] = mm
        vo_ref[...] = vv

    outs = pl.pallas_call(body, name=name, out_shape=[jax.ShapeDtypeStruct(w2.shape, F32)] * 3)(g2, w2, m2, v2)
    return [o.reshape(shape) for o in outs]


def _adam_shard(name, recvs, w, m, v, *, tr):
    _, r, c = w.shape

    def body(*refs):
        rc, (w_ref, m_ref, v_ref), (g_ref, d_ref, mo_ref, vo_ref) = refs[:DEPTH], refs[DEPTH:DEPTH + 3], refs[DEPTH + 3:]
        layer = pl.program_id(0)
        for cand in range(DEPTH):
            @pl.when(layer == cand)
            def _():
                g = rc[cand][0].astype(F32)
                for d in range(1, N_DEV):
                    g = g + rc[cand][d].astype(F32)
                dl, mm, vv = _adamw(w_ref[...], g, m_ref[...], v_ref[...])
                g_ref[...] = g
                d_ref[...] = dl
                mo_ref[...] = mm
                vo_ref[...] = vv

    def recv_spec(cand):
        return pl.BlockSpec((N_DEV, tr, c), lambda l, i: (0, jnp.where(l == cand, i, 0), 0))

    tile = pl.BlockSpec((None, tr, c), lambda l, i: (l, i, 0))
    return pl.pallas_call(
        body, name=name, grid=(DEPTH, r // tr),
        in_specs=[recv_spec(cand) for cand in range(DEPTH)] + [tile] * 3,
        out_specs=[tile] * 4,
        out_shape=[jax.ShapeDtypeStruct(w.shape, F32)] * 4,
        compiler_params=_params("parallel", "parallel"),
    )(*recvs, w, m, v)


def _sum_devices(name, x):
    def body(x_ref, o_ref):
        acc = x_ref[0]
        for d in range(1, N_DEV):
            acc = acc + x_ref[d]
        o_ref[...] = acc

    return pl.pallas_call(body, name=name, out_shape=jax.ShapeDtypeStruct(x.shape[1:], x.dtype))(x)


SMALL = (("lower_bounds", 1, 512), ("conv_w", CONV_K, WIDTH), ("hg_norm_w", 1, HEAD_DIM), ("b_gate", 3, D_MODEL),
         ("ln1_g", 1, D_MODEL), ("ln1_b", 1, D_MODEL), ("ln2_g", 1, D_MODEL), ("ln2_b", 1, D_MODEL))
SMALL_ROWS = sum(r for _, r, _ in SMALL)


def _pack_small(per_layer):
    flat = [a for layer in per_layer for a in layer]

    def body(*refs):
        ins, o_ref = refs[:-1], refs[-1]
        o_ref[...] = jnp.zeros_like(o_ref)
        it = iter(ins)
        for l in range(DEPTH):
            row = l * SMALL_ROWS
            for name, nrows, ncols in SMALL:
                ref = next(it)
                if name == "b_gate":
                    for k in range(nrows):
                        o_ref[row + k:row + k + 1, :] = ref[:, k * ncols:(k + 1) * ncols]
                else:
                    o_ref[row:row + nrows, 0:ncols] = ref[...]
                row += nrows

    return pl.pallas_call(body, name="pack_small_grads",
                          out_shape=jax.ShapeDtypeStruct((DEPTH * SMALL_ROWS, D_MODEL), F32))(*flat)


def _unpack_small(summed):
    s3 = summed.reshape(DEPTH, SMALL_ROWS, D_MODEL)
    out, row = {}, 0
    for name, nrows, ncols in SMALL:
        out[name] = s3[:, row:row + nrows, :ncols].reshape(DEPTH, nrows * ncols)
        row += nrows
    return out


def _natural_cols(g):
    nd = g.ndim
    perm = tuple(range(1, nd - 1)) + (0, nd - 1)
    t = jnp.transpose(g, perm)
    return t.reshape(t.shape[:-2] + (t.shape[-2] * t.shape[-1],))


def _natural_rows(g):
    return g.reshape(g.shape[0] * g.shape[1], g.shape[2])


def _hosted(hosts, key, fn):
    pairs = hosts.get(key) if hosts else None
    if callable(pairs):
        pairs = pairs()
    if not pairs:
        return fn(None)
    outs, recvs = fn([ex for ex, _ in pairs])
    for (_, hook), recv in zip(pairs, recvs):
        hook(recv)
    return outs


def _layer_fwd(cur, cur_b, mem2d, wl, *, bsz, seq, hosts=None):
    tm = min(512, seq)
    proj, hf = _hosted(hosts, "in_proj", lambda c: _in_proj(cur_b, wl["w_in"], tm=min(1024, seq), carry=c))
    y_b, o_pre, states = _hosted(hosts, "hgrn_fwd", lambda c: _hgrn_fwd(proj, hf, wl["lb"], wl["nw"], bsz=bsz,
                                                                         seq=seq, carry=c))
    mk, mv = _mem_kv(mem2d, wl["w_mk"], wl["w_mv"])
    y_c = _attn_fwd(proj, mk, mv, tm=tm, seq=seq)
    y_a, pa, pb, pc, merged, sga, sgb, sgc = _hosted(
        hosts, "merge_fwd", lambda c: _merge_fwd(proj, y_b, y_c, wl["conv"], wl["w_br"], wl["b_gate"], tm=tm,
                                                 seq=seq, carry=c))
    z1, x1, x1_b = _hosted(hosts, "wo_ln", lambda c: _linear_ln("wo_ln", merged, wl["w_o"], cur, wl["ln1_g"],
                                                                  wl["ln1_b"], tm=tm, carry=c))
    hid = _hosted(hosts, "mlp_up", lambda c: _mm_nn("mlp_up", x1_b, wl["w_up"], tm=min(1024, seq), tn=2048,
                                                     out_dtype=BF16, relu2=True, carry=c))
    z2, x2, x2_b = _linear_ln("down_ln", hid, wl["w_down"], x1, wl["ln2_g"], wl["ln2_b"], tm=tm)
    return dict(x_b=cur_b, proj=proj, hf=hf, y_a=y_a, y_b=y_b, y_c=y_c, o_pre=o_pre, states=states, mk=mk, mv=mv,
                proj3=(pa, pb, pc), gates3=(sga, sgb, sgc), merged=merged, z1=z1, x1_b=x1_b, hid=hid, z2=z2, x2=x2,
                x2_b=x2_b)


def _layer_bwd(dcur, mem2d, s, wl, *, bsz, seq, plan=None):
    tm = min(512, seq)
    tk = min(2048, bsz * seq)
    g = {}

    def run(key, fn):
        made = plan[key](g) if plan and key in plan else None
        return _hosted({key: [made]} if made else None, key, fn)

    dz2, dz2_b, dhpre, d_ln2g, d_ln2b = run(
        "ln2_bwd_down", lambda c: _ln_bwd_mm_nt("ln2_bwd_down", dcur, s["z2"], wl["ln2_g"], wl["w_down"],
                                                s["hid"], tm=tm, tn=1024, carry=c))
    g["w_down"] = _mm_tn("grad_w_down", s["hid"], dz2_b, tk=tk, tmo=1024, tno=1024)
    dx1 = _mm_nt_sum("mlp_up_bwd", [dhpre], [0], wl["w_up"], dz2, tm=tm)
    g["w_up"] = run("grad_w_up", lambda c: _mm_tn("grad_w_up", s["x1_b"], dhpre, tk=tk, tmo=1024, tno=2048, carry=c))
    dz1, dz1_b, dmerged, d_ln1g, d_ln1b, g["w_o"] = _ln_bwd_mm_nt("ln1_bwd_wo", dx1, s["z1"], wl["ln1_g"], wl["w_o"],
                                                                  left=s["merged"], tm=tm, tn=1024)
    dgate, dya, dyb, dyc, d_bg, g["w_br"] = _merge_bwd(dmerged, s["proj3"], s["gates3"],
                                                       (s["y_a"], s["y_b"], s["y_c"]), wl["w_br"], tm=tm)
    tall = min(1024, seq)
    d_conv, d_cw, gw_conv = _conv_bwd(s["proj"], dya, wl["conv"], s["x_b"], tm=tall, seq=seq)
    dhg, d_lb, d_nw = run(
        "hgrn_bwd", lambda c: _hgrn_bwd(s["proj"], s["hf"], wl["lb"], wl["nw"], s["o_pre"], s["states"], dyb,
                                        bsz=bsz, seq=seq, carry=c))
    dmq, dmk, dmv, gw_mq = _attn_bwd(s["proj"], s["mk"], s["mv"], dyc, s["x_b"], tm=tall, seq=seq)
    tkm = min(512, mem2d.shape[0])
    g["w_mk"] = _mm_tn("grad_w_mem", mem2d, dmk, tk=tkm, tmo=1024, tno=512)
    g["w_mv"] = _mm_tn("grad_w_mem", mem2d, dmv, tk=tkm, tmo=1024, tno=512)
    pieces = [d_conv, dhg, dmq, dgate]
    offsets = [C_CB, C_HQ, C_MQ, C_GA]
    g["w_in"] = jnp.concatenate(
        [gw_conv, _mm_tn("grad_w_in_hgrn", s["x_b"], dhg, tk=tk, tmo=1024, tno=2048), gw_mq,
         _mm_tn("grad_w_in_gates", s["x_b"], dgate, tk=tk, tmo=1024, tno=1536)], axis=1)
    dx = run("in_proj_bwd", lambda c: _mm_nt_sum("in_proj_bwd", pieces, offsets, wl["w_in"], dz1,
                                                 tm=min(512, seq), carry=c))
    return dx, g, [d_lb, d_cw, d_nw, d_bg, d_ln1g, d_ln1b, d_ln2g, d_ln2b]


def kernel(x, mem, lower_bounds, w_in, conv_w, hg_norm_w, w_mem_k, w_mem_v, w_branch, b_gate, w_o, ln1_g, ln1_b, w_up, w_down, ln2_g, ln2_b, loss_target, m_lower_bounds, m_w_in, m_conv_w, m_hg_norm_w, m_w_mem_k, m_w_mem_v, m_w_branch, m_b_gate, m_w_o, m_ln1_g, m_ln1_b, m_w_up, m_w_down, m_ln2_g, m_ln2_b, v_lower_bounds, v_w_in, v_conv_w, v_hg_norm_w, v_w_mem_k, v_w_mem_v, v_w_branch, v_b_gate, v_w_o, v_ln1_g, v_ln1_b, v_w_up, v_w_down, v_ln2_g, v_ln2_b):
    bsz, seq, _ = x.shape
    t = bsz * seq
    me = _my_id()

    sh = dict(w_in=w_in.astype(BF16), w_mk=w_mem_k.astype(BF16), w_mv=w_mem_v.astype(BF16),
              w_br=w_branch.astype(BF16), w_o=w_o.astype(BF16), w_up=w_up.astype(BF16), w_down=w_down.astype(BF16))
    half_rows = D_MODEL // 2
    sh["w_in_a"], sh["w_in_b"] = sh["w_in"][:, :half_rows], sh["w_in"][:, half_rows:]
    natural = dict(w_in=_natural_cols, w_in_a=_natural_cols, w_in_b=_natural_cols, w_mk=_natural_rows,
                   w_mv=_natural_rows, w_br=_natural_cols, w_o=_natural_rows, w_up=_natural_cols,
                   w_down=_natural_rows)

    lb_all = _lb_fwd(lower_bounds)
    layer_w = [dict(lb=lb_all[l][None], nw=hg_norm_w[l][None], b_gate=b_gate[l][None], ln1_g=ln1_g[l][None],
                    ln1_b=ln1_b[l][None], ln2_g=ln2_g[l][None], ln2_b=ln2_b[l][None]) for l in range(DEPTH)]
    half_full = {}

    def near(names, l):
        srcs = [sh[n][l] for n in names]
        ex = _Exchange(srcs, piece_shapes=[s_.shape for s_ in srcs], route="near")
        return ex, lambda recv_: half_full.update({(n, l): r for n, r in zip(names, recv_)})

    def relay(names, l):
        ex = _Exchange([half_full.pop((n, l)) for n in names], route="relay")

        def hook(recv_):
            for n, r in zip(names, recv_):
                layer_w[l][n] = natural[n](r)
        return ex, hook

    small4 = ["w_mk", "w_mv", "w_br", "w_o"]
    conv_shard = conv_w.reshape(DEPTH * CONV_K * (WIDTH // N_DEV) // 128, 128)
    first = near(["w_in"], 0)
    conv_ex = _Exchange([conv_shard], piece_shapes=[conv_shard.shape])
    got = _exchange("gather_first", [first[0], conv_ex])
    first[1](got[0])
    conv_full = _natural_cols(got[1][0].reshape(N_DEV, DEPTH, CONV_K, WIDTH // N_DEV))
    second = relay(["w_in"], 0)
    second[1](_exchange("relay_first", [second[0]])[0])

    x2d = x.reshape(t, D_MODEL)
    mem2d = mem.reshape(bsz * MEM_LEN, D_MODEL)
    target2d = loss_target.reshape(t, D_MODEL)

    saved = []
    cur, cur_b = x2d, x2d.astype(BF16)
    for l in range(DEPTH):
        wl = layer_w[l]
        wl["conv"] = conv_full[l]
        more = l + 1 < DEPTH
        now = ["w_up", "w_down"] + ([] if l else small4)
        hosts = {"in_proj": [near(now, l)],
                 "hgrn_fwd": lambda l=l, more=more, now=now: [relay(now, l)] + ([near(["w_in"], l + 1)] if more else [])}
        if more:
            hosts["merge_fwd"] = lambda l=l: [relay(["w_in"], l + 1), near(small4, l + 1)]
            hosts["mlp_up"] = lambda l=l: [relay(small4, l + 1)]
        s = _layer_fwd(cur, cur_b, mem2d, wl, bsz=bsz, seq=seq, hosts=hosts)
        saved.append(s)
        cur, cur_b = s["x2"], s["x2_b"]

    dcur, loss_tile = _loss_head(cur, target2d, tm=min(512, seq))
    loss = lax.psum(loss_tile[0, 0], ("x", "y", "c"))

    in_w = IN_COLS // N_DEV

    def in_half(r):
        return lambda ref, j: ref.at[pl.ds(r * half_rows, half_rows), pl.ds(j * in_w, in_w)]

    slicer = dict(w_in_a=in_half(0), w_in_b=in_half(1), w_mk=_rows(D_MODEL // N_DEV), w_mv=_rows(D_MODEL // N_DEV),
                  w_br=_cols(D_MODEL // N_DEV), w_o=_rows(D_MODEL // N_DEV), w_up=_cols(D_FF // N_DEV),
                  w_down=_rows(D_FF // N_DEV))
    source = dict(w_in_a="w_in", w_in_b="w_in")
    recv = [dict() for _ in range(DEPTH)]

    def scatter_of(names, g, into):
        ex = _Exchange([g[source.get(n, n)] for n in names], [slicer[n] for n in names],
                       [sh[n].shape[1:] for n in names])
        return ex, lambda recv_: into.update(zip(names, recv_))

    small_rows = [None] * DEPTH
    prev = None
    rest = ["w_in_b", "w_mk", "w_mv"]
    for l in reversed(range(DEPTH)):
        plan = {"grad_w_up": lambda g, l=l: scatter_of(["w_down"], g, recv[l]),
                "hgrn_bwd": lambda g, l=l: scatter_of(["w_up", "w_o", "w_br"], g, recv[l])}
        if l == 0:
            plan["in_proj_bwd"] = lambda g: scatter_of(["w_in_a"] + rest, g, recv[0])
        else:
            plan["in_proj_bwd"] = lambda g, l=l: scatter_of(["w_in_a"], g, recv[l])
        if prev is not None:
            plan["ln2_bwd_down"] = lambda g, l=l, prev=prev: scatter_of(rest, prev, recv[l + 1])
        dcur, prev, small_rows[l] = _layer_bwd(dcur, mem2d, saved[l], layer_w[l], bsz=bsz, seq=seq, plan=plan)
    for r in recv:
        r["w_in"] = jnp.concatenate([r.pop("w_in_a"), r.pop("w_in_b")], axis=1)

    packed = _pack_small(small_rows)
    all_small = _exchange("gather_small_grads", [_Exchange([packed], piece_shapes=[packed.shape])])[0][0]
    small_grads = _unpack_small(_sum_devices("sum_small_grads", all_small))
    small_grads["lower_bounds"] = _lb_bwd(lower_bounds, small_grads["lower_bounds"])
    conv_all = small_grads["conv_w"].reshape(DEPTH, CONV_K, WIDTH)
    small_grads["conv_w"] = lax.dynamic_slice_in_dim(conv_all, me * (WIDTH // N_DEV), WIDTH // N_DEV, axis=2)

    grads, deltas, new_m, new_v = {}, {}, {}, {}
    given = dict(lower_bounds=(lower_bounds, m_lower_bounds, v_lower_bounds), conv_w=(conv_w, m_conv_w, v_conv_w),
                 hg_norm_w=(hg_norm_w, m_hg_norm_w, v_hg_norm_w), b_gate=(b_gate, m_b_gate, v_b_gate),
                 ln1_g=(ln1_g, m_ln1_g, v_ln1_g), ln1_b=(ln1_b, m_ln1_b, v_ln1_b),
                 ln2_g=(ln2_g, m_ln2_g, v_ln2_g), ln2_b=(ln2_b, m_ln2_b, v_ln2_b))
    for name, (w_, m_, v_) in given.items():
        g_ = small_grads[name].reshape(w_.shape)
        grads[name] = g_
        deltas[name], new_m[name], new_v[name] = _adam_small("adam_" + name, g_, w_, m_, v_)

    big = dict(w_in=("w_in", w_in, m_w_in, v_w_in, 128), w_mem_k=("w_mk", w_mem_k, m_w_mem_k, v_w_mem_k, 128),
               w_mem_v=("w_mv", w_mem_v, m_w_mem_v, v_w_mem_v, 128),
               w_branch=("w_br", w_branch, m_w_branch, v_w_branch, 512), w_o=("w_o", w_o, m_w_o, v_w_o, 128),
               w_up=("w_up", w_up, m_w_up, v_w_up, 256), w_down=("w_down", w_down, m_w_down, v_w_down, 128))
    for name, (k, w_, m_, v_, tr) in big.items():
        shape = w_.shape
        flat = (DEPTH, -1, shape[-1])
        rc = [recv[l][k].reshape((N_DEV,) + w_.reshape(flat).shape[1:]) for l in range(DEPTH)]
        outs = _adam_shard("adam_" + name, rc, w_.reshape(flat), m_.reshape(flat), v_.reshape(flat), tr=tr)
        grads[name], deltas[name], new_m[name], new_v[name] = (o.reshape(shape) for o in outs)

    order = ["lower_bounds", "w_in", "conv_w", "hg_norm_w", "w_mem_k", "w_mem_v", "w_branch", "b_gate", "w_o",
             "ln1_g", "ln1_b", "w_up", "w_down", "ln2_g", "ln2_b"]
    return (loss, dcur.reshape(x.shape), *[grads[n] for n in order], *[deltas[n] for n in order],
            *[new_m[n] for n in order], *[new_v[n] for n in order])
```

```python
import functools

import jax
import jax.numpy as jnp
from jax import lax
from jax.experimental import pallas as pl
from jax.experimental.pallas import tpu as pltpu

F32 = jnp.float32
BF16 = jnp.bfloat16

N_DEV = 8
D_MODEL = 1024
DEPTH = 4
MEM_LEN = 256
CONV_K = 3
WIDTH = 512
HEADS = 4
HEAD_DIM = 128
CHUNK = 32
D_FF = 4 * D_MODEL
IN_COLS = 7168
ALPHA = (2.0 * DEPTH) ** 0.25
LN_EPS = 1e-5
RMS_EPS = 1e-6
ADAM_LR = 0.001
ADAM_B1 = 0.9
ADAM_B2 = 0.999
ADAM_EPS = 1e-08
ADAM_WD = 0.01
ADAM_STEP = 10

C_CB, C_CC, C_CH, C_HQ, C_HF, C_HI, C_HG, C_MQ, C_GA = 0, 512, 1024, 1536, 2048, 2560, 3072, 3584, 4096

ROWS_HG = 256
NT_DIMS = (((1,), (1,)), ((), ()))
TN_DIMS = (((0,), (0,)), ((), ()))
MESH = pl.DeviceIdType.MESH


def _dot(a, b):
    return jnp.dot(a, b, preferred_element_type=F32)


def _dot_nt(a, b):
    return lax.dot_general(a, b, NT_DIMS, preferred_element_type=F32)


def _dot_tn(a, b):
    return lax.dot_general(a, b, TN_DIMS, preferred_element_type=F32)


def _sigmoid(x):
    return 1.0 / (1.0 + jnp.exp(-x))


def _params(*sem):
    return pltpu.CompilerParams(dimension_semantics=sem)


def _resident(shape, single=False):
    nd = len(shape)
    if single:
        return pl.BlockSpec(shape, lambda *_: (0,) * nd, pipeline_mode=pl.Buffered(1))
    return pl.BlockSpec(shape, lambda *_: (0,) * nd)


def _my_id():
    return 4 * lax.axis_index("x") + 2 * lax.axis_index("y") + lax.axis_index("c")


class _Exchange:
    def __init__(self, srcs, slicers=None, piece_shapes=None, route="all"):
        self.srcs, self.n, self.route = list(srcs), len(srcs), route
        self.slicers = list(slicers) if slicers else [_whole] * self.n
        any_spec = pl.BlockSpec(memory_space=pl.ANY)
        self.in_specs = [any_spec] * self.n
        self.out_specs = [any_spec] * self.n
        if route == "relay":
            self.out_shape = [jax.ShapeDtypeStruct(a.shape, a.dtype) for a in srcs]
        else:
            self.out_shape = [jax.ShapeDtypeStruct((N_DEV,) + tuple(s), a.dtype) for s, a in zip(piece_shapes, srcs)]
        self.aliased = route == "relay"
        self.scratch = [pltpu.SemaphoreType.DMA((self.n * N_DEV,)), pltpu.SemaphoreType.DMA((self.n * N_DEV,)),
                        pltpu.SemaphoreType.DMA((self.n,))]

    def _peer(self, j, me):
        if self.route == "all":
            return j != me
        return (j != me) & ((j % 2 == lax.axis_index("c")) | (j // 2 == me // 2))

    def _remote(self, ins, outs, sems, k, j, me):
        return pltpu.make_async_remote_copy(
            src_ref=self.slicers[k](ins[k], j), dst_ref=outs[k].at[me],
            send_sem=sems[0].at[k * N_DEV + j], recv_sem=sems[1].at[k * N_DEV + me],
            device_id=(j // 4, (j // 2) % 2, j % 2), device_id_type=MESH)

    def _local(self, ins, outs, sems, k, j, me):
        return pltpu.make_async_copy(self.slicers[k](ins[k], j), outs[k].at[me], sems[2].at[k])

    def _relay(self, outs, sems, k, j):
        sibling = (lax.axis_index("x"), lax.axis_index("y"), 1 - lax.axis_index("c"))
        return pltpu.make_async_remote_copy(
            src_ref=outs[k].at[j], dst_ref=outs[k].at[j], send_sem=sems[0].at[k * N_DEV + j],
            recv_sem=sems[1].at[k * N_DEV + j], device_id=sibling, device_id_type=MESH)

    def _other_chip(self, j, same_core):
        on_my_core = j % 2 == lax.axis_index("c")
        return (on_my_core if same_core else ~on_my_core) & (j // 2 != _my_id() // 2)

    def start(self, ins, outs, sems):
        me = _my_id()
        for k in range(self.n):
            for j in range(N_DEV):
                if self.route == "relay":
                    @pl.when(self._other_chip(j, True))
                    def _():
                        self._relay(outs, sems, k, j).start()
                    continue

                @pl.when(self._peer(j, me))
                def _():
                    self._remote(ins, outs, sems, k, j, me).start()

                @pl.when(j == me)
                def _():
                    self._local(ins, outs, sems, k, j, me).start()

    def wait(self, ins, outs, sems):
        me = _my_id()
        for k in range(self.n):
            for j in range(N_DEV):
                if self.route == "relay":
                    @pl.when(self._other_chip(j, False))
                    def _():
                        self._relay(outs, sems, k, j).wait_recv()

                    @pl.when(self._other_chip(j, True))
                    def _():
                        self._relay(outs, sems, k, j).wait_send()
                    continue

                @pl.when(self._peer(j, me))
                def _():
                    pltpu.make_async_remote_copy(
                        src_ref=self.slicers[k](ins[k], j), dst_ref=outs[k].at[j],
                        send_sem=sems[0].at[k * N_DEV + j], recv_sem=sems[1].at[k * N_DEV + j],
                        device_id=(j // 4, (j // 2) % 2, j % 2), device_id_type=MESH).wait_recv()
                    self._remote(ins, outs, sems, k, j, me).wait_send()

                @pl.when(j == me)
                def _():
                    self._local(ins, outs, sems, k, j, me).wait()


def _carried(exchanges, n_in, n_out):
    c_in = [s for ex in exchanges for s in ex.in_specs]
    c_out = [s for ex in exchanges for s in ex.out_specs]
    shapes = [s for ex in exchanges for s in ex.out_shape]
    sems = [s for ex in exchanges for s in ex.scratch]
    srcs = [a for ex in exchanges for a in ex.srcs]
    aliases, off = {}, 0
    for ex in exchanges:
        if ex.aliased:
            aliases.update({n_in + off + k: n_out + off + k for k in range(ex.n)})
        off += ex.n
    total = off

    def split(refs, n_scr):
        ins, outs = refs[:n_in], refs[n_in + total:n_in + total + n_out]
        rest = refs[n_in + 2 * total + n_out:]
        scr, sem_refs = rest[:n_scr], rest[n_scr:]
        parts, off_ = [], 0
        for i, ex in enumerate(exchanges):
            parts.append((refs[n_in + off_:n_in + off_ + ex.n],
                          refs[n_in + total + n_out + off_:n_in + total + n_out + off_ + ex.n],
                          sem_refs[3 * i:3 * i + 3]))
            off_ += ex.n
        return ins, outs, scr, parts

    return c_in, c_out, shapes, sems, srcs, aliases, split


def _exchange(name, exchanges):
    c_in, c_out, shapes, sems, srcs, aliases, split = _carried(exchanges, 0, 0)

    def body(*refs):
        _, _, _, parts = split(refs, 0)
        for ex, part in zip(exchanges, parts):
            ex.start(*part)
        for ex, part in zip(exchanges, parts):
            ex.wait(*part)

    outs = pl.pallas_call(
        body, name=name, in_specs=c_in, out_specs=c_out, out_shape=shapes, scratch_shapes=sems,
        input_output_aliases=aliases, compiler_params=pltpu.CompilerParams(has_side_effects=True))(*srcs)
    return _per_exchange(exchanges, outs)


def _per_exchange(exchanges, flat):
    out, off = [], 0
    for ex in exchanges:
        out.append(flat[off:off + ex.n])
        off += ex.n
    return out


def _call(body, name, grid, in_specs, out_specs, out_shape, args, scratch=(), sem=None, carry=None):
    n_in, n_out, n_scr = len(in_specs), len(out_specs), len(scratch)
    if not carry:
        outs = pl.pallas_call(body, name=name, grid=grid, in_specs=in_specs, out_specs=out_specs,
                              out_shape=out_shape, scratch_shapes=list(scratch),
                              compiler_params=_params(*sem))(*args)
        return outs, None
    c_in, c_out, shapes, sems, srcs, aliases, split = _carried(carry, n_in, n_out)

    def hosted(*refs):
        ins, outs, scr, parts = split(refs, n_scr)
        first, last = True, True
        for d, size in enumerate(grid):
            first = first & (pl.program_id(d) == 0)
            last = last & (pl.program_id(d) == size - 1)

        @pl.when(first)
        def _():
            for ex, part in zip(carry, parts):
                ex.start(*part)

        body(*ins, *outs, *scr)

        @pl.when(last)
        def _():
            for ex, part in zip(carry, parts):
                ex.wait(*part)

    outs = pl.pallas_call(
        hosted, name=name + "_x", grid=grid, in_specs=list(in_specs) + c_in,
        out_specs=list(out_specs) + c_out, out_shape=list(out_shape) + shapes,
        scratch_shapes=list(scratch) + sems, input_output_aliases=aliases,
        compiler_params=_params(*(["arbitrary"] * len(grid))))(*args, *srcs)
    return outs[:n_out], _per_exchange(carry, outs[n_out:])


def _whole(ref, j):
    return ref


def _cols(width):
    return lambda ref, j: ref.at[(slice(None),) * (len(ref.shape) - 1) + (pl.ds(j * width, width),)]


def _rows(height):
    return lambda ref, j: ref.at[pl.ds(j * height, height)]


def _mm_nn(name, a, w, *, tm, tn, out_dtype, relu2=False, carry=None):
    t, k = a.shape
    n = w.shape[1]

    def body(a_ref, w_ref, o_ref):
        acc = _dot(a_ref[...].astype(BF16), w_ref[...])
        if relu2:
            r = jnp.maximum(acc, 0.0)
            acc = r * r
        o_ref[...] = acc.astype(out_dtype)

    outs, recv = _call(
        body, name, (t // tm, n // tn),
        [pl.BlockSpec((tm, k), lambda i, j: (i, 0)), pl.BlockSpec((k, tn), lambda i, j: (0, j))],
        [pl.BlockSpec((tm, tn), lambda i, j: (i, j))], [jax.ShapeDtypeStruct((t, n), out_dtype)], (a, w),
        sem=("parallel", "parallel"), carry=carry)
    return outs[0] if carry is None else (outs[0], recv)


def _in_proj(a, w, *, tm, carry=None):
    t, k = a.shape
    tn = IN_COLS // 4
    f_tile, f_off = C_HF // tn, C_HF % tn

    def body(a_ref, w_ref, o_ref, f_ref):
        acc = _dot(a_ref[...], w_ref[...])
        o_ref[...] = acc.astype(BF16)

        @pl.when(pl.program_id(1) == f_tile)
        def _():
            f_ref[...] = acc[:, f_off:f_off + WIDTH]

    outs, recv = _call(
        body, "in_proj", (t // tm, IN_COLS // tn),
        [pl.BlockSpec((tm, k), lambda i, j: (i, 0)), pl.BlockSpec((k, tn), lambda i, j: (0, j))],
        [pl.BlockSpec((tm, tn), lambda i, j: (i, j)), pl.BlockSpec((tm, WIDTH), lambda i, j: (i, 0))],
        [jax.ShapeDtypeStruct((t, IN_COLS), BF16), jax.ShapeDtypeStruct((t, WIDTH), F32)], (a, w),
        sem=("parallel", "arbitrary"), carry=carry)
    return outs if carry is None else (outs, recv)


def _linear_ln(name, a, w, resid, g, b, *, tm, carry=None):
    t, k = a.shape
    halves = [slice(0, tm // 2), slice(tm // 2, tm)] if k > D_MODEL else [slice(0, tm)]

    def body(a_ref, w_ref, r_ref, g_ref, b_ref, z_ref, x_ref, xb_ref):
        z = ALPHA * _Lanes(r_ref[s, :] for s in halves) + _ldot(_Lanes(a_ref[s, :] for s in halves), w_ref[...])
        zc = z - _mean(z, axis=-1, keepdims=True)
        y = zc * _rsqrt(_mean(zc * zc, axis=-1, keepdims=True) + LN_EPS) * g_ref[...] + b_ref[...]
        for s, zz, yy in zip(halves, z.xs, y.xs):
            z_ref[s, :] = zz
            x_ref[s, :] = yy
            xb_ref[s, :] = yy.astype(BF16)

    row = pl.BlockSpec((tm, D_MODEL), lambda i: (i, 0))
    outs, recv = _call(
        body, name, (t // tm,),
        [pl.BlockSpec((tm, k), lambda i: (i, 0)), _resident((k, D_MODEL)), row,
         _resident((1, D_MODEL)), _resident((1, D_MODEL))],
        [row, row, row],
        [jax.ShapeDtypeStruct((t, D_MODEL), F32)] * 2 + [jax.ShapeDtypeStruct((t, D_MODEL), BF16)],
        (a, w, resid, g, b), sem=("parallel",), carry=carry)
    return outs if carry is None else (outs, recv)


def _ln_bwd_mm_nt(name, dy, z, g, w, h=None, left=None, *, tm, tn, carry=None):
    t = dy.shape[0]
    n = w.shape[0]
    halves = [slice(0, tm // 2), slice(tm // 2, tm)]
    last = t // tm - 1

    def body(*refs):
        refs = list(refs)
        dy_ref, z_ref, g_ref, w_ref = refs[:4]
        del refs[:4]
        h_ref = refs.pop(0) if h is not None else None
        left_ref = refs.pop(0) if left is not None else None
        dz_ref, dzb_ref, o_ref, dg_ref, db_ref = refs[:5]
        gw_ref, acc_ref = (refs[5], refs[6]) if left is not None else (None, None)

        @pl.when(pl.program_id(0) == 0)
        def _():
            dg_ref[...] = jnp.zeros_like(dg_ref)
            db_ref[...] = jnp.zeros_like(db_ref)
            if left is not None:
                acc_ref[...] = jnp.zeros_like(acc_ref)

        zv = _Lanes(z_ref[s, :] for s in halves)
        dyv = _Lanes(dy_ref[s, :] for s in halves)
        mu = _mean(zv, axis=-1, keepdims=True)
        zc = zv - mu
        rstd = _rsqrt(_mean(zc * zc, axis=-1, keepdims=True) + LN_EPS)
        xh = zc * rstd
        gdy = dyv * g_ref[...]
        m1 = _mean(gdy, axis=-1, keepdims=True)
        m2 = _mean(gdy * xh, axis=-1, keepdims=True)
        dz = rstd * (gdy - m1 - xh * m2)
        dz_b = dz.astype(BF16)
        for s, a, a_b in zip(halves, dz.xs, dz_b.xs):
            dz_ref[s, :] = a
            dzb_ref[s, :] = a_b
        dg_ref[...] += _sum(dyv * xh, axis=0, keepdims=True).total()
        db_ref[...] += _sum(dyv, axis=0, keepdims=True).total()
        for c in range(n // tn):
            cols = slice(c * tn, (c + 1) * tn)
            acc = _ldot_nt(dz_b, w_ref[cols, :])
            if h is not None:
                acc = acc * (2.0 * _sqrt(_Lanes(h_ref[s, cols] for s in halves).astype(F32)))
            for s, a in zip(halves, acc.xs):
                o_ref[s, cols] = a.astype(BF16)
        if left is not None:
            acc_ref[...] += _ldot_tn(_Lanes(left_ref[s, :] for s in halves), dz_b).total()

            @pl.when(pl.program_id(0) == last)
            def _():
                gw_ref[...] = acc_ref[...].astype(BF16)

    row = pl.BlockSpec((tm, D_MODEL), lambda i: (i, 0))
    vec = _resident((1, D_MODEL))
    tile = pl.BlockSpec((tm, n), lambda i: (i, 0))
    in_specs = [row, row, vec, _resident((n, D_MODEL))]
    args = [dy, z, g, w]
    out_specs = [row, row, tile, vec, vec]
    out_shape = [jax.ShapeDtypeStruct((t, D_MODEL), F32), jax.ShapeDtypeStruct((t, D_MODEL), BF16),
                 jax.ShapeDtypeStruct((t, n), BF16), jax.ShapeDtypeStruct((1, D_MODEL), F32),
                 jax.ShapeDtypeStruct((1, D_MODEL), F32)]
    scratch = []
    if h is not None:
        in_specs.append(tile)
        args.append(h)
    if left is not None:
        m = left.shape[1]
        in_specs.append(pl.BlockSpec((tm, m), lambda i: (i, 0)))
        args.append(left)
        out_specs.append(_resident((m, D_MODEL)))
        out_shape.append(jax.ShapeDtypeStruct((m, D_MODEL), BF16))
        scratch.append(pltpu.VMEM((m, D_MODEL), F32))
    outs, recv = _call(body, name, (t // tm,), in_specs, out_specs, out_shape, args, scratch=scratch,
                       sem=("arbitrary",), carry=carry)
    return outs if carry is None else (outs, recv)


def _mm_tn(name, a, b, *, tk, tmo, tno, carry=None):
    t, m = a.shape
    n = b.shape[1]
    nk = t // tk

    def body(a_ref, b_ref, o_ref, acc_ref):
        k = pl.program_id(2)
        p = _dot_tn(a_ref[...].astype(BF16), b_ref[...].astype(BF16))

        @pl.when(k == 0)
        def _():
            acc_ref[...] = p

        @pl.when(k > 0)
        def _():
            acc_ref[...] += p

        @pl.when(k == nk - 1)
        def _():
            o_ref[...] = acc_ref[...].astype(BF16)

    outs, recv = _call(
        body, name, (m // tmo, n // tno, nk),
        [pl.BlockSpec((tk, tmo), lambda i, j, k: (k, i)), pl.BlockSpec((tk, tno), lambda i, j, k: (k, j))],
        [pl.BlockSpec((tmo, tno), lambda i, j, k: (i, j))], [jax.ShapeDtypeStruct((m, n), BF16)], (a, b),
        scratch=[pltpu.VMEM((tmo, tno), F32)], sem=("parallel", "parallel", "arbitrary"), carry=carry)
    return outs[0] if carry is None else (outs[0], recv)


def _mm_nt_sum(name, pieces, offsets, w, resid, *, tm, carry=None):
    t = resid.shape[0]
    widths = [p.shape[1] for p in pieces]
    n_p = len(pieces)

    def body(*refs):
        p_refs, w_ref, r_ref, o_ref = refs[:n_p], refs[n_p], refs[n_p + 1], refs[n_p + 2]
        acc = ALPHA * r_ref[...]
        for p_ref, off, wd in zip(p_refs, offsets, widths):
            acc = acc + _dot_nt(p_ref[...], w_ref[:, off:off + wd])
        o_ref[...] = acc

    row = pl.BlockSpec((tm, D_MODEL), lambda i: (i, 0))
    outs, recv = _call(
        body, name, (t // tm,),
        [pl.BlockSpec((tm, wd), lambda i: (i, 0)) for wd in widths] + [_resident(w.shape, single=True), row],
        [row], [jax.ShapeDtypeStruct((t, D_MODEL), F32)], (*pieces, w, resid), sem=("parallel",), carry=carry)
    return outs[0] if carry is None else (outs[0], recv)


def _chunk_mask(rows):
    r = lax.broadcasted_iota(jnp.int32, (rows, rows), 0)
    c = lax.broadcasted_iota(jnp.int32, (rows, rows), 1)
    return ((r // CHUNK) == (c // CHUNK)) & (c <= r)


class _Lanes:
    def __init__(self, xs):
        self.xs = list(xs)

    def _with(self, other, f):
        if isinstance(other, _Lanes):
            return _Lanes([f(a, b) for a, b in zip(self.xs, other.xs)])
        return _Lanes([f(a, other) for a in self.xs])

    def __add__(self, o):
        return self._with(o, lambda a, b: a + b)

    def __radd__(self, o):
        return self._with(o, lambda a, b: b + a)

    def __sub__(self, o):
        return self._with(o, lambda a, b: a - b)

    def __rsub__(self, o):
        return self._with(o, lambda a, b: b - a)

    def __mul__(self, o):
        return self._with(o, lambda a, b: a * b)

    def __rmul__(self, o):
        return self._with(o, lambda a, b: b * a)

    def __truediv__(self, o):
        return self._with(o, lambda a, b: a / b)

    def __rtruediv__(self, o):
        return self._with(o, lambda a, b: b / a)

    def __neg__(self):
        return _Lanes([-a for a in self.xs])

    def __ge__(self, o):
        return self._with(o, lambda a, b: a >= b)

    def __getitem__(self, idx):
        return _Lanes([a[idx] for a in self.xs])

    def astype(self, dtype):
        return _Lanes([a.astype(dtype) for a in self.xs])

    def total(self):
        return functools.reduce(lambda a, b: a + b, self.xs)


def _lift(f):
    def g(*args, **kw):
        lanes = [a for a in args if isinstance(a, _Lanes)]
        if not lanes:
            return f(*args, **kw)
        return _Lanes([f(*[a.xs[i] if isinstance(a, _Lanes) else a for a in args], **kw)
                       for i in range(len(lanes[0].xs))])
    return g


def _concat(parts, axis):
    if isinstance(parts[0], _Lanes):
        return _Lanes([jnp.concatenate([p.xs[i] for p in parts], axis=axis) for i in range(len(parts[0].xs))])
    return jnp.concatenate(parts, axis=axis)


_exp, _log, _abs, _sqrt, _where = _lift(jnp.exp), _lift(jnp.log), _lift(jnp.abs), _lift(jnp.sqrt), _lift(jnp.where)
_sum, _mean, _rsqrt, _bcast = _lift(jnp.sum), _lift(jnp.mean), _lift(lax.rsqrt), _lift(jnp.broadcast_to)
_ldot, _ldot_nt, _ldot_tn = _lift(_dot), _lift(_dot_nt), _lift(_dot_tn)
_lsigmoid = _lift(_sigmoid)


def _mask_sum(mask_b, x, transpose=False):
    f = _ldot_tn if transpose else _ldot
    hi = x.astype(BF16)
    lo = (x - hi.astype(F32)).astype(BF16)
    return f(mask_b, hi) + f(mask_b, lo)


def _chunk_row(x, pos, rows):
    nc = rows // CHUNK

    def one(a):
        a3 = a.reshape(nc, CHUNK, HEAD_DIM)
        return jnp.broadcast_to(a3[:, pos:pos + 1, :], (nc, CHUNK, HEAD_DIM)).reshape(rows, HEAD_DIM)

    return _lift(one)(x)


def _chunk_total(x, rows):
    nc = rows // CHUNK

    def one(a):
        tot = jnp.sum(a.reshape(nc, CHUNK, HEAD_DIM), axis=1, keepdims=True)
        return jnp.broadcast_to(tot, (nc, CHUNK, HEAD_DIM)).reshape(rows, HEAD_DIM)

    return _lift(one)(x)


def _sigmoid_pair(x):
    e = _exp(-_abs(x))
    big = 1.0 / (1.0 + e)
    small = e * big
    pos = x >= 0.0
    return _where(pos, big, small), _where(pos, small, big)


def _hg_gates(q_raw, fl, lb, rows, mask):
    tri = mask.astype(BF16)
    sg, sg_neg = _sigmoid_pair(fl)
    forget = lb + (1.0 - lb) * sg
    k = (1.0 - lb) * sg_neg
    sq = _lsigmoid(q_raw)
    qs = q_raw * sq
    bc = _mask_sum(tri, _log(forget))
    bref = _chunk_row(bc, CHUNK // 2 - 1, rows)
    blast = _chunk_row(bc, CHUNK - 1, rows)
    return dict(tri=tri, sg=sg, sg_neg=sg_neg, forget=forget, k=k, sq=sq, qs=qs,
                e_a=_exp(bc - bref), e_b=_exp(bref - bc), e_q=_exp(bc), e_k=_exp(blast - bc),
                dec=_exp(blast))


HG_GROUP = 4


def _hg_lanes(bsz):
    return [(hh, slice(hh * HEAD_DIM, (hh + 1) * HEAD_DIM), b) for hh in range(HG_GROUP) for b in range(bsz)]


def _hg_read(ref, lanes):
    return _Lanes(ref[b, :, cs].astype(F32) for _, cs, b in lanes)


def _hg_write(ref, lanes, val, offset=0):
    for (_, cs, b), a in zip(lanes, val.xs):
        ref[b, :, offset + cs.start:offset + cs.stop] = a


def _hgrn_fwd(proj, hf, lb, nw, *, bsz, seq, carry=None):
    rows = min(ROWS_HG, seq)
    nt = seq // rows
    nc = rows // CHUNK
    t = bsz * seq

    lanes = _hg_lanes(bsz)

    def body(q_ref, f_ref, v_ref, g_ref, lb_ref, nw_ref, y_ref, o_ref, st_ref, s_scr):
        @pl.when(pl.program_id(1) == 0)
        def _():
            s_scr[...] = jnp.zeros_like(s_scr)

        mask = _chunk_mask(rows)
        lb_v = _Lanes(lb_ref[:, cs] for _, cs, _ in lanes)
        gt = _hg_gates(_hg_read(q_ref, lanes), _hg_read(f_ref, lanes), lb_v, rows, mask)
        v_b = _hg_read(v_ref, lanes).astype(BF16)
        a_b = (gt["qs"] * gt["e_a"]).astype(BF16)
        b_b = (gt["k"] * gt["e_b"]).astype(BF16)
        qi_b = (gt["qs"] * gt["e_q"]).astype(BF16)
        ko_b = (gt["k"] * gt["e_k"]).astype(BF16)
        scores = _where(mask, _ldot_nt(a_b, b_b), 0.0)
        o_intra = _ldot(scores.astype(BF16), v_b)

        s = _Lanes(s_scr[i] for i in range(len(lanes)))
        parts = []
        for n in range(nc):
            sl = slice(n * CHUNK, (n + 1) * CHUNK)
            s_b = s.astype(BF16)
            for (hh, _, b), a in zip(lanes, s_b.xs):
                st_ref[hh, b, n] = a
            parts.append(_ldot_nt(qi_b[sl], s_b))
            s = s * gt["dec"][n * CHUNK:n * CHUNK + 1] + _ldot_tn(v_b[sl], ko_b[sl])
        for i, a in enumerate(s.xs):
            s_scr[i] = a
        o = o_intra + _concat(parts, 0)
        _hg_write(o_ref, lanes, o)
        r = _rsqrt(_mean(o * o, axis=-1, keepdims=True) + RMS_EPS)
        g = _hg_read(g_ref, lanes)
        _hg_write(y_ref, lanes, (o * r * nw_ref[...] * (g * _lsigmoid(g))).astype(BF16))

    wide = HG_GROUP * HEAD_DIM

    def col(base):
        return pl.BlockSpec((bsz, rows, wide), lambda h, j: (0, j, base // wide + h))

    out_tile = pl.BlockSpec((bsz, rows, wide), lambda h, j: (0, j, h))
    p3 = proj.reshape(bsz, seq, IN_COLS)
    outs, recv = _call(
        body, "hgrn_fwd", (HEADS // HG_GROUP, nt),
        [col(C_HQ), out_tile, col(C_HI), col(C_HG),
         pl.BlockSpec((1, wide), lambda h, j: (0, h)), _resident((1, HEAD_DIM))],
        [out_tile, out_tile,
         pl.BlockSpec((HG_GROUP, bsz, nc, HEAD_DIM, HEAD_DIM), lambda h, j: (h, 0, j, 0, 0))],
        [jax.ShapeDtypeStruct((bsz, seq, WIDTH), BF16), jax.ShapeDtypeStruct((bsz, seq, WIDTH), F32),
         jax.ShapeDtypeStruct((HEADS, bsz, seq // CHUNK, HEAD_DIM, HEAD_DIM), BF16)],
        (p3, hf.reshape(bsz, seq, WIDTH), p3, p3, lb, nw),
        scratch=[pltpu.VMEM((len(lanes), HEAD_DIM, HEAD_DIM), F32)],
        sem=("parallel", "arbitrary"), carry=carry)
    outs = [outs[0].reshape(t, WIDTH), outs[1].reshape(t, WIDTH), outs[2]]
    return outs if carry is None else (outs, recv)


def _hgrn_bwd(proj, hf, lb, nw, o_pre, states, dy, *, bsz, seq, carry=None):
    rows = min(ROWS_HG, seq)
    nt = seq // rows
    nc = rows // CHUNK
    t = bsz * seq
    lanes = _hg_lanes(bsz)

    def body(q_ref, f_ref, v_ref, g_ref, lb_ref, nw_ref, o_ref, st_ref, dy_ref, dh_ref, dlb_ref, dnw_ref, ds_scr):
        h, j = pl.program_id(0), pl.program_id(1)

        @pl.when(j == 0)
        def _():
            ds_scr[...] = jnp.zeros_like(ds_scr)
            dlb_ref[...] = jnp.zeros_like(dlb_ref)

        @pl.when((h == 0) & (j == 0))
        def _():
            dnw_ref[...] = jnp.zeros_like(dnw_ref)

        mask = _chunk_mask(rows)
        q_raw = _hg_read(q_ref, lanes)
        lb_v = _Lanes(lb_ref[:, cs] for _, cs, _ in lanes)
        gt = _hg_gates(q_raw, _hg_read(f_ref, lanes), lb_v, rows, mask)
        v_b = _hg_read(v_ref, lanes).astype(BF16)
        a_f = gt["qs"] * gt["e_a"]
        b_f = gt["k"] * gt["e_b"]
        qi_f = gt["qs"] * gt["e_q"]
        ko_f = gt["k"] * gt["e_k"]
        a_b, b_b, qi_b, ko_b = a_f.astype(BF16), b_f.astype(BF16), qi_f.astype(BF16), ko_f.astype(BF16)

        o = _hg_read(o_ref, lanes)
        nw_v = nw_ref[...]
        g = _hg_read(g_ref, lanes)
        dyv = _hg_read(dy_ref, lanes)
        r = _rsqrt(_mean(o * o, axis=-1, keepdims=True) + RMS_EPS)
        sgg = _lsigmoid(g)
        d_g = dyv * (o * r * nw_v) * (sgg * (1.0 + g * (1.0 - sgg)))
        d_on = dyv * (g * sgg)
        dnw_ref[...] += _sum(d_on * o * r, axis=0, keepdims=True).total()
        tt = d_on * nw_v
        d_o = r * tt - o * (r * r * r) * _mean(tt * o, axis=-1, keepdims=True)
        do_b = d_o.astype(BF16)

        sc_b = _where(mask, _ldot_nt(a_b, b_b), 0.0).astype(BF16)
        dsc_b = _where(mask, _ldot_nt(do_b, v_b), 0.0).astype(BF16)
        d_v = _ldot_tn(sc_b, do_b)
        d_a = _ldot(dsc_b, b_b)
        d_bm = _ldot_tn(dsc_b, a_b)

        ds = _Lanes(ds_scr[i] for i in range(len(lanes)))
        dqi_parts, dko_parts, dvi_parts, ddec_parts = [None] * nc, [None] * nc, [None] * nc, [None] * nc
        for n in reversed(range(nc)):
            sl = slice(n * CHUNK, (n + 1) * CHUNK)
            dec_n = gt["dec"][n * CHUNK:n * CHUNK + 1]
            ds_b = ds.astype(BF16)
            s_n = _Lanes(st_ref[hh, b, n] for hh, _, b in lanes)
            dqi_parts[n] = _ldot(do_b[sl], s_n)
            dko_parts[n] = _ldot(v_b[sl], ds_b)
            dvi_parts[n] = _ldot_nt(ko_b[sl], ds_b)
            d_dec = _sum(ds * s_n.astype(F32), axis=0, keepdims=True)
            ddec_parts[n] = _bcast(d_dec * dec_n, (CHUNK, HEAD_DIM))
            ds = ds * dec_n + _ldot_tn(do_b[sl], qi_b[sl])
        for i, a in enumerate(ds.xs):
            ds_scr[i] = a
        d_qi = _concat(dqi_parts, 0)
        d_ko = _concat(dko_parts, 0)
        d_v = d_v + _concat(dvi_parts, 0)

        d_qs = d_a * gt["e_a"] + d_qi * gt["e_q"]
        d_k = d_bm * gt["e_b"] + d_ko * gt["e_k"]
        t_a, t_b, t_q, t_k = d_a * a_f, d_bm * b_f, d_qi * qi_f, d_ko * ko_f
        d_bref = _chunk_total(t_b - t_a, rows)
        d_blast = _chunk_total(t_k, rows) + _concat(ddec_parts, 0)
        pos = lax.broadcasted_iota(jnp.int32, (rows, HEAD_DIM), 0) % CHUNK
        d_bc = (t_a - t_b + t_q - t_k + _where(pos == CHUNK // 2 - 1, d_bref, 0.0)
                + _where(pos == CHUNK - 1, d_blast, 0.0))
        d_logf = _mask_sum(gt["tri"], d_bc, transpose=True)

        sg, sg_neg = gt["sg"], gt["sg_neg"]
        inv_f = 1.0 / gt["forget"]
        common = (1.0 - lb_v) * sg * sg_neg
        d_fl = common * (d_logf * inv_f - d_k)
        d_lb = _sum(sg_neg * (d_logf * inv_f - d_k), axis=0, keepdims=True)
        for (_, cs, _), a in zip(lanes, d_lb.xs):
            dlb_ref[:, cs] += a
        sq = gt["sq"]
        _hg_write(dh_ref, lanes, (d_qs * (sq * (1.0 + q_raw * (1.0 - sq)))).astype(BF16), 0)
        _hg_write(dh_ref, lanes, d_fl.astype(BF16), WIDTH)
        _hg_write(dh_ref, lanes, d_v.astype(BF16), 2 * WIDTH)
        _hg_write(dh_ref, lanes, d_g.astype(BF16), 3 * WIDTH)

    assert HG_GROUP == HEADS, "the combined gradient block needs all heads in one grid step"
    wide = HG_GROUP * HEAD_DIM

    def col(base):
        return pl.BlockSpec((bsz, rows, wide), lambda h, j: (0, nt - 1 - j, base // wide + h))

    tile = pl.BlockSpec((bsz, rows, wide), lambda h, j: (0, nt - 1 - j, h))
    head_vec = pl.BlockSpec((1, wide), lambda h, j: (0, h))
    p3 = proj.reshape(bsz, seq, IN_COLS)
    outs, recv = _call(
        body, "hgrn_bwd", (HEADS // HG_GROUP, nt),
        [col(C_HQ), tile, col(C_HI), col(C_HG), head_vec, _resident((1, HEAD_DIM)), tile,
         pl.BlockSpec((HG_GROUP, bsz, nc, HEAD_DIM, HEAD_DIM), lambda h, j: (h, 0, nt - 1 - j, 0, 0)), tile],
        [pl.BlockSpec((bsz, rows, 4 * WIDTH), lambda h, j: (0, nt - 1 - j, 0)), head_vec, _resident((1, HEAD_DIM))],
        [jax.ShapeDtypeStruct((bsz, seq, 4 * WIDTH), BF16), jax.ShapeDtypeStruct((1, WIDTH), F32),
         jax.ShapeDtypeStruct((1, HEAD_DIM), F32)],
        (p3, hf.reshape(bsz, seq, WIDTH), p3, p3, lb, nw, o_pre.reshape(bsz, seq, WIDTH), states,
         dy.reshape(bsz, seq, WIDTH)),
        scratch=[pltpu.VMEM((len(lanes), HEAD_DIM, HEAD_DIM), F32)],
        sem=("arbitrary", "arbitrary"), carry=carry)
    outs = [outs[0].reshape(t, 4 * WIDTH), outs[1], outs[2]]
    return outs if carry is None else (outs, recv)


def _mem_kv(mem2d, w_k, w_v):
    rows = mem2d.shape[0]

    def body(m_ref, wk_ref, wv_ref, k_ref, v_ref):
        m_b = m_ref[...].astype(BF16)
        k_ref[...] = _dot(m_b, wk_ref[...]).astype(BF16)
        v_ref[...] = _dot(m_b, wv_ref[...]).astype(BF16)

    return pl.pallas_call(
        body, name="mem_kv", grid=(rows // MEM_LEN,),
        in_specs=[pl.BlockSpec((MEM_LEN, D_MODEL), lambda i: (i, 0)), _resident((D_MODEL, WIDTH)),
                  _resident((D_MODEL, WIDTH))],
        out_specs=[pl.BlockSpec((MEM_LEN, WIDTH), lambda i: (i, 0))] * 2,
        out_shape=[jax.ShapeDtypeStruct((rows, WIDTH), BF16)] * 2,
        compiler_params=_params("parallel"),
    )(mem2d, w_k, w_v)


def _softmax_rows(s):
    m = _lift(jnp.max)(s, axis=-1, keepdims=True)
    e = _exp(s - m)
    return e / _sum(e, axis=-1, keepdims=True)


def _attn_fwd(proj, mk, mv, *, tm, seq):
    t = proj.shape[0]
    per_b = seq // tm
    scale = HEAD_DIM ** -0.5

    def body(q_ref, k_ref, v_ref, y_ref):
        heads = [slice(h * HEAD_DIM, (h + 1) * HEAD_DIM) for h in range(HEADS)]
        q_b = _Lanes(q_ref[:, sl] for sl in heads).astype(BF16)
        p = _softmax_rows(_ldot_nt(q_b, _Lanes(k_ref[:, sl] for sl in heads)) * scale)
        out = _ldot(p.astype(BF16), _Lanes(v_ref[:, sl] for sl in heads))
        y_ref[...] = jnp.concatenate(out.xs, axis=-1).astype(BF16)

    kv = pl.BlockSpec((MEM_LEN, WIDTH), lambda i: (i // per_b, 0))
    return pl.pallas_call(
        body, name="attn_fwd", grid=(t // tm,),
        in_specs=[pl.BlockSpec((tm, WIDTH), lambda i: (i, C_MQ // WIDTH)), kv, kv],
        out_specs=pl.BlockSpec((tm, WIDTH), lambda i: (i, 0)),
        out_shape=jax.ShapeDtypeStruct((t, WIDTH), BF16),
        compiler_params=_params("parallel"),
    )(proj, mk, mv)


def _attn_bwd(proj, mk, mv, dy, x_b, *, tm, seq):
    t = proj.shape[0]
    per_b = seq // tm
    scale = HEAD_DIM ** -0.5
    last = t // tm - 1

    def body(q_ref, k_ref, v_ref, dy_ref, x_ref, dq_ref, dk_ref, dv_ref, gw_ref, acc_ref):
        i = pl.program_id(0)

        @pl.when(i % per_b == 0)
        def _():
            dk_ref[...] = jnp.zeros_like(dk_ref)
            dv_ref[...] = jnp.zeros_like(dv_ref)

        @pl.when(i == 0)
        def _():
            acc_ref[...] = jnp.zeros_like(acc_ref)

        heads = [slice(h * HEAD_DIM, (h + 1) * HEAD_DIM) for h in range(HEADS)]
        q_b = _Lanes(q_ref[:, sl] for sl in heads).astype(BF16)
        k_b, v_b = _Lanes(k_ref[:, sl] for sl in heads), _Lanes(v_ref[:, sl] for sl in heads)
        p = _softmax_rows(_ldot_nt(q_b, k_b) * scale)
        dy_b = _Lanes(dy_ref[:, sl] for sl in heads).astype(BF16)
        dp = _ldot_nt(dy_b, v_b)
        d_v = _ldot_tn(p.astype(BF16), dy_b)
        ds_b = (p * (dp - _sum(dp * p, axis=-1, keepdims=True)) * scale).astype(BF16)
        dq_b = jnp.concatenate(_ldot(ds_b, k_b).xs, axis=-1).astype(BF16)
        dq_ref[...] = dq_b
        dk_ref[...] += jnp.concatenate(_ldot_tn(ds_b, q_b).xs, axis=-1)
        dv_ref[...] += jnp.concatenate(d_v.xs, axis=-1)
        acc_ref[...] += _dot_tn(x_ref[...], dq_b)

        @pl.when(i == last)
        def _():
            gw_ref[...] = acc_ref[...].astype(BF16)

    kv = pl.BlockSpec((MEM_LEN, WIDTH), lambda i: (i // per_b, 0))
    tile = pl.BlockSpec((tm, WIDTH), lambda i: (i, 0))
    n_mem = mk.shape[0]
    return pl.pallas_call(
        body, name="attn_bwd", grid=(t // tm,),
        in_specs=[pl.BlockSpec((tm, WIDTH), lambda i: (i, C_MQ // WIDTH)), kv, kv, tile,
                  pl.BlockSpec((tm, D_MODEL), lambda i: (i, 0))],
        out_specs=[tile, kv, kv, _resident((D_MODEL, WIDTH))],
        out_shape=[jax.ShapeDtypeStruct((t, WIDTH), BF16), jax.ShapeDtypeStruct((n_mem, WIDTH), F32),
                   jax.ShapeDtypeStruct((n_mem, WIDTH), F32), jax.ShapeDtypeStruct((D_MODEL, WIDTH), BF16)],
        scratch_shapes=[pltpu.VMEM((D_MODEL, WIDTH), F32)],
        compiler_params=_params("arbitrary"),
    )(proj, mk, mv, dy, x_b)


HALO = 16


def _shift_down(u, halo, k, row):
    out = pltpu.roll(u, k, 0)
    for m in range(k):
        out = jnp.where(row == m, halo[HALO - k + m:HALO - k + m + 1, :], out)
    return out


def _shift_up(u, halo, k, row, tm):
    out = pltpu.roll(u, tm - k, 0)
    for m in range(k):
        out = jnp.where(row == tm - k + m, halo[m:m + 1, :], out)
    return out


def _merge_fwd(proj, y_b, y_c, conv_w, w_branch, b_gate, *, tm, seq, carry=None):
    t = proj.shape[0]
    per_b = seq // tm
    hb = tm // HALO

    def body(cb_ref, cc_ref, ch_ref, cch_ref, chh_ref, ga_ref, gb_ref, gc_ref, yb_ref, yc_ref, cw_ref, wb_ref,
             bg_ref, ya_ref, pa_ref, pb_ref, pc_ref, mg_ref, sa_ref, sb_ref, sc_ref):
        i = pl.program_id(0)
        row = lax.broadcasted_iota(jnp.int32, (tm, WIDTH), 0)
        u = cc_ref[...].astype(F32) * ch_ref[...].astype(F32)
        halo = jnp.where(i % per_b == 0, 0.0, cch_ref[...].astype(F32) * chh_ref[...].astype(F32))
        cw = cw_ref[...]
        y = cw[0:1] * _shift_down(u, halo, 2, row) + cw[1:2] * _shift_down(u, halo, 1, row) + cw[2:3] * u
        ya_b = (cb_ref[...].astype(F32) * y).astype(BF16)
        ya_ref[...] = ya_b
        merged = None
        for idx, (y_in, g_ref, p_ref, s_ref) in enumerate(((ya_b, ga_ref, pa_ref, sa_ref),
                                                            (yb_ref[...], gb_ref, pb_ref, sb_ref),
                                                            (yc_ref[...], gc_ref, pc_ref, sc_ref))):
            p = _dot(y_in, wb_ref[idx])
            p_ref[...] = p.astype(BF16)
            sg = _sigmoid(g_ref[...].astype(F32) + bg_ref[:, idx * D_MODEL:(idx + 1) * D_MODEL])
            s_ref[...] = sg.astype(BF16)
            term = sg * p
            merged = term if merged is None else merged + term
        mg_ref[...] = merged.astype(BF16)

    def half(c):
        return pl.BlockSpec((tm, WIDTH), lambda i: (i, c // WIDTH))

    def prev(c):
        return pl.BlockSpec((HALO, WIDTH), lambda i: (jnp.maximum(i * hb - 1, 0), c // WIDTH))

    def gate(k):
        return pl.BlockSpec((tm, D_MODEL), lambda i: (i, C_GA // D_MODEL + k))

    tile512 = pl.BlockSpec((tm, WIDTH), lambda i: (i, 0))
    tile1k = pl.BlockSpec((tm, D_MODEL), lambda i: (i, 0))
    outs, recv = _call(
        body, "merge_fwd", (t // tm,),
        [half(C_CB), half(C_CC), half(C_CH), prev(C_CC), prev(C_CH), gate(0), gate(1), gate(2),
         tile512, tile512, _resident((CONV_K, WIDTH)), _resident((3, WIDTH, D_MODEL)), _resident((1, 3 * D_MODEL))],
        [tile512] + [tile1k] * 7,
        [jax.ShapeDtypeStruct((t, WIDTH), BF16)] + [jax.ShapeDtypeStruct((t, D_MODEL), BF16)] * 7,
        (proj, proj, proj, proj, proj, proj, proj, proj, y_b, y_c, conv_w, w_branch, b_gate),
        sem=("parallel",), carry=carry)
    return outs if carry is None else (outs, recv)


def _merge_bwd(dmerged, projections, gates, branch_in, w_branch, *, tm):
    t = dmerged.shape[0]
    last = t // tm - 1

    def body(dm_ref, pa_ref, pb_ref, pc_ref, sa_ref, sb_ref, sc_ref, ya_ref, yb_ref, yc_ref, wb_ref,
             dgt_ref, dya_ref, dyb_ref, dyc_ref, dbg_ref, gw_ref, acc_ref):
        i = pl.program_id(0)

        @pl.when(i == 0)
        def _():
            dbg_ref[...] = jnp.zeros_like(dbg_ref)
            acc_ref[...] = jnp.zeros_like(acc_ref)

        dm = dm_ref[...].astype(F32)
        for idx, (p_ref, s_ref, y_ref, dy_ref) in enumerate(((pa_ref, sa_ref, ya_ref, dya_ref),
                                                             (pb_ref, sb_ref, yb_ref, dyb_ref),
                                                             (pc_ref, sc_ref, yc_ref, dyc_ref))):
            cols = slice(idx * D_MODEL, (idx + 1) * D_MODEL)
            sg = s_ref[...].astype(F32)
            dp = dm * sg
            dp_b = dp.astype(BF16)
            dgate = dp * p_ref[...].astype(F32) * (1.0 - sg)
            dgt_ref[:, cols] = dgate.astype(BF16)
            dbg_ref[:, cols] += jnp.sum(dgate, axis=0, keepdims=True)
            dy_ref[...] = _dot_nt(dp_b, wb_ref[idx]).astype(BF16)
            acc_ref[idx] += _dot_tn(y_ref[...], dp_b)

        @pl.when(i == last)
        def _():
            gw_ref[...] = acc_ref[...].astype(BF16)

    tile512 = pl.BlockSpec((tm, WIDTH), lambda i: (i, 0))
    tile1k = pl.BlockSpec((tm, D_MODEL), lambda i: (i, 0))
    return pl.pallas_call(
        body, name="merge_bwd", grid=(t // tm,),
        in_specs=[tile1k] * 7 + [tile512] * 3 + [_resident((3, WIDTH, D_MODEL))],
        out_specs=[pl.BlockSpec((tm, 3 * D_MODEL), lambda i: (i, 0)), tile512, tile512, tile512,
                   _resident((1, 3 * D_MODEL)), _resident((3, WIDTH, D_MODEL))],
        out_shape=[jax.ShapeDtypeStruct((t, 3 * D_MODEL), BF16)] + [jax.ShapeDtypeStruct((t, WIDTH), BF16)] * 3
                  + [jax.ShapeDtypeStruct((1, 3 * D_MODEL), F32), jax.ShapeDtypeStruct((3, WIDTH, D_MODEL), BF16)],
        scratch_shapes=[pltpu.VMEM((3, WIDTH, D_MODEL), F32)],
        compiler_params=_params("arbitrary"),
    )(dmerged, *projections, *gates, *branch_in, w_branch)


def _conv_bwd(proj, dya, conv_w, x_b, *, tm, seq):
    t = proj.shape[0]
    per_b = seq // tm
    hb = tm // HALO
    last_blk = t // HALO - 1
    last = t // tm - 1

    def body(cb_ref, cc_ref, ch_ref, cch_ref, chh_ref, dya_ref, cbn_ref, dyan_ref, cw_ref, x_ref,
             d_ref, dcw_ref, gw_ref, acc_ref):
        i = pl.program_id(0)

        @pl.when(i == 0)
        def _():
            dcw_ref[...] = jnp.zeros_like(dcw_ref)
            acc_ref[...] = jnp.zeros_like(acc_ref)

        row = lax.broadcasted_iota(jnp.int32, (tm, WIDTH), 0)
        cb, cc, ch = cb_ref[...].astype(F32), cc_ref[...].astype(F32), ch_ref[...].astype(F32)
        u = cc * ch
        halo = jnp.where(i % per_b == 0, 0.0, cch_ref[...].astype(F32) * chh_ref[...].astype(F32))
        u1 = _shift_down(u, halo, 1, row)
        u2 = _shift_down(u, halo, 2, row)
        cw = cw_ref[...]
        y = cw[0:1] * u2 + cw[1:2] * u1 + cw[2:3] * u
        dya = dya_ref[...].astype(F32)
        dy = dya * cb
        nxt = jnp.where(i % per_b == per_b - 1, 0.0, dyan_ref[...].astype(F32) * cbn_ref[...].astype(F32))
        du = cw[2:3] * dy + cw[1:2] * _shift_up(dy, nxt, 1, row, tm) + cw[0:1] * _shift_up(dy, nxt, 2, row, tm)
        d_ref[:, 0:WIDTH] = (dya * y).astype(BF16)
        d_ref[:, WIDTH:2 * WIDTH] = (du * ch).astype(BF16)
        d_ref[:, 2 * WIDTH:3 * WIDTH] = (du * cc).astype(BF16)
        dcw_ref[0:1, :] += jnp.sum(dy * u2, axis=0, keepdims=True)
        dcw_ref[1:2, :] += jnp.sum(dy * u1, axis=0, keepdims=True)
        dcw_ref[2:3, :] += jnp.sum(dy * u, axis=0, keepdims=True)
        acc_ref[...] += _dot_tn(x_ref[...], d_ref[...])

        @pl.when(i == last)
        def _():
            gw_ref[...] = acc_ref[...].astype(BF16)

    def half(c):
        return pl.BlockSpec((tm, WIDTH), lambda i: (i, c // WIDTH))

    def prev(c):
        return pl.BlockSpec((HALO, WIDTH), lambda i: (jnp.maximum(i * hb - 1, 0), c // WIDTH))

    def nxt(c):
        return pl.BlockSpec((HALO, WIDTH), lambda i: (jnp.minimum((i + 1) * hb, last_blk), c // WIDTH))

    return pl.pallas_call(
        body, name="conv_bwd", grid=(t // tm,),
        in_specs=[half(C_CB), half(C_CC), half(C_CH), prev(C_CC), prev(C_CH),
                  pl.BlockSpec((tm, WIDTH), lambda i: (i, 0)), nxt(C_CB), nxt(0), _resident((CONV_K, WIDTH)),
                  pl.BlockSpec((tm, D_MODEL), lambda i: (i, 0))],
        out_specs=[pl.BlockSpec((tm, 3 * WIDTH), lambda i: (i, 0)), _resident((CONV_K, WIDTH)),
                   _resident((D_MODEL, 3 * WIDTH))],
        out_shape=[jax.ShapeDtypeStruct((t, 3 * WIDTH), BF16), jax.ShapeDtypeStruct((CONV_K, WIDTH), F32),
                   jax.ShapeDtypeStruct((D_MODEL, 3 * WIDTH), BF16)],
        scratch_shapes=[pltpu.VMEM((D_MODEL, 3 * WIDTH), F32)],
        compiler_params=_params("arbitrary"),
    )(proj, proj, proj, proj, proj, dya, proj, dya, conv_w, x_b)


def _loss_head(y, target, *, tm):
    t = y.shape[0]

    def body(y_ref, t_ref, dy_ref, l_ref):
        @pl.when(pl.program_id(0) == 0)
        def _():
            l_ref[...] = jnp.zeros_like(l_ref)

        err = y_ref[...] - t_ref[...]
        dy_ref[...] = err * (1.0 / D_MODEL)
        per_row = jnp.sum(err * err, axis=-1, keepdims=True) * (1.0 / D_MODEL)
        l_ref[...] += 0.5 * jnp.sum(per_row, axis=0, keepdims=True)

    row = pl.BlockSpec((tm, D_MODEL), lambda i: (i, 0))
    return pl.pallas_call(
        body, name="loss_head", grid=(t // tm,),
        in_specs=[row, row], out_specs=[row, _resident((8, 128))],
        out_shape=[jax.ShapeDtypeStruct((t, D_MODEL), F32), jax.ShapeDtypeStruct((8, 128), F32)],
        compiler_params=_params("arbitrary"),
    )(y, target)


def _lb_softmax(lower_bounds):
    x = lower_bounds
    e = jnp.exp(x - jnp.max(x, axis=0, keepdims=True))
    return e / jnp.sum(e, axis=0, keepdims=True)


def _lb_fwd(lower_bounds):
    def body(x_ref, o_ref):
        s = _lb_softmax(x_ref[...])
        c = s[0:1]
        o_ref[0:1, :] = c - s[0:1]
        for l in range(1, DEPTH):
            c = c + s[l:l + 1]
            o_ref[l:l + 1, :] = c - s[0:1]

    return pl.pallas_call(body, name="lb_fwd", out_shape=jax.ShapeDtypeStruct(lower_bounds.shape, F32))(lower_bounds)


def _lb_bwd(lower_bounds, d_lb_all):
    def body(x_ref, d_ref, o_ref):
        s = _lb_softmax(x_ref[...])
        d = d_ref[...]
        rows = [jnp.zeros_like(d[0:1])]
        for j in range(1, DEPTH):
            acc = d[j:j + 1]
            for l in range(j + 1, DEPTH):
                acc = acc + d[l:l + 1]
            rows.append(acc)
        inner = rows[0] * s[0:1]
        for j in range(1, DEPTH):
            inner = inner + rows[j] * s[j:j + 1]
        for j in range(DEPTH):
            o_ref[j:j + 1, :] = s[j:j + 1] * (rows[j] - inner)

    return pl.pallas_call(body, name="lb_bwd", out_shape=jax.ShapeDtypeStruct(lower_bounds.shape, F32))(
        lower_bounds, d_lb_all)


def _adamw(w, g, m, v):
    m2 = ADAM_B1 * m + (1.0 - ADAM_B1) * g
    v2 = ADAM_B2 * v + (1.0 - ADAM_B2) * (g * g)
    m_hat = m2 / (1.0 - ADAM_B1 ** ADAM_STEP)
    v_hat = v2 / (1.0 - ADAM_B2 ** ADAM_STEP)
    delta = -ADAM_LR * (m_hat / (jnp.sqrt(v_hat) + ADAM_EPS) + ADAM_WD * w)
    return delta, m2, v2


def _adam_small(name, g, w, m, v):
    shape = w.shape
    flat = (-1, shape[-1])
    g2, w2, m2, v2 = (a.reshape(flat) for a in (g, w, m, v))

    def body(g_ref, w_ref, m_ref, v_ref, d_ref, mo_ref, vo_ref):
        d, mm, vv = _adamw(w_ref[...], g_ref[...], m_ref[...], v_ref[...])
        d_ref[...] = d
        mo_ref[...] = mm
        vo_ref[...] = vv

    outs = pl.pallas_call(body, name=name, out_shape=[jax.ShapeDtypeStruct(w2.shape, F32)] * 3)(g2, w2, m2, v2)
    return [o.reshape(shape) for o in outs]


def _adam_shard(name, recvs, w, m, v, *, tr):
    _, r, c = w.shape

    def body(*refs):
        rc, (w_ref, m_ref, v_ref), (g_ref, d_ref, mo_ref, vo_ref) = refs[:DEPTH], refs[DEPTH:DEPTH + 3], refs[DEPTH + 3:]
        layer = pl.program_id(0)
        for cand in range(DEPTH):
            @pl.when(layer == cand)
            def _():
                g = rc[cand][0].astype(F32)
                for d in range(1, N_DEV):
                    g = g + rc[cand][d].astype(F32)
                dl, mm, vv = _adamw(w_ref[...], g, m_ref[...], v_ref[...])
                g_ref[...] = g
                d_ref[...] = dl
                mo_ref[...] = mm
                vo_ref[...] = vv

    def recv_spec(cand):
        return pl.BlockSpec((N_DEV, tr, c), lambda l, i: (0, jnp.where(l == cand, i, 0), 0))

    tile = pl.BlockSpec((None, tr, c), lambda l, i: (l, i, 0))
    return pl.pallas_call(
        body, name=name, grid=(DEPTH, r // tr),
        in_specs=[recv_spec(cand) for cand in range(DEPTH)] + [tile] * 3,
        out_specs=[tile] * 4,
        out_shape=[jax.ShapeDtypeStruct(w.shape, F32)] * 4,
        compiler_params=_params("parallel", "parallel"),
    )(*recvs, w, m, v)


def _sum_devices(name, x):
    def body(x_ref, o_ref):
        acc = x_ref[0]
        for d in range(1, N_DEV):
            acc = acc + x_ref[d]
        o_ref[...] = acc

    return pl.pallas_call(body, name=name, out_shape=jax.ShapeDtypeStruct(x.shape[1:], x.dtype))(x)


SMALL = (("lower_bounds", 1, 512), ("conv_w", CONV_K, WIDTH), ("hg_norm_w", 1, HEAD_DIM), ("b_gate", 3, D_MODEL),
         ("ln1_g", 1, D_MODEL), ("ln1_b", 1, D_MODEL), ("ln2_g", 1, D_MODEL), ("ln2_b", 1, D_MODEL))
SMALL_ROWS = sum(r for _, r, _ in SMALL)


def _pack_small(per_layer):
    flat = [a for layer in per_layer for a in layer]

    def body(*refs):
        ins, o_ref = refs[:-1], refs[-1]
        o_ref[...] = jnp.zeros_like(o_ref)
        it = iter(ins)
        for l in range(DEPTH):
            row = l * SMALL_ROWS
            for name, nrows, ncols in SMALL:
                ref = next(it)
                if name == "b_gate":
                    for k in range(nrows):
                        o_ref[row + k:row + k + 1, :] = ref[:, k * ncols:(k + 1) * ncols]
                else:
                    o_ref[row:row + nrows, 0:ncols] = ref[...]
                row += nrows

    return pl.pallas_call(body, name="pack_small_grads",
                          out_shape=jax.ShapeDtypeStruct((DEPTH * SMALL_ROWS, D_MODEL), F32))(*flat)


def _unpack_small(summed):
    s3 = summed.reshape(DEPTH, SMALL_ROWS, D_MODEL)
    out, row = {}, 0
    for name, nrows, ncols in SMALL:
        out[name] = s3[:, row:row + nrows, :ncols].reshape(DEPTH, nrows * ncols)
        row += nrows
    return out


def _natural_cols(g):
    nd = g.ndim
    perm = tuple(range(1, nd - 1)) + (0, nd - 1)
    t = jnp.transpose(g, perm)
    return t.reshape(t.shape[:-2] + (t.shape[-2] * t.shape[-1],))


def _natural_rows(g):
    return g.reshape(g.shape[0] * g.shape[1], g.shape[2])


def _hosted(hosts, key, fn):
    pairs = hosts.get(key) if hosts else None
    if callable(pairs):
        pairs = pairs()
    if not pairs:
        return fn(None)
    outs, recvs = fn([ex for ex, _ in pairs])
    for (_, hook), recv in zip(pairs, recvs):
        hook(recv)
    return outs


def _layer_fwd(cur, cur_b, mem2d, wl, *, bsz, seq, hosts=None):
    tm = min(512, seq)
    proj, hf = _hosted(hosts, "in_proj", lambda c: _in_proj(cur_b, wl["w_in"], tm=min(1024, seq), carry=c))
    y_b, o_pre, states = _hosted(hosts, "hgrn_fwd", lambda c: _hgrn_fwd(proj, hf, wl["lb"], wl["nw"], bsz=bsz,
                                                                         seq=seq, carry=c))
    mk, mv = _mem_kv(mem2d, wl["w_mk"], wl["w_mv"])
    y_c = _attn_fwd(proj, mk, mv, tm=tm, seq=seq)
    y_a, pa, pb, pc, merged, sga, sgb, sgc = _hosted(
        hosts, "merge_fwd", lambda c: _merge_fwd(proj, y_b, y_c, wl["conv"], wl["w_br"], wl["b_gate"], tm=tm,
                                                 seq=seq, carry=c))
    z1, x1, x1_b = _hosted(hosts, "wo_ln", lambda c: _linear_ln("wo_ln", merged, wl["w_o"], cur, wl["ln1_g"],
                                                                  wl["ln1_b"], tm=tm, carry=c))
    hid = _hosted(hosts, "mlp_up", lambda c: _mm_nn("mlp_up", x1_b, wl["w_up"], tm=min(1024, seq), tn=2048,
                                                     out_dtype=BF16, relu2=True, carry=c))
    z2, x2, x2_b = _linear_ln("down_ln", hid, wl["w_down"], x1, wl["ln2_g"], wl["ln2_b"], tm=tm)
    return dict(x_b=cur_b, proj=proj, hf=hf, y_a=y_a, y_b=y_b, y_c=y_c, o_pre=o_pre, states=states, mk=mk, mv=mv,
                proj3=(pa, pb, pc), gates3=(sga, sgb, sgc), merged=merged, z1=z1, x1_b=x1_b, hid=hid, z2=z2, x2=x2,
                x2_b=x2_b)


def _layer_bwd(dcur, mem2d, s, wl, *, bsz, seq, plan=None):
    tm = min(512, seq)
    tk = min(2048, bsz * seq)
    g = {}

    def run(key, fn):
        made = plan[key](g) if plan and key in plan else None
        return _hosted({key: [made]} if made else None, key, fn)

    dz2, dz2_b, dhpre, d_ln2g, d_ln2b = run(
        "ln2_bwd_down", lambda c: _ln_bwd_mm_nt("ln2_bwd_down", dcur, s["z2"], wl["ln2_g"], wl["w_down"],
                                                s["hid"], tm=tm, tn=1024, carry=c))
    g["w_down"] = _mm_tn("grad_w_down", s["hid"], dz2_b, tk=tk, tmo=1024, tno=1024)
    dx1 = _mm_nt_sum("mlp_up_bwd", [dhpre], [0], wl["w_up"], dz2, tm=tm)
    g["w_up"] = run("grad_w_up", lambda c: _mm_tn("grad_w_up", s["x1_b"], dhpre, tk=tk, tmo=1024, tno=2048, carry=c))
    dz1, dz1_b, dmerged, d_ln1g, d_ln1b, g["w_o"] = _ln_bwd_mm_nt("ln1_bwd_wo", dx1, s["z1"], wl["ln1_g"], wl["w_o"],
                                                                  left=s["merged"], tm=min(1024, seq), tn=1024)
    dgate, dya, dyb, dyc, d_bg, g["w_br"] = _merge_bwd(dmerged, s["proj3"], s["gates3"],
                                                       (s["y_a"], s["y_b"], s["y_c"]), wl["w_br"], tm=tm)
    tall = min(1024, seq)
    d_conv, d_cw, gw_conv = _conv_bwd(s["proj"], dya, wl["conv"], s["x_b"], tm=tall, seq=seq)
    dhg, d_lb, d_nw = run(
        "hgrn_bwd", lambda c: _hgrn_bwd(s["proj"], s["hf"], wl["lb"], wl["nw"], s["o_pre"], s["states"], dyb,
                                        bsz=bsz, seq=seq, carry=c))
    dmq, dmk, dmv, gw_mq = _attn_bwd(s["proj"], s["mk"], s["mv"], dyc, s["x_b"], tm=tall, seq=seq)
    tkm = min(512, mem2d.shape[0])
    g["w_mk"] = _mm_tn("grad_w_mem", mem2d, dmk, tk=tkm, tmo=1024, tno=512)
    g["w_mv"] = _mm_tn("grad_w_mem", mem2d, dmv, tk=tkm, tmo=1024, tno=512)
    pieces = [d_conv, dhg, dmq, dgate]
    offsets = [C_CB, C_HQ, C_MQ, C_GA]
    g["w_in"] = jnp.concatenate(
        [gw_conv, _mm_tn("grad_w_in_hgrn", s["x_b"], dhg, tk=tk, tmo=1024, tno=2048), gw_mq,
         _mm_tn("grad_w_in_gates", s["x_b"], dgate, tk=tk, tmo=1024, tno=1536)], axis=1)
    dx = run("in_proj_bwd", lambda c: _mm_nt_sum("in_proj_bwd", pieces, offsets, wl["w_in"], dz1,
                                                 tm=min(512, seq), carry=c))
    return dx, g, [d_lb, d_cw, d_nw, d_bg, d_ln1g, d_ln1b, d_ln2g, d_ln2b]


def kernel(x, mem, lower_bounds, w_in, conv_w, hg_norm_w, w_mem_k, w_mem_v, w_branch, b_gate, w_o, ln1_g, ln1_b, w_up, w_down, ln2_g, ln2_b, loss_target, m_lower_bounds, m_w_in, m_conv_w, m_hg_norm_w, m_w_mem_k, m_w_mem_v, m_w_branch, m_b_gate, m_w_o, m_ln1_g, m_ln1_b, m_w_up, m_w_down, m_ln2_g, m_ln2_b, v_lower_bounds, v_w_in, v_conv_w, v_hg_norm_w, v_w_mem_k, v_w_mem_v, v_w_branch, v_b_gate, v_w_o, v_ln1_g, v_ln1_b, v_w_up, v_w_down, v_ln2_g, v_ln2_b):
    bsz, seq, _ = x.shape
    t = bsz * seq
    me = _my_id()

    sh = dict(w_in=w_in.astype(BF16), w_mk=w_mem_k.astype(BF16), w_mv=w_mem_v.astype(BF16),
              w_br=w_branch.astype(BF16), w_o=w_o.astype(BF16), w_up=w_up.astype(BF16), w_down=w_down.astype(BF16))
    half_rows = D_MODEL // 2
    sh["w_in_a"], sh["w_in_b"] = sh["w_in"][:, :half_rows], sh["w_in"][:, half_rows:]
    natural = dict(w_in=_natural_cols, w_in_a=_natural_cols, w_in_b=_natural_cols, w_mk=_natural_rows,
                   w_mv=_natural_rows, w_br=_natural_cols, w_o=_natural_rows, w_up=_natural_cols,
                   w_down=_natural_rows)

    lb_all = _lb_fwd(lower_bounds)
    layer_w = [dict(lb=lb_all[l][None], nw=hg_norm_w[l][None], b_gate=b_gate[l][None], ln1_g=ln1_g[l][None],
                    ln1_b=ln1_b[l][None], ln2_g=ln2_g[l][None], ln2_b=ln2_b[l][None]) for l in range(DEPTH)]
    half_full = {}

    def near(names, l):
        srcs = [sh[n][l] for n in names]
        ex = _Exchange(srcs, piece_shapes=[s_.shape for s_ in srcs], route="near")
        return ex, lambda recv_: half_full.update({(n, l): r for n, r in zip(names, recv_)})

    def relay(names, l):
        ex = _Exchange([half_full.pop((n, l)) for n in names], route="relay")

        def hook(recv_):
            for n, r in zip(names, recv_):
                layer_w[l][n] = natural[n](r)
        return ex, hook

    small4 = ["w_mk", "w_mv", "w_br", "w_o"]
    conv_shard = conv_w.reshape(DEPTH * CONV_K * (WIDTH // N_DEV) // 128, 128)
    first = near(["w_in"], 0)
    conv_ex = _Exchange([conv_shard], piece_shapes=[conv_shard.shape])
    got = _exchange("gather_first", [first[0], conv_ex])
    first[1](got[0])
    conv_full = _natural_cols(got[1][0].reshape(N_DEV, DEPTH, CONV_K, WIDTH // N_DEV))
    second = relay(["w_in"], 0)
    second[1](_exchange("relay_first", [second[0]])[0])

    x2d = x.reshape(t, D_MODEL)
    mem2d = mem.reshape(bsz * MEM_LEN, D_MODEL)
    target2d = loss_target.reshape(t, D_MODEL)

    saved = []
    cur, cur_b = x2d, x2d.astype(BF16)
    for l in range(DEPTH):
        wl = layer_w[l]
        wl["conv"] = conv_full[l]
        more = l + 1 < DEPTH
        now = ["w_up", "w_down"] + ([] if l else small4)
        hosts = {"in_proj": [near(now, l)],
                 "hgrn_fwd": lambda l=l, more=more, now=now: [relay(now, l)] + ([near(["w_in"], l + 1)] if more else [])}
        if more:
            hosts["merge_fwd"] = lambda l=l: [relay(["w_in"], l + 1), near(small4, l + 1)]
            hosts["mlp_up"] = lambda l=l: [relay(small4, l + 1)]
        s = _layer_fwd(cur, cur_b, mem2d, wl, bsz=bsz, seq=seq, hosts=hosts)
        saved.append(s)
        cur, cur_b = s["x2"], s["x2_b"]

    dcur, loss_tile = _loss_head(cur, target2d, tm=min(512, seq))
    loss = lax.psum(loss_tile[0, 0], ("x", "y", "c"))

    in_w = IN_COLS // N_DEV

    def in_half(r):
        return lambda ref, j: ref.at[pl.ds(r * half_rows, half_rows), pl.ds(j * in_w, in_w)]

    slicer = dict(w_in_a=in_half(0), w_in_b=in_half(1), w_mk=_rows(D_MODEL // N_DEV), w_mv=_rows(D_MODEL // N_DEV),
                  w_br=_cols(D_MODEL // N_DEV), w_o=_rows(D_MODEL // N_DEV), w_up=_cols(D_FF // N_DEV),
                  w_down=_rows(D_FF // N_DEV))
    source = dict(w_in_a="w_in", w_in_b="w_in")
    recv = [dict() for _ in range(DEPTH)]

    def scatter_of(names, g, into):
        ex = _Exchange([g[source.get(n, n)] for n in names], [slicer[n] for n in names],
                       [sh[n].shape[1:] for n in names])
        return ex, lambda recv_: into.update(zip(names, recv_))

    small_rows = [None] * DEPTH
    prev = None
    rest = ["w_in_b", "w_mk", "w_mv"]
    for l in reversed(range(DEPTH)):
        plan = {"grad_w_up": lambda g, l=l: scatter_of(["w_down"], g, recv[l]),
                "hgrn_bwd": lambda g, l=l: scatter_of(["w_up", "w_o", "w_br"], g, recv[l])}
        if l == 0:
            plan["in_proj_bwd"] = lambda g: scatter_of(["w_in_a"] + rest, g, recv[0])
        else:
            plan["in_proj_bwd"] = lambda g, l=l: scatter_of(["w_in_a"], g, recv[l])
        if prev is not None:
            plan["ln2_bwd_down"] = lambda g, l=l, prev=prev: scatter_of(rest, prev, recv[l + 1])
        dcur, prev, small_rows[l] = _layer_bwd(dcur, mem2d, saved[l], layer_w[l], bsz=bsz, seq=seq, plan=plan)
    for r in recv:
        r["w_in"] = jnp.concatenate([r.pop("w_in_a"), r.pop("w_in_b")], axis=1)

    packed = _pack_small(small_rows)
    all_small = _exchange("gather_small_grads", [_Exchange([packed], piece_shapes=[packed.shape])])[0][0]
    small_grads = _unpack_small(_sum_devices("sum_small_grads", all_small))
    small_grads["lower_bounds"] = _lb_bwd(lower_bounds, small_grads["lower_bounds"])
    conv_all = small_grads["conv_w"].reshape(DEPTH, CONV_K, WIDTH)
    small_grads["conv_w"] = lax.dynamic_slice_in_dim(conv_all, me * (WIDTH // N_DEV), WIDTH // N_DEV, axis=2)

    grads, deltas, new_m, new_v = {}, {}, {}, {}
    given = dict(lower_bounds=(lower_bounds, m_lower_bounds, v_lower_bounds), conv_w=(conv_w, m_conv_w, v_conv_w),
                 hg_norm_w=(hg_norm_w, m_hg_norm_w, v_hg_norm_w), b_gate=(b_gate, m_b_gate, v_b_gate),
                 ln1_g=(ln1_g, m_ln1_g, v_ln1_g), ln1_b=(ln1_b, m_ln1_b, v_ln1_b),
                 ln2_g=(ln2_g, m_ln2_g, v_ln2_g), ln2_b=(ln2_b, m_ln2_b, v_ln2_b))
    for name, (w_, m_, v_) in given.items():
        g_ = small_grads[name].reshape(w_.shape)
        grads[name] = g_
        deltas[name], new_m[name], new_v[name] = _adam_small("adam_" + name, g_, w_, m_, v_)

    big = dict(w_in=("w_in", w_in, m_w_in, v_w_in, 128), w_mem_k=("w_mk", w_mem_k, m_w_mem_k, v_w_mem_k, 128),
               w_mem_v=("w_mv", w_mem_v, m_w_mem_v, v_w_mem_v, 128),
               w_branch=("w_br", w_branch, m_w_branch, v_w_branch, 512), w_o=("w_o", w_o, m_w_o, v_w_o, 128),
               w_up=("w_up", w_up, m_w_up, v_w_up, 256), w_down=("w_down", w_down, m_w_down, v_w_down, 128))
    for name, (k, w_, m_, v_, tr) in big.items():
        shape = w_.shape
        flat = (DEPTH, -1, shape[-1])
        rc = [recv[l][k].reshape((N_DEV,) + w_.reshape(flat).shape[1:]) for l in range(DEPTH)]
        outs = _adam_shard("adam_" + name, rc, w_.reshape(flat), m_.reshape(flat), v_.reshape(flat), tr=tr)
        grads[name], deltas[name], new_m[name], new_v[name] = (o.reshape(shape) for o in outs)

    order = ["lower_bounds", "w_in", "conv_w", "hg_norm_w", "w_mem_k", "w_mem_v", "w_branch", "b_gate", "w_o",
             "ln1_g", "ln1_b", "w_up", "w_down", "ln2_g", "ln2_b"]
    return (loss, dcur.reshape(x.shape), *[grads[n] for n in order], *[deltas[n] for n in order],
            *[new_m[n] for n in order], *[new_v[n] for n in order])
```

```python
import functools

import jax
import jax.numpy as jnp
from jax import lax
from jax.experimental import pallas as pl
from jax.experimental.pallas import tpu as pltpu

F32 = jnp.float32
BF16 = jnp.bfloat16

N_DEV = 8
D_MODEL = 1024
DEPTH = 4
MEM_LEN = 256
CONV_K = 3
WIDTH = 512
HEADS = 4
HEAD_DIM = 128
CHUNK = 32
D_FF = 4 * D_MODEL
IN_COLS = 7168
ALPHA = (2.0 * DEPTH) ** 0.25
LN_EPS = 1e-5
RMS_EPS = 1e-6
ADAM_LR = 0.001
ADAM_B1 = 0.9
ADAM_B2 = 0.999
ADAM_EPS = 1e-08
ADAM_WD = 0.01
ADAM_STEP = 10

C_CB, C_CC, C_CH, C_HQ, C_HF, C_HI, C_HG, C_MQ, C_GA = 0, 512, 1024, 1536, 2048, 2560, 3072, 3584, 4096

ROWS_HG = 128
NT_DIMS = (((1,), (1,)), ((), ()))
TN_DIMS = (((0,), (0,)), ((), ()))
MESH = pl.DeviceIdType.MESH


def _dot(a, b):
    return jnp.dot(a, b, preferred_element_type=F32)


def _dot_nt(a, b):
    return lax.dot_general(a, b, NT_DIMS, preferred_element_type=F32)


def _dot_tn(a, b):
    return lax.dot_general(a, b, TN_DIMS, preferred_element_type=F32)


def _sigmoid(x):
    return 1.0 / (1.0 + jnp.exp(-x))


def _params(*sem):
    return pltpu.CompilerParams(dimension_semantics=sem)


def _resident(shape, single=False):
    nd = len(shape)
    if single:
        return pl.BlockSpec(shape, lambda *_: (0,) * nd, pipeline_mode=pl.Buffered(1))
    return pl.BlockSpec(shape, lambda *_: (0,) * nd)


def _my_id():
    return 4 * lax.axis_index("x") + 2 * lax.axis_index("y") + lax.axis_index("c")


class _Exchange:
    def __init__(self, srcs, slicers=None, piece_shapes=None, route="all"):
        self.srcs, self.n, self.route = list(srcs), len(srcs), route
        self.slicers = list(slicers) if slicers else [_whole] * self.n
        any_spec = pl.BlockSpec(memory_space=pl.ANY)
        self.in_specs = [any_spec] * self.n
        self.out_specs = [any_spec] * self.n
        if route == "relay":
            self.out_shape = [jax.ShapeDtypeStruct(a.shape, a.dtype) for a in srcs]
        else:
            self.out_shape = [jax.ShapeDtypeStruct((N_DEV,) + tuple(s), a.dtype) for s, a in zip(piece_shapes, srcs)]
        self.aliased = route == "relay"
        self.scratch = [pltpu.SemaphoreType.DMA((self.n * N_DEV,)), pltpu.SemaphoreType.DMA((self.n * N_DEV,)),
                        pltpu.SemaphoreType.DMA((self.n,))]

    def _peer(self, j, me):
        if self.route == "all":
            return j != me
        return (j != me) & ((j % 2 == lax.axis_index("c")) | (j // 2 == me // 2))

    def _remote(self, ins, outs, sems, k, j, me):
        return pltpu.make_async_remote_copy(
            src_ref=self.slicers[k](ins[k], j), dst_ref=outs[k].at[me],
            send_sem=sems[0].at[k * N_DEV + j], recv_sem=sems[1].at[k * N_DEV + me],
            device_id=(j // 4, (j // 2) % 2, j % 2), device_id_type=MESH)

    def _local(self, ins, outs, sems, k, j, me):
        return pltpu.make_async_copy(self.slicers[k](ins[k], j), outs[k].at[me], sems[2].at[k])

    def _relay(self, outs, sems, k, j):
        sibling = (lax.axis_index("x"), lax.axis_index("y"), 1 - lax.axis_index("c"))
        return pltpu.make_async_remote_copy(
            src_ref=outs[k].at[j], dst_ref=outs[k].at[j], send_sem=sems[0].at[k * N_DEV + j],
            recv_sem=sems[1].at[k * N_DEV + j], device_id=sibling, device_id_type=MESH)

    def _other_chip(self, j, same_core):
        on_my_core = j % 2 == lax.axis_index("c")
        return (on_my_core if same_core else ~on_my_core) & (j // 2 != _my_id() // 2)

    def start(self, ins, outs, sems):
        me = _my_id()
        for k in range(self.n):
            for j in range(N_DEV):
                if self.route == "relay":
                    @pl.when(self._other_chip(j, True))
                    def _():
                        self._relay(outs, sems, k, j).start()
                    continue

                @pl.when(self._peer(j, me))
                def _():
                    self._remote(ins, outs, sems, k, j, me).start()

                @pl.when(j == me)
                def _():
                    self._local(ins, outs, sems, k, j, me).start()

    def wait(self, ins, outs, sems):
        me = _my_id()
        for k in range(self.n):
            for j in range(N_DEV):
                if self.route == "relay":
                    @pl.when(self._other_chip(j, False))
                    def _():
                        self._relay(outs, sems, k, j).wait_recv()

                    @pl.when(self._other_chip(j, True))
                    def _():
                        self._relay(outs, sems, k, j).wait_send()
                    continue

                @pl.when(self._peer(j, me))
                def _():
                    pltpu.make_async_remote_copy(
                        src_ref=self.slicers[k](ins[k], j), dst_ref=outs[k].at[j],
                        send_sem=sems[0].at[k * N_DEV + j], recv_sem=sems[1].at[k * N_DEV + j],
                        device_id=(j // 4, (j // 2) % 2, j % 2), device_id_type=MESH).wait_recv()
                    self._remote(ins, outs, sems, k, j, me).wait_send()

                @pl.when(j == me)
                def _():
                    self._local(ins, outs, sems, k, j, me).wait()


def _carried(exchanges, n_in, n_out):
    c_in = [s for ex in exchanges for s in ex.in_specs]
    c_out = [s for ex in exchanges for s in ex.out_specs]
    shapes = [s for ex in exchanges for s in ex.out_shape]
    sems = [s for ex in exchanges for s in ex.scratch]
    srcs = [a for ex in exchanges for a in ex.srcs]
    aliases, off = {}, 0
    for ex in exchanges:
        if ex.aliased:
            aliases.update({n_in + off + k: n_out + off + k for k in range(ex.n)})
        off += ex.n
    total = off

    def split(refs, n_scr):
        ins, outs = refs[:n_in], refs[n_in + total:n_in + total + n_out]
        rest = refs[n_in + 2 * total + n_out:]
        scr, sem_refs = rest[:n_scr], rest[n_scr:]
        parts, off_ = [], 0
        for i, ex in enumerate(exchanges):
            parts.append((refs[n_in + off_:n_in + off_ + ex.n],
                          refs[n_in + total + n_out + off_:n_in + total + n_out + off_ + ex.n],
                          sem_refs[3 * i:3 * i + 3]))
            off_ += ex.n
        return ins, outs, scr, parts

    return c_in, c_out, shapes, sems, srcs, aliases, split


def _exchange(name, exchanges):
    c_in, c_out, shapes, sems, srcs, aliases, split = _carried(exchanges, 0, 0)

    def body(*refs):
        _, _, _, parts = split(refs, 0)
        for ex, part in zip(exchanges, parts):
            ex.start(*part)
        for ex, part in zip(exchanges, parts):
            ex.wait(*part)

    outs = pl.pallas_call(
        body, name=name, in_specs=c_in, out_specs=c_out, out_shape=shapes, scratch_shapes=sems,
        input_output_aliases=aliases, compiler_params=pltpu.CompilerParams(has_side_effects=True))(*srcs)
    return _per_exchange(exchanges, outs)


def _per_exchange(exchanges, flat):
    out, off = [], 0
    for ex in exchanges:
        out.append(flat[off:off + ex.n])
        off += ex.n
    return out


def _call(body, name, grid, in_specs, out_specs, out_shape, args, scratch=(), sem=None, carry=None):
    n_in, n_out, n_scr = len(in_specs), len(out_specs), len(scratch)
    if not carry:
        outs = pl.pallas_call(body, name=name, grid=grid, in_specs=in_specs, out_specs=out_specs,
                              out_shape=out_shape, scratch_shapes=list(scratch),
                              compiler_params=_params(*sem))(*args)
        return outs, None
    c_in, c_out, shapes, sems, srcs, aliases, split = _carried(carry, n_in, n_out)

    def hosted(*refs):
        ins, outs, scr, parts = split(refs, n_scr)
        first, last = True, True
        for d, size in enumerate(grid):
            first = first & (pl.program_id(d) == 0)
            last = last & (pl.program_id(d) == size - 1)

        @pl.when(first)
        def _():
            for ex, part in zip(carry, parts):
                ex.start(*part)

        body(*ins, *outs, *scr)

        @pl.when(last)
        def _():
            for ex, part in zip(carry, parts):
                ex.wait(*part)

    outs = pl.pallas_call(
        hosted, name=name + "_x", grid=grid, in_specs=list(in_specs) + c_in,
        out_specs=list(out_specs) + c_out, out_shape=list(out_shape) + shapes,
        scratch_shapes=list(scratch) + sems, input_output_aliases=aliases,
        compiler_params=_params(*(["arbitrary"] * len(grid))))(*args, *srcs)
    return outs[:n_out], _per_exchange(carry, outs[n_out:])


def _whole(ref, j):
    return ref


def _cols(width):
    return lambda ref, j: ref.at[(slice(None),) * (len(ref.shape) - 1) + (pl.ds(j * width, width),)]


def _rows(height):
    return lambda ref, j: ref.at[pl.ds(j * height, height)]


def _mm_nn(name, a, w, *, tm, tn, out_dtype, relu2=False, carry=None):
    t, k = a.shape
    n = w.shape[1]

    def body(a_ref, w_ref, o_ref):
        acc = _dot(a_ref[...].astype(BF16), w_ref[...])
        if relu2:
            r = jnp.maximum(acc, 0.0)
            acc = r * r
        o_ref[...] = acc.astype(out_dtype)

    outs, recv = _call(
        body, name, (t // tm, n // tn),
        [pl.BlockSpec((tm, k), lambda i, j: (i, 0)), pl.BlockSpec((k, tn), lambda i, j: (0, j))],
        [pl.BlockSpec((tm, tn), lambda i, j: (i, j))], [jax.ShapeDtypeStruct((t, n), out_dtype)], (a, w),
        sem=("parallel", "parallel"), carry=carry)
    return outs[0] if carry is None else (outs[0], recv)


def _in_proj(a, w, *, tm, carry=None):
    t, k = a.shape
    tn = IN_COLS // 4
    f_tile, f_off = C_HF // tn, C_HF % tn

    def body(a_ref, w_ref, o_ref, f_ref):
        acc = _dot(a_ref[...], w_ref[...])
        o_ref[...] = acc.astype(BF16)

        @pl.when(pl.program_id(1) == f_tile)
        def _():
            f_ref[...] = acc[:, f_off:f_off + WIDTH]

    outs, recv = _call(
        body, "in_proj", (t // tm, IN_COLS // tn),
        [pl.BlockSpec((tm, k), lambda i, j: (i, 0)), pl.BlockSpec((k, tn), lambda i, j: (0, j))],
        [pl.BlockSpec((tm, tn), lambda i, j: (i, j)), pl.BlockSpec((tm, WIDTH), lambda i, j: (i, 0))],
        [jax.ShapeDtypeStruct((t, IN_COLS), BF16), jax.ShapeDtypeStruct((t, WIDTH), F32)], (a, w),
        sem=("parallel", "arbitrary"), carry=carry)
    return outs if carry is None else (outs, recv)


def _linear_ln(name, a, w, resid, g, b, *, tm, carry=None):
    t, k = a.shape
    halves = [slice(0, tm // 2), slice(tm // 2, tm)] if k > D_MODEL else [slice(0, tm)]

    def body(a_ref, w_ref, r_ref, g_ref, b_ref, z_ref, x_ref, xb_ref):
        z = ALPHA * _Lanes(r_ref[s, :] for s in halves) + _ldot(_Lanes(a_ref[s, :] for s in halves), w_ref[...])
        zc = z - _mean(z, axis=-1, keepdims=True)
        y = zc * _rsqrt(_mean(zc * zc, axis=-1, keepdims=True) + LN_EPS) * g_ref[...] + b_ref[...]
        for s, zz, yy in zip(halves, z.xs, y.xs):
            z_ref[s, :] = zz
            x_ref[s, :] = yy
            xb_ref[s, :] = yy.astype(BF16)

    row = pl.BlockSpec((tm, D_MODEL), lambda i: (i, 0))
    outs, recv = _call(
        body, name, (t // tm,),
        [pl.BlockSpec((tm, k), lambda i: (i, 0)), _resident((k, D_MODEL)), row,
         _resident((1, D_MODEL)), _resident((1, D_MODEL))],
        [row, row, row],
        [jax.ShapeDtypeStruct((t, D_MODEL), F32)] * 2 + [jax.ShapeDtypeStruct((t, D_MODEL), BF16)],
        (a, w, resid, g, b), sem=("parallel",), carry=carry)
    return outs if carry is None else (outs, recv)


def _ln_bwd_mm_nt(name, dy, z, g, w, h=None, left=None, *, tm, tn, carry=None):
    t = dy.shape[0]
    n = w.shape[0]
    halves = [slice(0, tm // 2), slice(tm // 2, tm)]
    last = t // tm - 1

    def body(*refs):
        refs = list(refs)
        dy_ref, z_ref, g_ref, w_ref = refs[:4]
        del refs[:4]
        h_ref = refs.pop(0) if h is not None else None
        left_ref = refs.pop(0) if left is not None else None
        dz_ref, dzb_ref, o_ref, dg_ref, db_ref = refs[:5]
        gw_ref, acc_ref = (refs[5], refs[6]) if left is not None else (None, None)

        @pl.when(pl.program_id(0) == 0)
        def _():
            dg_ref[...] = jnp.zeros_like(dg_ref)
            db_ref[...] = jnp.zeros_like(db_ref)
            if left is not None:
                acc_ref[...] = jnp.zeros_like(acc_ref)

        zv = _Lanes(z_ref[s, :] for s in halves)
        dyv = _Lanes(dy_ref[s, :] for s in halves)
        mu = _mean(zv, axis=-1, keepdims=True)
        zc = zv - mu
        rstd = _rsqrt(_mean(zc * zc, axis=-1, keepdims=True) + LN_EPS)
        xh = zc * rstd
        gdy = dyv * g_ref[...]
        m1 = _mean(gdy, axis=-1, keepdims=True)
        m2 = _mean(gdy * xh, axis=-1, keepdims=True)
        dz = rstd * (gdy - m1 - xh * m2)
        dz_b = dz.astype(BF16)
        for s, a, a_b in zip(halves, dz.xs, dz_b.xs):
            dz_ref[s, :] = a
            dzb_ref[s, :] = a_b
        dg_ref[...] += _sum(dyv * xh, axis=0, keepdims=True).total()
        db_ref[...] += _sum(dyv, axis=0, keepdims=True).total()
        for c in range(n // tn):
            cols = slice(c * tn, (c + 1) * tn)
            acc = _ldot_nt(dz_b, w_ref[cols, :])
            if h is not None:
                acc = acc * (2.0 * _sqrt(_Lanes(h_ref[s, cols] for s in halves).astype(F32)))
            for s, a in zip(halves, acc.xs):
                o_ref[s, cols] = a.astype(BF16)
        if left is not None:
            acc_ref[...] += _ldot_tn(_Lanes(left_ref[s, :] for s in halves), dz_b).total()

            @pl.when(pl.program_id(0) == last)
            def _():
                gw_ref[...] = acc_ref[...].astype(BF16)

    row = pl.BlockSpec((tm, D_MODEL), lambda i: (i, 0))
    vec = _resident((1, D_MODEL))
    tile = pl.BlockSpec((tm, n), lambda i: (i, 0))
    in_specs = [row, row, vec, _resident((n, D_MODEL))]
    args = [dy, z, g, w]
    out_specs = [row, row, tile, vec, vec]
    out_shape = [jax.ShapeDtypeStruct((t, D_MODEL), F32), jax.ShapeDtypeStruct((t, D_MODEL), BF16),
                 jax.ShapeDtypeStruct((t, n), BF16), jax.ShapeDtypeStruct((1, D_MODEL), F32),
                 jax.ShapeDtypeStruct((1, D_MODEL), F32)]
    scratch = []
    if h is not None:
        in_specs.append(tile)
        args.append(h)
    if left is not None:
        m = left.shape[1]
        in_specs.append(pl.BlockSpec((tm, m), lambda i: (i, 0)))
        args.append(left)
        out_specs.append(_resident((m, D_MODEL)))
        out_shape.append(jax.ShapeDtypeStruct((m, D_MODEL), BF16))
        scratch.append(pltpu.VMEM((m, D_MODEL), F32))
    outs, recv = _call(body, name, (t // tm,), in_specs, out_specs, out_shape, args, scratch=scratch,
                       sem=("arbitrary",), carry=carry)
    return outs if carry is None else (outs, recv)


def _mm_tn(name, a, b, *, tk, tmo, tno, carry=None):
    t, m = a.shape
    n = b.shape[1]
    nk = t // tk

    def body(a_ref, b_ref, o_ref, acc_ref):
        k = pl.program_id(2)
        p = _dot_tn(a_ref[...].astype(BF16), b_ref[...].astype(BF16))

        @pl.when(k == 0)
        def _():
            acc_ref[...] = p

        @pl.when(k > 0)
        def _():
            acc_ref[...] += p

        @pl.when(k == nk - 1)
        def _():
            o_ref[...] = acc_ref[...].astype(BF16)

    outs, recv = _call(
        body, name, (m // tmo, n // tno, nk),
        [pl.BlockSpec((tk, tmo), lambda i, j, k: (k, i)), pl.BlockSpec((tk, tno), lambda i, j, k: (k, j))],
        [pl.BlockSpec((tmo, tno), lambda i, j, k: (i, j))], [jax.ShapeDtypeStruct((m, n), BF16)], (a, b),
        scratch=[pltpu.VMEM((tmo, tno), F32)], sem=("parallel", "parallel", "arbitrary"), carry=carry)
    return outs[0] if carry is None else (outs[0], recv)


def _mm_nt_sum(name, pieces, offsets, w, resid, *, tm, carry=None):
    t = resid.shape[0]
    widths = [p.shape[1] for p in pieces]
    n_p = len(pieces)

    def body(*refs):
        p_refs, w_ref, r_ref, o_ref = refs[:n_p], refs[n_p], refs[n_p + 1], refs[n_p + 2]
        acc = ALPHA * r_ref[...]
        for p_ref, off, wd in zip(p_refs, offsets, widths):
            acc = acc + _dot_nt(p_ref[...], w_ref[:, off:off + wd])
        o_ref[...] = acc

    row = pl.BlockSpec((tm, D_MODEL), lambda i: (i, 0))
    outs, recv = _call(
        body, name, (t // tm,),
        [pl.BlockSpec((tm, wd), lambda i: (i, 0)) for wd in widths] + [_resident(w.shape, single=True), row],
        [row], [jax.ShapeDtypeStruct((t, D_MODEL), F32)], (*pieces, w, resid), sem=("parallel",), carry=carry)
    return outs[0] if carry is None else (outs[0], recv)


def _chunk_mask(rows):
    r = lax.broadcasted_iota(jnp.int32, (rows, rows), 0)
    c = lax.broadcasted_iota(jnp.int32, (rows, rows), 1)
    return ((r // CHUNK) == (c // CHUNK)) & (c <= r)


class _Lanes:
    def __init__(self, xs):
        self.xs = list(xs)

    def _with(self, other, f):
        if isinstance(other, _Lanes):
            return _Lanes([f(a, b) for a, b in zip(self.xs, other.xs)])
        return _Lanes([f(a, other) for a in self.xs])

    def __add__(self, o):
        return self._with(o, lambda a, b: a + b)

    def __radd__(self, o):
        return self._with(o, lambda a, b: b + a)

    def __sub__(self, o):
        return self._with(o, lambda a, b: a - b)

    def __rsub__(self, o):
        return self._with(o, lambda a, b: b - a)

    def __mul__(self, o):
        return self._with(o, lambda a, b: a * b)

    def __rmul__(self, o):
        return self._with(o, lambda a, b: b * a)

    def __truediv__(self, o):
        return self._with(o, lambda a, b: a / b)

    def __rtruediv__(self, o):
        return self._with(o, lambda a, b: b / a)

    def __neg__(self):
        return _Lanes([-a for a in self.xs])

    def __ge__(self, o):
        return self._with(o, lambda a, b: a >= b)

    def __getitem__(self, idx):
        return _Lanes([a[idx] for a in self.xs])

    def astype(self, dtype):
        return _Lanes([a.astype(dtype) for a in self.xs])

    def total(self):
        return functools.reduce(lambda a, b: a + b, self.xs)


def _lift(f):
    def g(*args, **kw):
        lanes = [a for a in args if isinstance(a, _Lanes)]
        if not lanes:
            return f(*args, **kw)
        return _Lanes([f(*[a.xs[i] if isinstance(a, _Lanes) else a for a in args], **kw)
                       for i in range(len(lanes[0].xs))])
    return g


def _concat(parts, axis):
    if isinstance(parts[0], _Lanes):
        return _Lanes([jnp.concatenate([p.xs[i] for p in parts], axis=axis) for i in range(len(parts[0].xs))])
    return jnp.concatenate(parts, axis=axis)


_exp, _log, _abs, _sqrt, _where = _lift(jnp.exp), _lift(jnp.log), _lift(jnp.abs), _lift(jnp.sqrt), _lift(jnp.where)
_sum, _mean, _rsqrt, _bcast = _lift(jnp.sum), _lift(jnp.mean), _lift(lax.rsqrt), _lift(jnp.broadcast_to)
_ldot, _ldot_nt, _ldot_tn = _lift(_dot), _lift(_dot_nt), _lift(_dot_tn)
_lsigmoid = _lift(_sigmoid)


def _mask_sum(mask_b, x, transpose=False):
    f = _ldot_tn if transpose else _ldot
    hi = x.astype(BF16)
    lo = (x - hi.astype(F32)).astype(BF16)
    return f(mask_b, hi) + f(mask_b, lo)


def _chunk_row(x, pos, rows):
    nc = rows // CHUNK

    def one(a):
        a3 = a.reshape(nc, CHUNK, HEAD_DIM)
        return jnp.broadcast_to(a3[:, pos:pos + 1, :], (nc, CHUNK, HEAD_DIM)).reshape(rows, HEAD_DIM)

    return _lift(one)(x)


def _chunk_total(x, rows):
    nc = rows // CHUNK

    def one(a):
        tot = jnp.sum(a.reshape(nc, CHUNK, HEAD_DIM), axis=1, keepdims=True)
        return jnp.broadcast_to(tot, (nc, CHUNK, HEAD_DIM)).reshape(rows, HEAD_DIM)

    return _lift(one)(x)


def _sigmoid_pair(x):
    e = _exp(-_abs(x))
    big = 1.0 / (1.0 + e)
    small = e * big
    pos = x >= 0.0
    return _where(pos, big, small), _where(pos, small, big)


def _hg_gates(q_raw, fl, lb, rows, mask):
    tri = mask.astype(BF16)
    sg, sg_neg = _sigmoid_pair(fl)
    forget = lb + (1.0 - lb) * sg
    k = (1.0 - lb) * sg_neg
    sq = _lsigmoid(q_raw)
    qs = q_raw * sq
    bc = _mask_sum(tri, _log(forget))
    bref = _chunk_row(bc, CHUNK // 2 - 1, rows)
    blast = _chunk_row(bc, CHUNK - 1, rows)
    return dict(tri=tri, sg=sg, sg_neg=sg_neg, forget=forget, k=k, sq=sq, qs=qs,
                e_a=_exp(bc - bref), e_b=_exp(bref - bc), e_q=_exp(bc), e_k=_exp(blast - bc),
                dec=_exp(blast))


HG_GROUP = 4


def _hg_lanes(bsz):
    return [(hh, slice(hh * HEAD_DIM, (hh + 1) * HEAD_DIM), b) for hh in range(HG_GROUP) for b in range(bsz)]


def _hg_read(ref, lanes):
    return _Lanes(ref[b, :, cs].astype(F32) for _, cs, b in lanes)


def _hg_write(ref, lanes, val, offset=0):
    for (_, cs, b), a in zip(lanes, val.xs):
        ref[b, :, offset + cs.start:offset + cs.stop] = a


def _hgrn_fwd(proj, hf, lb, nw, *, bsz, seq, carry=None):
    rows = min(ROWS_HG, seq)
    nt = seq // rows
    nc = rows // CHUNK
    t = bsz * seq

    lanes = _hg_lanes(bsz)

    def body(q_ref, f_ref, v_ref, g_ref, lb_ref, nw_ref, y_ref, o_ref, st_ref, s_scr):
        @pl.when(pl.program_id(1) == 0)
        def _():
            s_scr[...] = jnp.zeros_like(s_scr)

        mask = _chunk_mask(rows)
        lb_v = _Lanes(lb_ref[:, cs] for _, cs, _ in lanes)
        gt = _hg_gates(_hg_read(q_ref, lanes), _hg_read(f_ref, lanes), lb_v, rows, mask)
        v_b = _hg_read(v_ref, lanes).astype(BF16)
        a_b = (gt["qs"] * gt["e_a"]).astype(BF16)
        b_b = (gt["k"] * gt["e_b"]).astype(BF16)
        qi_b = (gt["qs"] * gt["e_q"]).astype(BF16)
        ko_b = (gt["k"] * gt["e_k"]).astype(BF16)
        scores = _where(mask, _ldot_nt(a_b, b_b), 0.0)
        o_intra = _ldot(scores.astype(BF16), v_b)

        s = _Lanes(s_scr[i] for i in range(len(lanes)))
        parts = []
        for n in range(nc):
            sl = slice(n * CHUNK, (n + 1) * CHUNK)
            s_b = s.astype(BF16)
            for (hh, _, b), a in zip(lanes, s_b.xs):
                st_ref[hh, b, n] = a
            parts.append(_ldot_nt(qi_b[sl], s_b))
            s = s * gt["dec"][n * CHUNK:n * CHUNK + 1] + _ldot_tn(v_b[sl], ko_b[sl])
        for i, a in enumerate(s.xs):
            s_scr[i] = a
        o = o_intra + _concat(parts, 0)
        _hg_write(o_ref, lanes, o)
        r = _rsqrt(_mean(o * o, axis=-1, keepdims=True) + RMS_EPS)
        g = _hg_read(g_ref, lanes)
        _hg_write(y_ref, lanes, (o * r * nw_ref[...] * (g * _lsigmoid(g))).astype(BF16))

    wide = HG_GROUP * HEAD_DIM

    def col(base):
        return pl.BlockSpec((bsz, rows, wide), lambda h, j: (0, j, base // wide + h))

    out_tile = pl.BlockSpec((bsz, rows, wide), lambda h, j: (0, j, h))
    p3 = proj.reshape(bsz, seq, IN_COLS)
    outs, recv = _call(
        body, "hgrn_fwd", (HEADS // HG_GROUP, nt),
        [col(C_HQ), out_tile, col(C_HI), col(C_HG),
         pl.BlockSpec((1, wide), lambda h, j: (0, h)), _resident((1, HEAD_DIM))],
        [out_tile, out_tile,
         pl.BlockSpec((HG_GROUP, bsz, nc, HEAD_DIM, HEAD_DIM), lambda h, j: (h, 0, j, 0, 0))],
        [jax.ShapeDtypeStruct((bsz, seq, WIDTH), BF16), jax.ShapeDtypeStruct((bsz, seq, WIDTH), F32),
         jax.ShapeDtypeStruct((HEADS, bsz, seq // CHUNK, HEAD_DIM, HEAD_DIM), BF16)],
        (p3, hf.reshape(bsz, seq, WIDTH), p3, p3, lb, nw),
        scratch=[pltpu.VMEM((len(lanes), HEAD_DIM, HEAD_DIM), F32)],
        sem=("parallel", "arbitrary"), carry=carry)
    outs = [outs[0].reshape(t, WIDTH), outs[1].reshape(t, WIDTH), outs[2]]
    return outs if carry is None else (outs, recv)


def _hgrn_bwd(proj, hf, lb, nw, o_pre, states, dy, *, bsz, seq, carry=None):
    rows = min(ROWS_HG, seq)
    nt = seq // rows
    nc = rows // CHUNK
    t = bsz * seq
    lanes = _hg_lanes(bsz)

    def body(q_ref, f_ref, v_ref, g_ref, lb_ref, nw_ref, o_ref, st_ref, dy_ref, dh_ref, dlb_ref, dnw_ref, ds_scr):
        h, j = pl.program_id(0), pl.program_id(1)

        @pl.when(j == 0)
        def _():
            ds_scr[...] = jnp.zeros_like(ds_scr)
            dlb_ref[...] = jnp.zeros_like(dlb_ref)

        @pl.when((h == 0) & (j == 0))
        def _():
            dnw_ref[...] = jnp.zeros_like(dnw_ref)

        mask = _chunk_mask(rows)
        q_raw = _hg_read(q_ref, lanes)
        lb_v = _Lanes(lb_ref[:, cs] for _, cs, _ in lanes)
        gt = _hg_gates(q_raw, _hg_read(f_ref, lanes), lb_v, rows, mask)
        v_b = _hg_read(v_ref, lanes).astype(BF16)
        a_f = gt["qs"] * gt["e_a"]
        b_f = gt["k"] * gt["e_b"]
        qi_f = gt["qs"] * gt["e_q"]
        ko_f = gt["k"] * gt["e_k"]
        a_b, b_b, qi_b, ko_b = a_f.astype(BF16), b_f.astype(BF16), qi_f.astype(BF16), ko_f.astype(BF16)

        o = _hg_read(o_ref, lanes)
        nw_v = nw_ref[...]
        g = _hg_read(g_ref, lanes)
        dyv = _hg_read(dy_ref, lanes)
        r = _rsqrt(_mean(o * o, axis=-1, keepdims=True) + RMS_EPS)
        sgg = _lsigmoid(g)
        d_g = dyv * (o * r * nw_v) * (sgg * (1.0 + g * (1.0 - sgg)))
        d_on = dyv * (g * sgg)
        dnw_ref[...] += _sum(d_on * o * r, axis=0, keepdims=True).total()
        tt = d_on * nw_v
        d_o = r * tt - o * (r * r * r) * _mean(tt * o, axis=-1, keepdims=True)
        do_b = d_o.astype(BF16)

        sc_b = _where(mask, _ldot_nt(a_b, b_b), 0.0).astype(BF16)
        dsc_b = _where(mask, _ldot_nt(do_b, v_b), 0.0).astype(BF16)
        d_v = _ldot_tn(sc_b, do_b)
        d_a = _ldot(dsc_b, b_b)
        d_bm = _ldot_tn(dsc_b, a_b)

        ds = _Lanes(ds_scr[i] for i in range(len(lanes)))
        dqi_parts, dko_parts, dvi_parts, ddec_parts = [None] * nc, [None] * nc, [None] * nc, [None] * nc
        for n in reversed(range(nc)):
            sl = slice(n * CHUNK, (n + 1) * CHUNK)
            dec_n = gt["dec"][n * CHUNK:n * CHUNK + 1]
            ds_b = ds.astype(BF16)
            s_n = _Lanes(st_ref[hh, b, n] for hh, _, b in lanes)
            dqi_parts[n] = _ldot(do_b[sl], s_n)
            dko_parts[n] = _ldot(v_b[sl], ds_b)
            dvi_parts[n] = _ldot_nt(ko_b[sl], ds_b)
            d_dec = _sum(ds * s_n.astype(F32), axis=0, keepdims=True)
            ddec_parts[n] = _bcast(d_dec * dec_n, (CHUNK, HEAD_DIM))
            ds = ds * dec_n + _ldot_tn(do_b[sl], qi_b[sl])
        for i, a in enumerate(ds.xs):
            ds_scr[i] = a
        d_qi = _concat(dqi_parts, 0)
        d_ko = _concat(dko_parts, 0)
        d_v = d_v + _concat(dvi_parts, 0)

        d_qs = d_a * gt["e_a"] + d_qi * gt["e_q"]
        d_k = d_bm * gt["e_b"] + d_ko * gt["e_k"]
        t_a, t_b, t_q, t_k = d_a * a_f, d_bm * b_f, d_qi * qi_f, d_ko * ko_f
        d_bref = _chunk_total(t_b - t_a, rows)
        d_blast = _chunk_total(t_k, rows) + _concat(ddec_parts, 0)
        pos = lax.broadcasted_iota(jnp.int32, (rows, HEAD_DIM), 0) % CHUNK
        d_bc = (t_a - t_b + t_q - t_k + _where(pos == CHUNK // 2 - 1, d_bref, 0.0)
                + _where(pos == CHUNK - 1, d_blast, 0.0))
        d_logf = _mask_sum(gt["tri"], d_bc, transpose=True)

        sg, sg_neg = gt["sg"], gt["sg_neg"]
        inv_f = 1.0 / gt["forget"]
        common = (1.0 - lb_v) * sg * sg_neg
        d_fl = common * (d_logf * inv_f - d_k)
        d_lb = _sum(sg_neg * (d_logf * inv_f - d_k), axis=0, keepdims=True)
        for (_, cs, _), a in zip(lanes, d_lb.xs):
            dlb_ref[:, cs] += a
        sq = gt["sq"]
        _hg_write(dh_ref, lanes, (d_qs * (sq * (1.0 + q_raw * (1.0 - sq)))).astype(BF16), 0)
        _hg_write(dh_ref, lanes, d_fl.astype(BF16), WIDTH)
        _hg_write(dh_ref, lanes, d_v.astype(BF16), 2 * WIDTH)
        _hg_write(dh_ref, lanes, d_g.astype(BF16), 3 * WIDTH)

    assert HG_GROUP == HEADS, "the combined gradient block needs all heads in one grid step"
    wide = HG_GROUP * HEAD_DIM

    def col(base):
        return pl.BlockSpec((bsz, rows, wide), lambda h, j: (0, nt - 1 - j, base // wide + h))

    tile = pl.BlockSpec((bsz, rows, wide), lambda h, j: (0, nt - 1 - j, h))
    head_vec = pl.BlockSpec((1, wide), lambda h, j: (0, h))
    p3 = proj.reshape(bsz, seq, IN_COLS)
    outs, recv = _call(
        body, "hgrn_bwd", (HEADS // HG_GROUP, nt),
        [col(C_HQ), tile, col(C_HI), col(C_HG), head_vec, _resident((1, HEAD_DIM)), tile,
         pl.BlockSpec((HG_GROUP, bsz, nc, HEAD_DIM, HEAD_DIM), lambda h, j: (h, 0, nt - 1 - j, 0, 0)), tile],
        [pl.BlockSpec((bsz, rows, 4 * WIDTH), lambda h, j: (0, nt - 1 - j, 0)), head_vec, _resident((1, HEAD_DIM))],
        [jax.ShapeDtypeStruct((bsz, seq, 4 * WIDTH), BF16), jax.ShapeDtypeStruct((1, WIDTH), F32),
         jax.ShapeDtypeStruct((1, HEAD_DIM), F32)],
        (p3, hf.reshape(bsz, seq, WIDTH), p3, p3, lb, nw, o_pre.reshape(bsz, seq, WIDTH), states,
         dy.reshape(bsz, seq, WIDTH)),
        scratch=[pltpu.VMEM((len(lanes), HEAD_DIM, HEAD_DIM), F32)],
        sem=("arbitrary", "arbitrary"), carry=carry)
    outs = [outs[0].reshape(t, 4 * WIDTH), outs[1], outs[2]]
    return outs if carry is None else (outs, recv)


def _mem_kv(mem2d, w_k, w_v):
    rows = mem2d.shape[0]

    def body(m_ref, wk_ref, wv_ref, k_ref, v_ref):
        m_b = m_ref[...].astype(BF16)
        k_ref[...] = _dot(m_b, wk_ref[...]).astype(BF16)
        v_ref[...] = _dot(m_b, wv_ref[...]).astype(BF16)

    return pl.pallas_call(
        body, name="mem_kv", grid=(rows // MEM_LEN,),
        in_specs=[pl.BlockSpec((MEM_LEN, D_MODEL), lambda i: (i, 0)), _resident((D_MODEL, WIDTH)),
                  _resident((D_MODEL, WIDTH))],
        out_specs=[pl.BlockSpec((MEM_LEN, WIDTH), lambda i: (i, 0))] * 2,
        out_shape=[jax.ShapeDtypeStruct((rows, WIDTH), BF16)] * 2,
        compiler_params=_params("parallel"),
    )(mem2d, w_k, w_v)


def _softmax_rows(s):
    m = _lift(jnp.max)(s, axis=-1, keepdims=True)
    e = _exp(s - m)
    return e / _sum(e, axis=-1, keepdims=True)


def _attn_fwd(proj, mk, mv, *, tm, seq):
    t = proj.shape[0]
    per_b = seq // tm
    scale = HEAD_DIM ** -0.5

    def body(q_ref, k_ref, v_ref, y_ref):
        heads = [slice(h * HEAD_DIM, (h + 1) * HEAD_DIM) for h in range(HEADS)]
        q_b = _Lanes(q_ref[:, sl] for sl in heads).astype(BF16)
        p = _softmax_rows(_ldot_nt(q_b, _Lanes(k_ref[:, sl] for sl in heads)) * scale)
        out = _ldot(p.astype(BF16), _Lanes(v_ref[:, sl] for sl in heads))
        y_ref[...] = jnp.concatenate(out.xs, axis=-1).astype(BF16)

    kv = pl.BlockSpec((MEM_LEN, WIDTH), lambda i: (i // per_b, 0))
    return pl.pallas_call(
        body, name="attn_fwd", grid=(t // tm,),
        in_specs=[pl.BlockSpec((tm, WIDTH), lambda i: (i, C_MQ // WIDTH)), kv, kv],
        out_specs=pl.BlockSpec((tm, WIDTH), lambda i: (i, 0)),
        out_shape=jax.ShapeDtypeStruct((t, WIDTH), BF16),
        compiler_params=_params("parallel"),
    )(proj, mk, mv)


def _attn_bwd(proj, mk, mv, dy, x_b, *, tm, seq):
    t = proj.shape[0]
    per_b = seq // tm
    scale = HEAD_DIM ** -0.5
    last = t // tm - 1

    def body(q_ref, k_ref, v_ref, dy_ref, x_ref, dq_ref, dk_ref, dv_ref, gw_ref, acc_ref):
        i = pl.program_id(0)

        @pl.when(i % per_b == 0)
        def _():
            dk_ref[...] = jnp.zeros_like(dk_ref)
            dv_ref[...] = jnp.zeros_like(dv_ref)

        @pl.when(i == 0)
        def _():
            acc_ref[...] = jnp.zeros_like(acc_ref)

        heads = [slice(h * HEAD_DIM, (h + 1) * HEAD_DIM) for h in range(HEADS)]
        q_b = _Lanes(q_ref[:, sl] for sl in heads).astype(BF16)
        k_b, v_b = _Lanes(k_ref[:, sl] for sl in heads), _Lanes(v_ref[:, sl] for sl in heads)
        p = _softmax_rows(_ldot_nt(q_b, k_b) * scale)
        dy_b = _Lanes(dy_ref[:, sl] for sl in heads).astype(BF16)
        dp = _ldot_nt(dy_b, v_b)
        d_v = _ldot_tn(p.astype(BF16), dy_b)
        ds_b = (p * (dp - _sum(dp * p, axis=-1, keepdims=True)) * scale).astype(BF16)
        dq_b = jnp.concatenate(_ldot(ds_b, k_b).xs, axis=-1).astype(BF16)
        dq_ref[...] = dq_b
        dk_ref[...] += jnp.concatenate(_ldot_tn(ds_b, q_b).xs, axis=-1)
        dv_ref[...] += jnp.concatenate(d_v.xs, axis=-1)
        acc_ref[...] += _dot_tn(x_ref[...], dq_b)

        @pl.when(i == last)
        def _():
            gw_ref[...] = acc_ref[...].astype(BF16)

    kv = pl.BlockSpec((MEM_LEN, WIDTH), lambda i: (i // per_b, 0))
    tile = pl.BlockSpec((tm, WIDTH), lambda i: (i, 0))
    n_mem = mk.shape[0]
    return pl.pallas_call(
        body, name="attn_bwd", grid=(t // tm,),
        in_specs=[pl.BlockSpec((tm, WIDTH), lambda i: (i, C_MQ // WIDTH)), kv, kv, tile,
                  pl.BlockSpec((tm, D_MODEL), lambda i: (i, 0))],
        out_specs=[tile, kv, kv, _resident((D_MODEL, WIDTH))],
        out_shape=[jax.ShapeDtypeStruct((t, WIDTH), BF16), jax.ShapeDtypeStruct((n_mem, WIDTH), F32),
                   jax.ShapeDtypeStruct((n_mem, WIDTH), F32), jax.ShapeDtypeStruct((D_MODEL, WIDTH), BF16)],
        scratch_shapes=[pltpu.VMEM((D_MODEL, WIDTH), F32)],
        compiler_params=_params("arbitrary"),
    )(proj, mk, mv, dy, x_b)


HALO = 16


def _shift_down(u, halo, k, row):
    out = pltpu.roll(u, k, 0)
    for m in range(k):
        out = jnp.where(row == m, halo[HALO - k + m:HALO - k + m + 1, :], out)
    return out


def _shift_up(u, halo, k, row, tm):
    out = pltpu.roll(u, tm - k, 0)
    for m in range(k):
        out = jnp.where(row == tm - k + m, halo[m:m + 1, :], out)
    return out


def _merge_fwd(proj, y_b, y_c, conv_w, w_branch, b_gate, *, tm, seq, carry=None):
    t = proj.shape[0]
    per_b = seq // tm
    hb = tm // HALO

    def body(cb_ref, cc_ref, ch_ref, cch_ref, chh_ref, ga_ref, gb_ref, gc_ref, yb_ref, yc_ref, cw_ref, wb_ref,
             bg_ref, ya_ref, pa_ref, pb_ref, pc_ref, mg_ref, sa_ref, sb_ref, sc_ref):
        i = pl.program_id(0)
        row = lax.broadcasted_iota(jnp.int32, (tm, WIDTH), 0)
        u = cc_ref[...].astype(F32) * ch_ref[...].astype(F32)
        halo = jnp.where(i % per_b == 0, 0.0, cch_ref[...].astype(F32) * chh_ref[...].astype(F32))
        cw = cw_ref[...]
        y = cw[0:1] * _shift_down(u, halo, 2, row) + cw[1:2] * _shift_down(u, halo, 1, row) + cw[2:3] * u
        ya_b = (cb_ref[...].astype(F32) * y).astype(BF16)
        ya_ref[...] = ya_b
        merged = None
        for idx, (y_in, g_ref, p_ref, s_ref) in enumerate(((ya_b, ga_ref, pa_ref, sa_ref),
                                                            (yb_ref[...], gb_ref, pb_ref, sb_ref),
                                                            (yc_ref[...], gc_ref, pc_ref, sc_ref))):
            p = _dot(y_in, wb_ref[idx])
            p_ref[...] = p.astype(BF16)
            sg = _sigmoid(g_ref[...].astype(F32) + bg_ref[:, idx * D_MODEL:(idx + 1) * D_MODEL])
            s_ref[...] = sg.astype(BF16)
            term = sg * p
            merged = term if merged is None else merged + term
        mg_ref[...] = merged.astype(BF16)

    def half(c):
        return pl.BlockSpec((tm, WIDTH), lambda i: (i, c // WIDTH))

    def prev(c):
        return pl.BlockSpec((HALO, WIDTH), lambda i: (jnp.maximum(i * hb - 1, 0), c // WIDTH))

    def gate(k):
        return pl.BlockSpec((tm, D_MODEL), lambda i: (i, C_GA // D_MODEL + k))

    tile512 = pl.BlockSpec((tm, WIDTH), lambda i: (i, 0))
    tile1k = pl.BlockSpec((tm, D_MODEL), lambda i: (i, 0))
    outs, recv = _call(
        body, "merge_fwd", (t // tm,),
        [half(C_CB), half(C_CC), half(C_CH), prev(C_CC), prev(C_CH), gate(0), gate(1), gate(2),
         tile512, tile512, _resident((CONV_K, WIDTH)), _resident((3, WIDTH, D_MODEL)), _resident((1, 3 * D_MODEL))],
        [tile512] + [tile1k] * 7,
        [jax.ShapeDtypeStruct((t, WIDTH), BF16)] + [jax.ShapeDtypeStruct((t, D_MODEL), BF16)] * 7,
        (proj, proj, proj, proj, proj, proj, proj, proj, y_b, y_c, conv_w, w_branch, b_gate),
        sem=("parallel",), carry=carry)
    return outs if carry is None else (outs, recv)


def _merge_bwd(dmerged, projections, gates, branch_in, w_branch, *, tm):
    t = dmerged.shape[0]
    last = t // tm - 1

    def body(dm_ref, pa_ref, pb_ref, pc_ref, sa_ref, sb_ref, sc_ref, ya_ref, yb_ref, yc_ref, wb_ref,
             dgt_ref, dya_ref, dyb_ref, dyc_ref, dbg_ref, gw_ref, acc_ref):
        i = pl.program_id(0)

        @pl.when(i == 0)
        def _():
            dbg_ref[...] = jnp.zeros_like(dbg_ref)
            acc_ref[...] = jnp.zeros_like(acc_ref)

        dm = dm_ref[...].astype(F32)
        for idx, (p_ref, s_ref, y_ref, dy_ref) in enumerate(((pa_ref, sa_ref, ya_ref, dya_ref),
                                                             (pb_ref, sb_ref, yb_ref, dyb_ref),
                                                             (pc_ref, sc_ref, yc_ref, dyc_ref))):
            cols = slice(idx * D_MODEL, (idx + 1) * D_MODEL)
            sg = s_ref[...].astype(F32)
            dp = dm * sg
            dp_b = dp.astype(BF16)
            dgate = dp * p_ref[...].astype(F32) * (1.0 - sg)
            dgt_ref[:, cols] = dgate.astype(BF16)
            dbg_ref[:, cols] += jnp.sum(dgate, axis=0, keepdims=True)
            dy_ref[...] = _dot_nt(dp_b, wb_ref[idx]).astype(BF16)
            acc_ref[idx] += _dot_tn(y_ref[...], dp_b)

        @pl.when(i == last)
        def _():
            gw_ref[...] = acc_ref[...].astype(BF16)

    tile512 = pl.BlockSpec((tm, WIDTH), lambda i: (i, 0))
    tile1k = pl.BlockSpec((tm, D_MODEL), lambda i: (i, 0))
    return pl.pallas_call(
        body, name="merge_bwd", grid=(t // tm,),
        in_specs=[tile1k] * 7 + [tile512] * 3 + [_resident((3, WIDTH, D_MODEL))],
        out_specs=[pl.BlockSpec((tm, 3 * D_MODEL), lambda i: (i, 0)), tile512, tile512, tile512,
                   _resident((1, 3 * D_MODEL)), _resident((3, WIDTH, D_MODEL))],
        out_shape=[jax.ShapeDtypeStruct((t, 3 * D_MODEL), BF16)] + [jax.ShapeDtypeStruct((t, WIDTH), BF16)] * 3
                  + [jax.ShapeDtypeStruct((1, 3 * D_MODEL), F32), jax.ShapeDtypeStruct((3, WIDTH, D_MODEL), BF16)],
        scratch_shapes=[pltpu.VMEM((3, WIDTH, D_MODEL), F32)],
        compiler_params=_params("arbitrary"),
    )(dmerged, *projections, *gates, *branch_in, w_branch)


def _conv_bwd(proj, dya, conv_w, x_b, *, tm, seq):
    t = proj.shape[0]
    per_b = seq // tm
    hb = tm // HALO
    last_blk = t // HALO - 1
    last = t // tm - 1

    def body(cb_ref, cc_ref, ch_ref, cch_ref, chh_ref, dya_ref, cbn_ref, dyan_ref, cw_ref, x_ref,
             d_ref, dcw_ref, gw_ref, acc_ref):
        i = pl.program_id(0)

        @pl.when(i == 0)
        def _():
            dcw_ref[...] = jnp.zeros_like(dcw_ref)
            acc_ref[...] = jnp.zeros_like(acc_ref)

        row = lax.broadcasted_iota(jnp.int32, (tm, WIDTH), 0)
        cb, cc, ch = cb_ref[...].astype(F32), cc_ref[...].astype(F32), ch_ref[...].astype(F32)
        u = cc * ch
        halo = jnp.where(i % per_b == 0, 0.0, cch_ref[...].astype(F32) * chh_ref[...].astype(F32))
        u1 = _shift_down(u, halo, 1, row)
        u2 = _shift_down(u, halo, 2, row)
        cw = cw_ref[...]
        y = cw[0:1] * u2 + cw[1:2] * u1 + cw[2:3] * u
        dya = dya_ref[...].astype(F32)
        dy = dya * cb
        nxt = jnp.where(i % per_b == per_b - 1, 0.0, dyan_ref[...].astype(F32) * cbn_ref[...].astype(F32))
        du = cw[2:3] * dy + cw[1:2] * _shift_up(dy, nxt, 1, row, tm) + cw[0:1] * _shift_up(dy, nxt, 2, row, tm)
        d_ref[:, 0:WIDTH] = (dya * y).astype(BF16)
        d_ref[:, WIDTH:2 * WIDTH] = (du * ch).astype(BF16)
        d_ref[:, 2 * WIDTH:3 * WIDTH] = (du * cc).astype(BF16)
        dcw_ref[0:1, :] += jnp.sum(dy * u2, axis=0, keepdims=True)
        dcw_ref[1:2, :] += jnp.sum(dy * u1, axis=0, keepdims=True)
        dcw_ref[2:3, :] += jnp.sum(dy * u, axis=0, keepdims=True)
        acc_ref[...] += _dot_tn(x_ref[...], d_ref[...])

        @pl.when(i == last)
        def _():
            gw_ref[...] = acc_ref[...].astype(BF16)

    def half(c):
        return pl.BlockSpec((tm, WIDTH), lambda i: (i, c // WIDTH))

    def prev(c):
        return pl.BlockSpec((HALO, WIDTH), lambda i: (jnp.maximum(i * hb - 1, 0), c // WIDTH))

    def nxt(c):
        return pl.BlockSpec((HALO, WIDTH), lambda i: (jnp.minimum((i + 1) * hb, last_blk), c // WIDTH))

    return pl.pallas_call(
        body, name="conv_bwd", grid=(t // tm,),
        in_specs=[half(C_CB), half(C_CC), half(C_CH), prev(C_CC), prev(C_CH),
                  pl.BlockSpec((tm, WIDTH), lambda i: (i, 0)), nxt(C_CB), nxt(0), _resident((CONV_K, WIDTH)),
                  pl.BlockSpec((tm, D_MODEL), lambda i: (i, 0))],
        out_specs=[pl.BlockSpec((tm, 3 * WIDTH), lambda i: (i, 0)), _resident((CONV_K, WIDTH)),
                   _resident((D_MODEL, 3 * WIDTH))],
        out_shape=[jax.ShapeDtypeStruct((t, 3 * WIDTH), BF16), jax.ShapeDtypeStruct((CONV_K, WIDTH), F32),
                   jax.ShapeDtypeStruct((D_MODEL, 3 * WIDTH), BF16)],
        scratch_shapes=[pltpu.VMEM((D_MODEL, 3 * WIDTH), F32)],
        compiler_params=_params("arbitrary"),
    )(proj, proj, proj, proj, proj, dya, proj, dya, conv_w, x_b)


def _loss_head(y, target, *, tm):
    t = y.shape[0]

    def body(y_ref, t_ref, dy_ref, l_ref):
        @pl.when(pl.program_id(0) == 0)
        def _():
            l_ref[...] = jnp.zeros_like(l_ref)

        err = y_ref[...] - t_ref[...]
        dy_ref[...] = err * (1.0 / D_MODEL)
        per_row = jnp.sum(err * err, axis=-1, keepdims=True) * (1.0 / D_MODEL)
        l_ref[...] += 0.5 * jnp.sum(per_row, axis=0, keepdims=True)

    row = pl.BlockSpec((tm, D_MODEL), lambda i: (i, 0))
    return pl.pallas_call(
        body, name="loss_head", grid=(t // tm,),
        in_specs=[row, row], out_specs=[row, _resident((8, 128))],
        out_shape=[jax.ShapeDtypeStruct((t, D_MODEL), F32), jax.ShapeDtypeStruct((8, 128), F32)],
        compiler_params=_params("arbitrary"),
    )(y, target)


def _lb_softmax(lower_bounds):
    x = lower_bounds
    e = jnp.exp(x - jnp.max(x, axis=0, keepdims=True))
    return e / jnp.sum(e, axis=0, keepdims=True)


def _lb_fwd(lower_bounds):
    def body(x_ref, o_ref):
        s = _lb_softmax(x_ref[...])
        c = s[0:1]
        o_ref[0:1, :] = c - s[0:1]
        for l in range(1, DEPTH):
            c = c + s[l:l + 1]
            o_ref[l:l + 1, :] = c - s[0:1]

    return pl.pallas_call(body, name="lb_fwd", out_shape=jax.ShapeDtypeStruct(lower_bounds.shape, F32))(lower_bounds)


def _lb_bwd(lower_bounds, d_lb_all):
    def body(x_ref, d_ref, o_ref):
        s = _lb_softmax(x_ref[...])
        d = d_ref[...]
        rows = [jnp.zeros_like(d[0:1])]
        for j in range(1, DEPTH):
            acc = d[j:j + 1]
            for l in range(j + 1, DEPTH):
                acc = acc + d[l:l + 1]
            rows.append(acc)
        inner = rows[0] * s[0:1]
        for j in range(1, DEPTH):
            inner = inner + rows[j] * s[j:j + 1]
        for j in range(DEPTH):
            o_ref[j:j + 1, :] = s[j:j + 1] * (rows[j] - inner)

    return pl.pallas_call(body, name="lb_bwd", out_shape=jax.ShapeDtypeStruct(lower_bounds.shape, F32))(
        lower_bounds, d_lb_all)


def _adamw(w, g, m, v):
    m2 = ADAM_B1 * m + (1.0 - ADAM_B1) * g
    v2 = ADAM_B2 * v + (1.0 - ADAM_B2) * (g * g)
    m_hat = m2 / (1.0 - ADAM_B1 ** ADAM_STEP)
    v_hat = v2 / (1.0 - ADAM_B2 ** ADAM_STEP)
    delta = -ADAM_LR * (m_hat / (jnp.sqrt(v_hat) + ADAM_EPS) + ADAM_WD * w)
    return delta, m2, v2


def _adam_small(name, g, w, m, v):
    shape = w.shape
    flat = (-1, shape[-1])
    g2, w2, m2, v2 = (a.reshape(flat) for a in (g, w, m, v))

    def body(g_ref, w_ref, m_ref, v_ref, d_ref, mo_ref, vo_ref):
        d, mm, vv = _adamw(w_ref[...], g_ref[...], m_ref[...], v_ref[...])
        d_ref[...] = d
        mo_ref[...] = mm
        vo_ref[...] = vv

    outs = pl.pallas_call(body, name=name, out_shape=[jax.ShapeDtypeStruct(w2.shape, F32)] * 3)(g2, w2, m2, v2)
    return [o.reshape(shape) for o in outs]


def _adam_shard(name, recvs, w, m, v, *, tr):
    _, r, c = w.shape

    def body(*refs):
        rc, (w_ref, m_ref, v_ref), (g_ref, d_ref, mo_ref, vo_ref) = refs[:DEPTH], refs[DEPTH:DEPTH + 3], refs[DEPTH + 3:]
        layer = pl.program_id(0)
        for cand in range(DEPTH):
            @pl.when(layer == cand)
            def _():
                g = rc[cand][0].astype(F32)
                for d in range(1, N_DEV):
                    g = g + rc[cand][d].astype(F32)
                dl, mm, vv = _adamw(w_ref[...], g, m_ref[...], v_ref[...])
                g_ref[...] = g
                d_ref[...] = dl
                mo_ref[...] = mm
                vo_ref[...] = vv

    def recv_spec(cand):
        return pl.BlockSpec((N_DEV, tr, c), lambda l, i: (0, jnp.where(l == cand, i, 0), 0))

    tile = pl.BlockSpec((None, tr, c), lambda l, i: (l, i, 0))
    return pl.pallas_call(
        body, name=name, grid=(DEPTH, r // tr),
        in_specs=[recv_spec(cand) for cand in range(DEPTH)] + [tile] * 3,
        out_specs=[tile] * 4,
        out_shape=[jax.ShapeDtypeStruct(w.shape, F32)] * 4,
        compiler_params=_params("parallel", "parallel"),
    )(*recvs, w, m, v)


def _sum_devices(name, x):
    def body(x_ref, o_ref):
        acc = x_ref[0]
        for d in range(1, N_DEV):
            acc = acc + x_ref[d]
        o_ref[...] = acc

    return pl.pallas_call(body, name=name, out_shape=jax.ShapeDtypeStruct(x.shape[1:], x.dtype))(x)


SMALL = (("lower_bounds", 1, 512), ("conv_w", CONV_K, WIDTH), ("hg_norm_w", 1, HEAD_DIM), ("b_gate", 3, D_MODEL),
         ("ln1_g", 1, D_MODEL), ("ln1_b", 1, D_MODEL), ("ln2_g", 1, D_MODEL), ("ln2_b", 1, D_MODEL))
SMALL_ROWS = sum(r for _, r, _ in SMALL)


def _pack_small(per_layer):
    flat = [a for layer in per_layer for a in layer]

    def body(*refs):
        ins, o_ref = refs[:-1], refs[-1]
        o_ref[...] = jnp.zeros_like(o_ref)
        it = iter(ins)
        for l in range(DEPTH):
            row = l * SMALL_ROWS
            for name, nrows, ncols in SMALL:
                ref = next(it)
                if name == "b_gate":
                    for k in range(nrows):
                        o_ref[row + k:row + k + 1, :] = ref[:, k * ncols:(k + 1) * ncols]
                else:
                    o_ref[row:row + nrows, 0:ncols] = ref[...]
                row += nrows

    return pl.pallas_call(body, name="pack_small_grads",
                          out_shape=jax.ShapeDtypeStruct((DEPTH * SMALL_ROWS, D_MODEL), F32))(*flat)


def _unpack_small(summed):
    s3 = summed.reshape(DEPTH, SMALL_ROWS, D_MODEL)
    out, row = {}, 0
    for name, nrows, ncols in SMALL:
        out[name] = s3[:, row:row + nrows, :ncols].reshape(DEPTH, nrows * ncols)
        row += nrows
    return out


def _natural_cols(g):
    nd = g.ndim
    perm = tuple(range(1, nd - 1)) + (0, nd - 1)
    t = jnp.transpose(g, perm)
    return t.reshape(t.shape[:-2] + (t.shape[-2] * t.shape[-1],))


def _natural_rows(g):
    return g.reshape(g.shape[0] * g.shape[1], g.shape[2])


def _hosted(hosts, key, fn):
    pairs = hosts.get(key) if hosts else None
    if callable(pairs):
        pairs = pairs()
    if not pairs:
        return fn(None)
    outs, recvs = fn([ex for ex, _ in pairs])
    for (_, hook), recv in zip(pairs, recvs):
        hook(recv)
    return outs


def _layer_fwd(cur, cur_b, mem2d, wl, *, bsz, seq, hosts=None):
    tm = min(512, seq)
    proj, hf = _hosted(hosts, "in_proj", lambda c: _in_proj(cur_b, wl["w_in"], tm=min(1024, seq), carry=c))
    y_b, o_pre, states = _hosted(hosts, "hgrn_fwd", lambda c: _hgrn_fwd(proj, hf, wl["lb"], wl["nw"], bsz=bsz,
                                                                         seq=seq, carry=c))
    mk, mv = _mem_kv(mem2d, wl["w_mk"], wl["w_mv"])
    y_c = _attn_fwd(proj, mk, mv, tm=tm, seq=seq)
    y_a, pa, pb, pc, merged, sga, sgb, sgc = _hosted(
        hosts, "merge_fwd", lambda c: _merge_fwd(proj, y_b, y_c, wl["conv"], wl["w_br"], wl["b_gate"], tm=tm,
                                                 seq=seq, carry=c))
    z1, x1, x1_b = _hosted(hosts, "wo_ln", lambda c: _linear_ln("wo_ln", merged, wl["w_o"], cur, wl["ln1_g"],
                                                                  wl["ln1_b"], tm=tm, carry=c))
    hid = _hosted(hosts, "mlp_up", lambda c: _mm_nn("mlp_up", x1_b, wl["w_up"], tm=min(1024, seq), tn=2048,
                                                     out_dtype=BF16, relu2=True, carry=c))
    z2, x2, x2_b = _linear_ln("down_ln", hid, wl["w_down"], x1, wl["ln2_g"], wl["ln2_b"], tm=tm)
    return dict(x_b=cur_b, proj=proj, hf=hf, y_a=y_a, y_b=y_b, y_c=y_c, o_pre=o_pre, states=states, mk=mk, mv=mv,
                proj3=(pa, pb, pc), gates3=(sga, sgb, sgc), merged=merged, z1=z1, x1_b=x1_b, hid=hid, z2=z2, x2=x2,
                x2_b=x2_b)


def _layer_bwd(dcur, mem2d, s, wl, *, bsz, seq, plan=None):
    tm = min(512, seq)
    tk = min(2048, bsz * seq)
    g = {}

    def run(key, fn):
        made = plan[key](g) if plan and key in plan else None
        return _hosted({key: [made]} if made else None, key, fn)

    dz2, dz2_b, dhpre, d_ln2g, d_ln2b = run(
        "ln2_bwd_down", lambda c: _ln_bwd_mm_nt("ln2_bwd_down", dcur, s["z2"], wl["ln2_g"], wl["w_down"],
                                                s["hid"], tm=tm, tn=1024, carry=c))
    g["w_down"] = _mm_tn("grad_w_down", s["hid"], dz2_b, tk=tk, tmo=1024, tno=1024)
    dx1 = _mm_nt_sum("mlp_up_bwd", [dhpre], [0], wl["w_up"], dz2, tm=tm)
    g["w_up"] = run("grad_w_up", lambda c: _mm_tn("grad_w_up", s["x1_b"], dhpre, tk=tk, tmo=1024, tno=2048, carry=c))
    dz1, dz1_b, dmerged, d_ln1g, d_ln1b, g["w_o"] = _ln_bwd_mm_nt("ln1_bwd_wo", dx1, s["z1"], wl["ln1_g"], wl["w_o"],
                                                                  left=s["merged"], tm=min(1024, seq), tn=1024)
    dgate, dya, dyb, dyc, d_bg, g["w_br"] = _merge_bwd(dmerged, s["proj3"], s["gates3"],
                                                       (s["y_a"], s["y_b"], s["y_c"]), wl["w_br"], tm=tm)
    tall = min(1024, seq)
    d_conv, d_cw, gw_conv = _conv_bwd(s["proj"], dya, wl["conv"], s["x_b"], tm=tall, seq=seq)
    dhg, d_lb, d_nw = run(
        "hgrn_bwd", lambda c: _hgrn_bwd(s["proj"], s["hf"], wl["lb"], wl["nw"], s["o_pre"], s["states"], dyb,
                                        bsz=bsz, seq=seq, carry=c))
    dmq, dmk, dmv, gw_mq = _attn_bwd(s["proj"], s["mk"], s["mv"], dyc, s["x_b"], tm=tall, seq=seq)
    tkm = min(512, mem2d.shape[0])
    g["w_mk"] = _mm_tn("grad_w_mem", mem2d, dmk, tk=tkm, tmo=1024, tno=512)
    g["w_mv"] = _mm_tn("grad_w_mem", mem2d, dmv, tk=tkm, tmo=1024, tno=512)
    pieces = [d_conv, dhg, dmq, dgate]
    offsets = [C_CB, C_HQ, C_MQ, C_GA]
    g["w_in"] = jnp.concatenate(
        [gw_conv, _mm_tn("grad_w_in_hgrn", s["x_b"], dhg, tk=tk, tmo=1024, tno=2048), gw_mq,
         _mm_tn("grad_w_in_gates", s["x_b"], dgate, tk=tk, tmo=1024, tno=1536)], axis=1)
    dx = run("in_proj_bwd", lambda c: _mm_nt_sum("in_proj_bwd", pieces, offsets, wl["w_in"], dz1,
                                                 tm=min(512, seq), carry=c))
    return dx, g, [d_lb, d_cw, d_nw, d_bg, d_ln1g, d_ln1b, d_ln2g, d_ln2b]


def kernel(x, mem, lower_bounds, w_in, conv_w, hg_norm_w, w_mem_k, w_mem_v, w_branch, b_gate, w_o, ln1_g, ln1_b, w_up, w_down, ln2_g, ln2_b, loss_target, m_lower_bounds, m_w_in, m_conv_w, m_hg_norm_w, m_w_mem_k, m_w_mem_v, m_w_branch, m_b_gate, m_w_o, m_ln1_g, m_ln1_b, m_w_up, m_w_down, m_ln2_g, m_ln2_b, v_lower_bounds, v_w_in, v_conv_w, v_hg_norm_w, v_w_mem_k, v_w_mem_v, v_w_branch, v_b_gate, v_w_o, v_ln1_g, v_ln1_b, v_w_up, v_w_down, v_ln2_g, v_ln2_b):
    bsz, seq, _ = x.shape
    t = bsz * seq
    me = _my_id()

    sh = dict(w_in=w_in.astype(BF16), w_mk=w_mem_k.astype(BF16), w_mv=w_mem_v.astype(BF16),
              w_br=w_branch.astype(BF16), w_o=w_o.astype(BF16), w_up=w_up.astype(BF16), w_down=w_down.astype(BF16))
    half_rows = D_MODEL // 2
    sh["w_in_a"], sh["w_in_b"] = sh["w_in"][:, :half_rows], sh["w_in"][:, half_rows:]
    natural = dict(w_in=_natural_cols, w_in_a=_natural_cols, w_in_b=_natural_cols, w_mk=_natural_rows,
                   w_mv=_natural_rows, w_br=_natural_cols, w_o=_natural_rows, w_up=_natural_cols,
                   w_down=_natural_rows)

    lb_all = _lb_fwd(lower_bounds)
    layer_w = [dict(lb=lb_all[l][None], nw=hg_norm_w[l][None], b_gate=b_gate[l][None], ln1_g=ln1_g[l][None],
                    ln1_b=ln1_b[l][None], ln2_g=ln2_g[l][None], ln2_b=ln2_b[l][None]) for l in range(DEPTH)]
    half_full = {}

    def near(names, l):
        srcs = [sh[n][l] for n in names]
        ex = _Exchange(srcs, piece_shapes=[s_.shape for s_ in srcs], route="near")
        return ex, lambda recv_: half_full.update({(n, l): r for n, r in zip(names, recv_)})

    def relay(names, l):
        ex = _Exchange([half_full.pop((n, l)) for n in names], route="relay")

        def hook(recv_):
            for n, r in zip(names, recv_):
                layer_w[l][n] = natural[n](r)
        return ex, hook

    small4 = ["w_mk", "w_mv", "w_br", "w_o"]
    conv_shard = conv_w.reshape(DEPTH * CONV_K * (WIDTH // N_DEV) // 128, 128)
    first = near(["w_in"], 0)
    conv_ex = _Exchange([conv_shard], piece_shapes=[conv_shard.shape])
    got = _exchange("gather_first", [first[0], conv_ex])
    first[1](got[0])
    conv_full = _natural_cols(got[1][0].reshape(N_DEV, DEPTH, CONV_K, WIDTH // N_DEV))
    second = relay(["w_in"], 0)
    second[1](_exchange("relay_first", [second[0]])[0])

    x2d = x.reshape(t, D_MODEL)
    mem2d = mem.reshape(bsz * MEM_LEN, D_MODEL)
    target2d = loss_target.reshape(t, D_MODEL)

    saved = []
    cur, cur_b = x2d, x2d.astype(BF16)
    for l in range(DEPTH):
        wl = layer_w[l]
        wl["conv"] = conv_full[l]
        more = l + 1 < DEPTH
        now = ["w_up", "w_down"] + ([] if l else small4)
        hosts = {"in_proj": [near(now, l)],
                 "hgrn_fwd": lambda l=l, more=more, now=now: [relay(now, l)] + ([near(["w_in"], l + 1)] if more else [])}
        if more:
            hosts["merge_fwd"] = lambda l=l: [relay(["w_in"], l + 1), near(small4, l + 1)]
            hosts["mlp_up"] = lambda l=l: [relay(small4, l + 1)]
        s = _layer_fwd(cur, cur_b, mem2d, wl, bsz=bsz, seq=seq, hosts=hosts)
        saved.append(s)
        cur, cur_b = s["x2"], s["x2_b"]

    dcur, loss_tile = _loss_head(cur, target2d, tm=min(512, seq))
    loss = lax.psum(loss_tile[0, 0], ("x", "y", "c"))

    in_w = IN_COLS // N_DEV

    def in_half(r):
        return lambda ref, j: ref.at[pl.ds(r * half_rows, half_rows), pl.ds(j * in_w, in_w)]

    slicer = dict(w_in_a=in_half(0), w_in_b=in_half(1), w_mk=_rows(D_MODEL // N_DEV), w_mv=_rows(D_MODEL // N_DEV),
                  w_br=_cols(D_MODEL // N_DEV), w_o=_rows(D_MODEL // N_DEV), w_up=_cols(D_FF // N_DEV),
                  w_down=_rows(D_FF // N_DEV))
    source = dict(w_in_a="w_in", w_in_b="w_in")
    recv = [dict() for _ in range(DEPTH)]

    def scatter_of(names, g, into):
        ex = _Exchange([g[source.get(n, n)] for n in names], [slicer[n] for n in names],
                       [sh[n].shape[1:] for n in names])
        return ex, lambda recv_: into.update(zip(names, recv_))

    small_rows = [None] * DEPTH
    prev = None
    rest = ["w_in_b", "w_mk", "w_mv"]
    for l in reversed(range(DEPTH)):
        plan = {"grad_w_up": lambda g, l=l: scatter_of(["w_down"], g, recv[l]),
                "hgrn_bwd": lambda g, l=l: scatter_of(["w_up", "w_o", "w_br"], g, recv[l])}
        if l == 0:
            plan["in_proj_bwd"] = lambda g: scatter_of(["w_in_a"] + rest, g, recv[0])
        else:
            plan["in_proj_bwd"] = lambda g, l=l: scatter_of(["w_in_a"], g, recv[l])
        if prev is not None:
            plan["ln2_bwd_down"] = lambda g, l=l, prev=prev: scatter_of(rest, prev, recv[l + 1])
        dcur, prev, small_rows[l] = _layer_bwd(dcur, mem2d, saved[l], layer_w[l], bsz=bsz, seq=seq, plan=plan)
    for r in recv:
        r["w_in"] = jnp.concatenate([r.pop("w_in_a"), r.pop("w_in_b")], axis=1)

    packed = _pack_small(small_rows)
    all_small = _exchange("gather_small_grads", [_Exchange([packed], piece_shapes=[packed.shape])])[0][0]
    small_grads = _unpack_small(_sum_devices("sum_small_grads", all_small))
    small_grads["lower_bounds"] = _lb_bwd(lower_bounds, small_grads["lower_bounds"])
    conv_all = small_grads["conv_w"].reshape(DEPTH, CONV_K, WIDTH)
    small_grads["conv_w"] = lax.dynamic_slice_in_dim(conv_all, me * (WIDTH // N_DEV), WIDTH // N_DEV, axis=2)

    grads, deltas, new_m, new_v = {}, {}, {}, {}
    given = dict(lower_bounds=(lower_bounds, m_lower_bounds, v_lower_bounds), conv_w=(conv_w, m_conv_w, v_conv_w),
                 hg_norm_w=(hg_norm_w, m_hg_norm_w, v_hg_norm_w), b_gate=(b_gate, m_b_gate, v_b_gate),
                 ln1_g=(ln1_g, m_ln1_g, v_ln1_g), ln1_b=(ln1_b, m_ln1_b, v_ln1_b),
                 ln2_g=(ln2_g, m_ln2_g, v_ln2_g), ln2_b=(ln2_b, m_ln2_b, v_ln2_b))
    for name, (w_, m_, v_) in given.items():
        g_ = small_grads[name].reshape(w_.shape)
        grads[name] = g_
        deltas[name], new_m[name], new_v[name] = _adam_small("adam_" + name, g_, w_, m_, v_)

    big = dict(w_in=("w_in", w_in, m_w_in, v_w_in, 128), w_mem_k=("w_mk", w_mem_k, m_w_mem_k, v_w_mem_k, 128),
               w_mem_v=("w_mv", w_mem_v, m_w_mem_v, v_w_mem_v, 128),
               w_branch=("w_br", w_branch, m_w_branch, v_w_branch, 512), w_o=("w_o", w_o, m_w_o, v_w_o, 128),
               w_up=("w_up", w_up, m_w_up, v_w_up, 256), w_down=("w_down", w_down, m_w_down, v_w_down, 128))
    for name, (k, w_, m_, v_, tr) in big.items():
        shape = w_.shape
        flat = (DEPTH, -1, shape[-1])
        rc = [recv[l][k].reshape((N_DEV,) + w_.reshape(flat).shape[1:]) for l in range(DEPTH)]
        outs = _adam_shard("adam_" + name, rc, w_.reshape(flat), m_.reshape(flat), v_.reshape(flat), tr=tr)
        grads[name], deltas[name], new_m[name], new_v[name] = (o.reshape(shape) for o in outs)

    order = ["lower_bounds", "w_in", "conv_w", "hg_norm_w", "w_mem_k", "w_mem_v", "w_branch", "b_gate", "w_o",
             "ln1_g", "ln1_b", "w_up", "w_down", "ln2_g", "ln2_b"]
    return (loss, dcur.reshape(x.shape), *[grads[n] for n in order], *[deltas[n] for n in order],
            *[new_m[n] for n in order], *[new_v[n] for n in order])
```
